```python
import jax, jax.numpy as jnp
from jax import lax
import numpy as np

D_MODEL = 1024
BATCH = 8
SEQ = 4096
DEPTH = 4

A_WIDTH = D_MODEL
A_GROUPS = 8
A_GROUP_DIM = A_WIDTH // A_GROUPS
A_CHUNK = 128
B_HEADS = 8
B_DK = 128
B_DV = 128
B_CONV = 4
B_CHUNK = 64
FFN_HIDDEN = -(-8 * D_MODEL // (3 * 256)) * 256
IN_UV = 2 * A_WIDTH
QK_WIDTH = B_HEADS * B_DK
V_WIDTH = B_HEADS * B_DV
IN_QKV = 2 * QK_WIDTH + V_WIDTH
IN_GATES = 2 * D_MODEL
IN_TOTAL = IN_UV + IN_QKV + V_WIDTH + 2 * B_HEADS + IN_GATES
NORM_EPS = 1e-6

kernel_name = "hybrid_gmlp_gdn_adaln_block"


def rms_norm(x, w):
    xf = x.astype(jnp.float32)
    y = xf * lax.rsqrt(jnp.mean(xf * xf, axis=-1, keepdims=True) + NORM_EPS)
    return (y * w.astype(jnp.float32)).astype(x.dtype)


def l2_norm(x):
    return x * lax.rsqrt(jnp.sum(x * x, axis=-1, keepdims=True) + NORM_EPS)


def causal_dwconv(x, w):
    K, C = w.shape
    return lax.conv_general_dilated(
        x, w[:, None, :], window_strides=(1,), padding=[(K - 1, 0)],
        dimension_numbers=("NWC", "WIO", "NWC"), feature_group_count=C)


def chunked_spatial_gating(uv, w_s, b_s, g_v):
    Bn, T, _ = uv.shape
    u, v = jnp.split(uv, 2, axis=-1)
    v = v.reshape(Bn, T // A_CHUNK, A_CHUNK, A_GROUPS, A_GROUP_DIM)
    v = rms_norm(v, g_v.reshape(A_GROUPS, A_GROUP_DIM))
    causal = jnp.tril(jnp.ones((A_CHUNK, A_CHUNK), dtype=bool))
    w = jnp.where(causal, w_s, 0)
    s = jnp.einsum("gts,bnsgd->bntgd", w, v) + jnp.swapaxes(b_s, 0, 1)[None, None, :, :, None]
    return u * s.reshape(Bn, T, A_WIDTH)


def gated_delta_rule(q, k, v, g, beta):
    Bn, T, H, DK = q.shape
    DV = v.shape[-1]
    C = B_CHUNK
    N = T // C
    q = q * DK ** -0.5

    def chunk4(t):
        return t.reshape(Bn, N, C, H, t.shape[-1]).transpose(0, 3, 1, 2, 4)

    def chunk3(t):
        return t.reshape(Bn, N, C, H).transpose(0, 3, 1, 2)

    qc, kc, vc = chunk4(q), chunk4(k), chunk4(v)
    gc, bc = chunk3(g), chunk3(beta)
    gam = jnp.cumsum(gc, axis=-1)
    causal = jnp.tril(jnp.ones((C, C), dtype=bool))
    strict = jnp.tril(jnp.ones((C, C), dtype=bool), k=-1)
    decay_mat = jnp.exp(jnp.where(causal, gam[..., :, None] - gam[..., None, :], -jnp.inf))
    k_beta = kc * bc[..., None]
    kk = jnp.einsum("bhnid,bhnjd->bhnij", k_beta, kc)
    a_mat = jnp.where(strict, kk * decay_mat, 0.0)
    eye = jnp.broadcast_to(jnp.eye(C, dtype=a_mat.dtype), a_mat.shape)
    t_mat = lax.linalg.triangular_solve(a_mat, eye, left_side=True, lower=True, unit_diagonal=True)
    u_val = t_mat @ (vc * bc[..., None])
    w_key = t_mat @ (k_beta * jnp.exp(gam)[..., None])
    qk = jnp.einsum("bhnid,bhnjd->bhnij", qc, kc) * decay_mat
    q_dec = qc * jnp.exp(gam)[..., None]
    k_dec = kc * jnp.exp(gam[..., -1:] - gam)[..., None]
    g_last = jnp.exp(gam[..., -1])

    def step(S, xs):
        qd, kd, uv_, wk, a, gl = xs
        v_new = uv_ - wk @ S
        o = qd @ S + a @ v_new
        S = S * gl[..., None, None] + jnp.swapaxes(kd, -1, -2) @ v_new
        return S, o

    xs = tuple(jnp.moveaxis(t, 2, 0) for t in (q_dec, k_dec, u_val, w_key, qk, g_last))
    S0 = jnp.zeros((Bn, H, DK, DV), dtype=q.dtype)
    _, o = lax.scan(step, S0, xs)
    return o.transpose(1, 0, 3, 2, 4).reshape(Bn, T, H, DV)


def gated_deltanet(qkv, z, b_raw, a_raw, conv_w, a_log, dt_bias, g_o):
    Bn, T, _ = qkv.shape
    dt = qkv.dtype
    qkv = jax.nn.silu(causal_dwconv(qkv, conv_w)).astype(jnp.float32)
    q, k, v = jnp.split(qkv, [QK_WIDTH, 2 * QK_WIDTH], axis=-1)
    q = l2_norm(q.reshape(Bn, T, B_HEADS, B_DK))
    k = l2_norm(k.reshape(Bn, T, B_HEADS, B_DK))
    v = v.reshape(Bn, T, B_HEADS, B_DV)
    beta = jax.nn.sigmoid(b_raw.astype(jnp.float32))
    g = -jnp.exp(a_log.astype(jnp.float32)) * jax.nn.softplus(
        a_raw.astype(jnp.float32) + dt_bias.astype(jnp.float32))
    o = gated_delta_rule(q, k, v, g, beta)
    o = rms_norm(o, g_o) * jax.nn.silu(z.reshape(Bn, T, B_HEADS, B_DV).astype(jnp.float32))
    return o.reshape(Bn, T, V_WIDTH).astype(dt)


def _fwd_setup_inputs(seed: int = 0) -> dict:
    key = jax.random.key(seed)
    ks = jax.random.split(key, 24)
    nrm = jax.random.normal
    L, D, F = DEPTH, D_MODEL, FFN_HIDDEN
    dt_min, dt_max = 1e-3, 1e-1
    dt0 = jnp.exp(jax.random.uniform(ks[10], (L, B_HEADS)) * (np.log(dt_max) - np.log(dt_min)) + np.log(dt_min))
    return {
        "x": nrm(ks[0], (BATCH, SEQ, D), jnp.float32),
        "c": nrm(ks[1], (BATCH, D), jnp.float32),
        "ada_w": nrm(ks[2], (L, D, 6 * D), jnp.float32) * (0.5 * D ** -0.5),
        "ada_b": nrm(ks[3], (L, 6 * D), jnp.float32) * 0.02,
        "norm1_g": 1.0 + 0.02 * nrm(ks[4], (L, D), jnp.float32),
        "w_in": nrm(ks[5], (L, D, IN_TOTAL), jnp.float32) * D ** -0.5,
        "conv_w": nrm(ks[6], (L, B_CONV, IN_QKV), jnp.float32) * B_CONV ** -0.5,
        "spatial_w": nrm(ks[7], (L, A_GROUPS, A_CHUNK, A_CHUNK), jnp.float32) * A_CHUNK ** -0.5,
        "spatial_b": 1.0 + 0.02 * nrm(ks[8], (L, A_GROUPS, A_CHUNK), jnp.float32),
        "v_norm_g": 1.0 + 0.02 * nrm(ks[9], (L, A_WIDTH), jnp.float32),
        "a_log": jnp.log(jax.random.uniform(ks[11], (L, B_HEADS), jnp.float32, 1.0, 16.0)),
        "dt_bias": (dt0 + jnp.log(-jnp.expm1(-dt0))).astype(jnp.float32),
        "o_norm_g": 1.0 + 0.02 * nrm(ks[12], (L, B_DV), jnp.float32),
        "w_branch_a": nrm(ks[13], (L, A_WIDTH, D), jnp.float32) * A_WIDTH ** -0.5,
        "w_branch_b": nrm(ks[14], (L, V_WIDTH, D), jnp.float32) * V_WIDTH ** -0.5,
        "w_out": nrm(ks[15], (L, D, D), jnp.float32) * D ** -0.5,
        "norm2_g": 1.0 + 0.02 * nrm(ks[16], (L, D), jnp.float32),
        "w_ffn_in": nrm(ks[17], (L, D, 2 * F), jnp.float32) * D ** -0.5,
        "w_ffn_out": nrm(ks[18], (L, F, D), jnp.float32) * F ** -0.5,
        "final_g": 1.0 + 0.02 * nrm(ks[19], (D,), jnp.float32),
    }


def _fwd_reference(x, c, ada_w, ada_b, norm1_g, w_in, conv_w, spatial_w, spatial_b, v_norm_g,
              a_log, dt_bias, o_norm_g, w_branch_a, w_branch_b, w_out, norm2_g,
              w_ffn_in, w_ffn_out, final_g):
    cond = jax.nn.silu(c)
    offs = [IN_UV, IN_UV + IN_QKV, IN_UV + IN_QKV + V_WIDTH,
            IN_UV + IN_QKV + V_WIDTH + B_HEADS, IN_UV + IN_QKV + V_WIDTH + 2 * B_HEADS]
    for i in range(DEPTH):
        mod = cond @ ada_w[i] + ada_b[i]
        sh1, sc1, gt1, sh2, sc2, gt2 = [m[:, None, :] for m in jnp.split(mod, 6, axis=-1)]
        h = rms_norm(x, norm1_g[i]) * (1 + sc1) + sh1
        proj = h @ w_in[i]
        uv, qkv, z, b_raw, a_raw, gates = jnp.split(proj, offs, axis=-1)
        y_a = chunked_spatial_gating(jax.nn.gelu(uv), spatial_w[i], spatial_b[i], v_norm_g[i])
        y_b = gated_deltanet(qkv, z, b_raw, a_raw, conv_w[i], a_log[i], dt_bias[i], o_norm_g[i])
        g_a, g_b = jnp.split(jax.nn.sigmoid(gates), 2, axis=-1)
        merged = g_a * (y_a @ w_branch_a[i]) + g_b * (y_b @ w_branch_b[i])
        x = x + gt1 * (merged @ w_out[i])
        h = rms_norm(x, norm2_g[i]) * (1 + sc2) + sh2
        gate, up = jnp.split(h @ w_ffn_in[i], 2, axis=-1)
        x = x + gt2 * ((jax.nn.silu(gate) * up) @ w_ffn_out[i])
    return rms_norm(x, final_g)


import jax as _jax
import jax.numpy as _jnp

TWIN_FORMAT = 'train_step'
FWD_PARAMS = ['x', 'c', 'ada_w', 'ada_b', 'norm1_g', 'w_in', 'conv_w', 'spatial_w', 'spatial_b', 'v_norm_g', 'a_log', 'dt_bias', 'o_norm_g', 'w_branch_a', 'w_branch_b', 'w_out', 'norm2_g', 'w_ffn_in', 'w_ffn_out', 'final_g']
TWIN_WEIGHTS = ['ada_w', 'ada_b', 'norm1_g', 'w_in', 'conv_w', 'spatial_w', 'spatial_b', 'v_norm_g', 'a_log', 'dt_bias', 'o_norm_g', 'w_branch_a', 'w_branch_b', 'w_out', 'norm2_g', 'w_ffn_in', 'w_ffn_out', 'final_g']
TWIN_DIFF_INPUT = 'x'
TWIN_INPUTS = ['x', 'c', 'ada_w', 'ada_b', 'norm1_g', 'w_in', 'conv_w', 'spatial_w', 'spatial_b', 'v_norm_g', 'a_log', 'dt_bias', 'o_norm_g', 'w_branch_a', 'w_branch_b', 'w_out', 'norm2_g', 'w_ffn_in', 'w_ffn_out', 'final_g', 'loss_target', 'm_ada_w', 'm_ada_b', 'm_norm1_g', 'm_w_in', 'm_conv_w', 'm_spatial_w', 'm_spatial_b', 'm_v_norm_g', 'm_a_log', 'm_dt_bias', 'm_o_norm_g', 'm_w_branch_a', 'm_w_branch_b', 'm_w_out', 'm_norm2_g', 'm_w_ffn_in', 'm_w_ffn_out', 'm_final_g', 'v_ada_w', 'v_ada_b', 'v_norm1_g', 'v_w_in', 'v_conv_w', 'v_spatial_w', 'v_spatial_b', 'v_v_norm_g', 'v_a_log', 'v_dt_bias', 'v_o_norm_g', 'v_w_branch_a', 'v_w_branch_b', 'v_w_out', 'v_norm2_g', 'v_w_ffn_in', 'v_w_ffn_out', 'v_final_g']
TWIN_OUTPUTS = ['loss', 'grad_x', 'grad_ada_w', 'grad_ada_b', 'grad_norm1_g', 'grad_w_in', 'grad_conv_w', 'grad_spatial_w', 'grad_spatial_b', 'grad_v_norm_g', 'grad_a_log', 'grad_dt_bias', 'grad_o_norm_g', 'grad_w_branch_a', 'grad_w_branch_b', 'grad_w_out', 'grad_norm2_g', 'grad_w_ffn_in', 'grad_w_ffn_out', 'grad_final_g', 'delta_ada_w', 'delta_ada_b', 'delta_norm1_g', 'delta_w_in', 'delta_conv_w', 'delta_spatial_w', 'delta_spatial_b', 'delta_v_norm_g', 'delta_a_log', 'delta_dt_bias', 'delta_o_norm_g', 'delta_w_branch_a', 'delta_w_branch_b', 'delta_w_out', 'delta_norm2_g', 'delta_w_ffn_in', 'delta_w_ffn_out', 'delta_final_g', 'new_m_ada_w', 'new_m_ada_b', 'new_m_norm1_g', 'new_m_w_in', 'new_m_conv_w', 'new_m_spatial_w', 'new_m_spatial_b', 'new_m_v_norm_g', 'new_m_a_log', 'new_m_dt_bias', 'new_m_o_norm_g', 'new_m_w_branch_a', 'new_m_w_branch_b', 'new_m_w_out', 'new_m_norm2_g', 'new_m_w_ffn_in', 'new_m_w_ffn_out', 'new_m_final_g', 'new_v_ada_w', 'new_v_ada_b', 'new_v_norm1_g', 'new_v_w_in', 'new_v_conv_w', 'new_v_spatial_w', 'new_v_spatial_b', 'new_v_v_norm_g', 'new_v_a_log', 'new_v_dt_bias', 'new_v_o_norm_g', 'new_v_w_branch_a', 'new_v_w_branch_b', 'new_v_w_out', 'new_v_norm2_g', 'new_v_w_ffn_in', 'new_v_w_ffn_out', 'new_v_final_g']
TWIN_LEAF_KINDS = {'loss': 'loss', 'grad_x': 'grad_x', 'grad_ada_w': 'grad_w', 'grad_ada_b': 'grad_w', 'grad_norm1_g': 'grad_w', 'grad_w_in': 'grad_w', 'grad_conv_w': 'grad_w', 'grad_spatial_w': 'grad_w', 'grad_spatial_b': 'grad_w', 'grad_v_norm_g': 'grad_w', 'grad_a_log': 'grad_w', 'grad_dt_bias': 'grad_w', 'grad_o_norm_g': 'grad_w', 'grad_w_branch_a': 'grad_w', 'grad_w_branch_b': 'grad_w', 'grad_w_out': 'grad_w', 'grad_norm2_g': 'grad_w', 'grad_w_ffn_in': 'grad_w', 'grad_w_ffn_out': 'grad_w', 'grad_final_g': 'grad_w', 'delta_ada_w': 'delta_w', 'delta_ada_b': 'delta_w', 'delta_norm1_g': 'delta_w', 'delta_w_in': 'delta_w', 'delta_conv_w': 'delta_w', 'delta_spatial_w': 'delta_w', 'delta_spatial_b': 'delta_w', 'delta_v_norm_g': 'delta_w', 'delta_a_log': 'delta_w', 'delta_dt_bias': 'delta_w', 'delta_o_norm_g': 'delta_w', 'delta_w_branch_a': 'delta_w', 'delta_w_branch_b': 'delta_w', 'delta_w_out': 'delta_w', 'delta_norm2_g': 'delta_w', 'delta_w_ffn_in': 'delta_w', 'delta_w_ffn_out': 'delta_w', 'delta_final_g': 'delta_w', 'new_m_ada_w': 'new_m', 'new_m_ada_b': 'new_m', 'new_m_norm1_g': 'new_m', 'new_m_w_in': 'new_m', 'new_m_conv_w': 'new_m', 'new_m_spatial_w': 'new_m', 'new_m_spatial_b': 'new_m', 'new_m_v_norm_g': 'new_m', 'new_m_a_log': 'new_m', 'new_m_dt_bias': 'new_m', 'new_m_o_norm_g': 'new_m', 'new_m_w_branch_a': 'new_m', 'new_m_w_branch_b': 'new_m', 'new_m_w_out': 'new_m', 'new_m_norm2_g': 'new_m', 'new_m_w_ffn_in': 'new_m', 'new_m_w_ffn_out': 'new_m', 'new_m_final_g': 'new_m', 'new_v_ada_w': 'new_v', 'new_v_ada_b': 'new_v', 'new_v_norm1_g': 'new_v', 'new_v_w_in': 'new_v', 'new_v_conv_w': 'new_v', 'new_v_spatial_w': 'new_v', 'new_v_spatial_b': 'new_v', 'new_v_v_norm_g': 'new_v', 'new_v_a_log': 'new_v', 'new_v_dt_bias': 'new_v', 'new_v_o_norm_g': 'new_v', 'new_v_w_branch_a': 'new_v', 'new_v_w_branch_b': 'new_v', 'new_v_w_out': 'new_v', 'new_v_norm2_g': 'new_v', 'new_v_w_ffn_in': 'new_v', 'new_v_w_ffn_out': 'new_v', 'new_v_final_g': 'new_v'}


def _forward(args):
    return _fwd_reference(*[args[k] for k in FWD_PARAMS])


def _output_shape():
    out = _jax.eval_shape(lambda: _forward(_fwd_setup_inputs(0)))
    return out.shape, out.dtype

N_MICROBATCH = 1
ADAM_LR = 0.001
ADAM_B1 = 0.9
ADAM_B2 = 0.999
ADAM_EPS = 1e-08
ADAM_WD = 0.01
ADAM_STEP = 10
PER_EXAMPLE_BATCH_AXIS = {'x': 0, 'c': 0, 'loss_target': 0}
SHARED_INPUTS = []
_WEIGHT_DTYPES = {'ada_w': _jnp.float32, 'ada_b': _jnp.float32, 'norm1_g': _jnp.float32, 'w_in': _jnp.float32, 'conv_w': _jnp.float32, 'spatial_w': _jnp.float32, 'spatial_b': _jnp.float32, 'v_norm_g': _jnp.float32, 'a_log': _jnp.float32, 'dt_bias': _jnp.float32, 'o_norm_g': _jnp.float32, 'w_branch_a': _jnp.float32, 'w_branch_b': _jnp.float32, 'w_out': _jnp.float32, 'norm2_g': _jnp.float32, 'w_ffn_in': _jnp.float32, 'w_ffn_out': _jnp.float32, 'final_g': _jnp.float32}
MOMENT_SCALE = {'ada_w': 4.863293e-02, 'ada_b': 7.997666e-02, 'norm1_g': 4.573332e-02, 'w_in': 1.682278e-02, 'conv_w': 1.478124e-02, 'spatial_w': 1.531083e-02, 'spatial_b': 2.201396e-02, 'v_norm_g': 1.541689e-02, 'a_log': 6.480009e-02, 'dt_bias': 6.307986e-02, 'o_norm_g': 5.521028e-02, 'w_branch_a': 2.655376e-02, 'w_branch_b': 1.952793e-02, 'w_out': 3.308769e-02, 'norm2_g': 5.324386e-02, 'w_ffn_in': 2.326063e-02, 'w_ffn_out': 3.790217e-02, 'final_g': 3.199211e+01}


def _to_microbatches(a, axis):
    t = _jnp.moveaxis(a, axis, 0)
    t = t.reshape((N_MICROBATCH, t.shape[0] // N_MICROBATCH) + t.shape[1:])
    return _jnp.moveaxis(t, 1, axis + 1)


def setup_inputs(seed: int = 0) -> dict:
    inp = _fwd_setup_inputs(seed)
    key = _jax.random.fold_in(_jax.random.key(seed), 7919)
    shape, _ = _output_shape()
    out = dict(inp)
    out["loss_target"] = _jax.random.normal(_jax.random.fold_in(key, 0), shape, _jnp.float32)
    for i, name in enumerate(TWIN_WEIGHTS):
        w = inp[name].astype(_jnp.float32)
        if MOMENT_SCALE is None:
            s = _jnp.sqrt(_jnp.mean(_jnp.square(w)) + 1e-30)
        else:
            s = MOMENT_SCALE[name]
        km, kv = _jax.random.split(_jax.random.fold_in(key, i + 1))
        out[name] = w
        out["m_" + name] = s * _jax.random.normal(km, w.shape, _jnp.float32)
        out["v_" + name] = (s * s) * _jax.random.uniform(kv, w.shape, _jnp.float32, 0.5, 1.5)
    if N_MICROBATCH > 1:
        for name, axis in PER_EXAMPLE_BATCH_AXIS.items():
            out[name] = _to_microbatches(out[name], axis)
    return {'x': out['x'], 'c': out['c'], 'ada_w': out['ada_w'], 'ada_b': out['ada_b'], 'norm1_g': out['norm1_g'], 'w_in': out['w_in'], 'conv_w': out['conv_w'], 'spatial_w': out['spatial_w'], 'spatial_b': out['spatial_b'], 'v_norm_g': out['v_norm_g'], 'a_log': out['a_log'], 'dt_bias': out['dt_bias'], 'o_norm_g': out['o_norm_g'], 'w_branch_a': out['w_branch_a'], 'w_branch_b': out['w_branch_b'], 'w_out': out['w_out'], 'norm2_g': out['norm2_g'], 'w_ffn_in': out['w_ffn_in'], 'w_ffn_out': out['w_ffn_out'], 'final_g': out['final_g'], 'loss_target': out['loss_target'], 'm_ada_w': out['m_ada_w'], 'm_ada_b': out['m_ada_b'], 'm_norm1_g': out['m_norm1_g'], 'm_w_in': out['m_w_in'], 'm_conv_w': out['m_conv_w'], 'm_spatial_w': out['m_spatial_w'], 'm_spatial_b': out['m_spatial_b'], 'm_v_norm_g': out['m_v_norm_g'], 'm_a_log': out['m_a_log'], 'm_dt_bias': out['m_dt_bias'], 'm_o_norm_g': out['m_o_norm_g'], 'm_w_branch_a': out['m_w_branch_a'], 'm_w_branch_b': out['m_w_branch_b'], 'm_w_out': out['m_w_out'], 'm_norm2_g': out['m_norm2_g'], 'm_w_ffn_in': out['m_w_ffn_in'], 'm_w_ffn_out': out['m_w_ffn_out'], 'm_final_g': out['m_final_g'], 'v_ada_w': out['v_ada_w'], 'v_ada_b': out['v_ada_b'], 'v_norm1_g': out['v_norm1_g'], 'v_w_in': out['v_w_in'], 'v_conv_w': out['v_conv_w'], 'v_spatial_w': out['v_spatial_w'], 'v_spatial_b': out['v_spatial_b'], 'v_v_norm_g': out['v_v_norm_g'], 'v_a_log': out['v_a_log'], 'v_dt_bias': out['v_dt_bias'], 'v_o_norm_g': out['v_o_norm_g'], 'v_w_branch_a': out['v_w_branch_a'], 'v_w_branch_b': out['v_w_branch_b'], 'v_w_out': out['v_w_out'], 'v_norm2_g': out['v_norm2_g'], 'v_w_ffn_in': out['v_w_ffn_in'], 'v_w_ffn_out': out['v_w_ffn_out'], 'v_final_g': out['v_final_g']}


def _loss(weights, diff, rest, loss_target):
    with _jax.named_scope("forward"):
        args = {**rest, TWIN_DIFF_INPUT: diff, **{k: w.astype(_WEIGHT_DTYPES[k]) for k, w in weights.items()}}
        y = _forward(args)
    with _jax.named_scope("loss_head"):
        err = _jnp.square(y.astype(_jnp.float32) - loss_target)
        return 0.5 * _jnp.sum(_jnp.mean(err, axis=-1)) if err.ndim else 0.5 * err


def _adamw(w, g, m, v):
    m = ADAM_B1 * m + (1.0 - ADAM_B1) * g
    v = ADAM_B2 * v + (1.0 - ADAM_B2) * _jnp.square(g)
    m_hat = m / (1.0 - ADAM_B1 ** ADAM_STEP)
    v_hat = v / (1.0 - ADAM_B2 ** ADAM_STEP)
    delta = -ADAM_LR * (m_hat / (_jnp.sqrt(v_hat) + ADAM_EPS) + ADAM_WD * w)
    return delta, m, v


def reference(x, c, ada_w, ada_b, norm1_g, w_in, conv_w, spatial_w, spatial_b, v_norm_g, a_log, dt_bias, o_norm_g, w_branch_a, w_branch_b, w_out, norm2_g, w_ffn_in, w_ffn_out, final_g, loss_target, m_ada_w, m_ada_b, m_norm1_g, m_w_in, m_conv_w, m_spatial_w, m_spatial_b, m_v_norm_g, m_a_log, m_dt_bias, m_o_norm_g, m_w_branch_a, m_w_branch_b, m_w_out, m_norm2_g, m_w_ffn_in, m_w_ffn_out, m_final_g, v_ada_w, v_ada_b, v_norm1_g, v_w_in, v_conv_w, v_spatial_w, v_spatial_b, v_v_norm_g, v_a_log, v_dt_bias, v_o_norm_g, v_w_branch_a, v_w_branch_b, v_w_out, v_norm2_g, v_w_ffn_in, v_w_ffn_out, v_final_g):
    given = dict(x=x, c=c, ada_w=ada_w, ada_b=ada_b, norm1_g=norm1_g, w_in=w_in, conv_w=conv_w, spatial_w=spatial_w, spatial_b=spatial_b, v_norm_g=v_norm_g, a_log=a_log, dt_bias=dt_bias, o_norm_g=o_norm_g, w_branch_a=w_branch_a, w_branch_b=w_branch_b, w_out=w_out, norm2_g=norm2_g, w_ffn_in=w_ffn_in, w_ffn_out=w_ffn_out, final_g=final_g, loss_target=loss_target, m_ada_w=m_ada_w, m_ada_b=m_ada_b, m_norm1_g=m_norm1_g, m_w_in=m_w_in, m_conv_w=m_conv_w, m_spatial_w=m_spatial_w, m_spatial_b=m_spatial_b, m_v_norm_g=m_v_norm_g, m_a_log=m_a_log, m_dt_bias=m_dt_bias, m_o_norm_g=m_o_norm_g, m_w_branch_a=m_w_branch_a, m_w_branch_b=m_w_branch_b, m_w_out=m_w_out, m_norm2_g=m_norm2_g, m_w_ffn_in=m_w_ffn_in, m_w_ffn_out=m_w_ffn_out, m_final_g=m_final_g, v_ada_w=v_ada_w, v_ada_b=v_ada_b, v_norm1_g=v_norm1_g, v_w_in=v_w_in, v_conv_w=v_conv_w, v_spatial_w=v_spatial_w, v_spatial_b=v_spatial_b, v_v_norm_g=v_v_norm_g, v_a_log=v_a_log, v_dt_bias=v_dt_bias, v_o_norm_g=v_o_norm_g, v_w_branch_a=v_w_branch_a, v_w_branch_b=v_w_branch_b, v_w_out=v_w_out, v_norm2_g=v_norm2_g, v_w_ffn_in=v_w_ffn_in, v_w_ffn_out=v_w_ffn_out, v_final_g=v_final_g)
    weights = {n: given[n] for n in TWIN_WEIGHTS}
    shared = {n: given[n] for n in SHARED_INPUTS}
    per_example = {n: given[n] for n in ['x', 'c']}
    grad_fn = _jax.value_and_grad(_loss, argnums=(0, 1))

    def one_microbatch(ex, loss_target):
        ex = dict(ex)
        diff = ex.pop(TWIN_DIFF_INPUT)
        return grad_fn(weights, diff, {**shared, **ex}, loss_target)

    if N_MICROBATCH == 1:
        loss, (grad_w, grad_x) = one_microbatch(per_example, given["loss_target"])
    else:
        def body(carry, xs):
            loss_sum, grad_sum = carry
            l_k, (gw_k, gx_k) = one_microbatch(xs[0], xs[1])
            with _jax.named_scope("update"):
                return (loss_sum + l_k, _jax.tree.map(_jnp.add, grad_sum, gw_k)), gx_k

        init = (_jnp.zeros((), _jnp.float32), _jax.tree.map(_jnp.zeros_like, weights))
        (loss, grad_w), grad_x = _jax.lax.scan(body, init, (per_example, given["loss_target"]))
    with _jax.named_scope("update"):
        delta_w, new_m, new_v = {}, {}, {}
        for n in TWIN_WEIGHTS:
            delta_w[n], new_m[n], new_v[n] = _adamw(weights[n], grad_w[n], given["m_" + n], given["v_" + n])
    return (loss, grad_x, *[grad_w[n] for n in TWIN_WEIGHTS], *[delta_w[n] for n in TWIN_WEIGHTS],
            *[new_m[n] for n in TWIN_WEIGHTS], *[new_v[n] for n in TWIN_WEIGHTS])
```

```python
import functools
import math

import jax
import jax.numpy as jnp
from jax import lax
from jax.experimental import pallas as pl
from jax.experimental.pallas import tpu as pltpu

_F32 = jnp.float32
_BF = jnp.bfloat16
_MMT = jnp.bfloat16
_EPS = 1e-6
_GD = 128
_AC = 128
_BC = 64
_KC = 4
_LANE = 128
_VMEM_LIMIT = 56 * 1024 * 1024

_LR, _B1, _B2, _AEPS, _WD, _STEP = 0.001, 0.9, 0.999, 1e-08, 0.01, 10

_NN = (((1,), (0,)), ((), ()))
_NT = (((1,), (1,)), ((), ()))
_TN = (((0,), (0,)), ((), ()))

_MESH = pl.DeviceIdType.MESH


def _cp(*sem):
    return pltpu.CompilerParams(dimension_semantics=tuple(sem), vmem_limit_bytes=_VMEM_LIMIT)


def _dot(a, b, dn=_NN):
    return lax.dot_general(a.astype(_MMT), b.astype(_MMT), dn, preferred_element_type=_F32)


def _dot_hi(a, b):
    return lax.dot_general(a, b, _NN, precision=lax.Precision.HIGHEST, preferred_element_type=_F32)


def _pick(n, target, unit=_LANE):
    if n <= target:
        return n
    best = None
    for t in range(unit, target + 1, unit):
        if n % t == 0:
            best = t
    assert best is not None, (n, target)
    return best


def _sigmoid(x):
    return 1.0 / (1.0 + jnp.exp(-x))


def _silu(x):
    return x * _sigmoid(x)


def _dsilu(x):
    s = _sigmoid(x)
    return s * (1.0 + x * (1.0 - s))


_GK = math.sqrt(2.0 / math.pi)


def _gelu(x):
    return 0.5 * x * (1.0 + jnp.tanh(_GK * (x + 0.044715 * x * x * x)))


def _dgelu(x):
    t = jnp.tanh(_GK * (x + 0.044715 * x * x * x))
    return 0.5 * (1.0 + t) + 0.5 * x * (1.0 - t * t) * _GK * (1.0 + 3.0 * 0.044715 * x * x)


def _softplus(x):
    return jnp.maximum(x, 0.0) + jnp.log(1.0 + jnp.exp(-jnp.abs(x)))


def _rspec(tb, w, cb=0):
    return pl.BlockSpec((tb, w), lambda i: (i, cb))


def _fspec(shape):
    nd = len(shape)
    return pl.BlockSpec(tuple(shape), lambda i: (0,) * nd)


def _lspec(tail, li):
    nd = len(tail)
    return pl.BlockSpec((None,) + tuple(tail), lambda i: (li,) + (0,) * nd)


_ALL8 = [(kx, ky, kc) for kx in (0, 1) for ky in (0, 1) for kc in (0, 1) if (kx, ky, kc) != (0, 0, 0)]
_CHIPS = [(1, 0, 0), (0, 1, 0), (1, 1, 0)]
_SIBLING = [(0, 0, 1)]


def _slot_all8(x, y, c):
    return 4 * x + 2 * y + c


def _slot_chip(x, y, c):
    return 2 * x + y


def _slot_core(x, y, c):
    return c


def _xchg(srcs, flips, slot, nslots, gather, name):
    na, npeer = len(srcs), len(flips)

    def body(*refs):
        src_refs, out_refs = refs[:na], refs[na:2 * na]
        send_sems, recv_sems, local_sems = refs[2 * na:]
        x, y, c = lax.axis_index("x"), lax.axis_index("y"), lax.axis_index("c")
        mine = slot(x, y, c)
        copies = []
        for a in range(na):
            own = src_refs[a] if gather else src_refs[a].at[mine]
            loc = pltpu.make_async_copy(own, out_refs[a].at[mine], local_sems.at[a])
            loc.start()
            copies.append(loc)
            for p, (kx, ky, kc) in enumerate(flips):
                px = 1 - x if kx else x
                py = 1 - y if ky else y
                pc = 1 - c if kc else c
                src = src_refs[a] if gather else src_refs[a].at[slot(px, py, pc)]
                cp = pltpu.make_async_remote_copy(
                    src_ref=src, dst_ref=out_refs[a].at[mine],
                    send_sem=send_sems.at[a * npeer + p], recv_sem=recv_sems.at[a * npeer + p],
                    device_id=(px, py, pc), device_id_type=_MESH)
                cp.start()
                copies.append(cp)
        for cp in copies:
            cp.wait()

    hbm = pl.BlockSpec(memory_space=pl.ANY)
    outs = pl.pallas_call(
        body, name=name,
        out_shape=[jax.ShapeDtypeStruct((nslots,) + (s.shape if gather else s.shape[1:]), s.dtype) for s in srcs],
        in_specs=[hbm] * na, out_specs=[hbm] * na,
        scratch_shapes=[pltpu.SemaphoreType.DMA((na * npeer,)), pltpu.SemaphoreType.DMA((na * npeer,)),
                        pltpu.SemaphoreType.DMA((na,))],
    )(*srcs)
    return list(outs)


def _mm(a, b, name, li=None, trans_b=False, out_dtype=_F32, tm=1024, tn=512, tk=2048):
    M, K = a.shape
    bs = b.shape[-2:]
    N = bs[0] if trans_b else bs[1]
    tm, tn, tk = _pick(M, tm, 8), _pick(N, tn), _pick(K, tk)
    nk = K // tk
    lead = () if li is None else (None,)

    def bmap(i, j, k):
        idx = (j, k) if trans_b else (k, j)
        return idx if li is None else (li,) + idx

    def body(a_ref, b_ref, o_ref, acc):
        k = pl.program_id(2)
        part = lax.dot_general(a_ref[...], b_ref[...], _NT if trans_b else _NN, preferred_element_type=_F32)
        if nk == 1:
            o_ref[...] = part.astype(o_ref.dtype)
        else:
            @pl.when(k == 0)
            def _():
                acc[...] = part

            @pl.when(k > 0)
            def _():
                acc[...] += part

            @pl.when(k == nk - 1)
            def _():
                o_ref[...] = acc[...].astype(o_ref.dtype)

    return pl.pallas_call(
        body, name=name, grid=(M // tm, N // tn, nk),
        in_specs=[pl.BlockSpec((tm, tk), lambda i, j, k: (i, k)),
                  pl.BlockSpec(lead + ((tn, tk) if trans_b else (tk, tn)), bmap)],
        out_specs=pl.BlockSpec((tm, tn), lambda i, j, k: (i, j)),
        out_shape=jax.ShapeDtypeStruct((M, N), out_dtype),
        scratch_shapes=[pltpu.VMEM((tm, tn) if nk > 1 else (8, _LANE), _F32)],
        compiler_params=_cp("parallel", "parallel", "arbitrary"),
    )(a, b)


def _mm_tn(a, b, name, tm=512, tn=512):
    T, M = a.shape
    N = b.shape[1]
    tm, tn = _pick(M, tm), _pick(N, tn)

    def body(a_ref, b_ref, o_ref):
        o_ref[...] = lax.dot_general(a_ref[...], b_ref[...], _TN, preferred_element_type=_F32)

    return pl.pallas_call(
        body, name=name, grid=(M // tm, N // tn),
        in_specs=[pl.BlockSpec((T, tm), lambda i, j: (0, i)), pl.BlockSpec((T, tn), lambda i, j: (0, j))],
        out_specs=pl.BlockSpec((tm, tn), lambda i, j: (i, j)),
        out_shape=jax.ShapeDtypeStruct((M, N), _F32),
        compiler_params=_cp("parallel", "parallel"),
    )(a, b)


def _ada_fwd(c_all, ada_w, name):
    L, D, Ws = ada_w.shape
    B = c_all.shape[0]

    def body(c_ref, w_ref, o_ref):
        o_ref[...] = _dot(_silu(c_ref[...]), w_ref[...])

    return pl.pallas_call(
        body, name=name, grid=(L,),
        in_specs=[_fspec((B, D)), pl.BlockSpec((None, D, Ws), lambda l: (l, 0, 0))],
        out_specs=pl.BlockSpec((None, B, Ws), lambda l: (l, 0, 0)),
        out_shape=jax.ShapeDtypeStruct((L, B, Ws), _F32), compiler_params=_cp("parallel"),
    )(c_all, ada_w)


def _ada_bwd(c_all_t, dmod, name):
    D, B = c_all_t.shape
    L, _, Ws = dmod.shape

    def body(c_ref, d_ref, o_ref):
        ct = _silu(c_ref[...])
        d = d_ref[...]
        acc = ct[:, 0:1] * d[0:1, :]
        for b in range(1, B):
            acc = acc + ct[:, b:b + 1] * d[b:b + 1, :]
        o_ref[...] = acc

    return pl.pallas_call(
        body, name=name, grid=(L,),
        in_specs=[_fspec((D, B)), pl.BlockSpec((None, B, Ws), lambda l: (l, 0, 0))],
        out_specs=pl.BlockSpec((None, D, Ws), lambda l: (l, 0, 0)),
        out_shape=jax.ShapeDtypeStruct((L, D, Ws), _F32), compiler_params=_cp("parallel"),
    )(c_all_t, dmod)


def _norm_mod(x, g, sc, sh, name, tb=512):
    T, D = x.shape
    tb = _pick(T, tb, 8)

    def body(x_ref, g_ref, sc_ref, sh_ref, h_ref):
        xv = x_ref[...]
        r = lax.rsqrt(jnp.mean(xv * xv, axis=1, keepdims=True) + _EPS)
        h_ref[...] = (xv * r * (g_ref[...] * (1.0 + sc_ref[...])) + sh_ref[...]).astype(h_ref.dtype)

    return pl.pallas_call(
        body, name=name, grid=(T // tb,),
        in_specs=[_rspec(tb, D), _fspec((1, D)), _fspec((1, D)), _fspec((1, D))],
        out_specs=_rspec(tb, D), out_shape=jax.ShapeDtypeStruct((T, D), _MMT), compiler_params=_cp("parallel"),
    )(x, g, sc, sh)


def _norm_mod_bwd(x, dh, dres, g, sc, name, tb=256):
    T, D = x.shape
    tb = _pick(T, tb, 8)

    def body(x_ref, dh_ref, dr_ref, g_ref, sc_ref, dx_ref, dgm_ref, dsh_ref):
        i = pl.program_id(0)
        xv, dh_ = x_ref[...], dh_ref[...]
        r = lax.rsqrt(jnp.mean(xv * xv, axis=1, keepdims=True) + _EPS)
        xn = xv * r
        dxn = dh_ * (g_ref[...] * (1.0 + sc_ref[...]))
        dx_ref[...] = dr_ref[...] + r * (dxn - xn * jnp.mean(dxn * xn, axis=1, keepdims=True))

        @pl.when(i == 0)
        def _():
            dgm_ref[...] = jnp.zeros_like(dgm_ref)
            dsh_ref[...] = jnp.zeros_like(dsh_ref)

        dgm_ref[...] += jnp.sum(dh_ * xn, axis=0, keepdims=True)
        dsh_ref[...] += jnp.sum(dh_, axis=0, keepdims=True)

    return pl.pallas_call(
        body, name=name, grid=(T // tb,),
        in_specs=[_rspec(tb, D), _rspec(tb, D), _rspec(tb, D), _fspec((1, D)), _fspec((1, D))],
        out_specs=[_rspec(tb, D), _fspec((1, D)), _fspec((1, D))],
        out_shape=[jax.ShapeDtypeStruct((T, D), _F32), jax.ShapeDtypeStruct((1, D), _F32),
                   jax.ShapeDtypeStruct((1, D), _F32)],
        compiler_params=_cp("arbitrary"),
    )(x, dh, dres, g, sc)


def _resid(x, p, gt, name, tb=512):
    T, D = x.shape
    tb = _pick(T, tb, 8)

    def body(x_ref, p_ref, gt_ref, o_ref):
        o_ref[...] = x_ref[...] + gt_ref[...] * p_ref[...]

    return pl.pallas_call(
        body, name=name, grid=(T // tb,), in_specs=[_rspec(tb, D), _rspec(tb, D), _fspec((1, D))],
        out_specs=_rspec(tb, D), out_shape=jax.ShapeDtypeStruct((T, D), _F32), compiler_params=_cp("parallel"),
    )(x, p, gt)


def _resid_bwd(dx, p, gt, name, tb=512):
    T, D = dx.shape
    tb = _pick(T, tb, 8)

    def body(dx_ref, p_ref, gt_ref, dp_ref, dgt_ref):
        i = pl.program_id(0)
        d = dx_ref[...]
        dp_ref[...] = (d * gt_ref[...]).astype(dp_ref.dtype)

        @pl.when(i == 0)
        def _():
            dgt_ref[...] = jnp.zeros_like(dgt_ref)

        dgt_ref[...] += jnp.sum(d * p_ref[...], axis=0, keepdims=True)

    return pl.pallas_call(
        body, name=name, grid=(T // tb,), in_specs=[_rspec(tb, D), _rspec(tb, D), _fspec((1, D))],
        out_specs=[_rspec(tb, D), _fspec((1, D))],
        out_shape=[jax.ShapeDtypeStruct((T, D), _MMT), jax.ShapeDtypeStruct((1, D), _F32)],
        compiler_params=_cp("arbitrary"),
    )(dx, p, gt)


def _gmlp_chunk(u_raw, v_raw, sw_ref, sbt, gv, G):
    u, v = _gelu(u_raw), _gelu(v_raw)
    ii = lax.broadcasted_iota(jnp.int32, (_AC, _AC), 0)
    jj = lax.broadcasted_iota(jnp.int32, (_AC, _AC), 1)
    out = []
    for gi in range(G):
        sl = slice(gi * _GD, (gi + 1) * _GD)
        vg = v[:, sl]
        r = lax.rsqrt(jnp.mean(vg * vg, axis=1, keepdims=True) + _EPS)
        vhat = vg * r
        W = jnp.where(jj <= ii, sw_ref[gi], 0.0)
        s = _dot(W, vhat * gv[:, sl]) + sbt[:, gi:gi + 1]
        out.append((u[:, sl], s, vhat, r, W))
    return out


def _gmlp_fwd(proj, sw, sbt, gv, li, D, name):
    T = proj.shape[0]
    G = D // _GD

    def body(u_ref, v_ref, sw_ref, sbt_ref, gv_ref, y_ref):
        parts = _gmlp_chunk(u_ref[...], v_ref[...], sw_ref, sbt_ref[...], gv_ref[...], G)
        for gi, (u, s, _, _, _) in enumerate(parts):
            y_ref[:, gi * _GD:(gi + 1) * _GD] = (u * s).astype(y_ref.dtype)

    return pl.pallas_call(
        body, name=name, grid=(T // _AC,),
        in_specs=[_rspec(_AC, D, 0), _rspec(_AC, D, 1), _lspec((G, _AC, _AC), li), _lspec((_AC, G), li),
                  _lspec((1, D), li)],
        out_specs=_rspec(_AC, D), out_shape=jax.ShapeDtypeStruct((T, D), _MMT), compiler_params=_cp("parallel"),
    )(proj, proj, sw, sbt, gv)


def _gmlp_bwd(proj, dy, sw, sbt, gv, li, D, name):
    T = proj.shape[0]
    G = D // _GD

    def body(u_ref, v_ref, dy_ref, sw_ref, sbt_ref, gv_ref, duv_ref, dsw_ref, dsa_ref, dgv_ref):
        i = pl.program_id(0)

        @pl.when(i == 0)
        def _():
            dsw_ref[...] = jnp.zeros_like(dsw_ref)
            dsa_ref[...] = jnp.zeros_like(dsa_ref)
            dgv_ref[...] = jnp.zeros_like(dgv_ref)

        u_raw, v_raw, dy_, gv_ = u_ref[...], v_ref[...], dy_ref[...], gv_ref[...]
        parts = _gmlp_chunk(u_raw, v_raw, sw_ref, sbt_ref[...], gv_, G)
        ii = lax.broadcasted_iota(jnp.int32, (_AC, _AC), 0)
        jj = lax.broadcasted_iota(jnp.int32, (_AC, _AC), 1)
        dgu, dgv = _dgelu(u_raw), _dgelu(v_raw)
        for gi, (u, s, vhat, r, W) in enumerate(parts):
            sl = slice(gi * _GD, (gi + 1) * _GD)
            dyg = dy_[:, sl]
            ds = dyg * u
            vn = vhat * gv_[:, sl]
            dsw_ref[gi] += jnp.where(jj <= ii, _dot(ds, vn, _NT), 0.0)
            dsa_ref[:, sl] += ds
            dvn = _dot(W, ds, _TN)
            dgv_ref[:, sl] += jnp.sum(dvn * vhat, axis=0, keepdims=True)
            dvh = dvn * gv_[:, sl]
            dvg = r * (dvh - vhat * jnp.mean(dvh * vhat, axis=1, keepdims=True))
            duv_ref[:, sl] = (dyg * s * dgu[:, sl]).astype(duv_ref.dtype)
            duv_ref[:, D + gi * _GD:D + (gi + 1) * _GD] = (dvg * dgv[:, sl]).astype(duv_ref.dtype)

    return pl.pallas_call(
        body, name=name, grid=(T // _AC,),
        in_specs=[_rspec(_AC, D, 0), _rspec(_AC, D, 1), _rspec(_AC, D), _lspec((G, _AC, _AC), li),
                  _lspec((_AC, G), li), _lspec((1, D), li)],
        out_specs=[_rspec(_AC, 2 * D), _fspec((G, _AC, _AC)), _fspec((_AC, D)), _fspec((1, D))],
        out_shape=[jax.ShapeDtypeStruct((T, 2 * D), _MMT), jax.ShapeDtypeStruct((G, _AC, _AC), _F32),
                   jax.ShapeDtypeStruct((_AC, D), _F32), jax.ShapeDtypeStruct((1, D), _F32)],
        compiler_params=_cp("arbitrary"),
    )(proj, proj, dy, sw, sbt, gv)


def _conv_taps(halo, cur, first):
    tb = cur.shape[0]
    full = jnp.concatenate([jnp.where(first, 0.0, halo), cur], axis=0)
    return [full[8:] if j == _KC - 1 else pltpu.roll(full, _KC - 1 - j, 0)[8:] for j in range(_KC)]


def _prev_spec(tb, w, cb):
    return pl.BlockSpec((8, w), lambda i: (jnp.maximum(i * (tb // 8) - 1, 0), cb))


def _l2_heads(x, H):
    outs, rs = [], []
    for h in range(H):
        xh = x[:, h * _GD:(h + 1) * _GD]
        r = lax.rsqrt(jnp.sum(xh * xh, axis=1, keepdims=True) + _EPS)
        outs.append(xh * r)
        rs.append(r)
    return outs, rs


def _gate_rows(ba, alog_row, dtb_row, H):
    lane = lax.broadcasted_iota(jnp.int32, ba.shape, 1)
    beta = _sigmoid(ba)
    g = -jnp.exp(alog_row) * _softplus(ba + dtb_row)
    return lane, beta, g


def _conv_fwd(proj, cw, alog_row, dtb_row, li, D, name, tb=256):
    T = proj.shape[0]
    H = D // _GD
    tb = _pick(T, tb, 8)
    bac = (8 * D) // _LANE

    def body(q_ref, k_ref, v_ref, qh_ref, kh_ref, vh_ref, ba_ref, cw_ref, al_ref, dtb_ref,
             qo_ref, ko_ref, vo_ref, bg_ref):
        first = pl.program_id(0) == 0
        cw_ = cw_ref[...]
        for idx, (cur, halo, out) in enumerate(((q_ref, qh_ref, qo_ref), (k_ref, kh_ref, ko_ref),
                                                 (v_ref, vh_ref, vo_ref))):
            taps = _conv_taps(halo[...], cur[...], first)
            w = cw_[:, idx * D:(idx + 1) * D]
            cv = taps[0] * w[0:1, :]
            for j in range(1, _KC):
                cv = cv + taps[j] * w[j:j + 1, :]
            act = _silu(cv)
            if idx < 2:
                outs, _ = _l2_heads(act, H)
                for h in range(H):
                    out[:, h * _GD:(h + 1) * _GD] = outs[h]
            else:
                out[...] = act
        lane, beta, g = _gate_rows(ba_ref[...], al_ref[...], dtb_ref[...], H)
        bg_ref[...] = jnp.where(lane < H, beta, jnp.where(lane < 2 * H, g, 0.0))

    return pl.pallas_call(
        body, name=name, grid=(T // tb,),
        in_specs=[_rspec(tb, D, 2), _rspec(tb, D, 3), _rspec(tb, D, 4),
                  _prev_spec(tb, D, 2), _prev_spec(tb, D, 3), _prev_spec(tb, D, 4),
                  _rspec(tb, _LANE, bac), _lspec((_KC, 3 * D), li), _lspec((1, _LANE), li), _lspec((1, _LANE), li)],
        out_specs=[_rspec(tb, D), _rspec(tb, D), _rspec(tb, D), _rspec(tb, _LANE)],
        out_shape=[jax.ShapeDtypeStruct((T, D), _F32)] * 3 + [jax.ShapeDtypeStruct((T, _LANE), _F32)],
        compiler_params=_cp("parallel"),
    )(proj, proj, proj, proj, proj, proj, proj, cw, alog_row, dtb_row)


def _conv_bwd1(proj, dqn, dkn, dvs, dbg, cw, alog_row, dtb_row, li, D, name, tb=256):
    T = proj.shape[0]
    H = D // _GD
    tb = _pick(T, tb, 8)
    bac = (8 * D) // _LANE

    def body(q_ref, k_ref, v_ref, qh_ref, kh_ref, vh_ref, ba_ref, dq_ref, dk_ref, dv_ref, dbg_ref,
             cw_ref, al_ref, dtb_ref, dc_ref, dba_ref, dcw_ref, dal_ref, ddt_ref):
        i = pl.program_id(0)
        first = i == 0

        @pl.when(first)
        def _():
            dcw_ref[...] = jnp.zeros_like(dcw_ref)
            dal_ref[...] = jnp.zeros_like(dal_ref)
            ddt_ref[...] = jnp.zeros_like(ddt_ref)

        cw_ = cw_ref[...]
        for idx, (cur, halo, dref) in enumerate(((q_ref, qh_ref, dq_ref), (k_ref, kh_ref, dk_ref),
                                                  (v_ref, vh_ref, dv_ref))):
            taps = _conv_taps(halo[...], cur[...], first)
            w = cw_[:, idx * D:(idx + 1) * D]
            cv = taps[0] * w[0:1, :]
            for j in range(1, _KC):
                cv = cv + taps[j] * w[j:j + 1, :]
            dact = dref[...]
            if idx < 2:
                outs, rs = _l2_heads(_silu(cv), H)
                pieces = []
                for h in range(H):
                    dy = dact[:, h * _GD:(h + 1) * _GD]
                    pieces.append(rs[h] * (dy - outs[h] * jnp.sum(dy * outs[h], axis=1, keepdims=True)))
                dact = jnp.concatenate(pieces, axis=1)
            dcv = dact * _dsilu(cv)
            dc_ref[:, idx * D:(idx + 1) * D] = dcv
            for j in range(_KC):
                dcw_ref[j:j + 1, idx * D:(idx + 1) * D] += jnp.sum(dcv * taps[j], axis=0, keepdims=True)

        ba = ba_ref[...]
        lane, beta, g = _gate_rows(ba, al_ref[...], dtb_ref[...], H)
        dbg_ = dbg_ref[...]
        is_b, is_a = lane < H, jnp.logical_and(lane >= H, lane < 2 * H)
        da = dbg_ * (-jnp.exp(al_ref[...])) * _sigmoid(ba + dtb_ref[...])
        dba_ref[...] = jnp.where(is_b, dbg_ * beta * (1.0 - beta), jnp.where(is_a, da, 0.0)).astype(dba_ref.dtype)
        dal_ref[...] += jnp.sum(jnp.where(is_a, dbg_ * g, 0.0), axis=0, keepdims=True)
        ddt_ref[...] += jnp.sum(jnp.where(is_a, da, 0.0), axis=0, keepdims=True)

    return pl.pallas_call(
        body, name=name, grid=(T // tb,),
        in_specs=[_rspec(tb, D, 2), _rspec(tb, D, 3), _rspec(tb, D, 4),
                  _prev_spec(tb, D, 2), _prev_spec(tb, D, 3), _prev_spec(tb, D, 4),
                  _rspec(tb, _LANE, bac), _rspec(tb, D), _rspec(tb, D), _rspec(tb, D), _rspec(tb, _LANE),
                  _lspec((_KC, 3 * D), li), _lspec((1, _LANE), li), _lspec((1, _LANE), li)],
        out_specs=[_rspec(tb, 3 * D), _rspec(tb, _LANE), _fspec((_KC, 3 * D)), _fspec((1, _LANE)), _fspec((1, _LANE))],
        out_shape=[jax.ShapeDtypeStruct((T, 3 * D), _F32), jax.ShapeDtypeStruct((T, _LANE), _MMT),
                   jax.ShapeDtypeStruct((_KC, 3 * D), _F32), jax.ShapeDtypeStruct((1, _LANE), _F32),
                   jax.ShapeDtypeStruct((1, _LANE), _F32)],
        compiler_params=_cp("arbitrary"),
    )(proj, proj, proj, proj, proj, proj, proj, dqn, dkn, dvs, dbg, cw, alog_row, dtb_row)


def _conv_bwd2(dc, cw, li, name, tb=256):
    T, W = dc.shape
    tb = _pick(T, tb, 8)
    nb8 = T // 8

    def body(dc_ref, nx_ref, cw_ref, o_ref):
        last = pl.program_id(0) == pl.num_programs(0) - 1
        full = jnp.concatenate([dc_ref[...], jnp.where(last, 0.0, nx_ref[...])], axis=0)
        w = cw_ref[...]
        acc = full[:tb] * w[_KC - 1:_KC, :]
        for j in range(_KC - 1):
            sh = _KC - 1 - j
            acc = acc + pltpu.roll(full, tb + 8 - sh, 0)[:tb] * w[j:j + 1, :]
        o_ref[...] = acc.astype(o_ref.dtype)

    return pl.pallas_call(
        body, name=name, grid=(T // tb,),
        in_specs=[_rspec(tb, W), pl.BlockSpec((8, W), lambda i: (jnp.minimum((i + 1) * (tb // 8), nb8 - 1), 0)),
                  _lspec((_KC, W), li)],
        out_specs=_rspec(tb, W), out_shape=jax.ShapeDtypeStruct((T, W), _MMT), compiler_params=_cp("parallel"),
    )(dc, dc, cw)


def _inv_unit_lower(a):
    C = a.shape[0]
    ii = lax.broadcasted_iota(jnp.int32, (C, C), 0)
    jj = lax.broadcasted_iota(jnp.int32, (C, C), 1)
    x = jnp.where(ii == jj, 1.0, 0.0) - a
    p = a
    n = 1
    while 2 * n < C:
        p = _dot_hi(p, p)
        x = x + _dot_hi(x, p)
        n *= 2
    return x


def _gdn_chunk(q, k, v, g_row, b_row):
    C = q.shape[0]
    ii = lax.broadcasted_iota(jnp.int32, (C, C), 0)
    jj = lax.broadcasted_iota(jnp.int32, (C, C), 1)
    low, strict, eye = jj <= ii, jj < ii, ii == jj
    g_col = jnp.sum(jnp.where(eye, g_row, 0.0), axis=1, keepdims=True)
    b_col = jnp.sum(jnp.where(eye, b_row, 0.0), axis=1, keepdims=True)
    gam_col = jnp.sum(jnp.where(low, g_row, 0.0), axis=1, keepdims=True)
    gam_row = jnp.sum(jnp.where(jj >= ii, g_col, 0.0), axis=0, keepdims=True)
    gam_last = jnp.sum(g_row, axis=1, keepdims=True)
    decay = jnp.where(low, jnp.exp(jnp.where(low, gam_col - gam_row, 0.0)), 0.0)
    eg = jnp.exp(gam_col)
    ekd = jnp.exp(gam_last - gam_col)
    qs = q * (_GD ** -0.5)
    kb = k * b_col
    kk = _dot(kb, k, _NT)
    qkraw = _dot(qs, k, _NT)
    return dict(low=low, strict=strict, eye=eye, ii=ii, jj=jj, b_col=b_col, decay=decay, eg=eg, ekd=ekd,
                gl=jnp.exp(gam_last), qs=qs, kb=kb, kk=kk, qkraw=qkraw,
                A=jnp.where(strict, kk * decay, 0.0), vb=v * b_col, kbg=kb * eg,
                qk=qkraw * decay, q_dec=qs * eg, k_dec=k * ekd)


def _gdn_fwd(qn, kn, vs, g_r, b_r, name):
    T, D = qn.shape
    H, N, C = D // _GD, T // _BC, _BC

    def body(q_ref, k_ref, v_ref, g_ref, b_ref, o_ref, s_ref, t_ref, S):
        @pl.when(pl.program_id(1) == 0)
        def _():
            S[...] = jnp.zeros_like(S)

        cm = _gdn_chunk(q_ref[...], k_ref[...], v_ref[...], g_ref[...], b_ref[...])
        tm = _inv_unit_lower(cm["A"])
        u, w = _dot(tm, cm["vb"]), _dot(tm, cm["kbg"])
        s0 = S[...]
        s_ref[...] = s0
        t_ref[...] = tm
        v_new = u - _dot(w, s0)
        o_ref[...] = _dot(cm["q_dec"], s0) + _dot(cm["qk"], v_new)
        S[...] = s0 * cm["gl"] + _dot(cm["k_dec"], v_new, _TN)

    qspec = pl.BlockSpec((C, _GD), lambda h, n: (n, h))
    gspec = pl.BlockSpec((None, None, 1, C), lambda h, n: (h, n, 0, 0))
    return pl.pallas_call(
        body, name=name, grid=(H, N),
        in_specs=[qspec, qspec, qspec, gspec, gspec],
        out_specs=[qspec, pl.BlockSpec((None, None, _GD, _GD), lambda h, n: (h, n, 0, 0)),
                   pl.BlockSpec((None, None, C, C), lambda h, n: (h, n, 0, 0))],
        out_shape=[jax.ShapeDtypeStruct((T, D), _F32), jax.ShapeDtypeStruct((H, N, _GD, _GD), _F32),
                   jax.ShapeDtypeStruct((H, N, C, C), _F32)],
        scratch_shapes=[pltpu.VMEM((_GD, _GD), _F32)],
        compiler_params=_cp("arbitrary", "arbitrary"),
    )(qn, kn, vs, g_r, b_r)


def _gdn_bwd(qn, kn, vs, g_r, b_r, s_all, t_all, do, name):
    T, D = qn.shape
    H, N, C = D // _GD, T // _BC, _BC

    def body(q_ref, k_ref, v_ref, g_ref, b_ref, s_ref, t_ref, do_ref, dq_ref, dk_ref, dv_ref, dg_ref, db_ref, dS):
        @pl.when(pl.program_id(1) == 0)
        def _():
            dS[...] = jnp.zeros_like(dS)

        q, k, v = q_ref[...], k_ref[...], v_ref[...]
        cm = _gdn_chunk(q, k, v, g_ref[...], b_ref[...])
        low, strict, eye, ii, jj = cm["low"], cm["strict"], cm["eye"], cm["ii"], cm["jj"]
        decay, eg, ekd, gl, b_col = cm["decay"], cm["eg"], cm["ekd"], cm["gl"], cm["b_col"]
        tm, s0, do_, ds1 = t_ref[...], s_ref[...], do_ref[...], dS[...]
        u, w = _dot(tm, cm["vb"]), _dot(tm, cm["kbg"])
        v_new = u - _dot(w, s0)
        dv_new = _dot(cm["qk"], do_, _TN) + _dot(cm["k_dec"], ds1)
        dqk = jnp.where(low, _dot(do_, v_new, _NT), 0.0)
        dq_dec = _dot(do_, s0, _NT)
        dk_dec = _dot(v_new, ds1, _NT)
        dgl = jnp.sum(jnp.sum(ds1 * s0, axis=1, keepdims=True), axis=0, keepdims=True)
        dw = -_dot(dv_new, s0, _NT)
        dS[...] = _dot(cm["q_dec"], do_, _TN) + gl * ds1 - _dot(w, dv_new, _TN)
        dvb, dkbg = _dot(tm, dv_new, _TN), _dot(tm, dw, _TN)
        dA = -jnp.where(strict, _dot(dvb, u, _NT) + _dot(dkbg, w, _NT), 0.0)
        dkk, dqkraw = dA * decay, dqk * decay
        E = (dA * cm["kk"] + dqk * cm["qkraw"]) * decay
        dkb = _dot(dkk, k) + dkbg * eg
        dqs = _dot(dqkraw, k) + dq_dec * eg
        dk_ref[...] = _dot(dqkraw, cm["qs"], _TN) + _dot(dkk, cm["kb"], _TN) + dk_dec * ekd + dkb * b_col
        dv_ref[...] = dvb * b_col
        dq_ref[...] = dqs * (_GD ** -0.5)
        dbeta_col = jnp.sum(dvb * v + dkb * k, axis=1, keepdims=True)
        t_kd = jnp.sum(dk_dec * k, axis=1, keepdims=True) * ekd
        c1 = (jnp.sum(E, axis=1, keepdims=True) + jnp.sum(dkbg * cm["kb"], axis=1, keepdims=True) * eg
              + jnp.sum(dq_dec * cm["qs"], axis=1, keepdims=True) * eg - t_kd)
        r1 = jnp.sum(E, axis=0, keepdims=True)
        dgam_last = jnp.sum(t_kd, axis=0, keepdims=True) + dgl * gl
        dgam_col = c1 - jnp.sum(jnp.where(eye, r1, 0.0), axis=1, keepdims=True)
        dg_ref[...] = jnp.sum(jnp.where(ii >= jj, dgam_col, 0.0), axis=0, keepdims=True) + dgam_last
        db_ref[...] = jnp.sum(jnp.where(eye, dbeta_col, 0.0), axis=0, keepdims=True)

    qspec = pl.BlockSpec((C, _GD), lambda h, n: (N - 1 - n, h))
    gspec = pl.BlockSpec((None, None, 1, C), lambda h, n: (h, N - 1 - n, 0, 0))
    return pl.pallas_call(
        body, name=name, grid=(H, N),
        in_specs=[qspec, qspec, qspec, gspec, gspec,
                  pl.BlockSpec((None, None, _GD, _GD), lambda h, n: (h, N - 1 - n, 0, 0)),
                  pl.BlockSpec((None, None, C, C), lambda h, n: (h, N - 1 - n, 0, 0)), qspec],
        out_specs=[qspec, qspec, qspec, gspec, gspec],
        out_shape=[jax.ShapeDtypeStruct((T, D), _F32)] * 3 + [jax.ShapeDtypeStruct((H, N, 1, C), _F32)] * 2,
        scratch_shapes=[pltpu.VMEM((_GD, _GD), _F32)],
        compiler_params=_cp("arbitrary", "arbitrary"),
    )(qn, kn, vs, g_r, b_r, s_all, t_all, do)


def _onorm_fwd(o, proj, go, li, D, name, tb=512):
    T = o.shape[0]
    H = D // _GD
    tb = _pick(T, tb, 8)

    def body(o_ref, z_ref, go_ref, y_ref):
        ov, zv, g = o_ref[...], z_ref[...], go_ref[...]
        for h in range(H):
            sl = slice(h * _GD, (h + 1) * _GD)
            oh = ov[:, sl]
            r = lax.rsqrt(jnp.mean(oh * oh, axis=1, keepdims=True) + _EPS)
            y_ref[:, sl] = (oh * r * g * _silu(zv[:, sl])).astype(y_ref.dtype)

    return pl.pallas_call(
        body, name=name, grid=(T // tb,), in_specs=[_rspec(tb, D), _rspec(tb, D, 5), _lspec((1, _GD), li)],
        out_specs=_rspec(tb, D), out_shape=jax.ShapeDtypeStruct((T, D), _MMT), compiler_params=_cp("parallel"),
    )(o, proj, go)


def _onorm_bwd(dy, o, proj, go, li, D, name, tb=256):
    T = o.shape[0]
    H = D // _GD
    tb = _pick(T, tb, 8)

    def body(dy_ref, o_ref, z_ref, go_ref, do_ref, dz_ref, dgo_ref):
        @pl.when(pl.program_id(0) == 0)
        def _():
            dgo_ref[...] = jnp.zeros_like(dgo_ref)

        dyv, ov, zv, g = dy_ref[...], o_ref[...], z_ref[...], go_ref[...]
        dgo = jnp.zeros((1, _GD), _F32)
        for h in range(H):
            sl = slice(h * _GD, (h + 1) * _GD)
            oh, zh, dyh = ov[:, sl], zv[:, sl], dyv[:, sl]
            r = lax.rsqrt(jnp.mean(oh * oh, axis=1, keepdims=True) + _EPS)
            on = oh * r
            sz = _silu(zh)
            dgo = dgo + jnp.sum(dyh * sz * on, axis=0, keepdims=True)
            don = dyh * sz * g
            do_ref[:, sl] = r * (don - on * jnp.mean(don * on, axis=1, keepdims=True))
            dz_ref[:, sl] = (dyh * on * g * _dsilu(zh)).astype(dz_ref.dtype)
        dgo_ref[...] += dgo

    return pl.pallas_call(
        body, name=name, grid=(T // tb,),
        in_specs=[_rspec(tb, D), _rspec(tb, D), _rspec(tb, D, 5), _lspec((1, _GD), li)],
        out_specs=[_rspec(tb, D), _rspec(tb, D), _fspec((1, _GD))],
        out_shape=[jax.ShapeDtypeStruct((T, D), _F32), jax.ShapeDtypeStruct((T, D), _MMT),
                   jax.ShapeDtypeStruct((1, _GD), _F32)],
        compiler_params=_cp("arbitrary"),
    )(dy, o, proj, go)


def _merge(pa, pb, proj, D, name, tb=512):
    T = pa.shape[0]
    tb = _pick(T, tb, 8)

    def body(pa_ref, pb_ref, ga_ref, gb_ref, m_ref):
        m_ref[...] = (_sigmoid(ga_ref[...]) * pa_ref[...] + _sigmoid(gb_ref[...]) * pb_ref[...]).astype(m_ref.dtype)

    return pl.pallas_call(
        body, name=name, grid=(T // tb,), in_specs=[_rspec(tb, D), _rspec(tb, D), _rspec(tb, D, 6), _rspec(tb, D, 7)],
        out_specs=_rspec(tb, D), out_shape=jax.ShapeDtypeStruct((T, D), _MMT), compiler_params=_cp("parallel"),
    )(pa, pb, proj, proj)


def _merge_bwd(dm, pa, pb, proj, D, name, tb=256):
    T = pa.shape[0]
    tb = _pick(T, tb, 8)

    def body(dm_ref, pa_ref, pb_ref, ga_ref, gb_ref, dpa_ref, dpb_ref, dg_ref):
        d = dm_ref[...]
        sa, sb = _sigmoid(ga_ref[...]), _sigmoid(gb_ref[...])
        dpa_ref[...] = (d * sa).astype(dpa_ref.dtype)
        dpb_ref[...] = (d * sb).astype(dpb_ref.dtype)
        dg_ref[:, :D] = (d * pa_ref[...] * sa * (1.0 - sa)).astype(dg_ref.dtype)
        dg_ref[:, D:] = (d * pb_ref[...] * sb * (1.0 - sb)).astype(dg_ref.dtype)

    return pl.pallas_call(
        body, name=name, grid=(T // tb,),
        in_specs=[_rspec(tb, D), _rspec(tb, D), _rspec(tb, D), _rspec(tb, D, 6), _rspec(tb, D, 7)],
        out_specs=[_rspec(tb, D), _rspec(tb, D), _rspec(tb, 2 * D)],
        out_shape=[jax.ShapeDtypeStruct((T, D), _MMT)] * 2 + [jax.ShapeDtypeStruct((T, 2 * D), _MMT)],
        compiler_params=_cp("parallel"),
    )(dm, pa, pb, proj, proj)


def _swiglu(gu, name, tb=256):
    T, F2 = gu.shape
    F = F2 // 2
    tb = _pick(T, tb, 8)

    def body(g_ref, u_ref, a_ref):
        a_ref[...] = (_silu(g_ref[...]) * u_ref[...]).astype(a_ref.dtype)

    return pl.pallas_call(
        body, name=name, grid=(T // tb,), in_specs=[_rspec(tb, F, 0), _rspec(tb, F, 1)],
        out_specs=_rspec(tb, F), out_shape=jax.ShapeDtypeStruct((T, F), _MMT), compiler_params=_cp("parallel"),
    )(gu, gu)


def _swiglu_bwd(da, gu, name, tb=256):
    T, F2 = gu.shape
    F = F2 // 2
    tb = _pick(T, tb, 8)

    def body(da_ref, g_ref, u_ref, o_ref):
        d, g = da_ref[...], g_ref[...]
        o_ref[:, :F] = (d * u_ref[...] * _dsilu(g)).astype(o_ref.dtype)
        o_ref[:, F:] = (d * _silu(g)).astype(o_ref.dtype)

    return pl.pallas_call(
        body, name=name, grid=(T // tb,), in_specs=[_rspec(tb, F), _rspec(tb, F, 0), _rspec(tb, F, 1)],
        out_specs=_rspec(tb, F2), out_shape=jax.ShapeDtypeStruct((T, F2), _MMT), compiler_params=_cp("parallel"),
    )(da, gu, gu)


def _loss_head(x, tgt, fg, name, tb=256):
    T, D = x.shape
    tb = _pick(T, tb, 8)

    def body(x_ref, t_ref, fg_ref, loss_ref, dx_ref, dfg_ref):
        @pl.when(pl.program_id(0) == 0)
        def _():
            loss_ref[...] = jnp.zeros_like(loss_ref)
            dfg_ref[...] = jnp.zeros_like(dfg_ref)

        xv, fg_ = x_ref[...], fg_ref[...]
        r = lax.rsqrt(jnp.mean(xv * xv, axis=1, keepdims=True) + _EPS)
        xn = xv * r
        e = xn * fg_ - t_ref[...]
        loss_ref[...] += (0.5 / D) * jnp.sum(jnp.sum(e * e, axis=1, keepdims=True), axis=0, keepdims=True)
        dy = e * (1.0 / D)
        dfg_ref[...] += jnp.sum(dy * xn, axis=0, keepdims=True)
        dxn = dy * fg_
        dx_ref[...] = r * (dxn - xn * jnp.mean(dxn * xn, axis=1, keepdims=True))

    return pl.pallas_call(
        body, name=name, grid=(T // tb,), in_specs=[_rspec(tb, D), _rspec(tb, D), _fspec((1, D))],
        out_specs=[_fspec((1, 1)), _rspec(tb, D), _fspec((1, D))],
        out_shape=[jax.ShapeDtypeStruct((1, 1), _F32), jax.ShapeDtypeStruct((T, D), _F32),
                   jax.ShapeDtypeStruct((1, D), _F32)],
        compiler_params=_cp("arbitrary"),
    )(x, tgt, fg)


def _row_tile(R, W, budget=1 << 20):
    if R * W * 4 <= budget or R % 8:
        return R
    best = 8
    for t in range(8, R + 1, 8):
        if R % t == 0 and t * W * 4 <= budget:
            best = t
    return best


def _sum_slots(st, name):
    P, R, W = st.shape
    tb = _row_tile(R, W)

    def body(s_ref, o_ref):
        acc = s_ref[0].astype(_F32)
        for p in range(1, P):
            acc = acc + s_ref[p].astype(_F32)
        o_ref[...] = acc

    return pl.pallas_call(
        body, name=name, grid=(R // tb,), in_specs=[pl.BlockSpec((P, tb, W), lambda i: (0, i, 0))],
        out_specs=_rspec(tb, W), out_shape=jax.ShapeDtypeStruct((R, W), _F32), compiler_params=_cp("parallel"),
    )(st)


def _adamw(w, gst, m, v, name):
    R, W = w.shape
    P = gst.shape[0]
    tb = _row_tile(R, W, 1 << 19)
    c1, c2 = 1.0 - _B1 ** _STEP, 1.0 - _B2 ** _STEP

    def body(w_ref, g_ref, m_ref, v_ref, go_ref, d_ref, mo_ref, vo_ref):
        g = g_ref[0]
        for p in range(1, P):
            g = g + g_ref[p]
        mn = _B1 * m_ref[...] + (1.0 - _B1) * g
        vn = _B2 * v_ref[...] + (1.0 - _B2) * (g * g)
        go_ref[...] = g
        mo_ref[...] = mn
        vo_ref[...] = vn
        d_ref[...] = -_LR * ((mn / c1) / (jnp.sqrt(vn / c2) + _AEPS) + _WD * w_ref[...])

    spec = _rspec(tb, W)
    return pl.pallas_call(
        body, name=name, grid=(R // tb,),
        in_specs=[spec, pl.BlockSpec((P, tb, W), lambda i: (0, i, 0)), spec, spec],
        out_specs=[spec] * 4, out_shape=[jax.ShapeDtypeStruct((R, W), _F32)] * 4, compiler_params=_cp("parallel"),
    )(w, gst, m, v)


def _as2d(a):
    if a.ndim == 1:
        return a.reshape(1, -1)
    return a.reshape(-1, a.shape[-1])


def kernel(x, c, ada_w, ada_b, norm1_g, w_in, conv_w, spatial_w, spatial_b, v_norm_g, a_log, dt_bias, o_norm_g, w_branch_a, w_branch_b, w_out, norm2_g, w_ffn_in, w_ffn_out, final_g, loss_target, m_ada_w, m_ada_b, m_norm1_g, m_w_in, m_conv_w, m_spatial_w, m_spatial_b, m_v_norm_g, m_a_log, m_dt_bias, m_o_norm_g, m_w_branch_a, m_w_branch_b, m_w_out, m_norm2_g, m_w_ffn_in, m_w_ffn_out, m_final_g, v_ada_w, v_ada_b, v_norm1_g, v_w_in, v_conv_w, v_spatial_w, v_spatial_b, v_v_norm_g, v_a_log, v_dt_bias, v_o_norm_g, v_w_branch_a, v_w_branch_b, v_w_out, v_norm2_g, v_w_ffn_in, v_w_ffn_out, v_final_g):
    xb, tgt = x[0], loss_target[0]
    T, D = xb.shape
    L, H, G = ada_w.shape[0], a_log.shape[1], spatial_w.shape[1]
    F = 4 * w_ffn_out.shape[1]
    N = T // _BC
    Ws = ada_w.shape[2]
    Wc = w_in.shape[2]
    PW = 8 * D + _LANE
    ix, iy, ic = lax.axis_index("x"), lax.axis_index("y"), lax.axis_index("c")
    me = 4 * ix + 2 * iy + ic

    c_all = _xchg([c], _ALL8, _slot_all8, 8, True, "gather_c")[0].reshape(8, D)
    modp = _ada_fwd(c_all, ada_w, "ada_fwd")
    n_mod, n_cw = L * 8 * Ws, L * _KC * conv_w.shape[2]
    pad = (-(n_mod + n_cw)) % _LANE
    pay = jnp.concatenate([modp.reshape(-1), conv_w.reshape(-1), jnp.zeros((pad,), _F32)]).reshape(-1, _LANE)
    pay_all = _xchg([pay], _ALL8, _slot_all8, 8, True, "gather_mod")[0].reshape(8, -1)
    mod_full = jnp.concatenate([pay_all[2 * k, :n_mod].reshape(L, 8, Ws) for k in range(4)], axis=-1)
    cw_full = jnp.concatenate([pay_all[2 * k, n_mod:n_mod + n_cw].reshape(L, _KC, -1) for k in range(4)], axis=-1)
    mod = lax.dynamic_index_in_dim(mod_full, me, axis=1, keepdims=False) + ada_b
    mods = [[mod[l, j * D:(j + 1) * D].reshape(1, D) for j in range(6)] for l in range(L)]

    big = [w_in, w_branch_a, w_branch_b, w_out, w_ffn_in, w_ffn_out]
    gin, ga_, gb_, gout, gfi, gfo = _xchg([w.astype(_MMT) for w in big], _CHIPS, _slot_chip, 4, True, "gather_w")
    w_in_f = jnp.transpose(gin, (1, 2, 0, 3)).reshape(L, D, 4 * Wc)
    w_in_p = jnp.concatenate([w_in_f[..., :6 * D], w_in_f[..., 6 * D + 2 * H:], w_in_f[..., 6 * D:6 * D + 2 * H],
                              jnp.zeros((L, D, _LANE - 2 * H), _MMT)], axis=-1)
    w_a = jnp.transpose(ga_, (1, 0, 2, 3)).reshape(L, D, D)
    w_b = jnp.transpose(gb_, (1, 0, 2, 3)).reshape(L, D, D)
    w_o = jnp.transpose(gout, (1, 0, 2, 3)).reshape(L, D, D)
    w_fi = jnp.transpose(gfi, (1, 2, 0, 3)).reshape(L, D, 2 * F)
    w_fo = jnp.transpose(gfo, (1, 0, 2, 3)).reshape(L, F, D)

    sbt = jnp.transpose(spatial_b, (0, 2, 1))
    gv3, go3 = v_norm_g.reshape(L, 1, D), o_norm_g.reshape(L, 1, _GD)
    zpad = jnp.zeros((L, _LANE - 2 * H), _F32)
    alog_row = jnp.concatenate([jnp.zeros((L, H), _F32), a_log, zpad], axis=1).reshape(L, 1, _LANE)
    dtb_row = jnp.concatenate([jnp.zeros((L, H), _F32), dt_bias, zpad], axis=1).reshape(L, 1, _LANE)

    def rows_of(tok):
        return jnp.transpose(tok.reshape(N, _BC, H), (2, 0, 1)).reshape(H, N, 1, _BC)

    def toks_of(rows):
        return jnp.transpose(rows.reshape(H, N, _BC), (1, 2, 0)).reshape(T, H)

    saved = []
    xc = xb
    for l in range(L):
        sh1, sc1, gt1, sh2, sc2, gt2 = mods[l]
        g1, g2 = norm1_g[l].reshape(1, D), norm2_g[l].reshape(1, D)
        h = _norm_mod(xc, g1, sc1, sh1, f"norm1_{l}")
        proj = _mm(h, w_in_p, f"proj_{l}", li=l, tn=640)
        ya = _gmlp_fwd(proj, spatial_w, sbt, gv3, l, D, f"gmlp_{l}")
        qn, kn, vs, bg = _conv_fwd(proj, cw_full, alog_row, dtb_row, l, D, f"conv_{l}")
        g_r, b_r = rows_of(bg[:, H:2 * H]), rows_of(bg[:, :H])
        o, s_all, t_all = _gdn_fwd(qn, kn, vs, g_r, b_r, f"gdn_{l}")
        yb = _onorm_fwd(o, proj, go3, l, D, f"onorm_{l}")
        pa = _mm(ya, w_a, f"bra_{l}", li=l)
        pb = _mm(yb, w_b, f"brb_{l}", li=l)
        mg = _merge(pa, pb, proj, D, f"merge_{l}")
        p1 = _mm(mg, w_o, f"wout_{l}", li=l)
        x1 = _resid(xc, p1, gt1, f"res1_{l}")
        h2 = _norm_mod(x1, g2, sc2, sh2, f"norm2_{l}")
        gu = _mm(h2, w_fi, f"ffin_{l}", li=l)
        act = _swiglu(gu, f"swiglu_{l}")
        p2 = _mm(act, w_fo, f"ffout_{l}", li=l)
        x2 = _resid(x1, p2, gt2, f"res2_{l}")
        saved.append(dict(x=xc, h=h, proj=proj, ya=ya, yb=yb, qn=qn, kn=kn, vs=vs, g_r=g_r, b_r=b_r, o=o,
                          s_all=s_all, t_all=t_all, pa=pa, pb=pb, mg=mg, p1=p1, x1=x1, h2=h2, gu=gu, act=act, p2=p2))
        xc = x2

    loss11, dx, dfg = _loss_head(xc, tgt, final_g.reshape(1, D), "loss_head")
    loss = lax.psum(loss11[0, 0], ("x", "y", "c"))

    gbig = {k: [None] * L for k in ("w_in", "w_a", "w_b", "w_o", "w_fi", "w_fo")}
    small = {k: [None] * L for k in ("dmod", "n1", "n2", "sw", "sb", "gv", "cw", "al", "dt", "go")}
    for l in reversed(range(L)):
        sv = saved[l]
        sh1, sc1, gt1, sh2, sc2, gt2 = mods[l]
        g1, g2 = norm1_g[l].reshape(1, D), norm2_g[l].reshape(1, D)
        proj = sv["proj"]
        dp2, dgt2 = _resid_bwd(dx, sv["p2"], gt2, f"res2b_{l}")
        da = _mm(dp2, w_fo, f"ffoutb_{l}", li=l, trans_b=True)
        gbig["w_fo"][l] = _mm_tn(sv["act"], dp2, f"ffoutw_{l}")
        dgu = _swiglu_bwd(da, sv["gu"], f"swiglub_{l}")
        dh2 = _mm(dgu, w_fi, f"ffinb_{l}", li=l, trans_b=True)
        gbig["w_fi"][l] = _mm_tn(sv["h2"], dgu, f"ffinw_{l}")
        dx1, dgm2, dsh2 = _norm_mod_bwd(sv["x1"], dh2, dx, g2, sc2, f"norm2b_{l}")
        dp1, dgt1 = _resid_bwd(dx1, sv["p1"], gt1, f"res1b_{l}")
        dmg = _mm(dp1, w_o, f"woutb_{l}", li=l, trans_b=True)
        gbig["w_o"][l] = _mm_tn(sv["mg"], dp1, f"woutw_{l}")
        dpa, dpb, dgates = _merge_bwd(dmg, sv["pa"], sv["pb"], proj, D, f"mergeb_{l}")
        dya = _mm(dpa, w_a, f"brab_{l}", li=l, trans_b=True)
        gbig["w_a"][l] = _mm_tn(sv["ya"], dpa, f"braw_{l}")
        dyb = _mm(dpb, w_b, f"brbb_{l}", li=l, trans_b=True)
        gbig["w_b"][l] = _mm_tn(sv["yb"], dpb, f"brbw_{l}")
        duv, dsw, dsa, dgv = _gmlp_bwd(proj, dya, spatial_w, sbt, gv3, l, D, f"gmlpb_{l}")
        do, dz, dgo = _onorm_bwd(dyb, sv["o"], proj, go3, l, D, f"onormb_{l}")
        dqn, dkn, dvs, dg_r, db_r = _gdn_bwd(sv["qn"], sv["kn"], sv["vs"], sv["g_r"], sv["b_r"], sv["s_all"],
                                             sv["t_all"], do, f"gdnb_{l}")
        dbg = jnp.concatenate([toks_of(db_r), toks_of(dg_r), jnp.zeros((T, _LANE - 2 * H), _F32)], axis=1)
        dc, dba, dcw, dal, ddt = _conv_bwd1(proj, dqn, dkn, dvs, dbg, cw_full, alog_row, dtb_row, l, D, f"convb_{l}")
        dqkv = _conv_bwd2(dc, cw_full, l, f"convx_{l}")
        dproj = jnp.concatenate([duv, dqkv, dz, dgates, dba], axis=1)
        dh = _mm(dproj, w_in_p, f"projb_{l}", li=l, trans_b=True, tm=512, tk=1664)
        gbig["w_in"][l] = _mm_tn(sv["h"], dproj, f"projw_{l}", tn=640)
        dx, dgm1, dsh1 = _norm_mod_bwd(sv["x"], dh, dx1, g1, sc1, f"norm1b_{l}")
        small["dmod"][l] = jnp.concatenate([dsh1, dgm1 * g1, dgt1, dsh2, dgm2 * g2, dgt2], axis=1)
        small["n1"][l], small["n2"][l] = dgm1 * (1.0 + sc1), dgm2 * (1.0 + sc2)
        small["sw"][l], small["gv"][l], small["cw"][l], small["go"][l] = dsw, dgv, dcw, dgo
        small["sb"][l] = jnp.transpose(dsa.reshape(_AC, G, _GD).sum(axis=-1))
        small["al"][l], small["dt"][l] = dal[:, H:2 * H], ddt[:, H:2 * H]
    grad_x = dx.reshape(1, T, D)

    names_small = ["dmod", "n1", "n2", "sw", "sb", "gv", "cw", "al", "dt", "go"]
    flat = [jnp.stack(small[k]).reshape(-1) for k in names_small] + [dfg.reshape(-1)]
    sizes = [f.shape[0] for f in flat]
    tot = sum(sizes)
    pad = (-tot) % 1024
    pay = jnp.concatenate(flat + [jnp.zeros((pad,), _F32)]).reshape(-1, 1024)
    sm_all = _xchg([pay], _ALL8, _slot_all8, 8, True, "gather_small")[0].reshape(8, -1)
    offs = [0]
    for s in sizes:
        offs.append(offs[-1] + s)
    part = {k: sm_all[:, offs[i]:offs[i + 1]] for i, k in enumerate(names_small + ["fg"])}
    dmod_all = part["dmod"].reshape(8, L, 6 * D)

    outs = {}

    def update(nm, w, gst, m, v):
        shp = w.shape
        w2 = _as2d(w)
        g, d, mn, vn = _adamw(w2, gst.reshape((gst.shape[0],) + w2.shape), _as2d(m), _as2d(v), f"adamw_{nm}")
        outs[nm] = (g.reshape(shp), d.reshape(shp), mn.reshape(shp), vn.reshape(shp))

    chip = 2 * ix + iy
    dmod_t = jnp.transpose(dmod_all, (1, 0, 2))
    dmod_mine = lax.dynamic_slice_in_dim(dmod_t, chip * Ws, Ws, axis=2)
    g_ada_w = _ada_bwd(jnp.transpose(c_all), dmod_mine, "ada_bwd")
    update("ada_w", ada_w, g_ada_w[None], m_ada_w, v_ada_w)
    update("ada_b", ada_b, dmod_all, m_ada_b, v_ada_b)
    update("norm1_g", norm1_g, part["n1"], m_norm1_g, v_norm1_g)
    update("norm2_g", norm2_g, part["n2"], m_norm2_g, v_norm2_g)
    update("spatial_w", spatial_w, part["sw"], m_spatial_w, v_spatial_w)
    update("spatial_b", spatial_b, part["sb"], m_spatial_b, v_spatial_b)
    update("v_norm_g", v_norm_g, part["gv"], m_v_norm_g, v_v_norm_g)
    update("a_log", a_log, part["al"], m_a_log, v_a_log)
    update("dt_bias", dt_bias, part["dt"], m_dt_bias, v_dt_bias)
    update("o_norm_g", o_norm_g, part["go"], m_o_norm_g, v_o_norm_g)
    update("final_g", final_g, part["fg"], m_final_g, v_final_g)
    cw_cols = conv_w.shape[2]
    dcw_all = part["cw"].reshape(8, L, _KC, 4 * cw_cols)
    update("conv_w", conv_w, lax.dynamic_slice_in_dim(dcw_all, chip * cw_cols, cw_cols, axis=3), m_conv_w, v_conv_w)

    dw_in_p = jnp.stack(gbig["w_in"])
    dw_in_f = jnp.concatenate([dw_in_p[..., :6 * D], dw_in_p[..., 8 * D:8 * D + 2 * H], dw_in_p[..., 6 * D:8 * D]],
                              axis=-1)
    send = [
        jnp.transpose(dw_in_f.reshape(L, D, 4, Wc), (2, 0, 1, 3)),
        jnp.transpose(jnp.stack(gbig["w_a"]).reshape(L, 4, D // 4, D), (1, 0, 2, 3)),
        jnp.transpose(jnp.stack(gbig["w_b"]).reshape(L, 4, D // 4, D), (1, 0, 2, 3)),
        jnp.transpose(jnp.stack(gbig["w_o"]).reshape(L, 4, D // 4, D), (1, 0, 2, 3)),
        jnp.transpose(jnp.stack(gbig["w_fi"]).reshape(L, D, 4, 2 * F // 4), (2, 0, 1, 3)),
        jnp.transpose(jnp.stack(gbig["w_fo"]).reshape(L, 4, F // 4, D), (1, 0, 2, 3)),
    ]
    recv = _xchg([s.astype(_MMT) for s in send], _CHIPS, _slot_chip, 4, False, "scatter_grads")
    sums = [_sum_slots(r.reshape(4, -1, r.shape[-1]), f"sum4_{i}") for i, r in enumerate(recv)]
    both = _xchg(sums, _SIBLING, _slot_core, 2, True, "swap_cores")
    big_names = ["w_in", "w_branch_a", "w_branch_b", "w_out", "w_ffn_in", "w_ffn_out"]
    big_m = [m_w_in, m_w_branch_a, m_w_branch_b, m_w_out, m_w_ffn_in, m_w_ffn_out]
    big_v = [v_w_in, v_w_branch_a, v_w_branch_b, v_w_out, v_w_ffn_in, v_w_ffn_out]
    for nm, w, gst, m, v in zip(big_names, big, both, big_m, big_v):
        update(nm, w, gst, m, v)

    order = ["ada_w", "ada_b", "norm1_g", "w_in", "conv_w", "spatial_w", "spatial_b", "v_norm_g", "a_log", "dt_bias",
             "o_norm_g", "w_branch_a", "w_branch_b", "w_out", "norm2_g", "w_ffn_in", "w_ffn_out", "final_g"]
    return (loss, grad_x, *[outs[n][0] for n in order], *[outs[n][1] for n in order],
            *[outs[n][2] for n in order], *[outs[n][3] for n in order])
```

```python
import functools
import math

import jax
import jax.numpy as jnp
from jax import lax
from jax.experimental import pallas as pl
from jax.experimental.pallas import tpu as pltpu

_F32 = jnp.float32
_BF = jnp.bfloat16
_MMT = jnp.bfloat16
_EPS = 1e-6
_GD = 128
_AC = 128
_BC = 64
_KC = 4
_HB = 4
_NCH = 8
_LANE = 128
_VMEM_LIMIT = 56 * 1024 * 1024

_LR, _B1, _B2, _AEPS, _WD, _STEP = 0.001, 0.9, 0.999, 1e-08, 0.01, 10

_NN = (((1,), (0,)), ((), ()))
_NT = (((1,), (1,)), ((), ()))
_TN = (((0,), (0,)), ((), ()))

_MESH = pl.DeviceIdType.MESH


def _cp(*sem):
    return pltpu.CompilerParams(dimension_semantics=tuple(sem), vmem_limit_bytes=_VMEM_LIMIT)


def _dot(a, b, dn=_NN):
    return lax.dot_general(a.astype(_MMT), b.astype(_MMT), dn, preferred_element_type=_F32)


def _dot_hi(a, b):
    return lax.dot_general(a, b, _NN, precision=lax.Precision.HIGHEST, preferred_element_type=_F32)


def _pick(n, target, unit=_LANE):
    if n <= target:
        return n
    best = None
    for t in range(unit, target + 1, unit):
        if n % t == 0:
            best = t
    assert best is not None, (n, target)
    return best


def _sigmoid(x):
    return 1.0 / (1.0 + jnp.exp(-x))


def _silu(x):
    return x * _sigmoid(x)


def _dsilu(x):
    s = _sigmoid(x)
    return s * (1.0 + x * (1.0 - s))


_GK = math.sqrt(2.0 / math.pi)


def _gelu(x):
    return 0.5 * x * (1.0 + jnp.tanh(_GK * (x + 0.044715 * x * x * x)))


def _dgelu(x):
    t = jnp.tanh(_GK * (x + 0.044715 * x * x * x))
    return 0.5 * (1.0 + t) + 0.5 * x * (1.0 - t * t) * _GK * (1.0 + 3.0 * 0.044715 * x * x)


def _softplus(x):
    return jnp.maximum(x, 0.0) + jnp.log(1.0 + jnp.exp(-jnp.abs(x)))


def _rspec(tb, w, cb=0):
    return pl.BlockSpec((tb, w), lambda i: (i, cb))


def _fspec(shape):
    nd = len(shape)
    return pl.BlockSpec(tuple(shape), lambda i: (0,) * nd)


def _lspec(tail, li):
    nd = len(tail)
    return pl.BlockSpec((None,) + tuple(tail), lambda i: (li,) + (0,) * nd)


_ALL8 = [(kx, ky, kc) for kx in (0, 1) for ky in (0, 1) for kc in (0, 1) if (kx, ky, kc) != (0, 0, 0)]


def _slot_all8(x, y, c):
    return 4 * x + 2 * y + c


def _xchg(srcs, flips, slot, nslots, gather, name):
    na, npeer = len(srcs), len(flips)

    def body(*refs):
        src_refs, out_refs = refs[:na], refs[na:2 * na]
        send_sems, recv_sems, local_sems = refs[2 * na:]
        x, y, c = lax.axis_index("x"), lax.axis_index("y"), lax.axis_index("c")
        mine = slot(x, y, c)
        copies = []
        for a in range(na):
            own = src_refs[a] if gather else src_refs[a].at[mine]
            loc = pltpu.make_async_copy(own, out_refs[a].at[mine], local_sems.at[a])
            loc.start()
            copies.append(loc)
            for p, (kx, ky, kc) in enumerate(flips):
                px = 1 - x if kx else x
                py = 1 - y if ky else y
                pc = 1 - c if kc else c
                src = src_refs[a] if gather else src_refs[a].at[slot(px, py, pc)]
                cp = pltpu.make_async_remote_copy(
                    src_ref=src, dst_ref=out_refs[a].at[mine],
                    send_sem=send_sems.at[a * npeer + p], recv_sem=recv_sems.at[a * npeer + p],
                    device_id=(px, py, pc), device_id_type=_MESH)
                cp.start()
                copies.append(cp)
        for cp in copies:
            cp.wait()

    hbm = pl.BlockSpec(memory_space=pl.ANY)
    outs = pl.pallas_call(
        body, name=name,
        out_shape=[jax.ShapeDtypeStruct((nslots,) + (s.shape if gather else s.shape[1:]), s.dtype) for s in srcs],
        in_specs=[hbm] * na, out_specs=[hbm] * na,
        scratch_shapes=[pltpu.SemaphoreType.DMA((na * npeer,)), pltpu.SemaphoreType.DMA((na * npeer,)),
                        pltpu.SemaphoreType.DMA((na,))],
    )(*srcs)
    return list(outs)


def _rcopy(src, dst, ssem, rsem, dev):
    return pltpu.make_async_remote_copy(src_ref=src, dst_ref=dst, send_sem=ssem, recv_sem=rsem,
                                        device_id=dev, device_id_type=_MESH)


def _other_chips(x, y):
    return [(1 - x, y), (x, 1 - y), (1 - x, 1 - y)]


_HBM = pl.BlockSpec(memory_space=pl.ANY)


def _gather_halves(shard, name):
    Rp, W = shard.shape
    Rh = Rp // 2
    rc = Rh // _NCH

    def body(s_ref, o_ref, isend, irecv, dsend, drecv, lsem):
        x, y, c = lax.axis_index("x"), lax.axis_index("y"), lax.axis_index("c")
        chip, sib = 2 * x + y, (x, y, 1 - c)
        peers = _other_chips(x, y)

        def rows(half, q):
            return pl.ds(half * Rh + q * rc, rc)

        loc = pltpu.make_async_copy(s_ref, o_ref.at[chip], lsem)
        loc.start()
        started = []
        for j, (px, py) in enumerate(peers):
            for q in range(_NCH):
                cp = _rcopy(s_ref.at[rows(c, q)], o_ref.at[chip, rows(c, q)], isend.at[j * _NCH + q],
                            irecv.at[j * _NCH + q], (px, py, c))
                cp.start()
                started.append(cp)
        for j, (px, py) in enumerate(peers):
            pchip = 2 * px + py
            for q in range(_NCH):
                blk = o_ref.at[pchip, rows(c, q)]
                _rcopy(blk, blk, isend.at[j * _NCH + q], irecv.at[j * _NCH + q], (px, py, c)).wait_recv()
                fw = _rcopy(blk, blk, dsend.at[j * _NCH + q], drecv.at[j * _NCH + q], sib)
                fw.start()
                started.append(fw)
        for j, (px, py) in enumerate(peers):
            pchip = 2 * px + py
            for q in range(_NCH):
                blk = o_ref.at[pchip, rows(1 - c, q)]
                _rcopy(blk, blk, dsend.at[j * _NCH + q], drecv.at[j * _NCH + q], sib).wait_recv()
        for cp in started:
            cp.wait_send()
        loc.wait()

    n = 3 * _NCH
    return pl.pallas_call(
        body, name=name, out_shape=jax.ShapeDtypeStruct((4, Rp, W), shard.dtype), in_specs=[_HBM], out_specs=_HBM,
        scratch_shapes=[pltpu.SemaphoreType.DMA((n,))] * 4 + [pltpu.SemaphoreType.DMA],
    )(shard)


def _send_half_to_sibling(send, name):
    P, Rp, W = send.shape
    Rh = Rp // 2
    rc = Rh // _NCH

    def body(s_ref, o_ref, ssem, rsem):
        x, y, c = lax.axis_index("x"), lax.axis_index("y"), lax.axis_index("c")
        cps = []
        for k in range(P):
            for q in range(_NCH):
                cp = _rcopy(s_ref.at[k, pl.ds((1 - c) * Rh + q * rc, rc)], o_ref.at[k, pl.ds(q * rc, rc)],
                            ssem.at[k * _NCH + q], rsem.at[k * _NCH + q], (x, y, 1 - c))
                cp.start()
                cps.append(cp)
        for cp in cps:
            cp.wait()

    return pl.pallas_call(
        body, name=name, out_shape=jax.ShapeDtypeStruct((P, Rh, W), send.dtype), in_specs=[_HBM], out_specs=_HBM,
        scratch_shapes=[pltpu.SemaphoreType.DMA((P * _NCH,))] * 2,
    )(send)


def _scatter_to_chips(cs, name):
    P, Rh, W = cs.shape
    rc = Rh // _NCH

    def body(s_ref, o_ref, ssem, rsem, lsem):
        x, y, c = lax.axis_index("x"), lax.axis_index("y"), lax.axis_index("c")
        chip = 2 * x + y
        peers = _other_chips(x, y)
        loc = pltpu.make_async_copy(s_ref.at[chip], o_ref.at[chip], lsem)
        loc.start()
        cps = []
        for j, (px, py) in enumerate(peers):
            for q in range(_NCH):
                r = pl.ds(q * rc, rc)
                cp = _rcopy(s_ref.at[2 * px + py, r], o_ref.at[chip, r], ssem.at[j * _NCH + q], rsem.at[j * _NCH + q],
                            (px, py, c))
                cp.start()
                cps.append(cp)
        for j, (px, py) in enumerate(peers):
            for q in range(_NCH):
                blk = o_ref.at[2 * px + py, pl.ds(q * rc, rc)]
                _rcopy(blk, blk, ssem.at[j * _NCH + q], rsem.at[j * _NCH + q], (px, py, c)).wait_recv()
        for cp in cps:
            cp.wait_send()
        loc.wait()

    return pl.pallas_call(
        body, name=name, out_shape=jax.ShapeDtypeStruct((P, Rh, W), cs.dtype), in_specs=[_HBM], out_specs=_HBM,
        scratch_shapes=[pltpu.SemaphoreType.DMA((3 * _NCH,))] * 2 + [pltpu.SemaphoreType.DMA],
    )(cs)


def _swap_with_sibling(v, name):
    R, W = v.shape
    rc = R // _NCH

    def body(s_ref, o_ref, ssem, rsem):
        x, y, c = lax.axis_index("x"), lax.axis_index("y"), lax.axis_index("c")
        cps = []
        for q in range(_NCH):
            r = pl.ds(q * rc, rc)
            cp = _rcopy(s_ref.at[r], o_ref.at[r], ssem.at[q], rsem.at[q], (x, y, 1 - c))
            cp.start()
            cps.append(cp)
        for cp in cps:
            cp.wait()

    return pl.pallas_call(
        body, name=name, out_shape=jax.ShapeDtypeStruct((R, W), v.dtype), in_specs=[_HBM], out_specs=_HBM,
        scratch_shapes=[pltpu.SemaphoreType.DMA((_NCH,))] * 2,
    )(v)


def _mm(a, b, name, li=None, trans_b=False, out_dtype=_F32, tm=1024, tn=512, tk=2048):
    M, K = a.shape
    bs = b.shape[-2:]
    N = bs[0] if trans_b else bs[1]
    tm, tn, tk = _pick(M, tm, 8), _pick(N, tn), _pick(K, tk)
    nk = K // tk
    lead = () if li is None else (None,)

    def bmap(i, j, k):
        idx = (j, k) if trans_b else (k, j)
        return idx if li is None else (li,) + idx

    def body(a_ref, b_ref, o_ref, acc):
        k = pl.program_id(2)
        part = lax.dot_general(a_ref[...], b_ref[...], _NT if trans_b else _NN, preferred_element_type=_F32)
        if nk == 1:
            o_ref[...] = part.astype(o_ref.dtype)
        else:
            @pl.when(k == 0)
            def _():
                acc[...] = part

            @pl.when(k > 0)
            def _():
                acc[...] += part

            @pl.when(k == nk - 1)
            def _():
                o_ref[...] = acc[...].astype(o_ref.dtype)

    return pl.pallas_call(
        body, name=name, grid=(M // tm, N // tn, nk),
        in_specs=[pl.BlockSpec((tm, tk), lambda i, j, k: (i, k)),
                  pl.BlockSpec(lead + ((tn, tk) if trans_b else (tk, tn)), bmap)],
        out_specs=pl.BlockSpec((tm, tn), lambda i, j, k: (i, j)),
        out_shape=jax.ShapeDtypeStruct((M, N), out_dtype),
        scratch_shapes=[pltpu.VMEM((tm, tn) if nk > 1 else (8, _LANE), _F32)],
        compiler_params=_cp("parallel", "parallel", "arbitrary"),
    )(a, b)


def _mm_tn(a, b, name, tm=512, tn=512):
    T, M = a.shape
    N = b.shape[1]
    tm, tn = _pick(M, tm), _pick(N, tn)

    def body(a_ref, b_ref, o_ref):
        o_ref[...] = lax.dot_general(a_ref[...], b_ref[...], _TN, preferred_element_type=_F32)

    return pl.pallas_call(
        body, name=name, grid=(M // tm, N // tn),
        in_specs=[pl.BlockSpec((T, tm), lambda i, j: (0, i)), pl.BlockSpec((T, tn), lambda i, j: (0, j))],
        out_specs=pl.BlockSpec((tm, tn), lambda i, j: (i, j)),
        out_shape=jax.ShapeDtypeStruct((M, N), _F32),
        compiler_params=_cp("parallel", "parallel"),
    )(a, b)


def _ada_fwd(c_all, ada_w, name):
    L, D, Ws = ada_w.shape
    B = c_all.shape[0]

    def body(c_ref, w_ref, o_ref):
        o_ref[...] = _dot(_silu(c_ref[...]), w_ref[...])

    return pl.pallas_call(
        body, name=name, grid=(L,),
        in_specs=[_fspec((B, D)), pl.BlockSpec((None, D, Ws), lambda l: (l, 0, 0))],
        out_specs=pl.BlockSpec((None, B, Ws), lambda l: (l, 0, 0)),
        out_shape=jax.ShapeDtypeStruct((L, B, Ws), _F32), compiler_params=_cp("parallel"),
    )(c_all, ada_w)


def _ada_bwd(c_all_t, dmod, name):
    D, B = c_all_t.shape
    L, _, Ws = dmod.shape

    def body(c_ref, d_ref, o_ref):
        ct = _silu(c_ref[...])
        d = d_ref[...]
        acc = ct[:, 0:1] * d[0:1, :]
        for b in range(1, B):
            acc = acc + ct[:, b:b + 1] * d[b:b + 1, :]
        o_ref[...] = acc

    return pl.pallas_call(
        body, name=name, grid=(L,),
        in_specs=[_fspec((D, B)), pl.BlockSpec((None, B, Ws), lambda l: (l, 0, 0))],
        out_specs=pl.BlockSpec((None, D, Ws), lambda l: (l, 0, 0)),
        out_shape=jax.ShapeDtypeStruct((L, D, Ws), _F32), compiler_params=_cp("parallel"),
    )(c_all_t, dmod)


def _norm_mod(x, g, sc, sh, name, tb=512):
    T, D = x.shape
    tb = _pick(T, tb, 8)

    def body(x_ref, g_ref, sc_ref, sh_ref, h_ref):
        xv = x_ref[...]
        r = lax.rsqrt(jnp.mean(xv * xv, axis=1, keepdims=True) + _EPS)
        h_ref[...] = (xv * r * (g_ref[...] * (1.0 + sc_ref[...])) + sh_ref[...]).astype(h_ref.dtype)

    return pl.pallas_call(
        body, name=name, grid=(T // tb,),
        in_specs=[_rspec(tb, D), _fspec((1, D)), _fspec((1, D)), _fspec((1, D))],
        out_specs=_rspec(tb, D), out_shape=jax.ShapeDtypeStruct((T, D), _MMT), compiler_params=_cp("parallel"),
    )(x, g, sc, sh)


def _norm_mod_bwd(x, dh, dres, g, sc, name, tb=256):
    T, D = x.shape
    tb = _pick(T, tb, 8)

    def body(x_ref, dh_ref, dr_ref, g_ref, sc_ref, dx_ref, dgm_ref, dsh_ref):
        i = pl.program_id(0)
        xv, dh_ = x_ref[...], dh_ref[...]
        r = lax.rsqrt(jnp.mean(xv * xv, axis=1, keepdims=True) + _EPS)
        xn = xv * r
        dxn = dh_ * (g_ref[...] * (1.0 + sc_ref[...]))
        dx_ref[...] = dr_ref[...] + r * (dxn - xn * jnp.mean(dxn * xn, axis=1, keepdims=True))

        @pl.when(i == 0)
        def _():
            dgm_ref[...] = jnp.zeros_like(dgm_ref)
            dsh_ref[...] = jnp.zeros_like(dsh_ref)

        dgm_ref[...] += jnp.sum(dh_ * xn, axis=0, keepdims=True)
        dsh_ref[...] += jnp.sum(dh_, axis=0, keepdims=True)

    return pl.pallas_call(
        body, name=name, grid=(T // tb,),
        in_specs=[_rspec(tb, D), _rspec(tb, D), _rspec(tb, D), _fspec((1, D)), _fspec((1, D))],
        out_specs=[_rspec(tb, D), _fspec((1, D)), _fspec((1, D))],
        out_shape=[jax.ShapeDtypeStruct((T, D), _F32), jax.ShapeDtypeStruct((1, D), _F32),
                   jax.ShapeDtypeStruct((1, D), _F32)],
        compiler_params=_cp("arbitrary"),
    )(x, dh, dres, g, sc)


def _resid(x, p, gt, name, tb=512):
    T, D = x.shape
    tb = _pick(T, tb, 8)

    def body(x_ref, p_ref, gt_ref, o_ref):
        o_ref[...] = x_ref[...] + gt_ref[...] * p_ref[...]

    return pl.pallas_call(
        body, name=name, grid=(T // tb,), in_specs=[_rspec(tb, D), _rspec(tb, D), _fspec((1, D))],
        out_specs=_rspec(tb, D), out_shape=jax.ShapeDtypeStruct((T, D), _F32), compiler_params=_cp("parallel"),
    )(x, p, gt)


def _resid_bwd(dx, p, gt, name, tb=512):
    T, D = dx.shape
    tb = _pick(T, tb, 8)

    def body(dx_ref, p_ref, gt_ref, dp_ref, dgt_ref):
        i = pl.program_id(0)
        d = dx_ref[...]
        dp_ref[...] = (d * gt_ref[...]).astype(dp_ref.dtype)

        @pl.when(i == 0)
        def _():
            dgt_ref[...] = jnp.zeros_like(dgt_ref)

        dgt_ref[...] += jnp.sum(d * p_ref[...], axis=0, keepdims=True)

    return pl.pallas_call(
        body, name=name, grid=(T // tb,), in_specs=[_rspec(tb, D), _rspec(tb, D), _fspec((1, D))],
        out_specs=[_rspec(tb, D), _fspec((1, D))],
        out_shape=[jax.ShapeDtypeStruct((T, D), _MMT), jax.ShapeDtypeStruct((1, D), _F32)],
        compiler_params=_cp("arbitrary"),
    )(dx, p, gt)


def _gmlp_chunk(u_raw, v_raw, sw_ref, sbt, gv, G):
    u, v = _gelu(u_raw), _gelu(v_raw)
    ii = lax.broadcasted_iota(jnp.int32, (_AC, _AC), 0)
    jj = lax.broadcasted_iota(jnp.int32, (_AC, _AC), 1)
    out = []
    for gi in range(G):
        sl = slice(gi * _GD, (gi + 1) * _GD)
        vg = v[:, sl]
        r = lax.rsqrt(jnp.mean(vg * vg, axis=1, keepdims=True) + _EPS)
        vhat = vg * r
        W = jnp.where(jj <= ii, sw_ref[gi], 0.0)
        s = _dot(W, vhat * gv[:, sl]) + sbt[:, gi:gi + 1]
        out.append((u[:, sl], s, vhat, r, W))
    return out


def _gmlp_fwd(proj, sw, sbt, gv, li, D, name):
    T = proj.shape[0]
    G = D // _GD

    def body(u_ref, v_ref, sw_ref, sbt_ref, gv_ref, y_ref):
        parts = _gmlp_chunk(u_ref[...], v_ref[...], sw_ref, sbt_ref[...], gv_ref[...], G)
        for gi, (u, s, _, _, _) in enumerate(parts):
            y_ref[:, gi * _GD:(gi + 1) * _GD] = (u * s).astype(y_ref.dtype)

    return pl.pallas_call(
        body, name=name, grid=(T // _AC,),
        in_specs=[_rspec(_AC, D, 0), _rspec(_AC, D, 1), _lspec((G, _AC, _AC), li), _lspec((_AC, G), li),
                  _lspec((1, D), li)],
        out_specs=_rspec(_AC, D), out_shape=jax.ShapeDtypeStruct((T, D), _MMT), compiler_params=_cp("parallel"),
    )(proj, proj, sw, sbt, gv)


def _gmlp_bwd(proj, dy, sw, sbt, gv, li, D, name):
    T = proj.shape[0]
    G = D // _GD

    def body(u_ref, v_ref, dy_ref, sw_ref, sbt_ref, gv_ref, duv_ref, dsw_ref, dsa_ref, dgv_ref):
        i = pl.program_id(0)

        @pl.when(i == 0)
        def _():
            dsw_ref[...] = jnp.zeros_like(dsw_ref)
            dsa_ref[...] = jnp.zeros_like(dsa_ref)
            dgv_ref[...] = jnp.zeros_like(dgv_ref)

        u_raw, v_raw, dy_, gv_ = u_ref[...], v_ref[...], dy_ref[...], gv_ref[...]
        parts = _gmlp_chunk(u_raw, v_raw, sw_ref, sbt_ref[...], gv_, G)
        ii = lax.broadcasted_iota(jnp.int32, (_AC, _AC), 0)
        jj = lax.broadcasted_iota(jnp.int32, (_AC, _AC), 1)
        dgu, dgv = _dgelu(u_raw), _dgelu(v_raw)
        for gi, (u, s, vhat, r, W) in enumerate(parts):
            sl = slice(gi * _GD, (gi + 1) * _GD)
            dyg = dy_[:, sl]
            ds = dyg * u
            vn = vhat * gv_[:, sl]
            dsw_ref[gi] += jnp.where(jj <= ii, _dot(ds, vn, _NT), 0.0)
            dsa_ref[:, sl] += ds
            dvn = _dot(W, ds, _TN)
            dgv_ref[:, sl] += jnp.sum(dvn * vhat, axis=0, keepdims=True)
            dvh = dvn * gv_[:, sl]
            dvg = r * (dvh - vhat * jnp.mean(dvh * vhat, axis=1, keepdims=True))
            duv_ref[:, sl] = (dyg * s * dgu[:, sl]).astype(duv_ref.dtype)
            duv_ref[:, D + gi * _GD:D + (gi + 1) * _GD] = (dvg * dgv[:, sl]).astype(duv_ref.dtype)

    return pl.pallas_call(
        body, name=name, grid=(T // _AC,),
        in_specs=[_rspec(_AC, D, 0), _rspec(_AC, D, 1), _rspec(_AC, D), _lspec((G, _AC, _AC), li),
                  _lspec((_AC, G), li), _lspec((1, D), li)],
        out_specs=[_rspec(_AC, 2 * D), _fspec((G, _AC, _AC)), _fspec((_AC, D)), _fspec((1, D))],
        out_shape=[jax.ShapeDtypeStruct((T, 2 * D), _MMT), jax.ShapeDtypeStruct((G, _AC, _AC), _F32),
                   jax.ShapeDtypeStruct((_AC, D), _F32), jax.ShapeDtypeStruct((1, D), _F32)],
        compiler_params=_cp("arbitrary"),
    )(proj, proj, dy, sw, sbt, gv)


def _conv_taps(halo, cur, first):
    tb = cur.shape[0]
    full = jnp.concatenate([jnp.where(first, 0.0, halo), cur], axis=0)
    return [full[8:] if j == _KC - 1 else pltpu.roll(full, _KC - 1 - j, 0)[8:] for j in range(_KC)]


def _prev_spec(tb, w, cb):
    return pl.BlockSpec((8, w), lambda i: (jnp.maximum(i * (tb // 8) - 1, 0), cb))


def _l2_heads(x, H):
    outs, rs = [], []
    for h in range(H):
        xh = x[:, h * _GD:(h + 1) * _GD]
        r = lax.rsqrt(jnp.sum(xh * xh, axis=1, keepdims=True) + _EPS)
        outs.append(xh * r)
        rs.append(r)
    return outs, rs


def _gate_rows(ba, alog_row, dtb_row, H):
    lane = lax.broadcasted_iota(jnp.int32, ba.shape, 1)
    beta = _sigmoid(ba)
    g = -jnp.exp(alog_row) * _softplus(ba + dtb_row)
    return lane, beta, g


def _conv_fwd(proj, cw, alog_row, dtb_row, li, D, name, tb=256):
    T = proj.shape[0]
    H = D // _GD
    tb = _pick(T, tb, 8)
    bac = (8 * D) // _LANE

    def body(q_ref, k_ref, v_ref, qh_ref, kh_ref, vh_ref, ba_ref, cw_ref, al_ref, dtb_ref,
             qo_ref, ko_ref, vo_ref, bg_ref):
        first = pl.program_id(0) == 0
        cw_ = cw_ref[...]
        for idx, (cur, halo, out) in enumerate(((q_ref, qh_ref, qo_ref), (k_ref, kh_ref, ko_ref),
                                                 (v_ref, vh_ref, vo_ref))):
            taps = _conv_taps(halo[...], cur[...], first)
            w = cw_[:, idx * D:(idx + 1) * D]
            cv = taps[0] * w[0:1, :]
            for j in range(1, _KC):
                cv = cv + taps[j] * w[j:j + 1, :]
            act = _silu(cv)
            if idx < 2:
                outs, _ = _l2_heads(act, H)
                for h in range(H):
                    out[:, h * _GD:(h + 1) * _GD] = outs[h]
            else:
                out[...] = act
        lane, beta, g = _gate_rows(ba_ref[...], al_ref[...], dtb_ref[...], H)
        bg_ref[...] = jnp.where(lane < H, beta, jnp.where(lane < 2 * H, g, 0.0))

    return pl.pallas_call(
        body, name=name, grid=(T // tb,),
        in_specs=[_rspec(tb, D, 2), _rspec(tb, D, 3), _rspec(tb, D, 4),
                  _prev_spec(tb, D, 2), _prev_spec(tb, D, 3), _prev_spec(tb, D, 4),
                  _rspec(tb, _LANE, bac), _lspec((_KC, 3 * D), li), _lspec((1, _LANE), li), _lspec((1, _LANE), li)],
        out_specs=[_rspec(tb, D), _rspec(tb, D), _rspec(tb, D), _rspec(tb, _LANE)],
        out_shape=[jax.ShapeDtypeStruct((T, D), _F32)] * 3 + [jax.ShapeDtypeStruct((T, _LANE), _F32)],
        compiler_params=_cp("parallel"),
    )(proj, proj, proj, proj, proj, proj, proj, cw, alog_row, dtb_row)


def _conv_bwd1(proj, dqn, dkn, dvs, dbg, cw, alog_row, dtb_row, li, D, name, tb=256):
    T = proj.shape[0]
    H = D // _GD
    tb = _pick(T, tb, 8)
    bac = (8 * D) // _LANE

    def body(q_ref, k_ref, v_ref, qh_ref, kh_ref, vh_ref, ba_ref, dq_ref, dk_ref, dv_ref, dbg_ref,
             cw_ref, al_ref, dtb_ref, dc_ref, dba_ref, dcw_ref, dal_ref, ddt_ref):
        i = pl.program_id(0)
        first = i == 0

        @pl.when(first)
        def _():
            dcw_ref[...] = jnp.zeros_like(dcw_ref)
            dal_ref[...] = jnp.zeros_like(dal_ref)
            ddt_ref[...] = jnp.zeros_like(ddt_ref)

        cw_ = cw_ref[...]
        for idx, (cur, halo, dref) in enumerate(((q_ref, qh_ref, dq_ref), (k_ref, kh_ref, dk_ref),
                                                  (v_ref, vh_ref, dv_ref))):
            taps = _conv_taps(halo[...], cur[...], first)
            w = cw_[:, idx * D:(idx + 1) * D]
            cv = taps[0] * w[0:1, :]
            for j in range(1, _KC):
                cv = cv + taps[j] * w[j:j + 1, :]
            dact = dref[...]
            if idx < 2:
                outs, rs = _l2_heads(_silu(cv), H)
                pieces = []
                for h in range(H):
                    dy = dact[:, h * _GD:(h + 1) * _GD]
                    pieces.append(rs[h] * (dy - outs[h] * jnp.sum(dy * outs[h], axis=1, keepdims=True)))
                dact = jnp.concatenate(pieces, axis=1)
            dcv = dact * _dsilu(cv)
            dc_ref[:, idx * D:(idx + 1) * D] = dcv
            for j in range(_KC):
                dcw_ref[j:j + 1, idx * D:(idx + 1) * D] += jnp.sum(dcv * taps[j], axis=0, keepdims=True)

        ba = ba_ref[...]
        lane, beta, g = _gate_rows(ba, al_ref[...], dtb_ref[...], H)
        dbg_ = dbg_ref[...]
        is_b, is_a = lane < H, jnp.logical_and(lane >= H, lane < 2 * H)
        da = dbg_ * (-jnp.exp(al_ref[...])) * _sigmoid(ba + dtb_ref[...])
        dba_ref[...] = jnp.where(is_b, dbg_ * beta * (1.0 - beta), jnp.where(is_a, da, 0.0)).astype(dba_ref.dtype)
        dal_ref[...] += jnp.sum(jnp.where(is_a, dbg_ * g, 0.0), axis=0, keepdims=True)
        ddt_ref[...] += jnp.sum(jnp.where(is_a, da, 0.0), axis=0, keepdims=True)

    return pl.pallas_call(
        body, name=name, grid=(T // tb,),
        in_specs=[_rspec(tb, D, 2), _rspec(tb, D, 3), _rspec(tb, D, 4),
                  _prev_spec(tb, D, 2), _prev_spec(tb, D, 3), _prev_spec(tb, D, 4),
                  _rspec(tb, _LANE, bac), _rspec(tb, D), _rspec(tb, D), _rspec(tb, D), _rspec(tb, _LANE),
                  _lspec((_KC, 3 * D), li), _lspec((1, _LANE), li), _lspec((1, _LANE), li)],
        out_specs=[_rspec(tb, 3 * D), _rspec(tb, _LANE), _fspec((_KC, 3 * D)), _fspec((1, _LANE)), _fspec((1, _LANE))],
        out_shape=[jax.ShapeDtypeStruct((T, 3 * D), _F32), jax.ShapeDtypeStruct((T, _LANE), _MMT),
                   jax.ShapeDtypeStruct((_KC, 3 * D), _F32), jax.ShapeDtypeStruct((1, _LANE), _F32),
                   jax.ShapeDtypeStruct((1, _LANE), _F32)],
        compiler_params=_cp("arbitrary"),
    )(proj, proj, proj, proj, proj, proj, proj, dqn, dkn, dvs, dbg, cw, alog_row, dtb_row)


def _conv_bwd2(dc, cw, li, name, tb=256):
    T, W = dc.shape
    tb = _pick(T, tb, 8)
    nb8 = T // 8

    def body(dc_ref, nx_ref, cw_ref, o_ref):
        last = pl.program_id(0) == pl.num_programs(0) - 1
        full = jnp.concatenate([dc_ref[...], jnp.where(last, 0.0, nx_ref[...])], axis=0)
        w = cw_ref[...]
        acc = full[:tb] * w[_KC - 1:_KC, :]
        for j in range(_KC - 1):
            sh = _KC - 1 - j
            acc = acc + pltpu.roll(full, tb + 8 - sh, 0)[:tb] * w[j:j + 1, :]
        o_ref[...] = acc.astype(o_ref.dtype)

    return pl.pallas_call(
        body, name=name, grid=(T // tb,),
        in_specs=[_rspec(tb, W), pl.BlockSpec((8, W), lambda i: (jnp.minimum((i + 1) * (tb // 8), nb8 - 1), 0)),
                  _lspec((_KC, W), li)],
        out_specs=_rspec(tb, W), out_shape=jax.ShapeDtypeStruct((T, W), _MMT), compiler_params=_cp("parallel"),
    )(dc, dc, cw)


def _inv_unit_lower(a):
    C = a.shape[0]
    ii = lax.broadcasted_iota(jnp.int32, (C, C), 0)
    jj = lax.broadcasted_iota(jnp.int32, (C, C), 1)
    x = jnp.where(ii == jj, 1.0, 0.0) - a
    p = a
    n = 1
    while 2 * n < C:
        p = _dot_hi(p, p)
        x = x + _dot_hi(x, p)
        n *= 2
    return x


def _gdn_chunk(q, k, v, g_row, b_row):
    C = q.shape[0]
    ii = lax.broadcasted_iota(jnp.int32, (C, C), 0)
    jj = lax.broadcasted_iota(jnp.int32, (C, C), 1)
    low, strict, eye = jj <= ii, jj < ii, ii == jj
    g_col = jnp.sum(jnp.where(eye, g_row, 0.0), axis=1, keepdims=True)
    b_col = jnp.sum(jnp.where(eye, b_row, 0.0), axis=1, keepdims=True)
    gam_col = jnp.sum(jnp.where(low, g_row, 0.0), axis=1, keepdims=True)
    gam_row = jnp.sum(jnp.where(jj >= ii, g_col, 0.0), axis=0, keepdims=True)
    gam_last = jnp.sum(g_row, axis=1, keepdims=True)
    decay = jnp.where(low, jnp.exp(jnp.where(low, gam_col - gam_row, 0.0)), 0.0)
    eg = jnp.exp(gam_col)
    ekd = jnp.exp(gam_last - gam_col)
    qs = q * (_GD ** -0.5)
    kb = k * b_col
    kk = _dot(kb, k, _NT)
    qkraw = _dot(qs, k, _NT)
    return dict(low=low, strict=strict, eye=eye, ii=ii, jj=jj, b_col=b_col, decay=decay, eg=eg, ekd=ekd,
                gl=jnp.exp(gam_last), qs=qs, kb=kb, kk=kk, qkraw=qkraw,
                A=jnp.where(strict, kk * decay, 0.0), vb=v * b_col, kbg=kb * eg,
                qk=qkraw * decay, q_dec=qs * eg, k_dec=k * ekd)


def _gdn_fwd(qn, kn, vs, g_r, b_r, name):
    T, D = qn.shape
    H, N, C = D // _GD, T // _BC, _BC
    hb = min(_HB, H)

    def body(q_ref, k_ref, v_ref, g_ref, b_ref, o_ref, s_ref, t_ref, S):
        @pl.when(pl.program_id(1) == 0)
        def _():
            S[...] = jnp.zeros_like(S)

        for hh in range(hb):
            sl = slice(hh * _GD, (hh + 1) * _GD)
            cm = _gdn_chunk(q_ref[:, sl], k_ref[:, sl], v_ref[:, sl], g_ref[hh], b_ref[hh])
            tm = _inv_unit_lower(cm["A"])
            u, w = _dot(tm, cm["vb"]), _dot(tm, cm["kbg"])
            s0 = S[hh]
            s_ref[hh] = s0
            t_ref[hh] = tm
            v_new = u - _dot(w, s0)
            o_ref[:, sl] = _dot(cm["q_dec"], s0) + _dot(cm["qk"], v_new)
            S[hh] = s0 * cm["gl"] + _dot(cm["k_dec"], v_new, _TN)

    qspec = pl.BlockSpec((C, hb * _GD), lambda h, n: (n, h))
    gspec = pl.BlockSpec((hb, None, 1, C), lambda h, n: (h, n, 0, 0))
    return pl.pallas_call(
        body, name=name, grid=(H // hb, N),
        in_specs=[qspec, qspec, qspec, gspec, gspec],
        out_specs=[qspec, pl.BlockSpec((hb, None, _GD, _GD), lambda h, n: (h, n, 0, 0)),
                   pl.BlockSpec((hb, None, C, C), lambda h, n: (h, n, 0, 0))],
        out_shape=[jax.ShapeDtypeStruct((T, D), _F32), jax.ShapeDtypeStruct((H, N, _GD, _GD), _F32),
                   jax.ShapeDtypeStruct((H, N, C, C), _F32)],
        scratch_shapes=[pltpu.VMEM((hb, _GD, _GD), _F32)],
        compiler_params=_cp("arbitrary", "arbitrary"),
    )(qn, kn, vs, g_r, b_r)


def _gdn_bwd(qn, kn, vs, g_r, b_r, s_all, t_all, do, name):
    T, D = qn.shape
    H, N, C = D // _GD, T // _BC, _BC
    hb = min(_HB, H)

    def body(q_ref, k_ref, v_ref, g_ref, b_ref, s_ref, t_ref, do_ref, dq_ref, dk_ref, dv_ref, dg_ref, db_ref, dS):
        @pl.when(pl.program_id(1) == 0)
        def _():
            dS[...] = jnp.zeros_like(dS)

        for hh in range(hb):
            sl = slice(hh * _GD, (hh + 1) * _GD)
            q, k, v = q_ref[:, sl], k_ref[:, sl], v_ref[:, sl]
            cm = _gdn_chunk(q, k, v, g_ref[hh], b_ref[hh])
            low, strict, eye, ii, jj = cm["low"], cm["strict"], cm["eye"], cm["ii"], cm["jj"]
            decay, eg, ekd, gl, b_col = cm["decay"], cm["eg"], cm["ekd"], cm["gl"], cm["b_col"]
            tm, s0, do_, ds1 = t_ref[hh], s_ref[hh], do_ref[:, sl], dS[hh]
            u, w = _dot(tm, cm["vb"]), _dot(tm, cm["kbg"])
            v_new = u - _dot(w, s0)
            dv_new = _dot(cm["qk"], do_, _TN) + _dot(cm["k_dec"], ds1)
            dqk = jnp.where(low, _dot(do_, v_new, _NT), 0.0)
            dq_dec = _dot(do_, s0, _NT)
            dk_dec = _dot(v_new, ds1, _NT)
            dgl = jnp.sum(jnp.sum(ds1 * s0, axis=1, keepdims=True), axis=0, keepdims=True)
            dw = -_dot(dv_new, s0, _NT)
            dS[hh] = _dot(cm["q_dec"], do_, _TN) + gl * ds1 - _dot(w, dv_new, _TN)
            dvb, dkbg = _dot(tm, dv_new, _TN), _dot(tm, dw, _TN)
            dA = -jnp.where(strict, _dot(dvb, u, _NT) + _dot(dkbg, w, _NT), 0.0)
            dkk, dqkraw = dA * decay, dqk * decay
            E = (dA * cm["kk"] + dqk * cm["qkraw"]) * decay
            dkb = _dot(dkk, k) + dkbg * eg
            dqs = _dot(dqkraw, k) + dq_dec * eg
            dk_ref[:, sl] = _dot(dqkraw, cm["qs"], _TN) + _dot(dkk, cm["kb"], _TN) + dk_dec * ekd + dkb * b_col
            dv_ref[:, sl] = dvb * b_col
            dq_ref[:, sl] = dqs * (_GD ** -0.5)
            dbeta_col = jnp.sum(dvb * v + dkb * k, axis=1, keepdims=True)
            t_kd = jnp.sum(dk_dec * k, axis=1, keepdims=True) * ekd
            c1 = (jnp.sum(E, axis=1, keepdims=True) + jnp.sum(dkbg * cm["kb"], axis=1, keepdims=True) * eg
                  + jnp.sum(dq_dec * cm["qs"], axis=1, keepdims=True) * eg - t_kd)
            r1 = jnp.sum(E, axis=0, keepdims=True)
            dgam_last = jnp.sum(t_kd, axis=0, keepdims=True) + dgl * gl
            dgam_col = c1 - jnp.sum(jnp.where(eye, r1, 0.0), axis=1, keepdims=True)
            dg_ref[hh] = jnp.sum(jnp.where(ii >= jj, dgam_col, 0.0), axis=0, keepdims=True) + dgam_last
            db_ref[hh] = jnp.sum(jnp.where(eye, dbeta_col, 0.0), axis=0, keepdims=True)

    qspec = pl.BlockSpec((C, hb * _GD), lambda h, n: (N - 1 - n, h))
    gspec = pl.BlockSpec((hb, None, 1, C), lambda h, n: (h, N - 1 - n, 0, 0))
    return pl.pallas_call(
        body, name=name, grid=(H // hb, N),
        in_specs=[qspec, qspec, qspec, gspec, gspec,
                  pl.BlockSpec((hb, None, _GD, _GD), lambda h, n: (h, N - 1 - n, 0, 0)),
                  pl.BlockSpec((hb, None, C, C), lambda h, n: (h, N - 1 - n, 0, 0)), qspec],
        out_specs=[qspec, qspec, qspec, gspec, gspec],
        out_shape=[jax.ShapeDtypeStruct((T, D), _F32)] * 3 + [jax.ShapeDtypeStruct((H, N, 1, C), _F32)] * 2,
        scratch_shapes=[pltpu.VMEM((hb, _GD, _GD), _F32)],
        compiler_params=_cp("arbitrary", "arbitrary"),
    )(qn, kn, vs, g_r, b_r, s_all, t_all, do)


def _onorm_fwd(o, proj, go, li, D, name, tb=512):
    T = o.shape[0]
    H = D // _GD
    tb = _pick(T, tb, 8)

    def body(o_ref, z_ref, go_ref, y_ref):
        ov, zv, g = o_ref[...], z_ref[...], go_ref[...]
        for h in range(H):
            sl = slice(h * _GD, (h + 1) * _GD)
            oh = ov[:, sl]
            r = lax.rsqrt(jnp.mean(oh * oh, axis=1, keepdims=True) + _EPS)
            y_ref[:, sl] = (oh * r * g * _silu(zv[:, sl])).astype(y_ref.dtype)

    return pl.pallas_call(
        body, name=name, grid=(T // tb,), in_specs=[_rspec(tb, D), _rspec(tb, D, 5), _lspec((1, _GD), li)],
        out_specs=_rspec(tb, D), out_shape=jax.ShapeDtypeStruct((T, D), _MMT), compiler_params=_cp("parallel"),
    )(o, proj, go)


def _onorm_bwd(dy, o, proj, go, li, D, name, tb=256):
    T = o.shape[0]
    H = D // _GD
    tb = _pick(T, tb, 8)

    def body(dy_ref, o_ref, z_ref, go_ref, do_ref, dz_ref, dgo_ref):
        @pl.when(pl.program_id(0) == 0)
        def _():
            dgo_ref[...] = jnp.zeros_like(dgo_ref)

        dyv, ov, zv, g = dy_ref[...], o_ref[...], z_ref[...], go_ref[...]
        dgo = jnp.zeros((1, _GD), _F32)
        for h in range(H):
            sl = slice(h * _GD, (h + 1) * _GD)
            oh, zh, dyh = ov[:, sl], zv[:, sl], dyv[:, sl]
            r = lax.rsqrt(jnp.mean(oh * oh, axis=1, keepdims=True) + _EPS)
            on = oh * r
            sz = _silu(zh)
            dgo = dgo + jnp.sum(dyh * sz * on, axis=0, keepdims=True)
            don = dyh * sz * g
            do_ref[:, sl] = r * (don - on * jnp.mean(don * on, axis=1, keepdims=True))
            dz_ref[:, sl] = (dyh * on * g * _dsilu(zh)).astype(dz_ref.dtype)
        dgo_ref[...] += dgo

    return pl.pallas_call(
        body, name=name, grid=(T // tb,),
        in_specs=[_rspec(tb, D), _rspec(tb, D), _rspec(tb, D, 5), _lspec((1, _GD), li)],
        out_specs=[_rspec(tb, D), _rspec(tb, D), _fspec((1, _GD))],
        out_shape=[jax.ShapeDtypeStruct((T, D), _F32), jax.ShapeDtypeStruct((T, D), _MMT),
                   jax.ShapeDtypeStruct((1, _GD), _F32)],
        compiler_params=_cp("arbitrary"),
    )(dy, o, proj, go)


def _merge(pa, pb, proj, D, name, tb=512):
    T = pa.shape[0]
    tb = _pick(T, tb, 8)

    def body(pa_ref, pb_ref, ga_ref, gb_ref, m_ref):
        m_ref[...] = (_sigmoid(ga_ref[...]) * pa_ref[...] + _sigmoid(gb_ref[...]) * pb_ref[...]).astype(m_ref.dtype)

    return pl.pallas_call(
        body, name=name, grid=(T // tb,), in_specs=[_rspec(tb, D), _rspec(tb, D), _rspec(tb, D, 6), _rspec(tb, D, 7)],
        out_specs=_rspec(tb, D), out_shape=jax.ShapeDtypeStruct((T, D), _MMT), compiler_params=_cp("parallel"),
    )(pa, pb, proj, proj)


def _merge_bwd(dm, pa, pb, proj, D, name, tb=256):
    T = pa.shape[0]
    tb = _pick(T, tb, 8)

    def body(dm_ref, pa_ref, pb_ref, ga_ref, gb_ref, dpa_ref, dpb_ref, dg_ref):
        d = dm_ref[...]
        sa, sb = _sigmoid(ga_ref[...]), _sigmoid(gb_ref[...])
        dpa_ref[...] = (d * sa).astype(dpa_ref.dtype)
        dpb_ref[...] = (d * sb).astype(dpb_ref.dtype)
        dg_ref[:, :D] = (d * pa_ref[...] * sa * (1.0 - sa)).astype(dg_ref.dtype)
        dg_ref[:, D:] = (d * pb_ref[...] * sb * (1.0 - sb)).astype(dg_ref.dtype)

    return pl.pallas_call(
        body, name=name, grid=(T // tb,),
        in_specs=[_rspec(tb, D), _rspec(tb, D), _rspec(tb, D), _rspec(tb, D, 6), _rspec(tb, D, 7)],
        out_specs=[_rspec(tb, D), _rspec(tb, D), _rspec(tb, 2 * D)],
        out_shape=[jax.ShapeDtypeStruct((T, D), _MMT)] * 2 + [jax.ShapeDtypeStruct((T, 2 * D), _MMT)],
        compiler_params=_cp("parallel"),
    )(dm, pa, pb, proj, proj)


def _swiglu(gu, name, tb=256):
    T, F2 = gu.shape
    F = F2 // 2
    tb = _pick(T, tb, 8)

    def body(g_ref, u_ref, a_ref):
        a_ref[...] = (_silu(g_ref[...]) * u_ref[...]).astype(a_ref.dtype)

    return pl.pallas_call(
        body, name=name, grid=(T // tb,), in_specs=[_rspec(tb, F, 0), _rspec(tb, F, 1)],
        out_specs=_rspec(tb, F), out_shape=jax.ShapeDtypeStruct((T, F), _MMT), compiler_params=_cp("parallel"),
    )(gu, gu)


def _swiglu_bwd(da, gu, name, tb=256):
    T, F2 = gu.shape
    F = F2 // 2
    tb = _pick(T, tb, 8)

    def body(da_ref, g_ref, u_ref, o_ref):
        d, g = da_ref[...], g_ref[...]
        o_ref[:, :F] = (d * u_ref[...] * _dsilu(g)).astype(o_ref.dtype)
        o_ref[:, F:] = (d * _silu(g)).astype(o_ref.dtype)

    return pl.pallas_call(
        body, name=name, grid=(T // tb,), in_specs=[_rspec(tb, F), _rspec(tb, F, 0), _rspec(tb, F, 1)],
        out_specs=_rspec(tb, F2), out_shape=jax.ShapeDtypeStruct((T, F2), _MMT), compiler_params=_cp("parallel"),
    )(da, gu, gu)


def _loss_head(x, tgt, fg, name, tb=256):
    T, D = x.shape
    tb = _pick(T, tb, 8)

    def body(x_ref, t_ref, fg_ref, loss_ref, dx_ref, dfg_ref):
        @pl.when(pl.program_id(0) == 0)
        def _():
            loss_ref[...] = jnp.zeros_like(loss_ref)
            dfg_ref[...] = jnp.zeros_like(dfg_ref)

        xv, fg_ = x_ref[...], fg_ref[...]
        r = lax.rsqrt(jnp.mean(xv * xv, axis=1, keepdims=True) + _EPS)
        xn = xv * r
        e = xn * fg_ - t_ref[...]
        loss_ref[...] += (0.5 / D) * jnp.sum(jnp.sum(e * e, axis=1, keepdims=True), axis=0, keepdims=True)
        dy = e * (1.0 / D)
        dfg_ref[...] += jnp.sum(dy * xn, axis=0, keepdims=True)
        dxn = dy * fg_
        dx_ref[...] = r * (dxn - xn * jnp.mean(dxn * xn, axis=1, keepdims=True))

    return pl.pallas_call(
        body, name=name, grid=(T // tb,), in_specs=[_rspec(tb, D), _rspec(tb, D), _fspec((1, D))],
        out_specs=[_fspec((1, 1)), _rspec(tb, D), _fspec((1, D))],
        out_shape=[jax.ShapeDtypeStruct((1, 1), _F32), jax.ShapeDtypeStruct((T, D), _F32),
                   jax.ShapeDtypeStruct((1, D), _F32)],
        compiler_params=_cp("arbitrary"),
    )(x, tgt, fg)


def _row_tile(R, W, budget=1 << 20, unit=8):
    if R * W * 4 <= budget or R % unit:
        return R
    best = unit
    for t in range(unit, R + 1, unit):
        if R % t == 0 and t * W * 4 <= budget:
            best = t
    return best


def _add_own_half(send, got, half, name):
    P, Rp, W = send.shape
    Rh = Rp // 2
    tb = _row_tile(Rh, W, 1 << 21, 16)

    def body(h_ref, a_ref, b_ref, o_ref):
        o_ref[...] = (a_ref[...].astype(_F32) + b_ref[...].astype(_F32)).astype(o_ref.dtype)

    return pl.pallas_call(
        body, name=name,
        grid_spec=pltpu.PrefetchScalarGridSpec(
            num_scalar_prefetch=1, grid=(P, Rh // tb),
            in_specs=[pl.BlockSpec((None, None, tb, W), lambda k, i, h: (k, h[0], i, 0)),
                      pl.BlockSpec((None, tb, W), lambda k, i, h: (k, i, 0))],
            out_specs=pl.BlockSpec((None, tb, W), lambda k, i, h: (k, i, 0))),
        out_shape=jax.ShapeDtypeStruct((P, Rh, W), send.dtype), compiler_params=_cp("parallel", "parallel"),
    )(half, send.reshape(P, 2, Rh, W), got)


def _sum_slots(st, name):
    P, R, W = st.shape
    tb = _row_tile(R, W, 1 << 20, 16)

    def body(s_ref, o_ref):
        acc = s_ref[0].astype(_F32)
        for p in range(1, P):
            acc = acc + s_ref[p].astype(_F32)
        o_ref[...] = acc

    return pl.pallas_call(
        body, name=name, grid=(R // tb,), in_specs=[pl.BlockSpec((P, tb, W), lambda i: (0, i, 0))],
        out_specs=_rspec(tb, W), out_shape=jax.ShapeDtypeStruct((R, W), _F32), compiler_params=_cp("parallel"),
    )(st)


def _adamw(w, gst, m, v, name):
    R, W = w.shape
    P = gst.shape[0]
    tb = _row_tile(R, W, 1 << 19)
    c1, c2 = 1.0 - _B1 ** _STEP, 1.0 - _B2 ** _STEP

    def body(w_ref, g_ref, m_ref, v_ref, go_ref, d_ref, mo_ref, vo_ref):
        g = g_ref[0]
        for p in range(1, P):
            g = g + g_ref[p]
        mn = _B1 * m_ref[...] + (1.0 - _B1) * g
        vn = _B2 * v_ref[...] + (1.0 - _B2) * (g * g)
        go_ref[...] = g
        mo_ref[...] = mn
        vo_ref[...] = vn
        d_ref[...] = -_LR * ((mn / c1) / (jnp.sqrt(vn / c2) + _AEPS) + _WD * w_ref[...])

    spec = _rspec(tb, W)
    return pl.pallas_call(
        body, name=name, grid=(R // tb,),
        in_specs=[spec, pl.BlockSpec((P, tb, W), lambda i: (0, i, 0)), spec, spec],
        out_specs=[spec] * 4, out_shape=[jax.ShapeDtypeStruct((R, W), _F32)] * 4, compiler_params=_cp("parallel"),
    )(w, gst, m, v)


def _as2d(a):
    if a.ndim == 1:
        return a.reshape(1, -1)
    return a.reshape(-1, a.shape[-1])


def kernel(x, c, ada_w, ada_b, norm1_g, w_in, conv_w, spatial_w, spatial_b, v_norm_g, a_log, dt_bias, o_norm_g, w_branch_a, w_branch_b, w_out, norm2_g, w_ffn_in, w_ffn_out, final_g, loss_target, m_ada_w, m_ada_b, m_norm1_g, m_w_in, m_conv_w, m_spatial_w, m_spatial_b, m_v_norm_g, m_a_log, m_dt_bias, m_o_norm_g, m_w_branch_a, m_w_branch_b, m_w_out, m_norm2_g, m_w_ffn_in, m_w_ffn_out, m_final_g, v_ada_w, v_ada_b, v_norm1_g, v_w_in, v_conv_w, v_spatial_w, v_spatial_b, v_v_norm_g, v_a_log, v_dt_bias, v_o_norm_g, v_w_branch_a, v_w_branch_b, v_w_out, v_norm2_g, v_w_ffn_in, v_w_ffn_out, v_final_g):
    xb, tgt = x[0], loss_target[0]
    T, D = xb.shape
    L, H, G = ada_w.shape[0], a_log.shape[1], spatial_w.shape[1]
    F = 4 * w_ffn_out.shape[1]
    N = T // _BC
    Ws = ada_w.shape[2]
    Wc = w_in.shape[2]
    PW = 8 * D + _LANE
    ix, iy, ic = lax.axis_index("x"), lax.axis_index("y"), lax.axis_index("c")
    me = 4 * ix + 2 * iy + ic

    c_all = _xchg([c], _ALL8, _slot_all8, 8, True, "gather_c")[0].reshape(8, D)
    modp = _ada_fwd(c_all, ada_w, "ada_fwd")
    n_mod, n_cw = L * 8 * Ws, L * _KC * conv_w.shape[2]
    pad = (-(n_mod + n_cw)) % _LANE
    pay = jnp.concatenate([modp.reshape(-1), conv_w.reshape(-1), jnp.zeros((pad,), _F32)]).reshape(-1, _LANE)
    pay_all = _xchg([pay], _ALL8, _slot_all8, 8, True, "gather_mod")[0].reshape(8, -1)
    mod_full = jnp.concatenate([pay_all[2 * k, :n_mod].reshape(L, 8, Ws) for k in range(4)], axis=-1)
    cw_full = jnp.concatenate([pay_all[2 * k, n_mod:n_mod + n_cw].reshape(L, _KC, -1) for k in range(4)], axis=-1)
    mod = lax.dynamic_index_in_dim(mod_full, me, axis=1, keepdims=False) + ada_b
    mods = [[mod[l, j * D:(j + 1) * D].reshape(1, D) for j in range(6)] for l in range(L)]

    big = [w_in, w_branch_a, w_branch_b, w_out, w_ffn_in, w_ffn_out]
    transposed = [True, False, False, False, True, False]

    def to_rows(a, tr):
        return (jnp.transpose(a, (0, 2, 1)) if tr else a).reshape(-1, D)

    def from_rows(r, tr, per):
        a = r.reshape(L, per, D)
        return jnp.transpose(a, (0, 2, 1)) if tr else a

    pers = [Wc, D // 4, D // 4, D // 4, 2 * F // 4, F // 4]
    roff = [0]
    for p in pers:
        roff.append(roff[-1] + L * p)
    Rp = -(-roff[-1] // (32 * _NCH)) * (32 * _NCH)
    rpad = Rp - roff[-1]

    shard = jnp.concatenate([to_rows(w, tr).astype(_MMT) for w, tr in zip(big, transposed)]
                            + [jnp.zeros((rpad, D), _MMT)], axis=0)
    gw = _gather_halves(shard, "gather_w")

    def full_rows(i):
        g = gw[:, roff[i]:roff[i + 1]].reshape(4, L, pers[i], D)
        return jnp.transpose(g, (1, 0, 2, 3)).reshape(L, 4 * pers[i], D)

    wt_in_f = full_rows(0)
    wt_in_p = jnp.concatenate([wt_in_f[:, :6 * D], wt_in_f[:, 6 * D + 2 * H:], wt_in_f[:, 6 * D:6 * D + 2 * H],
                               jnp.zeros((L, _LANE - 2 * H, D), _MMT)], axis=1)
    w_a, w_b, w_o, wt_fi, w_fo = full_rows(1), full_rows(2), full_rows(3), full_rows(4), full_rows(5)

    sbt = jnp.transpose(spatial_b, (0, 2, 1))
    gv3, go3 = v_norm_g.reshape(L, 1, D), o_norm_g.reshape(L, 1, _GD)
    zpad = jnp.zeros((L, _LANE - 2 * H), _F32)
    alog_row = jnp.concatenate([jnp.zeros((L, H), _F32), a_log, zpad], axis=1).reshape(L, 1, _LANE)
    dtb_row = jnp.concatenate([jnp.zeros((L, H), _F32), dt_bias, zpad], axis=1).reshape(L, 1, _LANE)

    def rows_of(tok):
        return jnp.transpose(tok.reshape(N, _BC, H), (2, 0, 1)).reshape(H, N, 1, _BC)

    def toks_of(rows):
        return jnp.transpose(rows.reshape(H, N, _BC), (1, 2, 0)).reshape(T, H)

    saved = []
    xc = xb
    for l in range(L):
        sh1, sc1, gt1, sh2, sc2, gt2 = mods[l]
        g1, g2 = norm1_g[l].reshape(1, D), norm2_g[l].reshape(1, D)
        h = _norm_mod(xc, g1, sc1, sh1, f"norm1_{l}")
        proj = _mm(h, wt_in_p, f"proj_{l}", li=l, trans_b=True, tn=640)
        ya = _gmlp_fwd(proj, spatial_w, sbt, gv3, l, D, f"gmlp_{l}")
        qn, kn, vs, bg = _conv_fwd(proj, cw_full, alog_row, dtb_row, l, D, f"conv_{l}")
        g_r, b_r = rows_of(bg[:, H:2 * H]), rows_of(bg[:, :H])
        o, s_all, t_all = _gdn_fwd(qn, kn, vs, g_r, b_r, f"gdn_{l}")
        yb = _onorm_fwd(o, proj, go3, l, D, f"onorm_{l}")
        pa = _mm(ya, w_a, f"bra_{l}", li=l)
        pb = _mm(yb, w_b, f"brb_{l}", li=l)
        mg = _merge(pa, pb, proj, D, f"merge_{l}")
        p1 = _mm(mg, w_o, f"wout_{l}", li=l)
        x1 = _resid(xc, p1, gt1, f"res1_{l}")
        h2 = _norm_mod(x1, g2, sc2, sh2, f"norm2_{l}")
        gu = _mm(h2, wt_fi, f"ffin_{l}", li=l, trans_b=True)
        act = _swiglu(gu, f"swiglu_{l}")
        p2 = _mm(act, w_fo, f"ffout_{l}", li=l)
        x2 = _resid(x1, p2, gt2, f"res2_{l}")
        saved.append(dict(x=xc, h=h, proj=proj, ya=ya, yb=yb, qn=qn, kn=kn, vs=vs, g_r=g_r, b_r=b_r, o=o,
                          s_all=s_all, t_all=t_all, pa=pa, pb=pb, mg=mg, p1=p1, x1=x1, h2=h2, gu=gu, act=act, p2=p2))
        xc = x2

    loss11, dx, dfg = _loss_head(xc, tgt, final_g.reshape(1, D), "loss_head")
    loss = lax.psum(loss11[0, 0], ("x", "y", "c"))

    gbig = {k: [None] * L for k in ("w_in", "w_a", "w_b", "w_o", "w_fi", "w_fo")}
    small = {k: [None] * L for k in ("dmod", "n1", "n2", "sw", "sb", "gv", "cw", "al", "dt", "go")}
    for l in reversed(range(L)):
        sv = saved[l]
        sh1, sc1, gt1, sh2, sc2, gt2 = mods[l]
        g1, g2 = norm1_g[l].reshape(1, D), norm2_g[l].reshape(1, D)
        proj = sv["proj"]
        dp2, dgt2 = _resid_bwd(dx, sv["p2"], gt2, f"res2b_{l}")
        da = _mm(dp2, w_fo, f"ffoutb_{l}", li=l, trans_b=True)
        gbig["w_fo"][l] = _mm_tn(sv["act"], dp2, f"ffoutw_{l}")
        dgu = _swiglu_bwd(da, sv["gu"], f"swiglub_{l}")
        dh2 = _mm(dgu, wt_fi, f"ffinb_{l}", li=l)
        gbig["w_fi"][l] = _mm_tn(dgu, sv["h2"], f"ffinw_{l}")
        dx1, dgm2, dsh2 = _norm_mod_bwd(sv["x1"], dh2, dx, g2, sc2, f"norm2b_{l}")
        dp1, dgt1 = _resid_bwd(dx1, sv["p1"], gt1, f"res1b_{l}")
        dmg = _mm(dp1, w_o, f"woutb_{l}", li=l, trans_b=True)
        gbig["w_o"][l] = _mm_tn(sv["mg"], dp1, f"woutw_{l}")
        dpa, dpb, dgates = _merge_bwd(dmg, sv["pa"], sv["pb"], proj, D, f"mergeb_{l}")
        dya = _mm(dpa, w_a, f"brab_{l}", li=l, trans_b=True)
        gbig["w_a"][l] = _mm_tn(sv["ya"], dpa, f"braw_{l}")
        dyb = _mm(dpb, w_b, f"brbb_{l}", li=l, trans_b=True)
        gbig["w_b"][l] = _mm_tn(sv["yb"], dpb, f"brbw_{l}")
        duv, dsw, dsa, dgv = _gmlp_bwd(proj, dya, spatial_w, sbt, gv3, l, D, f"gmlpb_{l}")
        do, dz, dgo = _onorm_bwd(dyb, sv["o"], proj, go3, l, D, f"onormb_{l}")
        dqn, dkn, dvs, dg_r, db_r = _gdn_bwd(sv["qn"], sv["kn"], sv["vs"], sv["g_r"], sv["b_r"], sv["s_all"],
                                             sv["t_all"], do, f"gdnb_{l}")
        dbg = jnp.concatenate([toks_of(db_r), toks_of(dg_r), jnp.zeros((T, _LANE - 2 * H), _F32)], axis=1)
        dc, dba, dcw, dal, ddt = _conv_bwd1(proj, dqn, dkn, dvs, dbg, cw_full, alog_row, dtb_row, l, D, f"convb_{l}")
        dqkv = _conv_bwd2(dc, cw_full, l, f"convx_{l}")
        dproj = jnp.concatenate([duv, dqkv, dz, dgates, dba], axis=1)
        dh = _mm(dproj, wt_in_p, f"projb_{l}", li=l, tm=512, tk=1664)
        gbig["w_in"][l] = _mm_tn(dproj, sv["h"], f"projw_{l}", tm=640)
        dx, dgm1, dsh1 = _norm_mod_bwd(sv["x"], dh, dx1, g1, sc1, f"norm1b_{l}")
        small["dmod"][l] = jnp.concatenate([dsh1, dgm1 * g1, dgt1, dsh2, dgm2 * g2, dgt2], axis=1)
        small["n1"][l], small["n2"][l] = dgm1 * (1.0 + sc1), dgm2 * (1.0 + sc2)
        small["sw"][l], small["gv"][l], small["cw"][l], small["go"][l] = dsw, dgv, dcw, dgo
        small["sb"][l] = jnp.transpose(dsa.reshape(_AC, G, _GD).sum(axis=-1))
        small["al"][l], small["dt"][l] = dal[:, H:2 * H], ddt[:, H:2 * H]
    grad_x = dx.reshape(1, T, D)

    names_small = ["dmod", "n1", "n2", "sw", "sb", "gv", "cw", "al", "dt", "go"]
    flat = [jnp.stack(small[k]).reshape(-1) for k in names_small] + [dfg.reshape(-1)]
    sizes = [f.shape[0] for f in flat]
    tot = sum(sizes)
    pad = (-tot) % 1024
    pay = jnp.concatenate(flat + [jnp.zeros((pad,), _F32)]).reshape(-1, 1024)
    sm_all = _xchg([pay], _ALL8, _slot_all8, 8, True, "gather_small")[0].reshape(8, -1)
    offs = [0]
    for s in sizes:
        offs.append(offs[-1] + s)
    part = {k: sm_all[:, offs[i]:offs[i + 1]] for i, k in enumerate(names_small + ["fg"])}
    dmod_all = part["dmod"].reshape(8, L, 6 * D)

    outs = {}

    def update(nm, w, gst, m, v):
        shp = w.shape
        w2 = _as2d(w)
        g, d, mn, vn = _adamw(w2, gst.reshape((gst.shape[0],) + w2.shape), _as2d(m), _as2d(v), f"adamw_{nm}")
        outs[nm] = (g.reshape(shp), d.reshape(shp), mn.reshape(shp), vn.reshape(shp))

    chip = 2 * ix + iy
    dmod_t = jnp.transpose(dmod_all, (1, 0, 2))
    dmod_mine = lax.dynamic_slice_in_dim(dmod_t, chip * Ws, Ws, axis=2)
    g_ada_w = _ada_bwd(jnp.transpose(c_all), dmod_mine, "ada_bwd")
    update("ada_w", ada_w, g_ada_w[None], m_ada_w, v_ada_w)
    update("ada_b", ada_b, dmod_all, m_ada_b, v_ada_b)
    update("norm1_g", norm1_g, part["n1"], m_norm1_g, v_norm1_g)
    update("norm2_g", norm2_g, part["n2"], m_norm2_g, v_norm2_g)
    update("spatial_w", spatial_w, part["sw"], m_spatial_w, v_spatial_w)
    update("spatial_b", spatial_b, part["sb"], m_spatial_b, v_spatial_b)
    update("v_norm_g", v_norm_g, part["gv"], m_v_norm_g, v_v_norm_g)
    update("a_log", a_log, part["al"], m_a_log, v_a_log)
    update("dt_bias", dt_bias, part["dt"], m_dt_bias, v_dt_bias)
    update("o_norm_g", o_norm_g, part["go"], m_o_norm_g, v_o_norm_g)
    update("final_g", final_g, part["fg"], m_final_g, v_final_g)
    cw_cols = conv_w.shape[2]
    dcw_all = part["cw"].reshape(8, L, _KC, 4 * cw_cols)
    update("conv_w", conv_w, lax.dynamic_slice_in_dim(dcw_all, chip * cw_cols, cw_cols, axis=3), m_conv_w, v_conv_w)

    dwt_in_p = jnp.stack(gbig["w_in"])
    dwt_in_f = jnp.concatenate([dwt_in_p[:, :6 * D], dwt_in_p[:, 8 * D:8 * D + 2 * H], dwt_in_p[:, 6 * D:8 * D]], axis=1)
    gfull = [dwt_in_f] + [jnp.stack(gbig[k]) for k in ("w_a", "w_b", "w_o", "w_fi", "w_fo")]

    def chip_rows(g, per):
        return jnp.transpose(g.reshape(L, 4, per, D), (1, 0, 2, 3)).reshape(4, L * per, D)

    send = jnp.concatenate([chip_rows(g, per).astype(_MMT) for g, per in zip(gfull, pers)]
                           + [jnp.zeros((4, rpad, D), _MMT)], axis=1)
    got = _send_half_to_sibling(send, "reduce_cores")
    chipsum = _add_own_half(send, got, ic.astype(jnp.int32).reshape(1), "add_cores")
    parts = _scatter_to_chips(chipsum, "reduce_chips")
    mine = _sum_slots(parts, "add_chips")
    other = _swap_with_sibling(mine, "swap_cores")
    first = ic == 0
    gsum = jnp.concatenate([jnp.where(first, mine, other), jnp.where(first, other, mine)], axis=0)
    big_names = ["w_in", "w_branch_a", "w_branch_b", "w_out", "w_ffn_in", "w_ffn_out"]
    big_m = [m_w_in, m_w_branch_a, m_w_branch_b, m_w_out, m_w_ffn_in, m_w_ffn_out]
    big_v = [v_w_in, v_w_branch_a, v_w_branch_b, v_w_out, v_w_ffn_in, v_w_ffn_out]
    for i, (nm, w, m, v) in enumerate(zip(big_names, big, big_m, big_v)):
        update(nm, w, from_rows(gsum[roff[i]:roff[i + 1]], transposed[i], pers[i])[None], m, v)

    order = ["ada_w", "ada_b", "norm1_g", "w_in", "conv_w", "spatial_w", "spatial_b", "v_norm_g", "a_log", "dt_bias",
             "o_norm_g", "w_branch_a", "w_branch_b", "w_out", "norm2_g", "w_ffn_in", "w_ffn_out", "final_g"]
    return (loss, grad_x, *[outs[n][0] for n in order], *[outs[n][1] for n in order],
            *[outs[n][2] for n in order], *[outs[n][3] for n in order])
```

```python
import functools
import math

import jax
import jax.numpy as jnp
from jax import lax
from jax.experimental import pallas as pl
from jax.experimental.pallas import tpu as pltpu

_F32 = jnp.float32
_BF = jnp.bfloat16
_MMT = jnp.bfloat16
_EPS = 1e-6
_GD = 128
_AC = 128
_BC = 64
_KC = 4
_HB = 4
_NCH = 8
_LANE = 128
_VMEM_LIMIT = 56 * 1024 * 1024

_LR, _B1, _B2, _AEPS, _WD, _STEP = 0.001, 0.9, 0.999, 1e-08, 0.01, 10

_NN = (((1,), (0,)), ((), ()))
_NT = (((1,), (1,)), ((), ()))
_TN = (((0,), (0,)), ((), ()))

_MESH = pl.DeviceIdType.MESH


def _cp(*sem):
    return pltpu.CompilerParams(dimension_semantics=tuple(sem), vmem_limit_bytes=_VMEM_LIMIT)


def _dot(a, b, dn=_NN):
    return lax.dot_general(a.astype(_MMT), b.astype(_MMT), dn, preferred_element_type=_F32)


def _pick(n, target, unit=_LANE):
    if n <= target:
        return n
    best = None
    for t in range(unit, target + 1, unit):
        if n % t == 0:
            best = t
    assert best is not None, (n, target)
    return best


def _sigmoid(x):
    return 1.0 / (1.0 + jnp.exp(-x))


def _silu(x):
    return x * _sigmoid(x)


def _dsilu(x):
    s = _sigmoid(x)
    return s * (1.0 + x * (1.0 - s))


_GK = math.sqrt(2.0 / math.pi)


def _gelu(x):
    return 0.5 * x * (1.0 + jnp.tanh(_GK * (x + 0.044715 * x * x * x)))


def _dgelu(x):
    t = jnp.tanh(_GK * (x + 0.044715 * x * x * x))
    return 0.5 * (1.0 + t) + 0.5 * x * (1.0 - t * t) * _GK * (1.0 + 3.0 * 0.044715 * x * x)


def _softplus(x):
    return jnp.maximum(x, 0.0) + jnp.log(1.0 + jnp.exp(-jnp.abs(x)))


def _rspec(tb, w, cb=0):
    return pl.BlockSpec((tb, w), lambda i: (i, cb))


def _fspec(shape):
    nd = len(shape)
    return pl.BlockSpec(tuple(shape), lambda i: (0,) * nd)


def _lspec(tail, li):
    nd = len(tail)
    return pl.BlockSpec((None,) + tuple(tail), lambda i: (li,) + (0,) * nd)


_ALL8 = [(kx, ky, kc) for kx in (0, 1) for ky in (0, 1) for kc in (0, 1) if (kx, ky, kc) != (0, 0, 0)]


def _slot_all8(x, y, c):
    return 4 * x + 2 * y + c


def _xchg(srcs, flips, slot, nslots, gather, name):
    na, npeer = len(srcs), len(flips)

    def body(*refs):
        src_refs, out_refs = refs[:na], refs[na:2 * na]
        send_sems, recv_sems, local_sems = refs[2 * na:]
        x, y, c = lax.axis_index("x"), lax.axis_index("y"), lax.axis_index("c")
        mine = slot(x, y, c)
        copies = []
        for a in range(na):
            own = src_refs[a] if gather else src_refs[a].at[mine]
            loc = pltpu.make_async_copy(own, out_refs[a].at[mine], local_sems.at[a])
            loc.start()
            copies.append(loc)
            for p, (kx, ky, kc) in enumerate(flips):
                px = 1 - x if kx else x
                py = 1 - y if ky else y
                pc = 1 - c if kc else c
                src = src_refs[a] if gather else src_refs[a].at[slot(px, py, pc)]
                cp = pltpu.make_async_remote_copy(
                    src_ref=src, dst_ref=out_refs[a].at[mine],
                    send_sem=send_sems.at[a * npeer + p], recv_sem=recv_sems.at[a * npeer + p],
                    device_id=(px, py, pc), device_id_type=_MESH)
                cp.start()
                copies.append(cp)
        for cp in copies:
            cp.wait()

    hbm = pl.BlockSpec(memory_space=pl.ANY)
    outs = pl.pallas_call(
        body, name=name,
        out_shape=[jax.ShapeDtypeStruct((nslots,) + (s.shape if gather else s.shape[1:]), s.dtype) for s in srcs],
        in_specs=[hbm] * na, out_specs=[hbm] * na,
        scratch_shapes=[pltpu.SemaphoreType.DMA((na * npeer,)), pltpu.SemaphoreType.DMA((na * npeer,)),
                        pltpu.SemaphoreType.DMA((na,))],
    )(*srcs)
    return list(outs)


def _rcopy(src, dst, ssem, rsem, dev):
    return pltpu.make_async_remote_copy(src_ref=src, dst_ref=dst, send_sem=ssem, recv_sem=rsem,
                                        device_id=dev, device_id_type=_MESH)


def _other_chips(x, y):
    return [(1 - x, y), (x, 1 - y), (1 - x, 1 - y)]


_HBM = pl.BlockSpec(memory_space=pl.ANY)


def _gather_halves(shard, name):
    Rp, W = shard.shape
    Rh = Rp // 2
    rc = Rh // _NCH

    def body(s_ref, o_ref, isend, irecv, dsend, drecv, lsem):
        x, y, c = lax.axis_index("x"), lax.axis_index("y"), lax.axis_index("c")
        chip, sib = 2 * x + y, (x, y, 1 - c)
        peers = _other_chips(x, y)

        def rows(half, q):
            return pl.ds(half * Rh + q * rc, rc)

        loc = pltpu.make_async_copy(s_ref, o_ref.at[chip], lsem)
        loc.start()
        started = []
        for j, (px, py) in enumerate(peers):
            for q in range(_NCH):
                cp = _rcopy(s_ref.at[rows(c, q)], o_ref.at[chip, rows(c, q)], isend.at[j * _NCH + q],
                            irecv.at[j * _NCH + q], (px, py, c))
                cp.start()
                started.append(cp)
        for j, (px, py) in enumerate(peers):
            pchip = 2 * px + py
            for q in range(_NCH):
                blk = o_ref.at[pchip, rows(c, q)]
                _rcopy(blk, blk, isend.at[j * _NCH + q], irecv.at[j * _NCH + q], (px, py, c)).wait_recv()
                fw = _rcopy(blk, blk, dsend.at[j * _NCH + q], drecv.at[j * _NCH + q], sib)
                fw.start()
                started.append(fw)
        for j, (px, py) in enumerate(peers):
            pchip = 2 * px + py
            for q in range(_NCH):
                blk = o_ref.at[pchip, rows(1 - c, q)]
                _rcopy(blk, blk, dsend.at[j * _NCH + q], drecv.at[j * _NCH + q], sib).wait_recv()
        for cp in started:
            cp.wait_send()
        loc.wait()

    n = 3 * _NCH
    return pl.pallas_call(
        body, name=name, out_shape=jax.ShapeDtypeStruct((4, Rp, W), shard.dtype), in_specs=[_HBM], out_specs=_HBM,
        scratch_shapes=[pltpu.SemaphoreType.DMA((n,))] * 4 + [pltpu.SemaphoreType.DMA],
    )(shard)


def _send_half_to_sibling(send, name):
    P, Rp, W = send.shape
    Rh = Rp // 2
    rc = Rh // _NCH

    def body(s_ref, o_ref, ssem, rsem):
        x, y, c = lax.axis_index("x"), lax.axis_index("y"), lax.axis_index("c")
        cps = []
        for k in range(P):
            for q in range(_NCH):
                cp = _rcopy(s_ref.at[k, pl.ds((1 - c) * Rh + q * rc, rc)], o_ref.at[k, pl.ds(q * rc, rc)],
                            ssem.at[k * _NCH + q], rsem.at[k * _NCH + q], (x, y, 1 - c))
                cp.start()
                cps.append(cp)
        for cp in cps:
            cp.wait()

    return pl.pallas_call(
        body, name=name, out_shape=jax.ShapeDtypeStruct((P, Rh, W), send.dtype), in_specs=[_HBM], out_specs=_HBM,
        scratch_shapes=[pltpu.SemaphoreType.DMA((P * _NCH,))] * 2,
    )(send)


def _scatter_to_chips(cs, name):
    P, Rh, W = cs.shape
    rc = Rh // _NCH

    def body(s_ref, o_ref, ssem, rsem, lsem):
        x, y, c = lax.axis_index("x"), lax.axis_index("y"), lax.axis_index("c")
        chip = 2 * x + y
        peers = _other_chips(x, y)
        loc = pltpu.make_async_copy(s_ref.at[chip], o_ref.at[chip], lsem)
        loc.start()
        cps = []
        for j, (px, py) in enumerate(peers):
            for q in range(_NCH):
                r = pl.ds(q * rc, rc)
                cp = _rcopy(s_ref.at[2 * px + py, r], o_ref.at[chip, r], ssem.at[j * _NCH + q], rsem.at[j * _NCH + q],
                            (px, py, c))
                cp.start()
                cps.append(cp)
        for j, (px, py) in enumerate(peers):
            for q in range(_NCH):
                blk = o_ref.at[2 * px + py, pl.ds(q * rc, rc)]
                _rcopy(blk, blk, ssem.at[j * _NCH + q], rsem.at[j * _NCH + q], (px, py, c)).wait_recv()
        for cp in cps:
            cp.wait_send()
        loc.wait()

    return pl.pallas_call(
        body, name=name, out_shape=jax.ShapeDtypeStruct((P, Rh, W), cs.dtype), in_specs=[_HBM], out_specs=_HBM,
        scratch_shapes=[pltpu.SemaphoreType.DMA((3 * _NCH,))] * 2 + [pltpu.SemaphoreType.DMA],
    )(cs)


def _swap_with_sibling(v, name):
    R, W = v.shape
    rc = R // _NCH

    def body(s_ref, o_ref, ssem, rsem):
        x, y, c = lax.axis_index("x"), lax.axis_index("y"), lax.axis_index("c")
        cps = []
        for q in range(_NCH):
            r = pl.ds(q * rc, rc)
            cp = _rcopy(s_ref.at[r], o_ref.at[r], ssem.at[q], rsem.at[q], (x, y, 1 - c))
            cp.start()
            cps.append(cp)
        for cp in cps:
            cp.wait()

    return pl.pallas_call(
        body, name=name, out_shape=jax.ShapeDtypeStruct((R, W), v.dtype), in_specs=[_HBM], out_specs=_HBM,
        scratch_shapes=[pltpu.SemaphoreType.DMA((_NCH,))] * 2,
    )(v)


def _mm(a, b, name, li=None, trans_b=False, out_dtype=_F32, tm=1024, tn=512, tk=2048):
    M, K = a.shape
    bs = b.shape[-2:]
    N = bs[0] if trans_b else bs[1]
    tm, tn, tk = _pick(M, tm, 8), _pick(N, tn), _pick(K, tk)
    nk = K // tk
    lead = () if li is None else (None,)

    def bmap(i, j, k):
        idx = (j, k) if trans_b else (k, j)
        return idx if li is None else (li,) + idx

    def body(a_ref, b_ref, o_ref, acc):
        k = pl.program_id(2)
        part = lax.dot_general(a_ref[...], b_ref[...], _NT if trans_b else _NN, preferred_element_type=_F32)
        if nk == 1:
            o_ref[...] = part.astype(o_ref.dtype)
        else:
            @pl.when(k == 0)
            def _():
                acc[...] = part

            @pl.when(k > 0)
            def _():
                acc[...] += part

            @pl.when(k == nk - 1)
            def _():
                o_ref[...] = acc[...].astype(o_ref.dtype)

    return pl.pallas_call(
        body, name=name, grid=(M // tm, N // tn, nk),
        in_specs=[pl.BlockSpec((tm, tk), lambda i, j, k: (i, k)),
                  pl.BlockSpec(lead + ((tn, tk) if trans_b else (tk, tn)), bmap)],
        out_specs=pl.BlockSpec((tm, tn), lambda i, j, k: (i, j)),
        out_shape=jax.ShapeDtypeStruct((M, N), out_dtype),
        scratch_shapes=[pltpu.VMEM((tm, tn) if nk > 1 else (8, _LANE), _F32)],
        compiler_params=_cp("parallel", "parallel", "arbitrary"),
    )(a, b)


def _mm_tn(a, b, name, tm=512, tn=512):
    T, M = a.shape
    N = b.shape[1]
    tm, tn = _pick(M, tm), _pick(N, tn)

    def body(a_ref, b_ref, o_ref):
        o_ref[...] = lax.dot_general(a_ref[...], b_ref[...], _TN, preferred_element_type=_F32)

    return pl.pallas_call(
        body, name=name, grid=(M // tm, N // tn),
        in_specs=[pl.BlockSpec((T, tm), lambda i, j: (0, i)), pl.BlockSpec((T, tn), lambda i, j: (0, j))],
        out_specs=pl.BlockSpec((tm, tn), lambda i, j: (i, j)),
        out_shape=jax.ShapeDtypeStruct((M, N), _F32),
        compiler_params=_cp("parallel", "parallel"),
    )(a, b)


def _ada_fwd(c_all, ada_w, name):
    L, D, Ws = ada_w.shape
    B = c_all.shape[0]

    def body(c_ref, w_ref, o_ref):
        o_ref[...] = _dot(_silu(c_ref[...]), w_ref[...])

    return pl.pallas_call(
        body, name=name, grid=(L,),
        in_specs=[_fspec((B, D)), pl.BlockSpec((None, D, Ws), lambda l: (l, 0, 0))],
        out_specs=pl.BlockSpec((None, B, Ws), lambda l: (l, 0, 0)),
        out_shape=jax.ShapeDtypeStruct((L, B, Ws), _F32), compiler_params=_cp("parallel"),
    )(c_all, ada_w)


def _ada_bwd(c_all_t, dmod, name):
    D, B = c_all_t.shape
    L, _, Ws = dmod.shape

    def body(c_ref, d_ref, o_ref):
        ct = _silu(c_ref[...])
        d = d_ref[...]
        acc = ct[:, 0:1] * d[0:1, :]
        for b in range(1, B):
            acc = acc + ct[:, b:b + 1] * d[b:b + 1, :]
        o_ref[...] = acc

    return pl.pallas_call(
        body, name=name, grid=(L,),
        in_specs=[_fspec((D, B)), pl.BlockSpec((None, B, Ws), lambda l: (l, 0, 0))],
        out_specs=pl.BlockSpec((None, D, Ws), lambda l: (l, 0, 0)),
        out_shape=jax.ShapeDtypeStruct((L, D, Ws), _F32), compiler_params=_cp("parallel"),
    )(c_all_t, dmod)


def _norm_mod(x, g, sc, sh, name, tb=512):
    T, D = x.shape
    tb = _pick(T, tb, 8)

    def body(x_ref, g_ref, sc_ref, sh_ref, h_ref):
        xv = x_ref[...]
        r = lax.rsqrt(jnp.mean(xv * xv, axis=1, keepdims=True) + _EPS)
        h_ref[...] = (xv * r * (g_ref[...] * (1.0 + sc_ref[...])) + sh_ref[...]).astype(h_ref.dtype)

    return pl.pallas_call(
        body, name=name, grid=(T // tb,),
        in_specs=[_rspec(tb, D), _fspec((1, D)), _fspec((1, D)), _fspec((1, D))],
        out_specs=_rspec(tb, D), out_shape=jax.ShapeDtypeStruct((T, D), _MMT), compiler_params=_cp("parallel"),
    )(x, g, sc, sh)


def _norm_mod_bwd(x, dh, dres, g, sc, name, tb=256):
    T, D = x.shape
    tb = _pick(T, tb, 8)

    def body(x_ref, dh_ref, dr_ref, g_ref, sc_ref, dx_ref, dgm_ref, dsh_ref):
        i = pl.program_id(0)
        xv, dh_ = x_ref[...], dh_ref[...]
        r = lax.rsqrt(jnp.mean(xv * xv, axis=1, keepdims=True) + _EPS)
        xn = xv * r
        dxn = dh_ * (g_ref[...] * (1.0 + sc_ref[...]))
        dx_ref[...] = dr_ref[...] + r * (dxn - xn * jnp.mean(dxn * xn, axis=1, keepdims=True))

        @pl.when(i == 0)
        def _():
            dgm_ref[...] = jnp.zeros_like(dgm_ref)
            dsh_ref[...] = jnp.zeros_like(dsh_ref)

        dgm_ref[...] += jnp.sum(dh_ * xn, axis=0, keepdims=True)
        dsh_ref[...] += jnp.sum(dh_, axis=0, keepdims=True)

    return pl.pallas_call(
        body, name=name, grid=(T // tb,),
        in_specs=[_rspec(tb, D), _rspec(tb, D), _rspec(tb, D), _fspec((1, D)), _fspec((1, D))],
        out_specs=[_rspec(tb, D), _fspec((1, D)), _fspec((1, D))],
        out_shape=[jax.ShapeDtypeStruct((T, D), _F32), jax.ShapeDtypeStruct((1, D), _F32),
                   jax.ShapeDtypeStruct((1, D), _F32)],
        compiler_params=_cp("arbitrary"),
    )(x, dh, dres, g, sc)


def _resid(x, p, gt, name, tb=512):
    T, D = x.shape
    tb = _pick(T, tb, 8)

    def body(x_ref, p_ref, gt_ref, o_ref):
        o_ref[...] = x_ref[...] + gt_ref[...] * p_ref[...]

    return pl.pallas_call(
        body, name=name, grid=(T // tb,), in_specs=[_rspec(tb, D), _rspec(tb, D), _fspec((1, D))],
        out_specs=_rspec(tb, D), out_shape=jax.ShapeDtypeStruct((T, D), _F32), compiler_params=_cp("parallel"),
    )(x, p, gt)


def _resid_bwd(dx, p, gt, name, tb=512):
    T, D = dx.shape
    tb = _pick(T, tb, 8)

    def body(dx_ref, p_ref, gt_ref, dp_ref, dgt_ref):
        i = pl.program_id(0)
        d = dx_ref[...]
        dp_ref[...] = (d * gt_ref[...]).astype(dp_ref.dtype)

        @pl.when(i == 0)
        def _():
            dgt_ref[...] = jnp.zeros_like(dgt_ref)

        dgt_ref[...] += jnp.sum(d * p_ref[...], axis=0, keepdims=True)

    return pl.pallas_call(
        body, name=name, grid=(T // tb,), in_specs=[_rspec(tb, D), _rspec(tb, D), _fspec((1, D))],
        out_specs=[_rspec(tb, D), _fspec((1, D))],
        out_shape=[jax.ShapeDtypeStruct((T, D), _MMT), jax.ShapeDtypeStruct((1, D), _F32)],
        compiler_params=_cp("arbitrary"),
    )(dx, p, gt)


def _gmlp_chunk(u_raw, v_raw, sw_ref, sbt, gv, G):
    u, v = _gelu(u_raw), _gelu(v_raw)
    ii = lax.broadcasted_iota(jnp.int32, (_AC, _AC), 0)
    jj = lax.broadcasted_iota(jnp.int32, (_AC, _AC), 1)
    out = []
    for gi in range(G):
        sl = slice(gi * _GD, (gi + 1) * _GD)
        vg = v[:, sl]
        r = lax.rsqrt(jnp.mean(vg * vg, axis=1, keepdims=True) + _EPS)
        vhat = vg * r
        W = jnp.where(jj <= ii, sw_ref[gi], 0.0)
        s = _dot(W, vhat * gv[:, sl]) + sbt[:, gi:gi + 1]
        out.append((u[:, sl], s, vhat, r, W))
    return out


def _gmlp_fwd(proj, sw, sbt, gv, li, D, name):
    T = proj.shape[0]
    G = D // _GD

    def body(u_ref, v_ref, sw_ref, sbt_ref, gv_ref, y_ref):
        parts = _gmlp_chunk(u_ref[...], v_ref[...], sw_ref, sbt_ref[...], gv_ref[...], G)
        for gi, (u, s, _, _, _) in enumerate(parts):
            y_ref[:, gi * _GD:(gi + 1) * _GD] = (u * s).astype(y_ref.dtype)

    return pl.pallas_call(
        body, name=name, grid=(T // _AC,),
        in_specs=[_rspec(_AC, D, 0), _rspec(_AC, D, 1), _lspec((G, _AC, _AC), li), _lspec((_AC, G), li),
                  _lspec((1, D), li)],
        out_specs=_rspec(_AC, D), out_shape=jax.ShapeDtypeStruct((T, D), _MMT), compiler_params=_cp("parallel"),
    )(proj, proj, sw, sbt, gv)


def _gmlp_bwd(proj, dy, sw, sbt, gv, li, D, into, name):
    T = proj.shape[0]
    G = D // _GD

    def body(u_ref, v_ref, dy_ref, sw_ref, sbt_ref, gv_ref, _, duv_ref, dsw_ref, dsa_ref, dgv_ref):
        i = pl.program_id(0)

        @pl.when(i == 0)
        def _():
            dsw_ref[...] = jnp.zeros_like(dsw_ref)
            dsa_ref[...] = jnp.zeros_like(dsa_ref)
            dgv_ref[...] = jnp.zeros_like(dgv_ref)

        u_raw, v_raw, dy_, gv_ = u_ref[...], v_ref[...], dy_ref[...], gv_ref[...]
        parts = _gmlp_chunk(u_raw, v_raw, sw_ref, sbt_ref[...], gv_, G)
        ii = lax.broadcasted_iota(jnp.int32, (_AC, _AC), 0)
        jj = lax.broadcasted_iota(jnp.int32, (_AC, _AC), 1)
        dgu, dgv = _dgelu(u_raw), _dgelu(v_raw)
        for gi, (u, s, vhat, r, W) in enumerate(parts):
            sl = slice(gi * _GD, (gi + 1) * _GD)
            dyg = dy_[:, sl]
            ds = dyg * u
            vn = vhat * gv_[:, sl]
            dsw_ref[gi] += jnp.where(jj <= ii, _dot(ds, vn, _NT), 0.0)
            dsa_ref[:, sl] += ds
            dvn = _dot(W, ds, _TN)
            dgv_ref[:, sl] += jnp.sum(dvn * vhat, axis=0, keepdims=True)
            dvh = dvn * gv_[:, sl]
            dvg = r * (dvh - vhat * jnp.mean(dvh * vhat, axis=1, keepdims=True))
            duv_ref[:, sl] = (dyg * s * dgu[:, sl]).astype(duv_ref.dtype)
            duv_ref[:, D + gi * _GD:D + (gi + 1) * _GD] = (dvg * dgv[:, sl]).astype(duv_ref.dtype)

    return pl.pallas_call(
        body, name=name, grid=(T // _AC,),
        in_specs=[_rspec(_AC, D, 0), _rspec(_AC, D, 1), _rspec(_AC, D), _lspec((G, _AC, _AC), li),
                  _lspec((_AC, G), li), _lspec((1, D), li), _HBM],
        out_specs=[_rspec(_AC, 2 * D), _fspec((G, _AC, _AC)), _fspec((_AC, D)), _fspec((1, D))],
        out_shape=[jax.ShapeDtypeStruct(into.shape, into.dtype), jax.ShapeDtypeStruct((G, _AC, _AC), _F32),
                   jax.ShapeDtypeStruct((_AC, D), _F32), jax.ShapeDtypeStruct((1, D), _F32)],
        input_output_aliases={6: 0}, compiler_params=_cp("arbitrary"),
    )(proj, proj, dy, sw, sbt, gv, into)


def _conv_taps(halo, cur, first):
    tb = cur.shape[0]
    full = jnp.concatenate([jnp.where(first, 0.0, halo), cur], axis=0)
    return [full[8:] if j == _KC - 1 else pltpu.roll(full, _KC - 1 - j, 0)[8:] for j in range(_KC)]


def _prev_spec(tb, w, cb):
    return pl.BlockSpec((8, w), lambda i: (jnp.maximum(i * (tb // 8) - 1, 0), cb))


def _l2_heads(x, H):
    outs, rs = [], []
    for h in range(H):
        xh = x[:, h * _GD:(h + 1) * _GD]
        r = lax.rsqrt(jnp.sum(xh * xh, axis=1, keepdims=True) + _EPS)
        outs.append(xh * r)
        rs.append(r)
    return outs, rs


def _gate_rows(ba, alog_row, dtb_row, H):
    lane = lax.broadcasted_iota(jnp.int32, ba.shape, 1)
    beta = _sigmoid(ba)
    g = -jnp.exp(alog_row) * _softplus(ba + dtb_row)
    return lane, beta, g


def _conv_fwd(proj, cw, alog_row, dtb_row, li, D, name, tb=256):
    T = proj.shape[0]
    H = D // _GD
    tb = _pick(T, tb, 8)
    bac = (8 * D) // _LANE

    def body(q_ref, k_ref, v_ref, qh_ref, kh_ref, vh_ref, ba_ref, cw_ref, al_ref, dtb_ref,
             qo_ref, ko_ref, vo_ref, bg_ref):
        first = pl.program_id(0) == 0
        cw_ = cw_ref[...]
        for idx, (cur, halo, out) in enumerate(((q_ref, qh_ref, qo_ref), (k_ref, kh_ref, ko_ref),
                                                 (v_ref, vh_ref, vo_ref))):
            taps = _conv_taps(halo[...], cur[...], first)
            w = cw_[:, idx * D:(idx + 1) * D]
            cv = taps[0] * w[0:1, :]
            for j in range(1, _KC):
                cv = cv + taps[j] * w[j:j + 1, :]
            act = _silu(cv)
            if idx < 2:
                outs, _ = _l2_heads(act, H)
                for h in range(H):
                    out[:, h * _GD:(h + 1) * _GD] = outs[h]
            else:
                out[...] = act
        lane, beta, g = _gate_rows(ba_ref[...], al_ref[...], dtb_ref[...], H)
        bg_ref[...] = jnp.where(lane < H, beta, jnp.where(lane < 2 * H, g, 0.0))

    return pl.pallas_call(
        body, name=name, grid=(T // tb,),
        in_specs=[_rspec(tb, D, 2), _rspec(tb, D, 3), _rspec(tb, D, 4),
                  _prev_spec(tb, D, 2), _prev_spec(tb, D, 3), _prev_spec(tb, D, 4),
                  _rspec(tb, _LANE, bac), _lspec((_KC, 3 * D), li), _lspec((1, _LANE), li), _lspec((1, _LANE), li)],
        out_specs=[_rspec(tb, D), _rspec(tb, D), _rspec(tb, D), _rspec(tb, _LANE)],
        out_shape=[jax.ShapeDtypeStruct((T, D), _F32)] * 3 + [jax.ShapeDtypeStruct((T, _LANE), _F32)],
        compiler_params=_cp("parallel"),
    )(proj, proj, proj, proj, proj, proj, proj, cw, alog_row, dtb_row)


def _conv_bwd1(proj, dqn, dkn, dvs, dbg, cw, alog_row, dtb_row, li, D, into, name, tb=256):
    T = proj.shape[0]
    H = D // _GD
    tb = _pick(T, tb, 8)
    bac = (8 * D) // _LANE

    def body(q_ref, k_ref, v_ref, qh_ref, kh_ref, vh_ref, ba_ref, dq_ref, dk_ref, dv_ref, dbg_ref,
             cw_ref, al_ref, dtb_ref, _, dc_ref, dba_ref, dcw_ref, dal_ref, ddt_ref):
        i = pl.program_id(0)
        first = i == 0

        @pl.when(first)
        def _():
            dcw_ref[...] = jnp.zeros_like(dcw_ref)
            dal_ref[...] = jnp.zeros_like(dal_ref)
            ddt_ref[...] = jnp.zeros_like(ddt_ref)

        cw_ = cw_ref[...]
        for idx, (cur, halo, dref) in enumerate(((q_ref, qh_ref, dq_ref), (k_ref, kh_ref, dk_ref),
                                                  (v_ref, vh_ref, dv_ref))):
            taps = _conv_taps(halo[...], cur[...], first)
            w = cw_[:, idx * D:(idx + 1) * D]
            cv = taps[0] * w[0:1, :]
            for j in range(1, _KC):
                cv = cv + taps[j] * w[j:j + 1, :]
            dact = dref[...]
            if idx < 2:
                outs, rs = _l2_heads(_silu(cv), H)
                pieces = []
                for h in range(H):
                    dy = dact[:, h * _GD:(h + 1) * _GD]
                    pieces.append(rs[h] * (dy - outs[h] * jnp.sum(dy * outs[h], axis=1, keepdims=True)))
                dact = jnp.concatenate(pieces, axis=1)
            dcv = dact * _dsilu(cv)
            dc_ref[:, idx * D:(idx + 1) * D] = dcv
            for j in range(_KC):
                dcw_ref[j:j + 1, idx * D:(idx + 1) * D] += jnp.sum(dcv * taps[j], axis=0, keepdims=True)

        ba = ba_ref[...]
        lane, beta, g = _gate_rows(ba, al_ref[...], dtb_ref[...], H)
        dbg_ = dbg_ref[...]
        is_b, is_a = lane < H, jnp.logical_and(lane >= H, lane < 2 * H)
        da = dbg_ * (-jnp.exp(al_ref[...])) * _sigmoid(ba + dtb_ref[...])
        dba_ref[...] = jnp.where(is_b, dbg_ * beta * (1.0 - beta), jnp.where(is_a, da, 0.0)).astype(dba_ref.dtype)
        dal_ref[...] += jnp.sum(jnp.where(is_a, dbg_ * g, 0.0), axis=0, keepdims=True)
        ddt_ref[...] += jnp.sum(jnp.where(is_a, da, 0.0), axis=0, keepdims=True)

    return pl.pallas_call(
        body, name=name, grid=(T // tb,),
        in_specs=[_rspec(tb, D, 2), _rspec(tb, D, 3), _rspec(tb, D, 4),
                  _prev_spec(tb, D, 2), _prev_spec(tb, D, 3), _prev_spec(tb, D, 4),
                  _rspec(tb, _LANE, bac), _rspec(tb, D), _rspec(tb, D), _rspec(tb, D), _rspec(tb, _LANE),
                  _lspec((_KC, 3 * D), li), _lspec((1, _LANE), li), _lspec((1, _LANE), li), _HBM],
        out_specs=[_rspec(tb, 3 * D), _rspec(tb, _LANE, bac), _fspec((_KC, 3 * D)), _fspec((1, _LANE)),
                   _fspec((1, _LANE))],
        out_shape=[jax.ShapeDtypeStruct((T, 3 * D), _F32), jax.ShapeDtypeStruct(into.shape, into.dtype),
                   jax.ShapeDtypeStruct((_KC, 3 * D), _F32), jax.ShapeDtypeStruct((1, _LANE), _F32),
                   jax.ShapeDtypeStruct((1, _LANE), _F32)],
        input_output_aliases={14: 1}, compiler_params=_cp("arbitrary"),
    )(proj, proj, proj, proj, proj, proj, proj, dqn, dkn, dvs, dbg, cw, alog_row, dtb_row, into)


def _conv_bwd2(dc, cw, li, into, name, tb=256):
    T, W3 = dc.shape
    W = W3 // 3
    tb = _pick(T, tb, 8)
    nb8 = T // 8
    nrow = T // tb

    def body(dc_ref, nx_ref, cw_ref, _, o_ref):
        last = pl.program_id(0) == nrow - 1
        full = jnp.concatenate([dc_ref[...], jnp.where(last, 0.0, nx_ref[...])], axis=0)
        w = cw_ref[...]
        acc = full[:tb] * w[_KC - 1:_KC, :]
        for j in range(_KC - 1):
            sh = _KC - 1 - j
            acc = acc + pltpu.roll(full, tb + 8 - sh, 0)[:tb] * w[j:j + 1, :]
        o_ref[...] = acc.astype(o_ref.dtype)

    return pl.pallas_call(
        body, name=name, grid=(nrow, 3),
        in_specs=[pl.BlockSpec((tb, W), lambda i, j: (i, j)),
                  pl.BlockSpec((8, W), lambda i, j: (jnp.minimum((i + 1) * (tb // 8), nb8 - 1), j)),
                  pl.BlockSpec((None, _KC, W), lambda i, j: (li, 0, j)), _HBM],
        out_specs=pl.BlockSpec((tb, W), lambda i, j: (i, 2 + j)),
        out_shape=jax.ShapeDtypeStruct(into.shape, into.dtype), input_output_aliases={3: 0},
        compiler_params=_cp("parallel", "parallel"),
    )(dc, dc, cw, into)


def _split(a):
    hi = a.astype(_BF)
    return hi, (a - hi.astype(_F32)).astype(_BF)


def _dot3(a, b):
    (ah, al), (bh, bl) = a, b
    f = functools.partial(lax.dot_general, dimension_numbers=_NN, preferred_element_type=_F32)
    return f(ah, bh) + f(ah, bl) + f(al, bh)


def _inv_unit_lower(mats):
    C = mats[0].shape[0]
    ii = lax.broadcasted_iota(jnp.int32, (C, C), 0)
    jj = lax.broadcasted_iota(jnp.int32, (C, C), 1)
    xs = [jnp.where(ii == jj, 1.0, 0.0) - a for a in mats]
    ps = list(mats)
    n = 1
    while 2 * n < C:
        sp = [_split(p) for p in ps]
        ps = [_dot3(s, s) for s in sp]
        sp = [_split(p) for p in ps]
        xs = [x + _dot3(_split(x), s) for x, s in zip(xs, sp)]
        n *= 2
    return xs


def _gdn_chunk(q, k, v, g_row, b_row):
    C = q.shape[0]
    ii = lax.broadcasted_iota(jnp.int32, (C, C), 0)
    jj = lax.broadcasted_iota(jnp.int32, (C, C), 1)
    low, strict, eye = jj <= ii, jj < ii, ii == jj
    g_col = jnp.sum(jnp.where(eye, g_row, 0.0), axis=1, keepdims=True)
    b_col = jnp.sum(jnp.where(eye, b_row, 0.0), axis=1, keepdims=True)
    gam_col = jnp.sum(jnp.where(low, g_row, 0.0), axis=1, keepdims=True)
    gam_row = jnp.sum(jnp.where(jj >= ii, g_col, 0.0), axis=0, keepdims=True)
    gam_last = jnp.sum(g_row, axis=1, keepdims=True)
    decay = jnp.where(low, jnp.exp(jnp.where(low, gam_col - gam_row, 0.0)), 0.0)
    eg = jnp.exp(gam_col)
    ekd = jnp.exp(gam_last - gam_col)
    qs = q * (_GD ** -0.5)
    kb = k * b_col
    kk = _dot(kb, k, _NT)
    qkraw = _dot(qs, k, _NT)
    return dict(low=low, strict=strict, eye=eye, ii=ii, jj=jj, b_col=b_col, decay=decay, eg=eg, ekd=ekd,
                gl=jnp.exp(gam_last), qs=qs, kb=kb, kk=kk, qkraw=qkraw,
                A=jnp.where(strict, kk * decay, 0.0), vb=v * b_col, kbg=kb * eg,
                qk=qkraw * decay, q_dec=qs * eg, k_dec=k * ekd)


def _gdn_fwd(qn, kn, vs, g_r, b_r, name):
    T, D = qn.shape
    H, N, C = D // _GD, T // _BC, _BC
    hb = min(_HB, H)

    def body(q_ref, k_ref, v_ref, g_ref, b_ref, o_ref, s_ref, t_ref, S):
        @pl.when(pl.program_id(1) == 0)
        def _():
            S[...] = jnp.zeros_like(S)

        hs = range(hb)
        sls = [slice(hh * _GD, (hh + 1) * _GD) for hh in hs]
        cms = [_gdn_chunk(q_ref[:, sl], k_ref[:, sl], v_ref[:, sl], g_ref[hh], b_ref[hh]) for hh, sl in zip(hs, sls)]
        tms = _inv_unit_lower([cm["A"] for cm in cms])
        us = [_dot(tm, cm["vb"]) for tm, cm in zip(tms, cms)]
        ws = [_dot(tm, cm["kbg"]) for tm, cm in zip(tms, cms)]
        s0s = [S[hh] for hh in hs]
        for hh in hs:
            s_ref[hh] = s0s[hh]
            t_ref[hh] = tms[hh]
        v_news = [u - _dot(w, s0) for u, w, s0 in zip(us, ws, s0s)]
        qss = [_dot(cm["q_dec"], s0) for cm, s0 in zip(cms, s0s)]
        for hh in hs:
            o_ref[:, sls[hh]] = qss[hh] + _dot(cms[hh]["qk"], v_news[hh])
        for hh in hs:
            S[hh] = s0s[hh] * cms[hh]["gl"] + _dot(cms[hh]["k_dec"], v_news[hh], _TN)

    qspec = pl.BlockSpec((C, hb * _GD), lambda h, n: (n, h))
    gspec = pl.BlockSpec((hb, None, 1, C), lambda h, n: (h, n, 0, 0))
    return pl.pallas_call(
        body, name=name, grid=(H // hb, N),
        in_specs=[qspec, qspec, qspec, gspec, gspec],
        out_specs=[qspec, pl.BlockSpec((hb, None, _GD, _GD), lambda h, n: (h, n, 0, 0)),
                   pl.BlockSpec((hb, None, C, C), lambda h, n: (h, n, 0, 0))],
        out_shape=[jax.ShapeDtypeStruct((T, D), _F32), jax.ShapeDtypeStruct((H, N, _GD, _GD), _F32),
                   jax.ShapeDtypeStruct((H, N, C, C), _F32)],
        scratch_shapes=[pltpu.VMEM((hb, _GD, _GD), _F32)],
        compiler_params=_cp("arbitrary", "arbitrary"),
    )(qn, kn, vs, g_r, b_r)


def _gdn_bwd(qn, kn, vs, g_r, b_r, s_all, t_all, do, name):
    T, D = qn.shape
    H, N, C = D // _GD, T // _BC, _BC
    hb = min(_HB, H)

    def body(q_ref, k_ref, v_ref, g_ref, b_ref, s_ref, t_ref, do_ref, dq_ref, dk_ref, dv_ref, dg_ref, db_ref, dS):
        @pl.when(pl.program_id(1) == 0)
        def _():
            dS[...] = jnp.zeros_like(dS)

        hs = range(hb)
        sls = [slice(hh * _GD, (hh + 1) * _GD) for hh in hs]
        ks = [k_ref[:, sl] for sl in sls]
        vs_ = [v_ref[:, sl] for sl in sls]
        cms = [_gdn_chunk(q_ref[:, sl], k, v, g_ref[hh], b_ref[hh]) for hh, sl, k, v in zip(hs, sls, ks, vs_)]
        low, strict, eye, ii, jj = (cms[0][n] for n in ("low", "strict", "eye", "ii", "jj"))
        tms, s0s, dos, ds1s = [t_ref[hh] for hh in hs], [s_ref[hh] for hh in hs], [do_ref[:, sl] for sl in sls], \
            [dS[hh] for hh in hs]
        us = [_dot(tm, cm["vb"]) for tm, cm in zip(tms, cms)]
        ws = [_dot(tm, cm["kbg"]) for tm, cm in zip(tms, cms)]
        v_news = [u - _dot(w, s0) for u, w, s0 in zip(us, ws, s0s)]
        dv_news = [_dot(cm["qk"], do_, _TN) + _dot(cm["k_dec"], ds1) for cm, do_, ds1 in zip(cms, dos, ds1s)]
        dqks = [jnp.where(low, _dot(do_, vn, _NT), 0.0) for do_, vn in zip(dos, v_news)]
        dq_decs = [_dot(do_, s0, _NT) for do_, s0 in zip(dos, s0s)]
        dk_decs = [_dot(vn, ds1, _NT) for vn, ds1 in zip(v_news, ds1s)]
        dgls = [jnp.sum(jnp.sum(ds1 * s0, axis=1, keepdims=True), axis=0, keepdims=True) for ds1, s0 in zip(ds1s, s0s)]
        dws = [-_dot(dvn, s0, _NT) for dvn, s0 in zip(dv_news, s0s)]
        for hh in hs:
            dS[hh] = (_dot(cms[hh]["q_dec"], dos[hh], _TN) + cms[hh]["gl"] * ds1s[hh]
                      - _dot(ws[hh], dv_news[hh], _TN))
        dvbs = [_dot(tm, dvn, _TN) for tm, dvn in zip(tms, dv_news)]
        dkbgs = [_dot(tm, dw, _TN) for tm, dw in zip(tms, dws)]
        dAs = [-jnp.where(strict, _dot(dvb, u, _NT) + _dot(dkbg, w, _NT), 0.0)
               for dvb, u, dkbg, w in zip(dvbs, us, dkbgs, ws)]
        dkks = [dA * cm["decay"] for dA, cm in zip(dAs, cms)]
        dqkraws = [dqk * cm["decay"] for dqk, cm in zip(dqks, cms)]
        Es = [(dA * cm["kk"] + dqk * cm["qkraw"]) * cm["decay"] for dA, dqk, cm in zip(dAs, dqks, cms)]
        dkbs = [_dot(dkk, k) + dkbg * cm["eg"] for dkk, k, dkbg, cm in zip(dkks, ks, dkbgs, cms)]
        dqss = [_dot(dqr, k) + dqd * cm["eg"] for dqr, k, dqd, cm in zip(dqkraws, ks, dq_decs, cms)]
        for hh in hs:
            cm = cms[hh]
            dk_ref[:, sls[hh]] = (_dot(dqkraws[hh], cm["qs"], _TN) + _dot(dkks[hh], cm["kb"], _TN)
                                  + dk_decs[hh] * cm["ekd"] + dkbs[hh] * cm["b_col"])
            dv_ref[:, sls[hh]] = dvbs[hh] * cm["b_col"]
            dq_ref[:, sls[hh]] = dqss[hh] * (_GD ** -0.5)
        for hh in hs:
            cm, k, E = cms[hh], ks[hh], Es[hh]
            eg, ekd = cm["eg"], cm["ekd"]
            dbeta_col = jnp.sum(dvbs[hh] * vs_[hh] + dkbs[hh] * k, axis=1, keepdims=True)
            t_kd = jnp.sum(dk_decs[hh] * k, axis=1, keepdims=True) * ekd
            c1 = (jnp.sum(E, axis=1, keepdims=True) + jnp.sum(dkbgs[hh] * cm["kb"], axis=1, keepdims=True) * eg
                  + jnp.sum(dq_decs[hh] * cm["qs"], axis=1, keepdims=True) * eg - t_kd)
            r1 = jnp.sum(E, axis=0, keepdims=True)
            dgam_last = jnp.sum(t_kd, axis=0, keepdims=True) + dgls[hh] * cm["gl"]
            dgam_col = c1 - jnp.sum(jnp.where(eye, r1, 0.0), axis=1, keepdims=True)
            dg_ref[hh] = jnp.sum(jnp.where(ii >= jj, dgam_col, 0.0), axis=0, keepdims=True) + dgam_last
            db_ref[hh] = jnp.sum(jnp.where(eye, dbeta_col, 0.0), axis=0, keepdims=True)

    qspec = pl.BlockSpec((C, hb * _GD), lambda h, n: (N - 1 - n, h))
    gspec = pl.BlockSpec((hb, None, 1, C), lambda h, n: (h, N - 1 - n, 0, 0))
    return pl.pallas_call(
        body, name=name, grid=(H // hb, N),
        in_specs=[qspec, qspec, qspec, gspec, gspec,
                  pl.BlockSpec((hb, None, _GD, _GD), lambda h, n: (h, N - 1 - n, 0, 0)),
                  pl.BlockSpec((hb, None, C, C), lambda h, n: (h, N - 1 - n, 0, 0)), qspec],
        out_specs=[qspec, qspec, qspec, gspec, gspec],
        out_shape=[jax.ShapeDtypeStruct((T, D), _F32)] * 3 + [jax.ShapeDtypeStruct((H, N, 1, C), _F32)] * 2,
        scratch_shapes=[pltpu.VMEM((hb, _GD, _GD), _F32)],
        compiler_params=_cp("arbitrary", "arbitrary"),
    )(qn, kn, vs, g_r, b_r, s_all, t_all, do)


def _onorm_fwd(o, proj, go, li, D, name, tb=512):
    T = o.shape[0]
    H = D // _GD
    tb = _pick(T, tb, 8)

    def body(o_ref, z_ref, go_ref, y_ref):
        ov, zv, g = o_ref[...], z_ref[...], go_ref[...]
        for h in range(H):
            sl = slice(h * _GD, (h + 1) * _GD)
            oh = ov[:, sl]
            r = lax.rsqrt(jnp.mean(oh * oh, axis=1, keepdims=True) + _EPS)
            y_ref[:, sl] = (oh * r * g * _silu(zv[:, sl])).astype(y_ref.dtype)

    return pl.pallas_call(
        body, name=name, grid=(T // tb,), in_specs=[_rspec(tb, D), _rspec(tb, D, 5), _lspec((1, _GD), li)],
        out_specs=_rspec(tb, D), out_shape=jax.ShapeDtypeStruct((T, D), _MMT), compiler_params=_cp("parallel"),
    )(o, proj, go)


def _onorm_bwd(dy, o, proj, go, li, D, into, name, tb=256):
    T = o.shape[0]
    H = D // _GD
    tb = _pick(T, tb, 8)

    def body(dy_ref, o_ref, z_ref, go_ref, _, do_ref, dz_ref, dgo_ref):
        @pl.when(pl.program_id(0) == 0)
        def _():
            dgo_ref[...] = jnp.zeros_like(dgo_ref)

        dyv, ov, zv, g = dy_ref[...], o_ref[...], z_ref[...], go_ref[...]
        dgo = jnp.zeros((1, _GD), _F32)
        for h in range(H):
            sl = slice(h * _GD, (h + 1) * _GD)
            oh, zh, dyh = ov[:, sl], zv[:, sl], dyv[:, sl]
            r = lax.rsqrt(jnp.mean(oh * oh, axis=1, keepdims=True) + _EPS)
            on = oh * r
            sz = _silu(zh)
            dgo = dgo + jnp.sum(dyh * sz * on, axis=0, keepdims=True)
            don = dyh * sz * g
            do_ref[:, sl] = r * (don - on * jnp.mean(don * on, axis=1, keepdims=True))
            dz_ref[:, sl] = (dyh * on * g * _dsilu(zh)).astype(dz_ref.dtype)
        dgo_ref[...] += dgo

    return pl.pallas_call(
        body, name=name, grid=(T // tb,),
        in_specs=[_rspec(tb, D), _rspec(tb, D), _rspec(tb, D, 5), _lspec((1, _GD), li), _HBM],
        out_specs=[_rspec(tb, D), _rspec(tb, D, 5), _fspec((1, _GD))],
        out_shape=[jax.ShapeDtypeStruct((T, D), _F32), jax.ShapeDtypeStruct(into.shape, into.dtype),
                   jax.ShapeDtypeStruct((1, _GD), _F32)],
        input_output_aliases={4: 1}, compiler_params=_cp("arbitrary"),
    )(dy, o, proj, go, into)


def _merge(pa, pb, proj, D, name, tb=512):
    T = pa.shape[0]
    tb = _pick(T, tb, 8)

    def body(pa_ref, pb_ref, ga_ref, gb_ref, m_ref):
        m_ref[...] = (_sigmoid(ga_ref[...]) * pa_ref[...] + _sigmoid(gb_ref[...]) * pb_ref[...]).astype(m_ref.dtype)

    return pl.pallas_call(
        body, name=name, grid=(T // tb,), in_specs=[_rspec(tb, D), _rspec(tb, D), _rspec(tb, D, 6), _rspec(tb, D, 7)],
        out_specs=_rspec(tb, D), out_shape=jax.ShapeDtypeStruct((T, D), _MMT), compiler_params=_cp("parallel"),
    )(pa, pb, proj, proj)


def _merge_bwd(dm, pa, pb, proj, D, name, tb=256):
    T, PW = proj.shape
    tb = _pick(T, tb, 8)

    def body(dm_ref, pa_ref, pb_ref, ga_ref, gb_ref, dpa_ref, dpb_ref, dg_ref):
        d = dm_ref[...]
        sa, sb = _sigmoid(ga_ref[...]), _sigmoid(gb_ref[...])
        dpa_ref[...] = (d * sa).astype(dpa_ref.dtype)
        dpb_ref[...] = (d * sb).astype(dpb_ref.dtype)
        dg_ref[:, :D] = (d * pa_ref[...] * sa * (1.0 - sa)).astype(dg_ref.dtype)
        dg_ref[:, D:] = (d * pb_ref[...] * sb * (1.0 - sb)).astype(dg_ref.dtype)

    return pl.pallas_call(
        body, name=name, grid=(T // tb,),
        in_specs=[_rspec(tb, D), _rspec(tb, D), _rspec(tb, D), _rspec(tb, D, 6), _rspec(tb, D, 7)],
        out_specs=[_rspec(tb, D), _rspec(tb, D), _rspec(tb, 2 * D, 3)],
        out_shape=[jax.ShapeDtypeStruct((T, D), _MMT)] * 2 + [jax.ShapeDtypeStruct((T, PW), _MMT)],
        compiler_params=_cp("parallel"),
    )(dm, pa, pb, proj, proj)


def _swiglu(gu, name, tb=256):
    T, F2 = gu.shape
    F = F2 // 2
    tb = _pick(T, tb, 8)

    def body(g_ref, u_ref, a_ref):
        a_ref[...] = (_silu(g_ref[...]) * u_ref[...]).astype(a_ref.dtype)

    return pl.pallas_call(
        body, name=name, grid=(T // tb,), in_specs=[_rspec(tb, F, 0), _rspec(tb, F, 1)],
        out_specs=_rspec(tb, F), out_shape=jax.ShapeDtypeStruct((T, F), _MMT), compiler_params=_cp("parallel"),
    )(gu, gu)


def _swiglu_bwd(da, gu, name, tb=256):
    T, F2 = gu.shape
    F = F2 // 2
    tb = _pick(T, tb, 8)

    def body(da_ref, g_ref, u_ref, o_ref):
        d, g = da_ref[...], g_ref[...]
        o_ref[:, :F] = (d * u_ref[...] * _dsilu(g)).astype(o_ref.dtype)
        o_ref[:, F:] = (d * _silu(g)).astype(o_ref.dtype)

    return pl.pallas_call(
        body, name=name, grid=(T // tb,), in_specs=[_rspec(tb, F), _rspec(tb, F, 0), _rspec(tb, F, 1)],
        out_specs=_rspec(tb, F2), out_shape=jax.ShapeDtypeStruct((T, F2), _MMT), compiler_params=_cp("parallel"),
    )(da, gu, gu)


def _loss_head(x, tgt, fg, name, tb=256):
    T, D = x.shape
    tb = _pick(T, tb, 8)

    def body(x_ref, t_ref, fg_ref, loss_ref, dx_ref, dfg_ref):
        @pl.when(pl.program_id(0) == 0)
        def _():
            loss_ref[...] = jnp.zeros_like(loss_ref)
            dfg_ref[...] = jnp.zeros_like(dfg_ref)

        xv, fg_ = x_ref[...], fg_ref[...]
        r = lax.rsqrt(jnp.mean(xv * xv, axis=1, keepdims=True) + _EPS)
        xn = xv * r
        e = xn * fg_ - t_ref[...]
        loss_ref[...] += (0.5 / D) * jnp.sum(jnp.sum(e * e, axis=1, keepdims=True), axis=0, keepdims=True)
        dy = e * (1.0 / D)
        dfg_ref[...] += jnp.sum(dy * xn, axis=0, keepdims=True)
        dxn = dy * fg_
        dx_ref[...] = r * (dxn - xn * jnp.mean(dxn * xn, axis=1, keepdims=True))

    return pl.pallas_call(
        body, name=name, grid=(T // tb,), in_specs=[_rspec(tb, D), _rspec(tb, D), _fspec((1, D))],
        out_specs=[_fspec((1, 1)), _rspec(tb, D), _fspec((1, D))],
        out_shape=[jax.ShapeDtypeStruct((1, 1), _F32), jax.ShapeDtypeStruct((T, D), _F32),
                   jax.ShapeDtypeStruct((1, D), _F32)],
        compiler_params=_cp("arbitrary"),
    )(x, tgt, fg)


def _row_tile(R, W, budget=1 << 20, unit=8):
    if R * W * 4 <= budget or R % unit:
        return R
    best = unit
    for t in range(unit, R + 1, unit):
        if R % t == 0 and t * W * 4 <= budget:
            best = t
    return best


def _add_own_half(send, got, half, name):
    P, Rp, W = send.shape
    Rh = Rp // 2
    tb = _row_tile(Rh, W, 1 << 21, 16)

    def body(h_ref, a_ref, b_ref, o_ref):
        o_ref[...] = (a_ref[...].astype(_F32) + b_ref[...].astype(_F32)).astype(o_ref.dtype)

    return pl.pallas_call(
        body, name=name,
        grid_spec=pltpu.PrefetchScalarGridSpec(
            num_scalar_prefetch=1, grid=(P, Rh // tb),
            in_specs=[pl.BlockSpec((None, None, tb, W), lambda k, i, h: (k, h[0], i, 0)),
                      pl.BlockSpec((None, tb, W), lambda k, i, h: (k, i, 0))],
            out_specs=pl.BlockSpec((None, tb, W), lambda k, i, h: (k, i, 0))),
        out_shape=jax.ShapeDtypeStruct((P, Rh, W), send.dtype), compiler_params=_cp("parallel", "parallel"),
    )(half, send.reshape(P, 2, Rh, W), got)


def _sum_slots(st, name):
    P, R, W = st.shape
    tb = _row_tile(R, W, 1 << 20, 16)

    def body(s_ref, o_ref):
        acc = s_ref[0].astype(_F32)
        for p in range(1, P):
            acc = acc + s_ref[p].astype(_F32)
        o_ref[...] = acc

    return pl.pallas_call(
        body, name=name, grid=(R // tb,), in_specs=[pl.BlockSpec((P, tb, W), lambda i: (0, i, 0))],
        out_specs=_rspec(tb, W), out_shape=jax.ShapeDtypeStruct((R, W), _F32), compiler_params=_cp("parallel"),
    )(st)


def _adamw(w, gst, m, v, name):
    R, W = w.shape
    P = gst.shape[0]
    tb = _row_tile(R, W, 1 << 19)
    c1, c2 = 1.0 - _B1 ** _STEP, 1.0 - _B2 ** _STEP

    def body(w_ref, g_ref, m_ref, v_ref, go_ref, d_ref, mo_ref, vo_ref):
        g = g_ref[0]
        for p in range(1, P):
            g = g + g_ref[p]
        mn = _B1 * m_ref[...] + (1.0 - _B1) * g
        vn = _B2 * v_ref[...] + (1.0 - _B2) * (g * g)
        go_ref[...] = g
        mo_ref[...] = mn
        vo_ref[...] = vn
        d_ref[...] = -_LR * ((mn / c1) / (jnp.sqrt(vn / c2) + _AEPS) + _WD * w_ref[...])

    spec = _rspec(tb, W)
    return pl.pallas_call(
        body, name=name, grid=(R // tb,),
        in_specs=[spec, pl.BlockSpec((P, tb, W), lambda i: (0, i, 0)), spec, spec],
        out_specs=[spec] * 4, out_shape=[jax.ShapeDtypeStruct((R, W), _F32)] * 4, compiler_params=_cp("parallel"),
    )(w, gst, m, v)


def _as2d(a):
    if a.ndim == 1:
        return a.reshape(1, -1)
    return a.reshape(-1, a.shape[-1])


def kernel(x, c, ada_w, ada_b, norm1_g, w_in, conv_w, spatial_w, spatial_b, v_norm_g, a_log, dt_bias, o_norm_g, w_branch_a, w_branch_b, w_out, norm2_g, w_ffn_in, w_ffn_out, final_g, loss_target, m_ada_w, m_ada_b, m_norm1_g, m_w_in, m_conv_w, m_spatial_w, m_spatial_b, m_v_norm_g, m_a_log, m_dt_bias, m_o_norm_g, m_w_branch_a, m_w_branch_b, m_w_out, m_norm2_g, m_w_ffn_in, m_w_ffn_out, m_final_g, v_ada_w, v_ada_b, v_norm1_g, v_w_in, v_conv_w, v_spatial_w, v_spatial_b, v_v_norm_g, v_a_log, v_dt_bias, v_o_norm_g, v_w_branch_a, v_w_branch_b, v_w_out, v_norm2_g, v_w_ffn_in, v_w_ffn_out, v_final_g):
    xb, tgt = x[0], loss_target[0]
    T, D = xb.shape
    L, H, G = ada_w.shape[0], a_log.shape[1], spatial_w.shape[1]
    F = 4 * w_ffn_out.shape[1]
    N = T // _BC
    Ws = ada_w.shape[2]
    Wc = w_in.shape[2]
    PW = 8 * D + _LANE
    ix, iy, ic = lax.axis_index("x"), lax.axis_index("y"), lax.axis_index("c")
    me = 4 * ix + 2 * iy + ic

    c_all = _xchg([c], _ALL8, _slot_all8, 8, True, "gather_c")[0].reshape(8, D)
    modp = _ada_fwd(c_all, ada_w, "ada_fwd")
    n_mod, n_cw = L * 8 * Ws, L * _KC * conv_w.shape[2]
    pad = (-(n_mod + n_cw)) % _LANE
    pay = jnp.concatenate([modp.reshape(-1), conv_w.reshape(-1), jnp.zeros((pad,), _F32)]).reshape(-1, _LANE)
    pay_all = _xchg([pay], _ALL8, _slot_all8, 8, True, "gather_mod")[0].reshape(8, -1)
    mod_full = jnp.concatenate([pay_all[2 * k, :n_mod].reshape(L, 8, Ws) for k in range(4)], axis=-1)
    cw_full = jnp.concatenate([pay_all[2 * k, n_mod:n_mod + n_cw].reshape(L, _KC, -1) for k in range(4)], axis=-1)
    mod = lax.dynamic_index_in_dim(mod_full, me, axis=1, keepdims=False) + ada_b
    mods = [[mod[l, j * D:(j + 1) * D].reshape(1, D) for j in range(6)] for l in range(L)]

    big = [w_in, w_branch_a, w_branch_b, w_out, w_ffn_in, w_ffn_out]
    transposed = [True, False, False, False, True, False]

    def to_rows(a, tr):
        return (jnp.transpose(a, (0, 2, 1)) if tr else a).reshape(-1, D)

    def from_rows(r, tr, per):
        a = r.reshape(L, per, D)
        return jnp.transpose(a, (0, 2, 1)) if tr else a

    pers = [Wc, D // 4, D // 4, D // 4, 2 * F // 4, F // 4]
    roff = [0]
    for p in pers:
        roff.append(roff[-1] + L * p)
    Rp = -(-roff[-1] // (32 * _NCH)) * (32 * _NCH)
    rpad = Rp - roff[-1]

    shard = jnp.concatenate([to_rows(w, tr).astype(_MMT) for w, tr in zip(big, transposed)]
                            + [jnp.zeros((rpad, D), _MMT)], axis=0)
    gw = _gather_halves(shard, "gather_w")

    def full_rows(i):
        g = gw[:, roff[i]:roff[i + 1]].reshape(4, L, pers[i], D)
        return jnp.transpose(g, (1, 0, 2, 3)).reshape(L, 4 * pers[i], D)

    wt_in_f = full_rows(0)
    wt_in_p = jnp.concatenate([wt_in_f[:, :6 * D], wt_in_f[:, 6 * D + 2 * H:], wt_in_f[:, 6 * D:6 * D + 2 * H],
                               jnp.zeros((L, _LANE - 2 * H, D), _MMT)], axis=1)
    w_a, w_b, w_o, wt_fi, w_fo = full_rows(1), full_rows(2), full_rows(3), full_rows(4), full_rows(5)

    sbt = jnp.transpose(spatial_b, (0, 2, 1))
    gv3, go3 = v_norm_g.reshape(L, 1, D), o_norm_g.reshape(L, 1, _GD)
    zpad = jnp.zeros((L, _LANE - 2 * H), _F32)
    alog_row = jnp.concatenate([jnp.zeros((L, H), _F32), a_log, zpad], axis=1).reshape(L, 1, _LANE)
    dtb_row = jnp.concatenate([jnp.zeros((L, H), _F32), dt_bias, zpad], axis=1).reshape(L, 1, _LANE)

    def rows_of(tok):
        return jnp.transpose(tok.reshape(N, _BC, H), (2, 0, 1)).reshape(H, N, 1, _BC)

    def toks_of(rows):
        return jnp.transpose(rows.reshape(H, N, _BC), (1, 2, 0)).reshape(T, H)

    saved = []
    xc = xb
    for l in range(L):
        sh1, sc1, gt1, sh2, sc2, gt2 = mods[l]
        g1, g2 = norm1_g[l].reshape(1, D), norm2_g[l].reshape(1, D)
        h = _norm_mod(xc, g1, sc1, sh1, f"norm1_{l}")
        proj = _mm(h, wt_in_p, f"proj_{l}", li=l, trans_b=True, tn=640)
        ya = _gmlp_fwd(proj, spatial_w, sbt, gv3, l, D, f"gmlp_{l}")
        qn, kn, vs, bg = _conv_fwd(proj, cw_full, alog_row, dtb_row, l, D, f"conv_{l}")
        g_r, b_r = rows_of(bg[:, H:2 * H]), rows_of(bg[:, :H])
        o, s_all, t_all = _gdn_fwd(qn, kn, vs, g_r, b_r, f"gdn_{l}")
        yb = _onorm_fwd(o, proj, go3, l, D, f"onorm_{l}")
        pa = _mm(ya, w_a, f"bra_{l}", li=l)
        pb = _mm(yb, w_b, f"brb_{l}", li=l)
        mg = _merge(pa, pb, proj, D, f"merge_{l}")
        p1 = _mm(mg, w_o, f"wout_{l}", li=l)
        x1 = _resid(xc, p1, gt1, f"res1_{l}")
        h2 = _norm_mod(x1, g2, sc2, sh2, f"norm2_{l}")
        gu = _mm(h2, wt_fi, f"ffin_{l}", li=l, trans_b=True)
        act = _swiglu(gu, f"swiglu_{l}")
        p2 = _mm(act, w_fo, f"ffout_{l}", li=l)
        x2 = _resid(x1, p2, gt2, f"res2_{l}")
        saved.append(dict(x=xc, h=h, proj=proj, ya=ya, yb=yb, qn=qn, kn=kn, vs=vs, g_r=g_r, b_r=b_r, o=o,
                          s_all=s_all, t_all=t_all, pa=pa, pb=pb, mg=mg, p1=p1, x1=x1, h2=h2, gu=gu, act=act, p2=p2))
        xc = x2

    loss11, dx, dfg = _loss_head(xc, tgt, final_g.reshape(1, D), "loss_head")
    loss = lax.psum(loss11[0, 0], ("x", "y", "c"))

    gbig = {k: [None] * L for k in ("w_in", "w_a", "w_b", "w_o", "w_fi", "w_fo")}
    small = {k: [None] * L for k in ("dmod", "n1", "n2", "sw", "sb", "gv", "cw", "al", "dt", "go")}
    for l in reversed(range(L)):
        sv = saved[l]
        sh1, sc1, gt1, sh2, sc2, gt2 = mods[l]
        g1, g2 = norm1_g[l].reshape(1, D), norm2_g[l].reshape(1, D)
        proj = sv["proj"]
        dp2, dgt2 = _resid_bwd(dx, sv["p2"], gt2, f"res2b_{l}")
        da = _mm(dp2, w_fo, f"ffoutb_{l}", li=l, trans_b=True)
        gbig["w_fo"][l] = _mm_tn(sv["act"], dp2, f"ffoutw_{l}")
        dgu = _swiglu_bwd(da, sv["gu"], f"swiglub_{l}")
        dh2 = _mm(dgu, wt_fi, f"ffinb_{l}", li=l)
        gbig["w_fi"][l] = _mm_tn(dgu, sv["h2"], f"ffinw_{l}")
        dx1, dgm2, dsh2 = _norm_mod_bwd(sv["x1"], dh2, dx, g2, sc2, f"norm2b_{l}")
        dp1, dgt1 = _resid_bwd(dx1, sv["p1"], gt1, f"res1b_{l}")
        dmg = _mm(dp1, w_o, f"woutb_{l}", li=l, trans_b=True)
        gbig["w_o"][l] = _mm_tn(sv["mg"], dp1, f"woutw_{l}")
        dpa, dpb, dproj = _merge_bwd(dmg, sv["pa"], sv["pb"], proj, D, f"mergeb_{l}")
        dya = _mm(dpa, w_a, f"brab_{l}", li=l, trans_b=True)
        gbig["w_a"][l] = _mm_tn(sv["ya"], dpa, f"braw_{l}")
        dyb = _mm(dpb, w_b, f"brbb_{l}", li=l, trans_b=True)
        gbig["w_b"][l] = _mm_tn(sv["yb"], dpb, f"brbw_{l}")
        dproj, dsw, dsa, dgv = _gmlp_bwd(proj, dya, spatial_w, sbt, gv3, l, D, dproj, f"gmlpb_{l}")
        do, dproj, dgo = _onorm_bwd(dyb, sv["o"], proj, go3, l, D, dproj, f"onormb_{l}")
        dqn, dkn, dvs, dg_r, db_r = _gdn_bwd(sv["qn"], sv["kn"], sv["vs"], sv["g_r"], sv["b_r"], sv["s_all"],
                                             sv["t_all"], do, f"gdnb_{l}")
        dbg = jnp.concatenate([toks_of(db_r), toks_of(dg_r), jnp.zeros((T, _LANE - 2 * H), _F32)], axis=1)
        dc, dproj, dcw, dal, ddt = _conv_bwd1(proj, dqn, dkn, dvs, dbg, cw_full, alog_row, dtb_row, l, D, dproj,
                                              f"convb_{l}")
        dproj = _conv_bwd2(dc, cw_full, l, dproj, f"convx_{l}")
        dh = _mm(dproj, wt_in_p, f"projb_{l}", li=l, tm=512, tk=1664)
        gbig["w_in"][l] = _mm_tn(dproj, sv["h"], f"projw_{l}", tm=640)
        dx, dgm1, dsh1 = _norm_mod_bwd(sv["x"], dh, dx1, g1, sc1, f"norm1b_{l}")
        small["dmod"][l] = jnp.concatenate([dsh1, dgm1 * g1, dgt1, dsh2, dgm2 * g2, dgt2], axis=1)
        small["n1"][l], small["n2"][l] = dgm1 * (1.0 + sc1), dgm2 * (1.0 + sc2)
        small["sw"][l], small["gv"][l], small["cw"][l], small["go"][l] = dsw, dgv, dcw, dgo
        small["sb"][l] = jnp.transpose(dsa.reshape(_AC, G, _GD).sum(axis=-1))
        small["al"][l], small["dt"][l] = dal[:, H:2 * H], ddt[:, H:2 * H]
    grad_x = dx.reshape(1, T, D)

    names_small = ["dmod", "n1", "n2", "sw", "sb", "gv", "cw", "al", "dt", "go"]
    flat = [jnp.stack(small[k]).reshape(-1) for k in names_small] + [dfg.reshape(-1)]
    sizes = [f.shape[0] for f in flat]
    tot = sum(sizes)
    pad = (-tot) % 1024
    pay = jnp.concatenate(flat + [jnp.zeros((pad,), _F32)]).reshape(-1, 1024)
    sm_all = _xchg([pay], _ALL8, _slot_all8, 8, True, "gather_small")[0].reshape(8, -1)
    offs = [0]
    for s in sizes:
        offs.append(offs[-1] + s)
    part = {k: sm_all[:, offs[i]:offs[i + 1]] for i, k in enumerate(names_small + ["fg"])}
    dmod_all = part["dmod"].reshape(8, L, 6 * D)

    outs = {}

    def update(nm, w, gst, m, v):
        shp = w.shape
        w2 = _as2d(w)
        g, d, mn, vn = _adamw(w2, gst.reshape((gst.shape[0],) + w2.shape), _as2d(m), _as2d(v), f"adamw_{nm}")
        outs[nm] = (g.reshape(shp), d.reshape(shp), mn.reshape(shp), vn.reshape(shp))

    chip = 2 * ix + iy
    dmod_t = jnp.transpose(dmod_all, (1, 0, 2))
    dmod_mine = lax.dynamic_slice_in_dim(dmod_t, chip * Ws, Ws, axis=2)
    g_ada_w = _ada_bwd(jnp.transpose(c_all), dmod_mine, "ada_bwd")
    update("ada_w", ada_w, g_ada_w[None], m_ada_w, v_ada_w)
    update("ada_b", ada_b, dmod_all, m_ada_b, v_ada_b)
    update("norm1_g", norm1_g, part["n1"], m_norm1_g, v_norm1_g)
    update("norm2_g", norm2_g, part["n2"], m_norm2_g, v_norm2_g)
    update("spatial_w", spatial_w, part["sw"], m_spatial_w, v_spatial_w)
    update("spatial_b", spatial_b, part["sb"], m_spatial_b, v_spatial_b)
    update("v_norm_g", v_norm_g, part["gv"], m_v_norm_g, v_v_norm_g)
    update("a_log", a_log, part["al"], m_a_log, v_a_log)
    update("dt_bias", dt_bias, part["dt"], m_dt_bias, v_dt_bias)
    update("o_norm_g", o_norm_g, part["go"], m_o_norm_g, v_o_norm_g)
    update("final_g", final_g, part["fg"], m_final_g, v_final_g)
    cw_cols = conv_w.shape[2]
    dcw_all = part["cw"].reshape(8, L, _KC, 4 * cw_cols)
    update("conv_w", conv_w, lax.dynamic_slice_in_dim(dcw_all, chip * cw_cols, cw_cols, axis=3), m_conv_w, v_conv_w)

    dwt_in_p = jnp.stack(gbig["w_in"])
    dwt_in_f = jnp.concatenate([dwt_in_p[:, :6 * D], dwt_in_p[:, 8 * D:8 * D + 2 * H], dwt_in_p[:, 6 * D:8 * D]], axis=1)
    gfull = [dwt_in_f] + [jnp.stack(gbig[k]) for k in ("w_a", "w_b", "w_o", "w_fi", "w_fo")]

    def chip_rows(g, per):
        return jnp.transpose(g.reshape(L, 4, per, D), (1, 0, 2, 3)).reshape(4, L * per, D)

    send = jnp.concatenate([chip_rows(g, per).astype(_MMT) for g, per in zip(gfull, pers)]
                           + [jnp.zeros((4, rpad, D), _MMT)], axis=1)
    got = _send_half_to_sibling(send, "reduce_cores")
    chipsum = _add_own_half(send, got, ic.astype(jnp.int32).reshape(1), "add_cores")
    parts = _scatter_to_chips(chipsum, "reduce_chips")
    mine = _sum_slots(parts, "add_chips")
    other = _swap_with_sibling(mine, "swap_cores")
    first = ic == 0
    gsum = jnp.concatenate([jnp.where(first, mine, other), jnp.where(first, other, mine)], axis=0)
    big_names = ["w_in", "w_branch_a", "w_branch_b", "w_out", "w_ffn_in", "w_ffn_out"]
    big_m = [m_w_in, m_w_branch_a, m_w_branch_b, m_w_out, m_w_ffn_in, m_w_ffn_out]
    big_v = [v_w_in, v_w_branch_a, v_w_branch_b, v_w_out, v_w_ffn_in, v_w_ffn_out]
    for i, (nm, w, m, v) in enumerate(zip(big_names, big, big_m, big_v)):
        update(nm, w, from_rows(gsum[roff[i]:roff[i + 1]], transposed[i], pers[i])[None], m, v)

    order = ["ada_w", "ada_b", "norm1_g", "w_in", "conv_w", "spatial_w", "spatial_b", "v_norm_g", "a_log", "dt_bias",
             "o_norm_g", "w_branch_a", "w_branch_b", "w_out", "norm2_g", "w_ffn_in", "w_ffn_out", "final_g"]
    return (loss, grad_x, *[outs[n][0] for n in order], *[outs[n][1] for n in order],
            *[outs[n][2] for n in order], *[outs[n][3] for n in order])
```

```python
import functools
import math

import jax
import jax.numpy as jnp
from jax import lax
from jax.experimental import pallas as pl
from jax.experimental.pallas import tpu as pltpu

_F32 = jnp.float32
_BF = jnp.bfloat16
_MMT = jnp.bfloat16
_EPS = 1e-6
_GD = 128
_AC = 128
_BC = 64
_KC = 4
_HB = 8
_NCH = 8
_LANE = 128
_VMEM_LIMIT = 56 * 1024 * 1024

_LR, _B1, _B2, _AEPS, _WD, _STEP = 0.001, 0.9, 0.999, 1e-08, 0.01, 10

_NN = (((1,), (0,)), ((), ()))
_NT = (((1,), (1,)), ((), ()))
_TN = (((0,), (0,)), ((), ()))

_MESH = pl.DeviceIdType.MESH


def _cp(*sem):
    return pltpu.CompilerParams(dimension_semantics=tuple(sem), vmem_limit_bytes=_VMEM_LIMIT)


def _dot(a, b, dn=_NN):
    return lax.dot_general(a.astype(_MMT), b.astype(_MMT), dn, preferred_element_type=_F32)


def _pick(n, target, unit=_LANE):
    if n <= target:
        return n
    best = None
    for t in range(unit, target + 1, unit):
        if n % t == 0:
            best = t
    assert best is not None, (n, target)
    return best


def _sigmoid(x):
    return 1.0 / (1.0 + jnp.exp(-x))


def _silu(x):
    return x * _sigmoid(x)


def _dsilu(x):
    s = _sigmoid(x)
    return s * (1.0 + x * (1.0 - s))


_GK = math.sqrt(2.0 / math.pi)


def _gelu(x):
    return 0.5 * x * (1.0 + jnp.tanh(_GK * (x + 0.044715 * x * x * x)))


def _dgelu(x):
    t = jnp.tanh(_GK * (x + 0.044715 * x * x * x))
    return 0.5 * (1.0 + t) + 0.5 * x * (1.0 - t * t) * _GK * (1.0 + 3.0 * 0.044715 * x * x)


def _softplus(x):
    return jnp.maximum(x, 0.0) + jnp.log(1.0 + jnp.exp(-jnp.abs(x)))


def _rspec(tb, w, cb=0):
    return pl.BlockSpec((tb, w), lambda i: (i, cb))


def _fspec(shape):
    nd = len(shape)
    return pl.BlockSpec(tuple(shape), lambda i: (0,) * nd)


def _lspec(tail, li):
    nd = len(tail)
    return pl.BlockSpec((None,) + tuple(tail), lambda i: (li,) + (0,) * nd)


_ALL8 = [(kx, ky, kc) for kx in (0, 1) for ky in (0, 1) for kc in (0, 1) if (kx, ky, kc) != (0, 0, 0)]


def _slot_all8(x, y, c):
    return 4 * x + 2 * y + c


def _xchg(srcs, flips, slot, nslots, gather, name):
    na, npeer = len(srcs), len(flips)

    def body(*refs):
        src_refs, out_refs = refs[:na], refs[na:2 * na]
        send_sems, recv_sems, local_sems = refs[2 * na:]
        x, y, c = lax.axis_index("x"), lax.axis_index("y"), lax.axis_index("c")
        mine = slot(x, y, c)
        copies = []
        for a in range(na):
            own = src_refs[a] if gather else src_refs[a].at[mine]
            loc = pltpu.make_async_copy(own, out_refs[a].at[mine], local_sems.at[a])
            loc.start()
            copies.append(loc)
            for p, (kx, ky, kc) in enumerate(flips):
                px = 1 - x if kx else x
                py = 1 - y if ky else y
                pc = 1 - c if kc else c
                src = src_refs[a] if gather else src_refs[a].at[slot(px, py, pc)]
                cp = pltpu.make_async_remote_copy(
                    src_ref=src, dst_ref=out_refs[a].at[mine],
                    send_sem=send_sems.at[a * npeer + p], recv_sem=recv_sems.at[a * npeer + p],
                    device_id=(px, py, pc), device_id_type=_MESH)
                cp.start()
                copies.append(cp)
        for cp in copies:
            cp.wait()

    hbm = pl.BlockSpec(memory_space=pl.ANY)
    outs = pl.pallas_call(
        body, name=name,
        out_shape=[jax.ShapeDtypeStruct((nslots,) + (s.shape if gather else s.shape[1:]), s.dtype) for s in srcs],
        in_specs=[hbm] * na, out_specs=[hbm] * na,
        scratch_shapes=[pltpu.SemaphoreType.DMA((na * npeer,)), pltpu.SemaphoreType.DMA((na * npeer,)),
                        pltpu.SemaphoreType.DMA((na,))],
    )(*srcs)
    return list(outs)


def _rcopy(src, dst, ssem, rsem, dev):
    return pltpu.make_async_remote_copy(src_ref=src, dst_ref=dst, send_sem=ssem, recv_sem=rsem,
                                        device_id=dev, device_id_type=_MESH)


def _other_chips(x, y):
    return [(1 - x, y), (x, 1 - y), (1 - x, 1 - y)]


_HBM = pl.BlockSpec(memory_space=pl.ANY)


def _gather_halves(shard, name):
    Rp, W = shard.shape
    Rh = Rp // 2
    rc = Rh // _NCH

    def body(s_ref, o_ref, isend, irecv, dsend, drecv, lsem):
        x, y, c = lax.axis_index("x"), lax.axis_index("y"), lax.axis_index("c")
        chip, sib = 2 * x + y, (x, y, 1 - c)
        peers = _other_chips(x, y)

        def rows(half, q):
            return pl.ds(half * Rh + q * rc, rc)

        loc = pltpu.make_async_copy(s_ref, o_ref.at[chip], lsem)
        loc.start()
        started = []
        for j, (px, py) in enumerate(peers):
            for q in range(_NCH):
                cp = _rcopy(s_ref.at[rows(c, q)], o_ref.at[chip, rows(c, q)], isend.at[j * _NCH + q],
                            irecv.at[j * _NCH + q], (px, py, c))
                cp.start()
                started.append(cp)
        for j, (px, py) in enumerate(peers):
            pchip = 2 * px + py
            for q in range(_NCH):
                blk = o_ref.at[pchip, rows(c, q)]
                _rcopy(blk, blk, isend.at[j * _NCH + q], irecv.at[j * _NCH + q], (px, py, c)).wait_recv()
                fw = _rcopy(blk, blk, dsend.at[j * _NCH + q], drecv.at[j * _NCH + q], sib)
                fw.start()
                started.append(fw)
        for j, (px, py) in enumerate(peers):
            pchip = 2 * px + py
            for q in range(_NCH):
                blk = o_ref.at[pchip, rows(1 - c, q)]
                _rcopy(blk, blk, dsend.at[j * _NCH + q], drecv.at[j * _NCH + q], sib).wait_recv()
        for cp in started:
            cp.wait_send()
        loc.wait()

    n = 3 * _NCH
    return pl.pallas_call(
        body, name=name, out_shape=jax.ShapeDtypeStruct((4, Rp, W), shard.dtype), in_specs=[_HBM], out_specs=_HBM,
        scratch_shapes=[pltpu.SemaphoreType.DMA((n,))] * 4 + [pltpu.SemaphoreType.DMA],
    )(shard)


def _send_half_to_sibling(send, name):
    P, Rp, W = send.shape
    Rh = Rp // 2
    rc = Rh // _NCH

    def body(s_ref, o_ref, ssem, rsem):
        x, y, c = lax.axis_index("x"), lax.axis_index("y"), lax.axis_index("c")
        cps = []
        for k in range(P):
            for q in range(_NCH):
                cp = _rcopy(s_ref.at[k, pl.ds((1 - c) * Rh + q * rc, rc)], o_ref.at[k, pl.ds(q * rc, rc)],
                            ssem.at[k * _NCH + q], rsem.at[k * _NCH + q], (x, y, 1 - c))
                cp.start()
                cps.append(cp)
        for cp in cps:
            cp.wait()

    return pl.pallas_call(
        body, name=name, out_shape=jax.ShapeDtypeStruct((P, Rh, W), send.dtype), in_specs=[_HBM], out_specs=_HBM,
        scratch_shapes=[pltpu.SemaphoreType.DMA((P * _NCH,))] * 2,
    )(send)


def _scatter_to_chips(cs, name):
    P, Rh, W = cs.shape
    rc = Rh // _NCH

    def body(s_ref, o_ref, ssem, rsem, lsem):
        x, y, c = lax.axis_index("x"), lax.axis_index("y"), lax.axis_index("c")
        chip = 2 * x + y
        peers = _other_chips(x, y)
        loc = pltpu.make_async_copy(s_ref.at[chip], o_ref.at[chip], lsem)
        loc.start()
        cps = []
        for j, (px, py) in enumerate(peers):
            for q in range(_NCH):
                r = pl.ds(q * rc, rc)
                cp = _rcopy(s_ref.at[2 * px + py, r], o_ref.at[chip, r], ssem.at[j * _NCH + q], rsem.at[j * _NCH + q],
                            (px, py, c))
                cp.start()
                cps.append(cp)
        for j, (px, py) in enumerate(peers):
            for q in range(_NCH):
                blk = o_ref.at[2 * px + py, pl.ds(q * rc, rc)]
                _rcopy(blk, blk, ssem.at[j * _NCH + q], rsem.at[j * _NCH + q], (px, py, c)).wait_recv()
        for cp in cps:
            cp.wait_send()
        loc.wait()

    return pl.pallas_call(
        body, name=name, out_shape=jax.ShapeDtypeStruct((P, Rh, W), cs.dtype), in_specs=[_HBM], out_specs=_HBM,
        scratch_shapes=[pltpu.SemaphoreType.DMA((3 * _NCH,))] * 2 + [pltpu.SemaphoreType.DMA],
    )(cs)


def _swap_with_sibling(v, name):
    R, W = v.shape
    rc = R // _NCH

    def body(s_ref, o_ref, ssem, rsem):
        x, y, c = lax.axis_index("x"), lax.axis_index("y"), lax.axis_index("c")
        cps = []
        for q in range(_NCH):
            r = pl.ds(q * rc, rc)
            cp = _rcopy(s_ref.at[r], o_ref.at[r], ssem.at[q], rsem.at[q], (x, y, 1 - c))
            cp.start()
            cps.append(cp)
        for cp in cps:
            cp.wait()

    return pl.pallas_call(
        body, name=name, out_shape=jax.ShapeDtypeStruct((R, W), v.dtype), in_specs=[_HBM], out_specs=_HBM,
        scratch_shapes=[pltpu.SemaphoreType.DMA((_NCH,))] * 2,
    )(v)


def _mm(a, b, name, li=None, trans_b=False, out_dtype=_F32, tm=1024, tn=512, tk=2048):
    M, K = a.shape
    bs = b.shape[-2:]
    N = bs[0] if trans_b else bs[1]
    tm, tn, tk = _pick(M, tm, 8), _pick(N, tn), _pick(K, tk)
    nk = K // tk
    lead = () if li is None else (None,)

    def bmap(i, j, k):
        idx = (j, k) if trans_b else (k, j)
        return idx if li is None else (li,) + idx

    def body(a_ref, b_ref, o_ref, acc):
        k = pl.program_id(2)
        part = lax.dot_general(a_ref[...], b_ref[...], _NT if trans_b else _NN, preferred_element_type=_F32)
        if nk == 1:
            o_ref[...] = part.astype(o_ref.dtype)
        else:
            @pl.when(k == 0)
            def _():
                acc[...] = part

            @pl.when(k > 0)
            def _():
                acc[...] += part

            @pl.when(k == nk - 1)
            def _():
                o_ref[...] = acc[...].astype(o_ref.dtype)

    return pl.pallas_call(
        body, name=name, grid=(M // tm, N // tn, nk),
        in_specs=[pl.BlockSpec((tm, tk), lambda i, j, k: (i, k)),
                  pl.BlockSpec(lead + ((tn, tk) if trans_b else (tk, tn)), bmap)],
        out_specs=pl.BlockSpec((tm, tn), lambda i, j, k: (i, j)),
        out_shape=jax.ShapeDtypeStruct((M, N), out_dtype),
        scratch_shapes=[pltpu.VMEM((tm, tn) if nk > 1 else (8, _LANE), _F32)],
        compiler_params=_cp("parallel", "parallel", "arbitrary"),
    )(a, b)


def _mm_tn(a, b, name, tm=512, tn=512):
    T, M = a.shape
    N = b.shape[1]
    tm, tn = _pick(M, tm), _pick(N, tn)

    def body(a_ref, b_ref, o_ref):
        o_ref[...] = lax.dot_general(a_ref[...], b_ref[...], _TN, preferred_element_type=_F32)

    return pl.pallas_call(
        body, name=name, grid=(M // tm, N // tn),
        in_specs=[pl.BlockSpec((T, tm), lambda i, j: (0, i)), pl.BlockSpec((T, tn), lambda i, j: (0, j))],
        out_specs=pl.BlockSpec((tm, tn), lambda i, j: (i, j)),
        out_shape=jax.ShapeDtypeStruct((M, N), _F32),
        compiler_params=_cp("parallel", "parallel"),
    )(a, b)


def _ada_fwd(c_all, ada_w, name):
    L, D, Ws = ada_w.shape
    B = c_all.shape[0]

    def body(c_ref, w_ref, o_ref):
        o_ref[...] = _dot(_silu(c_ref[...]), w_ref[...])

    return pl.pallas_call(
        body, name=name, grid=(L,),
        in_specs=[_fspec((B, D)), pl.BlockSpec((None, D, Ws), lambda l: (l, 0, 0))],
        out_specs=pl.BlockSpec((None, B, Ws), lambda l: (l, 0, 0)),
        out_shape=jax.ShapeDtypeStruct((L, B, Ws), _F32), compiler_params=_cp("parallel"),
    )(c_all, ada_w)


def _ada_bwd(c_all_t, dmod, name):
    D, B = c_all_t.shape
    L, _, Ws = dmod.shape

    def body(c_ref, d_ref, o_ref):
        ct = _silu(c_ref[...])
        d = d_ref[...]
        acc = ct[:, 0:1] * d[0:1, :]
        for b in range(1, B):
            acc = acc + ct[:, b:b + 1] * d[b:b + 1, :]
        o_ref[...] = acc

    return pl.pallas_call(
        body, name=name, grid=(L,),
        in_specs=[_fspec((D, B)), pl.BlockSpec((None, B, Ws), lambda l: (l, 0, 0))],
        out_specs=pl.BlockSpec((None, D, Ws), lambda l: (l, 0, 0)),
        out_shape=jax.ShapeDtypeStruct((L, D, Ws), _F32), compiler_params=_cp("parallel"),
    )(c_all_t, dmod)


def _norm_mod(x, g, sc, sh, name, tb=512):
    T, D = x.shape
    tb = _pick(T, tb, 8)

    def body(x_ref, g_ref, sc_ref, sh_ref, h_ref):
        xv = x_ref[...]
        r = lax.rsqrt(jnp.mean(xv * xv, axis=1, keepdims=True) + _EPS)
        h_ref[...] = (xv * r * (g_ref[...] * (1.0 + sc_ref[...])) + sh_ref[...]).astype(h_ref.dtype)

    return pl.pallas_call(
        body, name=name, grid=(T // tb,),
        in_specs=[_rspec(tb, D), _fspec((1, D)), _fspec((1, D)), _fspec((1, D))],
        out_specs=_rspec(tb, D), out_shape=jax.ShapeDtypeStruct((T, D), _MMT), compiler_params=_cp("parallel"),
    )(x, g, sc, sh)


def _norm_mod_bwd(x, dh, dres, g, sc, name, tb=256):
    T, D = x.shape
    tb = _pick(T, tb, 8)

    def body(x_ref, dh_ref, dr_ref, g_ref, sc_ref, dx_ref, dgm_ref, dsh_ref):
        i = pl.program_id(0)
        xv, dh_ = x_ref[...], dh_ref[...]
        r = lax.rsqrt(jnp.mean(xv * xv, axis=1, keepdims=True) + _EPS)
        xn = xv * r
        dxn = dh_ * (g_ref[...] * (1.0 + sc_ref[...]))
        dx_ref[...] = dr_ref[...] + r * (dxn - xn * jnp.mean(dxn * xn, axis=1, keepdims=True))

        @pl.when(i == 0)
        def _():
            dgm_ref[...] = jnp.zeros_like(dgm_ref)
            dsh_ref[...] = jnp.zeros_like(dsh_ref)

        dgm_ref[...] += jnp.sum(dh_ * xn, axis=0, keepdims=True)
        dsh_ref[...] += jnp.sum(dh_, axis=0, keepdims=True)

    return pl.pallas_call(
        body, name=name, grid=(T // tb,),
        in_specs=[_rspec(tb, D), _rspec(tb, D), _rspec(tb, D), _fspec((1, D)), _fspec((1, D))],
        out_specs=[_rspec(tb, D), _fspec((1, D)), _fspec((1, D))],
        out_shape=[jax.ShapeDtypeStruct((T, D), _F32), jax.ShapeDtypeStruct((1, D), _F32),
                   jax.ShapeDtypeStruct((1, D), _F32)],
        compiler_params=_cp("arbitrary"),
    )(x, dh, dres, g, sc)


def _resid(x, p, gt, name, tb=512):
    T, D = x.shape
    tb = _pick(T, tb, 8)

    def body(x_ref, p_ref, gt_ref, o_ref):
        o_ref[...] = x_ref[...] + gt_ref[...] * p_ref[...]

    return pl.pallas_call(
        body, name=name, grid=(T // tb,), in_specs=[_rspec(tb, D), _rspec(tb, D), _fspec((1, D))],
        out_specs=_rspec(tb, D), out_shape=jax.ShapeDtypeStruct((T, D), _F32), compiler_params=_cp("parallel"),
    )(x, p, gt)


def _resid_bwd(dx, p, gt, name, tb=512):
    T, D = dx.shape
    tb = _pick(T, tb, 8)

    def body(dx_ref, p_ref, gt_ref, dp_ref, dgt_ref):
        i = pl.program_id(0)
        d = dx_ref[...]
        dp_ref[...] = (d * gt_ref[...]).astype(dp_ref.dtype)

        @pl.when(i == 0)
        def _():
            dgt_ref[...] = jnp.zeros_like(dgt_ref)

        dgt_ref[...] += jnp.sum(d * p_ref[...], axis=0, keepdims=True)

    return pl.pallas_call(
        body, name=name, grid=(T // tb,), in_specs=[_rspec(tb, D), _rspec(tb, D), _fspec((1, D))],
        out_specs=[_rspec(tb, D), _fspec((1, D))],
        out_shape=[jax.ShapeDtypeStruct((T, D), _MMT), jax.ShapeDtypeStruct((1, D), _F32)],
        compiler_params=_cp("arbitrary"),
    )(dx, p, gt)


def _gmlp_chunk(u_raw, v_raw, sw_ref, sbt, gv, G):
    u, v = _gelu(u_raw), _gelu(v_raw)
    ii = lax.broadcasted_iota(jnp.int32, (_AC, _AC), 0)
    jj = lax.broadcasted_iota(jnp.int32, (_AC, _AC), 1)
    out = []
    for gi in range(G):
        sl = slice(gi * _GD, (gi + 1) * _GD)
        vg = v[:, sl]
        r = lax.rsqrt(jnp.mean(vg * vg, axis=1, keepdims=True) + _EPS)
        vhat = vg * r
        W = jnp.where(jj <= ii, sw_ref[gi], 0.0)
        s = _dot(W, vhat * gv[:, sl]) + sbt[:, gi:gi + 1]
        out.append((u[:, sl], s, vhat, r, W))
    return out


def _gmlp_fwd(proj, sw, sbt, gv, li, D, name):
    T = proj.shape[0]
    G = D // _GD

    def body(u_ref, v_ref, sw_ref, sbt_ref, gv_ref, y_ref):
        parts = _gmlp_chunk(u_ref[...], v_ref[...], sw_ref, sbt_ref[...], gv_ref[...], G)
        for gi, (u, s, _, _, _) in enumerate(parts):
            y_ref[:, gi * _GD:(gi + 1) * _GD] = (u * s).astype(y_ref.dtype)

    return pl.pallas_call(
        body, name=name, grid=(T // _AC,),
        in_specs=[_rspec(_AC, D, 0), _rspec(_AC, D, 1), _lspec((G, _AC, _AC), li), _lspec((_AC, G), li),
                  _lspec((1, D), li)],
        out_specs=_rspec(_AC, D), out_shape=jax.ShapeDtypeStruct((T, D), _MMT), compiler_params=_cp("parallel"),
    )(proj, proj, sw, sbt, gv)


def _gmlp_bwd(proj, dy, sw, sbt, gv, li, D, into, name):
    T = proj.shape[0]
    G = D // _GD

    def body(u_ref, v_ref, dy_ref, sw_ref, sbt_ref, gv_ref, _, duv_ref, dsw_ref, dsa_ref, dgv_ref):
        i = pl.program_id(0)

        @pl.when(i == 0)
        def _():
            dsw_ref[...] = jnp.zeros_like(dsw_ref)
            dsa_ref[...] = jnp.zeros_like(dsa_ref)
            dgv_ref[...] = jnp.zeros_like(dgv_ref)

        u_raw, v_raw, dy_, gv_ = u_ref[...], v_ref[...], dy_ref[...], gv_ref[...]
        parts = _gmlp_chunk(u_raw, v_raw, sw_ref, sbt_ref[...], gv_, G)
        ii = lax.broadcasted_iota(jnp.int32, (_AC, _AC), 0)
        jj = lax.broadcasted_iota(jnp.int32, (_AC, _AC), 1)
        dgu, dgv = _dgelu(u_raw), _dgelu(v_raw)
        for gi, (u, s, vhat, r, W) in enumerate(parts):
            sl = slice(gi * _GD, (gi + 1) * _GD)
            dyg = dy_[:, sl]
            ds = dyg * u
            vn = vhat * gv_[:, sl]
            dsw_ref[gi] += jnp.where(jj <= ii, _dot(ds, vn, _NT), 0.0)
            dsa_ref[:, sl] += ds
            dvn = _dot(W, ds, _TN)
            dgv_ref[:, sl] += jnp.sum(dvn * vhat, axis=0, keepdims=True)
            dvh = dvn * gv_[:, sl]
            dvg = r * (dvh - vhat * jnp.mean(dvh * vhat, axis=1, keepdims=True))
            duv_ref[:, sl] = (dyg * s * dgu[:, sl]).astype(duv_ref.dtype)
            duv_ref[:, D + gi * _GD:D + (gi + 1) * _GD] = (dvg * dgv[:, sl]).astype(duv_ref.dtype)

    return pl.pallas_call(
        body, name=name, grid=(T // _AC,),
        in_specs=[_rspec(_AC, D, 0), _rspec(_AC, D, 1), _rspec(_AC, D), _lspec((G, _AC, _AC), li),
                  _lspec((_AC, G), li), _lspec((1, D), li), _HBM],
        out_specs=[_rspec(_AC, 2 * D), _fspec((G, _AC, _AC)), _fspec((_AC, D)), _fspec((1, D))],
        out_shape=[jax.ShapeDtypeStruct(into.shape, into.dtype), jax.ShapeDtypeStruct((G, _AC, _AC), _F32),
                   jax.ShapeDtypeStruct((_AC, D), _F32), jax.ShapeDtypeStruct((1, D), _F32)],
        input_output_aliases={6: 0}, compiler_params=_cp("arbitrary"),
    )(proj, proj, dy, sw, sbt, gv, into)


def _conv_taps(halo, cur, first):
    tb = cur.shape[0]
    full = jnp.concatenate([jnp.where(first, 0.0, halo), cur], axis=0)
    return [full[8:] if j == _KC - 1 else pltpu.roll(full, _KC - 1 - j, 0)[8:] for j in range(_KC)]


def _prev_spec(tb, w, cb):
    return pl.BlockSpec((8, w), lambda i: (jnp.maximum(i * (tb // 8) - 1, 0), cb))


def _l2_heads(x, H):
    outs, rs = [], []
    for h in range(H):
        xh = x[:, h * _GD:(h + 1) * _GD]
        r = lax.rsqrt(jnp.sum(xh * xh, axis=1, keepdims=True) + _EPS)
        outs.append(xh * r)
        rs.append(r)
    return outs, rs


def _gate_rows(ba, alog_row, dtb_row, H):
    lane = lax.broadcasted_iota(jnp.int32, ba.shape, 1)
    beta = _sigmoid(ba)
    g = -jnp.exp(alog_row) * _softplus(ba + dtb_row)
    return lane, beta, g


def _conv_fwd(proj, cw, alog_row, dtb_row, li, D, name, tb=256):
    T = proj.shape[0]
    H = D // _GD
    tb = _pick(T, tb, 8)
    bac = (8 * D) // _LANE

    def body(q_ref, k_ref, v_ref, qh_ref, kh_ref, vh_ref, ba_ref, cw_ref, al_ref, dtb_ref,
             qo_ref, ko_ref, vo_ref, bg_ref):
        first = pl.program_id(0) == 0
        cw_ = cw_ref[...]
        for idx, (cur, halo, out) in enumerate(((q_ref, qh_ref, qo_ref), (k_ref, kh_ref, ko_ref),
                                                 (v_ref, vh_ref, vo_ref))):
            taps = _conv_taps(halo[...], cur[...], first)
            w = cw_[:, idx * D:(idx + 1) * D]
            cv = taps[0] * w[0:1, :]
            for j in range(1, _KC):
                cv = cv + taps[j] * w[j:j + 1, :]
            act = _silu(cv)
            if idx < 2:
                outs, _ = _l2_heads(act, H)
                for h in range(H):
                    out[:, h * _GD:(h + 1) * _GD] = outs[h]
            else:
                out[...] = act
        lane, beta, g = _gate_rows(ba_ref[...], al_ref[...], dtb_ref[...], H)
        bg_ref[...] = jnp.where(lane < H, beta, jnp.where(lane < 2 * H, g, 0.0))

    return pl.pallas_call(
        body, name=name, grid=(T // tb,),
        in_specs=[_rspec(tb, D, 2), _rspec(tb, D, 3), _rspec(tb, D, 4),
                  _prev_spec(tb, D, 2), _prev_spec(tb, D, 3), _prev_spec(tb, D, 4),
                  _rspec(tb, _LANE, bac), _lspec((_KC, 3 * D), li), _lspec((1, _LANE), li), _lspec((1, _LANE), li)],
        out_specs=[_rspec(tb, D), _rspec(tb, D), _rspec(tb, D), _rspec(tb, _LANE)],
        out_shape=[jax.ShapeDtypeStruct((T, D), _F32)] * 3 + [jax.ShapeDtypeStruct((T, _LANE), _F32)],
        compiler_params=_cp("parallel"),
    )(proj, proj, proj, proj, proj, proj, proj, cw, alog_row, dtb_row)


def _conv_bwd1(proj, dqn, dkn, dvs, dbg, cw, alog_row, dtb_row, li, D, into, name, tb=256):
    T = proj.shape[0]
    H = D // _GD
    tb = _pick(T, tb, 8)
    bac = (8 * D) // _LANE

    def body(q_ref, k_ref, v_ref, qh_ref, kh_ref, vh_ref, ba_ref, dq_ref, dk_ref, dv_ref, dbg_ref,
             cw_ref, al_ref, dtb_ref, _, dc_ref, dba_ref, dcw_ref, dal_ref, ddt_ref):
        i = pl.program_id(0)
        first = i == 0

        @pl.when(first)
        def _():
            dcw_ref[...] = jnp.zeros_like(dcw_ref)
            dal_ref[...] = jnp.zeros_like(dal_ref)
            ddt_ref[...] = jnp.zeros_like(ddt_ref)

        cw_ = cw_ref[...]
        for idx, (cur, halo, dref) in enumerate(((q_ref, qh_ref, dq_ref), (k_ref, kh_ref, dk_ref),
                                                  (v_ref, vh_ref, dv_ref))):
            taps = _conv_taps(halo[...], cur[...], first)
            w = cw_[:, idx * D:(idx + 1) * D]
            cv = taps[0] * w[0:1, :]
            for j in range(1, _KC):
                cv = cv + taps[j] * w[j:j + 1, :]
            dact = dref[...]
            if idx < 2:
                outs, rs = _l2_heads(_silu(cv), H)
                pieces = []
                for h in range(H):
                    dy = dact[:, h * _GD:(h + 1) * _GD]
                    pieces.append(rs[h] * (dy - outs[h] * jnp.sum(dy * outs[h], axis=1, keepdims=True)))
                dact = jnp.concatenate(pieces, axis=1)
            dcv = dact * _dsilu(cv)
            dc_ref[:, idx * D:(idx + 1) * D] = dcv
            for j in range(_KC):
                dcw_ref[j:j + 1, idx * D:(idx + 1) * D] += jnp.sum(dcv * taps[j], axis=0, keepdims=True)

        ba = ba_ref[...]
        lane, beta, g = _gate_rows(ba, al_ref[...], dtb_ref[...], H)
        dbg_ = dbg_ref[...]
        is_b, is_a = lane < H, jnp.logical_and(lane >= H, lane < 2 * H)
        da = dbg_ * (-jnp.exp(al_ref[...])) * _sigmoid(ba + dtb_ref[...])
        dba_ref[...] = jnp.where(is_b, dbg_ * beta * (1.0 - beta), jnp.where(is_a, da, 0.0)).astype(dba_ref.dtype)
        dal_ref[...] += jnp.sum(jnp.where(is_a, dbg_ * g, 0.0), axis=0, keepdims=True)
        ddt_ref[...] += jnp.sum(jnp.where(is_a, da, 0.0), axis=0, keepdims=True)

    return pl.pallas_call(
        body, name=name, grid=(T // tb,),
        in_specs=[_rspec(tb, D, 2), _rspec(tb, D, 3), _rspec(tb, D, 4),
                  _prev_spec(tb, D, 2), _prev_spec(tb, D, 3), _prev_spec(tb, D, 4),
                  _rspec(tb, _LANE, bac), _rspec(tb, D), _rspec(tb, D), _rspec(tb, D), _rspec(tb, _LANE),
                  _lspec((_KC, 3 * D), li), _lspec((1, _LANE), li), _lspec((1, _LANE), li), _HBM],
        out_specs=[_rspec(tb, 3 * D), _rspec(tb, _LANE, bac), _fspec((_KC, 3 * D)), _fspec((1, _LANE)),
                   _fspec((1, _LANE))],
        out_shape=[jax.ShapeDtypeStruct((T, 3 * D), _F32), jax.ShapeDtypeStruct(into.shape, into.dtype),
                   jax.ShapeDtypeStruct((_KC, 3 * D), _F32), jax.ShapeDtypeStruct((1, _LANE), _F32),
                   jax.ShapeDtypeStruct((1, _LANE), _F32)],
        input_output_aliases={14: 1}, compiler_params=_cp("arbitrary"),
    )(proj, proj, proj, proj, proj, proj, proj, dqn, dkn, dvs, dbg, cw, alog_row, dtb_row, into)


def _conv_bwd2(dc, cw, li, into, name, tb=256):
    T, W3 = dc.shape
    W = W3 // 3
    tb = _pick(T, tb, 8)
    nb8 = T // 8
    nrow = T // tb

    def body(dc_ref, nx_ref, cw_ref, _, o_ref):
        last = pl.program_id(0) == nrow - 1
        full = jnp.concatenate([dc_ref[...], jnp.where(last, 0.0, nx_ref[...])], axis=0)
        w = cw_ref[...]
        acc = full[:tb] * w[_KC - 1:_KC, :]
        for j in range(_KC - 1):
            sh = _KC - 1 - j
            acc = acc + pltpu.roll(full, tb + 8 - sh, 0)[:tb] * w[j:j + 1, :]
        o_ref[...] = acc.astype(o_ref.dtype)

    return pl.pallas_call(
        body, name=name, grid=(nrow, 3),
        in_specs=[pl.BlockSpec((tb, W), lambda i, j: (i, j)),
                  pl.BlockSpec((8, W), lambda i, j: (jnp.minimum((i + 1) * (tb // 8), nb8 - 1), j)),
                  pl.BlockSpec((None, _KC, W), lambda i, j: (li, 0, j)), _HBM],
        out_specs=pl.BlockSpec((tb, W), lambda i, j: (i, 2 + j)),
        out_shape=jax.ShapeDtypeStruct(into.shape, into.dtype), input_output_aliases={3: 0},
        compiler_params=_cp("parallel", "parallel"),
    )(dc, dc, cw, into)


def _split(a):
    hi = a.astype(_BF)
    return hi, (a - hi.astype(_F32)).astype(_BF)


def _dot3(a, b):
    (ah, al), (bh, bl) = a, b
    f = functools.partial(lax.dot_general, dimension_numbers=_NN, preferred_element_type=_F32)
    return f(ah, bh) + f(ah, bl) + f(al, bh)


def _inv_unit_lower(mats):
    C = mats[0].shape[0]
    ii = lax.broadcasted_iota(jnp.int32, (C, C), 0)
    jj = lax.broadcasted_iota(jnp.int32, (C, C), 1)
    xs = [jnp.where(ii == jj, 1.0, 0.0) - a for a in mats]
    ps = list(mats)
    n = 1
    while 2 * n < C:
        sp = [_split(p) for p in ps]
        ps = [_dot3(s, s) for s in sp]
        sp = [_split(p) for p in ps]
        xs = [x + _dot3(_split(x), s) for x, s in zip(xs, sp)]
        n *= 2
    return xs


def _gdn_chunk(q, k, v, g_row, b_row):
    C = q.shape[0]
    ii = lax.broadcasted_iota(jnp.int32, (C, C), 0)
    jj = lax.broadcasted_iota(jnp.int32, (C, C), 1)
    low, strict, eye = jj <= ii, jj < ii, ii == jj
    g_col = jnp.sum(jnp.where(eye, g_row, 0.0), axis=1, keepdims=True)
    b_col = jnp.sum(jnp.where(eye, b_row, 0.0), axis=1, keepdims=True)
    gam_col = jnp.sum(jnp.where(low, g_row, 0.0), axis=1, keepdims=True)
    gam_row = jnp.sum(jnp.where(jj >= ii, g_col, 0.0), axis=0, keepdims=True)
    gam_last = jnp.sum(g_row, axis=1, keepdims=True)
    decay = jnp.where(low, jnp.exp(jnp.where(low, gam_col - gam_row, 0.0)), 0.0)
    eg = jnp.exp(gam_col)
    ekd = jnp.exp(gam_last - gam_col)
    qs = q * (_GD ** -0.5)
    kb = k * b_col
    kk = _dot(kb, k, _NT)
    qkraw = _dot(qs, k, _NT)
    return dict(low=low, strict=strict, eye=eye, ii=ii, jj=jj, b_col=b_col, decay=decay, eg=eg, ekd=ekd,
                gl=jnp.exp(gam_last), qs=qs, kb=kb, kk=kk, qkraw=qkraw,
                A=jnp.where(strict, kk * decay, 0.0), vb=v * b_col, kbg=kb * eg,
                qk=qkraw * decay, q_dec=qs * eg, k_dec=k * ekd)


def _gdn_fwd(qn, kn, vs, g_r, b_r, name):
    T, D = qn.shape
    H, N, C = D // _GD, T // _BC, _BC
    hb = min(_HB, H)

    def body(q_ref, k_ref, v_ref, g_ref, b_ref, o_ref, s_ref, t_ref, S):
        @pl.when(pl.program_id(1) == 0)
        def _():
            S[...] = jnp.zeros_like(S)

        hs = range(hb)
        sls = [slice(hh * _GD, (hh + 1) * _GD) for hh in hs]
        cms = [_gdn_chunk(q_ref[:, sl], k_ref[:, sl], v_ref[:, sl], g_ref[hh], b_ref[hh]) for hh, sl in zip(hs, sls)]
        tms = _inv_unit_lower([cm["A"] for cm in cms])
        us = [_dot(tm, cm["vb"]) for tm, cm in zip(tms, cms)]
        ws = [_dot(tm, cm["kbg"]) for tm, cm in zip(tms, cms)]
        s0s = [S[hh] for hh in hs]
        for hh in hs:
            s_ref[hh] = s0s[hh]
            t_ref[hh] = tms[hh]
        v_news = [u - _dot(w, s0) for u, w, s0 in zip(us, ws, s0s)]
        qss = [_dot(cm["q_dec"], s0) for cm, s0 in zip(cms, s0s)]
        for hh in hs:
            o_ref[:, sls[hh]] = qss[hh] + _dot(cms[hh]["qk"], v_news[hh])
        for hh in hs:
            S[hh] = s0s[hh] * cms[hh]["gl"] + _dot(cms[hh]["k_dec"], v_news[hh], _TN)

    qspec = pl.BlockSpec((C, hb * _GD), lambda h, n: (n, h))
    gspec = pl.BlockSpec((hb, None, 1, C), lambda h, n: (h, n, 0, 0))
    return pl.pallas_call(
        body, name=name, grid=(H // hb, N),
        in_specs=[qspec, qspec, qspec, gspec, gspec],
        out_specs=[qspec, pl.BlockSpec((hb, None, _GD, _GD), lambda h, n: (h, n, 0, 0)),
                   pl.BlockSpec((hb, None, C, C), lambda h, n: (h, n, 0, 0))],
        out_shape=[jax.ShapeDtypeStruct((T, D), _F32), jax.ShapeDtypeStruct((H, N, _GD, _GD), _F32),
                   jax.ShapeDtypeStruct((H, N, C, C), _F32)],
        scratch_shapes=[pltpu.VMEM((hb, _GD, _GD), _F32)],
        compiler_params=_cp("arbitrary", "arbitrary"),
    )(qn, kn, vs, g_r, b_r)


def _gdn_bwd(qn, kn, vs, g_r, b_r, s_all, t_all, do, name):
    T, D = qn.shape
    H, N, C = D // _GD, T // _BC, _BC
    hb = min(_HB, H)

    def body(q_ref, k_ref, v_ref, g_ref, b_ref, s_ref, t_ref, do_ref, dq_ref, dk_ref, dv_ref, dg_ref, db_ref, dS):
        @pl.when(pl.program_id(1) == 0)
        def _():
            dS[...] = jnp.zeros_like(dS)

        hs = range(hb)
        sls = [slice(hh * _GD, (hh + 1) * _GD) for hh in hs]
        ks = [k_ref[:, sl] for sl in sls]
        vs_ = [v_ref[:, sl] for sl in sls]
        cms = [_gdn_chunk(q_ref[:, sl], k, v, g_ref[hh], b_ref[hh]) for hh, sl, k, v in zip(hs, sls, ks, vs_)]
        low, strict, eye, ii, jj = (cms[0][n] for n in ("low", "strict", "eye", "ii", "jj"))
        tms, s0s, dos, ds1s = [t_ref[hh] for hh in hs], [s_ref[hh] for hh in hs], [do_ref[:, sl] for sl in sls], \
            [dS[hh] for hh in hs]
        us = [_dot(tm, cm["vb"]) for tm, cm in zip(tms, cms)]
        ws = [_dot(tm, cm["kbg"]) for tm, cm in zip(tms, cms)]
        v_news = [u - _dot(w, s0) for u, w, s0 in zip(us, ws, s0s)]
        dv_news = [_dot(cm["qk"], do_, _TN) + _dot(cm["k_dec"], ds1) for cm, do_, ds1 in zip(cms, dos, ds1s)]
        dqks = [jnp.where(low, _dot(do_, vn, _NT), 0.0) for do_, vn in zip(dos, v_news)]
        dq_decs = [_dot(do_, s0, _NT) for do_, s0 in zip(dos, s0s)]
        dk_decs = [_dot(vn, ds1, _NT) for vn, ds1 in zip(v_news, ds1s)]
        dgls = [jnp.sum(jnp.sum(ds1 * s0, axis=1, keepdims=True), axis=0, keepdims=True) for ds1, s0 in zip(ds1s, s0s)]
        dws = [-_dot(dvn, s0, _NT) for dvn, s0 in zip(dv_news, s0s)]
        for hh in hs:
            dS[hh] = (_dot(cms[hh]["q_dec"], dos[hh], _TN) + cms[hh]["gl"] * ds1s[hh]
                      - _dot(ws[hh], dv_news[hh], _TN))
        dvbs = [_dot(tm, dvn, _TN) for tm, dvn in zip(tms, dv_news)]
        dkbgs = [_dot(tm, dw, _TN) for tm, dw in zip(tms, dws)]
        dAs = [-jnp.where(strict, _dot(dvb, u, _NT) + _dot(dkbg, w, _NT), 0.0)
               for dvb, u, dkbg, w in zip(dvbs, us, dkbgs, ws)]
        dkks = [dA * cm["decay"] for dA, cm in zip(dAs, cms)]
        dqkraws = [dqk * cm["decay"] for dqk, cm in zip(dqks, cms)]
        Es = [(dA * cm["kk"] + dqk * cm["qkraw"]) * cm["decay"] for dA, dqk, cm in zip(dAs, dqks, cms)]
        dkbs = [_dot(dkk, k) + dkbg * cm["eg"] for dkk, k, dkbg, cm in zip(dkks, ks, dkbgs, cms)]
        dqss = [_dot(dqr, k) + dqd * cm["eg"] for dqr, k, dqd, cm in zip(dqkraws, ks, dq_decs, cms)]
        for hh in hs:
            cm = cms[hh]
            dk_ref[:, sls[hh]] = (_dot(dqkraws[hh], cm["qs"], _TN) + _dot(dkks[hh], cm["kb"], _TN)
                                  + dk_decs[hh] * cm["ekd"] + dkbs[hh] * cm["b_col"])
            dv_ref[:, sls[hh]] = dvbs[hh] * cm["b_col"]
            dq_ref[:, sls[hh]] = dqss[hh] * (_GD ** -0.5)
        for hh in hs:
            cm, k, E = cms[hh], ks[hh], Es[hh]
            eg, ekd = cm["eg"], cm["ekd"]
            dbeta_col = jnp.sum(dvbs[hh] * vs_[hh] + dkbs[hh] * k, axis=1, keepdims=True)
            t_kd = jnp.sum(dk_decs[hh] * k, axis=1, keepdims=True) * ekd
            c1 = (jnp.sum(E, axis=1, keepdims=True) + jnp.sum(dkbgs[hh] * cm["kb"], axis=1, keepdims=True) * eg
                  + jnp.sum(dq_decs[hh] * cm["qs"], axis=1, keepdims=True) * eg - t_kd)
            r1 = jnp.sum(E, axis=0, keepdims=True)
            dgam_last = jnp.sum(t_kd, axis=0, keepdims=True) + dgls[hh] * cm["gl"]
            dgam_col = c1 - jnp.sum(jnp.where(eye, r1, 0.0), axis=1, keepdims=True)
            dg_ref[hh] = jnp.sum(jnp.where(ii >= jj, dgam_col, 0.0), axis=0, keepdims=True) + dgam_last
            db_ref[hh] = jnp.sum(jnp.where(eye, dbeta_col, 0.0), axis=0, keepdims=True)

    qspec = pl.BlockSpec((C, hb * _GD), lambda h, n: (N - 1 - n, h))
    gspec = pl.BlockSpec((hb, None, 1, C), lambda h, n: (h, N - 1 - n, 0, 0))
    return pl.pallas_call(
        body, name=name, grid=(H // hb, N),
        in_specs=[qspec, qspec, qspec, gspec, gspec,
                  pl.BlockSpec((hb, None, _GD, _GD), lambda h, n: (h, N - 1 - n, 0, 0)),
                  pl.BlockSpec((hb, None, C, C), lambda h, n: (h, N - 1 - n, 0, 0)), qspec],
        out_specs=[qspec, qspec, qspec, gspec, gspec],
        out_shape=[jax.ShapeDtypeStruct((T, D), _F32)] * 3 + [jax.ShapeDtypeStruct((H, N, 1, C), _F32)] * 2,
        scratch_shapes=[pltpu.VMEM((hb, _GD, _GD), _F32)],
        compiler_params=_cp("arbitrary", "arbitrary"),
    )(qn, kn, vs, g_r, b_r, s_all, t_all, do)


def _onorm_fwd(o, proj, go, li, D, name, tb=512):
    T = o.shape[0]
    H = D // _GD
    tb = _pick(T, tb, 8)

    def body(o_ref, z_ref, go_ref, y_ref):
        ov, zv, g = o_ref[...], z_ref[...], go_ref[...]
        for h in range(H):
            sl = slice(h * _GD, (h + 1) * _GD)
            oh = ov[:, sl]
            r = lax.rsqrt(jnp.mean(oh * oh, axis=1, keepdims=True) + _EPS)
            y_ref[:, sl] = (oh * r * g * _silu(zv[:, sl])).astype(y_ref.dtype)

    return pl.pallas_call(
        body, name=name, grid=(T // tb,), in_specs=[_rspec(tb, D), _rspec(tb, D, 5), _lspec((1, _GD), li)],
        out_specs=_rspec(tb, D), out_shape=jax.ShapeDtypeStruct((T, D), _MMT), compiler_params=_cp("parallel"),
    )(o, proj, go)


def _onorm_bwd(dy, o, proj, go, li, D, into, name, tb=256):
    T = o.shape[0]
    H = D // _GD
    tb = _pick(T, tb, 8)

    def body(dy_ref, o_ref, z_ref, go_ref, _, do_ref, dz_ref, dgo_ref):
        @pl.when(pl.program_id(0) == 0)
        def _():
            dgo_ref[...] = jnp.zeros_like(dgo_ref)

        dyv, ov, zv, g = dy_ref[...], o_ref[...], z_ref[...], go_ref[...]
        dgo = jnp.zeros((1, _GD), _F32)
        for h in range(H):
            sl = slice(h * _GD, (h + 1) * _GD)
            oh, zh, dyh = ov[:, sl], zv[:, sl], dyv[:, sl]
            r = lax.rsqrt(jnp.mean(oh * oh, axis=1, keepdims=True) + _EPS)
            on = oh * r
            sz = _silu(zh)
            dgo = dgo + jnp.sum(dyh * sz * on, axis=0, keepdims=True)
            don = dyh * sz * g
            do_ref[:, sl] = r * (don - on * jnp.mean(don * on, axis=1, keepdims=True))
            dz_ref[:, sl] = (dyh * on * g * _dsilu(zh)).astype(dz_ref.dtype)
        dgo_ref[...] += dgo

    return pl.pallas_call(
        body, name=name, grid=(T // tb,),
        in_specs=[_rspec(tb, D), _rspec(tb, D), _rspec(tb, D, 5), _lspec((1, _GD), li), _HBM],
        out_specs=[_rspec(tb, D), _rspec(tb, D, 5), _fspec((1, _GD))],
        out_shape=[jax.ShapeDtypeStruct((T, D), _F32), jax.ShapeDtypeStruct(into.shape, into.dtype),
                   jax.ShapeDtypeStruct((1, _GD), _F32)],
        input_output_aliases={4: 1}, compiler_params=_cp("arbitrary"),
    )(dy, o, proj, go, into)


def _merge(pa, pb, proj, D, name, tb=512):
    T = pa.shape[0]
    tb = _pick(T, tb, 8)

    def body(pa_ref, pb_ref, ga_ref, gb_ref, m_ref):
        m_ref[...] = (_sigmoid(ga_ref[...]) * pa_ref[...] + _sigmoid(gb_ref[...]) * pb_ref[...]).astype(m_ref.dtype)

    return pl.pallas_call(
        body, name=name, grid=(T // tb,), in_specs=[_rspec(tb, D), _rspec(tb, D), _rspec(tb, D, 6), _rspec(tb, D, 7)],
        out_specs=_rspec(tb, D), out_shape=jax.ShapeDtypeStruct((T, D), _MMT), compiler_params=_cp("parallel"),
    )(pa, pb, proj, proj)


def _merge_bwd(dm, pa, pb, proj, D, name, tb=256):
    T, PW = proj.shape
    tb = _pick(T, tb, 8)

    def body(dm_ref, pa_ref, pb_ref, ga_ref, gb_ref, dpa_ref, dpb_ref, dg_ref):
        d = dm_ref[...]
        sa, sb = _sigmoid(ga_ref[...]), _sigmoid(gb_ref[...])
        dpa_ref[...] = (d * sa).astype(dpa_ref.dtype)
        dpb_ref[...] = (d * sb).astype(dpb_ref.dtype)
        dg_ref[:, :D] = (d * pa_ref[...] * sa * (1.0 - sa)).astype(dg_ref.dtype)
        dg_ref[:, D:] = (d * pb_ref[...] * sb * (1.0 - sb)).astype(dg_ref.dtype)

    return pl.pallas_call(
        body, name=name, grid=(T // tb,),
        in_specs=[_rspec(tb, D), _rspec(tb, D), _rspec(tb, D), _rspec(tb, D, 6), _rspec(tb, D, 7)],
        out_specs=[_rspec(tb, D), _rspec(tb, D), _rspec(tb, 2 * D, 3)],
        out_shape=[jax.ShapeDtypeStruct((T, D), _MMT)] * 2 + [jax.ShapeDtypeStruct((T, PW), _MMT)],
        compiler_params=_cp("parallel"),
    )(dm, pa, pb, proj, proj)


def _swiglu(gu, name, tb=256):
    T, F2 = gu.shape
    F = F2 // 2
    tb = _pick(T, tb, 8)

    def body(g_ref, u_ref, a_ref):
        a_ref[...] = (_silu(g_ref[...]) * u_ref[...]).astype(a_ref.dtype)

    return pl.pallas_call(
        body, name=name, grid=(T // tb,), in_specs=[_rspec(tb, F, 0), _rspec(tb, F, 1)],
        out_specs=_rspec(tb, F), out_shape=jax.ShapeDtypeStruct((T, F), _MMT), compiler_params=_cp("parallel"),
    )(gu, gu)


def _swiglu_bwd(da, gu, name, tb=256):
    T, F2 = gu.shape
    F = F2 // 2
    tb = _pick(T, tb, 8)

    def body(da_ref, g_ref, u_ref, o_ref):
        d, g = da_ref[...], g_ref[...]
        o_ref[:, :F] = (d * u_ref[...] * _dsilu(g)).astype(o_ref.dtype)
        o_ref[:, F:] = (d * _silu(g)).astype(o_ref.dtype)

    return pl.pallas_call(
        body, name=name, grid=(T // tb,), in_specs=[_rspec(tb, F), _rspec(tb, F, 0), _rspec(tb, F, 1)],
        out_specs=_rspec(tb, F2), out_shape=jax.ShapeDtypeStruct((T, F2), _MMT), compiler_params=_cp("parallel"),
    )(da, gu, gu)


def _loss_head(x, tgt, fg, name, tb=256):
    T, D = x.shape
    tb = _pick(T, tb, 8)

    def body(x_ref, t_ref, fg_ref, loss_ref, dx_ref, dfg_ref):
        @pl.when(pl.program_id(0) == 0)
        def _():
            loss_ref[...] = jnp.zeros_like(loss_ref)
            dfg_ref[...] = jnp.zeros_like(dfg_ref)

        xv, fg_ = x_ref[...], fg_ref[...]
        r = lax.rsqrt(jnp.mean(xv * xv, axis=1, keepdims=True) + _EPS)
        xn = xv * r
        e = xn * fg_ - t_ref[...]
        loss_ref[...] += (0.5 / D) * jnp.sum(jnp.sum(e * e, axis=1, keepdims=True), axis=0, keepdims=True)
        dy = e * (1.0 / D)
        dfg_ref[...] += jnp.sum(dy * xn, axis=0, keepdims=True)
        dxn = dy * fg_
        dx_ref[...] = r * (dxn - xn * jnp.mean(dxn * xn, axis=1, keepdims=True))

    return pl.pallas_call(
        body, name=name, grid=(T // tb,), in_specs=[_rspec(tb, D), _rspec(tb, D), _fspec((1, D))],
        out_specs=[_fspec((1, 1)), _rspec(tb, D), _fspec((1, D))],
        out_shape=[jax.ShapeDtypeStruct((1, 1), _F32), jax.ShapeDtypeStruct((T, D), _F32),
                   jax.ShapeDtypeStruct((1, D), _F32)],
        compiler_params=_cp("arbitrary"),
    )(x, tgt, fg)


def _row_tile(R, W, budget=1 << 20, unit=8):
    if R * W * 4 <= budget or R % unit:
        return R
    best = unit
    for t in range(unit, R + 1, unit):
        if R % t == 0 and t * W * 4 <= budget:
            best = t
    return best


def _add_own_half(send, got, half, name):
    P, Rp, W = send.shape
    Rh = Rp // 2
    tb = _row_tile(Rh, W, 1 << 21, 16)

    def body(h_ref, a_ref, b_ref, o_ref):
        o_ref[...] = (a_ref[...].astype(_F32) + b_ref[...].astype(_F32)).astype(o_ref.dtype)

    return pl.pallas_call(
        body, name=name,
        grid_spec=pltpu.PrefetchScalarGridSpec(
            num_scalar_prefetch=1, grid=(P, Rh // tb),
            in_specs=[pl.BlockSpec((None, None, tb, W), lambda k, i, h: (k, h[0], i, 0)),
                      pl.BlockSpec((None, tb, W), lambda k, i, h: (k, i, 0))],
            out_specs=pl.BlockSpec((None, tb, W), lambda k, i, h: (k, i, 0))),
        out_shape=jax.ShapeDtypeStruct((P, Rh, W), send.dtype), compiler_params=_cp("parallel", "parallel"),
    )(half, send.reshape(P, 2, Rh, W), got)


def _sum_slots(st, name):
    P, R, W = st.shape
    tb = _row_tile(R, W, 1 << 20, 16)

    def body(s_ref, o_ref):
        acc = s_ref[0].astype(_F32)
        for p in range(1, P):
            acc = acc + s_ref[p].astype(_F32)
        o_ref[...] = acc

    return pl.pallas_call(
        body, name=name, grid=(R // tb,), in_specs=[pl.BlockSpec((P, tb, W), lambda i: (0, i, 0))],
        out_specs=_rspec(tb, W), out_shape=jax.ShapeDtypeStruct((R, W), _F32), compiler_params=_cp("parallel"),
    )(st)


def _adamw(w, gst, m, v, name):
    R, W = w.shape
    P = gst.shape[0]
    tb = _row_tile(R, W, 1 << 19)
    c1, c2 = 1.0 - _B1 ** _STEP, 1.0 - _B2 ** _STEP

    def body(w_ref, g_ref, m_ref, v_ref, go_ref, d_ref, mo_ref, vo_ref):
        g = g_ref[0]
        for p in range(1, P):
            g = g + g_ref[p]
        mn = _B1 * m_ref[...] + (1.0 - _B1) * g
        vn = _B2 * v_ref[...] + (1.0 - _B2) * (g * g)
        go_ref[...] = g
        mo_ref[...] = mn
        vo_ref[...] = vn
        d_ref[...] = -_LR * ((mn / c1) / (jnp.sqrt(vn / c2) + _AEPS) + _WD * w_ref[...])

    spec = _rspec(tb, W)
    return pl.pallas_call(
        body, name=name, grid=(R // tb,),
        in_specs=[spec, pl.BlockSpec((P, tb, W), lambda i: (0, i, 0)), spec, spec],
        out_specs=[spec] * 4, out_shape=[jax.ShapeDtypeStruct((R, W), _F32)] * 4, compiler_params=_cp("parallel"),
    )(w, gst, m, v)


def _as2d(a):
    if a.ndim == 1:
        return a.reshape(1, -1)
    return a.reshape(-1, a.shape[-1])


def kernel(x, c, ada_w, ada_b, norm1_g, w_in, conv_w, spatial_w, spatial_b, v_norm_g, a_log, dt_bias, o_norm_g, w_branch_a, w_branch_b, w_out, norm2_g, w_ffn_in, w_ffn_out, final_g, loss_target, m_ada_w, m_ada_b, m_norm1_g, m_w_in, m_conv_w, m_spatial_w, m_spatial_b, m_v_norm_g, m_a_log, m_dt_bias, m_o_norm_g, m_w_branch_a, m_w_branch_b, m_w_out, m_norm2_g, m_w_ffn_in, m_w_ffn_out, m_final_g, v_ada_w, v_ada_b, v_norm1_g, v_w_in, v_conv_w, v_spatial_w, v_spatial_b, v_v_norm_g, v_a_log, v_dt_bias, v_o_norm_g, v_w_branch_a, v_w_branch_b, v_w_out, v_norm2_g, v_w_ffn_in, v_w_ffn_out, v_final_g):
    xb, tgt = x[0], loss_target[0]
    T, D = xb.shape
    L, H, G = ada_w.shape[0], a_log.shape[1], spatial_w.shape[1]
    F = 4 * w_ffn_out.shape[1]
    N = T // _BC
    Ws = ada_w.shape[2]
    Wc = w_in.shape[2]
    PW = 8 * D + _LANE
    ix, iy, ic = lax.axis_index("x"), lax.axis_index("y"), lax.axis_index("c")
    me = 4 * ix + 2 * iy + ic

    c_all = _xchg([c], _ALL8, _slot_all8, 8, True, "gather_c")[0].reshape(8, D)
    modp = _ada_fwd(c_all, ada_w, "ada_fwd")
    n_mod, n_cw = L * 8 * Ws, L * _KC * conv_w.shape[2]
    pad = (-(n_mod + n_cw)) % _LANE
    pay = jnp.concatenate([modp.reshape(-1), conv_w.reshape(-1), jnp.zeros((pad,), _F32)]).reshape(-1, _LANE)
    pay_all = _xchg([pay], _ALL8, _slot_all8, 8, True, "gather_mod")[0].reshape(8, -1)
    mod_full = jnp.concatenate([pay_all[2 * k, :n_mod].reshape(L, 8, Ws) for k in range(4)], axis=-1)
    cw_full = jnp.concatenate([pay_all[2 * k, n_mod:n_mod + n_cw].reshape(L, _KC, -1) for k in range(4)], axis=-1)
    mod = lax.dynamic_index_in_dim(mod_full, me, axis=1, keepdims=False) + ada_b
    mods = [[mod[l, j * D:(j + 1) * D].reshape(1, D) for j in range(6)] for l in range(L)]

    big = [w_in, w_branch_a, w_branch_b, w_out, w_ffn_in, w_ffn_out]
    chip = 2 * ix + iy
    starts = [(k * Wc) // 16 * 16 for k in range(4)]
    Hh = max(-(-((k + 1) * Wc) // 16) * 16 - starts[k] for k in range(4))
    No = max(s + Hh for s in starts)
    my_off = jnp.asarray([k * Wc - starts[k] for k in range(4)], jnp.int32)[chip]
    cuts = sorted(set(starts + [s + Hh for s in starts]))

    pers = [Hh, D // 4, D // 4, D // 4, 2 * F // 4, F // 4]
    roff = [0]
    for p in pers:
        roff.append(roff[-1] + L * p)
    Rp = -(-roff[-1] // (32 * _NCH)) * (32 * _NCH)
    rpad = Rp - roff[-1]

    hull = lax.dynamic_update_slice(jnp.zeros((L, Hh, D), _F32), jnp.transpose(w_in, (0, 2, 1)), (0, my_off, 0))
    shard = jnp.concatenate(
        [hull.reshape(-1, D).astype(_MMT), w_branch_a.reshape(-1, D).astype(_MMT),
         w_branch_b.reshape(-1, D).astype(_MMT), w_out.reshape(-1, D).astype(_MMT),
         jnp.transpose(w_ffn_in, (0, 2, 1)).reshape(-1, D).astype(_MMT), w_ffn_out.reshape(-1, D).astype(_MMT),
         jnp.zeros((rpad, D), _MMT)], axis=0)
    gw = _gather_halves(shard, "gather_w")

    def slab(i, l, k):
        a = roff[i] + l * pers[i]
        return gw[k, a:a + pers[i]]

    def joined(i, l):
        return jnp.concatenate([slab(i, l, k) for k in range(4)], axis=0)

    def orig_rows(hulls, a, b):
        edges = sorted(set([a, b] + [c_ for c_ in cuts if a < c_ < b]))
        out = []
        for lo, hi in zip(edges[:-1], edges[1:]):
            cov = [k for k in range(4) if starts[k] <= lo and hi <= starts[k] + Hh]
            piece = hulls[cov[0]][lo - starts[cov[0]]:hi - starts[cov[0]]]
            for k in cov[1:]:
                piece = piece + hulls[k][lo - starts[k]:hi - starts[k]]
            out.append(piece)
        return out

    wt_in_p = []
    for l in range(L):
        hulls = [slab(0, l, k) for k in range(4)]
        wt_in_p.append(jnp.concatenate(
            orig_rows(hulls, 0, 6 * D) + orig_rows(hulls, 6 * D + 2 * H, 8 * D + 2 * H)
            + orig_rows(hulls, 6 * D, 6 * D + 2 * H) + [jnp.zeros((_LANE - 2 * H, D), _MMT)], axis=0))
    w_a, w_b, w_o, wt_fi, w_fo = ([joined(i, l) for l in range(L)] for i in range(1, 6))

    sbt = jnp.transpose(spatial_b, (0, 2, 1))
    gv3, go3 = v_norm_g.reshape(L, 1, D), o_norm_g.reshape(L, 1, _GD)
    zpad = jnp.zeros((L, _LANE - 2 * H), _F32)
    alog_row = jnp.concatenate([jnp.zeros((L, H), _F32), a_log, zpad], axis=1).reshape(L, 1, _LANE)
    dtb_row = jnp.concatenate([jnp.zeros((L, H), _F32), dt_bias, zpad], axis=1).reshape(L, 1, _LANE)

    def rows_of(tok):
        return jnp.transpose(tok.reshape(N, _BC, H), (2, 0, 1)).reshape(H, N, 1, _BC)

    def toks_of(rows):
        return jnp.transpose(rows.reshape(H, N, _BC), (1, 2, 0)).reshape(T, H)

    saved = []
    xc = xb
    for l in range(L):
        sh1, sc1, gt1, sh2, sc2, gt2 = mods[l]
        g1, g2 = norm1_g[l].reshape(1, D), norm2_g[l].reshape(1, D)
        h = _norm_mod(xc, g1, sc1, sh1, f"norm1_{l}")
        proj = _mm(h, wt_in_p[l], f"proj_{l}", trans_b=True, tn=640)
        ya = _gmlp_fwd(proj, spatial_w, sbt, gv3, l, D, f"gmlp_{l}")
        qn, kn, vs, bg = _conv_fwd(proj, cw_full, alog_row, dtb_row, l, D, f"conv_{l}")
        g_r, b_r = rows_of(bg[:, H:2 * H]), rows_of(bg[:, :H])
        o, s_all, t_all = _gdn_fwd(qn, kn, vs, g_r, b_r, f"gdn_{l}")
        yb = _onorm_fwd(o, proj, go3, l, D, f"onorm_{l}")
        pa = _mm(ya, w_a[l], f"bra_{l}")
        pb = _mm(yb, w_b[l], f"brb_{l}")
        mg = _merge(pa, pb, proj, D, f"merge_{l}")
        p1 = _mm(mg, w_o[l], f"wout_{l}")
        x1 = _resid(xc, p1, gt1, f"res1_{l}")
        h2 = _norm_mod(x1, g2, sc2, sh2, f"norm2_{l}")
        gu = _mm(h2, wt_fi[l], f"ffin_{l}", trans_b=True)
        act = _swiglu(gu, f"swiglu_{l}")
        p2 = _mm(act, w_fo[l], f"ffout_{l}")
        x2 = _resid(x1, p2, gt2, f"res2_{l}")
        saved.append(dict(x=xc, h=h, proj=proj, ya=ya, yb=yb, qn=qn, kn=kn, vs=vs, g_r=g_r, b_r=b_r, o=o,
                          s_all=s_all, t_all=t_all, pa=pa, pb=pb, mg=mg, p1=p1, x1=x1, h2=h2, gu=gu, act=act, p2=p2))
        xc = x2

    loss11, dx, dfg = _loss_head(xc, tgt, final_g.reshape(1, D), "loss_head")
    loss = lax.psum(loss11[0, 0], ("x", "y", "c"))

    gbig = {k: [None] * L for k in ("w_in", "w_a", "w_b", "w_o", "w_fi", "w_fo")}
    small = {k: [None] * L for k in ("dmod", "n1", "n2", "sw", "sb", "gv", "cw", "al", "dt", "go")}
    for l in reversed(range(L)):
        sv = saved[l]
        sh1, sc1, gt1, sh2, sc2, gt2 = mods[l]
        g1, g2 = norm1_g[l].reshape(1, D), norm2_g[l].reshape(1, D)
        proj = sv["proj"]
        dp2, dgt2 = _resid_bwd(dx, sv["p2"], gt2, f"res2b_{l}")
        da = _mm(dp2, w_fo[l], f"ffoutb_{l}", trans_b=True)
        gbig["w_fo"][l] = _mm_tn(sv["act"], dp2, f"ffoutw_{l}")
        dgu = _swiglu_bwd(da, sv["gu"], f"swiglub_{l}")
        dh2 = _mm(dgu, wt_fi[l], f"ffinb_{l}")
        gbig["w_fi"][l] = _mm_tn(dgu, sv["h2"], f"ffinw_{l}")
        dx1, dgm2, dsh2 = _norm_mod_bwd(sv["x1"], dh2, dx, g2, sc2, f"norm2b_{l}")
        dp1, dgt1 = _resid_bwd(dx1, sv["p1"], gt1, f"res1b_{l}")
        dmg = _mm(dp1, w_o[l], f"woutb_{l}", trans_b=True)
        gbig["w_o"][l] = _mm_tn(sv["mg"], dp1, f"woutw_{l}")
        dpa, dpb, dproj = _merge_bwd(dmg, sv["pa"], sv["pb"], proj, D, f"mergeb_{l}")
        dya = _mm(dpa, w_a[l], f"brab_{l}", trans_b=True)
        gbig["w_a"][l] = _mm_tn(sv["ya"], dpa, f"braw_{l}")
        dyb = _mm(dpb, w_b[l], f"brbb_{l}", trans_b=True)
        gbig["w_b"][l] = _mm_tn(sv["yb"], dpb, f"brbw_{l}")
        dproj, dsw, dsa, dgv = _gmlp_bwd(proj, dya, spatial_w, sbt, gv3, l, D, dproj, f"gmlpb_{l}")
        do, dproj, dgo = _onorm_bwd(dyb, sv["o"], proj, go3, l, D, dproj, f"onormb_{l}")
        dqn, dkn, dvs, dg_r, db_r = _gdn_bwd(sv["qn"], sv["kn"], sv["vs"], sv["g_r"], sv["b_r"], sv["s_all"],
                                             sv["t_all"], do, f"gdnb_{l}")
        dbg = jnp.concatenate([toks_of(db_r), toks_of(dg_r), jnp.zeros((T, _LANE - 2 * H), _F32)], axis=1)
        dc, dproj, dcw, dal, ddt = _conv_bwd1(proj, dqn, dkn, dvs, dbg, cw_full, alog_row, dtb_row, l, D, dproj,
                                              f"convb_{l}")
        dproj = _conv_bwd2(dc, cw_full, l, dproj, f"convx_{l}")
        dh = _mm(dproj, wt_in_p[l], f"projb_{l}", tm=512, tk=1664)
        gbig["w_in"][l] = _mm_tn(dproj, sv["h"], f"projw_{l}", tm=640)
        dx, dgm1, dsh1 = _norm_mod_bwd(sv["x"], dh, dx1, g1, sc1, f"norm1b_{l}")
        small["dmod"][l] = jnp.concatenate([dsh1, dgm1 * g1, dgt1, dsh2, dgm2 * g2, dgt2], axis=1)
        small["n1"][l], small["n2"][l] = dgm1 * (1.0 + sc1), dgm2 * (1.0 + sc2)
        small["sw"][l], small["gv"][l], small["cw"][l], small["go"][l] = dsw, dgv, dcw, dgo
        small["sb"][l] = jnp.transpose(dsa.reshape(_AC, G, _GD).sum(axis=-1))
        small["al"][l], small["dt"][l] = dal[:, H:2 * H], ddt[:, H:2 * H]
    grad_x = dx.reshape(1, T, D)

    names_small = ["dmod", "n1", "n2", "sw", "sb", "gv", "cw", "al", "dt", "go"]
    flat = [jnp.stack(small[k]).reshape(-1) for k in names_small] + [dfg.reshape(-1)]
    sizes = [f.shape[0] for f in flat]
    tot = sum(sizes)
    pad = (-tot) % 1024
    pay = jnp.concatenate(flat + [jnp.zeros((pad,), _F32)]).reshape(-1, 1024)
    sm_all = _xchg([pay], _ALL8, _slot_all8, 8, True, "gather_small")[0].reshape(8, -1)
    offs = [0]
    for s in sizes:
        offs.append(offs[-1] + s)
    part = {k: sm_all[:, offs[i]:offs[i + 1]] for i, k in enumerate(names_small + ["fg"])}
    dmod_all = part["dmod"].reshape(8, L, 6 * D)

    outs = {}

    def update(nm, w, gst, m, v):
        shp = w.shape
        w2 = _as2d(w)
        g, d, mn, vn = _adamw(w2, gst.reshape((gst.shape[0],) + w2.shape), _as2d(m), _as2d(v), f"adamw_{nm}")
        outs[nm] = (g.reshape(shp), d.reshape(shp), mn.reshape(shp), vn.reshape(shp))

    chip = 2 * ix + iy
    dmod_t = jnp.transpose(dmod_all, (1, 0, 2))
    dmod_mine = lax.dynamic_slice_in_dim(dmod_t, chip * Ws, Ws, axis=2)
    g_ada_w = _ada_bwd(jnp.transpose(c_all), dmod_mine, "ada_bwd")
    update("ada_w", ada_w, g_ada_w[None], m_ada_w, v_ada_w)
    update("ada_b", ada_b, dmod_all, m_ada_b, v_ada_b)
    update("norm1_g", norm1_g, part["n1"], m_norm1_g, v_norm1_g)
    update("norm2_g", norm2_g, part["n2"], m_norm2_g, v_norm2_g)
    update("spatial_w", spatial_w, part["sw"], m_spatial_w, v_spatial_w)
    update("spatial_b", spatial_b, part["sb"], m_spatial_b, v_spatial_b)
    update("v_norm_g", v_norm_g, part["gv"], m_v_norm_g, v_v_norm_g)
    update("a_log", a_log, part["al"], m_a_log, v_a_log)
    update("dt_bias", dt_bias, part["dt"], m_dt_bias, v_dt_bias)
    update("o_norm_g", o_norm_g, part["go"], m_o_norm_g, v_o_norm_g)
    update("final_g", final_g, part["fg"], m_final_g, v_final_g)
    cw_cols = conv_w.shape[2]
    dcw_all = part["cw"].reshape(8, L, _KC, 4 * cw_cols)
    update("conv_w", conv_w, lax.dynamic_slice_in_dim(dcw_all, chip * cw_cols, cw_cols, axis=3), m_conv_w, v_conv_w)

    def hull_of(p, k):
        a, b = starts[k], starts[k] + Hh
        out = []
        for lo, hi, plo in ((0, 6 * D, 0), (6 * D, 6 * D + 2 * H, 8 * D), (6 * D + 2 * H, 8 * D + 2 * H, 6 * D),
                            (8 * D + 2 * H, No, None)):
            s, e = max(a, lo), min(b, hi)
            if s < e:
                out.append(jnp.zeros((e - s, D), _MMT) if plo is None else p[plo + s - lo:plo + e - lo].astype(_MMT))
        return out

    pieces = []
    for k in range(4):
        for l in range(L):
            pieces += hull_of(gbig["w_in"][l], k)
        for i, nm in enumerate(("w_a", "w_b", "w_o", "w_fi", "w_fo")):
            per = pers[i + 1]
            pieces += [gbig[nm][l][k * per:(k + 1) * per].astype(_MMT) for l in range(L)]
        pieces.append(jnp.zeros((rpad, D), _MMT))
    send = jnp.concatenate(pieces, axis=0).reshape(4, Rp, D)
    got = _send_half_to_sibling(send, "reduce_cores")
    chipsum = _add_own_half(send, got, ic.astype(jnp.int32).reshape(1), "add_cores")
    parts = _scatter_to_chips(chipsum, "reduce_chips")
    mine = _sum_slots(parts, "add_chips")
    other = _swap_with_sibling(mine, "swap_cores")
    first = ic == 0
    gsum = jnp.concatenate([jnp.where(first, mine, other), jnp.where(first, other, mine)], axis=0)
    big_names = ["w_in", "w_branch_a", "w_branch_b", "w_out", "w_ffn_in", "w_ffn_out"]
    big_m = [m_w_in, m_w_branch_a, m_w_branch_b, m_w_out, m_w_ffn_in, m_w_ffn_out]
    big_v = [v_w_in, v_w_branch_a, v_w_branch_b, v_w_out, v_w_ffn_in, v_w_ffn_out]
    for i, (nm, w, m, v) in enumerate(zip(big_names, big, big_m, big_v)):
        g = gsum[roff[i]:roff[i + 1]].reshape(L, pers[i], D)
        if i == 0:
            g = jnp.transpose(lax.dynamic_slice_in_dim(g, my_off, Wc, axis=1), (0, 2, 1))
        elif i == 4:
            g = jnp.transpose(g, (0, 2, 1))
        update(nm, w, g[None], m, v)

    order = ["ada_w", "ada_b", "norm1_g", "w_in", "conv_w", "spatial_w", "spatial_b", "v_norm_g", "a_log", "dt_bias",
             "o_norm_g", "w_branch_a", "w_branch_b", "w_out", "norm2_g", "w_ffn_in", "w_ffn_out", "final_g"]
    return (loss, grad_x, *[outs[n][0] for n in order], *[outs[n][1] for n in order],
            *[outs[n][2] for n in order], *[outs[n][3] for n in order])
```

```python
import functools
import math

import jax
import jax.numpy as jnp
from jax import lax
from jax.experimental import pallas as pl
from jax.experimental.pallas import tpu as pltpu

_F32 = jnp.float32
_BF = jnp.bfloat16
_MMT = jnp.bfloat16
_EPS = 1e-6
_GD = 128
_AC = 128
_BC = 64
_KC = 4
_HB = 8
_NCH = 8
_LANE = 128
_VMEM_LIMIT = 56 * 1024 * 1024

_LR, _B1, _B2, _AEPS, _WD, _STEP = 0.001, 0.9, 0.999, 1e-08, 0.01, 10

_NN = (((1,), (0,)), ((), ()))
_NT = (((1,), (1,)), ((), ()))
_TN = (((0,), (0,)), ((), ()))

_MESH = pl.DeviceIdType.MESH


def _cp(*sem):
    return pltpu.CompilerParams(dimension_semantics=tuple(sem), vmem_limit_bytes=_VMEM_LIMIT)


def _dot(a, b, dn=_NN):
    return lax.dot_general(a.astype(_MMT), b.astype(_MMT), dn, preferred_element_type=_F32)


def _pick(n, target, unit=_LANE):
    if n <= target:
        return n
    best = None
    for t in range(unit, target + 1, unit):
        if n % t == 0:
            best = t
    assert best is not None, (n, target)
    return best


def _sigmoid(x):
    return 1.0 / (1.0 + jnp.exp(-x))


def _silu(x):
    return x * _sigmoid(x)


def _dsilu(x):
    s = _sigmoid(x)
    return s * (1.0 + x * (1.0 - s))


_GK = math.sqrt(2.0 / math.pi)


def _gelu(x):
    return 0.5 * x * (1.0 + jnp.tanh(_GK * (x + 0.044715 * x * x * x)))


def _dgelu(x):
    t = jnp.tanh(_GK * (x + 0.044715 * x * x * x))
    return 0.5 * (1.0 + t) + 0.5 * x * (1.0 - t * t) * _GK * (1.0 + 3.0 * 0.044715 * x * x)


def _softplus(x):
    return jnp.maximum(x, 0.0) + jnp.log(1.0 + jnp.exp(-jnp.abs(x)))


def _rspec(tb, w, cb=0):
    return pl.BlockSpec((tb, w), lambda i: (i, cb))


def _fspec(shape):
    nd = len(shape)
    return pl.BlockSpec(tuple(shape), lambda i: (0,) * nd)


def _lspec(tail, li):
    nd = len(tail)
    return pl.BlockSpec((None,) + tuple(tail), lambda i: (li,) + (0,) * nd)


def _slot_all8(x, y, c):
    return 4 * x + 2 * y + c


def _gather8(v, name):
    R, W = v.shape

    def body(v_ref, o_ref, ssem, rsem, lsem):
        x, y, c = lax.axis_index("x"), lax.axis_index("y"), lax.axis_index("c")
        sib = (x, y, 1 - c)
        chips = _other_chips(x, y)

        def slot(px, py, pc):
            return o_ref.at[_slot_all8(px, py, pc)]

        own = pltpu.make_async_copy(v_ref, slot(x, y, c), lsem)
        own.start()
        started = [_rcopy(v_ref, slot(x, y, c), ssem.at[0], rsem.at[0], sib)]
        started += [_rcopy(v_ref, slot(x, y, c), ssem.at[1 + j], rsem.at[1 + j], (px, py, c))
                    for j, (px, py) in enumerate(chips)]
        for cp in started:
            cp.start()
        for j, (px, py) in enumerate(chips):
            blk = slot(px, py, c)
            _rcopy(blk, blk, ssem.at[1 + j], rsem.at[1 + j], (px, py, c)).wait_recv()
            fw = _rcopy(blk, blk, ssem.at[4 + j], rsem.at[4 + j], sib)
            fw.start()
            started.append(fw)
        blk = slot(x, y, 1 - c)
        _rcopy(blk, blk, ssem.at[0], rsem.at[0], sib).wait_recv()
        for j, (px, py) in enumerate(chips):
            blk = slot(px, py, 1 - c)
            _rcopy(blk, blk, ssem.at[4 + j], rsem.at[4 + j], sib).wait_recv()
        for cp in started:
            cp.wait_send()
        own.wait()

    return pl.pallas_call(
        body, name=name, out_shape=jax.ShapeDtypeStruct((8, R, W), v.dtype), in_specs=[_HBM], out_specs=_HBM,
        scratch_shapes=[pltpu.SemaphoreType.DMA((7,)), pltpu.SemaphoreType.DMA((7,)), pltpu.SemaphoreType.DMA],
    )(v)


def _rcopy(src, dst, ssem, rsem, dev):
    return pltpu.make_async_remote_copy(src_ref=src, dst_ref=dst, send_sem=ssem, recv_sem=rsem,
                                        device_id=dev, device_id_type=_MESH)


def _other_chips(x, y):
    return [(1 - x, y), (x, 1 - y), (1 - x, 1 - y)]


_HBM = pl.BlockSpec(memory_space=pl.ANY)


def _gather_halves(shard, name):
    Rp, W = shard.shape
    Rh = Rp // 2
    rc = Rh // _NCH

    def body(s_ref, o_ref, isend, irecv, dsend, drecv, lsem):
        x, y, c = lax.axis_index("x"), lax.axis_index("y"), lax.axis_index("c")
        chip, sib = 2 * x + y, (x, y, 1 - c)
        peers = _other_chips(x, y)

        def rows(half, q):
            return pl.ds(half * Rh + q * rc, rc)

        locs = []
        for half in range(2):
            for q in range(_NCH):
                lc = pltpu.make_async_copy(s_ref.at[rows(half, q)], o_ref.at[chip, rows(half, q)],
                                           lsem.at[half * _NCH + q])
                lc.start()
                locs.append(lc)
        started = []
        for j, (px, py) in enumerate(peers):
            for q in range(_NCH):
                cp = _rcopy(s_ref.at[rows(c, q)], o_ref.at[chip, rows(c, q)], isend.at[j * _NCH + q],
                            irecv.at[j * _NCH + q], (px, py, c))
                cp.start()
                started.append(cp)
        for j, (px, py) in enumerate(peers):
            pchip = 2 * px + py
            for q in range(_NCH):
                blk = o_ref.at[pchip, rows(c, q)]
                _rcopy(blk, blk, isend.at[j * _NCH + q], irecv.at[j * _NCH + q], (px, py, c)).wait_recv()
                fw = _rcopy(blk, blk, dsend.at[j * _NCH + q], drecv.at[j * _NCH + q], sib)
                fw.start()
                started.append(fw)
        for j, (px, py) in enumerate(peers):
            pchip = 2 * px + py
            for q in range(_NCH):
                blk = o_ref.at[pchip, rows(1 - c, q)]
                _rcopy(blk, blk, dsend.at[j * _NCH + q], drecv.at[j * _NCH + q], sib).wait_recv()
        for cp in started:
            cp.wait_send()
        for lc in locs:
            lc.wait()

    n = 3 * _NCH
    return pl.pallas_call(
        body, name=name, out_shape=jax.ShapeDtypeStruct((4, Rp, W), shard.dtype), in_specs=[_HBM], out_specs=_HBM,
        scratch_shapes=[pltpu.SemaphoreType.DMA((n,))] * 4 + [pltpu.SemaphoreType.DMA((2 * _NCH,))],
    )(shard)


def _send_half_to_sibling(send, name):
    P, Rp, W = send.shape
    Rh = Rp // 2
    rc = Rh // _NCH

    def body(s_ref, o_ref, ssem, rsem):
        x, y, c = lax.axis_index("x"), lax.axis_index("y"), lax.axis_index("c")
        cps = []
        for k in range(P):
            for q in range(_NCH):
                cp = _rcopy(s_ref.at[k, pl.ds((1 - c) * Rh + q * rc, rc)], o_ref.at[k, pl.ds(q * rc, rc)],
                            ssem.at[k * _NCH + q], rsem.at[k * _NCH + q], (x, y, 1 - c))
                cp.start()
                cps.append(cp)
        for cp in cps:
            cp.wait()

    return pl.pallas_call(
        body, name=name, out_shape=jax.ShapeDtypeStruct((P, Rh, W), send.dtype), in_specs=[_HBM], out_specs=_HBM,
        scratch_shapes=[pltpu.SemaphoreType.DMA((P * _NCH,))] * 2,
    )(send)


def _scatter_to_chips(cs, name):
    P, Rh, W = cs.shape
    rc = Rh // _NCH

    def body(s_ref, o_ref, ssem, rsem, lsem):
        x, y, c = lax.axis_index("x"), lax.axis_index("y"), lax.axis_index("c")
        chip = 2 * x + y
        peers = _other_chips(x, y)
        locs = []
        for q in range(_NCH):
            r = pl.ds(q * rc, rc)
            lc = pltpu.make_async_copy(s_ref.at[chip, r], o_ref.at[chip, r], lsem.at[q])
            lc.start()
            locs.append(lc)
        cps = []
        for j, (px, py) in enumerate(peers):
            for q in range(_NCH):
                r = pl.ds(q * rc, rc)
                cp = _rcopy(s_ref.at[2 * px + py, r], o_ref.at[chip, r], ssem.at[j * _NCH + q], rsem.at[j * _NCH + q],
                            (px, py, c))
                cp.start()
                cps.append(cp)
        for j, (px, py) in enumerate(peers):
            for q in range(_NCH):
                blk = o_ref.at[2 * px + py, pl.ds(q * rc, rc)]
                _rcopy(blk, blk, ssem.at[j * _NCH + q], rsem.at[j * _NCH + q], (px, py, c)).wait_recv()
        for cp in cps:
            cp.wait_send()
        for lc in locs:
            lc.wait()

    return pl.pallas_call(
        body, name=name, out_shape=jax.ShapeDtypeStruct((P, Rh, W), cs.dtype), in_specs=[_HBM], out_specs=_HBM,
        scratch_shapes=[pltpu.SemaphoreType.DMA((3 * _NCH,))] * 2 + [pltpu.SemaphoreType.DMA((_NCH,))],
    )(cs)


def _swap_with_sibling(v, name):
    R, W = v.shape
    rc = R // _NCH

    def body(s_ref, o_ref, ssem, rsem):
        x, y, c = lax.axis_index("x"), lax.axis_index("y"), lax.axis_index("c")
        cps = []
        for q in range(_NCH):
            r = pl.ds(q * rc, rc)
            cp = _rcopy(s_ref.at[r], o_ref.at[r], ssem.at[q], rsem.at[q], (x, y, 1 - c))
            cp.start()
            cps.append(cp)
        for cp in cps:
            cp.wait()

    return pl.pallas_call(
        body, name=name, out_shape=jax.ShapeDtypeStruct((R, W), v.dtype), in_specs=[_HBM], out_specs=_HBM,
        scratch_shapes=[pltpu.SemaphoreType.DMA((_NCH,))] * 2,
    )(v)


def _mm(a, b, name, li=None, trans_b=False, out_dtype=_F32, tm=1024, tn=512, tk=2048):
    M, K = a.shape
    bs = b.shape[-2:]
    N = bs[0] if trans_b else bs[1]
    tm, tn, tk = _pick(M, tm, 8), _pick(N, tn), _pick(K, tk)
    nk = K // tk
    lead = () if li is None else (None,)

    def bmap(i, j, k):
        idx = (j, k) if trans_b else (k, j)
        return idx if li is None else (li,) + idx

    def body(a_ref, b_ref, o_ref, acc):
        k = pl.program_id(2)
        part = lax.dot_general(a_ref[...], b_ref[...], _NT if trans_b else _NN, preferred_element_type=_F32)
        if nk == 1:
            o_ref[...] = part.astype(o_ref.dtype)
        else:
            @pl.when(k == 0)
            def _():
                acc[...] = part

            @pl.when(k > 0)
            def _():
                acc[...] += part

            @pl.when(k == nk - 1)
            def _():
                o_ref[...] = acc[...].astype(o_ref.dtype)

    return pl.pallas_call(
        body, name=name, grid=(M // tm, N // tn, nk),
        in_specs=[pl.BlockSpec((tm, tk), lambda i, j, k: (i, k)),
                  pl.BlockSpec(lead + ((tn, tk) if trans_b else (tk, tn)), bmap)],
        out_specs=pl.BlockSpec((tm, tn), lambda i, j, k: (i, j)),
        out_shape=jax.ShapeDtypeStruct((M, N), out_dtype),
        scratch_shapes=[pltpu.VMEM((tm, tn) if nk > 1 else (8, _LANE), _F32)],
        compiler_params=_cp("parallel", "parallel", "arbitrary"),
    )(a, b)


def _mm_tn(a, b, name, tm=512, tn=512):
    T, M = a.shape
    N = b.shape[1]
    tm, tn = _pick(M, tm), _pick(N, tn)

    def body(a_ref, b_ref, o_ref):
        o_ref[...] = lax.dot_general(a_ref[...], b_ref[...], _TN, preferred_element_type=_F32)

    return pl.pallas_call(
        body, name=name, grid=(M // tm, N // tn),
        in_specs=[pl.BlockSpec((T, tm), lambda i, j: (0, i)), pl.BlockSpec((T, tn), lambda i, j: (0, j))],
        out_specs=pl.BlockSpec((tm, tn), lambda i, j: (i, j)),
        out_shape=jax.ShapeDtypeStruct((M, N), _F32),
        compiler_params=_cp("parallel", "parallel"),
    )(a, b)


def _ada_fwd(c_all, ada_w, name):
    L, D, Ws = ada_w.shape
    B = c_all.shape[0]

    def body(c_ref, w_ref, o_ref):
        o_ref[...] = _dot(_silu(c_ref[...]), w_ref[...])

    return pl.pallas_call(
        body, name=name, grid=(L,),
        in_specs=[_fspec((B, D)), pl.BlockSpec((None, D, Ws), lambda l: (l, 0, 0))],
        out_specs=pl.BlockSpec((None, B, Ws), lambda l: (l, 0, 0)),
        out_shape=jax.ShapeDtypeStruct((L, B, Ws), _F32), compiler_params=_cp("parallel"),
    )(c_all, ada_w)


def _ada_bwd(c_all_t, dmod, name):
    D, B = c_all_t.shape
    L, _, Ws = dmod.shape

    def body(c_ref, d_ref, o_ref):
        ct = _silu(c_ref[...])
        d = d_ref[...]
        acc = ct[:, 0:1] * d[0:1, :]
        for b in range(1, B):
            acc = acc + ct[:, b:b + 1] * d[b:b + 1, :]
        o_ref[...] = acc

    return pl.pallas_call(
        body, name=name, grid=(L,),
        in_specs=[_fspec((D, B)), pl.BlockSpec((None, B, Ws), lambda l: (l, 0, 0))],
        out_specs=pl.BlockSpec((None, D, Ws), lambda l: (l, 0, 0)),
        out_shape=jax.ShapeDtypeStruct((L, D, Ws), _F32), compiler_params=_cp("parallel"),
    )(c_all_t, dmod)


def _norm_mod(x, g, sc, sh, name, tb=512):
    T, D = x.shape
    tb = _pick(T, tb, 8)

    def body(x_ref, g_ref, sc_ref, sh_ref, h_ref):
        xv = x_ref[...]
        r = lax.rsqrt(jnp.mean(xv * xv, axis=1, keepdims=True) + _EPS)
        h_ref[...] = (xv * r * (g_ref[...] * (1.0 + sc_ref[...])) + sh_ref[...]).astype(h_ref.dtype)

    return pl.pallas_call(
        body, name=name, grid=(T // tb,),
        in_specs=[_rspec(tb, D), _fspec((1, D)), _fspec((1, D)), _fspec((1, D))],
        out_specs=_rspec(tb, D), out_shape=jax.ShapeDtypeStruct((T, D), _MMT), compiler_params=_cp("parallel"),
    )(x, g, sc, sh)


def _norm_mod_bwd(x, dh, dres, g, sc, name, tb=256):
    T, D = x.shape
    tb = _pick(T, tb, 8)

    def body(x_ref, dh_ref, dr_ref, g_ref, sc_ref, dx_ref, dgm_ref, dsh_ref):
        i = pl.program_id(0)
        xv, dh_ = x_ref[...], dh_ref[...]
        r = lax.rsqrt(jnp.mean(xv * xv, axis=1, keepdims=True) + _EPS)
        xn = xv * r
        dxn = dh_ * (g_ref[...] * (1.0 + sc_ref[...]))
        dx_ref[...] = dr_ref[...] + r * (dxn - xn * jnp.mean(dxn * xn, axis=1, keepdims=True))

        @pl.when(i == 0)
        def _():
            dgm_ref[...] = jnp.zeros_like(dgm_ref)
            dsh_ref[...] = jnp.zeros_like(dsh_ref)

        dgm_ref[...] += jnp.sum(dh_ * xn, axis=0, keepdims=True)
        dsh_ref[...] += jnp.sum(dh_, axis=0, keepdims=True)

    return pl.pallas_call(
        body, name=name, grid=(T // tb,),
        in_specs=[_rspec(tb, D), _rspec(tb, D), _rspec(tb, D), _fspec((1, D)), _fspec((1, D))],
        out_specs=[_rspec(tb, D), _fspec((1, D)), _fspec((1, D))],
        out_shape=[jax.ShapeDtypeStruct((T, D), _F32), jax.ShapeDtypeStruct((1, D), _F32),
                   jax.ShapeDtypeStruct((1, D), _F32)],
        compiler_params=_cp("arbitrary"),
    )(x, dh, dres, g, sc)


def _resid(x, p, gt, name, tb=512):
    T, D = x.shape
    tb = _pick(T, tb, 8)

    def body(x_ref, p_ref, gt_ref, o_ref):
        o_ref[...] = x_ref[...] + gt_ref[...] * p_ref[...]

    return pl.pallas_call(
        body, name=name, grid=(T // tb,), in_specs=[_rspec(tb, D), _rspec(tb, D), _fspec((1, D))],
        out_specs=_rspec(tb, D), out_shape=jax.ShapeDtypeStruct((T, D), _F32), compiler_params=_cp("parallel"),
    )(x, p, gt)


def _resid_bwd(dx, p, gt, name, tb=512):
    T, D = dx.shape
    tb = _pick(T, tb, 8)

    def body(dx_ref, p_ref, gt_ref, dp_ref, dgt_ref):
        i = pl.program_id(0)
        d = dx_ref[...]
        dp_ref[...] = (d * gt_ref[...]).astype(dp_ref.dtype)

        @pl.when(i == 0)
        def _():
            dgt_ref[...] = jnp.zeros_like(dgt_ref)

        dgt_ref[...] += jnp.sum(d * p_ref[...], axis=0, keepdims=True)

    return pl.pallas_call(
        body, name=name, grid=(T // tb,), in_specs=[_rspec(tb, D), _rspec(tb, D), _fspec((1, D))],
        out_specs=[_rspec(tb, D), _fspec((1, D))],
        out_shape=[jax.ShapeDtypeStruct((T, D), _MMT), jax.ShapeDtypeStruct((1, D), _F32)],
        compiler_params=_cp("arbitrary"),
    )(dx, p, gt)


def _gmlp_chunk(u_raw, v_raw, sw_ref, sbt, gv, G):
    u, v = _gelu(u_raw), _gelu(v_raw)
    ii = lax.broadcasted_iota(jnp.int32, (_AC, _AC), 0)
    jj = lax.broadcasted_iota(jnp.int32, (_AC, _AC), 1)
    out = []
    for gi in range(G):
        sl = slice(gi * _GD, (gi + 1) * _GD)
        vg = v[:, sl]
        r = lax.rsqrt(jnp.mean(vg * vg, axis=1, keepdims=True) + _EPS)
        vhat = vg * r
        W = jnp.where(jj <= ii, sw_ref[gi], 0.0)
        s = _dot(W, vhat * gv[:, sl]) + sbt[:, gi:gi + 1]
        out.append((u[:, sl], s, vhat, r, W))
    return out


def _gmlp_fwd(proj, sw, sbt, gv, li, D, name):
    T = proj.shape[0]
    G = D // _GD

    def body(u_ref, v_ref, sw_ref, sbt_ref, gv_ref, y_ref):
        parts = _gmlp_chunk(u_ref[...], v_ref[...], sw_ref, sbt_ref[...], gv_ref[...], G)
        for gi, (u, s, _, _, _) in enumerate(parts):
            y_ref[:, gi * _GD:(gi + 1) * _GD] = (u * s).astype(y_ref.dtype)

    return pl.pallas_call(
        body, name=name, grid=(T // _AC,),
        in_specs=[_rspec(_AC, D, 0), _rspec(_AC, D, 1), _lspec((G, _AC, _AC), li), _lspec((_AC, G), li),
                  _lspec((1, D), li)],
        out_specs=_rspec(_AC, D), out_shape=jax.ShapeDtypeStruct((T, D), _MMT), compiler_params=_cp("parallel"),
    )(proj, proj, sw, sbt, gv)


def _gmlp_bwd(proj, dy, sw, sbt, gv, li, D, into, name):
    T = proj.shape[0]
    G = D // _GD

    def body(u_ref, v_ref, dy_ref, sw_ref, sbt_ref, gv_ref, _, duv_ref, dsw_ref, dsa_ref, dgv_ref):
        i = pl.program_id(0)

        @pl.when(i == 0)
        def _():
            dsw_ref[...] = jnp.zeros_like(dsw_ref)
            dsa_ref[...] = jnp.zeros_like(dsa_ref)
            dgv_ref[...] = jnp.zeros_like(dgv_ref)

        u_raw, v_raw, dy_, gv_ = u_ref[...], v_ref[...], dy_ref[...].astype(_F32), gv_ref[...]
        parts = _gmlp_chunk(u_raw, v_raw, sw_ref, sbt_ref[...], gv_, G)
        ii = lax.broadcasted_iota(jnp.int32, (_AC, _AC), 0)
        jj = lax.broadcasted_iota(jnp.int32, (_AC, _AC), 1)
        dgu, dgv = _dgelu(u_raw), _dgelu(v_raw)
        for gi, (u, s, vhat, r, W) in enumerate(parts):
            sl = slice(gi * _GD, (gi + 1) * _GD)
            dyg = dy_[:, sl]
            ds = dyg * u
            vn = vhat * gv_[:, sl]
            dsw_ref[gi] += jnp.where(jj <= ii, _dot(ds, vn, _NT), 0.0)
            dsa_ref[:, sl] += ds
            dvn = _dot(W, ds, _TN)
            dgv_ref[:, sl] += jnp.sum(dvn * vhat, axis=0, keepdims=True)
            dvh = dvn * gv_[:, sl]
            dvg = r * (dvh - vhat * jnp.mean(dvh * vhat, axis=1, keepdims=True))
            duv_ref[:, sl] = (dyg * s * dgu[:, sl]).astype(duv_ref.dtype)
            duv_ref[:, D + gi * _GD:D + (gi + 1) * _GD] = (dvg * dgv[:, sl]).astype(duv_ref.dtype)

    return pl.pallas_call(
        body, name=name, grid=(T // _AC,),
        in_specs=[_rspec(_AC, D, 0), _rspec(_AC, D, 1), _rspec(_AC, D), _lspec((G, _AC, _AC), li),
                  _lspec((_AC, G), li), _lspec((1, D), li), _HBM],
        out_specs=[_rspec(_AC, 2 * D), _fspec((G, _AC, _AC)), _fspec((_AC, D)), _fspec((1, D))],
        out_shape=[jax.ShapeDtypeStruct(into.shape, into.dtype), jax.ShapeDtypeStruct((G, _AC, _AC), _F32),
                   jax.ShapeDtypeStruct((_AC, D), _F32), jax.ShapeDtypeStruct((1, D), _F32)],
        input_output_aliases={6: 0}, compiler_params=_cp("arbitrary"),
    )(proj, proj, dy, sw, sbt, gv, into)


def _conv_taps(halo, cur, first):
    tb = cur.shape[0]
    full = jnp.concatenate([jnp.where(first, 0.0, halo), cur], axis=0)
    return [full[8:] if j == _KC - 1 else pltpu.roll(full, _KC - 1 - j, 0)[8:] for j in range(_KC)]


def _prev_spec(tb, w, cb):
    return pl.BlockSpec((8, w), lambda i: (jnp.maximum(i * (tb // 8) - 1, 0), cb))


def _l2_heads(x, H):
    outs, rs = [], []
    for h in range(H):
        xh = x[:, h * _GD:(h + 1) * _GD]
        r = lax.rsqrt(jnp.sum(xh * xh, axis=1, keepdims=True) + _EPS)
        outs.append(xh * r)
        rs.append(r)
    return outs, rs


def _gate_rows(ba, alog_row, dtb_row, H):
    lane = lax.broadcasted_iota(jnp.int32, ba.shape, 1)
    beta = _sigmoid(ba)
    g = -jnp.exp(alog_row) * _softplus(ba + dtb_row)
    return lane, beta, g


def _conv_fwd(proj, cw, alog_row, dtb_row, li, D, name, tb=256):
    T = proj.shape[0]
    H = D // _GD
    tb = _pick(T, tb, 8)
    bac = (8 * D) // _LANE

    def body(q_ref, k_ref, v_ref, qh_ref, kh_ref, vh_ref, ba_ref, cw_ref, al_ref, dtb_ref,
             qo_ref, ko_ref, vo_ref, bg_ref):
        first = pl.program_id(0) == 0
        cw_ = cw_ref[...]
        for idx, (cur, halo, out) in enumerate(((q_ref, qh_ref, qo_ref), (k_ref, kh_ref, ko_ref),
                                                 (v_ref, vh_ref, vo_ref))):
            taps = _conv_taps(halo[...], cur[...], first)
            w = cw_[:, idx * D:(idx + 1) * D]
            cv = taps[0] * w[0:1, :]
            for j in range(1, _KC):
                cv = cv + taps[j] * w[j:j + 1, :]
            act = _silu(cv)
            if idx < 2:
                outs, _ = _l2_heads(act, H)
                for h in range(H):
                    out[:, h * _GD:(h + 1) * _GD] = outs[h]
            else:
                out[...] = act
        lane, beta, g = _gate_rows(ba_ref[...], al_ref[...], dtb_ref[...], H)
        bg_ref[...] = jnp.where(lane < H, beta, jnp.where(lane < 2 * H, g, 0.0))

    return pl.pallas_call(
        body, name=name, grid=(T // tb,),
        in_specs=[_rspec(tb, D, 2), _rspec(tb, D, 3), _rspec(tb, D, 4),
                  _prev_spec(tb, D, 2), _prev_spec(tb, D, 3), _prev_spec(tb, D, 4),
                  _rspec(tb, _LANE, bac), _lspec((_KC, 3 * D), li), _lspec((1, _LANE), li), _lspec((1, _LANE), li)],
        out_specs=[_rspec(tb, D), _rspec(tb, D), _rspec(tb, D), _rspec(tb, _LANE)],
        out_shape=[jax.ShapeDtypeStruct((T, D), _F32)] * 3 + [jax.ShapeDtypeStruct((T, _LANE), _F32)],
        compiler_params=_cp("parallel"),
    )(proj, proj, proj, proj, proj, proj, proj, cw, alog_row, dtb_row)


def _conv_bwd1(proj, dqn, dkn, dvs, dbg, cw, alog_row, dtb_row, li, D, into, name, tb=256):
    T = proj.shape[0]
    H = D // _GD
    tb = _pick(T, tb, 8)
    bac = (8 * D) // _LANE

    def body(q_ref, k_ref, v_ref, qh_ref, kh_ref, vh_ref, ba_ref, dq_ref, dk_ref, dv_ref, dbg_ref,
             cw_ref, al_ref, dtb_ref, _, dc_ref, dba_ref, dcw_ref, dal_ref, ddt_ref):
        i = pl.program_id(0)
        first = i == 0

        @pl.when(first)
        def _():
            dcw_ref[...] = jnp.zeros_like(dcw_ref)
            dal_ref[...] = jnp.zeros_like(dal_ref)
            ddt_ref[...] = jnp.zeros_like(ddt_ref)

        cw_ = cw_ref[...]
        for idx, (cur, halo, dref) in enumerate(((q_ref, qh_ref, dq_ref), (k_ref, kh_ref, dk_ref),
                                                  (v_ref, vh_ref, dv_ref))):
            taps = _conv_taps(halo[...], cur[...], first)
            w = cw_[:, idx * D:(idx + 1) * D]
            cv = taps[0] * w[0:1, :]
            for j in range(1, _KC):
                cv = cv + taps[j] * w[j:j + 1, :]
            dact = dref[...]
            if idx < 2:
                outs, rs = _l2_heads(_silu(cv), H)
                pieces = []
                for h in range(H):
                    dy = dact[:, h * _GD:(h + 1) * _GD]
                    pieces.append(rs[h] * (dy - outs[h] * jnp.sum(dy * outs[h], axis=1, keepdims=True)))
                dact = jnp.concatenate(pieces, axis=1)
            dcv = dact * _dsilu(cv)
            dc_ref[:, idx * D:(idx + 1) * D] = dcv
            for j in range(_KC):
                dcw_ref[j:j + 1, idx * D:(idx + 1) * D] += jnp.sum(dcv * taps[j], axis=0, keepdims=True)

        ba = ba_ref[...]
        lane, beta, g = _gate_rows(ba, al_ref[...], dtb_ref[...], H)
        dbg_ = dbg_ref[...]
        is_b, is_a = lane < H, jnp.logical_and(lane >= H, lane < 2 * H)
        da = dbg_ * (-jnp.exp(al_ref[...])) * _sigmoid(ba + dtb_ref[...])
        dba_ref[...] = jnp.where(is_b, dbg_ * beta * (1.0 - beta), jnp.where(is_a, da, 0.0)).astype(dba_ref.dtype)
        dal_ref[...] += jnp.sum(jnp.where(is_a, dbg_ * g, 0.0), axis=0, keepdims=True)
        ddt_ref[...] += jnp.sum(jnp.where(is_a, da, 0.0), axis=0, keepdims=True)

    return pl.pallas_call(
        body, name=name, grid=(T // tb,),
        in_specs=[_rspec(tb, D, 2), _rspec(tb, D, 3), _rspec(tb, D, 4),
                  _prev_spec(tb, D, 2), _prev_spec(tb, D, 3), _prev_spec(tb, D, 4),
                  _rspec(tb, _LANE, bac), _rspec(tb, D), _rspec(tb, D), _rspec(tb, D), _rspec(tb, _LANE),
                  _lspec((_KC, 3 * D), li), _lspec((1, _LANE), li), _lspec((1, _LANE), li), _HBM],
        out_specs=[_rspec(tb, 3 * D), _rspec(tb, _LANE, bac), _fspec((_KC, 3 * D)), _fspec((1, _LANE)),
                   _fspec((1, _LANE))],
        out_shape=[jax.ShapeDtypeStruct((T, 3 * D), _F32), jax.ShapeDtypeStruct(into.shape, into.dtype),
                   jax.ShapeDtypeStruct((_KC, 3 * D), _F32), jax.ShapeDtypeStruct((1, _LANE), _F32),
                   jax.ShapeDtypeStruct((1, _LANE), _F32)],
        input_output_aliases={14: 1}, compiler_params=_cp("arbitrary"),
    )(proj, proj, proj, proj, proj, proj, proj, dqn, dkn, dvs, dbg, cw, alog_row, dtb_row, into)


def _conv_bwd2(dc, cw, li, into, name, tb=256):
    T, W3 = dc.shape
    W = W3 // 3
    tb = _pick(T, tb, 8)
    nb8 = T // 8
    nrow = T // tb

    def body(dc_ref, nx_ref, cw_ref, _, o_ref):
        last = pl.program_id(0) == nrow - 1
        full = jnp.concatenate([dc_ref[...], jnp.where(last, 0.0, nx_ref[...])], axis=0)
        w = cw_ref[...]
        acc = full[:tb] * w[_KC - 1:_KC, :]
        for j in range(_KC - 1):
            sh = _KC - 1 - j
            acc = acc + pltpu.roll(full, tb + 8 - sh, 0)[:tb] * w[j:j + 1, :]
        o_ref[...] = acc.astype(o_ref.dtype)

    return pl.pallas_call(
        body, name=name, grid=(nrow, 3),
        in_specs=[pl.BlockSpec((tb, W), lambda i, j: (i, j)),
                  pl.BlockSpec((8, W), lambda i, j: (jnp.minimum((i + 1) * (tb // 8), nb8 - 1), j)),
                  pl.BlockSpec((None, _KC, W), lambda i, j: (li, 0, j)), _HBM],
        out_specs=pl.BlockSpec((tb, W), lambda i, j: (i, 2 + j)),
        out_shape=jax.ShapeDtypeStruct(into.shape, into.dtype), input_output_aliases={3: 0},
        compiler_params=_cp("parallel", "parallel"),
    )(dc, dc, cw, into)


def _split(a):
    hi = a.astype(_BF)
    return hi, (a - hi.astype(_F32)).astype(_BF)


def _dot3(a, b):
    (ah, al), (bh, bl) = a, b
    f = functools.partial(lax.dot_general, dimension_numbers=_NN, preferred_element_type=_F32)
    return f(ah, bh) + f(ah, bl) + f(al, bh)


def _inv_unit_lower(mats):
    C = mats[0].shape[0]
    ii = lax.broadcasted_iota(jnp.int32, (C, C), 0)
    jj = lax.broadcasted_iota(jnp.int32, (C, C), 1)
    xs = [jnp.where(ii == jj, 1.0, 0.0) - a for a in mats]
    ps = list(mats)
    n = 1
    while 2 * n < C:
        sp = [_split(p) for p in ps]
        ps = [_dot3(s, s) for s in sp]
        sp = [_split(p) for p in ps]
        xs = [x + _dot3(_split(x), s) for x, s in zip(xs, sp)]
        n *= 2
    return xs


def _gdn_chunk(q, k, v, g_row, b_row):
    C = q.shape[0]
    ii = lax.broadcasted_iota(jnp.int32, (C, C), 0)
    jj = lax.broadcasted_iota(jnp.int32, (C, C), 1)
    low, strict, eye = jj <= ii, jj < ii, ii == jj
    g_col = jnp.sum(jnp.where(eye, g_row, 0.0), axis=1, keepdims=True)
    b_col = jnp.sum(jnp.where(eye, b_row, 0.0), axis=1, keepdims=True)
    gam_col = jnp.sum(jnp.where(low, g_row, 0.0), axis=1, keepdims=True)
    gam_row = jnp.sum(jnp.where(jj >= ii, g_col, 0.0), axis=0, keepdims=True)
    gam_last = jnp.sum(g_row, axis=1, keepdims=True)
    decay = jnp.where(low, jnp.exp(jnp.where(low, gam_col - gam_row, 0.0)), 0.0)
    eg = jnp.exp(gam_col)
    ekd = jnp.exp(gam_last - gam_col)
    qs = q * (_GD ** -0.5)
    kb = k * b_col
    kk = _dot(kb, k, _NT)
    qkraw = _dot(qs, k, _NT)
    return dict(low=low, strict=strict, eye=eye, ii=ii, jj=jj, b_col=b_col, decay=decay, eg=eg, ekd=ekd,
                gl=jnp.exp(gam_last), qs=qs, kb=kb, kk=kk, qkraw=qkraw,
                A=jnp.where(strict, kk * decay, 0.0), vb=v * b_col, kbg=kb * eg,
                qk=qkraw * decay, q_dec=qs * eg, k_dec=k * ekd)


def _gdn_fwd(qn, kn, vs, g_r, b_r, name):
    T, D = qn.shape
    H, N, C = D // _GD, T // _BC, _BC
    hb = min(_HB, H)

    def body(q_ref, k_ref, v_ref, g_ref, b_ref, o_ref, s_ref, t_ref, S):
        @pl.when(pl.program_id(1) == 0)
        def _():
            S[...] = jnp.zeros_like(S)

        hs = range(hb)
        sls = [slice(hh * _GD, (hh + 1) * _GD) for hh in hs]
        cms = [_gdn_chunk(q_ref[:, sl], k_ref[:, sl], v_ref[:, sl], g_ref[hh], b_ref[hh]) for hh, sl in zip(hs, sls)]
        tms = _inv_unit_lower([cm["A"] for cm in cms])
        us = [_dot(tm, cm["vb"]) for tm, cm in zip(tms, cms)]
        ws = [_dot(tm, cm["kbg"]) for tm, cm in zip(tms, cms)]
        s0s = [S[hh] for hh in hs]
        for hh in hs:
            s_ref[hh] = s0s[hh]
            t_ref[hh] = tms[hh]
        v_news = [u - _dot(w, s0) for u, w, s0 in zip(us, ws, s0s)]
        qss = [_dot(cm["q_dec"], s0) for cm, s0 in zip(cms, s0s)]
        for hh in hs:
            o_ref[:, sls[hh]] = qss[hh] + _dot(cms[hh]["qk"], v_news[hh])
        for hh in hs:
            S[hh] = s0s[hh] * cms[hh]["gl"] + _dot(cms[hh]["k_dec"], v_news[hh], _TN)

    qspec = pl.BlockSpec((C, hb * _GD), lambda h, n: (n, h))
    gspec = pl.BlockSpec((hb, None, 1, C), lambda h, n: (h, n, 0, 0))
    return pl.pallas_call(
        body, name=name, grid=(H // hb, N),
        in_specs=[qspec, qspec, qspec, gspec, gspec],
        out_specs=[qspec, pl.BlockSpec((hb, None, _GD, _GD), lambda h, n: (h, n, 0, 0)),
                   pl.BlockSpec((hb, None, C, C), lambda h, n: (h, n, 0, 0))],
        out_shape=[jax.ShapeDtypeStruct((T, D), _F32), jax.ShapeDtypeStruct((H, N, _GD, _GD), _F32),
                   jax.ShapeDtypeStruct((H, N, C, C), _F32)],
        scratch_shapes=[pltpu.VMEM((hb, _GD, _GD), _F32)],
        compiler_params=_cp("arbitrary", "arbitrary"),
    )(qn, kn, vs, g_r, b_r)


def _gdn_bwd(qn, kn, vs, g_r, b_r, s_all, t_all, do, name):
    T, D = qn.shape
    H, N, C = D // _GD, T // _BC, _BC
    hb = min(_HB, H)

    def body(q_ref, k_ref, v_ref, g_ref, b_ref, s_ref, t_ref, do_ref, dq_ref, dk_ref, dv_ref, dg_ref, db_ref, dS):
        @pl.when(pl.program_id(1) == 0)
        def _():
            dS[...] = jnp.zeros_like(dS)

        hs = range(hb)
        sls = [slice(hh * _GD, (hh + 1) * _GD) for hh in hs]
        ks = [k_ref[:, sl] for sl in sls]
        vs_ = [v_ref[:, sl] for sl in sls]
        cms = [_gdn_chunk(q_ref[:, sl], k, v, g_ref[hh], b_ref[hh]) for hh, sl, k, v in zip(hs, sls, ks, vs_)]
        low, strict, eye, ii, jj = (cms[0][n] for n in ("low", "strict", "eye", "ii", "jj"))
        tms, s0s, dos, ds1s = [t_ref[hh] for hh in hs], [s_ref[hh] for hh in hs], [do_ref[:, sl] for sl in sls], \
            [dS[hh] for hh in hs]
        us = [_dot(tm, cm["vb"]) for tm, cm in zip(tms, cms)]
        ws = [_dot(tm, cm["kbg"]) for tm, cm in zip(tms, cms)]
        v_news = [u - _dot(w, s0) for u, w, s0 in zip(us, ws, s0s)]
        dv_news = [_dot(cm["qk"], do_, _TN) + _dot(cm["k_dec"], ds1) for cm, do_, ds1 in zip(cms, dos, ds1s)]
        dqks = [jnp.where(low, _dot(do_, vn, _NT), 0.0) for do_, vn in zip(dos, v_news)]
        dq_decs = [_dot(do_, s0, _NT) for do_, s0 in zip(dos, s0s)]
        dk_decs = [_dot(vn, ds1, _NT) for vn, ds1 in zip(v_news, ds1s)]
        dgls = [jnp.sum(jnp.sum(ds1 * s0, axis=1, keepdims=True), axis=0, keepdims=True) for ds1, s0 in zip(ds1s, s0s)]
        dws = [-_dot(dvn, s0, _NT) for dvn, s0 in zip(dv_news, s0s)]
        for hh in hs:
            dS[hh] = (_dot(cms[hh]["q_dec"], dos[hh], _TN) + cms[hh]["gl"] * ds1s[hh]
                      - _dot(ws[hh], dv_news[hh], _TN))
        dvbs = [_dot(tm, dvn, _TN) for tm, dvn in zip(tms, dv_news)]
        dkbgs = [_dot(tm, dw, _TN) for tm, dw in zip(tms, dws)]
        dAs = [-jnp.where(strict, _dot(dvb, u, _NT) + _dot(dkbg, w, _NT), 0.0)
               for dvb, u, dkbg, w in zip(dvbs, us, dkbgs, ws)]
        dkks = [dA * cm["decay"] for dA, cm in zip(dAs, cms)]
        dqkraws = [dqk * cm["decay"] for dqk, cm in zip(dqks, cms)]
        Es = [(dA * cm["kk"] + dqk * cm["qkraw"]) * cm["decay"] for dA, dqk, cm in zip(dAs, dqks, cms)]
        dkbs = [_dot(dkk, k) + dkbg * cm["eg"] for dkk, k, dkbg, cm in zip(dkks, ks, dkbgs, cms)]
        dqss = [_dot(dqr, k) + dqd * cm["eg"] for dqr, k, dqd, cm in zip(dqkraws, ks, dq_decs, cms)]
        for hh in hs:
            cm = cms[hh]
            dk_ref[:, sls[hh]] = (_dot(dqkraws[hh], cm["qs"], _TN) + _dot(dkks[hh], cm["kb"], _TN)
                                  + dk_decs[hh] * cm["ekd"] + dkbs[hh] * cm["b_col"])
            dv_ref[:, sls[hh]] = dvbs[hh] * cm["b_col"]
            dq_ref[:, sls[hh]] = dqss[hh] * (_GD ** -0.5)
        for hh in hs:
            cm, k, E = cms[hh], ks[hh], Es[hh]
            eg, ekd = cm["eg"], cm["ekd"]
            dbeta_col = jnp.sum(dvbs[hh] * vs_[hh] + dkbs[hh] * k, axis=1, keepdims=True)
            t_kd = jnp.sum(dk_decs[hh] * k, axis=1, keepdims=True) * ekd
            c1 = (jnp.sum(E, axis=1, keepdims=True) + jnp.sum(dkbgs[hh] * cm["kb"], axis=1, keepdims=True) * eg
                  + jnp.sum(dq_decs[hh] * cm["qs"], axis=1, keepdims=True) * eg - t_kd)
            r1 = jnp.sum(E, axis=0, keepdims=True)
            dgam_last = jnp.sum(t_kd, axis=0, keepdims=True) + dgls[hh] * cm["gl"]
            dgam_col = c1 - jnp.sum(jnp.where(eye, r1, 0.0), axis=1, keepdims=True)
            dg_ref[hh] = jnp.sum(jnp.where(ii >= jj, dgam_col, 0.0), axis=0, keepdims=True) + dgam_last
            db_ref[hh] = jnp.sum(jnp.where(eye, dbeta_col, 0.0), axis=0, keepdims=True)

    qspec = pl.BlockSpec((C, hb * _GD), lambda h, n: (N - 1 - n, h))
    gspec = pl.BlockSpec((hb, None, 1, C), lambda h, n: (h, N - 1 - n, 0, 0))
    return pl.pallas_call(
        body, name=name, grid=(H // hb, N),
        in_specs=[qspec, qspec, qspec, gspec, gspec,
                  pl.BlockSpec((hb, None, _GD, _GD), lambda h, n: (h, N - 1 - n, 0, 0)),
                  pl.BlockSpec((hb, None, C, C), lambda h, n: (h, N - 1 - n, 0, 0)), qspec],
        out_specs=[qspec, qspec, qspec, gspec, gspec],
        out_shape=[jax.ShapeDtypeStruct((T, D), _F32)] * 3 + [jax.ShapeDtypeStruct((H, N, 1, C), _F32)] * 2,
        scratch_shapes=[pltpu.VMEM((hb, _GD, _GD), _F32)],
        compiler_params=_cp("arbitrary", "arbitrary"),
    )(qn, kn, vs, g_r, b_r, s_all, t_all, do)


def _onorm_fwd(o, proj, go, li, D, name, tb=512):
    T = o.shape[0]
    H = D // _GD
    tb = _pick(T, tb, 8)

    def body(o_ref, z_ref, go_ref, y_ref):
        ov, zv, g = o_ref[...], z_ref[...], go_ref[...]
        for h in range(H):
            sl = slice(h * _GD, (h + 1) * _GD)
            oh = ov[:, sl]
            r = lax.rsqrt(jnp.mean(oh * oh, axis=1, keepdims=True) + _EPS)
            y_ref[:, sl] = (oh * r * g * _silu(zv[:, sl])).astype(y_ref.dtype)

    return pl.pallas_call(
        body, name=name, grid=(T // tb,), in_specs=[_rspec(tb, D), _rspec(tb, D, 5), _lspec((1, _GD), li)],
        out_specs=_rspec(tb, D), out_shape=jax.ShapeDtypeStruct((T, D), _MMT), compiler_params=_cp("parallel"),
    )(o, proj, go)


def _onorm_bwd(dy, o, proj, go, li, D, into, name, tb=256):
    T = o.shape[0]
    H = D // _GD
    tb = _pick(T, tb, 8)

    def body(dy_ref, o_ref, z_ref, go_ref, _, do_ref, dz_ref, dgo_ref):
        @pl.when(pl.program_id(0) == 0)
        def _():
            dgo_ref[...] = jnp.zeros_like(dgo_ref)

        dyv, ov, zv, g = dy_ref[...].astype(_F32), o_ref[...], z_ref[...], go_ref[...]
        dgo = jnp.zeros((1, _GD), _F32)
        for h in range(H):
            sl = slice(h * _GD, (h + 1) * _GD)
            oh, zh, dyh = ov[:, sl], zv[:, sl], dyv[:, sl]
            r = lax.rsqrt(jnp.mean(oh * oh, axis=1, keepdims=True) + _EPS)
            on = oh * r
            sz = _silu(zh)
            dgo = dgo + jnp.sum(dyh * sz * on, axis=0, keepdims=True)
            don = dyh * sz * g
            do_ref[:, sl] = r * (don - on * jnp.mean(don * on, axis=1, keepdims=True))
            dz_ref[:, sl] = (dyh * on * g * _dsilu(zh)).astype(dz_ref.dtype)
        dgo_ref[...] += dgo

    return pl.pallas_call(
        body, name=name, grid=(T // tb,),
        in_specs=[_rspec(tb, D), _rspec(tb, D), _rspec(tb, D, 5), _lspec((1, _GD), li), _HBM],
        out_specs=[_rspec(tb, D), _rspec(tb, D, 5), _fspec((1, _GD))],
        out_shape=[jax.ShapeDtypeStruct((T, D), _F32), jax.ShapeDtypeStruct(into.shape, into.dtype),
                   jax.ShapeDtypeStruct((1, _GD), _F32)],
        input_output_aliases={4: 1}, compiler_params=_cp("arbitrary"),
    )(dy, o, proj, go, into)


def _merge(pa, pb, proj, D, name, tb=512):
    T = pa.shape[0]
    tb = _pick(T, tb, 8)

    def body(pa_ref, pb_ref, ga_ref, gb_ref, m_ref):
        m_ref[...] = (_sigmoid(ga_ref[...]) * pa_ref[...].astype(_F32)
                      + _sigmoid(gb_ref[...]) * pb_ref[...].astype(_F32)).astype(m_ref.dtype)

    return pl.pallas_call(
        body, name=name, grid=(T // tb,), in_specs=[_rspec(tb, D), _rspec(tb, D), _rspec(tb, D, 6), _rspec(tb, D, 7)],
        out_specs=_rspec(tb, D), out_shape=jax.ShapeDtypeStruct((T, D), _MMT), compiler_params=_cp("parallel"),
    )(pa, pb, proj, proj)


def _merge_bwd(dm, pa, pb, proj, D, name, tb=256):
    T, PW = proj.shape
    tb = _pick(T, tb, 8)

    def body(dm_ref, pa_ref, pb_ref, ga_ref, gb_ref, dpa_ref, dpb_ref, dg_ref):
        d = dm_ref[...].astype(_F32)
        sa, sb = _sigmoid(ga_ref[...]), _sigmoid(gb_ref[...])
        dpa_ref[...] = (d * sa).astype(dpa_ref.dtype)
        dpb_ref[...] = (d * sb).astype(dpb_ref.dtype)
        dg_ref[:, :D] = (d * pa_ref[...].astype(_F32) * sa * (1.0 - sa)).astype(dg_ref.dtype)
        dg_ref[:, D:] = (d * pb_ref[...].astype(_F32) * sb * (1.0 - sb)).astype(dg_ref.dtype)

    return pl.pallas_call(
        body, name=name, grid=(T // tb,),
        in_specs=[_rspec(tb, D), _rspec(tb, D), _rspec(tb, D), _rspec(tb, D, 6), _rspec(tb, D, 7)],
        out_specs=[_rspec(tb, D), _rspec(tb, D), _rspec(tb, 2 * D, 3)],
        out_shape=[jax.ShapeDtypeStruct((T, D), _MMT)] * 2 + [jax.ShapeDtypeStruct((T, PW), _MMT)],
        compiler_params=_cp("parallel"),
    )(dm, pa, pb, proj, proj)


def _swiglu(gu, name, tb=256):
    T, F2 = gu.shape
    F = F2 // 2
    tb = _pick(T, tb, 8)

    def body(g_ref, u_ref, a_ref):
        a_ref[...] = (_silu(g_ref[...].astype(_F32)) * u_ref[...].astype(_F32)).astype(a_ref.dtype)

    return pl.pallas_call(
        body, name=name, grid=(T // tb,), in_specs=[_rspec(tb, F, 0), _rspec(tb, F, 1)],
        out_specs=_rspec(tb, F), out_shape=jax.ShapeDtypeStruct((T, F), _MMT), compiler_params=_cp("parallel"),
    )(gu, gu)


def _swiglu_bwd(da, gu, name, tb=256):
    T, F2 = gu.shape
    F = F2 // 2
    tb = _pick(T, tb, 8)

    def body(da_ref, g_ref, u_ref, o_ref):
        d, g = da_ref[...].astype(_F32), g_ref[...].astype(_F32)
        o_ref[:, :F] = (d * u_ref[...].astype(_F32) * _dsilu(g)).astype(o_ref.dtype)
        o_ref[:, F:] = (d * _silu(g)).astype(o_ref.dtype)

    return pl.pallas_call(
        body, name=name, grid=(T // tb,), in_specs=[_rspec(tb, F), _rspec(tb, F, 0), _rspec(tb, F, 1)],
        out_specs=_rspec(tb, F2), out_shape=jax.ShapeDtypeStruct((T, F2), _MMT), compiler_params=_cp("parallel"),
    )(da, gu, gu)


def _loss_head(x, tgt, fg, name, tb=256):
    T, D = x.shape
    tb = _pick(T, tb, 8)

    def body(x_ref, t_ref, fg_ref, loss_ref, dx_ref, dfg_ref):
        @pl.when(pl.program_id(0) == 0)
        def _():
            loss_ref[...] = jnp.zeros_like(loss_ref)
            dfg_ref[...] = jnp.zeros_like(dfg_ref)

        xv, fg_ = x_ref[...], fg_ref[...]
        r = lax.rsqrt(jnp.mean(xv * xv, axis=1, keepdims=True) + _EPS)
        xn = xv * r
        e = xn * fg_ - t_ref[...]
        loss_ref[...] += (0.5 / D) * jnp.sum(jnp.sum(e * e, axis=1, keepdims=True), axis=0, keepdims=True)
        dy = e * (1.0 / D)
        dfg_ref[...] += jnp.sum(dy * xn, axis=0, keepdims=True)
        dxn = dy * fg_
        dx_ref[...] = r * (dxn - xn * jnp.mean(dxn * xn, axis=1, keepdims=True))

    return pl.pallas_call(
        body, name=name, grid=(T // tb,), in_specs=[_rspec(tb, D), _rspec(tb, D), _fspec((1, D))],
        out_specs=[_fspec((1, 1)), _rspec(tb, D), _fspec((1, D))],
        out_shape=[jax.ShapeDtypeStruct((1, 1), _F32), jax.ShapeDtypeStruct((T, D), _F32),
                   jax.ShapeDtypeStruct((1, D), _F32)],
        compiler_params=_cp("arbitrary"),
    )(x, tgt, fg)


def _row_tile(R, W, budget=1 << 20, unit=8):
    if R * W * 4 <= budget or R % unit:
        return R
    best = unit
    for t in range(unit, R + 1, unit):
        if R % t == 0 and t * W * 4 <= budget:
            best = t
    return best


def _add_own_half(send, got, half, name):
    P, Rp, W = send.shape
    Rh = Rp // 2
    tb = _row_tile(Rh, W, 1 << 21, 16)

    def body(h_ref, a_ref, b_ref, o_ref):
        o_ref[...] = (a_ref[...].astype(_F32) + b_ref[...].astype(_F32)).astype(o_ref.dtype)

    return pl.pallas_call(
        body, name=name,
        grid_spec=pltpu.PrefetchScalarGridSpec(
            num_scalar_prefetch=1, grid=(P, Rh // tb),
            in_specs=[pl.BlockSpec((None, None, tb, W), lambda k, i, h: (k, h[0], i, 0)),
                      pl.BlockSpec((None, tb, W), lambda k, i, h: (k, i, 0))],
            out_specs=pl.BlockSpec((None, tb, W), lambda k, i, h: (k, i, 0))),
        out_shape=jax.ShapeDtypeStruct((P, Rh, W), send.dtype), compiler_params=_cp("parallel", "parallel"),
    )(half, send.reshape(P, 2, Rh, W), got)


def _sum_slots(st, name):
    P, R, W = st.shape
    tb = _row_tile(R, W, 1 << 20, 16)

    def body(s_ref, o_ref):
        acc = s_ref[0].astype(_F32)
        for p in range(1, P):
            acc = acc + s_ref[p].astype(_F32)
        o_ref[...] = acc

    return pl.pallas_call(
        body, name=name, grid=(R // tb,), in_specs=[pl.BlockSpec((P, tb, W), lambda i: (0, i, 0))],
        out_specs=_rspec(tb, W), out_shape=jax.ShapeDtypeStruct((R, W), _F32), compiler_params=_cp("parallel"),
    )(st)


def _adamw(w, gst, m, v, name):
    R, W = w.shape
    P = gst.shape[0]
    tb = _row_tile(R, W, 1 << 19)
    c1, c2 = 1.0 - _B1 ** _STEP, 1.0 - _B2 ** _STEP

    def body(w_ref, g_ref, m_ref, v_ref, go_ref, d_ref, mo_ref, vo_ref):
        g = g_ref[0]
        for p in range(1, P):
            g = g + g_ref[p]
        mn = _B1 * m_ref[...] + (1.0 - _B1) * g
        vn = _B2 * v_ref[...] + (1.0 - _B2) * (g * g)
        go_ref[...] = g
        mo_ref[...] = mn
        vo_ref[...] = vn
        d_ref[...] = -_LR * ((mn / c1) / (jnp.sqrt(vn / c2) + _AEPS) + _WD * w_ref[...])

    spec = _rspec(tb, W)
    return pl.pallas_call(
        body, name=name, grid=(R // tb,),
        in_specs=[spec, pl.BlockSpec((P, tb, W), lambda i: (0, i, 0)), spec, spec],
        out_specs=[spec] * 4, out_shape=[jax.ShapeDtypeStruct((R, W), _F32)] * 4, compiler_params=_cp("parallel"),
    )(w, gst, m, v)


def _as2d(a):
    if a.ndim == 1:
        return a.reshape(1, -1)
    return a.reshape(-1, a.shape[-1])


def kernel(x, c, ada_w, ada_b, norm1_g, w_in, conv_w, spatial_w, spatial_b, v_norm_g, a_log, dt_bias, o_norm_g, w_branch_a, w_branch_b, w_out, norm2_g, w_ffn_in, w_ffn_out, final_g, loss_target, m_ada_w, m_ada_b, m_norm1_g, m_w_in, m_conv_w, m_spatial_w, m_spatial_b, m_v_norm_g, m_a_log, m_dt_bias, m_o_norm_g, m_w_branch_a, m_w_branch_b, m_w_out, m_norm2_g, m_w_ffn_in, m_w_ffn_out, m_final_g, v_ada_w, v_ada_b, v_norm1_g, v_w_in, v_conv_w, v_spatial_w, v_spatial_b, v_v_norm_g, v_a_log, v_dt_bias, v_o_norm_g, v_w_branch_a, v_w_branch_b, v_w_out, v_norm2_g, v_w_ffn_in, v_w_ffn_out, v_final_g):
    xb, tgt = x[0], loss_target[0]
    T, D = xb.shape
    L, H, G = ada_w.shape[0], a_log.shape[1], spatial_w.shape[1]
    F = 4 * w_ffn_out.shape[1]
    N = T // _BC
    Ws = ada_w.shape[2]
    Wc = w_in.shape[2]
    PW = 8 * D + _LANE
    ix, iy, ic = lax.axis_index("x"), lax.axis_index("y"), lax.axis_index("c")
    me = 4 * ix + 2 * iy + ic

    c_all = _gather8(c, "gather_c").reshape(8, D)
    modp = _ada_fwd(c_all, ada_w, "ada_fwd")
    n_mod, n_cw = L * 8 * Ws, L * _KC * conv_w.shape[2]
    pad = (-(n_mod + n_cw)) % _LANE
    pay = jnp.concatenate([modp.reshape(-1), conv_w.reshape(-1), jnp.zeros((pad,), _F32)]).reshape(-1, _LANE)
    pay_all = _gather8(pay, "gather_mod").reshape(8, -1)
    mod_full = jnp.concatenate([pay_all[2 * k, :n_mod].reshape(L, 8, Ws) for k in range(4)], axis=-1)
    cw_full = jnp.concatenate([pay_all[2 * k, n_mod:n_mod + n_cw].reshape(L, _KC, -1) for k in range(4)], axis=-1)
    mod = lax.dynamic_index_in_dim(mod_full, me, axis=1, keepdims=False) + ada_b
    mods = [[mod[l, j * D:(j + 1) * D].reshape(1, D) for j in range(6)] for l in range(L)]

    big = [w_in, w_branch_a, w_branch_b, w_out, w_ffn_in, w_ffn_out]
    chip = 2 * ix + iy
    starts = [(k * Wc) // 16 * 16 for k in range(4)]
    Hh = max(-(-((k + 1) * Wc) // 16) * 16 - starts[k] for k in range(4))
    No = max(s + Hh for s in starts)
    my_off = jnp.asarray([k * Wc - starts[k] for k in range(4)], jnp.int32)[chip]
    cuts = sorted(set(starts + [s + Hh for s in starts]))

    pers = [Hh, D // 4, D // 4, D // 4, 2 * F // 4, F // 4]
    roff = [0]
    for p in pers:
        roff.append(roff[-1] + L * p)
    Rp = -(-roff[-1] // (32 * _NCH)) * (32 * _NCH)
    rpad = Rp - roff[-1]

    hull = lax.dynamic_update_slice(jnp.zeros((L, Hh, D), _F32), jnp.transpose(w_in, (0, 2, 1)), (0, my_off, 0))
    shard = jnp.concatenate(
        [hull.reshape(-1, D).astype(_MMT), w_branch_a.reshape(-1, D).astype(_MMT),
         w_branch_b.reshape(-1, D).astype(_MMT), w_out.reshape(-1, D).astype(_MMT),
         jnp.transpose(w_ffn_in, (0, 2, 1)).reshape(-1, D).astype(_MMT), w_ffn_out.reshape(-1, D).astype(_MMT),
         jnp.zeros((rpad, D), _MMT)], axis=0)
    gw = _gather_halves(shard, "gather_w")

    def slab(i, l, k):
        a = roff[i] + l * pers[i]
        return gw[k, a:a + pers[i]]

    def joined(i, l):
        return jnp.concatenate([slab(i, l, k) for k in range(4)], axis=0)

    def orig_rows(hulls, a, b):
        edges = sorted(set([a, b] + [c_ for c_ in cuts if a < c_ < b]))
        out = []
        for lo, hi in zip(edges[:-1], edges[1:]):
            cov = [k for k in range(4) if starts[k] <= lo and hi <= starts[k] + Hh]
            piece = hulls[cov[0]][lo - starts[cov[0]]:hi - starts[cov[0]]]
            for k in cov[1:]:
                piece = piece + hulls[k][lo - starts[k]:hi - starts[k]]
            out.append(piece)
        return out

    wt_in_p = []
    for l in range(L):
        hulls = [slab(0, l, k) for k in range(4)]
        wt_in_p.append(jnp.concatenate(
            orig_rows(hulls, 0, 6 * D) + orig_rows(hulls, 6 * D + 2 * H, 8 * D + 2 * H)
            + orig_rows(hulls, 6 * D, 6 * D + 2 * H) + [jnp.zeros((_LANE - 2 * H, D), _MMT)], axis=0))
    w_a, w_b, w_o, wt_fi, w_fo = ([joined(i, l) for l in range(L)] for i in range(1, 6))

    sbt = jnp.transpose(spatial_b, (0, 2, 1))
    gv3, go3 = v_norm_g.reshape(L, 1, D), o_norm_g.reshape(L, 1, _GD)
    zpad = jnp.zeros((L, _LANE - 2 * H), _F32)
    alog_row = jnp.concatenate([jnp.zeros((L, H), _F32), a_log, zpad], axis=1).reshape(L, 1, _LANE)
    dtb_row = jnp.concatenate([jnp.zeros((L, H), _F32), dt_bias, zpad], axis=1).reshape(L, 1, _LANE)

    def rows_of(tok):
        return jnp.transpose(tok.reshape(N, _BC, H), (2, 0, 1)).reshape(H, N, 1, _BC)

    def toks_of(rows):
        return jnp.transpose(rows.reshape(H, N, _BC), (1, 2, 0)).reshape(T, H)

    saved = []
    xc = xb
    for l in range(L):
        sh1, sc1, gt1, sh2, sc2, gt2 = mods[l]
        g1, g2 = norm1_g[l].reshape(1, D), norm2_g[l].reshape(1, D)
        h = _norm_mod(xc, g1, sc1, sh1, f"norm1_{l}")
        proj = _mm(h, wt_in_p[l], f"proj_{l}", trans_b=True, tn=640)
        ya = _gmlp_fwd(proj, spatial_w, sbt, gv3, l, D, f"gmlp_{l}")
        qn, kn, vs, bg = _conv_fwd(proj, cw_full, alog_row, dtb_row, l, D, f"conv_{l}")
        g_r, b_r = rows_of(bg[:, H:2 * H]), rows_of(bg[:, :H])
        o, s_all, t_all = _gdn_fwd(qn, kn, vs, g_r, b_r, f"gdn_{l}")
        yb = _onorm_fwd(o, proj, go3, l, D, f"onorm_{l}")
        pa = _mm(ya, w_a[l], f"bra_{l}", out_dtype=_MMT)
        pb = _mm(yb, w_b[l], f"brb_{l}", out_dtype=_MMT)
        mg = _merge(pa, pb, proj, D, f"merge_{l}")
        p1 = _mm(mg, w_o[l], f"wout_{l}")
        x1 = _resid(xc, p1, gt1, f"res1_{l}")
        h2 = _norm_mod(x1, g2, sc2, sh2, f"norm2_{l}")
        gu = _mm(h2, wt_fi[l], f"ffin_{l}", trans_b=True, out_dtype=_MMT)
        act = _swiglu(gu, f"swiglu_{l}")
        p2 = _mm(act, w_fo[l], f"ffout_{l}")
        x2 = _resid(x1, p2, gt2, f"res2_{l}")
        saved.append(dict(x=xc, h=h, proj=proj, ya=ya, yb=yb, qn=qn, kn=kn, vs=vs, g_r=g_r, b_r=b_r, o=o,
                          s_all=s_all, t_all=t_all, pa=pa, pb=pb, mg=mg, p1=p1, x1=x1, h2=h2, gu=gu, act=act, p2=p2))
        xc = x2

    loss11, dx, dfg = _loss_head(xc, tgt, final_g.reshape(1, D), "loss_head")
    loss = lax.psum(loss11[0, 0], ("x", "y", "c"))

    gbig = {k: [None] * L for k in ("w_in", "w_a", "w_b", "w_o", "w_fi", "w_fo")}
    small = {k: [None] * L for k in ("dmod", "n1", "n2", "sw", "sb", "gv", "cw", "al", "dt", "go")}
    for l in reversed(range(L)):
        sv = saved[l]
        sh1, sc1, gt1, sh2, sc2, gt2 = mods[l]
        g1, g2 = norm1_g[l].reshape(1, D), norm2_g[l].reshape(1, D)
        proj = sv["proj"]
        dp2, dgt2 = _resid_bwd(dx, sv["p2"], gt2, f"res2b_{l}")
        da = _mm(dp2, w_fo[l], f"ffoutb_{l}", trans_b=True, out_dtype=_MMT)
        gbig["w_fo"][l] = _mm_tn(sv["act"], dp2, f"ffoutw_{l}")
        dgu = _swiglu_bwd(da, sv["gu"], f"swiglub_{l}")
        dh2 = _mm(dgu, wt_fi[l], f"ffinb_{l}")
        gbig["w_fi"][l] = _mm_tn(dgu, sv["h2"], f"ffinw_{l}")
        dx1, dgm2, dsh2 = _norm_mod_bwd(sv["x1"], dh2, dx, g2, sc2, f"norm2b_{l}")
        dp1, dgt1 = _resid_bwd(dx1, sv["p1"], gt1, f"res1b_{l}")
        dmg = _mm(dp1, w_o[l], f"woutb_{l}", trans_b=True, out_dtype=_MMT)
        gbig["w_o"][l] = _mm_tn(sv["mg"], dp1, f"woutw_{l}")
        dpa, dpb, dproj = _merge_bwd(dmg, sv["pa"], sv["pb"], proj, D, f"mergeb_{l}")
        dya = _mm(dpa, w_a[l], f"brab_{l}", trans_b=True, out_dtype=_MMT)
        gbig["w_a"][l] = _mm_tn(sv["ya"], dpa, f"braw_{l}")
        dyb = _mm(dpb, w_b[l], f"brbb_{l}", trans_b=True, out_dtype=_MMT)
        gbig["w_b"][l] = _mm_tn(sv["yb"], dpb, f"brbw_{l}")
        dproj, dsw, dsa, dgv = _gmlp_bwd(proj, dya, spatial_w, sbt, gv3, l, D, dproj, f"gmlpb_{l}")
        do, dproj, dgo = _onorm_bwd(dyb, sv["o"], proj, go3, l, D, dproj, f"onormb_{l}")
        dqn, dkn, dvs, dg_r, db_r = _gdn_bwd(sv["qn"], sv["kn"], sv["vs"], sv["g_r"], sv["b_r"], sv["s_all"],
                                             sv["t_all"], do, f"gdnb_{l}")
        dbg = jnp.concatenate([toks_of(db_r), toks_of(dg_r), jnp.zeros((T, _LANE - 2 * H), _F32)], axis=1)
        dc, dproj, dcw, dal, ddt = _conv_bwd1(proj, dqn, dkn, dvs, dbg, cw_full, alog_row, dtb_row, l, D, dproj,
                                              f"convb_{l}")
        dproj = _conv_bwd2(dc, cw_full, l, dproj, f"convx_{l}")
        dh = _mm(dproj, wt_in_p[l], f"projb_{l}", tm=512, tk=1664)
        gbig["w_in"][l] = _mm_tn(dproj, sv["h"], f"projw_{l}", tm=640)
        dx, dgm1, dsh1 = _norm_mod_bwd(sv["x"], dh, dx1, g1, sc1, f"norm1b_{l}")
        small["dmod"][l] = jnp.concatenate([dsh1, dgm1 * g1, dgt1, dsh2, dgm2 * g2, dgt2], axis=1)
        small["n1"][l], small["n2"][l] = dgm1 * (1.0 + sc1), dgm2 * (1.0 + sc2)
        small["sw"][l], small["gv"][l], small["cw"][l], small["go"][l] = dsw, dgv, dcw, dgo
        small["sb"][l] = jnp.transpose(dsa.reshape(_AC, G, _GD).sum(axis=-1))
        small["al"][l], small["dt"][l] = dal[:, H:2 * H], ddt[:, H:2 * H]
    grad_x = dx.reshape(1, T, D)

    names_small = ["dmod", "n1", "n2", "sw", "sb", "gv", "cw", "al", "dt", "go"]
    flat = [jnp.stack(small[k]).reshape(-1) for k in names_small] + [dfg.reshape(-1)]
    sizes = [f.shape[0] for f in flat]
    tot = sum(sizes)
    pad = (-tot) % 1024
    pay = jnp.concatenate(flat + [jnp.zeros((pad,), _F32)]).reshape(-1, 1024)
    sm_all = _gather8(pay, "gather_small").reshape(8, -1)
    offs = [0]
    for s in sizes:
        offs.append(offs[-1] + s)
    part = {k: sm_all[:, offs[i]:offs[i + 1]] for i, k in enumerate(names_small + ["fg"])}
    dmod_all = part["dmod"].reshape(8, L, 6 * D)

    outs = {}

    def update(nm, w, gst, m, v):
        shp = w.shape
        w2 = _as2d(w)
        g, d, mn, vn = _adamw(w2, gst.reshape((gst.shape[0],) + w2.shape), _as2d(m), _as2d(v), f"adamw_{nm}")
        outs[nm] = (g.reshape(shp), d.reshape(shp), mn.reshape(shp), vn.reshape(shp))

    chip = 2 * ix + iy
    dmod_t = jnp.transpose(dmod_all, (1, 0, 2))
    dmod_mine = lax.dynamic_slice_in_dim(dmod_t, chip * Ws, Ws, axis=2)
    g_ada_w = _ada_bwd(jnp.transpose(c_all), dmod_mine, "ada_bwd")
    update("ada_w", ada_w, g_ada_w[None], m_ada_w, v_ada_w)
    update("ada_b", ada_b, dmod_all, m_ada_b, v_ada_b)
    update("norm1_g", norm1_g, part["n1"], m_norm1_g, v_norm1_g)
    update("norm2_g", norm2_g, part["n2"], m_norm2_g, v_norm2_g)
    update("spatial_w", spatial_w, part["sw"], m_spatial_w, v_spatial_w)
    update("spatial_b", spatial_b, part["sb"], m_spatial_b, v_spatial_b)
    update("v_norm_g", v_norm_g, part["gv"], m_v_norm_g, v_v_norm_g)
    update("a_log", a_log, part["al"], m_a_log, v_a_log)
    update("dt_bias", dt_bias, part["dt"], m_dt_bias, v_dt_bias)
    update("o_norm_g", o_norm_g, part["go"], m_o_norm_g, v_o_norm_g)
    update("final_g", final_g, part["fg"], m_final_g, v_final_g)
    cw_cols = conv_w.shape[2]
    dcw_all = part["cw"].reshape(8, L, _KC, 4 * cw_cols)
    update("conv_w", conv_w, lax.dynamic_slice_in_dim(dcw_all, chip * cw_cols, cw_cols, axis=3), m_conv_w, v_conv_w)

    def hull_of(p, k):
        a, b = starts[k], starts[k] + Hh
        out = []
        for lo, hi, plo in ((0, 6 * D, 0), (6 * D, 6 * D + 2 * H, 8 * D), (6 * D + 2 * H, 8 * D + 2 * H, 6 * D),
                            (8 * D + 2 * H, No, None)):
            s, e = max(a, lo), min(b, hi)
            if s < e:
                out.append(jnp.zeros((e - s, D), _MMT) if plo is None else p[plo + s - lo:plo + e - lo].astype(_MMT))
        return out

    pieces = []
    for k in range(4):
        for l in range(L):
            pieces += hull_of(gbig["w_in"][l], k)
        for i, nm in enumerate(("w_a", "w_b", "w_o", "w_fi", "w_fo")):
            per = pers[i + 1]
            pieces += [gbig[nm][l][k * per:(k + 1) * per].astype(_MMT) for l in range(L)]
        pieces.append(jnp.zeros((rpad, D), _MMT))
    send = jnp.concatenate(pieces, axis=0).reshape(4, Rp, D)
    got = _send_half_to_sibling(send, "reduce_cores")
    chipsum = _add_own_half(send, got, ic.astype(jnp.int32).reshape(1), "add_cores")
    parts = _scatter_to_chips(chipsum, "reduce_chips")
    mine = _sum_slots(parts, "add_chips")
    other = _swap_with_sibling(mine, "swap_cores")
    first = ic == 0
    gsum = jnp.concatenate([jnp.where(first, mine, other), jnp.where(first, other, mine)], axis=0)
    big_names = ["w_in", "w_branch_a", "w_branch_b", "w_out", "w_ffn_in", "w_ffn_out"]
    big_m = [m_w_in, m_w_branch_a, m_w_branch_b, m_w_out, m_w_ffn_in, m_w_ffn_out]
    big_v = [v_w_in, v_w_branch_a, v_w_branch_b, v_w_out, v_w_ffn_in, v_w_ffn_out]
    for i, (nm, w, m, v) in enumerate(zip(big_names, big, big_m, big_v)):
        g = gsum[roff[i]:roff[i + 1]].reshape(L, pers[i], D)
        if i == 0:
            g = jnp.transpose(lax.dynamic_slice_in_dim(g, my_off, Wc, axis=1), (0, 2, 1))
        elif i == 4:
            g = jnp.transpose(g, (0, 2, 1))
        update(nm, w, g[None], m, v)

    order = ["ada_w", "ada_b", "norm1_g", "w_in", "conv_w", "spatial_w", "spatial_b", "v_norm_g", "a_log", "dt_bias",
             "o_norm_g", "w_branch_a", "w_branch_b", "w_out", "norm2_g", "w_ffn_in", "w_ffn_out", "final_g"]
    return (loss, grad_x, *[outs[n][0] for n in order], *[outs[n][1] for n in order],
            *[outs[n][2] for n in order], *[outs[n][3] for n in order])
```

```python
import functools
import math

import jax
import jax.numpy as jnp
from jax import lax
from jax.experimental import pallas as pl
from jax.experimental.pallas import tpu as pltpu

_F32 = jnp.float32
_BF = jnp.bfloat16
_MMT = jnp.bfloat16
_EPS = 1e-6
_GD = 128
_AC = 128
_BC = 64
_KC = 4
_HB = 8
_NCH = 8
_LANE = 128
_VMEM_LIMIT = 56 * 1024 * 1024

_LR, _B1, _B2, _AEPS, _WD, _STEP = 0.001, 0.9, 0.999, 1e-08, 0.01, 10

_NN = (((1,), (0,)), ((), ()))
_NT = (((1,), (1,)), ((), ()))
_TN = (((0,), (0,)), ((), ()))

_MESH = pl.DeviceIdType.MESH


def _cp(*sem):
    return pltpu.CompilerParams(dimension_semantics=tuple(sem), vmem_limit_bytes=_VMEM_LIMIT)


def _dot(a, b, dn=_NN):
    return lax.dot_general(a.astype(_MMT), b.astype(_MMT), dn, preferred_element_type=_F32)


def _pick(n, target, unit=_LANE):
    if n <= target:
        return n
    best = None
    for t in range(unit, target + 1, unit):
        if n % t == 0:
            best = t
    assert best is not None, (n, target)
    return best


def _sigmoid(x):
    return 1.0 / (1.0 + jnp.exp(-x))


def _silu(x):
    return x * _sigmoid(x)


def _dsilu(x):
    s = _sigmoid(x)
    return s * (1.0 + x * (1.0 - s))


_GK = math.sqrt(2.0 / math.pi)


def _gelu(x):
    return 0.5 * x * (1.0 + jnp.tanh(_GK * (x + 0.044715 * x * x * x)))


def _dgelu(x):
    t = jnp.tanh(_GK * (x + 0.044715 * x * x * x))
    return 0.5 * (1.0 + t) + 0.5 * x * (1.0 - t * t) * _GK * (1.0 + 3.0 * 0.044715 * x * x)


def _softplus(x):
    return jnp.maximum(x, 0.0) + jnp.log(1.0 + jnp.exp(-jnp.abs(x)))


def _rspec(tb, w, cb=0):
    return pl.BlockSpec((tb, w), lambda i: (i, cb))


def _fspec(shape):
    nd = len(shape)
    return pl.BlockSpec(tuple(shape), lambda i: (0,) * nd)


def _lspec(tail, li):
    nd = len(tail)
    return pl.BlockSpec((None,) + tuple(tail), lambda i: (li,) + (0,) * nd)


def _slot_all8(x, y, c):
    return 4 * x + 2 * y + c


def _gather8(v, name):
    R, W = v.shape

    def body(v_ref, o_ref, ssem, rsem, lsem):
        x, y, c = lax.axis_index("x"), lax.axis_index("y"), lax.axis_index("c")
        sib = (x, y, 1 - c)
        chips = _other_chips(x, y)

        def slot(px, py, pc):
            return o_ref.at[_slot_all8(px, py, pc)]

        own = pltpu.make_async_copy(v_ref, slot(x, y, c), lsem)
        own.start()
        started = [_rcopy(v_ref, slot(x, y, c), ssem.at[0], rsem.at[0], sib)]
        started += [_rcopy(v_ref, slot(x, y, c), ssem.at[1 + j], rsem.at[1 + j], (px, py, c))
                    for j, (px, py) in enumerate(chips)]
        for cp in started:
            cp.start()
        for j, (px, py) in enumerate(chips):
            blk = slot(px, py, c)
            _rcopy(blk, blk, ssem.at[1 + j], rsem.at[1 + j], (px, py, c)).wait_recv()
            fw = _rcopy(blk, blk, ssem.at[4 + j], rsem.at[4 + j], sib)
            fw.start()
            started.append(fw)
        blk = slot(x, y, 1 - c)
        _rcopy(blk, blk, ssem.at[0], rsem.at[0], sib).wait_recv()
        for j, (px, py) in enumerate(chips):
            blk = slot(px, py, 1 - c)
            _rcopy(blk, blk, ssem.at[4 + j], rsem.at[4 + j], sib).wait_recv()
        for cp in started:
            cp.wait_send()
        own.wait()

    return pl.pallas_call(
        body, name=name, out_shape=jax.ShapeDtypeStruct((8, R, W), v.dtype), in_specs=[_HBM], out_specs=_HBM,
        scratch_shapes=[pltpu.SemaphoreType.DMA((7,)), pltpu.SemaphoreType.DMA((7,)), pltpu.SemaphoreType.DMA],
    )(v)


def _rcopy(src, dst, ssem, rsem, dev):
    return pltpu.make_async_remote_copy(src_ref=src, dst_ref=dst, send_sem=ssem, recv_sem=rsem,
                                        device_id=dev, device_id_type=_MESH)


def _other_chips(x, y):
    return [(1 - x, y), (x, 1 - y), (1 - x, 1 - y)]


_HBM = pl.BlockSpec(memory_space=pl.ANY)


def _gather_chips(shard, name):
    Rp, W = shard.shape
    Rh = Rp // 2
    rc = Rh // _NCH
    hq = _NCH // 2

    def body(s_ref, o_ref, ssem, rsem, lsem):
        x, y, c = lax.axis_index("x"), lax.axis_index("y"), lax.axis_index("c")
        chip = 2 * x + y
        xn, yn, dg = _other_chips(x, y)
        cx, cy, cd = 2 * xn[0] + xn[1], 2 * yn[0] + yn[1], 2 * dg[0] + dg[1]

        def rows(q):
            return pl.ds(c * Rh + q * rc, rc)

        locs = []
        for q in range(_NCH):
            lc = pltpu.make_async_copy(s_ref.at[rows(q)], o_ref.at[chip, rows(q)], lsem.at[q])
            lc.start()
            locs.append(lc)
        started = []
        for q in range(_NCH):
            for j, nb in ((0, xn), (1, yn)):
                cp = _rcopy(s_ref.at[rows(q)], o_ref.at[chip, rows(q)], ssem.at[j * _NCH + q], rsem.at[j * _NCH + q],
                            (nb[0], nb[1], c))
                cp.start()
                started.append(cp)
        for q in range(_NCH):
            bx = o_ref.at[cx, rows(q)]
            _rcopy(bx, bx, ssem.at[q], rsem.at[q], (xn[0], xn[1], c)).wait_recv()
            if q >= hq:
                rl = _rcopy(bx, bx, ssem.at[2 * _NCH + q], rsem.at[2 * _NCH + q], (yn[0], yn[1], c))
                rl.start()
                started.append(rl)
            by = o_ref.at[cy, rows(q)]
            _rcopy(by, by, ssem.at[_NCH + q], rsem.at[_NCH + q], (yn[0], yn[1], c)).wait_recv()
            if q < hq:
                rl = _rcopy(by, by, ssem.at[2 * _NCH + q], rsem.at[2 * _NCH + q], (xn[0], xn[1], c))
                rl.start()
                started.append(rl)
        for q in range(_NCH):
            bd = o_ref.at[cd, rows(q)]
            _rcopy(bd, bd, ssem.at[2 * _NCH + q], rsem.at[2 * _NCH + q], (dg[0], dg[1], c)).wait_recv()
        for cp in started:
            cp.wait_send()
        for lc in locs:
            lc.wait()

    return pl.pallas_call(
        body, name=name, out_shape=jax.ShapeDtypeStruct((4, Rp, W), shard.dtype), in_specs=[_HBM], out_specs=_HBM,
        scratch_shapes=[pltpu.SemaphoreType.DMA((3 * _NCH,))] * 2 + [pltpu.SemaphoreType.DMA((_NCH,))],
    )(shard)


def _fill_from_sibling(buf, name):
    P, Rp, W = buf.shape
    Rh = Rp // 2
    rc = Rh // _NCH

    def body(s_ref, o_ref, ssem, rsem):
        x, y, c = lax.axis_index("x"), lax.axis_index("y"), lax.axis_index("c")
        cps = []
        for k in range(P):
            for q in range(_NCH):
                r = pl.ds(c * Rh + q * rc, rc)
                cp = _rcopy(s_ref.at[k, r], o_ref.at[k, r], ssem.at[k * _NCH + q], rsem.at[k * _NCH + q],
                            (x, y, 1 - c))
                cp.start()
                cps.append(cp)
        for k in range(P):
            for q in range(_NCH):
                blk = o_ref.at[k, pl.ds((1 - c) * Rh + q * rc, rc)]
                _rcopy(blk, blk, ssem.at[k * _NCH + q], rsem.at[k * _NCH + q], (x, y, 1 - c)).wait_recv()
        for cp in cps:
            cp.wait_send()

    return pl.pallas_call(
        body, name=name, out_shape=jax.ShapeDtypeStruct(buf.shape, buf.dtype), in_specs=[_HBM], out_specs=_HBM,
        scratch_shapes=[pltpu.SemaphoreType.DMA((P * _NCH,))] * 2, input_output_aliases={0: 0},
    )(buf)


def _send_half_to_sibling(send, name):
    P, Rp, W = send.shape
    Rh = Rp // 2
    rc = Rh // _NCH

    def body(s_ref, o_ref, ssem, rsem):
        x, y, c = lax.axis_index("x"), lax.axis_index("y"), lax.axis_index("c")
        cps = []
        for k in range(P):
            for q in range(_NCH):
                cp = _rcopy(s_ref.at[k, pl.ds((1 - c) * Rh + q * rc, rc)], o_ref.at[k, pl.ds(q * rc, rc)],
                            ssem.at[k * _NCH + q], rsem.at[k * _NCH + q], (x, y, 1 - c))
                cp.start()
                cps.append(cp)
        for cp in cps:
            cp.wait()

    return pl.pallas_call(
        body, name=name, out_shape=jax.ShapeDtypeStruct((P, Rh, W), send.dtype), in_specs=[_HBM], out_specs=_HBM,
        scratch_shapes=[pltpu.SemaphoreType.DMA((P * _NCH,))] * 2,
    )(send)


def _scatter_to_chips(cs, name):
    P, Rh, W = cs.shape
    rc = Rh // _NCH

    def body(s_ref, o_ref, ssem, rsem, lsem):
        x, y, c = lax.axis_index("x"), lax.axis_index("y"), lax.axis_index("c")
        chip = 2 * x + y
        peers = _other_chips(x, y)
        locs = []
        for q in range(_NCH):
            r = pl.ds(q * rc, rc)
            lc = pltpu.make_async_copy(s_ref.at[chip, r], o_ref.at[chip, r], lsem.at[q])
            lc.start()
            locs.append(lc)
        cps = []
        for j, (px, py) in enumerate(peers):
            for q in range(_NCH):
                r = pl.ds(q * rc, rc)
                cp = _rcopy(s_ref.at[2 * px + py, r], o_ref.at[chip, r], ssem.at[j * _NCH + q], rsem.at[j * _NCH + q],
                            (px, py, c))
                cp.start()
                cps.append(cp)
        for j, (px, py) in enumerate(peers):
            for q in range(_NCH):
                blk = o_ref.at[2 * px + py, pl.ds(q * rc, rc)]
                _rcopy(blk, blk, ssem.at[j * _NCH + q], rsem.at[j * _NCH + q], (px, py, c)).wait_recv()
        for cp in cps:
            cp.wait_send()
        for lc in locs:
            lc.wait()

    return pl.pallas_call(
        body, name=name, out_shape=jax.ShapeDtypeStruct((P, Rh, W), cs.dtype), in_specs=[_HBM], out_specs=_HBM,
        scratch_shapes=[pltpu.SemaphoreType.DMA((3 * _NCH,))] * 2 + [pltpu.SemaphoreType.DMA((_NCH,))],
    )(cs)


def _swap_with_sibling(v, name):
    R, W = v.shape
    rc = R // _NCH

    def body(s_ref, o_ref, ssem, rsem):
        x, y, c = lax.axis_index("x"), lax.axis_index("y"), lax.axis_index("c")
        cps = []
        for q in range(_NCH):
            r = pl.ds(q * rc, rc)
            cp = _rcopy(s_ref.at[r], o_ref.at[r], ssem.at[q], rsem.at[q], (x, y, 1 - c))
            cp.start()
            cps.append(cp)
        for cp in cps:
            cp.wait()

    return pl.pallas_call(
        body, name=name, out_shape=jax.ShapeDtypeStruct((R, W), v.dtype), in_specs=[_HBM], out_specs=_HBM,
        scratch_shapes=[pltpu.SemaphoreType.DMA((_NCH,))] * 2,
    )(v)


def _mm(a, b, name, li=None, trans_b=False, out_dtype=_F32, tm=1024, tn=1024, tk=2048):
    M, K = a.shape
    bs = b.shape[-2:]
    N = bs[0] if trans_b else bs[1]
    tm, tn, tk = _pick(M, tm, 8), _pick(N, tn), _pick(K, tk)
    nk = K // tk
    lead = () if li is None else (None,)

    def bmap(i, j, k):
        idx = (j, k) if trans_b else (k, j)
        return idx if li is None else (li,) + idx

    def body(a_ref, b_ref, o_ref, acc):
        k = pl.program_id(2)
        part = lax.dot_general(a_ref[...], b_ref[...], _NT if trans_b else _NN, preferred_element_type=_F32)
        if nk == 1:
            o_ref[...] = part.astype(o_ref.dtype)
        else:
            @pl.when(k == 0)
            def _():
                acc[...] = part

            @pl.when(k > 0)
            def _():
                acc[...] += part

            @pl.when(k == nk - 1)
            def _():
                o_ref[...] = acc[...].astype(o_ref.dtype)

    return pl.pallas_call(
        body, name=name, grid=(M // tm, N // tn, nk),
        in_specs=[pl.BlockSpec((tm, tk), lambda i, j, k: (i, k)),
                  pl.BlockSpec(lead + ((tn, tk) if trans_b else (tk, tn)), bmap)],
        out_specs=pl.BlockSpec((tm, tn), lambda i, j, k: (i, j)),
        out_shape=jax.ShapeDtypeStruct((M, N), out_dtype),
        scratch_shapes=[pltpu.VMEM((tm, tn) if nk > 1 else (8, _LANE), _F32)],
        compiler_params=_cp("parallel", "parallel", "arbitrary"),
    )(a, b)


def _mm_tn(a, b, name, tm=512, tn=512):
    T, M = a.shape
    N = b.shape[1]
    tm, tn = _pick(M, tm), _pick(N, tn)

    def body(a_ref, b_ref, o_ref):
        o_ref[...] = lax.dot_general(a_ref[...], b_ref[...], _TN, preferred_element_type=_F32)

    return pl.pallas_call(
        body, name=name, grid=(M // tm, N // tn),
        in_specs=[pl.BlockSpec((T, tm), lambda i, j: (0, i)), pl.BlockSpec((T, tn), lambda i, j: (0, j))],
        out_specs=pl.BlockSpec((tm, tn), lambda i, j: (i, j)),
        out_shape=jax.ShapeDtypeStruct((M, N), _F32),
        compiler_params=_cp("parallel", "parallel"),
    )(a, b)


def _ada_fwd(c_all, ada_w, name):
    L, D, Ws = ada_w.shape
    B = c_all.shape[0]

    def body(c_ref, w_ref, o_ref):
        o_ref[...] = _dot(_silu(c_ref[...]), w_ref[...])

    return pl.pallas_call(
        body, name=name, grid=(L,),
        in_specs=[_fspec((B, D)), pl.BlockSpec((None, D, Ws), lambda l: (l, 0, 0))],
        out_specs=pl.BlockSpec((None, B, Ws), lambda l: (l, 0, 0)),
        out_shape=jax.ShapeDtypeStruct((L, B, Ws), _F32), compiler_params=_cp("parallel"),
    )(c_all, ada_w)


def _ada_bwd(c_all_t, dmod, name):
    D, B = c_all_t.shape
    L, _, Ws = dmod.shape

    def body(c_ref, d_ref, o_ref):
        ct = _silu(c_ref[...])
        d = d_ref[...]
        acc = ct[:, 0:1] * d[0:1, :]
        for b in range(1, B):
            acc = acc + ct[:, b:b + 1] * d[b:b + 1, :]
        o_ref[...] = acc

    return pl.pallas_call(
        body, name=name, grid=(L,),
        in_specs=[_fspec((D, B)), pl.BlockSpec((None, B, Ws), lambda l: (l, 0, 0))],
        out_specs=pl.BlockSpec((None, D, Ws), lambda l: (l, 0, 0)),
        out_shape=jax.ShapeDtypeStruct((L, D, Ws), _F32), compiler_params=_cp("parallel"),
    )(c_all_t, dmod)


def _norm_mod(x, g, sc, sh, name, tb=512):
    T, D = x.shape
    tb = _pick(T, tb, 8)

    def body(x_ref, g_ref, sc_ref, sh_ref, h_ref):
        xv = x_ref[...]
        r = lax.rsqrt(jnp.mean(xv * xv, axis=1, keepdims=True) + _EPS)
        h_ref[...] = (xv * r * (g_ref[...] * (1.0 + sc_ref[...])) + sh_ref[...]).astype(h_ref.dtype)

    return pl.pallas_call(
        body, name=name, grid=(T // tb,),
        in_specs=[_rspec(tb, D), _fspec((1, D)), _fspec((1, D)), _fspec((1, D))],
        out_specs=_rspec(tb, D), out_shape=jax.ShapeDtypeStruct((T, D), _MMT), compiler_params=_cp("parallel"),
    )(x, g, sc, sh)


def _norm_mod_bwd(x, dh, dres, g, sc, name, tb=256):
    T, D = x.shape
    tb = _pick(T, tb, 8)

    def body(x_ref, dh_ref, dr_ref, g_ref, sc_ref, dx_ref, dgm_ref, dsh_ref):
        i = pl.program_id(0)
        xv, dh_ = x_ref[...], dh_ref[...]
        r = lax.rsqrt(jnp.mean(xv * xv, axis=1, keepdims=True) + _EPS)
        xn = xv * r
        dxn = dh_ * (g_ref[...] * (1.0 + sc_ref[...]))
        dx_ref[...] = dr_ref[...] + r * (dxn - xn * jnp.mean(dxn * xn, axis=1, keepdims=True))

        @pl.when(i == 0)
        def _():
            dgm_ref[...] = jnp.zeros_like(dgm_ref)
            dsh_ref[...] = jnp.zeros_like(dsh_ref)

        dgm_ref[...] += jnp.sum(dh_ * xn, axis=0, keepdims=True)
        dsh_ref[...] += jnp.sum(dh_, axis=0, keepdims=True)

    return pl.pallas_call(
        body, name=name, grid=(T // tb,),
        in_specs=[_rspec(tb, D), _rspec(tb, D), _rspec(tb, D), _fspec((1, D)), _fspec((1, D))],
        out_specs=[_rspec(tb, D), _fspec((1, D)), _fspec((1, D))],
        out_shape=[jax.ShapeDtypeStruct((T, D), _F32), jax.ShapeDtypeStruct((1, D), _F32),
                   jax.ShapeDtypeStruct((1, D), _F32)],
        compiler_params=_cp("arbitrary"),
    )(x, dh, dres, g, sc)


def _resid(x, p, gt, name, tb=512):
    T, D = x.shape
    tb = _pick(T, tb, 8)

    def body(x_ref, p_ref, gt_ref, o_ref):
        o_ref[...] = x_ref[...] + gt_ref[...] * p_ref[...]

    return pl.pallas_call(
        body, name=name, grid=(T // tb,), in_specs=[_rspec(tb, D), _rspec(tb, D), _fspec((1, D))],
        out_specs=_rspec(tb, D), out_shape=jax.ShapeDtypeStruct((T, D), _F32), compiler_params=_cp("parallel"),
    )(x, p, gt)


def _resid_bwd(dx, p, gt, name, tb=512):
    T, D = dx.shape
    tb = _pick(T, tb, 8)

    def body(dx_ref, p_ref, gt_ref, dp_ref, dgt_ref):
        i = pl.program_id(0)
        d = dx_ref[...]
        dp_ref[...] = (d * gt_ref[...]).astype(dp_ref.dtype)

        @pl.when(i == 0)
        def _():
            dgt_ref[...] = jnp.zeros_like(dgt_ref)

        dgt_ref[...] += jnp.sum(d * p_ref[...], axis=0, keepdims=True)

    return pl.pallas_call(
        body, name=name, grid=(T // tb,), in_specs=[_rspec(tb, D), _rspec(tb, D), _fspec((1, D))],
        out_specs=[_rspec(tb, D), _fspec((1, D))],
        out_shape=[jax.ShapeDtypeStruct((T, D), _MMT), jax.ShapeDtypeStruct((1, D), _F32)],
        compiler_params=_cp("arbitrary"),
    )(dx, p, gt)


def _gmlp_chunk(u_raw, v_raw, sw_ref, sbt, gv, G):
    u, v = _gelu(u_raw), _gelu(v_raw)
    ii = lax.broadcasted_iota(jnp.int32, (_AC, _AC), 0)
    jj = lax.broadcasted_iota(jnp.int32, (_AC, _AC), 1)
    out = []
    for gi in range(G):
        sl = slice(gi * _GD, (gi + 1) * _GD)
        vg = v[:, sl]
        r = lax.rsqrt(jnp.mean(vg * vg, axis=1, keepdims=True) + _EPS)
        vhat = vg * r
        W = jnp.where(jj <= ii, sw_ref[gi], 0.0)
        s = _dot(W, vhat * gv[:, sl]) + sbt[:, gi:gi + 1]
        out.append((u[:, sl], s, vhat, r, W))
    return out


def _gmlp_fwd(proj, sw, sbt, gv, li, D, name):
    T = proj.shape[0]
    G = D // _GD

    def body(u_ref, v_ref, sw_ref, sbt_ref, gv_ref, y_ref):
        parts = _gmlp_chunk(u_ref[...], v_ref[...], sw_ref, sbt_ref[...], gv_ref[...], G)
        for gi, (u, s, _, _, _) in enumerate(parts):
            y_ref[:, gi * _GD:(gi + 1) * _GD] = (u * s).astype(y_ref.dtype)

    return pl.pallas_call(
        body, name=name, grid=(T // _AC,),
        in_specs=[_rspec(_AC, D, 0), _rspec(_AC, D, 1), _lspec((G, _AC, _AC), li), _lspec((_AC, G), li),
                  _lspec((1, D), li)],
        out_specs=_rspec(_AC, D), out_shape=jax.ShapeDtypeStruct((T, D), _MMT), compiler_params=_cp("parallel"),
    )(proj, proj, sw, sbt, gv)


def _gmlp_bwd(proj, dy, sw, sbt, gv, li, D, into, name):
    T = proj.shape[0]
    G = D // _GD

    def body(u_ref, v_ref, dy_ref, sw_ref, sbt_ref, gv_ref, _, duv_ref, dsw_ref, dsa_ref, dgv_ref):
        i = pl.program_id(0)

        @pl.when(i == 0)
        def _():
            dsw_ref[...] = jnp.zeros_like(dsw_ref)
            dsa_ref[...] = jnp.zeros_like(dsa_ref)
            dgv_ref[...] = jnp.zeros_like(dgv_ref)

        u_raw, v_raw, dy_, gv_ = u_ref[...], v_ref[...], dy_ref[...].astype(_F32), gv_ref[...]
        parts = _gmlp_chunk(u_raw, v_raw, sw_ref, sbt_ref[...], gv_, G)
        ii = lax.broadcasted_iota(jnp.int32, (_AC, _AC), 0)
        jj = lax.broadcasted_iota(jnp.int32, (_AC, _AC), 1)
        dgu, dgv = _dgelu(u_raw), _dgelu(v_raw)
        for gi, (u, s, vhat, r, W) in enumerate(parts):
            sl = slice(gi * _GD, (gi + 1) * _GD)
            dyg = dy_[:, sl]
            ds = dyg * u
            vn = vhat * gv_[:, sl]
            dsw_ref[gi] += jnp.where(jj <= ii, _dot(ds, vn, _NT), 0.0)
            dsa_ref[:, sl] += ds
            dvn = _dot(W, ds, _TN)
            dgv_ref[:, sl] += jnp.sum(dvn * vhat, axis=0, keepdims=True)
            dvh = dvn * gv_[:, sl]
            dvg = r * (dvh - vhat * jnp.mean(dvh * vhat, axis=1, keepdims=True))
            duv_ref[:, sl] = (dyg * s * dgu[:, sl]).astype(duv_ref.dtype)
            duv_ref[:, D + gi * _GD:D + (gi + 1) * _GD] = (dvg * dgv[:, sl]).astype(duv_ref.dtype)

    return pl.pallas_call(
        body, name=name, grid=(T // _AC,),
        in_specs=[_rspec(_AC, D, 0), _rspec(_AC, D, 1), _rspec(_AC, D), _lspec((G, _AC, _AC), li),
                  _lspec((_AC, G), li), _lspec((1, D), li), _HBM],
        out_specs=[_rspec(_AC, 2 * D), _fspec((G, _AC, _AC)), _fspec((_AC, D)), _fspec((1, D))],
        out_shape=[jax.ShapeDtypeStruct(into.shape, into.dtype), jax.ShapeDtypeStruct((G, _AC, _AC), _F32),
                   jax.ShapeDtypeStruct((_AC, D), _F32), jax.ShapeDtypeStruct((1, D), _F32)],
        input_output_aliases={6: 0}, compiler_params=_cp("arbitrary"),
    )(proj, proj, dy, sw, sbt, gv, into)


def _conv_taps(halo, cur, first):
    tb = cur.shape[0]
    full = jnp.concatenate([jnp.where(first, 0.0, halo), cur], axis=0)
    return [full[8:] if j == _KC - 1 else pltpu.roll(full, _KC - 1 - j, 0)[8:] for j in range(_KC)]


def _prev_spec(tb, w, cb):
    return pl.BlockSpec((8, w), lambda i: (jnp.maximum(i * (tb // 8) - 1, 0), cb))


def _l2_heads(x, H):
    outs, rs = [], []
    for h in range(H):
        xh = x[:, h * _GD:(h + 1) * _GD]
        r = lax.rsqrt(jnp.sum(xh * xh, axis=1, keepdims=True) + _EPS)
        outs.append(xh * r)
        rs.append(r)
    return outs, rs


def _gate_rows(ba, alog_row, dtb_row, H):
    lane = lax.broadcasted_iota(jnp.int32, ba.shape, 1)
    beta = _sigmoid(ba)
    g = -jnp.exp(alog_row) * _softplus(ba + dtb_row)
    return lane, beta, g


def _conv_fwd(proj, cw, alog_row, dtb_row, li, D, name, tb=256):
    T = proj.shape[0]
    H = D // _GD
    tb = _pick(T, tb, 8)
    bac = (8 * D) // _LANE

    def body(q_ref, k_ref, v_ref, qh_ref, kh_ref, vh_ref, ba_ref, cw_ref, al_ref, dtb_ref,
             qo_ref, ko_ref, vo_ref, bg_ref):
        first = pl.program_id(0) == 0
        cw_ = cw_ref[...]
        for idx, (cur, halo, out) in enumerate(((q_ref, qh_ref, qo_ref), (k_ref, kh_ref, ko_ref),
                                                 (v_ref, vh_ref, vo_ref))):
            taps = _conv_taps(halo[...], cur[...], first)
            w = cw_[:, idx * D:(idx + 1) * D]
            cv = taps[0] * w[0:1, :]
            for j in range(1, _KC):
                cv = cv + taps[j] * w[j:j + 1, :]
            act = _silu(cv)
            if idx < 2:
                outs, _ = _l2_heads(act, H)
                for h in range(H):
                    out[:, h * _GD:(h + 1) * _GD] = outs[h]
            else:
                out[...] = act
        lane, beta, g = _gate_rows(ba_ref[...], al_ref[...], dtb_ref[...], H)
        bg_ref[...] = jnp.where(lane < H, beta, jnp.where(lane < 2 * H, g, 0.0))

    return pl.pallas_call(
        body, name=name, grid=(T // tb,),
        in_specs=[_rspec(tb, D, 2), _rspec(tb, D, 3), _rspec(tb, D, 4),
                  _prev_spec(tb, D, 2), _prev_spec(tb, D, 3), _prev_spec(tb, D, 4),
                  _rspec(tb, _LANE, bac), _lspec((_KC, 3 * D), li), _lspec((1, _LANE), li), _lspec((1, _LANE), li)],
        out_specs=[_rspec(tb, D), _rspec(tb, D), _rspec(tb, D), _rspec(tb, _LANE)],
        out_shape=[jax.ShapeDtypeStruct((T, D), _F32)] * 3 + [jax.ShapeDtypeStruct((T, _LANE), _F32)],
        compiler_params=_cp("parallel"),
    )(proj, proj, proj, proj, proj, proj, proj, cw, alog_row, dtb_row)


def _conv_bwd1(proj, dqn, dkn, dvs, dbg, cw, alog_row, dtb_row, li, D, into, name, tb=256):
    T = proj.shape[0]
    H = D // _GD
    tb = _pick(T, tb, 8)
    bac = (8 * D) // _LANE

    def body(q_ref, k_ref, v_ref, qh_ref, kh_ref, vh_ref, ba_ref, dq_ref, dk_ref, dv_ref, dbg_ref,
             cw_ref, al_ref, dtb_ref, _, dc_ref, dba_ref, dcw_ref, dal_ref, ddt_ref):
        i = pl.program_id(0)
        first = i == 0

        @pl.when(first)
        def _():
            dcw_ref[...] = jnp.zeros_like(dcw_ref)
            dal_ref[...] = jnp.zeros_like(dal_ref)
            ddt_ref[...] = jnp.zeros_like(ddt_ref)

        cw_ = cw_ref[...]
        for idx, (cur, halo, dref) in enumerate(((q_ref, qh_ref, dq_ref), (k_ref, kh_ref, dk_ref),
                                                  (v_ref, vh_ref, dv_ref))):
            taps = _conv_taps(halo[...], cur[...], first)
            w = cw_[:, idx * D:(idx + 1) * D]
            cv = taps[0] * w[0:1, :]
            for j in range(1, _KC):
                cv = cv + taps[j] * w[j:j + 1, :]
            dact = dref[...]
            if idx < 2:
                outs, rs = _l2_heads(_silu(cv), H)
                pieces = []
                for h in range(H):
                    dy = dact[:, h * _GD:(h + 1) * _GD]
                    pieces.append(rs[h] * (dy - outs[h] * jnp.sum(dy * outs[h], axis=1, keepdims=True)))
                dact = jnp.concatenate(pieces, axis=1)
            dcv = dact * _dsilu(cv)
            dc_ref[:, idx * D:(idx + 1) * D] = dcv
            for j in range(_KC):
                dcw_ref[j:j + 1, idx * D:(idx + 1) * D] += jnp.sum(dcv * taps[j], axis=0, keepdims=True)

        ba = ba_ref[...]
        lane, beta, g = _gate_rows(ba, al_ref[...], dtb_ref[...], H)
        dbg_ = dbg_ref[...]
        is_b, is_a = lane < H, jnp.logical_and(lane >= H, lane < 2 * H)
        da = dbg_ * (-jnp.exp(al_ref[...])) * _sigmoid(ba + dtb_ref[...])
        dba_ref[...] = jnp.where(is_b, dbg_ * beta * (1.0 - beta), jnp.where(is_a, da, 0.0)).astype(dba_ref.dtype)
        dal_ref[...] += jnp.sum(jnp.where(is_a, dbg_ * g, 0.0), axis=0, keepdims=True)
        ddt_ref[...] += jnp.sum(jnp.where(is_a, da, 0.0), axis=0, keepdims=True)

    return pl.pallas_call(
        body, name=name, grid=(T // tb,),
        in_specs=[_rspec(tb, D, 2), _rspec(tb, D, 3), _rspec(tb, D, 4),
                  _prev_spec(tb, D, 2), _prev_spec(tb, D, 3), _prev_spec(tb, D, 4),
                  _rspec(tb, _LANE, bac), _rspec(tb, D), _rspec(tb, D), _rspec(tb, D), _rspec(tb, _LANE),
                  _lspec((_KC, 3 * D), li), _lspec((1, _LANE), li), _lspec((1, _LANE), li), _HBM],
        out_specs=[_rspec(tb, 3 * D), _rspec(tb, _LANE, bac), _fspec((_KC, 3 * D)), _fspec((1, _LANE)),
                   _fspec((1, _LANE))],
        out_shape=[jax.ShapeDtypeStruct((T, 3 * D), _F32), jax.ShapeDtypeStruct(into.shape, into.dtype),
                   jax.ShapeDtypeStruct((_KC, 3 * D), _F32), jax.ShapeDtypeStruct((1, _LANE), _F32),
                   jax.ShapeDtypeStruct((1, _LANE), _F32)],
        input_output_aliases={14: 1}, compiler_params=_cp("arbitrary"),
    )(proj, proj, proj, proj, proj, proj, proj, dqn, dkn, dvs, dbg, cw, alog_row, dtb_row, into)


def _conv_bwd2(dc, cw, li, into, name, tb=256):
    T, W3 = dc.shape
    W = W3 // 3
    tb = _pick(T, tb, 8)
    nb8 = T // 8
    nrow = T // tb

    def body(dc_ref, nx_ref, cw_ref, _, o_ref):
        last = pl.program_id(0) == nrow - 1
        full = jnp.concatenate([dc_ref[...], jnp.where(last, 0.0, nx_ref[...])], axis=0)
        w = cw_ref[...]
        acc = full[:tb] * w[_KC - 1:_KC, :]
        for j in range(_KC - 1):
            sh = _KC - 1 - j
            acc = acc + pltpu.roll(full, tb + 8 - sh, 0)[:tb] * w[j:j + 1, :]
        o_ref[...] = acc.astype(o_ref.dtype)

    return pl.pallas_call(
        body, name=name, grid=(nrow, 3),
        in_specs=[pl.BlockSpec((tb, W), lambda i, j: (i, j)),
                  pl.BlockSpec((8, W), lambda i, j: (jnp.minimum((i + 1) * (tb // 8), nb8 - 1), j)),
                  pl.BlockSpec((None, _KC, W), lambda i, j: (li, 0, j)), _HBM],
        out_specs=pl.BlockSpec((tb, W), lambda i, j: (i, 2 + j)),
        out_shape=jax.ShapeDtypeStruct(into.shape, into.dtype), input_output_aliases={3: 0},
        compiler_params=_cp("parallel", "parallel"),
    )(dc, dc, cw, into)


def _split(a):
    hi = a.astype(_BF)
    return hi, (a - hi.astype(_F32)).astype(_BF)


def _dot3(a, b):
    (ah, al), (bh, bl) = a, b
    f = functools.partial(lax.dot_general, dimension_numbers=_NN, preferred_element_type=_F32)
    return f(ah, bh) + f(ah, bl) + f(al, bh)


def _inv_unit_lower(mats):
    C = mats[0].shape[0]
    ii = lax.broadcasted_iota(jnp.int32, (C, C), 0)
    jj = lax.broadcasted_iota(jnp.int32, (C, C), 1)
    xs = [jnp.where(ii == jj, 1.0, 0.0) - a for a in mats]
    ps = list(mats)
    n = 1
    while 2 * n < C:
        sp = [_split(p) for p in ps]
        ps = [_dot3(s, s) for s in sp]
        sp = [_split(p) for p in ps]
        xs = [x + _dot3(_split(x), s) for x, s in zip(xs, sp)]
        n *= 2
    return xs


def _gdn_chunk(q, k, v, g_row, b_row):
    C = q.shape[0]
    ii = lax.broadcasted_iota(jnp.int32, (C, C), 0)
    jj = lax.broadcasted_iota(jnp.int32, (C, C), 1)
    low, strict, eye = jj <= ii, jj < ii, ii == jj
    g_col = jnp.sum(jnp.where(eye, g_row, 0.0), axis=1, keepdims=True)
    b_col = jnp.sum(jnp.where(eye, b_row, 0.0), axis=1, keepdims=True)
    gam_col = jnp.sum(jnp.where(low, g_row, 0.0), axis=1, keepdims=True)
    gam_row = jnp.sum(jnp.where(jj >= ii, g_col, 0.0), axis=0, keepdims=True)
    gam_last = jnp.sum(g_row, axis=1, keepdims=True)
    decay = jnp.where(low, jnp.exp(jnp.where(low, gam_col - gam_row, 0.0)), 0.0)
    eg = jnp.exp(gam_col)
    ekd = jnp.exp(gam_last - gam_col)
    qs = q * (_GD ** -0.5)
    kb = k * b_col
    kk = _dot(kb, k, _NT)
    qkraw = _dot(qs, k, _NT)
    return dict(low=low, strict=strict, eye=eye, ii=ii, jj=jj, b_col=b_col, decay=decay, eg=eg, ekd=ekd,
                gl=jnp.exp(gam_last), qs=qs, kb=kb, kk=kk, qkraw=qkraw,
                A=jnp.where(strict, kk * decay, 0.0), vb=v * b_col, kbg=kb * eg,
                qk=qkraw * decay, q_dec=qs * eg, k_dec=k * ekd)


def _gdn_fwd(qn, kn, vs, g_r, b_r, name):
    T, D = qn.shape
    H, N, C = D // _GD, T // _BC, _BC
    hb = min(_HB, H)

    def body(q_ref, k_ref, v_ref, g_ref, b_ref, o_ref, s_ref, t_ref, S):
        @pl.when(pl.program_id(1) == 0)
        def _():
            S[...] = jnp.zeros_like(S)

        hs = range(hb)
        sls = [slice(hh * _GD, (hh + 1) * _GD) for hh in hs]
        cms = [_gdn_chunk(q_ref[:, sl], k_ref[:, sl], v_ref[:, sl], g_ref[hh], b_ref[hh]) for hh, sl in zip(hs, sls)]
        tms = _inv_unit_lower([cm["A"] for cm in cms])
        us = [_dot(tm, cm["vb"]) for tm, cm in zip(tms, cms)]
        ws = [_dot(tm, cm["kbg"]) for tm, cm in zip(tms, cms)]
        s0s = [S[hh] for hh in hs]
        for hh in hs:
            s_ref[hh] = s0s[hh]
            t_ref[hh] = tms[hh]
        v_news = [u - _dot(w, s0) for u, w, s0 in zip(us, ws, s0s)]
        qss = [_dot(cm["q_dec"], s0) for cm, s0 in zip(cms, s0s)]
        for hh in hs:
            o_ref[:, sls[hh]] = qss[hh] + _dot(cms[hh]["qk"], v_news[hh])
        for hh in hs:
            S[hh] = s0s[hh] * cms[hh]["gl"] + _dot(cms[hh]["k_dec"], v_news[hh], _TN)

    qspec = pl.BlockSpec((C, hb * _GD), lambda h, n: (n, h))
    gspec = pl.BlockSpec((hb, None, 1, C), lambda h, n: (h, n, 0, 0))
    return pl.pallas_call(
        body, name=name, grid=(H // hb, N),
        in_specs=[qspec, qspec, qspec, gspec, gspec],
        out_specs=[qspec, pl.BlockSpec((hb, None, _GD, _GD), lambda h, n: (h, n, 0, 0)),
                   pl.BlockSpec((hb, None, C, C), lambda h, n: (h, n, 0, 0))],
        out_shape=[jax.ShapeDtypeStruct((T, D), _F32), jax.ShapeDtypeStruct((H, N, _GD, _GD), _F32),
                   jax.ShapeDtypeStruct((H, N, C, C), _F32)],
        scratch_shapes=[pltpu.VMEM((hb, _GD, _GD), _F32)],
        compiler_params=_cp("arbitrary", "arbitrary"),
    )(qn, kn, vs, g_r, b_r)


def _gdn_bwd(qn, kn, vs, g_r, b_r, s_all, t_all, do, name):
    T, D = qn.shape
    H, N, C = D // _GD, T // _BC, _BC
    hb = min(_HB, H)

    def body(q_ref, k_ref, v_ref, g_ref, b_ref, s_ref, t_ref, do_ref, dq_ref, dk_ref, dv_ref, dg_ref, db_ref, dS):
        @pl.when(pl.program_id(1) == 0)
        def _():
            dS[...] = jnp.zeros_like(dS)

        hs = range(hb)
        sls = [slice(hh * _GD, (hh + 1) * _GD) for hh in hs]
        ks = [k_ref[:, sl] for sl in sls]
        vs_ = [v_ref[:, sl] for sl in sls]
        cms = [_gdn_chunk(q_ref[:, sl], k, v, g_ref[hh], b_ref[hh]) for hh, sl, k, v in zip(hs, sls, ks, vs_)]
        low, strict, eye, ii, jj = (cms[0][n] for n in ("low", "strict", "eye", "ii", "jj"))
        tms, s0s, dos, ds1s = [t_ref[hh] for hh in hs], [s_ref[hh] for hh in hs], [do_ref[:, sl] for sl in sls], \
            [dS[hh] for hh in hs]
        us = [_dot(tm, cm["vb"]) for tm, cm in zip(tms, cms)]
        ws = [_dot(tm, cm["kbg"]) for tm, cm in zip(tms, cms)]
        v_news = [u - _dot(w, s0) for u, w, s0 in zip(us, ws, s0s)]
        dv_news = [_dot(cm["qk"], do_, _TN) + _dot(cm["k_dec"], ds1) for cm, do_, ds1 in zip(cms, dos, ds1s)]
        dqks = [jnp.where(low, _dot(do_, vn, _NT), 0.0) for do_, vn in zip(dos, v_news)]
        dq_decs = [_dot(do_, s0, _NT) for do_, s0 in zip(dos, s0s)]
        dk_decs = [_dot(vn, ds1, _NT) for vn, ds1 in zip(v_news, ds1s)]
        dgls = [jnp.sum(jnp.sum(ds1 * s0, axis=1, keepdims=True), axis=0, keepdims=True) for ds1, s0 in zip(ds1s, s0s)]
        dws = [-_dot(dvn, s0, _NT) for dvn, s0 in zip(dv_news, s0s)]
        for hh in hs:
            dS[hh] = (_dot(cms[hh]["q_dec"], dos[hh], _TN) + cms[hh]["gl"] * ds1s[hh]
                      - _dot(ws[hh], dv_news[hh], _TN))
        dvbs = [_dot(tm, dvn, _TN) for tm, dvn in zip(tms, dv_news)]
        dkbgs = [_dot(tm, dw, _TN) for tm, dw in zip(tms, dws)]
        dAs = [-jnp.where(strict, _dot(dvb, u, _NT) + _dot(dkbg, w, _NT), 0.0)
               for dvb, u, dkbg, w in zip(dvbs, us, dkbgs, ws)]
        dkks = [dA * cm["decay"] for dA, cm in zip(dAs, cms)]
        dqkraws = [dqk * cm["decay"] for dqk, cm in zip(dqks, cms)]
        Es = [(dA * cm["kk"] + dqk * cm["qkraw"]) * cm["decay"] for dA, dqk, cm in zip(dAs, dqks, cms)]
        dkbs = [_dot(dkk, k) + dkbg * cm["eg"] for dkk, k, dkbg, cm in zip(dkks, ks, dkbgs, cms)]
        dqss = [_dot(dqr, k) + dqd * cm["eg"] for dqr, k, dqd, cm in zip(dqkraws, ks, dq_decs, cms)]
        for hh in hs:
            cm = cms[hh]
            dk_ref[:, sls[hh]] = (_dot(dqkraws[hh], cm["qs"], _TN) + _dot(dkks[hh], cm["kb"], _TN)
                                  + dk_decs[hh] * cm["ekd"] + dkbs[hh] * cm["b_col"])
            dv_ref[:, sls[hh]] = dvbs[hh] * cm["b_col"]
            dq_ref[:, sls[hh]] = dqss[hh] * (_GD ** -0.5)
        for hh in hs:
            cm, k, E = cms[hh], ks[hh], Es[hh]
            eg, ekd = cm["eg"], cm["ekd"]
            dbeta_col = jnp.sum(dvbs[hh] * vs_[hh] + dkbs[hh] * k, axis=1, keepdims=True)
            t_kd = jnp.sum(dk_decs[hh] * k, axis=1, keepdims=True) * ekd
            c1 = (jnp.sum(E, axis=1, keepdims=True) + jnp.sum(dkbgs[hh] * cm["kb"], axis=1, keepdims=True) * eg
                  + jnp.sum(dq_decs[hh] * cm["qs"], axis=1, keepdims=True) * eg - t_kd)
            r1 = jnp.sum(E, axis=0, keepdims=True)
            dgam_last = jnp.sum(t_kd, axis=0, keepdims=True) + dgls[hh] * cm["gl"]
            dgam_col = c1 - jnp.sum(jnp.where(eye, r1, 0.0), axis=1, keepdims=True)
            dg_ref[hh] = jnp.sum(jnp.where(ii >= jj, dgam_col, 0.0), axis=0, keepdims=True) + dgam_last
            db_ref[hh] = jnp.sum(jnp.where(eye, dbeta_col, 0.0), axis=0, keepdims=True)

    qspec = pl.BlockSpec((C, hb * _GD), lambda h, n: (N - 1 - n, h))
    gspec = pl.BlockSpec((hb, None, 1, C), lambda h, n: (h, N - 1 - n, 0, 0))
    return pl.pallas_call(
        body, name=name, grid=(H // hb, N),
        in_specs=[qspec, qspec, qspec, gspec, gspec,
                  pl.BlockSpec((hb, None, _GD, _GD), lambda h, n: (h, N - 1 - n, 0, 0)),
                  pl.BlockSpec((hb, None, C, C), lambda h, n: (h, N - 1 - n, 0, 0)), qspec],
        out_specs=[qspec, qspec, qspec, gspec, gspec],
        out_shape=[jax.ShapeDtypeStruct((T, D), _F32)] * 3 + [jax.ShapeDtypeStruct((H, N, 1, C), _F32)] * 2,
        scratch_shapes=[pltpu.VMEM((hb, _GD, _GD), _F32)],
        compiler_params=_cp("arbitrary", "arbitrary"),
    )(qn, kn, vs, g_r, b_r, s_all, t_all, do)


def _onorm_fwd(o, proj, go, li, D, name, tb=512):
    T = o.shape[0]
    H = D // _GD
    tb = _pick(T, tb, 8)

    def body(o_ref, z_ref, go_ref, y_ref):
        ov, zv, g = o_ref[...], z_ref[...], go_ref[...]
        for h in range(H):
            sl = slice(h * _GD, (h + 1) * _GD)
            oh = ov[:, sl]
            r = lax.rsqrt(jnp.mean(oh * oh, axis=1, keepdims=True) + _EPS)
            y_ref[:, sl] = (oh * r * g * _silu(zv[:, sl])).astype(y_ref.dtype)

    return pl.pallas_call(
        body, name=name, grid=(T // tb,), in_specs=[_rspec(tb, D), _rspec(tb, D, 5), _lspec((1, _GD), li)],
        out_specs=_rspec(tb, D), out_shape=jax.ShapeDtypeStruct((T, D), _MMT), compiler_params=_cp("parallel"),
    )(o, proj, go)


def _onorm_bwd(dy, o, proj, go, li, D, into, name, tb=256):
    T = o.shape[0]
    H = D // _GD
    tb = _pick(T, tb, 8)

    def body(dy_ref, o_ref, z_ref, go_ref, _, do_ref, dz_ref, dgo_ref):
        @pl.when(pl.program_id(0) == 0)
        def _():
            dgo_ref[...] = jnp.zeros_like(dgo_ref)

        dyv, ov, zv, g = dy_ref[...].astype(_F32), o_ref[...], z_ref[...], go_ref[...]
        dgo = jnp.zeros((1, _GD), _F32)
        for h in range(H):
            sl = slice(h * _GD, (h + 1) * _GD)
            oh, zh, dyh = ov[:, sl], zv[:, sl], dyv[:, sl]
            r = lax.rsqrt(jnp.mean(oh * oh, axis=1, keepdims=True) + _EPS)
            on = oh * r
            sz = _silu(zh)
            dgo = dgo + jnp.sum(dyh * sz * on, axis=0, keepdims=True)
            don = dyh * sz * g
            do_ref[:, sl] = r * (don - on * jnp.mean(don * on, axis=1, keepdims=True))
            dz_ref[:, sl] = (dyh * on * g * _dsilu(zh)).astype(dz_ref.dtype)
        dgo_ref[...] += dgo

    return pl.pallas_call(
        body, name=name, grid=(T // tb,),
        in_specs=[_rspec(tb, D), _rspec(tb, D), _rspec(tb, D, 5), _lspec((1, _GD), li), _HBM],
        out_specs=[_rspec(tb, D), _rspec(tb, D, 5), _fspec((1, _GD))],
        out_shape=[jax.ShapeDtypeStruct((T, D), _F32), jax.ShapeDtypeStruct(into.shape, into.dtype),
                   jax.ShapeDtypeStruct((1, _GD), _F32)],
        input_output_aliases={4: 1}, compiler_params=_cp("arbitrary"),
    )(dy, o, proj, go, into)


def _merge(pa, pb, proj, D, name, tb=512):
    T = pa.shape[0]
    tb = _pick(T, tb, 8)

    def body(pa_ref, pb_ref, ga_ref, gb_ref, m_ref):
        m_ref[...] = (_sigmoid(ga_ref[...]) * pa_ref[...].astype(_F32)
                      + _sigmoid(gb_ref[...]) * pb_ref[...].astype(_F32)).astype(m_ref.dtype)

    return pl.pallas_call(
        body, name=name, grid=(T // tb,), in_specs=[_rspec(tb, D), _rspec(tb, D), _rspec(tb, D, 6), _rspec(tb, D, 7)],
        out_specs=_rspec(tb, D), out_shape=jax.ShapeDtypeStruct((T, D), _MMT), compiler_params=_cp("parallel"),
    )(pa, pb, proj, proj)


def _merge_bwd(dm, pa, pb, proj, D, name, tb=256):
    T, PW = proj.shape
    tb = _pick(T, tb, 8)

    def body(dm_ref, pa_ref, pb_ref, ga_ref, gb_ref, dpa_ref, dpb_ref, dg_ref):
        d = dm_ref[...].astype(_F32)
        sa, sb = _sigmoid(ga_ref[...]), _sigmoid(gb_ref[...])
        dpa_ref[...] = (d * sa).astype(dpa_ref.dtype)
        dpb_ref[...] = (d * sb).astype(dpb_ref.dtype)
        dg_ref[:, :D] = (d * pa_ref[...].astype(_F32) * sa * (1.0 - sa)).astype(dg_ref.dtype)
        dg_ref[:, D:] = (d * pb_ref[...].astype(_F32) * sb * (1.0 - sb)).astype(dg_ref.dtype)

    return pl.pallas_call(
        body, name=name, grid=(T // tb,),
        in_specs=[_rspec(tb, D), _rspec(tb, D), _rspec(tb, D), _rspec(tb, D, 6), _rspec(tb, D, 7)],
        out_specs=[_rspec(tb, D), _rspec(tb, D), _rspec(tb, 2 * D, 3)],
        out_shape=[jax.ShapeDtypeStruct((T, D), _MMT)] * 2 + [jax.ShapeDtypeStruct((T, PW), _MMT)],
        compiler_params=_cp("parallel"),
    )(dm, pa, pb, proj, proj)


def _swiglu(gu, name, tb=256):
    T, F2 = gu.shape
    F = F2 // 2
    tb = _pick(T, tb, 8)

    def body(g_ref, u_ref, a_ref):
        a_ref[...] = (_silu(g_ref[...].astype(_F32)) * u_ref[...].astype(_F32)).astype(a_ref.dtype)

    return pl.pallas_call(
        body, name=name, grid=(T // tb,), in_specs=[_rspec(tb, F, 0), _rspec(tb, F, 1)],
        out_specs=_rspec(tb, F), out_shape=jax.ShapeDtypeStruct((T, F), _MMT), compiler_params=_cp("parallel"),
    )(gu, gu)


def _swiglu_bwd(da, gu, name, tb=256):
    T, F2 = gu.shape
    F = F2 // 2
    tb = _pick(T, tb, 8)

    def body(da_ref, g_ref, u_ref, o_ref):
        d, g = da_ref[...].astype(_F32), g_ref[...].astype(_F32)
        o_ref[:, :F] = (d * u_ref[...].astype(_F32) * _dsilu(g)).astype(o_ref.dtype)
        o_ref[:, F:] = (d * _silu(g)).astype(o_ref.dtype)

    return pl.pallas_call(
        body, name=name, grid=(T // tb,), in_specs=[_rspec(tb, F), _rspec(tb, F, 0), _rspec(tb, F, 1)],
        out_specs=_rspec(tb, F2), out_shape=jax.ShapeDtypeStruct((T, F2), _MMT), compiler_params=_cp("parallel"),
    )(da, gu, gu)


def _loss_head(x, tgt, fg, name, tb=256):
    T, D = x.shape
    tb = _pick(T, tb, 8)

    def body(x_ref, t_ref, fg_ref, loss_ref, dx_ref, dfg_ref):
        @pl.when(pl.program_id(0) == 0)
        def _():
            loss_ref[...] = jnp.zeros_like(loss_ref)
            dfg_ref[...] = jnp.zeros_like(dfg_ref)

        xv, fg_ = x_ref[...], fg_ref[...]
        r = lax.rsqrt(jnp.mean(xv * xv, axis=1, keepdims=True) + _EPS)
        xn = xv * r
        e = xn * fg_ - t_ref[...]
        loss_ref[...] += (0.5 / D) * jnp.sum(jnp.sum(e * e, axis=1, keepdims=True), axis=0, keepdims=True)
        dy = e * (1.0 / D)
        dfg_ref[...] += jnp.sum(dy * xn, axis=0, keepdims=True)
        dxn = dy * fg_
        dx_ref[...] = r * (dxn - xn * jnp.mean(dxn * xn, axis=1, keepdims=True))

    return pl.pallas_call(
        body, name=name, grid=(T // tb,), in_specs=[_rspec(tb, D), _rspec(tb, D), _fspec((1, D))],
        out_specs=[_fspec((1, 1)), _rspec(tb, D), _fspec((1, D))],
        out_shape=[jax.ShapeDtypeStruct((1, 1), _F32), jax.ShapeDtypeStruct((T, D), _F32),
                   jax.ShapeDtypeStruct((1, D), _F32)],
        compiler_params=_cp("arbitrary"),
    )(x, tgt, fg)


def _row_tile(R, W, budget=1 << 20, unit=8):
    if R * W * 4 <= budget or R % unit:
        return R
    best = unit
    for t in range(unit, R + 1, unit):
        if R % t == 0 and t * W * 4 <= budget:
            best = t
    return best


def _add_own_half(send, got, half, name):
    P, Rp, W = send.shape
    Rh = Rp // 2
    tb = _row_tile(Rh, W, 1 << 21, 16)

    def body(h_ref, a_ref, b_ref, o_ref):
        o_ref[...] = (a_ref[...].astype(_F32) + b_ref[...].astype(_F32)).astype(o_ref.dtype)

    return pl.pallas_call(
        body, name=name,
        grid_spec=pltpu.PrefetchScalarGridSpec(
            num_scalar_prefetch=1, grid=(P, Rh // tb),
            in_specs=[pl.BlockSpec((None, None, tb, W), lambda k, i, h: (k, h[0], i, 0)),
                      pl.BlockSpec((None, tb, W), lambda k, i, h: (k, i, 0))],
            out_specs=pl.BlockSpec((None, tb, W), lambda k, i, h: (k, i, 0))),
        out_shape=jax.ShapeDtypeStruct((P, Rh, W), send.dtype), compiler_params=_cp("parallel", "parallel"),
    )(half, send.reshape(P, 2, Rh, W), got)


def _sum_slots(st, name):
    P, R, W = st.shape
    tb = _row_tile(R, W, 1 << 20, 16)

    def body(s_ref, o_ref):
        acc = s_ref[0].astype(_F32)
        for p in range(1, P):
            acc = acc + s_ref[p].astype(_F32)
        o_ref[...] = acc

    return pl.pallas_call(
        body, name=name, grid=(R // tb,), in_specs=[pl.BlockSpec((P, tb, W), lambda i: (0, i, 0))],
        out_specs=_rspec(tb, W), out_shape=jax.ShapeDtypeStruct((R, W), _F32), compiler_params=_cp("parallel"),
    )(st)


def _adamw(w, gst, m, v, name):
    R, W = w.shape
    P = gst.shape[0]
    tb = _row_tile(R, W, 1 << 19)
    c1, c2 = 1.0 - _B1 ** _STEP, 1.0 - _B2 ** _STEP

    def body(w_ref, g_ref, m_ref, v_ref, go_ref, d_ref, mo_ref, vo_ref):
        g = g_ref[0]
        for p in range(1, P):
            g = g + g_ref[p]
        mn = _B1 * m_ref[...] + (1.0 - _B1) * g
        vn = _B2 * v_ref[...] + (1.0 - _B2) * (g * g)
        go_ref[...] = g
        mo_ref[...] = mn
        vo_ref[...] = vn
        d_ref[...] = -_LR * ((mn / c1) / (jnp.sqrt(vn / c2) + _AEPS) + _WD * w_ref[...])

    spec = _rspec(tb, W)
    return pl.pallas_call(
        body, name=name, grid=(R // tb,),
        in_specs=[spec, pl.BlockSpec((P, tb, W), lambda i: (0, i, 0)), spec, spec],
        out_specs=[spec] * 4, out_shape=[jax.ShapeDtypeStruct((R, W), _F32)] * 4, compiler_params=_cp("parallel"),
    )(w, gst, m, v)


def _as2d(a):
    if a.ndim == 1:
        return a.reshape(1, -1)
    return a.reshape(-1, a.shape[-1])


def kernel(x, c, ada_w, ada_b, norm1_g, w_in, conv_w, spatial_w, spatial_b, v_norm_g, a_log, dt_bias, o_norm_g, w_branch_a, w_branch_b, w_out, norm2_g, w_ffn_in, w_ffn_out, final_g, loss_target, m_ada_w, m_ada_b, m_norm1_g, m_w_in, m_conv_w, m_spatial_w, m_spatial_b, m_v_norm_g, m_a_log, m_dt_bias, m_o_norm_g, m_w_branch_a, m_w_branch_b, m_w_out, m_norm2_g, m_w_ffn_in, m_w_ffn_out, m_final_g, v_ada_w, v_ada_b, v_norm1_g, v_w_in, v_conv_w, v_spatial_w, v_spatial_b, v_v_norm_g, v_a_log, v_dt_bias, v_o_norm_g, v_w_branch_a, v_w_branch_b, v_w_out, v_norm2_g, v_w_ffn_in, v_w_ffn_out, v_final_g):
    xb, tgt = x[0], loss_target[0]
    T, D = xb.shape
    L, H, G = ada_w.shape[0], a_log.shape[1], spatial_w.shape[1]
    F = 4 * w_ffn_out.shape[1]
    N = T // _BC
    Ws = ada_w.shape[2]
    Wc = w_in.shape[2]
    PW = 8 * D + _LANE
    ix, iy, ic = lax.axis_index("x"), lax.axis_index("y"), lax.axis_index("c")
    me = 4 * ix + 2 * iy + ic

    c_all = _gather8(c, "gather_c").reshape(8, D)
    modp = _ada_fwd(c_all, ada_w, "ada_fwd")
    n_mod, n_cw = L * 8 * Ws, L * _KC * conv_w.shape[2]
    pad = (-(n_mod + n_cw)) % _LANE
    pay = jnp.concatenate([modp.reshape(-1), conv_w.reshape(-1), jnp.zeros((pad,), _F32)]).reshape(-1, _LANE)
    pay_all = _gather8(pay, "gather_mod").reshape(8, -1)
    mod_full = jnp.concatenate([pay_all[2 * k, :n_mod].reshape(L, 8, Ws) for k in range(4)], axis=-1)
    cw_full = jnp.concatenate([pay_all[2 * k, n_mod:n_mod + n_cw].reshape(L, _KC, -1) for k in range(4)], axis=-1)
    mod = lax.dynamic_index_in_dim(mod_full, me, axis=1, keepdims=False) + ada_b
    mods = [[mod[l, j * D:(j + 1) * D].reshape(1, D) for j in range(6)] for l in range(L)]

    big = [w_in, w_branch_a, w_branch_b, w_out, w_ffn_in, w_ffn_out]
    chip = 2 * ix + iy
    starts = [(k * Wc) // 16 * 16 for k in range(4)]
    Hh = max(-(-((k + 1) * Wc) // 16) * 16 - starts[k] for k in range(4))
    No = max(s + Hh for s in starts)
    my_off = jnp.asarray([k * Wc - starts[k] for k in range(4)], jnp.int32)[chip]
    cuts = sorted(set(starts + [s + Hh for s in starts]))

    pers = [Hh, D // 4, D // 4, D // 4, 2 * F // 4, F // 4]
    roff = [0]
    for p in pers:
        roff.append(roff[-1] + L * p)
    Rp = -(-roff[-1] // (32 * _NCH)) * (32 * _NCH)
    rpad = Rp - roff[-1]

    hull = lax.dynamic_update_slice(jnp.zeros((L, Hh, D), _F32), jnp.transpose(w_in, (0, 2, 1)), (0, my_off, 0))
    shard = jnp.concatenate(
        [hull.reshape(-1, D).astype(_MMT), w_branch_a.reshape(-1, D).astype(_MMT),
         w_branch_b.reshape(-1, D).astype(_MMT), w_out.reshape(-1, D).astype(_MMT),
         jnp.transpose(w_ffn_in, (0, 2, 1)).reshape(-1, D).astype(_MMT), w_ffn_out.reshape(-1, D).astype(_MMT),
         jnp.zeros((rpad, D), _MMT)], axis=0)
    gw = _fill_from_sibling(_gather_chips(shard, "gather_w"), "gather_w_sib")

    def slab(i, l, k):
        a = roff[i] + l * pers[i]
        return gw[k, a:a + pers[i]]

    def joined(i, l):
        return jnp.concatenate([slab(i, l, k) for k in range(4)], axis=0)

    def orig_rows(hulls, a, b):
        edges = sorted(set([a, b] + [c_ for c_ in cuts if a < c_ < b]))
        out = []
        for lo, hi in zip(edges[:-1], edges[1:]):
            cov = [k for k in range(4) if starts[k] <= lo and hi <= starts[k] + Hh]
            piece = hulls[cov[0]][lo - starts[cov[0]]:hi - starts[cov[0]]]
            for k in cov[1:]:
                piece = piece + hulls[k][lo - starts[k]:hi - starts[k]]
            out.append(piece)
        return out

    wt_in_p = []
    for l in range(L):
        hulls = [slab(0, l, k) for k in range(4)]
        wt_in_p.append(jnp.concatenate(
            orig_rows(hulls, 0, 6 * D) + orig_rows(hulls, 6 * D + 2 * H, 8 * D + 2 * H)
            + orig_rows(hulls, 6 * D, 6 * D + 2 * H) + [jnp.zeros((_LANE - 2 * H, D), _MMT)], axis=0))
    w_a, w_b, w_o, wt_fi, w_fo = ([joined(i, l) for l in range(L)] for i in range(1, 6))

    sbt = jnp.transpose(spatial_b, (0, 2, 1))
    gv3, go3 = v_norm_g.reshape(L, 1, D), o_norm_g.reshape(L, 1, _GD)
    zpad = jnp.zeros((L, _LANE - 2 * H), _F32)
    alog_row = jnp.concatenate([jnp.zeros((L, H), _F32), a_log, zpad], axis=1).reshape(L, 1, _LANE)
    dtb_row = jnp.concatenate([jnp.zeros((L, H), _F32), dt_bias, zpad], axis=1).reshape(L, 1, _LANE)

    def rows_of(tok):
        return jnp.transpose(tok.reshape(N, _BC, H), (2, 0, 1)).reshape(H, N, 1, _BC)

    def toks_of(rows):
        return jnp.transpose(rows.reshape(H, N, _BC), (1, 2, 0)).reshape(T, H)

    saved = []
    xc = xb
    for l in range(L):
        sh1, sc1, gt1, sh2, sc2, gt2 = mods[l]
        g1, g2 = norm1_g[l].reshape(1, D), norm2_g[l].reshape(1, D)
        h = _norm_mod(xc, g1, sc1, sh1, f"norm1_{l}")
        proj = _mm(h, wt_in_p[l], f"proj_{l}", trans_b=True, tn=1664)
        ya = _gmlp_fwd(proj, spatial_w, sbt, gv3, l, D, f"gmlp_{l}")
        qn, kn, vs, bg = _conv_fwd(proj, cw_full, alog_row, dtb_row, l, D, f"conv_{l}")
        g_r, b_r = rows_of(bg[:, H:2 * H]), rows_of(bg[:, :H])
        o, s_all, t_all = _gdn_fwd(qn, kn, vs, g_r, b_r, f"gdn_{l}")
        yb = _onorm_fwd(o, proj, go3, l, D, f"onorm_{l}")
        pa = _mm(ya, w_a[l], f"bra_{l}", out_dtype=_MMT)
        pb = _mm(yb, w_b[l], f"brb_{l}", out_dtype=_MMT)
        mg = _merge(pa, pb, proj, D, f"merge_{l}")
        p1 = _mm(mg, w_o[l], f"wout_{l}")
        x1 = _resid(xc, p1, gt1, f"res1_{l}")
        h2 = _norm_mod(x1, g2, sc2, sh2, f"norm2_{l}")
        gu = _mm(h2, wt_fi[l], f"ffin_{l}", trans_b=True, out_dtype=_MMT)
        act = _swiglu(gu, f"swiglu_{l}")
        p2 = _mm(act, w_fo[l], f"ffout_{l}")
        x2 = _resid(x1, p2, gt2, f"res2_{l}")
        saved.append(dict(x=xc, h=h, proj=proj, ya=ya, yb=yb, qn=qn, kn=kn, vs=vs, g_r=g_r, b_r=b_r, o=o,
                          s_all=s_all, t_all=t_all, pa=pa, pb=pb, mg=mg, p1=p1, x1=x1, h2=h2, gu=gu, act=act, p2=p2))
        xc = x2

    loss11, dx, dfg = _loss_head(xc, tgt, final_g.reshape(1, D), "loss_head")
    loss = lax.psum(loss11[0, 0], ("x", "y", "c"))

    gbig = {k: [None] * L for k in ("w_in", "w_a", "w_b", "w_o", "w_fi", "w_fo")}
    small = {k: [None] * L for k in ("dmod", "n1", "n2", "sw", "sb", "gv", "cw", "al", "dt", "go")}
    for l in reversed(range(L)):
        sv = saved[l]
        sh1, sc1, gt1, sh2, sc2, gt2 = mods[l]
        g1, g2 = norm1_g[l].reshape(1, D), norm2_g[l].reshape(1, D)
        proj = sv["proj"]
        dp2, dgt2 = _resid_bwd(dx, sv["p2"], gt2, f"res2b_{l}")
        da = _mm(dp2, w_fo[l], f"ffoutb_{l}", trans_b=True, out_dtype=_MMT)
        gbig["w_fo"][l] = _mm_tn(sv["act"], dp2, f"ffoutw_{l}")
        dgu = _swiglu_bwd(da, sv["gu"], f"swiglub_{l}")
        dh2 = _mm(dgu, wt_fi[l], f"ffinb_{l}")
        gbig["w_fi"][l] = _mm_tn(dgu, sv["h2"], f"ffinw_{l}")
        dx1, dgm2, dsh2 = _norm_mod_bwd(sv["x1"], dh2, dx, g2, sc2, f"norm2b_{l}")
        dp1, dgt1 = _resid_bwd(dx1, sv["p1"], gt1, f"res1b_{l}")
        dmg = _mm(dp1, w_o[l], f"woutb_{l}", trans_b=True, out_dtype=_MMT)
        gbig["w_o"][l] = _mm_tn(sv["mg"], dp1, f"woutw_{l}")
        dpa, dpb, dproj = _merge_bwd(dmg, sv["pa"], sv["pb"], proj, D, f"mergeb_{l}")
        dya = _mm(dpa, w_a[l], f"brab_{l}", trans_b=True, out_dtype=_MMT)
        gbig["w_a"][l] = _mm_tn(sv["ya"], dpa, f"braw_{l}")
        dyb = _mm(dpb, w_b[l], f"brbb_{l}", trans_b=True, out_dtype=_MMT)
        gbig["w_b"][l] = _mm_tn(sv["yb"], dpb, f"brbw_{l}")
        dproj, dsw, dsa, dgv = _gmlp_bwd(proj, dya, spatial_w, sbt, gv3, l, D, dproj, f"gmlpb_{l}")
        do, dproj, dgo = _onorm_bwd(dyb, sv["o"], proj, go3, l, D, dproj, f"onormb_{l}")
        dqn, dkn, dvs, dg_r, db_r = _gdn_bwd(sv["qn"], sv["kn"], sv["vs"], sv["g_r"], sv["b_r"], sv["s_all"],
                                             sv["t_all"], do, f"gdnb_{l}")
        dbg = jnp.concatenate([toks_of(db_r), toks_of(dg_r), jnp.zeros((T, _LANE - 2 * H), _F32)], axis=1)
        dc, dproj, dcw, dal, ddt = _conv_bwd1(proj, dqn, dkn, dvs, dbg, cw_full, alog_row, dtb_row, l, D, dproj,
                                              f"convb_{l}")
        dproj = _conv_bwd2(dc, cw_full, l, dproj, f"convx_{l}")
        dh = _mm(dproj, wt_in_p[l], f"projb_{l}", tk=1664)
        gbig["w_in"][l] = _mm_tn(dproj, sv["h"], f"projw_{l}", tm=640)
        dx, dgm1, dsh1 = _norm_mod_bwd(sv["x"], dh, dx1, g1, sc1, f"norm1b_{l}")
        small["dmod"][l] = jnp.concatenate([dsh1, dgm1 * g1, dgt1, dsh2, dgm2 * g2, dgt2], axis=1)
        small["n1"][l], small["n2"][l] = dgm1 * (1.0 + sc1), dgm2 * (1.0 + sc2)
        small["sw"][l], small["gv"][l], small["cw"][l], small["go"][l] = dsw, dgv, dcw, dgo
        small["sb"][l] = jnp.transpose(dsa.reshape(_AC, G, _GD).sum(axis=-1))
        small["al"][l], small["dt"][l] = dal[:, H:2 * H], ddt[:, H:2 * H]
    grad_x = dx.reshape(1, T, D)

    names_small = ["dmod", "n1", "n2", "sw", "sb", "gv", "cw", "al", "dt", "go"]
    flat = [jnp.stack(small[k]).reshape(-1) for k in names_small] + [dfg.reshape(-1)]
    sizes = [f.shape[0] for f in flat]
    tot = sum(sizes)
    pad = (-tot) % 1024
    pay = jnp.concatenate(flat + [jnp.zeros((pad,), _F32)]).reshape(-1, 1024)
    sm_all = _gather8(pay, "gather_small").reshape(8, -1)
    offs = [0]
    for s in sizes:
        offs.append(offs[-1] + s)
    part = {k: sm_all[:, offs[i]:offs[i + 1]] for i, k in enumerate(names_small + ["fg"])}
    dmod_all = part["dmod"].reshape(8, L, 6 * D)

    outs = {}

    def update(nm, w, gst, m, v):
        shp = w.shape
        w2 = _as2d(w)
        g, d, mn, vn = _adamw(w2, gst.reshape((gst.shape[0],) + w2.shape), _as2d(m), _as2d(v), f"adamw_{nm}")
        outs[nm] = (g.reshape(shp), d.reshape(shp), mn.reshape(shp), vn.reshape(shp))

    chip = 2 * ix + iy
    dmod_t = jnp.transpose(dmod_all, (1, 0, 2))
    dmod_mine = lax.dynamic_slice_in_dim(dmod_t, chip * Ws, Ws, axis=2)
    g_ada_w = _ada_bwd(jnp.transpose(c_all), dmod_mine, "ada_bwd")
    update("ada_w", ada_w, g_ada_w[None], m_ada_w, v_ada_w)
    update("ada_b", ada_b, dmod_all, m_ada_b, v_ada_b)
    update("norm1_g", norm1_g, part["n1"], m_norm1_g, v_norm1_g)
    update("norm2_g", norm2_g, part["n2"], m_norm2_g, v_norm2_g)
    update("spatial_w", spatial_w, part["sw"], m_spatial_w, v_spatial_w)
    update("spatial_b", spatial_b, part["sb"], m_spatial_b, v_spatial_b)
    update("v_norm_g", v_norm_g, part["gv"], m_v_norm_g, v_v_norm_g)
    update("a_log", a_log, part["al"], m_a_log, v_a_log)
    update("dt_bias", dt_bias, part["dt"], m_dt_bias, v_dt_bias)
    update("o_norm_g", o_norm_g, part["go"], m_o_norm_g, v_o_norm_g)
    update("final_g", final_g, part["fg"], m_final_g, v_final_g)
    cw_cols = conv_w.shape[2]
    dcw_all = part["cw"].reshape(8, L, _KC, 4 * cw_cols)
    update("conv_w", conv_w, lax.dynamic_slice_in_dim(dcw_all, chip * cw_cols, cw_cols, axis=3), m_conv_w, v_conv_w)

    def hull_of(p, k):
        a, b = starts[k], starts[k] + Hh
        out = []
        for lo, hi, plo in ((0, 6 * D, 0), (6 * D, 6 * D + 2 * H, 8 * D), (6 * D + 2 * H, 8 * D + 2 * H, 6 * D),
                            (8 * D + 2 * H, No, None)):
            s, e = max(a, lo), min(b, hi)
            if s < e:
                out.append(jnp.zeros((e - s, D), _MMT) if plo is None else p[plo + s - lo:plo + e - lo].astype(_MMT))
        return out

    pieces = []
    for k in range(4):
        for l in range(L):
            pieces += hull_of(gbig["w_in"][l], k)
        for i, nm in enumerate(("w_a", "w_b", "w_o", "w_fi", "w_fo")):
            per = pers[i + 1]
            pieces += [gbig[nm][l][k * per:(k + 1) * per].astype(_MMT) for l in range(L)]
        pieces.append(jnp.zeros((rpad, D), _MMT))
    send = jnp.concatenate(pieces, axis=0).reshape(4, Rp, D)
    got = _send_half_to_sibling(send, "reduce_cores")
    chipsum = _add_own_half(send, got, ic.astype(jnp.int32).reshape(1), "add_cores")
    parts = _scatter_to_chips(chipsum, "reduce_chips")
    mine = _sum_slots(parts, "add_chips")
    other = _swap_with_sibling(mine, "swap_cores")
    first = ic == 0
    gsum = jnp.concatenate([jnp.where(first, mine, other), jnp.where(first, other, mine)], axis=0)
    big_names = ["w_in", "w_branch_a", "w_branch_b", "w_out", "w_ffn_in", "w_ffn_out"]
    big_m = [m_w_in, m_w_branch_a, m_w_branch_b, m_w_out, m_w_ffn_in, m_w_ffn_out]
    big_v = [v_w_in, v_w_branch_a, v_w_branch_b, v_w_out, v_w_ffn_in, v_w_ffn_out]
    for i, (nm, w, m, v) in enumerate(zip(big_names, big, big_m, big_v)):
        g = gsum[roff[i]:roff[i + 1]].reshape(L, pers[i], D)
        if i == 0:
            g = jnp.transpose(lax.dynamic_slice_in_dim(g, my_off, Wc, axis=1), (0, 2, 1))
        elif i == 4:
            g = jnp.transpose(g, (0, 2, 1))
        update(nm, w, g[None], m, v)

    order = ["ada_w", "ada_b", "norm1_g", "w_in", "conv_w", "spatial_w", "spatial_b", "v_norm_g", "a_log", "dt_bias",
             "o_norm_g", "w_branch_a", "w_branch_b", "w_out", "norm2_g", "w_ffn_in", "w_ffn_out", "final_g"]
    return (loss, grad_x, *[outs[n][0] for n in order], *[outs[n][1] for n in order],
            *[outs[n][2] for n in order], *[outs[n][3] for n in order])
```

```python
import functools
import math

import jax
import jax.numpy as jnp
from jax import lax
from jax.experimental import pallas as pl
from jax.experimental.pallas import tpu as pltpu

_F32 = jnp.float32
_BF = jnp.bfloat16
_MMT = jnp.bfloat16
_EPS = 1e-6
_GD = 128
_AC = 128
_BC = 64
_KC = 4
_HB = 8
_NCH = 8
_LANE = 128
_VMEM_LIMIT = 56 * 1024 * 1024

_LR, _B1, _B2, _AEPS, _WD, _STEP = 0.001, 0.9, 0.999, 1e-08, 0.01, 10

_NN = (((1,), (0,)), ((), ()))
_NT = (((1,), (1,)), ((), ()))
_TN = (((0,), (0,)), ((), ()))

_MESH = pl.DeviceIdType.MESH


def _cp(*sem):
    return pltpu.CompilerParams(dimension_semantics=tuple(sem), vmem_limit_bytes=_VMEM_LIMIT)


def _dot(a, b, dn=_NN):
    return lax.dot_general(a.astype(_MMT), b.astype(_MMT), dn, preferred_element_type=_F32)


def _pick(n, target, unit=_LANE):
    if n <= target:
        return n
    best = None
    for t in range(unit, target + 1, unit):
        if n % t == 0:
            best = t
    assert best is not None, (n, target)
    return best


def _sigmoid(x):
    return 1.0 / (1.0 + jnp.exp(-x))


def _silu(x):
    return x * _sigmoid(x)


def _dsilu(x):
    s = _sigmoid(x)
    return s * (1.0 + x * (1.0 - s))


_GK = math.sqrt(2.0 / math.pi)


def _gelu(x):
    return 0.5 * x * (1.0 + jnp.tanh(_GK * (x + 0.044715 * x * x * x)))


def _dgelu(x):
    t = jnp.tanh(_GK * (x + 0.044715 * x * x * x))
    return 0.5 * (1.0 + t) + 0.5 * x * (1.0 - t * t) * _GK * (1.0 + 3.0 * 0.044715 * x * x)


def _softplus(x):
    return jnp.maximum(x, 0.0) + jnp.log(1.0 + jnp.exp(-jnp.abs(x)))


def _rspec(tb, w, cb=0):
    return pl.BlockSpec((tb, w), lambda i: (i, cb))


def _fspec(shape):
    nd = len(shape)
    return pl.BlockSpec(tuple(shape), lambda i: (0,) * nd)


def _lspec(tail, li):
    nd = len(tail)
    return pl.BlockSpec((None,) + tuple(tail), lambda i: (li,) + (0,) * nd)


def _slot_all8(x, y, c):
    return 4 * x + 2 * y + c


def _gather8(v, name):
    R, W = v.shape

    def body(v_ref, o_ref, ssem, rsem, lsem):
        x, y, c = lax.axis_index("x"), lax.axis_index("y"), lax.axis_index("c")
        sib = (x, y, 1 - c)
        chips = _other_chips(x, y)

        def slot(px, py, pc):
            return o_ref.at[_slot_all8(px, py, pc)]

        own = pltpu.make_async_copy(v_ref, slot(x, y, c), lsem)
        own.start()
        started = [_rcopy(v_ref, slot(x, y, c), ssem.at[0], rsem.at[0], sib)]
        started += [_rcopy(v_ref, slot(x, y, c), ssem.at[1 + j], rsem.at[1 + j], (px, py, c))
                    for j, (px, py) in enumerate(chips)]
        for cp in started:
            cp.start()
        for j, (px, py) in enumerate(chips):
            blk = slot(px, py, c)
            _rcopy(blk, blk, ssem.at[1 + j], rsem.at[1 + j], (px, py, c)).wait_recv()
            fw = _rcopy(blk, blk, ssem.at[4 + j], rsem.at[4 + j], sib)
            fw.start()
            started.append(fw)
        blk = slot(x, y, 1 - c)
        _rcopy(blk, blk, ssem.at[0], rsem.at[0], sib).wait_recv()
        for j, (px, py) in enumerate(chips):
            blk = slot(px, py, 1 - c)
            _rcopy(blk, blk, ssem.at[4 + j], rsem.at[4 + j], sib).wait_recv()
        for cp in started:
            cp.wait_send()
        own.wait()

    return pl.pallas_call(
        body, name=name, out_shape=jax.ShapeDtypeStruct((8, R, W), v.dtype), in_specs=[_HBM], out_specs=_HBM,
        scratch_shapes=[pltpu.SemaphoreType.DMA((7,)), pltpu.SemaphoreType.DMA((7,)), pltpu.SemaphoreType.DMA],
    )(v)


def _rcopy(src, dst, ssem, rsem, dev):
    return pltpu.make_async_remote_copy(src_ref=src, dst_ref=dst, send_sem=ssem, recv_sem=rsem,
                                        device_id=dev, device_id_type=_MESH)


def _other_chips(x, y):
    return [(1 - x, y), (x, 1 - y), (1 - x, 1 - y)]


_HBM = pl.BlockSpec(memory_space=pl.ANY)


def _gather_chips(shard, name):
    Rp, W = shard.shape
    Rh = Rp // 2
    rc = Rh // _NCH
    hq = _NCH // 2

    def body(s_ref, o_ref, ssem, rsem, lsem):
        x, y, c = lax.axis_index("x"), lax.axis_index("y"), lax.axis_index("c")
        chip = 2 * x + y
        xn, yn, dg = _other_chips(x, y)
        cx, cy, cd = 2 * xn[0] + xn[1], 2 * yn[0] + yn[1], 2 * dg[0] + dg[1]

        def rows(q):
            return pl.ds(c * Rh + q * rc, rc)

        locs = []
        for q in range(_NCH):
            lc = pltpu.make_async_copy(s_ref.at[rows(q)], o_ref.at[chip, rows(q)], lsem.at[q])
            lc.start()
            locs.append(lc)
        started = []
        for q in range(_NCH):
            for j, nb in ((0, xn), (1, yn)):
                cp = _rcopy(s_ref.at[rows(q)], o_ref.at[chip, rows(q)], ssem.at[j * _NCH + q], rsem.at[j * _NCH + q],
                            (nb[0], nb[1], c))
                cp.start()
                started.append(cp)
        for q in range(_NCH):
            bx = o_ref.at[cx, rows(q)]
            _rcopy(bx, bx, ssem.at[q], rsem.at[q], (xn[0], xn[1], c)).wait_recv()
            if q >= hq:
                rl = _rcopy(bx, bx, ssem.at[2 * _NCH + q], rsem.at[2 * _NCH + q], (yn[0], yn[1], c))
                rl.start()
                started.append(rl)
            by = o_ref.at[cy, rows(q)]
            _rcopy(by, by, ssem.at[_NCH + q], rsem.at[_NCH + q], (yn[0], yn[1], c)).wait_recv()
            if q < hq:
                rl = _rcopy(by, by, ssem.at[2 * _NCH + q], rsem.at[2 * _NCH + q], (xn[0], xn[1], c))
                rl.start()
                started.append(rl)
        for q in range(_NCH):
            bd = o_ref.at[cd, rows(q)]
            _rcopy(bd, bd, ssem.at[2 * _NCH + q], rsem.at[2 * _NCH + q], (dg[0], dg[1], c)).wait_recv()
        for cp in started:
            cp.wait_send()
        for lc in locs:
            lc.wait()

    return pl.pallas_call(
        body, name=name, out_shape=jax.ShapeDtypeStruct((4, Rp, W), shard.dtype), in_specs=[_HBM], out_specs=_HBM,
        scratch_shapes=[pltpu.SemaphoreType.DMA((3 * _NCH,))] * 2 + [pltpu.SemaphoreType.DMA((_NCH,))],
    )(shard)


def _fill_from_sibling(buf, name):
    P, Rp, W = buf.shape
    Rh = Rp // 2
    rc = Rh // _NCH

    def body(s_ref, o_ref, ssem, rsem):
        x, y, c = lax.axis_index("x"), lax.axis_index("y"), lax.axis_index("c")
        cps = []
        for k in range(P):
            for q in range(_NCH):
                r = pl.ds(c * Rh + q * rc, rc)
                cp = _rcopy(s_ref.at[k, r], o_ref.at[k, r], ssem.at[k * _NCH + q], rsem.at[k * _NCH + q],
                            (x, y, 1 - c))
                cp.start()
                cps.append(cp)
        for k in range(P):
            for q in range(_NCH):
                blk = o_ref.at[k, pl.ds((1 - c) * Rh + q * rc, rc)]
                _rcopy(blk, blk, ssem.at[k * _NCH + q], rsem.at[k * _NCH + q], (x, y, 1 - c)).wait_recv()
        for cp in cps:
            cp.wait_send()

    return pl.pallas_call(
        body, name=name, out_shape=jax.ShapeDtypeStruct(buf.shape, buf.dtype), in_specs=[_HBM], out_specs=_HBM,
        scratch_shapes=[pltpu.SemaphoreType.DMA((P * _NCH,))] * 2, input_output_aliases={0: 0},
    )(buf)


def _send_half_to_sibling(send, name):
    P, Rp, W = send.shape
    Rh = Rp // 2
    rc = Rh // _NCH

    def body(s_ref, o_ref, ssem, rsem):
        x, y, c = lax.axis_index("x"), lax.axis_index("y"), lax.axis_index("c")
        cps = []
        for k in range(P):
            for q in range(_NCH):
                cp = _rcopy(s_ref.at[k, pl.ds((1 - c) * Rh + q * rc, rc)], o_ref.at[k, pl.ds(q * rc, rc)],
                            ssem.at[k * _NCH + q], rsem.at[k * _NCH + q], (x, y, 1 - c))
                cp.start()
                cps.append(cp)
        for cp in cps:
            cp.wait()

    return pl.pallas_call(
        body, name=name, out_shape=jax.ShapeDtypeStruct((P, Rh, W), send.dtype), in_specs=[_HBM], out_specs=_HBM,
        scratch_shapes=[pltpu.SemaphoreType.DMA((P * _NCH,))] * 2,
    )(send)


def _scatter_to_chips(cs, name):
    P, Rh, W = cs.shape
    rc = Rh // _NCH

    def body(s_ref, o_ref, ssem, rsem, lsem):
        x, y, c = lax.axis_index("x"), lax.axis_index("y"), lax.axis_index("c")
        chip = 2 * x + y
        peers = _other_chips(x, y)
        locs = []
        for q in range(_NCH):
            r = pl.ds(q * rc, rc)
            lc = pltpu.make_async_copy(s_ref.at[chip, r], o_ref.at[chip, r], lsem.at[q])
            lc.start()
            locs.append(lc)
        cps = []
        for j, (px, py) in enumerate(peers):
            for q in range(_NCH):
                r = pl.ds(q * rc, rc)
                cp = _rcopy(s_ref.at[2 * px + py, r], o_ref.at[chip, r], ssem.at[j * _NCH + q], rsem.at[j * _NCH + q],
                            (px, py, c))
                cp.start()
                cps.append(cp)
        for j, (px, py) in enumerate(peers):
            for q in range(_NCH):
                blk = o_ref.at[2 * px + py, pl.ds(q * rc, rc)]
                _rcopy(blk, blk, ssem.at[j * _NCH + q], rsem.at[j * _NCH + q], (px, py, c)).wait_recv()
        for cp in cps:
            cp.wait_send()
        for lc in locs:
            lc.wait()

    return pl.pallas_call(
        body, name=name, out_shape=jax.ShapeDtypeStruct((P, Rh, W), cs.dtype), in_specs=[_HBM], out_specs=_HBM,
        scratch_shapes=[pltpu.SemaphoreType.DMA((3 * _NCH,))] * 2 + [pltpu.SemaphoreType.DMA((_NCH,))],
    )(cs)


def _swap_with_sibling(v, name):
    R, W = v.shape
    rc = R // _NCH

    def body(s_ref, o_ref, ssem, rsem):
        x, y, c = lax.axis_index("x"), lax.axis_index("y"), lax.axis_index("c")
        cps = []
        for q in range(_NCH):
            r = pl.ds(q * rc, rc)
            cp = _rcopy(s_ref.at[r], o_ref.at[r], ssem.at[q], rsem.at[q], (x, y, 1 - c))
            cp.start()
            cps.append(cp)
        for cp in cps:
            cp.wait()

    return pl.pallas_call(
        body, name=name, out_shape=jax.ShapeDtypeStruct((R, W), v.dtype), in_specs=[_HBM], out_specs=_HBM,
        scratch_shapes=[pltpu.SemaphoreType.DMA((_NCH,))] * 2,
    )(v)


def _mm(a, b, name, li=None, trans_b=False, out_dtype=_F32, tm=1024, tn=1024, tk=2048):
    M, K = a.shape
    bs = b.shape[-2:]
    N = bs[0] if trans_b else bs[1]
    tm, tn, tk = _pick(M, tm, 8), _pick(N, tn), _pick(K, tk)
    nk = K // tk
    lead = () if li is None else (None,)

    def bmap(i, j, k):
        idx = (j, k) if trans_b else (k, j)
        return idx if li is None else (li,) + idx

    def body(a_ref, b_ref, o_ref, acc):
        k = pl.program_id(2)
        part = lax.dot_general(a_ref[...], b_ref[...], _NT if trans_b else _NN, preferred_element_type=_F32)
        if nk == 1:
            o_ref[...] = part.astype(o_ref.dtype)
        else:
            @pl.when(k == 0)
            def _():
                acc[...] = part

            @pl.when(k > 0)
            def _():
                acc[...] += part

            @pl.when(k == nk - 1)
            def _():
                o_ref[...] = acc[...].astype(o_ref.dtype)

    return pl.pallas_call(
        body, name=name, grid=(M // tm, N // tn, nk),
        in_specs=[pl.BlockSpec((tm, tk), lambda i, j, k: (i, k)),
                  pl.BlockSpec(lead + ((tn, tk) if trans_b else (tk, tn)), bmap)],
        out_specs=pl.BlockSpec((tm, tn), lambda i, j, k: (i, j)),
        out_shape=jax.ShapeDtypeStruct((M, N), out_dtype),
        scratch_shapes=[pltpu.VMEM((tm, tn) if nk > 1 else (8, _LANE), _F32)],
        compiler_params=_cp("parallel", "parallel", "arbitrary"),
    )(a, b)


def _mm_tn(a, b, name, tm=512, tn=512):
    T, M = a.shape
    N = b.shape[1]
    tm, tn = _pick(M, tm), _pick(N, tn)

    def body(a_ref, b_ref, o_ref):
        o_ref[...] = lax.dot_general(a_ref[...], b_ref[...], _TN, preferred_element_type=_F32)

    return pl.pallas_call(
        body, name=name, grid=(M // tm, N // tn),
        in_specs=[pl.BlockSpec((T, tm), lambda i, j: (0, i)), pl.BlockSpec((T, tn), lambda i, j: (0, j))],
        out_specs=pl.BlockSpec((tm, tn), lambda i, j: (i, j)),
        out_shape=jax.ShapeDtypeStruct((M, N), _F32),
        compiler_params=_cp("parallel", "parallel"),
    )(a, b)


def _mm_res(a, b, x, gt, name, tm=1024, tn=1024, tk=2048):
    M, K = a.shape
    N = b.shape[1]
    tm, tn, tk = _pick(M, tm, 8), _pick(N, tn), _pick(K, tk)
    nk = K // tk

    def body(a_ref, b_ref, x_ref, gt_ref, p_ref, o_ref):
        k = pl.program_id(2)
        part = lax.dot_general(a_ref[...], b_ref[...], _NN, preferred_element_type=_F32)

        @pl.when(k == 0)
        def _():
            p_ref[...] = part

        @pl.when(k > 0)
        def _():
            p_ref[...] += part

        @pl.when(k == nk - 1)
        def _():
            o_ref[...] = x_ref[...] + gt_ref[...] * p_ref[...]

    tile = pl.BlockSpec((tm, tn), lambda i, j, k: (i, j))
    return pl.pallas_call(
        body, name=name, grid=(M // tm, N // tn, nk),
        in_specs=[pl.BlockSpec((tm, tk), lambda i, j, k: (i, k)), pl.BlockSpec((tk, tn), lambda i, j, k: (k, j)),
                  tile, pl.BlockSpec((1, tn), lambda i, j, k: (0, j))],
        out_specs=[tile, tile], out_shape=[jax.ShapeDtypeStruct((M, N), _F32)] * 2,
        compiler_params=_cp("parallel", "parallel", "arbitrary"),
    )(a, b, x, gt)


def _ffin_swiglu(a, wt, name, tm=1024, tn=1408):
    M, K = a.shape
    F = wt.shape[0] // 2
    tm, tn = _pick(M, tm, 8), _pick(F, tn)
    nj = F // tn

    def body(a_ref, bg_ref, bu_ref, g_ref, u_ref, act_ref):
        av = a_ref[...]
        g = lax.dot_general(av, bg_ref[...], _NT, preferred_element_type=_F32)
        u = lax.dot_general(av, bu_ref[...], _NT, preferred_element_type=_F32)
        g_ref[...] = g.astype(g_ref.dtype)
        u_ref[...] = u.astype(u_ref.dtype)
        act_ref[...] = (_silu(g) * u).astype(act_ref.dtype)

    tile = pl.BlockSpec((tm, tn), lambda i, j: (i, j))
    return pl.pallas_call(
        body, name=name, grid=(M // tm, nj),
        in_specs=[pl.BlockSpec((tm, K), lambda i, j: (i, 0)), pl.BlockSpec((tn, K), lambda i, j: (j, 0)),
                  pl.BlockSpec((tn, K), lambda i, j: (j + nj, 0))],
        out_specs=[tile] * 3, out_shape=[jax.ShapeDtypeStruct((M, F), _MMT)] * 3,
        compiler_params=_cp("parallel", "parallel"),
    )(a, wt, wt)


def _branch_merge(ya, yb, w_a, w_b, proj, name, tm=1024, tn=512):
    M, K = ya.shape
    N = w_a.shape[1]
    tm, tn = _pick(M, tm, 8), _pick(N, tn)
    nj = N // tn

    def body(ya_ref, yb_ref, wa_ref, wb_ref, ga_ref, gb_ref, pa_ref, pb_ref, m_ref):
        pa = lax.dot_general(ya_ref[...], wa_ref[...], _NN, preferred_element_type=_F32)
        pb = lax.dot_general(yb_ref[...], wb_ref[...], _NN, preferred_element_type=_F32)
        pa_ref[...] = pa.astype(pa_ref.dtype)
        pb_ref[...] = pb.astype(pb_ref.dtype)
        m_ref[...] = (_sigmoid(ga_ref[...]) * pa + _sigmoid(gb_ref[...]) * pb).astype(m_ref.dtype)

    row = pl.BlockSpec((tm, K), lambda i, j: (i, 0))
    col = pl.BlockSpec((K, tn), lambda i, j: (0, j))
    tile = pl.BlockSpec((tm, tn), lambda i, j: (i, j))
    return pl.pallas_call(
        body, name=name, grid=(M // tm, nj),
        in_specs=[row, row, col, col, pl.BlockSpec((tm, tn), lambda i, j: (i, 6 * nj + j)),
                  pl.BlockSpec((tm, tn), lambda i, j: (i, 7 * nj + j))],
        out_specs=[tile] * 3, out_shape=[jax.ShapeDtypeStruct((M, N), _MMT)] * 3,
        compiler_params=_cp("parallel", "parallel"),
    )(ya, yb, w_a, w_b, proj, proj)


def _ada_fwd(c_all, ada_w, name):
    L, D, Ws = ada_w.shape
    B = c_all.shape[0]

    def body(c_ref, w_ref, o_ref):
        o_ref[...] = _dot(_silu(c_ref[...]), w_ref[...])

    return pl.pallas_call(
        body, name=name, grid=(L,),
        in_specs=[_fspec((B, D)), pl.BlockSpec((None, D, Ws), lambda l: (l, 0, 0))],
        out_specs=pl.BlockSpec((None, B, Ws), lambda l: (l, 0, 0)),
        out_shape=jax.ShapeDtypeStruct((L, B, Ws), _F32), compiler_params=_cp("parallel"),
    )(c_all, ada_w)


def _ada_bwd(c_all_t, dmod, name):
    D, B = c_all_t.shape
    L, _, Ws = dmod.shape

    def body(c_ref, d_ref, o_ref):
        ct = _silu(c_ref[...])
        d = d_ref[...]
        acc = ct[:, 0:1] * d[0:1, :]
        for b in range(1, B):
            acc = acc + ct[:, b:b + 1] * d[b:b + 1, :]
        o_ref[...] = acc

    return pl.pallas_call(
        body, name=name, grid=(L,),
        in_specs=[_fspec((D, B)), pl.BlockSpec((None, B, Ws), lambda l: (l, 0, 0))],
        out_specs=pl.BlockSpec((None, D, Ws), lambda l: (l, 0, 0)),
        out_shape=jax.ShapeDtypeStruct((L, D, Ws), _F32), compiler_params=_cp("parallel"),
    )(c_all_t, dmod)


def _norm_mod(x, g, sc, sh, name, tb=512):
    T, D = x.shape
    tb = _pick(T, tb, 8)

    def body(x_ref, g_ref, sc_ref, sh_ref, h_ref):
        xv = x_ref[...]
        r = lax.rsqrt(jnp.mean(xv * xv, axis=1, keepdims=True) + _EPS)
        h_ref[...] = (xv * r * (g_ref[...] * (1.0 + sc_ref[...])) + sh_ref[...]).astype(h_ref.dtype)

    return pl.pallas_call(
        body, name=name, grid=(T // tb,),
        in_specs=[_rspec(tb, D), _fspec((1, D)), _fspec((1, D)), _fspec((1, D))],
        out_specs=_rspec(tb, D), out_shape=jax.ShapeDtypeStruct((T, D), _MMT), compiler_params=_cp("parallel"),
    )(x, g, sc, sh)


def _norm_mod_bwd(x, dh, dres, g, sc, name, tb=256):
    T, D = x.shape
    tb = _pick(T, tb, 8)

    def body(x_ref, dh_ref, dr_ref, g_ref, sc_ref, dx_ref, dgm_ref, dsh_ref):
        i = pl.program_id(0)
        xv, dh_ = x_ref[...], dh_ref[...]
        r = lax.rsqrt(jnp.mean(xv * xv, axis=1, keepdims=True) + _EPS)
        xn = xv * r
        dxn = dh_ * (g_ref[...] * (1.0 + sc_ref[...]))
        dx_ref[...] = dr_ref[...] + r * (dxn - xn * jnp.mean(dxn * xn, axis=1, keepdims=True))

        @pl.when(i == 0)
        def _():
            dgm_ref[...] = jnp.zeros_like(dgm_ref)
            dsh_ref[...] = jnp.zeros_like(dsh_ref)

        dgm_ref[...] += jnp.sum(dh_ * xn, axis=0, keepdims=True)
        dsh_ref[...] += jnp.sum(dh_, axis=0, keepdims=True)

    return pl.pallas_call(
        body, name=name, grid=(T // tb,),
        in_specs=[_rspec(tb, D), _rspec(tb, D), _rspec(tb, D), _fspec((1, D)), _fspec((1, D))],
        out_specs=[_rspec(tb, D), _fspec((1, D)), _fspec((1, D))],
        out_shape=[jax.ShapeDtypeStruct((T, D), _F32), jax.ShapeDtypeStruct((1, D), _F32),
                   jax.ShapeDtypeStruct((1, D), _F32)],
        compiler_params=_cp("arbitrary"),
    )(x, dh, dres, g, sc)


def _resid_bwd(dx, p, gt, name, tb=512):
    T, D = dx.shape
    tb = _pick(T, tb, 8)

    def body(dx_ref, p_ref, gt_ref, dp_ref, dgt_ref):
        i = pl.program_id(0)
        d = dx_ref[...]
        dp_ref[...] = (d * gt_ref[...]).astype(dp_ref.dtype)

        @pl.when(i == 0)
        def _():
            dgt_ref[...] = jnp.zeros_like(dgt_ref)

        dgt_ref[...] += jnp.sum(d * p_ref[...], axis=0, keepdims=True)

    return pl.pallas_call(
        body, name=name, grid=(T // tb,), in_specs=[_rspec(tb, D), _rspec(tb, D), _fspec((1, D))],
        out_specs=[_rspec(tb, D), _fspec((1, D))],
        out_shape=[jax.ShapeDtypeStruct((T, D), _MMT), jax.ShapeDtypeStruct((1, D), _F32)],
        compiler_params=_cp("arbitrary"),
    )(dx, p, gt)


def _gmlp_chunk(u_raw, v_raw, sw_ref, sbt, gv, G):
    u, v = _gelu(u_raw), _gelu(v_raw)
    ii = lax.broadcasted_iota(jnp.int32, (_AC, _AC), 0)
    jj = lax.broadcasted_iota(jnp.int32, (_AC, _AC), 1)
    out = []
    for gi in range(G):
        sl = slice(gi * _GD, (gi + 1) * _GD)
        vg = v[:, sl]
        r = lax.rsqrt(jnp.mean(vg * vg, axis=1, keepdims=True) + _EPS)
        vhat = vg * r
        W = jnp.where(jj <= ii, sw_ref[gi], 0.0)
        s = _dot(W, vhat * gv[:, sl]) + sbt[:, gi:gi + 1]
        out.append((u[:, sl], s, vhat, r, W))
    return out


def _gmlp_fwd(proj, sw, sbt, gv, li, D, name):
    T = proj.shape[0]
    G = D // _GD

    def body(u_ref, v_ref, sw_ref, sbt_ref, gv_ref, y_ref):
        parts = _gmlp_chunk(u_ref[...], v_ref[...], sw_ref, sbt_ref[...], gv_ref[...], G)
        for gi, (u, s, _, _, _) in enumerate(parts):
            y_ref[:, gi * _GD:(gi + 1) * _GD] = (u * s).astype(y_ref.dtype)

    return pl.pallas_call(
        body, name=name, grid=(T // _AC,),
        in_specs=[_rspec(_AC, D, 0), _rspec(_AC, D, 1), _lspec((G, _AC, _AC), li), _lspec((_AC, G), li),
                  _lspec((1, D), li)],
        out_specs=_rspec(_AC, D), out_shape=jax.ShapeDtypeStruct((T, D), _MMT), compiler_params=_cp("parallel"),
    )(proj, proj, sw, sbt, gv)


def _gmlp_bwd(proj, dy, sw, sbt, gv, li, D, into, name):
    T = proj.shape[0]
    G = D // _GD

    def body(u_ref, v_ref, dy_ref, sw_ref, sbt_ref, gv_ref, _, duv_ref, dsw_ref, dsa_ref, dgv_ref):
        i = pl.program_id(0)

        @pl.when(i == 0)
        def _():
            dsw_ref[...] = jnp.zeros_like(dsw_ref)
            dsa_ref[...] = jnp.zeros_like(dsa_ref)
            dgv_ref[...] = jnp.zeros_like(dgv_ref)

        u_raw, v_raw, dy_, gv_ = u_ref[...], v_ref[...], dy_ref[...].astype(_F32), gv_ref[...]
        parts = _gmlp_chunk(u_raw, v_raw, sw_ref, sbt_ref[...], gv_, G)
        ii = lax.broadcasted_iota(jnp.int32, (_AC, _AC), 0)
        jj = lax.broadcasted_iota(jnp.int32, (_AC, _AC), 1)
        dgu, dgv = _dgelu(u_raw), _dgelu(v_raw)
        for gi, (u, s, vhat, r, W) in enumerate(parts):
            sl = slice(gi * _GD, (gi + 1) * _GD)
            dyg = dy_[:, sl]
            ds = dyg * u
            vn = vhat * gv_[:, sl]
            dsw_ref[gi] += jnp.where(jj <= ii, _dot(ds, vn, _NT), 0.0)
            dsa_ref[:, sl] += ds
            dvn = _dot(W, ds, _TN)
            dgv_ref[:, sl] += jnp.sum(dvn * vhat, axis=0, keepdims=True)
            dvh = dvn * gv_[:, sl]
            dvg = r * (dvh - vhat * jnp.mean(dvh * vhat, axis=1, keepdims=True))
            duv_ref[:, sl] = (dyg * s * dgu[:, sl]).astype(duv_ref.dtype)
            duv_ref[:, D + gi * _GD:D + (gi + 1) * _GD] = (dvg * dgv[:, sl]).astype(duv_ref.dtype)

    return pl.pallas_call(
        body, name=name, grid=(T // _AC,),
        in_specs=[_rspec(_AC, D, 0), _rspec(_AC, D, 1), _rspec(_AC, D), _lspec((G, _AC, _AC), li),
                  _lspec((_AC, G), li), _lspec((1, D), li), _HBM],
        out_specs=[_rspec(_AC, 2 * D), _fspec((G, _AC, _AC)), _fspec((_AC, D)), _fspec((1, D))],
        out_shape=[jax.ShapeDtypeStruct(into.shape, into.dtype), jax.ShapeDtypeStruct((G, _AC, _AC), _F32),
                   jax.ShapeDtypeStruct((_AC, D), _F32), jax.ShapeDtypeStruct((1, D), _F32)],
        input_output_aliases={6: 0}, compiler_params=_cp("arbitrary"),
    )(proj, proj, dy, sw, sbt, gv, into)


def _conv_taps(halo, cur, first):
    tb = cur.shape[0]
    full = jnp.concatenate([jnp.where(first, 0.0, halo), cur], axis=0)
    return [full[8:] if j == _KC - 1 else pltpu.roll(full, _KC - 1 - j, 0)[8:] for j in range(_KC)]


def _prev_spec(tb, w, cb):
    return pl.BlockSpec((8, w), lambda i: (jnp.maximum(i * (tb // 8) - 1, 0), cb))


def _l2_heads(x, H):
    outs, rs = [], []
    for h in range(H):
        xh = x[:, h * _GD:(h + 1) * _GD]
        r = lax.rsqrt(jnp.sum(xh * xh, axis=1, keepdims=True) + _EPS)
        outs.append(xh * r)
        rs.append(r)
    return outs, rs


def _gate_rows(ba, alog_row, dtb_row, H):
    lane = lax.broadcasted_iota(jnp.int32, ba.shape, 1)
    beta = _sigmoid(ba)
    g = -jnp.exp(alog_row) * _softplus(ba + dtb_row)
    return lane, beta, g


def _conv_fwd(proj, cw, alog_row, dtb_row, li, D, name, tb=256):
    T = proj.shape[0]
    H = D // _GD
    tb = _pick(T, tb, 8)
    bac = (8 * D) // _LANE

    def body(q_ref, k_ref, v_ref, qh_ref, kh_ref, vh_ref, ba_ref, cw_ref, al_ref, dtb_ref,
             qo_ref, ko_ref, vo_ref, bg_ref):
        first = pl.program_id(0) == 0
        cw_ = cw_ref[...]
        for idx, (cur, halo, out) in enumerate(((q_ref, qh_ref, qo_ref), (k_ref, kh_ref, ko_ref),
                                                 (v_ref, vh_ref, vo_ref))):
            taps = _conv_taps(halo[...], cur[...], first)
            w = cw_[:, idx * D:(idx + 1) * D]
            cv = taps[0] * w[0:1, :]
            for j in range(1, _KC):
                cv = cv + taps[j] * w[j:j + 1, :]
            act = _silu(cv)
            if idx < 2:
                outs, _ = _l2_heads(act, H)
                for h in range(H):
                    out[:, h * _GD:(h + 1) * _GD] = outs[h]
            else:
                out[...] = act
        lane, beta, g = _gate_rows(ba_ref[...], al_ref[...], dtb_ref[...], H)
        bg_ref[...] = jnp.where(lane < H, beta, jnp.where(lane < 2 * H, g, 0.0))

    return pl.pallas_call(
        body, name=name, grid=(T // tb,),
        in_specs=[_rspec(tb, D, 2), _rspec(tb, D, 3), _rspec(tb, D, 4),
                  _prev_spec(tb, D, 2), _prev_spec(tb, D, 3), _prev_spec(tb, D, 4),
                  _rspec(tb, _LANE, bac), _lspec((_KC, 3 * D), li), _lspec((1, _LANE), li), _lspec((1, _LANE), li)],
        out_specs=[_rspec(tb, D), _rspec(tb, D), _rspec(tb, D), _rspec(tb, _LANE)],
        out_shape=[jax.ShapeDtypeStruct((T, D), _F32)] * 3 + [jax.ShapeDtypeStruct((T, _LANE), _F32)],
        compiler_params=_cp("parallel"),
    )(proj, proj, proj, proj, proj, proj, proj, cw, alog_row, dtb_row)


def _conv_bwd1(proj, dqn, dkn, dvs, dbg, cw, alog_row, dtb_row, li, D, into, name, tb=256):
    T = proj.shape[0]
    H = D // _GD
    tb = _pick(T, tb, 8)
    bac = (8 * D) // _LANE

    def body(q_ref, k_ref, v_ref, qh_ref, kh_ref, vh_ref, ba_ref, dq_ref, dk_ref, dv_ref, dbg_ref,
             cw_ref, al_ref, dtb_ref, _, dc_ref, dba_ref, dcw_ref, dal_ref, ddt_ref):
        i = pl.program_id(0)
        first = i == 0

        @pl.when(first)
        def _():
            dcw_ref[...] = jnp.zeros_like(dcw_ref)
            dal_ref[...] = jnp.zeros_like(dal_ref)
            ddt_ref[...] = jnp.zeros_like(ddt_ref)

        cw_ = cw_ref[...]
        for idx, (cur, halo, dref) in enumerate(((q_ref, qh_ref, dq_ref), (k_ref, kh_ref, dk_ref),
                                                  (v_ref, vh_ref, dv_ref))):
            taps = _conv_taps(halo[...], cur[...], first)
            w = cw_[:, idx * D:(idx + 1) * D]
            cv = taps[0] * w[0:1, :]
            for j in range(1, _KC):
                cv = cv + taps[j] * w[j:j + 1, :]
            dact = dref[...]
            if idx < 2:
                outs, rs = _l2_heads(_silu(cv), H)
                pieces = []
                for h in range(H):
                    dy = dact[:, h * _GD:(h + 1) * _GD]
                    pieces.append(rs[h] * (dy - outs[h] * jnp.sum(dy * outs[h], axis=1, keepdims=True)))
                dact = jnp.concatenate(pieces, axis=1)
            dcv = dact * _dsilu(cv)
            dc_ref[:, idx * D:(idx + 1) * D] = dcv
            for j in range(_KC):
                dcw_ref[j:j + 1, idx * D:(idx + 1) * D] += jnp.sum(dcv * taps[j], axis=0, keepdims=True)

        ba = ba_ref[...]
        lane, beta, g = _gate_rows(ba, al_ref[...], dtb_ref[...], H)
        dbg_ = dbg_ref[...]
        is_b, is_a = lane < H, jnp.logical_and(lane >= H, lane < 2 * H)
        da = dbg_ * (-jnp.exp(al_ref[...])) * _sigmoid(ba + dtb_ref[...])
        dba_ref[...] = jnp.where(is_b, dbg_ * beta * (1.0 - beta), jnp.where(is_a, da, 0.0)).astype(dba_ref.dtype)
        dal_ref[...] += jnp.sum(jnp.where(is_a, dbg_ * g, 0.0), axis=0, keepdims=True)
        ddt_ref[...] += jnp.sum(jnp.where(is_a, da, 0.0), axis=0, keepdims=True)

    return pl.pallas_call(
        body, name=name, grid=(T // tb,),
        in_specs=[_rspec(tb, D, 2), _rspec(tb, D, 3), _rspec(tb, D, 4),
                  _prev_spec(tb, D, 2), _prev_spec(tb, D, 3), _prev_spec(tb, D, 4),
                  _rspec(tb, _LANE, bac), _rspec(tb, D), _rspec(tb, D), _rspec(tb, D), _rspec(tb, _LANE),
                  _lspec((_KC, 3 * D), li), _lspec((1, _LANE), li), _lspec((1, _LANE), li), _HBM],
        out_specs=[_rspec(tb, 3 * D), _rspec(tb, _LANE, bac), _fspec((_KC, 3 * D)), _fspec((1, _LANE)),
                   _fspec((1, _LANE))],
        out_shape=[jax.ShapeDtypeStruct((T, 3 * D), _F32), jax.ShapeDtypeStruct(into.shape, into.dtype),
                   jax.ShapeDtypeStruct((_KC, 3 * D), _F32), jax.ShapeDtypeStruct((1, _LANE), _F32),
                   jax.ShapeDtypeStruct((1, _LANE), _F32)],
        input_output_aliases={14: 1}, compiler_params=_cp("arbitrary"),
    )(proj, proj, proj, proj, proj, proj, proj, dqn, dkn, dvs, dbg, cw, alog_row, dtb_row, into)


def _conv_bwd2(dc, cw, li, into, name, tb=256):
    T, W3 = dc.shape
    W = W3 // 3
    tb = _pick(T, tb, 8)
    nb8 = T // 8
    nrow = T // tb

    def body(dc_ref, nx_ref, cw_ref, _, o_ref):
        last = pl.program_id(0) == nrow - 1
        full = jnp.concatenate([dc_ref[...], jnp.where(last, 0.0, nx_ref[...])], axis=0)
        w = cw_ref[...]
        acc = full[:tb] * w[_KC - 1:_KC, :]
        for j in range(_KC - 1):
            sh = _KC - 1 - j
            acc = acc + pltpu.roll(full, tb + 8 - sh, 0)[:tb] * w[j:j + 1, :]
        o_ref[...] = acc.astype(o_ref.dtype)

    return pl.pallas_call(
        body, name=name, grid=(nrow, 3),
        in_specs=[pl.BlockSpec((tb, W), lambda i, j: (i, j)),
                  pl.BlockSpec((8, W), lambda i, j: (jnp.minimum((i + 1) * (tb // 8), nb8 - 1), j)),
                  pl.BlockSpec((None, _KC, W), lambda i, j: (li, 0, j)), _HBM],
        out_specs=pl.BlockSpec((tb, W), lambda i, j: (i, 2 + j)),
        out_shape=jax.ShapeDtypeStruct(into.shape, into.dtype), input_output_aliases={3: 0},
        compiler_params=_cp("parallel", "parallel"),
    )(dc, dc, cw, into)


def _split(a):
    hi = a.astype(_BF)
    return hi, (a - hi.astype(_F32)).astype(_BF)


def _dot3(a, b):
    (ah, al), (bh, bl) = a, b
    f = functools.partial(lax.dot_general, dimension_numbers=_NN, preferred_element_type=_F32)
    return f(ah, bh) + f(ah, bl) + f(al, bh)


def _inv_unit_lower(mats):
    C = mats[0].shape[0]
    ii = lax.broadcasted_iota(jnp.int32, (C, C), 0)
    jj = lax.broadcasted_iota(jnp.int32, (C, C), 1)
    xs = [jnp.where(ii == jj, 1.0, 0.0) - a for a in mats]
    ps = list(mats)
    n = 1
    while 2 * n < C:
        sp = [_split(p) for p in ps]
        ps = [_dot3(s, s) for s in sp]
        sp = [_split(p) for p in ps]
        xs = [x + _dot3(_split(x), s) for x, s in zip(xs, sp)]
        n *= 2
    return xs


def _gdn_chunk(q, k, v, g_row, b_row):
    C = q.shape[0]
    ii = lax.broadcasted_iota(jnp.int32, (C, C), 0)
    jj = lax.broadcasted_iota(jnp.int32, (C, C), 1)
    low, strict, eye = jj <= ii, jj < ii, ii == jj
    g_col = jnp.sum(jnp.where(eye, g_row, 0.0), axis=1, keepdims=True)
    b_col = jnp.sum(jnp.where(eye, b_row, 0.0), axis=1, keepdims=True)
    gam_col = jnp.sum(jnp.where(low, g_row, 0.0), axis=1, keepdims=True)
    gam_row = jnp.sum(jnp.where(jj >= ii, g_col, 0.0), axis=0, keepdims=True)
    gam_last = jnp.sum(g_row, axis=1, keepdims=True)
    decay = jnp.where(low, jnp.exp(jnp.where(low, gam_col - gam_row, 0.0)), 0.0)
    eg = jnp.exp(gam_col)
    ekd = jnp.exp(gam_last - gam_col)
    qs = q * (_GD ** -0.5)
    kb = k * b_col
    kk = _dot(kb, k, _NT)
    qkraw = _dot(qs, k, _NT)
    return dict(low=low, strict=strict, eye=eye, ii=ii, jj=jj, b_col=b_col, decay=decay, eg=eg, ekd=ekd,
                gl=jnp.exp(gam_last), qs=qs, kb=kb, kk=kk, qkraw=qkraw,
                A=jnp.where(strict, kk * decay, 0.0), vb=v * b_col, kbg=kb * eg,
                qk=qkraw * decay, q_dec=qs * eg, k_dec=k * ekd)


def _gdn_fwd(qn, kn, vs, g_r, b_r, name):
    T, D = qn.shape
    H, N, C = D // _GD, T // _BC, _BC
    hb = min(_HB, H)

    def body(q_ref, k_ref, v_ref, g_ref, b_ref, o_ref, s_ref, t_ref, S):
        @pl.when(pl.program_id(1) == 0)
        def _():
            S[...] = jnp.zeros_like(S)

        hs = range(hb)
        sls = [slice(hh * _GD, (hh + 1) * _GD) for hh in hs]
        cms = [_gdn_chunk(q_ref[:, sl], k_ref[:, sl], v_ref[:, sl], g_ref[hh], b_ref[hh]) for hh, sl in zip(hs, sls)]
        tms = _inv_unit_lower([cm["A"] for cm in cms])
        us = [_dot(tm, cm["vb"]) for tm, cm in zip(tms, cms)]
        ws = [_dot(tm, cm["kbg"]) for tm, cm in zip(tms, cms)]
        s0s = [S[hh] for hh in hs]
        for hh in hs:
            s_ref[hh] = s0s[hh]
            t_ref[hh] = tms[hh]
        v_news = [u - _dot(w, s0) for u, w, s0 in zip(us, ws, s0s)]
        qss = [_dot(cm["q_dec"], s0) for cm, s0 in zip(cms, s0s)]
        for hh in hs:
            o_ref[:, sls[hh]] = qss[hh] + _dot(cms[hh]["qk"], v_news[hh])
        for hh in hs:
            S[hh] = s0s[hh] * cms[hh]["gl"] + _dot(cms[hh]["k_dec"], v_news[hh], _TN)

    qspec = pl.BlockSpec((C, hb * _GD), lambda h, n: (n, h))
    gspec = pl.BlockSpec((hb, None, 1, C), lambda h, n: (h, n, 0, 0))
    return pl.pallas_call(
        body, name=name, grid=(H // hb, N),
        in_specs=[qspec, qspec, qspec, gspec, gspec],
        out_specs=[qspec, pl.BlockSpec((hb, None, _GD, _GD), lambda h, n: (h, n, 0, 0)),
                   pl.BlockSpec((hb, None, C, C), lambda h, n: (h, n, 0, 0))],
        out_shape=[jax.ShapeDtypeStruct((T, D), _F32), jax.ShapeDtypeStruct((H, N, _GD, _GD), _F32),
                   jax.ShapeDtypeStruct((H, N, C, C), _F32)],
        scratch_shapes=[pltpu.VMEM((hb, _GD, _GD), _F32)],
        compiler_params=_cp("arbitrary", "arbitrary"),
    )(qn, kn, vs, g_r, b_r)


def _gdn_bwd(qn, kn, vs, g_r, b_r, s_all, t_all, do, name):
    T, D = qn.shape
    H, N, C = D // _GD, T // _BC, _BC
    hb = min(_HB, H)

    def body(q_ref, k_ref, v_ref, g_ref, b_ref, s_ref, t_ref, do_ref, dq_ref, dk_ref, dv_ref, dg_ref, db_ref, dS):
        @pl.when(pl.program_id(1) == 0)
        def _():
            dS[...] = jnp.zeros_like(dS)

        hs = range(hb)
        sls = [slice(hh * _GD, (hh + 1) * _GD) for hh in hs]
        ks = [k_ref[:, sl] for sl in sls]
        vs_ = [v_ref[:, sl] for sl in sls]
        cms = [_gdn_chunk(q_ref[:, sl], k, v, g_ref[hh], b_ref[hh]) for hh, sl, k, v in zip(hs, sls, ks, vs_)]
        low, strict, eye, ii, jj = (cms[0][n] for n in ("low", "strict", "eye", "ii", "jj"))
        tms, s0s, dos, ds1s = [t_ref[hh] for hh in hs], [s_ref[hh] for hh in hs], [do_ref[:, sl] for sl in sls], \
            [dS[hh] for hh in hs]
        us = [_dot(tm, cm["vb"]) for tm, cm in zip(tms, cms)]
        ws = [_dot(tm, cm["kbg"]) for tm, cm in zip(tms, cms)]
        v_news = [u - _dot(w, s0) for u, w, s0 in zip(us, ws, s0s)]
        dv_news = [_dot(cm["qk"], do_, _TN) + _dot(cm["k_dec"], ds1) for cm, do_, ds1 in zip(cms, dos, ds1s)]
        dqks = [jnp.where(low, _dot(do_, vn, _NT), 0.0) for do_, vn in zip(dos, v_news)]
        dq_decs = [_dot(do_, s0, _NT) for do_, s0 in zip(dos, s0s)]
        dk_decs = [_dot(vn, ds1, _NT) for vn, ds1 in zip(v_news, ds1s)]
        dgls = [jnp.sum(jnp.sum(ds1 * s0, axis=1, keepdims=True), axis=0, keepdims=True) for ds1, s0 in zip(ds1s, s0s)]
        dws = [-_dot(dvn, s0, _NT) for dvn, s0 in zip(dv_news, s0s)]
        for hh in hs:
            dS[hh] = (_dot(cms[hh]["q_dec"], dos[hh], _TN) + cms[hh]["gl"] * ds1s[hh]
                      - _dot(ws[hh], dv_news[hh], _TN))
        dvbs = [_dot(tm, dvn, _TN) for tm, dvn in zip(tms, dv_news)]
        dkbgs = [_dot(tm, dw, _TN) for tm, dw in zip(tms, dws)]
        dAs = [-jnp.where(strict, _dot(dvb, u, _NT) + _dot(dkbg, w, _NT), 0.0)
               for dvb, u, dkbg, w in zip(dvbs, us, dkbgs, ws)]
        dkks = [dA * cm["decay"] for dA, cm in zip(dAs, cms)]
        dqkraws = [dqk * cm["decay"] for dqk, cm in zip(dqks, cms)]
        Es = [(dA * cm["kk"] + dqk * cm["qkraw"]) * cm["decay"] for dA, dqk, cm in zip(dAs, dqks, cms)]
        dkbs = [_dot(dkk, k) + dkbg * cm["eg"] for dkk, k, dkbg, cm in zip(dkks, ks, dkbgs, cms)]
        dqss = [_dot(dqr, k) + dqd * cm["eg"] for dqr, k, dqd, cm in zip(dqkraws, ks, dq_decs, cms)]
        for hh in hs:
            cm = cms[hh]
            dk_ref[:, sls[hh]] = (_dot(dqkraws[hh], cm["qs"], _TN) + _dot(dkks[hh], cm["kb"], _TN)
                                  + dk_decs[hh] * cm["ekd"] + dkbs[hh] * cm["b_col"])
            dv_ref[:, sls[hh]] = dvbs[hh] * cm["b_col"]
            dq_ref[:, sls[hh]] = dqss[hh] * (_GD ** -0.5)
        for hh in hs:
            cm, k, E = cms[hh], ks[hh], Es[hh]
            eg, ekd = cm["eg"], cm["ekd"]
            dbeta_col = jnp.sum(dvbs[hh] * vs_[hh] + dkbs[hh] * k, axis=1, keepdims=True)
            t_kd = jnp.sum(dk_decs[hh] * k, axis=1, keepdims=True) * ekd
            c1 = (jnp.sum(E, axis=1, keepdims=True) + jnp.sum(dkbgs[hh] * cm["kb"], axis=1, keepdims=True) * eg
                  + jnp.sum(dq_decs[hh] * cm["qs"], axis=1, keepdims=True) * eg - t_kd)
            r1 = jnp.sum(E, axis=0, keepdims=True)
            dgam_last = jnp.sum(t_kd, axis=0, keepdims=True) + dgls[hh] * cm["gl"]
            dgam_col = c1 - jnp.sum(jnp.where(eye, r1, 0.0), axis=1, keepdims=True)
            dg_ref[hh] = jnp.sum(jnp.where(ii >= jj, dgam_col, 0.0), axis=0, keepdims=True) + dgam_last
            db_ref[hh] = jnp.sum(jnp.where(eye, dbeta_col, 0.0), axis=0, keepdims=True)

    qspec = pl.BlockSpec((C, hb * _GD), lambda h, n: (N - 1 - n, h))
    gspec = pl.BlockSpec((hb, None, 1, C), lambda h, n: (h, N - 1 - n, 0, 0))
    return pl.pallas_call(
        body, name=name, grid=(H // hb, N),
        in_specs=[qspec, qspec, qspec, gspec, gspec,
                  pl.BlockSpec((hb, None, _GD, _GD), lambda h, n: (h, N - 1 - n, 0, 0)),
                  pl.BlockSpec((hb, None, C, C), lambda h, n: (h, N - 1 - n, 0, 0)), qspec],
        out_specs=[qspec, qspec, qspec, gspec, gspec],
        out_shape=[jax.ShapeDtypeStruct((T, D), _F32)] * 3 + [jax.ShapeDtypeStruct((H, N, 1, C), _F32)] * 2,
        scratch_shapes=[pltpu.VMEM((hb, _GD, _GD), _F32)],
        compiler_params=_cp("arbitrary", "arbitrary"),
    )(qn, kn, vs, g_r, b_r, s_all, t_all, do)


def _onorm_fwd(o, proj, go, li, D, name, tb=512):
    T = o.shape[0]
    H = D // _GD
    tb = _pick(T, tb, 8)

    def body(o_ref, z_ref, go_ref, y_ref):
        ov, zv, g = o_ref[...], z_ref[...], go_ref[...]
        for h in range(H):
            sl = slice(h * _GD, (h + 1) * _GD)
            oh = ov[:, sl]
            r = lax.rsqrt(jnp.mean(oh * oh, axis=1, keepdims=True) + _EPS)
            y_ref[:, sl] = (oh * r * g * _silu(zv[:, sl])).astype(y_ref.dtype)

    return pl.pallas_call(
        body, name=name, grid=(T // tb,), in_specs=[_rspec(tb, D), _rspec(tb, D, 5), _lspec((1, _GD), li)],
        out_specs=_rspec(tb, D), out_shape=jax.ShapeDtypeStruct((T, D), _MMT), compiler_params=_cp("parallel"),
    )(o, proj, go)


def _onorm_bwd(dy, o, proj, go, li, D, into, name, tb=256):
    T = o.shape[0]
    H = D // _GD
    tb = _pick(T, tb, 8)

    def body(dy_ref, o_ref, z_ref, go_ref, _, do_ref, dz_ref, dgo_ref):
        @pl.when(pl.program_id(0) == 0)
        def _():
            dgo_ref[...] = jnp.zeros_like(dgo_ref)

        dyv, ov, zv, g = dy_ref[...].astype(_F32), o_ref[...], z_ref[...], go_ref[...]
        dgo = jnp.zeros((1, _GD), _F32)
        for h in range(H):
            sl = slice(h * _GD, (h + 1) * _GD)
            oh, zh, dyh = ov[:, sl], zv[:, sl], dyv[:, sl]
            r = lax.rsqrt(jnp.mean(oh * oh, axis=1, keepdims=True) + _EPS)
            on = oh * r
            sz = _silu(zh)
            dgo = dgo + jnp.sum(dyh * sz * on, axis=0, keepdims=True)
            don = dyh * sz * g
            do_ref[:, sl] = r * (don - on * jnp.mean(don * on, axis=1, keepdims=True))
            dz_ref[:, sl] = (dyh * on * g * _dsilu(zh)).astype(dz_ref.dtype)
        dgo_ref[...] += dgo

    return pl.pallas_call(
        body, name=name, grid=(T // tb,),
        in_specs=[_rspec(tb, D), _rspec(tb, D), _rspec(tb, D, 5), _lspec((1, _GD), li), _HBM],
        out_specs=[_rspec(tb, D), _rspec(tb, D, 5), _fspec((1, _GD))],
        out_shape=[jax.ShapeDtypeStruct((T, D), _F32), jax.ShapeDtypeStruct(into.shape, into.dtype),
                   jax.ShapeDtypeStruct((1, _GD), _F32)],
        input_output_aliases={4: 1}, compiler_params=_cp("arbitrary"),
    )(dy, o, proj, go, into)


def _merge_bwd(dm, pa, pb, proj, D, name, tb=256):
    T, PW = proj.shape
    tb = _pick(T, tb, 8)

    def body(dm_ref, pa_ref, pb_ref, ga_ref, gb_ref, dpa_ref, dpb_ref, dg_ref):
        d = dm_ref[...].astype(_F32)
        sa, sb = _sigmoid(ga_ref[...]), _sigmoid(gb_ref[...])
        dpa_ref[...] = (d * sa).astype(dpa_ref.dtype)
        dpb_ref[...] = (d * sb).astype(dpb_ref.dtype)
        dg_ref[:, :D] = (d * pa_ref[...].astype(_F32) * sa * (1.0 - sa)).astype(dg_ref.dtype)
        dg_ref[:, D:] = (d * pb_ref[...].astype(_F32) * sb * (1.0 - sb)).astype(dg_ref.dtype)

    return pl.pallas_call(
        body, name=name, grid=(T // tb,),
        in_specs=[_rspec(tb, D), _rspec(tb, D), _rspec(tb, D), _rspec(tb, D, 6), _rspec(tb, D, 7)],
        out_specs=[_rspec(tb, D), _rspec(tb, D), _rspec(tb, 2 * D, 3)],
        out_shape=[jax.ShapeDtypeStruct((T, D), _MMT)] * 2 + [jax.ShapeDtypeStruct((T, PW), _MMT)],
        compiler_params=_cp("parallel"),
    )(dm, pa, pb, proj, proj)


def _swiglu_bwd(da, gate, up, name, tb=256):
    T, F = gate.shape
    F2 = 2 * F
    tb = _pick(T, tb, 8)

    def body(da_ref, g_ref, u_ref, o_ref):
        d, g = da_ref[...].astype(_F32), g_ref[...].astype(_F32)
        o_ref[:, :F] = (d * u_ref[...].astype(_F32) * _dsilu(g)).astype(o_ref.dtype)
        o_ref[:, F:] = (d * _silu(g)).astype(o_ref.dtype)

    return pl.pallas_call(
        body, name=name, grid=(T // tb,), in_specs=[_rspec(tb, F), _rspec(tb, F), _rspec(tb, F)],
        out_specs=_rspec(tb, F2), out_shape=jax.ShapeDtypeStruct((T, F2), _MMT), compiler_params=_cp("parallel"),
    )(da, gate, up)


def _loss_head(x, tgt, fg, name, tb=256):
    T, D = x.shape
    tb = _pick(T, tb, 8)

    def body(x_ref, t_ref, fg_ref, loss_ref, dx_ref, dfg_ref):
        @pl.when(pl.program_id(0) == 0)
        def _():
            loss_ref[...] = jnp.zeros_like(loss_ref)
            dfg_ref[...] = jnp.zeros_like(dfg_ref)

        xv, fg_ = x_ref[...], fg_ref[...]
        r = lax.rsqrt(jnp.mean(xv * xv, axis=1, keepdims=True) + _EPS)
        xn = xv * r
        e = xn * fg_ - t_ref[...]
        loss_ref[...] += (0.5 / D) * jnp.sum(jnp.sum(e * e, axis=1, keepdims=True), axis=0, keepdims=True)
        dy = e * (1.0 / D)
        dfg_ref[...] += jnp.sum(dy * xn, axis=0, keepdims=True)
        dxn = dy * fg_
        dx_ref[...] = r * (dxn - xn * jnp.mean(dxn * xn, axis=1, keepdims=True))

    return pl.pallas_call(
        body, name=name, grid=(T // tb,), in_specs=[_rspec(tb, D), _rspec(tb, D), _fspec((1, D))],
        out_specs=[_fspec((1, 1)), _rspec(tb, D), _fspec((1, D))],
        out_shape=[jax.ShapeDtypeStruct((1, 1), _F32), jax.ShapeDtypeStruct((T, D), _F32),
                   jax.ShapeDtypeStruct((1, D), _F32)],
        compiler_params=_cp("arbitrary"),
    )(x, tgt, fg)


def _row_tile(R, W, budget=1 << 20, unit=8):
    if R * W * 4 <= budget or R % unit:
        return R
    best = unit
    for t in range(unit, R + 1, unit):
        if R % t == 0 and t * W * 4 <= budget:
            best = t
    return best


def _add_own_half(send, got, half, name):
    P, Rp, W = send.shape
    Rh = Rp // 2
    tb = _row_tile(Rh, W, 1 << 21, 16)

    def body(h_ref, a_ref, b_ref, o_ref):
        o_ref[...] = (a_ref[...].astype(_F32) + b_ref[...].astype(_F32)).astype(o_ref.dtype)

    return pl.pallas_call(
        body, name=name,
        grid_spec=pltpu.PrefetchScalarGridSpec(
            num_scalar_prefetch=1, grid=(P, Rh // tb),
            in_specs=[pl.BlockSpec((None, None, tb, W), lambda k, i, h: (k, h[0], i, 0)),
                      pl.BlockSpec((None, tb, W), lambda k, i, h: (k, i, 0))],
            out_specs=pl.BlockSpec((None, tb, W), lambda k, i, h: (k, i, 0))),
        out_shape=jax.ShapeDtypeStruct((P, Rh, W), send.dtype), compiler_params=_cp("parallel", "parallel"),
    )(half, send.reshape(P, 2, Rh, W), got)


def _sum_slots(st, name):
    P, R, W = st.shape
    tb = _row_tile(R, W, 1 << 20, 16)

    def body(s_ref, o_ref):
        acc = s_ref[0].astype(_F32)
        for p in range(1, P):
            acc = acc + s_ref[p].astype(_F32)
        o_ref[...] = acc

    return pl.pallas_call(
        body, name=name, grid=(R // tb,), in_specs=[pl.BlockSpec((P, tb, W), lambda i: (0, i, 0))],
        out_specs=_rspec(tb, W), out_shape=jax.ShapeDtypeStruct((R, W), _F32), compiler_params=_cp("parallel"),
    )(st)


def _adamw(w, gst, m, v, name):
    R, W = w.shape
    P = gst.shape[0]
    tb = _row_tile(R, W, 1 << 20)
    c1, c2 = 1.0 - _B1 ** _STEP, 1.0 - _B2 ** _STEP

    def body(w_ref, g_ref, m_ref, v_ref, go_ref, d_ref, mo_ref, vo_ref):
        g = g_ref[0]
        for p in range(1, P):
            g = g + g_ref[p]
        mn = _B1 * m_ref[...] + (1.0 - _B1) * g
        vn = _B2 * v_ref[...] + (1.0 - _B2) * (g * g)
        go_ref[...] = g
        mo_ref[...] = mn
        vo_ref[...] = vn
        d_ref[...] = -_LR * ((mn / c1) / (jnp.sqrt(vn / c2) + _AEPS) + _WD * w_ref[...])

    spec = _rspec(tb, W)
    return pl.pallas_call(
        body, name=name, grid=(R // tb,),
        in_specs=[spec, pl.BlockSpec((P, tb, W), lambda i: (0, i, 0)), spec, spec],
        out_specs=[spec] * 4, out_shape=[jax.ShapeDtypeStruct((R, W), _F32)] * 4, compiler_params=_cp("parallel"),
    )(w, gst, m, v)


def _as2d(a):
    if a.ndim == 1:
        return a.reshape(1, -1)
    return a.reshape(-1, a.shape[-1])


def kernel(x, c, ada_w, ada_b, norm1_g, w_in, conv_w, spatial_w, spatial_b, v_norm_g, a_log, dt_bias, o_norm_g, w_branch_a, w_branch_b, w_out, norm2_g, w_ffn_in, w_ffn_out, final_g, loss_target, m_ada_w, m_ada_b, m_norm1_g, m_w_in, m_conv_w, m_spatial_w, m_spatial_b, m_v_norm_g, m_a_log, m_dt_bias, m_o_norm_g, m_w_branch_a, m_w_branch_b, m_w_out, m_norm2_g, m_w_ffn_in, m_w_ffn_out, m_final_g, v_ada_w, v_ada_b, v_norm1_g, v_w_in, v_conv_w, v_spatial_w, v_spatial_b, v_v_norm_g, v_a_log, v_dt_bias, v_o_norm_g, v_w_branch_a, v_w_branch_b, v_w_out, v_norm2_g, v_w_ffn_in, v_w_ffn_out, v_final_g):
    xb, tgt = x[0], loss_target[0]
    T, D = xb.shape
    L, H, G = ada_w.shape[0], a_log.shape[1], spatial_w.shape[1]
    F = 4 * w_ffn_out.shape[1]
    N = T // _BC
    Ws = ada_w.shape[2]
    Wc = w_in.shape[2]
    PW = 8 * D + _LANE
    ix, iy, ic = lax.axis_index("x"), lax.axis_index("y"), lax.axis_index("c")
    me = 4 * ix + 2 * iy + ic

    c_all = _gather8(c, "gather_c").reshape(8, D)
    modp = _ada_fwd(c_all, ada_w, "ada_fwd")
    n_mod, n_cw = L * 8 * Ws, L * _KC * conv_w.shape[2]
    pad = (-(n_mod + n_cw)) % _LANE
    pay = jnp.concatenate([modp.reshape(-1), conv_w.reshape(-1), jnp.zeros((pad,), _F32)]).reshape(-1, _LANE)
    pay_all = _gather8(pay, "gather_mod").reshape(8, -1)
    mod_full = jnp.concatenate([pay_all[2 * k, :n_mod].reshape(L, 8, Ws) for k in range(4)], axis=-1)
    cw_full = jnp.concatenate([pay_all[2 * k, n_mod:n_mod + n_cw].reshape(L, _KC, -1) for k in range(4)], axis=-1)
    mod = lax.dynamic_index_in_dim(mod_full, me, axis=1, keepdims=False) + ada_b
    mods = [[mod[l, j * D:(j + 1) * D].reshape(1, D) for j in range(6)] for l in range(L)]

    big = [w_in, w_branch_a, w_branch_b, w_out, w_ffn_in, w_ffn_out]
    chip = 2 * ix + iy
    starts = [(k * Wc) // 16 * 16 for k in range(4)]
    Hh = max(-(-((k + 1) * Wc) // 16) * 16 - starts[k] for k in range(4))
    No = max(s + Hh for s in starts)
    my_off = jnp.asarray([k * Wc - starts[k] for k in range(4)], jnp.int32)[chip]
    cuts = sorted(set(starts + [s + Hh for s in starts]))

    pers = [Hh, D // 4, D // 4, D // 4, 2 * F // 4, F // 4]
    roff = [0]
    for p in pers:
        roff.append(roff[-1] + L * p)
    Rp = -(-roff[-1] // (32 * _NCH)) * (32 * _NCH)
    rpad = Rp - roff[-1]

    hull = lax.dynamic_update_slice(jnp.zeros((L, Hh, D), _F32), jnp.transpose(w_in, (0, 2, 1)), (0, my_off, 0))
    shard = jnp.concatenate(
        [hull.reshape(-1, D).astype(_MMT), w_branch_a.reshape(-1, D).astype(_MMT),
         w_branch_b.reshape(-1, D).astype(_MMT), w_out.reshape(-1, D).astype(_MMT),
         jnp.transpose(w_ffn_in, (0, 2, 1)).reshape(-1, D).astype(_MMT), w_ffn_out.reshape(-1, D).astype(_MMT),
         jnp.zeros((rpad, D), _MMT)], axis=0)
    gw = _fill_from_sibling(_gather_chips(shard, "gather_w"), "gather_w_sib")

    def slab(i, l, k):
        a = roff[i] + l * pers[i]
        return gw[k, a:a + pers[i]]

    def joined(i, l):
        return jnp.concatenate([slab(i, l, k) for k in range(4)], axis=0)

    def orig_rows(hulls, a, b):
        edges = sorted(set([a, b] + [c_ for c_ in cuts if a < c_ < b]))
        out = []
        for lo, hi in zip(edges[:-1], edges[1:]):
            cov = [k for k in range(4) if starts[k] <= lo and hi <= starts[k] + Hh]
            piece = hulls[cov[0]][lo - starts[cov[0]]:hi - starts[cov[0]]]
            for k in cov[1:]:
                piece = piece + hulls[k][lo - starts[k]:hi - starts[k]]
            out.append(piece)
        return out

    wt_in_p = []
    for l in range(L):
        hulls = [slab(0, l, k) for k in range(4)]
        wt_in_p.append(jnp.concatenate(
            orig_rows(hulls, 0, 6 * D) + orig_rows(hulls, 6 * D + 2 * H, 8 * D + 2 * H)
            + orig_rows(hulls, 6 * D, 6 * D + 2 * H) + [jnp.zeros((_LANE - 2 * H, D), _MMT)], axis=0))
    w_a, w_b, w_o, wt_fi, w_fo = ([joined(i, l) for l in range(L)] for i in range(1, 6))

    sbt = jnp.transpose(spatial_b, (0, 2, 1))
    gv3, go3 = v_norm_g.reshape(L, 1, D), o_norm_g.reshape(L, 1, _GD)
    zpad = jnp.zeros((L, _LANE - 2 * H), _F32)
    alog_row = jnp.concatenate([jnp.zeros((L, H), _F32), a_log, zpad], axis=1).reshape(L, 1, _LANE)
    dtb_row = jnp.concatenate([jnp.zeros((L, H), _F32), dt_bias, zpad], axis=1).reshape(L, 1, _LANE)

    def rows_of(tok):
        return jnp.transpose(tok.reshape(N, _BC, H), (2, 0, 1)).reshape(H, N, 1, _BC)

    def toks_of(rows):
        return jnp.transpose(rows.reshape(H, N, _BC), (1, 2, 0)).reshape(T, H)

    saved = []
    xc = xb
    for l in range(L):
        sh1, sc1, gt1, sh2, sc2, gt2 = mods[l]
        g1, g2 = norm1_g[l].reshape(1, D), norm2_g[l].reshape(1, D)
        h = _norm_mod(xc, g1, sc1, sh1, f"norm1_{l}")
        proj = _mm(h, wt_in_p[l], f"proj_{l}", trans_b=True, tn=1664)
        ya = _gmlp_fwd(proj, spatial_w, sbt, gv3, l, D, f"gmlp_{l}")
        qn, kn, vs, bg = _conv_fwd(proj, cw_full, alog_row, dtb_row, l, D, f"conv_{l}")
        g_r, b_r = rows_of(bg[:, H:2 * H]), rows_of(bg[:, :H])
        o, s_all, t_all = _gdn_fwd(qn, kn, vs, g_r, b_r, f"gdn_{l}")
        yb = _onorm_fwd(o, proj, go3, l, D, f"onorm_{l}")
        pa, pb, mg = _branch_merge(ya, yb, w_a[l], w_b[l], proj, f"branch_{l}")
        p1, x1 = _mm_res(mg, w_o[l], xc, gt1, f"wout_{l}")
        h2 = _norm_mod(x1, g2, sc2, sh2, f"norm2_{l}")
        gate, up, act = _ffin_swiglu(h2, wt_fi[l], f"ffin_{l}")
        p2, x2 = _mm_res(act, w_fo[l], x1, gt2, f"ffout_{l}")
        saved.append(dict(x=xc, h=h, proj=proj, ya=ya, yb=yb, qn=qn, kn=kn, vs=vs, g_r=g_r, b_r=b_r, o=o,
                          s_all=s_all, t_all=t_all, pa=pa, pb=pb, mg=mg, p1=p1, x1=x1, h2=h2, gate=gate, up=up,
                          act=act, p2=p2))
        xc = x2

    loss11, dx, dfg = _loss_head(xc, tgt, final_g.reshape(1, D), "loss_head")
    loss = lax.psum(loss11[0, 0], ("x", "y", "c"))

    gbig = {k: [None] * L for k in ("w_in", "w_a", "w_b", "w_o", "w_fi", "w_fo")}
    small = {k: [None] * L for k in ("dmod", "n1", "n2", "sw", "sb", "gv", "cw", "al", "dt", "go")}
    for l in reversed(range(L)):
        sv = saved[l]
        sh1, sc1, gt1, sh2, sc2, gt2 = mods[l]
        g1, g2 = norm1_g[l].reshape(1, D), norm2_g[l].reshape(1, D)
        proj = sv["proj"]
        dp2, dgt2 = _resid_bwd(dx, sv["p2"], gt2, f"res2b_{l}")
        da = _mm(dp2, w_fo[l], f"ffoutb_{l}", trans_b=True, out_dtype=_MMT)
        gbig["w_fo"][l] = _mm_tn(sv["act"], dp2, f"ffoutw_{l}")
        dgu = _swiglu_bwd(da, sv["gate"], sv["up"], f"swiglub_{l}")
        dh2 = _mm(dgu, wt_fi[l], f"ffinb_{l}")
        gbig["w_fi"][l] = _mm_tn(dgu, sv["h2"], f"ffinw_{l}")
        dx1, dgm2, dsh2 = _norm_mod_bwd(sv["x1"], dh2, dx, g2, sc2, f"norm2b_{l}")
        dp1, dgt1 = _resid_bwd(dx1, sv["p1"], gt1, f"res1b_{l}")
        dmg = _mm(dp1, w_o[l], f"woutb_{l}", trans_b=True, out_dtype=_MMT)
        gbig["w_o"][l] = _mm_tn(sv["mg"], dp1, f"woutw_{l}")
        dpa, dpb, dproj = _merge_bwd(dmg, sv["pa"], sv["pb"], proj, D, f"mergeb_{l}")
        dya = _mm(dpa, w_a[l], f"brab_{l}", trans_b=True, out_dtype=_MMT)
        gbig["w_a"][l] = _mm_tn(sv["ya"], dpa, f"braw_{l}")
        dyb = _mm(dpb, w_b[l], f"brbb_{l}", trans_b=True, out_dtype=_MMT)
        gbig["w_b"][l] = _mm_tn(sv["yb"], dpb, f"brbw_{l}")
        dproj, dsw, dsa, dgv = _gmlp_bwd(proj, dya, spatial_w, sbt, gv3, l, D, dproj, f"gmlpb_{l}")
        do, dproj, dgo = _onorm_bwd(dyb, sv["o"], proj, go3, l, D, dproj, f"onormb_{l}")
        dqn, dkn, dvs, dg_r, db_r = _gdn_bwd(sv["qn"], sv["kn"], sv["vs"], sv["g_r"], sv["b_r"], sv["s_all"],
                                             sv["t_all"], do, f"gdnb_{l}")
        dbg = jnp.concatenate([toks_of(db_r), toks_of(dg_r), jnp.zeros((T, _LANE - 2 * H), _F32)], axis=1)
        dc, dproj, dcw, dal, ddt = _conv_bwd1(proj, dqn, dkn, dvs, dbg, cw_full, alog_row, dtb_row, l, D, dproj,
                                              f"convb_{l}")
        dproj = _conv_bwd2(dc, cw_full, l, dproj, f"convx_{l}")
        dh = _mm(dproj, wt_in_p[l], f"projb_{l}", tk=1664)
        gbig["w_in"][l] = _mm_tn(dproj, sv["h"], f"projw_{l}", tm=640)
        dx, dgm1, dsh1 = _norm_mod_bwd(sv["x"], dh, dx1, g1, sc1, f"norm1b_{l}")
        small["dmod"][l] = jnp.concatenate([dsh1, dgm1 * g1, dgt1, dsh2, dgm2 * g2, dgt2], axis=1)
        small["n1"][l], small["n2"][l] = dgm1 * (1.0 + sc1), dgm2 * (1.0 + sc2)
        small["sw"][l], small["gv"][l], small["cw"][l], small["go"][l] = dsw, dgv, dcw, dgo
        small["sb"][l] = jnp.transpose(dsa.reshape(_AC, G, _GD).sum(axis=-1))
        small["al"][l], small["dt"][l] = dal[:, H:2 * H], ddt[:, H:2 * H]
    grad_x = dx.reshape(1, T, D)

    names_small = ["dmod", "n1", "n2", "sw", "sb", "gv", "cw", "al", "dt", "go"]
    flat = [jnp.stack(small[k]).reshape(-1) for k in names_small] + [dfg.reshape(-1)]
    sizes = [f.shape[0] for f in flat]
    tot = sum(sizes)
    pad = (-tot) % 1024
    pay = jnp.concatenate(flat + [jnp.zeros((pad,), _F32)]).reshape(-1, 1024)
    sm_all = _gather8(pay, "gather_small").reshape(8, -1)
    offs = [0]
    for s in sizes:
        offs.append(offs[-1] + s)
    part = {k: sm_all[:, offs[i]:offs[i + 1]] for i, k in enumerate(names_small + ["fg"])}
    dmod_all = part["dmod"].reshape(8, L, 6 * D)

    outs = {}

    def update(nm, w, gst, m, v):
        shp = w.shape
        w2 = _as2d(w)
        g, d, mn, vn = _adamw(w2, gst.reshape((gst.shape[0],) + w2.shape), _as2d(m), _as2d(v), f"adamw_{nm}")
        outs[nm] = (g.reshape(shp), d.reshape(shp), mn.reshape(shp), vn.reshape(shp))

    chip = 2 * ix + iy
    dmod_t = jnp.transpose(dmod_all, (1, 0, 2))
    dmod_mine = lax.dynamic_slice_in_dim(dmod_t, chip * Ws, Ws, axis=2)
    g_ada_w = _ada_bwd(jnp.transpose(c_all), dmod_mine, "ada_bwd")
    update("ada_w", ada_w, g_ada_w[None], m_ada_w, v_ada_w)
    update("ada_b", ada_b, dmod_all, m_ada_b, v_ada_b)
    update("norm1_g", norm1_g, part["n1"], m_norm1_g, v_norm1_g)
    update("norm2_g", norm2_g, part["n2"], m_norm2_g, v_norm2_g)
    update("spatial_w", spatial_w, part["sw"], m_spatial_w, v_spatial_w)
    update("spatial_b", spatial_b, part["sb"], m_spatial_b, v_spatial_b)
    update("v_norm_g", v_norm_g, part["gv"], m_v_norm_g, v_v_norm_g)
    update("a_log", a_log, part["al"], m_a_log, v_a_log)
    update("dt_bias", dt_bias, part["dt"], m_dt_bias, v_dt_bias)
    update("o_norm_g", o_norm_g, part["go"], m_o_norm_g, v_o_norm_g)
    update("final_g", final_g, part["fg"], m_final_g, v_final_g)
    cw_cols = conv_w.shape[2]
    dcw_all = part["cw"].reshape(8, L, _KC, 4 * cw_cols)
    update("conv_w", conv_w, lax.dynamic_slice_in_dim(dcw_all, chip * cw_cols, cw_cols, axis=3), m_conv_w, v_conv_w)

    def hull_of(p, k):
        a, b = starts[k], starts[k] + Hh
        out = []
        for lo, hi, plo in ((0, 6 * D, 0), (6 * D, 6 * D + 2 * H, 8 * D), (6 * D + 2 * H, 8 * D + 2 * H, 6 * D),
                            (8 * D + 2 * H, No, None)):
            s, e = max(a, lo), min(b, hi)
            if s < e:
                out.append(jnp.zeros((e - s, D), _MMT) if plo is None else p[plo + s - lo:plo + e - lo].astype(_MMT))
        return out

    pieces = []
    for k in range(4):
        for l in range(L):
            pieces += hull_of(gbig["w_in"][l], k)
        for i, nm in enumerate(("w_a", "w_b", "w_o", "w_fi", "w_fo")):
            per = pers[i + 1]
            pieces += [gbig[nm][l][k * per:(k + 1) * per].astype(_MMT) for l in range(L)]
        pieces.append(jnp.zeros((rpad, D), _MMT))
    send = jnp.concatenate(pieces, axis=0).reshape(4, Rp, D)
    got = _send_half_to_sibling(send, "reduce_cores")
    chipsum = _add_own_half(send, got, ic.astype(jnp.int32).reshape(1), "add_cores")
    parts = _scatter_to_chips(chipsum, "reduce_chips")
    mine = _sum_slots(parts, "add_chips")
    other = _swap_with_sibling(mine, "swap_cores")
    first = ic == 0
    gsum = jnp.concatenate([jnp.where(first, mine, other), jnp.where(first, other, mine)], axis=0)
    big_names = ["w_in", "w_branch_a", "w_branch_b", "w_out", "w_ffn_in", "w_ffn_out"]
    big_m = [m_w_in, m_w_branch_a, m_w_branch_b, m_w_out, m_w_ffn_in, m_w_ffn_out]
    big_v = [v_w_in, v_w_branch_a, v_w_branch_b, v_w_out, v_w_ffn_in, v_w_ffn_out]
    for i, (nm, w, m, v) in enumerate(zip(big_names, big, big_m, big_v)):
        g = gsum[roff[i]:roff[i + 1]].reshape(L, pers[i], D)
        if i == 0:
            g = jnp.transpose(lax.dynamic_slice_in_dim(g, my_off, Wc, axis=1), (0, 2, 1))
        elif i == 4:
            g = jnp.transpose(g, (0, 2, 1))
        update(nm, w, g[None], m, v)

    order = ["ada_w", "ada_b", "norm1_g", "w_in", "conv_w", "spatial_w", "spatial_b", "v_norm_g", "a_log", "dt_bias",
             "o_norm_g", "w_branch_a", "w_branch_b", "w_out", "norm2_g", "w_ffn_in", "w_ffn_out", "final_g"]
    return (loss, grad_x, *[outs[n][0] for n in order], *[outs[n][1] for n in order],
            *[outs[n][2] for n in order], *[outs[n][3] for n in order])
```

```python
import functools
import math

import jax
import jax.numpy as jnp
from jax import lax
from jax.experimental import pallas as pl
from jax.experimental.pallas import tpu as pltpu

_F32 = jnp.float32
_BF = jnp.bfloat16
_MMT = jnp.bfloat16
_EPS = 1e-6
_GD = 128
_AC = 128
_BC = 64
_KC = 4
_HB = 8
_NCH = 8
_LANE = 128
_VMEM_LIMIT = 56 * 1024 * 1024

_LR, _B1, _B2, _AEPS, _WD, _STEP = 0.001, 0.9, 0.999, 1e-08, 0.01, 10

_NN = (((1,), (0,)), ((), ()))
_NT = (((1,), (1,)), ((), ()))
_TN = (((0,), (0,)), ((), ()))

_MESH = pl.DeviceIdType.MESH


def _cp(*sem):
    return pltpu.CompilerParams(dimension_semantics=tuple(sem), vmem_limit_bytes=_VMEM_LIMIT)


def _dot(a, b, dn=_NN):
    return lax.dot_general(a.astype(_MMT), b.astype(_MMT), dn, preferred_element_type=_F32)


def _pick(n, target, unit=_LANE):
    if n <= target:
        return n
    best = None
    for t in range(unit, target + 1, unit):
        if n % t == 0:
            best = t
    assert best is not None, (n, target)
    return best


def _sigmoid(x):
    return 0.5 * jnp.tanh(0.5 * x) + 0.5


def _silu(x):
    return x * _sigmoid(x)


def _dsilu(x):
    s = _sigmoid(x)
    return s * (1.0 + x * (1.0 - s))


_GK = math.sqrt(2.0 / math.pi)


def _gelu(x):
    return 0.5 * x * (1.0 + jnp.tanh(_GK * (x + 0.044715 * x * x * x)))


def _dgelu(x):
    t = jnp.tanh(_GK * (x + 0.044715 * x * x * x))
    return 0.5 * (1.0 + t) + 0.5 * x * (1.0 - t * t) * _GK * (1.0 + 3.0 * 0.044715 * x * x)


def _softplus(x):
    return jnp.maximum(x, 0.0) + jnp.log(1.0 + jnp.exp(-jnp.abs(x)))


def _rspec(tb, w, cb=0):
    return pl.BlockSpec((tb, w), lambda i: (i, cb))


def _fspec(shape):
    nd = len(shape)
    return pl.BlockSpec(tuple(shape), lambda i: (0,) * nd)


def _lspec(tail, li):
    nd = len(tail)
    return pl.BlockSpec((None,) + tuple(tail), lambda i: (li,) + (0,) * nd)


def _slot_all8(x, y, c):
    return 4 * x + 2 * y + c


def _gather8(v, name):
    R, W = v.shape

    def body(v_ref, o_ref, ssem, rsem, lsem):
        x, y, c = lax.axis_index("x"), lax.axis_index("y"), lax.axis_index("c")
        sib = (x, y, 1 - c)
        chips = _other_chips(x, y)

        def slot(px, py, pc):
            return o_ref.at[_slot_all8(px, py, pc)]

        own = pltpu.make_async_copy(v_ref, slot(x, y, c), lsem)
        own.start()
        started = [_rcopy(v_ref, slot(x, y, c), ssem.at[0], rsem.at[0], sib)]
        started += [_rcopy(v_ref, slot(x, y, c), ssem.at[1 + j], rsem.at[1 + j], (px, py, c))
                    for j, (px, py) in enumerate(chips)]
        for cp in started:
            cp.start()
        for j, (px, py) in enumerate(chips):
            blk = slot(px, py, c)
            _rcopy(blk, blk, ssem.at[1 + j], rsem.at[1 + j], (px, py, c)).wait_recv()
            fw = _rcopy(blk, blk, ssem.at[4 + j], rsem.at[4 + j], sib)
            fw.start()
            started.append(fw)
        blk = slot(x, y, 1 - c)
        _rcopy(blk, blk, ssem.at[0], rsem.at[0], sib).wait_recv()
        for j, (px, py) in enumerate(chips):
            blk = slot(px, py, 1 - c)
            _rcopy(blk, blk, ssem.at[4 + j], rsem.at[4 + j], sib).wait_recv()
        for cp in started:
            cp.wait_send()
        own.wait()

    return pl.pallas_call(
        body, name=name, out_shape=jax.ShapeDtypeStruct((8, R, W), v.dtype), in_specs=[_HBM], out_specs=_HBM,
        scratch_shapes=[pltpu.SemaphoreType.DMA((7,)), pltpu.SemaphoreType.DMA((7,)), pltpu.SemaphoreType.DMA],
    )(v)


def _rcopy(src, dst, ssem, rsem, dev):
    return pltpu.make_async_remote_copy(src_ref=src, dst_ref=dst, send_sem=ssem, recv_sem=rsem,
                                        device_id=dev, device_id_type=_MESH)


def _other_chips(x, y):
    return [(1 - x, y), (x, 1 - y), (1 - x, 1 - y)]


_HBM = pl.BlockSpec(memory_space=pl.ANY)


def _gather_chips(shard, name):
    Rp, W = shard.shape
    Rh = Rp // 2
    rc = Rh // _NCH
    hq = _NCH // 2

    def body(s_ref, o_ref, ssem, rsem, lsem):
        x, y, c = lax.axis_index("x"), lax.axis_index("y"), lax.axis_index("c")
        chip = 2 * x + y
        xn, yn, dg = _other_chips(x, y)
        cx, cy, cd = 2 * xn[0] + xn[1], 2 * yn[0] + yn[1], 2 * dg[0] + dg[1]

        def rows(q):
            return pl.ds(c * Rh + q * rc, rc)

        locs = []
        for q in range(_NCH):
            lc = pltpu.make_async_copy(s_ref.at[rows(q)], o_ref.at[chip, rows(q)], lsem.at[q])
            lc.start()
            locs.append(lc)
        started = []
        for q in range(_NCH):
            for j, nb in ((0, xn), (1, yn)):
                cp = _rcopy(s_ref.at[rows(q)], o_ref.at[chip, rows(q)], ssem.at[j * _NCH + q], rsem.at[j * _NCH + q],
                            (nb[0], nb[1], c))
                cp.start()
                started.append(cp)
        for q in range(_NCH):
            bx = o_ref.at[cx, rows(q)]
            _rcopy(bx, bx, ssem.at[q], rsem.at[q], (xn[0], xn[1], c)).wait_recv()
            if q >= hq:
                rl = _rcopy(bx, bx, ssem.at[2 * _NCH + q], rsem.at[2 * _NCH + q], (yn[0], yn[1], c))
                rl.start()
                started.append(rl)
            by = o_ref.at[cy, rows(q)]
            _rcopy(by, by, ssem.at[_NCH + q], rsem.at[_NCH + q], (yn[0], yn[1], c)).wait_recv()
            if q < hq:
                rl = _rcopy(by, by, ssem.at[2 * _NCH + q], rsem.at[2 * _NCH + q], (xn[0], xn[1], c))
                rl.start()
                started.append(rl)
        for q in range(_NCH):
            bd = o_ref.at[cd, rows(q)]
            _rcopy(bd, bd, ssem.at[2 * _NCH + q], rsem.at[2 * _NCH + q], (dg[0], dg[1], c)).wait_recv()
        for cp in started:
            cp.wait_send()
        for lc in locs:
            lc.wait()

    return pl.pallas_call(
        body, name=name, out_shape=jax.ShapeDtypeStruct((4, Rp, W), shard.dtype), in_specs=[_HBM], out_specs=_HBM,
        scratch_shapes=[pltpu.SemaphoreType.DMA((3 * _NCH,))] * 2 + [pltpu.SemaphoreType.DMA((_NCH,))],
    )(shard)


def _fill_from_sibling(buf, name):
    P, Rp, W = buf.shape
    Rh = Rp // 2
    rc = Rh // _NCH

    def body(s_ref, o_ref, ssem, rsem):
        x, y, c = lax.axis_index("x"), lax.axis_index("y"), lax.axis_index("c")
        cps = []
        for k in range(P):
            for q in range(_NCH):
                r = pl.ds(c * Rh + q * rc, rc)
                cp = _rcopy(s_ref.at[k, r], o_ref.at[k, r], ssem.at[k * _NCH + q], rsem.at[k * _NCH + q],
                            (x, y, 1 - c))
                cp.start()
                cps.append(cp)
        for k in range(P):
            for q in range(_NCH):
                blk = o_ref.at[k, pl.ds((1 - c) * Rh + q * rc, rc)]
                _rcopy(blk, blk, ssem.at[k * _NCH + q], rsem.at[k * _NCH + q], (x, y, 1 - c)).wait_recv()
        for cp in cps:
            cp.wait_send()

    return pl.pallas_call(
        body, name=name, out_shape=jax.ShapeDtypeStruct(buf.shape, buf.dtype), in_specs=[_HBM], out_specs=_HBM,
        scratch_shapes=[pltpu.SemaphoreType.DMA((P * _NCH,))] * 2, input_output_aliases={0: 0},
    )(buf)


def _send_half_to_sibling(send, name):
    P, Rp, W = send.shape
    Rh = Rp // 2
    rc = Rh // _NCH

    def body(s_ref, o_ref, ssem, rsem):
        x, y, c = lax.axis_index("x"), lax.axis_index("y"), lax.axis_index("c")
        cps = []
        for k in range(P):
            for q in range(_NCH):
                cp = _rcopy(s_ref.at[k, pl.ds((1 - c) * Rh + q * rc, rc)], o_ref.at[k, pl.ds(q * rc, rc)],
                            ssem.at[k * _NCH + q], rsem.at[k * _NCH + q], (x, y, 1 - c))
                cp.start()
                cps.append(cp)
        for cp in cps:
            cp.wait()

    return pl.pallas_call(
        body, name=name, out_shape=jax.ShapeDtypeStruct((P, Rh, W), send.dtype), in_specs=[_HBM], out_specs=_HBM,
        scratch_shapes=[pltpu.SemaphoreType.DMA((P * _NCH,))] * 2,
    )(send)


def _scatter_to_chips(cs, name):
    P, Rh, W = cs.shape
    rc = Rh // _NCH

    def body(s_ref, o_ref, ssem, rsem, lsem):
        x, y, c = lax.axis_index("x"), lax.axis_index("y"), lax.axis_index("c")
        chip = 2 * x + y
        peers = _other_chips(x, y)
        locs = []
        for q in range(_NCH):
            r = pl.ds(q * rc, rc)
            lc = pltpu.make_async_copy(s_ref.at[chip, r], o_ref.at[chip, r], lsem.at[q])
            lc.start()
            locs.append(lc)
        cps = []
        for j, (px, py) in enumerate(peers):
            for q in range(_NCH):
                r = pl.ds(q * rc, rc)
                cp = _rcopy(s_ref.at[2 * px + py, r], o_ref.at[chip, r], ssem.at[j * _NCH + q], rsem.at[j * _NCH + q],
                            (px, py, c))
                cp.start()
                cps.append(cp)
        for j, (px, py) in enumerate(peers):
            for q in range(_NCH):
                blk = o_ref.at[2 * px + py, pl.ds(q * rc, rc)]
                _rcopy(blk, blk, ssem.at[j * _NCH + q], rsem.at[j * _NCH + q], (px, py, c)).wait_recv()
        for cp in cps:
            cp.wait_send()
        for lc in locs:
            lc.wait()

    return pl.pallas_call(
        body, name=name, out_shape=jax.ShapeDtypeStruct((P, Rh, W), cs.dtype), in_specs=[_HBM], out_specs=_HBM,
        scratch_shapes=[pltpu.SemaphoreType.DMA((3 * _NCH,))] * 2 + [pltpu.SemaphoreType.DMA((_NCH,))],
    )(cs)


def _swap_with_sibling(v, name):
    R, W = v.shape
    rc = R // _NCH

    def body(s_ref, o_ref, ssem, rsem):
        x, y, c = lax.axis_index("x"), lax.axis_index("y"), lax.axis_index("c")
        cps = []
        for q in range(_NCH):
            r = pl.ds(q * rc, rc)
            cp = _rcopy(s_ref.at[r], o_ref.at[r], ssem.at[q], rsem.at[q], (x, y, 1 - c))
            cp.start()
            cps.append(cp)
        for cp in cps:
            cp.wait()

    return pl.pallas_call(
        body, name=name, out_shape=jax.ShapeDtypeStruct((R, W), v.dtype), in_specs=[_HBM], out_specs=_HBM,
        scratch_shapes=[pltpu.SemaphoreType.DMA((_NCH,))] * 2,
    )(v)


def _mm(a, b, name, li=None, trans_b=False, out_dtype=_F32, tm=1024, tn=1024, tk=2048):
    M, K = a.shape
    bs = b.shape[-2:]
    N = bs[0] if trans_b else bs[1]
    tm, tn, tk = _pick(M, tm, 8), _pick(N, tn), _pick(K, tk)
    nk = K // tk
    lead = () if li is None else (None,)

    def bmap(i, j, k):
        idx = (j, k) if trans_b else (k, j)
        return idx if li is None else (li,) + idx

    def body(a_ref, b_ref, o_ref, acc):
        k = pl.program_id(2)
        part = lax.dot_general(a_ref[...], b_ref[...], _NT if trans_b else _NN, preferred_element_type=_F32)
        if nk == 1:
            o_ref[...] = part.astype(o_ref.dtype)
        else:
            @pl.when(k == 0)
            def _():
                acc[...] = part

            @pl.when(k > 0)
            def _():
                acc[...] += part

            @pl.when(k == nk - 1)
            def _():
                o_ref[...] = acc[...].astype(o_ref.dtype)

    return pl.pallas_call(
        body, name=name, grid=(M // tm, N // tn, nk),
        in_specs=[pl.BlockSpec((tm, tk), lambda i, j, k: (i, k)),
                  pl.BlockSpec(lead + ((tn, tk) if trans_b else (tk, tn)), bmap)],
        out_specs=pl.BlockSpec((tm, tn), lambda i, j, k: (i, j)),
        out_shape=jax.ShapeDtypeStruct((M, N), out_dtype),
        scratch_shapes=[pltpu.VMEM((tm, tn) if nk > 1 else (8, _LANE), _F32)],
        compiler_params=_cp("parallel", "parallel", "arbitrary"),
    )(a, b)


def _mm_tn(a, b, name, tm=512, tn=512):
    T, M = a.shape
    N = b.shape[1]
    tm, tn = _pick(M, tm), _pick(N, tn)

    def body(a_ref, b_ref, o_ref):
        o_ref[...] = lax.dot_general(a_ref[...], b_ref[...], _TN, preferred_element_type=_F32).astype(o_ref.dtype)

    return pl.pallas_call(
        body, name=name, grid=(M // tm, N // tn),
        in_specs=[pl.BlockSpec((T, tm), lambda i, j: (0, i)), pl.BlockSpec((T, tn), lambda i, j: (0, j))],
        out_specs=pl.BlockSpec((tm, tn), lambda i, j: (i, j)),
        out_shape=jax.ShapeDtypeStruct((M, N), _MMT),
        compiler_params=_cp("parallel", "parallel"),
    )(a, b)


def _mm_res(a, b, x, gt, name, tm=1024, tn=1024, tk=2048):
    M, K = a.shape
    N = b.shape[1]
    tm, tn, tk = _pick(M, tm, 8), _pick(N, tn), _pick(K, tk)
    nk = K // tk

    def body(a_ref, b_ref, x_ref, gt_ref, p_ref, o_ref):
        k = pl.program_id(2)
        part = lax.dot_general(a_ref[...], b_ref[...], _NN, preferred_element_type=_F32)

        @pl.when(k == 0)
        def _():
            p_ref[...] = part

        @pl.when(k > 0)
        def _():
            p_ref[...] += part

        @pl.when(k == nk - 1)
        def _():
            o_ref[...] = x_ref[...] + gt_ref[...] * p_ref[...]

    tile = pl.BlockSpec((tm, tn), lambda i, j, k: (i, j))
    return pl.pallas_call(
        body, name=name, grid=(M // tm, N // tn, nk),
        in_specs=[pl.BlockSpec((tm, tk), lambda i, j, k: (i, k)), pl.BlockSpec((tk, tn), lambda i, j, k: (k, j)),
                  tile, pl.BlockSpec((1, tn), lambda i, j, k: (0, j))],
        out_specs=[tile, tile], out_shape=[jax.ShapeDtypeStruct((M, N), _F32)] * 2,
        compiler_params=_cp("parallel", "parallel", "arbitrary"),
    )(a, b, x, gt)


def _ffin_swiglu(a, wt, name, tm=1024, tn=1408):
    M, K = a.shape
    F = wt.shape[0] // 2
    tm, tn = _pick(M, tm, 8), _pick(F, tn)
    nj = F // tn

    def body(a_ref, bg_ref, bu_ref, g_ref, u_ref, act_ref):
        av = a_ref[...]
        g = lax.dot_general(av, bg_ref[...], _NT, preferred_element_type=_F32)
        u = lax.dot_general(av, bu_ref[...], _NT, preferred_element_type=_F32)
        g_ref[...] = g.astype(g_ref.dtype)
        u_ref[...] = u.astype(u_ref.dtype)
        act_ref[...] = (_silu(g) * u).astype(act_ref.dtype)

    tile = pl.BlockSpec((tm, tn), lambda i, j: (i, j))
    return pl.pallas_call(
        body, name=name, grid=(M // tm, nj),
        in_specs=[pl.BlockSpec((tm, K), lambda i, j: (i, 0)), pl.BlockSpec((tn, K), lambda i, j: (j, 0)),
                  pl.BlockSpec((tn, K), lambda i, j: (j + nj, 0))],
        out_specs=[tile] * 3, out_shape=[jax.ShapeDtypeStruct((M, F), _MMT)] * 3,
        compiler_params=_cp("parallel", "parallel"),
    )(a, wt, wt)


def _branch_merge(ya, yb, w_a, w_b, proj, name, tm=1024, tn=512):
    M, K = ya.shape
    N = w_a.shape[1]
    tm, tn = _pick(M, tm, 8), _pick(N, tn)
    nj = N // tn

    def body(ya_ref, yb_ref, wa_ref, wb_ref, ga_ref, gb_ref, pa_ref, pb_ref, m_ref):
        pa = lax.dot_general(ya_ref[...], wa_ref[...], _NN, preferred_element_type=_F32)
        pb = lax.dot_general(yb_ref[...], wb_ref[...], _NN, preferred_element_type=_F32)
        pa_ref[...] = pa.astype(pa_ref.dtype)
        pb_ref[...] = pb.astype(pb_ref.dtype)
        m_ref[...] = (_sigmoid(ga_ref[...]) * pa + _sigmoid(gb_ref[...]) * pb).astype(m_ref.dtype)

    row = pl.BlockSpec((tm, K), lambda i, j: (i, 0))
    col = pl.BlockSpec((K, tn), lambda i, j: (0, j))
    tile = pl.BlockSpec((tm, tn), lambda i, j: (i, j))
    return pl.pallas_call(
        body, name=name, grid=(M // tm, nj),
        in_specs=[row, row, col, col, pl.BlockSpec((tm, tn), lambda i, j: (i, 6 * nj + j)),
                  pl.BlockSpec((tm, tn), lambda i, j: (i, 7 * nj + j))],
        out_specs=[tile] * 3, out_shape=[jax.ShapeDtypeStruct((M, N), _MMT)] * 3,
        compiler_params=_cp("parallel", "parallel"),
    )(ya, yb, w_a, w_b, proj, proj)


def _ada_fwd(c_all, ada_w, name):
    L, D, Ws = ada_w.shape
    B = c_all.shape[0]

    def body(c_ref, w_ref, o_ref):
        o_ref[...] = _dot(_silu(c_ref[...]), w_ref[...])

    return pl.pallas_call(
        body, name=name, grid=(L,),
        in_specs=[_fspec((B, D)), pl.BlockSpec((None, D, Ws), lambda l: (l, 0, 0))],
        out_specs=pl.BlockSpec((None, B, Ws), lambda l: (l, 0, 0)),
        out_shape=jax.ShapeDtypeStruct((L, B, Ws), _F32), compiler_params=_cp("parallel"),
    )(c_all, ada_w)


def _ada_bwd(c_all_t, dmod, name):
    D, B = c_all_t.shape
    L, _, Ws = dmod.shape

    def body(c_ref, d_ref, o_ref):
        ct = _silu(c_ref[...])
        d = d_ref[...]
        acc = ct[:, 0:1] * d[0:1, :]
        for b in range(1, B):
            acc = acc + ct[:, b:b + 1] * d[b:b + 1, :]
        o_ref[...] = acc

    return pl.pallas_call(
        body, name=name, grid=(L,),
        in_specs=[_fspec((D, B)), pl.BlockSpec((None, B, Ws), lambda l: (l, 0, 0))],
        out_specs=pl.BlockSpec((None, D, Ws), lambda l: (l, 0, 0)),
        out_shape=jax.ShapeDtypeStruct((L, D, Ws), _F32), compiler_params=_cp("parallel"),
    )(c_all_t, dmod)


def _norm_mod(x, g, sc, sh, name, tb=512):
    T, D = x.shape
    tb = _pick(T, tb, 8)

    def body(x_ref, g_ref, sc_ref, sh_ref, h_ref):
        xv = x_ref[...]
        r = lax.rsqrt(jnp.mean(xv * xv, axis=1, keepdims=True) + _EPS)
        h_ref[...] = (xv * r * (g_ref[...] * (1.0 + sc_ref[...])) + sh_ref[...]).astype(h_ref.dtype)

    return pl.pallas_call(
        body, name=name, grid=(T // tb,),
        in_specs=[_rspec(tb, D), _fspec((1, D)), _fspec((1, D)), _fspec((1, D))],
        out_specs=_rspec(tb, D), out_shape=jax.ShapeDtypeStruct((T, D), _MMT), compiler_params=_cp("parallel"),
    )(x, g, sc, sh)


def _norm_mod_bwd(x, dh, dres, g, sc, name, tb=256):
    T, D = x.shape
    tb = _pick(T, tb, 8)

    def body(x_ref, dh_ref, dr_ref, g_ref, sc_ref, dx_ref, dgm_ref, dsh_ref):
        i = pl.program_id(0)
        xv, dh_ = x_ref[...], dh_ref[...]
        r = lax.rsqrt(jnp.mean(xv * xv, axis=1, keepdims=True) + _EPS)
        xn = xv * r
        dxn = dh_ * (g_ref[...] * (1.0 + sc_ref[...]))
        dx_ref[...] = dr_ref[...] + r * (dxn - xn * jnp.mean(dxn * xn, axis=1, keepdims=True))

        @pl.when(i == 0)
        def _():
            dgm_ref[...] = jnp.zeros_like(dgm_ref)
            dsh_ref[...] = jnp.zeros_like(dsh_ref)

        dgm_ref[...] += jnp.sum(dh_ * xn, axis=0, keepdims=True)
        dsh_ref[...] += jnp.sum(dh_, axis=0, keepdims=True)

    return pl.pallas_call(
        body, name=name, grid=(T // tb,),
        in_specs=[_rspec(tb, D), _rspec(tb, D), _rspec(tb, D), _fspec((1, D)), _fspec((1, D))],
        out_specs=[_rspec(tb, D), _fspec((1, D)), _fspec((1, D))],
        out_shape=[jax.ShapeDtypeStruct((T, D), _F32), jax.ShapeDtypeStruct((1, D), _F32),
                   jax.ShapeDtypeStruct((1, D), _F32)],
        compiler_params=_cp("arbitrary"),
    )(x, dh, dres, g, sc)


def _resid_bwd(dx, p, gt, name, tb=512):
    T, D = dx.shape
    tb = _pick(T, tb, 8)

    def body(dx_ref, p_ref, gt_ref, dp_ref, dgt_ref):
        i = pl.program_id(0)
        d = dx_ref[...]
        dp_ref[...] = (d * gt_ref[...]).astype(dp_ref.dtype)

        @pl.when(i == 0)
        def _():
            dgt_ref[...] = jnp.zeros_like(dgt_ref)

        dgt_ref[...] += jnp.sum(d * p_ref[...], axis=0, keepdims=True)

    return pl.pallas_call(
        body, name=name, grid=(T // tb,), in_specs=[_rspec(tb, D), _rspec(tb, D), _fspec((1, D))],
        out_specs=[_rspec(tb, D), _fspec((1, D))],
        out_shape=[jax.ShapeDtypeStruct((T, D), _MMT), jax.ShapeDtypeStruct((1, D), _F32)],
        compiler_params=_cp("arbitrary"),
    )(dx, p, gt)


def _gmlp_chunk(u_raw, v_raw, sw_ref, sbt, gv, G):
    u, v = _gelu(u_raw), _gelu(v_raw)
    ii = lax.broadcasted_iota(jnp.int32, (_AC, _AC), 0)
    jj = lax.broadcasted_iota(jnp.int32, (_AC, _AC), 1)
    out = []
    for gi in range(G):
        sl = slice(gi * _GD, (gi + 1) * _GD)
        vg = v[:, sl]
        r = lax.rsqrt(jnp.mean(vg * vg, axis=1, keepdims=True) + _EPS)
        vhat = vg * r
        W = jnp.where(jj <= ii, sw_ref[gi], 0.0)
        s = _dot(W, vhat * gv[:, sl]) + sbt[:, gi:gi + 1]
        out.append((u[:, sl], s, vhat, r, W))
    return out


def _gmlp_fwd(proj, sw, sbt, gv, li, D, name):
    T = proj.shape[0]
    G = D // _GD

    def body(u_ref, v_ref, sw_ref, sbt_ref, gv_ref, y_ref):
        parts = _gmlp_chunk(u_ref[...], v_ref[...], sw_ref, sbt_ref[...], gv_ref[...], G)
        for gi, (u, s, _, _, _) in enumerate(parts):
            y_ref[:, gi * _GD:(gi + 1) * _GD] = (u * s).astype(y_ref.dtype)

    return pl.pallas_call(
        body, name=name, grid=(T // _AC,),
        in_specs=[_rspec(_AC, D, 0), _rspec(_AC, D, 1), _lspec((G, _AC, _AC), li), _lspec((_AC, G), li),
                  _lspec((1, D), li)],
        out_specs=_rspec(_AC, D), out_shape=jax.ShapeDtypeStruct((T, D), _MMT), compiler_params=_cp("parallel"),
    )(proj, proj, sw, sbt, gv)


def _gmlp_bwd(proj, dy, sw, sbt, gv, li, D, into, name):
    T = proj.shape[0]
    G = D // _GD

    def body(u_ref, v_ref, dy_ref, sw_ref, sbt_ref, gv_ref, _, duv_ref, dsw_ref, dsa_ref, dgv_ref):
        i = pl.program_id(0)

        @pl.when(i == 0)
        def _():
            dsw_ref[...] = jnp.zeros_like(dsw_ref)
            dsa_ref[...] = jnp.zeros_like(dsa_ref)
            dgv_ref[...] = jnp.zeros_like(dgv_ref)

        u_raw, v_raw, dy_, gv_ = u_ref[...], v_ref[...], dy_ref[...].astype(_F32), gv_ref[...]
        parts = _gmlp_chunk(u_raw, v_raw, sw_ref, sbt_ref[...], gv_, G)
        ii = lax.broadcasted_iota(jnp.int32, (_AC, _AC), 0)
        jj = lax.broadcasted_iota(jnp.int32, (_AC, _AC), 1)
        dgu, dgv = _dgelu(u_raw), _dgelu(v_raw)
        for gi, (u, s, vhat, r, W) in enumerate(parts):
            sl = slice(gi * _GD, (gi + 1) * _GD)
            dyg = dy_[:, sl]
            ds = dyg * u
            vn = vhat * gv_[:, sl]
            dsw_ref[gi] += jnp.where(jj <= ii, _dot(ds, vn, _NT), 0.0)
            dsa_ref[:, sl] += ds
            dvn = _dot(W, ds, _TN)
            dgv_ref[:, sl] += jnp.sum(dvn * vhat, axis=0, keepdims=True)
            dvh = dvn * gv_[:, sl]
            dvg = r * (dvh - vhat * jnp.mean(dvh * vhat, axis=1, keepdims=True))
            duv_ref[:, sl] = (dyg * s * dgu[:, sl]).astype(duv_ref.dtype)
            duv_ref[:, D + gi * _GD:D + (gi + 1) * _GD] = (dvg * dgv[:, sl]).astype(duv_ref.dtype)

    return pl.pallas_call(
        body, name=name, grid=(T // _AC,),
        in_specs=[_rspec(_AC, D, 0), _rspec(_AC, D, 1), _rspec(_AC, D), _lspec((G, _AC, _AC), li),
                  _lspec((_AC, G), li), _lspec((1, D), li), _HBM],
        out_specs=[_rspec(_AC, 2 * D), _fspec((G, _AC, _AC)), _fspec((_AC, D)), _fspec((1, D))],
        out_shape=[jax.ShapeDtypeStruct(into.shape, into.dtype), jax.ShapeDtypeStruct((G, _AC, _AC), _F32),
                   jax.ShapeDtypeStruct((_AC, D), _F32), jax.ShapeDtypeStruct((1, D), _F32)],
        input_output_aliases={6: 0}, compiler_params=_cp("arbitrary"),
    )(proj, proj, dy, sw, sbt, gv, into)


def _conv_taps(halo, cur, first):
    tb = cur.shape[0]
    full = jnp.concatenate([jnp.where(first, 0.0, halo), cur], axis=0)
    return [full[8:] if j == _KC - 1 else pltpu.roll(full, _KC - 1 - j, 0)[8:] for j in range(_KC)]


def _prev_spec(tb, w, cb):
    return pl.BlockSpec((8, w), lambda i: (jnp.maximum(i * (tb // 8) - 1, 0), cb))


def _l2_heads(x, H):
    outs, rs = [], []
    for h in range(H):
        xh = x[:, h * _GD:(h + 1) * _GD]
        r = lax.rsqrt(jnp.sum(xh * xh, axis=1, keepdims=True) + _EPS)
        outs.append(xh * r)
        rs.append(r)
    return outs, rs


def _gate_rows(ba, alog_row, dtb_row, H):
    lane = lax.broadcasted_iota(jnp.int32, ba.shape, 1)
    beta = _sigmoid(ba)
    g = -jnp.exp(alog_row) * _softplus(ba + dtb_row)
    return lane, beta, g


def _conv_fwd(proj, cw, alog_row, dtb_row, li, D, name, tb=256):
    T = proj.shape[0]
    H = D // _GD
    tb = _pick(T, tb, 8)
    bac = (8 * D) // _LANE

    def body(q_ref, k_ref, v_ref, qh_ref, kh_ref, vh_ref, ba_ref, cw_ref, al_ref, dtb_ref,
             qo_ref, ko_ref, vo_ref, bg_ref):
        first = pl.program_id(0) == 0
        cw_ = cw_ref[...]
        for idx, (cur, halo, out) in enumerate(((q_ref, qh_ref, qo_ref), (k_ref, kh_ref, ko_ref),
                                                 (v_ref, vh_ref, vo_ref))):
            taps = _conv_taps(halo[...], cur[...], first)
            w = cw_[:, idx * D:(idx + 1) * D]
            cv = taps[0] * w[0:1, :]
            for j in range(1, _KC):
                cv = cv + taps[j] * w[j:j + 1, :]
            act = _silu(cv)
            if idx < 2:
                outs, _ = _l2_heads(act, H)
                for h in range(H):
                    out[:, h * _GD:(h + 1) * _GD] = outs[h]
            else:
                out[...] = act
        lane, beta, g = _gate_rows(ba_ref[...], al_ref[...], dtb_ref[...], H)
        bg_ref[...] = jnp.where(lane < H, beta, jnp.where(lane < 2 * H, g, 0.0))

    return pl.pallas_call(
        body, name=name, grid=(T // tb,),
        in_specs=[_rspec(tb, D, 2), _rspec(tb, D, 3), _rspec(tb, D, 4),
                  _prev_spec(tb, D, 2), _prev_spec(tb, D, 3), _prev_spec(tb, D, 4),
                  _rspec(tb, _LANE, bac), _lspec((_KC, 3 * D), li), _lspec((1, _LANE), li), _lspec((1, _LANE), li)],
        out_specs=[_rspec(tb, D), _rspec(tb, D), _rspec(tb, D), _rspec(tb, _LANE)],
        out_shape=[jax.ShapeDtypeStruct((T, D), _F32)] * 3 + [jax.ShapeDtypeStruct((T, _LANE), _F32)],
        compiler_params=_cp("parallel"),
    )(proj, proj, proj, proj, proj, proj, proj, cw, alog_row, dtb_row)


def _conv_bwd1(proj, dqn, dkn, dvs, dbg, cw, alog_row, dtb_row, li, D, into, name, tb=256):
    T = proj.shape[0]
    H = D // _GD
    tb = _pick(T, tb, 8)
    bac = (8 * D) // _LANE

    def body(q_ref, k_ref, v_ref, qh_ref, kh_ref, vh_ref, ba_ref, dq_ref, dk_ref, dv_ref, dbg_ref,
             cw_ref, al_ref, dtb_ref, _, dc_ref, dba_ref, dcw_ref, dal_ref, ddt_ref):
        i = pl.program_id(0)
        first = i == 0

        @pl.when(first)
        def _():
            dcw_ref[...] = jnp.zeros_like(dcw_ref)
            dal_ref[...] = jnp.zeros_like(dal_ref)
            ddt_ref[...] = jnp.zeros_like(ddt_ref)

        cw_ = cw_ref[...]
        for idx, (cur, halo, dref) in enumerate(((q_ref, qh_ref, dq_ref), (k_ref, kh_ref, dk_ref),
                                                  (v_ref, vh_ref, dv_ref))):
            taps = _conv_taps(halo[...], cur[...], first)
            w = cw_[:, idx * D:(idx + 1) * D]
            cv = taps[0] * w[0:1, :]
            for j in range(1, _KC):
                cv = cv + taps[j] * w[j:j + 1, :]
            dact = dref[...]
            if idx < 2:
                outs, rs = _l2_heads(_silu(cv), H)
                pieces = []
                for h in range(H):
                    dy = dact[:, h * _GD:(h + 1) * _GD]
                    pieces.append(rs[h] * (dy - outs[h] * jnp.sum(dy * outs[h], axis=1, keepdims=True)))
                dact = jnp.concatenate(pieces, axis=1)
            dcv = dact * _dsilu(cv)
            dc_ref[:, idx * D:(idx + 1) * D] = dcv
            for j in range(_KC):
                colsum = _dot(jnp.ones((8, tb), _F32), dcv * taps[j])
                dcw_ref[j:j + 1, idx * D:(idx + 1) * D] += colsum[0:1, :]

        ba = ba_ref[...]
        lane, beta, g = _gate_rows(ba, al_ref[...], dtb_ref[...], H)
        dbg_ = dbg_ref[...]
        is_b, is_a = lane < H, jnp.logical_and(lane >= H, lane < 2 * H)
        da = dbg_ * (-jnp.exp(al_ref[...])) * _sigmoid(ba + dtb_ref[...])
        dba_ref[...] = jnp.where(is_b, dbg_ * beta * (1.0 - beta), jnp.where(is_a, da, 0.0)).astype(dba_ref.dtype)
        dal_ref[...] += jnp.sum(jnp.where(is_a, dbg_ * g, 0.0), axis=0, keepdims=True)
        ddt_ref[...] += jnp.sum(jnp.where(is_a, da, 0.0), axis=0, keepdims=True)

    return pl.pallas_call(
        body, name=name, grid=(T // tb,),
        in_specs=[_rspec(tb, D, 2), _rspec(tb, D, 3), _rspec(tb, D, 4),
                  _prev_spec(tb, D, 2), _prev_spec(tb, D, 3), _prev_spec(tb, D, 4),
                  _rspec(tb, _LANE, bac), _rspec(tb, D), _rspec(tb, D), _rspec(tb, D), _rspec(tb, _LANE),
                  _lspec((_KC, 3 * D), li), _lspec((1, _LANE), li), _lspec((1, _LANE), li), _HBM],
        out_specs=[_rspec(tb, 3 * D), _rspec(tb, _LANE, bac), _fspec((_KC, 3 * D)), _fspec((1, _LANE)),
                   _fspec((1, _LANE))],
        out_shape=[jax.ShapeDtypeStruct((T, 3 * D), _F32), jax.ShapeDtypeStruct(into.shape, into.dtype),
                   jax.ShapeDtypeStruct((_KC, 3 * D), _F32), jax.ShapeDtypeStruct((1, _LANE), _F32),
                   jax.ShapeDtypeStruct((1, _LANE), _F32)],
        input_output_aliases={14: 1}, compiler_params=_cp("arbitrary"),
    )(proj, proj, proj, proj, proj, proj, proj, dqn, dkn, dvs, dbg, cw, alog_row, dtb_row, into)


def _conv_bwd2(dc, cw, li, into, name, tb=256):
    T, W3 = dc.shape
    W = W3 // 3
    tb = _pick(T, tb, 8)
    nb8 = T // 8
    nrow = T // tb

    def body(dc_ref, nx_ref, cw_ref, _, o_ref):
        last = pl.program_id(0) == nrow - 1
        full = jnp.concatenate([dc_ref[...], jnp.where(last, 0.0, nx_ref[...])], axis=0)
        w = cw_ref[...]
        acc = full[:tb] * w[_KC - 1:_KC, :]
        for j in range(_KC - 1):
            sh = _KC - 1 - j
            acc = acc + pltpu.roll(full, tb + 8 - sh, 0)[:tb] * w[j:j + 1, :]
        o_ref[...] = acc.astype(o_ref.dtype)

    return pl.pallas_call(
        body, name=name, grid=(nrow, 3),
        in_specs=[pl.BlockSpec((tb, W), lambda i, j: (i, j)),
                  pl.BlockSpec((8, W), lambda i, j: (jnp.minimum((i + 1) * (tb // 8), nb8 - 1), j)),
                  pl.BlockSpec((None, _KC, W), lambda i, j: (li, 0, j)), _HBM],
        out_specs=pl.BlockSpec((tb, W), lambda i, j: (i, 2 + j)),
        out_shape=jax.ShapeDtypeStruct(into.shape, into.dtype), input_output_aliases={3: 0},
        compiler_params=_cp("parallel", "parallel"),
    )(dc, dc, cw, into)


def _split(a):
    hi = a.astype(_BF)
    return hi, (a - hi.astype(_F32)).astype(_BF)


def _dot3(a, b):
    (ah, al), (bh, bl) = a, b
    f = functools.partial(lax.dot_general, dimension_numbers=_NN, preferred_element_type=_F32)
    return f(ah, bh) + f(ah, bl) + f(al, bh)


def _inv_unit_lower(mats):
    C = mats[0].shape[0]
    ii = lax.broadcasted_iota(jnp.int32, (C, C), 0)
    jj = lax.broadcasted_iota(jnp.int32, (C, C), 1)
    xs = [jnp.where(ii == jj, 1.0, 0.0) - a for a in mats]
    ps = list(mats)
    n = 1
    while 2 * n < C:
        sp = [_split(p) for p in ps]
        ps = [_dot3(s, s) for s in sp]
        sp = [_split(p) for p in ps]
        xs = [x + _dot3(_split(x), s) for x, s in zip(xs, sp)]
        n *= 2
    return xs


def _gdn_chunk(q, k, v, g_row, b_row):
    C = q.shape[0]
    ii = lax.broadcasted_iota(jnp.int32, (C, C), 0)
    jj = lax.broadcasted_iota(jnp.int32, (C, C), 1)
    low, strict, eye = jj <= ii, jj < ii, ii == jj
    g_col = jnp.sum(jnp.where(eye, g_row, 0.0), axis=1, keepdims=True)
    b_col = jnp.sum(jnp.where(eye, b_row, 0.0), axis=1, keepdims=True)
    gam_col = jnp.sum(jnp.where(low, g_row, 0.0), axis=1, keepdims=True)
    gam_row = jnp.sum(jnp.where(jj >= ii, g_col, 0.0), axis=0, keepdims=True)
    gam_last = jnp.sum(g_row, axis=1, keepdims=True)
    decay = jnp.where(low, jnp.exp(jnp.where(low, gam_col - gam_row, 0.0)), 0.0)
    eg = jnp.exp(gam_col)
    ekd = jnp.exp(gam_last - gam_col)
    qs = q * (_GD ** -0.5)
    kb = k * b_col
    kk = _dot(kb, k, _NT)
    qkraw = _dot(qs, k, _NT)
    return dict(low=low, strict=strict, eye=eye, ii=ii, jj=jj, b_col=b_col, decay=decay, eg=eg, ekd=ekd,
                gl=jnp.exp(gam_last), qs=qs, kb=kb, kk=kk, qkraw=qkraw,
                A=jnp.where(strict, kk * decay, 0.0), vb=v * b_col, kbg=kb * eg,
                qk=qkraw * decay, q_dec=qs * eg, k_dec=k * ekd)


def _gdn_fwd(qn, kn, vs, g_r, b_r, name):
    T, D = qn.shape
    H, N, C = D // _GD, T // _BC, _BC
    hb = min(_HB, H)

    def body(q_ref, k_ref, v_ref, g_ref, b_ref, o_ref, s_ref, t_ref, S):
        @pl.when(pl.program_id(1) == 0)
        def _():
            S[...] = jnp.zeros_like(S)

        hs = range(hb)
        sls = [slice(hh * _GD, (hh + 1) * _GD) for hh in hs]
        cms = [_gdn_chunk(q_ref[:, sl], k_ref[:, sl], v_ref[:, sl], g_ref[hh], b_ref[hh]) for hh, sl in zip(hs, sls)]
        tms = _inv_unit_lower([cm["A"] for cm in cms])
        us = [_dot(tm, cm["vb"]) for tm, cm in zip(tms, cms)]
        ws = [_dot(tm, cm["kbg"]) for tm, cm in zip(tms, cms)]
        s0s = [S[hh] for hh in hs]
        for hh in hs:
            s_ref[hh] = s0s[hh]
            t_ref[hh] = tms[hh]
        v_news = [u - _dot(w, s0) for u, w, s0 in zip(us, ws, s0s)]
        qss = [_dot(cm["q_dec"], s0) for cm, s0 in zip(cms, s0s)]
        for hh in hs:
            o_ref[:, sls[hh]] = qss[hh] + _dot(cms[hh]["qk"], v_news[hh])
        for hh in hs:
            S[hh] = s0s[hh] * cms[hh]["gl"] + _dot(cms[hh]["k_dec"], v_news[hh], _TN)

    qspec = pl.BlockSpec((C, hb * _GD), lambda h, n: (n, h))
    gspec = pl.BlockSpec((hb, None, 1, C), lambda h, n: (h, n, 0, 0))
    return pl.pallas_call(
        body, name=name, grid=(H // hb, N),
        in_specs=[qspec, qspec, qspec, gspec, gspec],
        out_specs=[qspec, pl.BlockSpec((hb, None, _GD, _GD), lambda h, n: (h, n, 0, 0)),
                   pl.BlockSpec((hb, None, C, C), lambda h, n: (h, n, 0, 0))],
        out_shape=[jax.ShapeDtypeStruct((T, D), _F32), jax.ShapeDtypeStruct((H, N, _GD, _GD), _F32),
                   jax.ShapeDtypeStruct((H, N, C, C), _F32)],
        scratch_shapes=[pltpu.VMEM((hb, _GD, _GD), _F32)],
        compiler_params=_cp("arbitrary", "arbitrary"),
    )(qn, kn, vs, g_r, b_r)


def _gdn_bwd(qn, kn, vs, g_r, b_r, s_all, t_all, do, name):
    T, D = qn.shape
    H, N, C = D // _GD, T // _BC, _BC
    hb = min(_HB, H)

    def body(q_ref, k_ref, v_ref, g_ref, b_ref, s_ref, t_ref, do_ref, dq_ref, dk_ref, dv_ref, dg_ref, db_ref, dS):
        @pl.when(pl.program_id(1) == 0)
        def _():
            dS[...] = jnp.zeros_like(dS)

        hs = range(hb)
        sls = [slice(hh * _GD, (hh + 1) * _GD) for hh in hs]
        ks = [k_ref[:, sl] for sl in sls]
        vs_ = [v_ref[:, sl] for sl in sls]
        cms = [_gdn_chunk(q_ref[:, sl], k, v, g_ref[hh], b_ref[hh]) for hh, sl, k, v in zip(hs, sls, ks, vs_)]
        low, strict, eye, ii, jj = (cms[0][n] for n in ("low", "strict", "eye", "ii", "jj"))
        tms, s0s, dos, ds1s = [t_ref[hh] for hh in hs], [s_ref[hh] for hh in hs], [do_ref[:, sl] for sl in sls], \
            [dS[hh] for hh in hs]
        us = [_dot(tm, cm["vb"]) for tm, cm in zip(tms, cms)]
        ws = [_dot(tm, cm["kbg"]) for tm, cm in zip(tms, cms)]
        v_news = [u - _dot(w, s0) for u, w, s0 in zip(us, ws, s0s)]
        dv_news = [_dot(cm["qk"], do_, _TN) + _dot(cm["k_dec"], ds1) for cm, do_, ds1 in zip(cms, dos, ds1s)]
        dqks = [jnp.where(low, _dot(do_, vn, _NT), 0.0) for do_, vn in zip(dos, v_news)]
        dq_decs = [_dot(do_, s0, _NT) for do_, s0 in zip(dos, s0s)]
        dk_decs = [_dot(vn, ds1, _NT) for vn, ds1 in zip(v_news, ds1s)]
        dgls = [jnp.sum(jnp.sum(ds1 * s0, axis=1, keepdims=True), axis=0, keepdims=True) for ds1, s0 in zip(ds1s, s0s)]
        dws = [-_dot(dvn, s0, _NT) for dvn, s0 in zip(dv_news, s0s)]
        for hh in hs:
            dS[hh] = (_dot(cms[hh]["q_dec"], dos[hh], _TN) + cms[hh]["gl"] * ds1s[hh]
                      - _dot(ws[hh], dv_news[hh], _TN))
        dvbs = [_dot(tm, dvn, _TN) for tm, dvn in zip(tms, dv_news)]
        dkbgs = [_dot(tm, dw, _TN) for tm, dw in zip(tms, dws)]
        dAs = [-jnp.where(strict, _dot(dvb, u, _NT) + _dot(dkbg, w, _NT), 0.0)
               for dvb, u, dkbg, w in zip(dvbs, us, dkbgs, ws)]
        dkks = [dA * cm["decay"] for dA, cm in zip(dAs, cms)]
        dqkraws = [dqk * cm["decay"] for dqk, cm in zip(dqks, cms)]
        Es = [(dA * cm["kk"] + dqk * cm["qkraw"]) * cm["decay"] for dA, dqk, cm in zip(dAs, dqks, cms)]
        dkbs = [_dot(dkk, k) + dkbg * cm["eg"] for dkk, k, dkbg, cm in zip(dkks, ks, dkbgs, cms)]
        dqss = [_dot(dqr, k) + dqd * cm["eg"] for dqr, k, dqd, cm in zip(dqkraws, ks, dq_decs, cms)]
        for hh in hs:
            cm = cms[hh]
            dk_ref[:, sls[hh]] = (_dot(dqkraws[hh], cm["qs"], _TN) + _dot(dkks[hh], cm["kb"], _TN)
                                  + dk_decs[hh] * cm["ekd"] + dkbs[hh] * cm["b_col"])
            dv_ref[:, sls[hh]] = dvbs[hh] * cm["b_col"]
            dq_ref[:, sls[hh]] = dqss[hh] * (_GD ** -0.5)
        for hh in hs:
            cm, k, E = cms[hh], ks[hh], Es[hh]
            eg, ekd = cm["eg"], cm["ekd"]
            dbeta_col = jnp.sum(dvbs[hh] * vs_[hh] + dkbs[hh] * k, axis=1, keepdims=True)
            t_kd = jnp.sum(dk_decs[hh] * k, axis=1, keepdims=True) * ekd
            c1 = (jnp.sum(E, axis=1, keepdims=True) + jnp.sum(dkbgs[hh] * cm["kb"], axis=1, keepdims=True) * eg
                  + jnp.sum(dq_decs[hh] * cm["qs"], axis=1, keepdims=True) * eg - t_kd)
            r1 = jnp.sum(E, axis=0, keepdims=True)
            dgam_last = jnp.sum(t_kd, axis=0, keepdims=True) + dgls[hh] * cm["gl"]
            dgam_col = c1 - jnp.sum(jnp.where(eye, r1, 0.0), axis=1, keepdims=True)
            dg_ref[hh] = jnp.sum(jnp.where(ii >= jj, dgam_col, 0.0), axis=0, keepdims=True) + dgam_last
            db_ref[hh] = jnp.sum(jnp.where(eye, dbeta_col, 0.0), axis=0, keepdims=True)

    qspec = pl.BlockSpec((C, hb * _GD), lambda h, n: (N - 1 - n, h))
    gspec = pl.BlockSpec((hb, None, 1, C), lambda h, n: (h, N - 1 - n, 0, 0))
    return pl.pallas_call(
        body, name=name, grid=(H // hb, N),
        in_specs=[qspec, qspec, qspec, gspec, gspec,
                  pl.BlockSpec((hb, None, _GD, _GD), lambda h, n: (h, N - 1 - n, 0, 0)),
                  pl.BlockSpec((hb, None, C, C), lambda h, n: (h, N - 1 - n, 0, 0)), qspec],
        out_specs=[qspec, qspec, qspec, gspec, gspec],
        out_shape=[jax.ShapeDtypeStruct((T, D), _F32)] * 3 + [jax.ShapeDtypeStruct((H, N, 1, C), _F32)] * 2,
        scratch_shapes=[pltpu.VMEM((hb, _GD, _GD), _F32)],
        compiler_params=_cp("arbitrary", "arbitrary"),
    )(qn, kn, vs, g_r, b_r, s_all, t_all, do)


def _onorm_fwd(o, proj, go, li, D, name, tb=512):
    T = o.shape[0]
    H = D // _GD
    tb = _pick(T, tb, 8)

    def body(o_ref, z_ref, go_ref, y_ref):
        ov, zv, g = o_ref[...], z_ref[...], go_ref[...]
        for h in range(H):
            sl = slice(h * _GD, (h + 1) * _GD)
            oh = ov[:, sl]
            r = lax.rsqrt(jnp.mean(oh * oh, axis=1, keepdims=True) + _EPS)
            y_ref[:, sl] = (oh * r * g * _silu(zv[:, sl])).astype(y_ref.dtype)

    return pl.pallas_call(
        body, name=name, grid=(T // tb,), in_specs=[_rspec(tb, D), _rspec(tb, D, 5), _lspec((1, _GD), li)],
        out_specs=_rspec(tb, D), out_shape=jax.ShapeDtypeStruct((T, D), _MMT), compiler_params=_cp("parallel"),
    )(o, proj, go)


def _onorm_bwd(dy, o, proj, go, li, D, into, name, tb=256):
    T = o.shape[0]
    H = D // _GD
    tb = _pick(T, tb, 8)

    def body(dy_ref, o_ref, z_ref, go_ref, _, do_ref, dz_ref, dgo_ref):
        @pl.when(pl.program_id(0) == 0)
        def _():
            dgo_ref[...] = jnp.zeros_like(dgo_ref)

        dyv, ov, zv, g = dy_ref[...].astype(_F32), o_ref[...], z_ref[...], go_ref[...]
        dgo = jnp.zeros((1, _GD), _F32)
        for h in range(H):
            sl = slice(h * _GD, (h + 1) * _GD)
            oh, zh, dyh = ov[:, sl], zv[:, sl], dyv[:, sl]
            r = lax.rsqrt(jnp.mean(oh * oh, axis=1, keepdims=True) + _EPS)
            on = oh * r
            sz = _silu(zh)
            dgo = dgo + jnp.sum(dyh * sz * on, axis=0, keepdims=True)
            don = dyh * sz * g
            do_ref[:, sl] = r * (don - on * jnp.mean(don * on, axis=1, keepdims=True))
            dz_ref[:, sl] = (dyh * on * g * _dsilu(zh)).astype(dz_ref.dtype)
        dgo_ref[...] += dgo

    return pl.pallas_call(
        body, name=name, grid=(T // tb,),
        in_specs=[_rspec(tb, D), _rspec(tb, D), _rspec(tb, D, 5), _lspec((1, _GD), li), _HBM],
        out_specs=[_rspec(tb, D), _rspec(tb, D, 5), _fspec((1, _GD))],
        out_shape=[jax.ShapeDtypeStruct((T, D), _F32), jax.ShapeDtypeStruct(into.shape, into.dtype),
                   jax.ShapeDtypeStruct((1, _GD), _F32)],
        input_output_aliases={4: 1}, compiler_params=_cp("arbitrary"),
    )(dy, o, proj, go, into)


def _merge_bwd(dm, pa, pb, proj, D, name, tb=256):
    T, PW = proj.shape
    tb = _pick(T, tb, 8)

    def body(dm_ref, pa_ref, pb_ref, ga_ref, gb_ref, dpa_ref, dpb_ref, dg_ref):
        d = dm_ref[...].astype(_F32)
        sa, sb = _sigmoid(ga_ref[...]), _sigmoid(gb_ref[...])
        dpa_ref[...] = (d * sa).astype(dpa_ref.dtype)
        dpb_ref[...] = (d * sb).astype(dpb_ref.dtype)
        dg_ref[:, :D] = (d * pa_ref[...].astype(_F32) * sa * (1.0 - sa)).astype(dg_ref.dtype)
        dg_ref[:, D:] = (d * pb_ref[...].astype(_F32) * sb * (1.0 - sb)).astype(dg_ref.dtype)

    return pl.pallas_call(
        body, name=name, grid=(T // tb,),
        in_specs=[_rspec(tb, D), _rspec(tb, D), _rspec(tb, D), _rspec(tb, D, 6), _rspec(tb, D, 7)],
        out_specs=[_rspec(tb, D), _rspec(tb, D), _rspec(tb, 2 * D, 3)],
        out_shape=[jax.ShapeDtypeStruct((T, D), _MMT)] * 2 + [jax.ShapeDtypeStruct((T, PW), _MMT)],
        compiler_params=_cp("parallel"),
    )(dm, pa, pb, proj, proj)


def _swiglu_bwd(da, gate, up, name, tb=256):
    T, F = gate.shape
    F2 = 2 * F
    tb = _pick(T, tb, 8)

    def body(da_ref, g_ref, u_ref, o_ref):
        d, g = da_ref[...].astype(_F32), g_ref[...].astype(_F32)
        o_ref[:, :F] = (d * u_ref[...].astype(_F32) * _dsilu(g)).astype(o_ref.dtype)
        o_ref[:, F:] = (d * _silu(g)).astype(o_ref.dtype)

    return pl.pallas_call(
        body, name=name, grid=(T // tb,), in_specs=[_rspec(tb, F), _rspec(tb, F), _rspec(tb, F)],
        out_specs=_rspec(tb, F2), out_shape=jax.ShapeDtypeStruct((T, F2), _MMT), compiler_params=_cp("parallel"),
    )(da, gate, up)


def _loss_head(x, tgt, fg, name, tb=256):
    T, D = x.shape
    tb = _pick(T, tb, 8)

    def body(x_ref, t_ref, fg_ref, loss_ref, dx_ref, dfg_ref):
        @pl.when(pl.program_id(0) == 0)
        def _():
            loss_ref[...] = jnp.zeros_like(loss_ref)
            dfg_ref[...] = jnp.zeros_like(dfg_ref)

        xv, fg_ = x_ref[...], fg_ref[...]
        r = lax.rsqrt(jnp.mean(xv * xv, axis=1, keepdims=True) + _EPS)
        xn = xv * r
        e = xn * fg_ - t_ref[...]
        loss_ref[...] += (0.5 / D) * jnp.sum(jnp.sum(e * e, axis=1, keepdims=True), axis=0, keepdims=True)
        dy = e * (1.0 / D)
        dfg_ref[...] += jnp.sum(dy * xn, axis=0, keepdims=True)
        dxn = dy * fg_
        dx_ref[...] = r * (dxn - xn * jnp.mean(dxn * xn, axis=1, keepdims=True))

    return pl.pallas_call(
        body, name=name, grid=(T // tb,), in_specs=[_rspec(tb, D), _rspec(tb, D), _fspec((1, D))],
        out_specs=[_fspec((1, 1)), _rspec(tb, D), _fspec((1, D))],
        out_shape=[jax.ShapeDtypeStruct((1, 1), _F32), jax.ShapeDtypeStruct((T, D), _F32),
                   jax.ShapeDtypeStruct((1, D), _F32)],
        compiler_params=_cp("arbitrary"),
    )(x, tgt, fg)


def _row_tile(R, W, budget=1 << 20, unit=8):
    if R * W * 4 <= budget or R % unit:
        return R
    best = unit
    for t in range(unit, R + 1, unit):
        if R % t == 0 and t * W * 4 <= budget:
            best = t
    return best


def _add_own_half(send, got, half, name):
    P, Rp, W = send.shape
    Rh = Rp // 2
    tb = _row_tile(Rh, W, 1 << 21, 16)

    def body(h_ref, a_ref, b_ref, o_ref):
        o_ref[...] = (a_ref[...].astype(_F32) + b_ref[...].astype(_F32)).astype(o_ref.dtype)

    return pl.pallas_call(
        body, name=name,
        grid_spec=pltpu.PrefetchScalarGridSpec(
            num_scalar_prefetch=1, grid=(P, Rh // tb),
            in_specs=[pl.BlockSpec((None, None, tb, W), lambda k, i, h: (k, h[0], i, 0)),
                      pl.BlockSpec((None, tb, W), lambda k, i, h: (k, i, 0))],
            out_specs=pl.BlockSpec((None, tb, W), lambda k, i, h: (k, i, 0))),
        out_shape=jax.ShapeDtypeStruct((P, Rh, W), send.dtype), compiler_params=_cp("parallel", "parallel"),
    )(half, send.reshape(P, 2, Rh, W), got)


def _sum_slots(st, name):
    P, R, W = st.shape
    tb = _row_tile(R, W, 1 << 20, 16)

    def body(s_ref, o_ref):
        acc = s_ref[0].astype(_F32)
        for p in range(1, P):
            acc = acc + s_ref[p].astype(_F32)
        o_ref[...] = acc

    return pl.pallas_call(
        body, name=name, grid=(R // tb,), in_specs=[pl.BlockSpec((P, tb, W), lambda i: (0, i, 0))],
        out_specs=_rspec(tb, W), out_shape=jax.ShapeDtypeStruct((R, W), _F32), compiler_params=_cp("parallel"),
    )(st)


def _adamw(w, gst, m, v, name):
    R, W = w.shape
    P = gst.shape[0]
    tb = _row_tile(R, W, 1 << 20)
    c1, c2 = 1.0 - _B1 ** _STEP, 1.0 - _B2 ** _STEP

    def body(w_ref, g_ref, m_ref, v_ref, go_ref, d_ref, mo_ref, vo_ref):
        g = g_ref[0]
        for p in range(1, P):
            g = g + g_ref[p]
        mn = _B1 * m_ref[...] + (1.0 - _B1) * g
        vn = _B2 * v_ref[...] + (1.0 - _B2) * (g * g)
        go_ref[...] = g
        mo_ref[...] = mn
        vo_ref[...] = vn
        d_ref[...] = -_LR * ((mn / c1) / (jnp.sqrt(vn / c2) + _AEPS) + _WD * w_ref[...])

    spec = _rspec(tb, W)
    return pl.pallas_call(
        body, name=name, grid=(R // tb,),
        in_specs=[spec, pl.BlockSpec((P, tb, W), lambda i: (0, i, 0)), spec, spec],
        out_specs=[spec] * 4, out_shape=[jax.ShapeDtypeStruct((R, W), _F32)] * 4, compiler_params=_cp("parallel"),
    )(w, gst, m, v)


def _as2d(a):
    if a.ndim == 1:
        return a.reshape(1, -1)
    return a.reshape(-1, a.shape[-1])


def kernel(x, c, ada_w, ada_b, norm1_g, w_in, conv_w, spatial_w, spatial_b, v_norm_g, a_log, dt_bias, o_norm_g, w_branch_a, w_branch_b, w_out, norm2_g, w_ffn_in, w_ffn_out, final_g, loss_target, m_ada_w, m_ada_b, m_norm1_g, m_w_in, m_conv_w, m_spatial_w, m_spatial_b, m_v_norm_g, m_a_log, m_dt_bias, m_o_norm_g, m_w_branch_a, m_w_branch_b, m_w_out, m_norm2_g, m_w_ffn_in, m_w_ffn_out, m_final_g, v_ada_w, v_ada_b, v_norm1_g, v_w_in, v_conv_w, v_spatial_w, v_spatial_b, v_v_norm_g, v_a_log, v_dt_bias, v_o_norm_g, v_w_branch_a, v_w_branch_b, v_w_out, v_norm2_g, v_w_ffn_in, v_w_ffn_out, v_final_g):
    xb, tgt = x[0], loss_target[0]
    T, D = xb.shape
    L, H, G = ada_w.shape[0], a_log.shape[1], spatial_w.shape[1]
    F = 4 * w_ffn_out.shape[1]
    N = T // _BC
    Ws = ada_w.shape[2]
    Wc = w_in.shape[2]
    PW = 8 * D + _LANE
    ix, iy, ic = lax.axis_index("x"), lax.axis_index("y"), lax.axis_index("c")
    me = 4 * ix + 2 * iy + ic

    c_all = _gather8(c, "gather_c").reshape(8, D)
    modp = _ada_fwd(c_all, ada_w, "ada_fwd")
    n_mod, n_cw = L * 8 * Ws, L * _KC * conv_w.shape[2]
    pad = (-(n_mod + n_cw)) % _LANE
    pay = jnp.concatenate([modp.reshape(-1), conv_w.reshape(-1), jnp.zeros((pad,), _F32)]).reshape(-1, _LANE)
    pay_all = _gather8(pay, "gather_mod").reshape(8, -1)
    mod_full = jnp.concatenate([pay_all[2 * k, :n_mod].reshape(L, 8, Ws) for k in range(4)], axis=-1)
    cw_full = jnp.concatenate([pay_all[2 * k, n_mod:n_mod + n_cw].reshape(L, _KC, -1) for k in range(4)], axis=-1)
    mod = lax.dynamic_index_in_dim(mod_full, me, axis=1, keepdims=False) + ada_b
    mods = [[mod[l, j * D:(j + 1) * D].reshape(1, D) for j in range(6)] for l in range(L)]

    big = [w_in, w_branch_a, w_branch_b, w_out, w_ffn_in, w_ffn_out]
    chip = 2 * ix + iy
    starts = [(k * Wc) // 16 * 16 for k in range(4)]
    Hh = max(-(-((k + 1) * Wc) // 16) * 16 - starts[k] for k in range(4))
    No = max(s + Hh for s in starts)
    my_off = jnp.asarray([k * Wc - starts[k] for k in range(4)], jnp.int32)[chip]
    cuts = sorted(set(starts + [s + Hh for s in starts]))

    pers = [Hh, D // 4, D // 4, D // 4, 2 * F // 4, F // 4]
    roff = [0]
    for p in pers:
        roff.append(roff[-1] + L * p)
    Rp = -(-roff[-1] // (32 * _NCH)) * (32 * _NCH)
    rpad = Rp - roff[-1]

    hull = lax.dynamic_update_slice(jnp.zeros((L, Hh, D), _F32), jnp.transpose(w_in, (0, 2, 1)), (0, my_off, 0))
    shard = jnp.concatenate(
        [hull.reshape(-1, D).astype(_MMT), w_branch_a.reshape(-1, D).astype(_MMT),
         w_branch_b.reshape(-1, D).astype(_MMT), w_out.reshape(-1, D).astype(_MMT),
         jnp.transpose(w_ffn_in, (0, 2, 1)).reshape(-1, D).astype(_MMT), w_ffn_out.reshape(-1, D).astype(_MMT),
         jnp.zeros((rpad, D), _MMT)], axis=0)
    gw = _fill_from_sibling(_gather_chips(shard, "gather_w"), "gather_w_sib")

    def slab(i, l, k):
        a = roff[i] + l * pers[i]
        return gw[k, a:a + pers[i]]

    def joined(i, l):
        return jnp.concatenate([slab(i, l, k) for k in range(4)], axis=0)

    def orig_rows(hulls, a, b):
        edges = sorted(set([a, b] + [c_ for c_ in cuts if a < c_ < b]))
        out = []
        for lo, hi in zip(edges[:-1], edges[1:]):
            cov = [k for k in range(4) if starts[k] <= lo and hi <= starts[k] + Hh]
            piece = hulls[cov[0]][lo - starts[cov[0]]:hi - starts[cov[0]]]
            for k in cov[1:]:
                piece = piece + hulls[k][lo - starts[k]:hi - starts[k]]
            out.append(piece)
        return out

    wt_in_p = []
    for l in range(L):
        hulls = [slab(0, l, k) for k in range(4)]
        wt_in_p.append(jnp.concatenate(
            orig_rows(hulls, 0, 6 * D) + orig_rows(hulls, 6 * D + 2 * H, 8 * D + 2 * H)
            + orig_rows(hulls, 6 * D, 6 * D + 2 * H) + [jnp.zeros((_LANE - 2 * H, D), _MMT)], axis=0))
    w_a, w_b, w_o, wt_fi, w_fo = ([joined(i, l) for l in range(L)] for i in range(1, 6))

    sbt = jnp.transpose(spatial_b, (0, 2, 1))
    gv3, go3 = v_norm_g.reshape(L, 1, D), o_norm_g.reshape(L, 1, _GD)
    zpad = jnp.zeros((L, _LANE - 2 * H), _F32)
    alog_row = jnp.concatenate([jnp.zeros((L, H), _F32), a_log, zpad], axis=1).reshape(L, 1, _LANE)
    dtb_row = jnp.concatenate([jnp.zeros((L, H), _F32), dt_bias, zpad], axis=1).reshape(L, 1, _LANE)

    def rows_of(tok):
        return jnp.transpose(tok.reshape(N, _BC, H), (2, 0, 1)).reshape(H, N, 1, _BC)

    def toks_of(rows):
        return jnp.transpose(rows.reshape(H, N, _BC), (1, 2, 0)).reshape(T, H)

    saved = []
    xc = xb
    for l in range(L):
        sh1, sc1, gt1, sh2, sc2, gt2 = mods[l]
        g1, g2 = norm1_g[l].reshape(1, D), norm2_g[l].reshape(1, D)
        h = _norm_mod(xc, g1, sc1, sh1, f"norm1_{l}")
        proj = _mm(h, wt_in_p[l], f"proj_{l}", trans_b=True, tn=1664)
        ya = _gmlp_fwd(proj, spatial_w, sbt, gv3, l, D, f"gmlp_{l}")
        qn, kn, vs, bg = _conv_fwd(proj, cw_full, alog_row, dtb_row, l, D, f"conv_{l}")
        g_r, b_r = rows_of(bg[:, H:2 * H]), rows_of(bg[:, :H])
        o, s_all, t_all = _gdn_fwd(qn, kn, vs, g_r, b_r, f"gdn_{l}")
        yb = _onorm_fwd(o, proj, go3, l, D, f"onorm_{l}")
        pa, pb, mg = _branch_merge(ya, yb, w_a[l], w_b[l], proj, f"branch_{l}")
        p1, x1 = _mm_res(mg, w_o[l], xc, gt1, f"wout_{l}")
        h2 = _norm_mod(x1, g2, sc2, sh2, f"norm2_{l}")
        gate, up, act = _ffin_swiglu(h2, wt_fi[l], f"ffin_{l}")
        p2, x2 = _mm_res(act, w_fo[l], x1, gt2, f"ffout_{l}")
        saved.append(dict(x=xc, h=h, proj=proj, ya=ya, yb=yb, qn=qn, kn=kn, vs=vs, g_r=g_r, b_r=b_r, o=o,
                          s_all=s_all, t_all=t_all, pa=pa, pb=pb, mg=mg, p1=p1, x1=x1, h2=h2, gate=gate, up=up,
                          act=act, p2=p2))
        xc = x2

    loss11, dx, dfg = _loss_head(xc, tgt, final_g.reshape(1, D), "loss_head")
    loss = lax.psum(loss11[0, 0], ("x", "y", "c"))

    gbig = {k: [None] * L for k in ("w_in", "w_a", "w_b", "w_o", "w_fi", "w_fo")}
    small = {k: [None] * L for k in ("dmod", "n1", "n2", "sw", "sb", "gv", "cw", "al", "dt", "go")}
    for l in reversed(range(L)):
        sv = saved[l]
        sh1, sc1, gt1, sh2, sc2, gt2 = mods[l]
        g1, g2 = norm1_g[l].reshape(1, D), norm2_g[l].reshape(1, D)
        proj = sv["proj"]
        dp2, dgt2 = _resid_bwd(dx, sv["p2"], gt2, f"res2b_{l}")
        da = _mm(dp2, w_fo[l], f"ffoutb_{l}", trans_b=True, out_dtype=_MMT)
        gbig["w_fo"][l] = _mm_tn(sv["act"], dp2, f"ffoutw_{l}")
        dgu = _swiglu_bwd(da, sv["gate"], sv["up"], f"swiglub_{l}")
        dh2 = _mm(dgu, wt_fi[l], f"ffinb_{l}")
        gbig["w_fi"][l] = _mm_tn(dgu, sv["h2"], f"ffinw_{l}")
        dx1, dgm2, dsh2 = _norm_mod_bwd(sv["x1"], dh2, dx, g2, sc2, f"norm2b_{l}")
        dp1, dgt1 = _resid_bwd(dx1, sv["p1"], gt1, f"res1b_{l}")
        dmg = _mm(dp1, w_o[l], f"woutb_{l}", trans_b=True, out_dtype=_MMT)
        gbig["w_o"][l] = _mm_tn(sv["mg"], dp1, f"woutw_{l}")
        dpa, dpb, dproj = _merge_bwd(dmg, sv["pa"], sv["pb"], proj, D, f"mergeb_{l}")
        dya = _mm(dpa, w_a[l], f"brab_{l}", trans_b=True, out_dtype=_MMT)
        gbig["w_a"][l] = _mm_tn(sv["ya"], dpa, f"braw_{l}")
        dyb = _mm(dpb, w_b[l], f"brbb_{l}", trans_b=True, out_dtype=_MMT)
        gbig["w_b"][l] = _mm_tn(sv["yb"], dpb, f"brbw_{l}")
        dproj, dsw, dsa, dgv = _gmlp_bwd(proj, dya, spatial_w, sbt, gv3, l, D, dproj, f"gmlpb_{l}")
        do, dproj, dgo = _onorm_bwd(dyb, sv["o"], proj, go3, l, D, dproj, f"onormb_{l}")
        dqn, dkn, dvs, dg_r, db_r = _gdn_bwd(sv["qn"], sv["kn"], sv["vs"], sv["g_r"], sv["b_r"], sv["s_all"],
                                             sv["t_all"], do, f"gdnb_{l}")
        dbg = jnp.concatenate([toks_of(db_r), toks_of(dg_r), jnp.zeros((T, _LANE - 2 * H), _F32)], axis=1)
        dc, dproj, dcw, dal, ddt = _conv_bwd1(proj, dqn, dkn, dvs, dbg, cw_full, alog_row, dtb_row, l, D, dproj,
                                              f"convb_{l}")
        dproj = _conv_bwd2(dc, cw_full, l, dproj, f"convx_{l}")
        dh = _mm(dproj, wt_in_p[l], f"projb_{l}", tk=1664)
        gbig["w_in"][l] = _mm_tn(dproj, sv["h"], f"projw_{l}", tm=640)
        dx, dgm1, dsh1 = _norm_mod_bwd(sv["x"], dh, dx1, g1, sc1, f"norm1b_{l}")
        small["dmod"][l] = jnp.concatenate([dsh1, dgm1 * g1, dgt1, dsh2, dgm2 * g2, dgt2], axis=1)
        small["n1"][l], small["n2"][l] = dgm1 * (1.0 + sc1), dgm2 * (1.0 + sc2)
        small["sw"][l], small["gv"][l], small["cw"][l], small["go"][l] = dsw, dgv, dcw, dgo
        small["sb"][l] = jnp.transpose(dsa.reshape(_AC, G, _GD).sum(axis=-1))
        small["al"][l], small["dt"][l] = dal[:, H:2 * H], ddt[:, H:2 * H]
    grad_x = dx.reshape(1, T, D)

    names_small = ["dmod", "n1", "n2", "sw", "sb", "gv", "cw", "al", "dt", "go"]
    flat = [jnp.stack(small[k]).reshape(-1) for k in names_small] + [dfg.reshape(-1)]
    sizes = [f.shape[0] for f in flat]
    tot = sum(sizes)
    pad = (-tot) % 1024
    pay = jnp.concatenate(flat + [jnp.zeros((pad,), _F32)]).reshape(-1, 1024)
    sm_all = _gather8(pay, "gather_small").reshape(8, -1)
    offs = [0]
    for s in sizes:
        offs.append(offs[-1] + s)
    part = {k: sm_all[:, offs[i]:offs[i + 1]] for i, k in enumerate(names_small + ["fg"])}
    dmod_all = part["dmod"].reshape(8, L, 6 * D)

    outs = {}

    def update(nm, w, gst, m, v):
        shp = w.shape
        w2 = _as2d(w)
        g, d, mn, vn = _adamw(w2, gst.reshape((gst.shape[0],) + w2.shape), _as2d(m), _as2d(v), f"adamw_{nm}")
        outs[nm] = (g.reshape(shp), d.reshape(shp), mn.reshape(shp), vn.reshape(shp))

    chip = 2 * ix + iy
    dmod_t = jnp.transpose(dmod_all, (1, 0, 2))
    dmod_mine = lax.dynamic_slice_in_dim(dmod_t, chip * Ws, Ws, axis=2)
    g_ada_w = _ada_bwd(jnp.transpose(c_all), dmod_mine, "ada_bwd")
    update("ada_w", ada_w, g_ada_w[None], m_ada_w, v_ada_w)
    update("ada_b", ada_b, dmod_all, m_ada_b, v_ada_b)
    update("norm1_g", norm1_g, part["n1"], m_norm1_g, v_norm1_g)
    update("norm2_g", norm2_g, part["n2"], m_norm2_g, v_norm2_g)
    update("spatial_w", spatial_w, part["sw"], m_spatial_w, v_spatial_w)
    update("spatial_b", spatial_b, part["sb"], m_spatial_b, v_spatial_b)
    update("v_norm_g", v_norm_g, part["gv"], m_v_norm_g, v_v_norm_g)
    update("a_log", a_log, part["al"], m_a_log, v_a_log)
    update("dt_bias", dt_bias, part["dt"], m_dt_bias, v_dt_bias)
    update("o_norm_g", o_norm_g, part["go"], m_o_norm_g, v_o_norm_g)
    update("final_g", final_g, part["fg"], m_final_g, v_final_g)
    cw_cols = conv_w.shape[2]
    dcw_all = part["cw"].reshape(8, L, _KC, 4 * cw_cols)
    update("conv_w", conv_w, lax.dynamic_slice_in_dim(dcw_all, chip * cw_cols, cw_cols, axis=3), m_conv_w, v_conv_w)

    def hull_of(p, k):
        a, b = starts[k], starts[k] + Hh
        out = []
        for lo, hi, plo in ((0, 6 * D, 0), (6 * D, 6 * D + 2 * H, 8 * D), (6 * D + 2 * H, 8 * D + 2 * H, 6 * D),
                            (8 * D + 2 * H, No, None)):
            s, e = max(a, lo), min(b, hi)
            if s < e:
                out.append(jnp.zeros((e - s, D), _MMT) if plo is None else p[plo + s - lo:plo + e - lo].astype(_MMT))
        return out

    pieces = []
    for k in range(4):
        for l in range(L):
            pieces += hull_of(gbig["w_in"][l], k)
        for i, nm in enumerate(("w_a", "w_b", "w_o", "w_fi", "w_fo")):
            per = pers[i + 1]
            pieces += [gbig[nm][l][k * per:(k + 1) * per].astype(_MMT) for l in range(L)]
        pieces.append(jnp.zeros((rpad, D), _MMT))
    send = jnp.concatenate(pieces, axis=0).reshape(4, Rp, D)
    got = _send_half_to_sibling(send, "reduce_cores")
    chipsum = _add_own_half(send, got, ic.astype(jnp.int32).reshape(1), "add_cores")
    parts = _scatter_to_chips(chipsum, "reduce_chips")
    mine = _sum_slots(parts, "add_chips")
    other = _swap_with_sibling(mine, "swap_cores")
    first = ic == 0
    gsum = jnp.concatenate([jnp.where(first, mine, other), jnp.where(first, other, mine)], axis=0)
    big_names = ["w_in", "w_branch_a", "w_branch_b", "w_out", "w_ffn_in", "w_ffn_out"]
    big_m = [m_w_in, m_w_branch_a, m_w_branch_b, m_w_out, m_w_ffn_in, m_w_ffn_out]
    big_v = [v_w_in, v_w_branch_a, v_w_branch_b, v_w_out, v_w_ffn_in, v_w_ffn_out]
    for i, (nm, w, m, v) in enumerate(zip(big_names, big, big_m, big_v)):
        g = gsum[roff[i]:roff[i + 1]].reshape(L, pers[i], D)
        if i == 0:
            g = jnp.transpose(lax.dynamic_slice_in_dim(g, my_off, Wc, axis=1), (0, 2, 1))
        elif i == 4:
            g = jnp.transpose(g, (0, 2, 1))
        update(nm, w, g[None], m, v)

    order = ["ada_w", "ada_b", "norm1_g", "w_in", "conv_w", "spatial_w", "spatial_b", "v_norm_g", "a_log", "dt_bias",
             "o_norm_g", "w_branch_a", "w_branch_b", "w_out", "norm2_g", "w_ffn_in", "w_ffn_out", "final_g"]
    return (loss, grad_x, *[outs[n][0] for n in order], *[outs[n][1] for n in order],
            *[outs[n][2] for n in order], *[outs[n][3] for n in order])
```

```python
import functools
import math

import jax
import jax.numpy as jnp
from jax import lax
from jax.experimental import pallas as pl
from jax.experimental.pallas import tpu as pltpu

_F32 = jnp.float32
_BF = jnp.bfloat16
_MMT = jnp.bfloat16
_EPS = 1e-6
_GD = 128
_AC = 128
_BC = 64
_KC = 4
_HB = 8
_NCH = 8
_LANE = 128
_VMEM_LIMIT = 56 * 1024 * 1024

_LR, _B1, _B2, _AEPS, _WD, _STEP = 0.001, 0.9, 0.999, 1e-08, 0.01, 10

_NN = (((1,), (0,)), ((), ()))
_NT = (((1,), (1,)), ((), ()))
_TN = (((0,), (0,)), ((), ()))

_MESH = pl.DeviceIdType.MESH


def _cp(*sem):
    return pltpu.CompilerParams(dimension_semantics=tuple(sem), vmem_limit_bytes=_VMEM_LIMIT)


def _dot(a, b, dn=_NN):
    return lax.dot_general(a.astype(_MMT), b.astype(_MMT), dn, preferred_element_type=_F32)


def _pick(n, target, unit=_LANE):
    if n <= target:
        return n
    best = None
    for t in range(unit, target + 1, unit):
        if n % t == 0:
            best = t
    assert best is not None, (n, target)
    return best


def _sigmoid(x):
    return 0.5 * jnp.tanh(0.5 * x) + 0.5


def _silu(x):
    return x * _sigmoid(x)


def _dsilu(x):
    s = _sigmoid(x)
    return s * (1.0 + x * (1.0 - s))


_GK = math.sqrt(2.0 / math.pi)


def _gelu(x):
    return 0.5 * x * (1.0 + jnp.tanh(_GK * (x + 0.044715 * x * x * x)))


def _dgelu(x):
    t = jnp.tanh(_GK * (x + 0.044715 * x * x * x))
    return 0.5 * (1.0 + t) + 0.5 * x * (1.0 - t * t) * _GK * (1.0 + 3.0 * 0.044715 * x * x)


def _softplus(x):
    return jnp.maximum(x, 0.0) + jnp.log(1.0 + jnp.exp(-jnp.abs(x)))


def _rspec(tb, w, cb=0):
    return pl.BlockSpec((tb, w), lambda i: (i, cb))


def _fspec(shape):
    nd = len(shape)
    return pl.BlockSpec(tuple(shape), lambda i: (0,) * nd)


def _lspec(tail, li):
    nd = len(tail)
    return pl.BlockSpec((None,) + tuple(tail), lambda i: (li,) + (0,) * nd)


def _slot_all8(x, y, c):
    return 4 * x + 2 * y + c


def _gather8(v, name):
    R, W = v.shape

    def body(v_ref, o_ref, ssem, rsem, lsem):
        x, y, c = lax.axis_index("x"), lax.axis_index("y"), lax.axis_index("c")
        sib = (x, y, 1 - c)
        chips = _other_chips(x, y)

        def slot(px, py, pc):
            return o_ref.at[_slot_all8(px, py, pc)]

        own = pltpu.make_async_copy(v_ref, slot(x, y, c), lsem)
        own.start()
        started = [_rcopy(v_ref, slot(x, y, c), ssem.at[0], rsem.at[0], sib)]
        started += [_rcopy(v_ref, slot(x, y, c), ssem.at[1 + j], rsem.at[1 + j], (px, py, c))
                    for j, (px, py) in enumerate(chips)]
        for cp in started:
            cp.start()
        for j, (px, py) in enumerate(chips):
            blk = slot(px, py, c)
            _rcopy(blk, blk, ssem.at[1 + j], rsem.at[1 + j], (px, py, c)).wait_recv()
            fw = _rcopy(blk, blk, ssem.at[4 + j], rsem.at[4 + j], sib)
            fw.start()
            started.append(fw)
        blk = slot(x, y, 1 - c)
        _rcopy(blk, blk, ssem.at[0], rsem.at[0], sib).wait_recv()
        for j, (px, py) in enumerate(chips):
            blk = slot(px, py, 1 - c)
            _rcopy(blk, blk, ssem.at[4 + j], rsem.at[4 + j], sib).wait_recv()
        for cp in started:
            cp.wait_send()
        own.wait()

    return pl.pallas_call(
        body, name=name, out_shape=jax.ShapeDtypeStruct((8, R, W), v.dtype), in_specs=[_HBM], out_specs=_HBM,
        scratch_shapes=[pltpu.SemaphoreType.DMA((7,)), pltpu.SemaphoreType.DMA((7,)), pltpu.SemaphoreType.DMA],
    )(v)


def _rcopy(src, dst, ssem, rsem, dev):
    return pltpu.make_async_remote_copy(src_ref=src, dst_ref=dst, send_sem=ssem, recv_sem=rsem,
                                        device_id=dev, device_id_type=_MESH)


def _other_chips(x, y):
    return [(1 - x, y), (x, 1 - y), (1 - x, 1 - y)]


_HBM = pl.BlockSpec(memory_space=pl.ANY)


def _gather_chips(shard, name):
    Rp, W = shard.shape
    Rh = Rp // 2
    rc = Rh // _NCH
    hq = _NCH // 2

    def body(s_ref, o_ref, ssem, rsem, lsem):
        x, y, c = lax.axis_index("x"), lax.axis_index("y"), lax.axis_index("c")
        chip = 2 * x + y
        xn, yn, dg = _other_chips(x, y)
        cx, cy, cd = 2 * xn[0] + xn[1], 2 * yn[0] + yn[1], 2 * dg[0] + dg[1]

        def rows(q):
            return pl.ds(c * Rh + q * rc, rc)

        locs = []
        for q in range(_NCH):
            lc = pltpu.make_async_copy(s_ref.at[rows(q)], o_ref.at[chip, rows(q)], lsem.at[q])
            lc.start()
            locs.append(lc)
        started = []
        for q in range(_NCH):
            for j, nb in ((0, xn), (1, yn)):
                cp = _rcopy(s_ref.at[rows(q)], o_ref.at[chip, rows(q)], ssem.at[j * _NCH + q], rsem.at[j * _NCH + q],
                            (nb[0], nb[1], c))
                cp.start()
                started.append(cp)
        for q in range(_NCH):
            bx = o_ref.at[cx, rows(q)]
            _rcopy(bx, bx, ssem.at[q], rsem.at[q], (xn[0], xn[1], c)).wait_recv()
            if q >= hq:
                rl = _rcopy(bx, bx, ssem.at[2 * _NCH + q], rsem.at[2 * _NCH + q], (yn[0], yn[1], c))
                rl.start()
                started.append(rl)
            by = o_ref.at[cy, rows(q)]
            _rcopy(by, by, ssem.at[_NCH + q], rsem.at[_NCH + q], (yn[0], yn[1], c)).wait_recv()
            if q < hq:
                rl = _rcopy(by, by, ssem.at[2 * _NCH + q], rsem.at[2 * _NCH + q], (xn[0], xn[1], c))
                rl.start()
                started.append(rl)
        for q in range(_NCH):
            bd = o_ref.at[cd, rows(q)]
            _rcopy(bd, bd, ssem.at[2 * _NCH + q], rsem.at[2 * _NCH + q], (dg[0], dg[1], c)).wait_recv()
        for cp in started:
            cp.wait_send()
        for lc in locs:
            lc.wait()

    return pl.pallas_call(
        body, name=name, out_shape=jax.ShapeDtypeStruct((4, Rp, W), shard.dtype), in_specs=[_HBM], out_specs=_HBM,
        scratch_shapes=[pltpu.SemaphoreType.DMA((3 * _NCH,))] * 2 + [pltpu.SemaphoreType.DMA((_NCH,))],
    )(shard)


def _fill_from_sibling(buf, name):
    P, Rp, W = buf.shape
    Rh = Rp // 2
    rc = Rh // _NCH

    def body(s_ref, o_ref, ssem, rsem):
        x, y, c = lax.axis_index("x"), lax.axis_index("y"), lax.axis_index("c")
        cps = []
        for k in range(P):
            for q in range(_NCH):
                r = pl.ds(c * Rh + q * rc, rc)
                cp = _rcopy(s_ref.at[k, r], o_ref.at[k, r], ssem.at[k * _NCH + q], rsem.at[k * _NCH + q],
                            (x, y, 1 - c))
                cp.start()
                cps.append(cp)
        for k in range(P):
            for q in range(_NCH):
                blk = o_ref.at[k, pl.ds((1 - c) * Rh + q * rc, rc)]
                _rcopy(blk, blk, ssem.at[k * _NCH + q], rsem.at[k * _NCH + q], (x, y, 1 - c)).wait_recv()
        for cp in cps:
            cp.wait_send()

    return pl.pallas_call(
        body, name=name, out_shape=jax.ShapeDtypeStruct(buf.shape, buf.dtype), in_specs=[_HBM], out_specs=_HBM,
        scratch_shapes=[pltpu.SemaphoreType.DMA((P * _NCH,))] * 2, input_output_aliases={0: 0},
    )(buf)


def _pack_halves(arrays, pieces, P, Rp, name):
    W = arrays[0].shape[1]
    Rh = Rp // 2
    na = len(arrays)
    subs = []
    for ai, s0, n, k, f0 in pieces:
        for h in (0, 1):
            lo, hi = max(f0, h * Rh), min(f0 + n, (h + 1) * Rh)
            if lo < hi:
                subs.append((ai, s0 + lo - f0, hi - lo, k, lo - h * Rh, h))

    def body(*refs):
        srcs, (mine_ref, got_ref), (lsem, ssem, rsem) = refs[:na], refs[na:na + 2], refs[na + 2:]
        x, y, c = lax.axis_index("x"), lax.axis_index("y"), lax.axis_index("c")
        sib = (x, y, 1 - c)

        def local(i):
            ai, s0, n, k, f0, _ = subs[i]
            return pltpu.make_async_copy(srcs[ai].at[pl.ds(s0, n)], mine_ref.at[k, pl.ds(f0, n)], lsem.at[i])

        def remote(i):
            ai, s0, n, k, f0, _ = subs[i]
            return _rcopy(srcs[ai].at[pl.ds(s0, n)], got_ref.at[k, pl.ds(f0, n)], ssem.at[i], rsem.at[i], sib)

        for i, sub in enumerate(subs):
            @pl.when(c == sub[5])
            def _():
                local(i).start()

            @pl.when(c != sub[5])
            def _():
                remote(i).start()

        for i, sub in enumerate(subs):
            @pl.when(c == sub[5])
            def _():
                local(i).wait()
                remote(i).wait_recv()

            @pl.when(c != sub[5])
            def _():
                remote(i).wait_send()

    ns = len(subs)
    half = jax.ShapeDtypeStruct((P, Rh, W), arrays[0].dtype)
    return pl.pallas_call(
        body, name=name, out_shape=[half, half], in_specs=[_HBM] * na, out_specs=[_HBM, _HBM],
        scratch_shapes=[pltpu.SemaphoreType.DMA((ns,))] * 3,
    )(*arrays)


def _add_pair(a, b, name):
    shp = a.shape
    a2, b2 = a.reshape(-1, shp[-1]), b.reshape(-1, shp[-1])
    R, W = a2.shape
    tb = _row_tile(R, W, 1 << 21, 16)

    def body(a_ref, b_ref, o_ref):
        o_ref[...] = (a_ref[...].astype(_F32) + b_ref[...].astype(_F32)).astype(o_ref.dtype)

    return pl.pallas_call(
        body, name=name, grid=(R // tb,), in_specs=[_rspec(tb, W), _rspec(tb, W)], out_specs=_rspec(tb, W),
        out_shape=jax.ShapeDtypeStruct((R, W), a.dtype), compiler_params=_cp("parallel"),
    )(a2, b2).reshape(shp)


def _scatter_to_chips(cs, name):
    P, Rh, W = cs.shape
    rc = Rh // _NCH

    def body(s_ref, o_ref, ssem, rsem, lsem):
        x, y, c = lax.axis_index("x"), lax.axis_index("y"), lax.axis_index("c")
        chip = 2 * x + y
        peers = _other_chips(x, y)
        locs = []
        for q in range(_NCH):
            r = pl.ds(q * rc, rc)
            lc = pltpu.make_async_copy(s_ref.at[chip, r], o_ref.at[chip, r], lsem.at[q])
            lc.start()
            locs.append(lc)
        cps = []
        for j, (px, py) in enumerate(peers):
            for q in range(_NCH):
                r = pl.ds(q * rc, rc)
                cp = _rcopy(s_ref.at[2 * px + py, r], o_ref.at[chip, r], ssem.at[j * _NCH + q], rsem.at[j * _NCH + q],
                            (px, py, c))
                cp.start()
                cps.append(cp)
        for j, (px, py) in enumerate(peers):
            for q in range(_NCH):
                blk = o_ref.at[2 * px + py, pl.ds(q * rc, rc)]
                _rcopy(blk, blk, ssem.at[j * _NCH + q], rsem.at[j * _NCH + q], (px, py, c)).wait_recv()
        for cp in cps:
            cp.wait_send()
        for lc in locs:
            lc.wait()

    return pl.pallas_call(
        body, name=name, out_shape=jax.ShapeDtypeStruct((P, Rh, W), cs.dtype), in_specs=[_HBM], out_specs=_HBM,
        scratch_shapes=[pltpu.SemaphoreType.DMA((3 * _NCH,))] * 2 + [pltpu.SemaphoreType.DMA((_NCH,))],
    )(cs)


def _swap_with_sibling(v, name):
    R, W = v.shape
    rc = R // _NCH

    def body(s_ref, o_ref, ssem, rsem):
        x, y, c = lax.axis_index("x"), lax.axis_index("y"), lax.axis_index("c")
        cps = []
        for q in range(_NCH):
            r = pl.ds(q * rc, rc)
            cp = _rcopy(s_ref.at[r], o_ref.at[r], ssem.at[q], rsem.at[q], (x, y, 1 - c))
            cp.start()
            cps.append(cp)
        for cp in cps:
            cp.wait()

    return pl.pallas_call(
        body, name=name, out_shape=jax.ShapeDtypeStruct((R, W), v.dtype), in_specs=[_HBM], out_specs=_HBM,
        scratch_shapes=[pltpu.SemaphoreType.DMA((_NCH,))] * 2,
    )(v)


def _mm(a, b, name, li=None, trans_b=False, out_dtype=_F32, tm=1024, tn=1024, tk=2048):
    M, K = a.shape
    bs = b.shape[-2:]
    N = bs[0] if trans_b else bs[1]
    tm, tn, tk = _pick(M, tm, 8), _pick(N, tn), _pick(K, tk)
    nk = K // tk
    lead = () if li is None else (None,)

    def bmap(i, j, k):
        idx = (j, k) if trans_b else (k, j)
        return idx if li is None else (li,) + idx

    def body(a_ref, b_ref, o_ref, acc):
        k = pl.program_id(2)
        part = lax.dot_general(a_ref[...], b_ref[...], _NT if trans_b else _NN, preferred_element_type=_F32)
        if nk == 1:
            o_ref[...] = part.astype(o_ref.dtype)
        else:
            @pl.when(k == 0)
            def _():
                acc[...] = part

            @pl.when(k > 0)
            def _():
                acc[...] += part

            @pl.when(k == nk - 1)
            def _():
                o_ref[...] = acc[...].astype(o_ref.dtype)

    return pl.pallas_call(
        body, name=name, grid=(M // tm, N // tn, nk),
        in_specs=[pl.BlockSpec((tm, tk), lambda i, j, k: (i, k)),
                  pl.BlockSpec(lead + ((tn, tk) if trans_b else (tk, tn)), bmap)],
        out_specs=pl.BlockSpec((tm, tn), lambda i, j, k: (i, j)),
        out_shape=jax.ShapeDtypeStruct((M, N), out_dtype),
        scratch_shapes=[pltpu.VMEM((tm, tn) if nk > 1 else (8, _LANE), _F32)],
        compiler_params=_cp("parallel", "parallel", "arbitrary"),
    )(a, b)


def _mm_tn(a, b, name, tm=512, tn=1024):
    T, M = a.shape
    N = b.shape[1]
    tm, tn = _pick(M, tm), _pick(N, tn)

    def body(a_ref, b_ref, o_ref):
        o_ref[...] = lax.dot_general(a_ref[...], b_ref[...], _TN, preferred_element_type=_F32).astype(o_ref.dtype)

    return pl.pallas_call(
        body, name=name, grid=(M // tm, N // tn),
        in_specs=[pl.BlockSpec((T, tm), lambda i, j: (0, i)), pl.BlockSpec((T, tn), lambda i, j: (0, j))],
        out_specs=pl.BlockSpec((tm, tn), lambda i, j: (i, j)),
        out_shape=jax.ShapeDtypeStruct((M, N), _MMT),
        compiler_params=_cp("parallel", "parallel"),
    )(a, b)


def _mm_res(a, b, x, gt, name, tm=1024, tn=1024, tk=2048):
    M, K = a.shape
    N = b.shape[1]
    tm, tn, tk = _pick(M, tm, 8), _pick(N, tn), _pick(K, tk)
    nk = K // tk

    def body(a_ref, b_ref, x_ref, gt_ref, p_ref, o_ref):
        k = pl.program_id(2)
        part = lax.dot_general(a_ref[...], b_ref[...], _NN, preferred_element_type=_F32)

        @pl.when(k == 0)
        def _():
            p_ref[...] = part

        @pl.when(k > 0)
        def _():
            p_ref[...] += part

        @pl.when(k == nk - 1)
        def _():
            o_ref[...] = x_ref[...] + gt_ref[...] * p_ref[...]

    tile = pl.BlockSpec((tm, tn), lambda i, j, k: (i, j))
    return pl.pallas_call(
        body, name=name, grid=(M // tm, N // tn, nk),
        in_specs=[pl.BlockSpec((tm, tk), lambda i, j, k: (i, k)), pl.BlockSpec((tk, tn), lambda i, j, k: (k, j)),
                  tile, pl.BlockSpec((1, tn), lambda i, j, k: (0, j))],
        out_specs=[tile, tile], out_shape=[jax.ShapeDtypeStruct((M, N), _F32)] * 2,
        compiler_params=_cp("parallel", "parallel", "arbitrary"),
    )(a, b, x, gt)


def _ffin_swiglu(a, wt, name, tm=1024, tn=1408):
    M, K = a.shape
    F = wt.shape[0] // 2
    tm, tn = _pick(M, tm, 8), _pick(F, tn)
    nj = F // tn

    def body(a_ref, bg_ref, bu_ref, g_ref, u_ref, act_ref):
        av = a_ref[...]
        g = lax.dot_general(av, bg_ref[...], _NT, preferred_element_type=_F32)
        u = lax.dot_general(av, bu_ref[...], _NT, preferred_element_type=_F32)
        g_ref[...] = g.astype(g_ref.dtype)
        u_ref[...] = u.astype(u_ref.dtype)
        act_ref[...] = (_silu(g) * u).astype(act_ref.dtype)

    tile = pl.BlockSpec((tm, tn), lambda i, j: (i, j))
    return pl.pallas_call(
        body, name=name, grid=(M // tm, nj),
        in_specs=[pl.BlockSpec((tm, K), lambda i, j: (i, 0)), pl.BlockSpec((tn, K), lambda i, j: (j, 0)),
                  pl.BlockSpec((tn, K), lambda i, j: (j + nj, 0))],
        out_specs=[tile] * 3, out_shape=[jax.ShapeDtypeStruct((M, F), _MMT)] * 3,
        compiler_params=_cp("parallel", "parallel"),
    )(a, wt, wt)


def _branch_merge(ya, yb, w_a, w_b, proj, name, tm=1024, tn=512):
    M, K = ya.shape
    N = w_a.shape[1]
    tm, tn = _pick(M, tm, 8), _pick(N, tn)
    nj = N // tn

    def body(ya_ref, yb_ref, wa_ref, wb_ref, ga_ref, gb_ref, pa_ref, pb_ref, m_ref):
        pa = lax.dot_general(ya_ref[...], wa_ref[...], _NN, preferred_element_type=_F32)
        pb = lax.dot_general(yb_ref[...], wb_ref[...], _NN, preferred_element_type=_F32)
        pa_ref[...] = pa.astype(pa_ref.dtype)
        pb_ref[...] = pb.astype(pb_ref.dtype)
        m_ref[...] = (_sigmoid(ga_ref[...]) * pa + _sigmoid(gb_ref[...]) * pb).astype(m_ref.dtype)

    row = pl.BlockSpec((tm, K), lambda i, j: (i, 0))
    col = pl.BlockSpec((K, tn), lambda i, j: (0, j))
    tile = pl.BlockSpec((tm, tn), lambda i, j: (i, j))
    return pl.pallas_call(
        body, name=name, grid=(M // tm, nj),
        in_specs=[row, row, col, col, pl.BlockSpec((tm, tn), lambda i, j: (i, 6 * nj + j)),
                  pl.BlockSpec((tm, tn), lambda i, j: (i, 7 * nj + j))],
        out_specs=[tile] * 3, out_shape=[jax.ShapeDtypeStruct((M, N), _MMT)] * 3,
        compiler_params=_cp("parallel", "parallel"),
    )(ya, yb, w_a, w_b, proj, proj)


def _ada_fwd(c_all, ada_w, name):
    L, D, Ws = ada_w.shape
    B = c_all.shape[0]

    def body(c_ref, w_ref, o_ref):
        o_ref[...] = _dot(_silu(c_ref[...]), w_ref[...])

    return pl.pallas_call(
        body, name=name, grid=(L,),
        in_specs=[_fspec((B, D)), pl.BlockSpec((None, D, Ws), lambda l: (l, 0, 0))],
        out_specs=pl.BlockSpec((None, B, Ws), lambda l: (l, 0, 0)),
        out_shape=jax.ShapeDtypeStruct((L, B, Ws), _F32), compiler_params=_cp("parallel"),
    )(c_all, ada_w)


def _ada_bwd(c_all_t, dmod, name):
    D, B = c_all_t.shape
    L, _, Ws = dmod.shape

    def body(c_ref, d_ref, o_ref):
        ct = _silu(c_ref[...])
        d = d_ref[...]
        acc = ct[:, 0:1] * d[0:1, :]
        for b in range(1, B):
            acc = acc + ct[:, b:b + 1] * d[b:b + 1, :]
        o_ref[...] = acc

    return pl.pallas_call(
        body, name=name, grid=(L,),
        in_specs=[_fspec((D, B)), pl.BlockSpec((None, B, Ws), lambda l: (l, 0, 0))],
        out_specs=pl.BlockSpec((None, D, Ws), lambda l: (l, 0, 0)),
        out_shape=jax.ShapeDtypeStruct((L, D, Ws), _F32), compiler_params=_cp("parallel"),
    )(c_all_t, dmod)


def _norm_mod(x, g, sc, sh, name, tb=512):
    T, D = x.shape
    tb = _pick(T, tb, 8)

    def body(x_ref, g_ref, sc_ref, sh_ref, h_ref):
        xv = x_ref[...]
        r = lax.rsqrt(jnp.mean(xv * xv, axis=1, keepdims=True) + _EPS)
        h_ref[...] = (xv * r * (g_ref[...] * (1.0 + sc_ref[...])) + sh_ref[...]).astype(h_ref.dtype)

    return pl.pallas_call(
        body, name=name, grid=(T // tb,),
        in_specs=[_rspec(tb, D), _fspec((1, D)), _fspec((1, D)), _fspec((1, D))],
        out_specs=_rspec(tb, D), out_shape=jax.ShapeDtypeStruct((T, D), _MMT), compiler_params=_cp("parallel"),
    )(x, g, sc, sh)


def _norm_mod_bwd(x, dh, dres, g, sc, name, tb=256):
    T, D = x.shape
    tb = _pick(T, tb, 8)

    def body(x_ref, dh_ref, dr_ref, g_ref, sc_ref, dx_ref, dgm_ref, dsh_ref):
        i = pl.program_id(0)
        xv, dh_ = x_ref[...], dh_ref[...]
        r = lax.rsqrt(jnp.mean(xv * xv, axis=1, keepdims=True) + _EPS)
        xn = xv * r
        dxn = dh_ * (g_ref[...] * (1.0 + sc_ref[...]))
        dx_ref[...] = dr_ref[...] + r * (dxn - xn * jnp.mean(dxn * xn, axis=1, keepdims=True))

        @pl.when(i == 0)
        def _():
            dgm_ref[...] = jnp.zeros_like(dgm_ref)
            dsh_ref[...] = jnp.zeros_like(dsh_ref)

        dgm_ref[...] += jnp.sum(dh_ * xn, axis=0, keepdims=True)
        dsh_ref[...] += jnp.sum(dh_, axis=0, keepdims=True)

    return pl.pallas_call(
        body, name=name, grid=(T // tb,),
        in_specs=[_rspec(tb, D), _rspec(tb, D), _rspec(tb, D), _fspec((1, D)), _fspec((1, D))],
        out_specs=[_rspec(tb, D), _fspec((1, D)), _fspec((1, D))],
        out_shape=[jax.ShapeDtypeStruct((T, D), _F32), jax.ShapeDtypeStruct((1, D), _F32),
                   jax.ShapeDtypeStruct((1, D), _F32)],
        compiler_params=_cp("arbitrary"),
    )(x, dh, dres, g, sc)


def _resid_bwd(dx, p, gt, name, tb=512):
    T, D = dx.shape
    tb = _pick(T, tb, 8)

    def body(dx_ref, p_ref, gt_ref, dp_ref, dgt_ref):
        i = pl.program_id(0)
        d = dx_ref[...]
        dp_ref[...] = (d * gt_ref[...]).astype(dp_ref.dtype)

        @pl.when(i == 0)
        def _():
            dgt_ref[...] = jnp.zeros_like(dgt_ref)

        dgt_ref[...] += jnp.sum(d * p_ref[...], axis=0, keepdims=True)

    return pl.pallas_call(
        body, name=name, grid=(T // tb,), in_specs=[_rspec(tb, D), _rspec(tb, D), _fspec((1, D))],
        out_specs=[_rspec(tb, D), _fspec((1, D))],
        out_shape=[jax.ShapeDtypeStruct((T, D), _MMT), jax.ShapeDtypeStruct((1, D), _F32)],
        compiler_params=_cp("arbitrary"),
    )(dx, p, gt)


def _gmlp_chunk(u_raw, v_raw, sw_ref, sbt, gv, G):
    u, v = _gelu(u_raw), _gelu(v_raw)
    ii = lax.broadcasted_iota(jnp.int32, (_AC, _AC), 0)
    jj = lax.broadcasted_iota(jnp.int32, (_AC, _AC), 1)
    out = []
    for gi in range(G):
        sl = slice(gi * _GD, (gi + 1) * _GD)
        vg = v[:, sl]
        r = lax.rsqrt(jnp.mean(vg * vg, axis=1, keepdims=True) + _EPS)
        vhat = vg * r
        W = jnp.where(jj <= ii, sw_ref[gi], 0.0)
        s = _dot(W, vhat * gv[:, sl]) + sbt[:, gi:gi + 1]
        out.append((u[:, sl], s, vhat, r, W))
    return out


def _gmlp_fwd(proj, sw, sbt, gv, li, D, name):
    T = proj.shape[0]
    G = D // _GD

    def body(u_ref, v_ref, sw_ref, sbt_ref, gv_ref, y_ref):
        parts = _gmlp_chunk(u_ref[...], v_ref[...], sw_ref, sbt_ref[...], gv_ref[...], G)
        for gi, (u, s, _, _, _) in enumerate(parts):
            y_ref[:, gi * _GD:(gi + 1) * _GD] = (u * s).astype(y_ref.dtype)

    return pl.pallas_call(
        body, name=name, grid=(T // _AC,),
        in_specs=[_rspec(_AC, D, 0), _rspec(_AC, D, 1), _lspec((G, _AC, _AC), li), _lspec((_AC, G), li),
                  _lspec((1, D), li)],
        out_specs=_rspec(_AC, D), out_shape=jax.ShapeDtypeStruct((T, D), _MMT), compiler_params=_cp("parallel"),
    )(proj, proj, sw, sbt, gv)


def _gmlp_bwd(proj, dy, sw, sbt, gv, li, D, into, name):
    T = proj.shape[0]
    G = D // _GD

    def body(u_ref, v_ref, dy_ref, sw_ref, sbt_ref, gv_ref, _, duv_ref, dsw_ref, dsa_ref, dgv_ref):
        i = pl.program_id(0)

        @pl.when(i == 0)
        def _():
            dsw_ref[...] = jnp.zeros_like(dsw_ref)
            dsa_ref[...] = jnp.zeros_like(dsa_ref)
            dgv_ref[...] = jnp.zeros_like(dgv_ref)

        u_raw, v_raw, dy_, gv_ = u_ref[...], v_ref[...], dy_ref[...].astype(_F32), gv_ref[...]
        parts = _gmlp_chunk(u_raw, v_raw, sw_ref, sbt_ref[...], gv_, G)
        ii = lax.broadcasted_iota(jnp.int32, (_AC, _AC), 0)
        jj = lax.broadcasted_iota(jnp.int32, (_AC, _AC), 1)
        dgu, dgv = _dgelu(u_raw), _dgelu(v_raw)
        for gi, (u, s, vhat, r, W) in enumerate(parts):
            sl = slice(gi * _GD, (gi + 1) * _GD)
            dyg = dy_[:, sl]
            ds = dyg * u
            vn = vhat * gv_[:, sl]
            dsw_ref[gi] += jnp.where(jj <= ii, _dot(ds, vn, _NT), 0.0)
            dsa_ref[:, sl] += ds
            dvn = _dot(W, ds, _TN)
            dgv_ref[:, sl] += jnp.sum(dvn * vhat, axis=0, keepdims=True)
            dvh = dvn * gv_[:, sl]
            dvg = r * (dvh - vhat * jnp.mean(dvh * vhat, axis=1, keepdims=True))
            duv_ref[:, sl] = (dyg * s * dgu[:, sl]).astype(duv_ref.dtype)
            duv_ref[:, D + gi * _GD:D + (gi + 1) * _GD] = (dvg * dgv[:, sl]).astype(duv_ref.dtype)

    return pl.pallas_call(
        body, name=name, grid=(T // _AC,),
        in_specs=[_rspec(_AC, D, 0), _rspec(_AC, D, 1), _rspec(_AC, D), _lspec((G, _AC, _AC), li),
                  _lspec((_AC, G), li), _lspec((1, D), li), _HBM],
        out_specs=[_rspec(_AC, 2 * D), _fspec((G, _AC, _AC)), _fspec((_AC, D)), _fspec((1, D))],
        out_shape=[jax.ShapeDtypeStruct(into.shape, into.dtype), jax.ShapeDtypeStruct((G, _AC, _AC), _F32),
                   jax.ShapeDtypeStruct((_AC, D), _F32), jax.ShapeDtypeStruct((1, D), _F32)],
        input_output_aliases={6: 0}, compiler_params=_cp("arbitrary"),
    )(proj, proj, dy, sw, sbt, gv, into)


def _conv_taps(halo, cur, first):
    tb = cur.shape[0]
    full = jnp.concatenate([jnp.where(first, 0.0, halo), cur], axis=0)
    return [full[8:] if j == _KC - 1 else pltpu.roll(full, _KC - 1 - j, 0)[8:] for j in range(_KC)]


def _prev_spec(tb, w, cb):
    return pl.BlockSpec((8, w), lambda i: (jnp.maximum(i * (tb // 8) - 1, 0), cb))


def _l2_heads(x, H):
    outs, rs = [], []
    for h in range(H):
        xh = x[:, h * _GD:(h + 1) * _GD]
        r = lax.rsqrt(jnp.sum(xh * xh, axis=1, keepdims=True) + _EPS)
        outs.append(xh * r)
        rs.append(r)
    return outs, rs


def _gate_rows(ba, alog_row, dtb_row, H):
    lane = lax.broadcasted_iota(jnp.int32, ba.shape, 1)
    beta = _sigmoid(ba)
    g = -jnp.exp(alog_row) * _softplus(ba + dtb_row)
    return lane, beta, g


def _conv_fwd(proj, cw, alog_row, dtb_row, li, D, name, tb=256):
    T = proj.shape[0]
    H = D // _GD
    tb = _pick(T, tb, 8)
    bac = (8 * D) // _LANE

    def body(q_ref, k_ref, v_ref, qh_ref, kh_ref, vh_ref, ba_ref, cw_ref, al_ref, dtb_ref,
             qo_ref, ko_ref, vo_ref, bg_ref):
        first = pl.program_id(0) == 0
        cw_ = cw_ref[...]
        for idx, (cur, halo, out) in enumerate(((q_ref, qh_ref, qo_ref), (k_ref, kh_ref, ko_ref),
                                                 (v_ref, vh_ref, vo_ref))):
            taps = _conv_taps(halo[...], cur[...], first)
            w = cw_[:, idx * D:(idx + 1) * D]
            cv = taps[0] * w[0:1, :]
            for j in range(1, _KC):
                cv = cv + taps[j] * w[j:j + 1, :]
            act = _silu(cv)
            if idx < 2:
                outs, _ = _l2_heads(act, H)
                for h in range(H):
                    out[:, h * _GD:(h + 1) * _GD] = outs[h]
            else:
                out[...] = act
        lane, beta, g = _gate_rows(ba_ref[...], al_ref[...], dtb_ref[...], H)
        bg_ref[...] = jnp.where(lane < H, beta, jnp.where(lane < 2 * H, g, 0.0))

    return pl.pallas_call(
        body, name=name, grid=(T // tb,),
        in_specs=[_rspec(tb, D, 2), _rspec(tb, D, 3), _rspec(tb, D, 4),
                  _prev_spec(tb, D, 2), _prev_spec(tb, D, 3), _prev_spec(tb, D, 4),
                  _rspec(tb, _LANE, bac), _lspec((_KC, 3 * D), li), _lspec((1, _LANE), li), _lspec((1, _LANE), li)],
        out_specs=[_rspec(tb, D), _rspec(tb, D), _rspec(tb, D), _rspec(tb, _LANE)],
        out_shape=[jax.ShapeDtypeStruct((T, D), _F32)] * 3 + [jax.ShapeDtypeStruct((T, _LANE), _F32)],
        compiler_params=_cp("parallel"),
    )(proj, proj, proj, proj, proj, proj, proj, cw, alog_row, dtb_row)


def _conv_bwd1(proj, dqn, dkn, dvs, dbg, cw, alog_row, dtb_row, li, D, into, name, tb=256):
    T = proj.shape[0]
    H = D // _GD
    tb = _pick(T, tb, 8)
    bac = (8 * D) // _LANE

    def body(q_ref, k_ref, v_ref, qh_ref, kh_ref, vh_ref, ba_ref, dq_ref, dk_ref, dv_ref, dbg_ref,
             cw_ref, al_ref, dtb_ref, _, dc_ref, dba_ref, dcw_ref, dal_ref, ddt_ref):
        i = pl.program_id(0)
        first = i == 0

        @pl.when(first)
        def _():
            dcw_ref[...] = jnp.zeros_like(dcw_ref)
            dal_ref[...] = jnp.zeros_like(dal_ref)
            ddt_ref[...] = jnp.zeros_like(ddt_ref)

        cw_ = cw_ref[...]
        for idx, (cur, halo, dref) in enumerate(((q_ref, qh_ref, dq_ref), (k_ref, kh_ref, dk_ref),
                                                  (v_ref, vh_ref, dv_ref))):
            taps = _conv_taps(halo[...], cur[...], first)
            w = cw_[:, idx * D:(idx + 1) * D]
            cv = taps[0] * w[0:1, :]
            for j in range(1, _KC):
                cv = cv + taps[j] * w[j:j + 1, :]
            dact = dref[...]
            if idx < 2:
                outs, rs = _l2_heads(_silu(cv), H)
                pieces = []
                for h in range(H):
                    dy = dact[:, h * _GD:(h + 1) * _GD]
                    pieces.append(rs[h] * (dy - outs[h] * jnp.sum(dy * outs[h], axis=1, keepdims=True)))
                dact = jnp.concatenate(pieces, axis=1)
            dcv = dact * _dsilu(cv)
            dc_ref[:, idx * D:(idx + 1) * D] = dcv
            for j in range(_KC):
                colsum = _dot(jnp.ones((8, tb), _F32), dcv * taps[j])
                dcw_ref[j:j + 1, idx * D:(idx + 1) * D] += colsum[0:1, :]

        ba = ba_ref[...]
        lane, beta, g = _gate_rows(ba, al_ref[...], dtb_ref[...], H)
        dbg_ = dbg_ref[...]
        is_b, is_a = lane < H, jnp.logical_and(lane >= H, lane < 2 * H)
        da = dbg_ * (-jnp.exp(al_ref[...])) * _sigmoid(ba + dtb_ref[...])
        dba_ref[...] = jnp.where(is_b, dbg_ * beta * (1.0 - beta), jnp.where(is_a, da, 0.0)).astype(dba_ref.dtype)
        dal_ref[...] += jnp.sum(jnp.where(is_a, dbg_ * g, 0.0), axis=0, keepdims=True)
        ddt_ref[...] += jnp.sum(jnp.where(is_a, da, 0.0), axis=0, keepdims=True)

    return pl.pallas_call(
        body, name=name, grid=(T // tb,),
        in_specs=[_rspec(tb, D, 2), _rspec(tb, D, 3), _rspec(tb, D, 4),
                  _prev_spec(tb, D, 2), _prev_spec(tb, D, 3), _prev_spec(tb, D, 4),
                  _rspec(tb, _LANE, bac), _rspec(tb, D), _rspec(tb, D), _rspec(tb, D), _rspec(tb, _LANE),
                  _lspec((_KC, 3 * D), li), _lspec((1, _LANE), li), _lspec((1, _LANE), li), _HBM],
        out_specs=[_rspec(tb, 3 * D), _rspec(tb, _LANE, bac), _fspec((_KC, 3 * D)), _fspec((1, _LANE)),
                   _fspec((1, _LANE))],
        out_shape=[jax.ShapeDtypeStruct((T, 3 * D), _F32), jax.ShapeDtypeStruct(into.shape, into.dtype),
                   jax.ShapeDtypeStruct((_KC, 3 * D), _F32), jax.ShapeDtypeStruct((1, _LANE), _F32),
                   jax.ShapeDtypeStruct((1, _LANE), _F32)],
        input_output_aliases={14: 1}, compiler_params=_cp("arbitrary"),
    )(proj, proj, proj, proj, proj, proj, proj, dqn, dkn, dvs, dbg, cw, alog_row, dtb_row, into)


def _conv_bwd2(dc, cw, li, into, name, tb=256):
    T, W3 = dc.shape
    W = W3 // 3
    tb = _pick(T, tb, 8)
    nb8 = T // 8
    nrow = T // tb

    def body(dc_ref, nx_ref, cw_ref, _, o_ref):
        last = pl.program_id(0) == nrow - 1
        full = jnp.concatenate([dc_ref[...], jnp.where(last, 0.0, nx_ref[...])], axis=0)
        w = cw_ref[...]
        acc = full[:tb] * w[_KC - 1:_KC, :]
        for j in range(_KC - 1):
            sh = _KC - 1 - j
            acc = acc + pltpu.roll(full, tb + 8 - sh, 0)[:tb] * w[j:j + 1, :]
        o_ref[...] = acc.astype(o_ref.dtype)

    return pl.pallas_call(
        body, name=name, grid=(nrow, 3),
        in_specs=[pl.BlockSpec((tb, W), lambda i, j: (i, j)),
                  pl.BlockSpec((8, W), lambda i, j: (jnp.minimum((i + 1) * (tb // 8), nb8 - 1), j)),
                  pl.BlockSpec((None, _KC, W), lambda i, j: (li, 0, j)), _HBM],
        out_specs=pl.BlockSpec((tb, W), lambda i, j: (i, 2 + j)),
        out_shape=jax.ShapeDtypeStruct(into.shape, into.dtype), input_output_aliases={3: 0},
        compiler_params=_cp("parallel", "parallel"),
    )(dc, dc, cw, into)


def _split(a):
    hi = a.astype(_BF)
    return hi, (a - hi.astype(_F32)).astype(_BF)


def _dot3(a, b):
    (ah, al), (bh, bl) = a, b
    f = functools.partial(lax.dot_general, dimension_numbers=_NN, preferred_element_type=_F32)
    return f(ah, bh) + f(ah, bl) + f(al, bh)


def _inv_unit_lower(mats):
    C = mats[0].shape[0]
    ii = lax.broadcasted_iota(jnp.int32, (C, C), 0)
    jj = lax.broadcasted_iota(jnp.int32, (C, C), 1)
    xs = [jnp.where(ii == jj, 1.0, 0.0) - a for a in mats]
    ps = list(mats)
    n = 1
    while 2 * n < C:
        sp = [_split(p) for p in ps]
        ps = [_dot3(s, s) for s in sp]
        sp = [_split(p) for p in ps]
        xs = [x + _dot3(_split(x), s) for x, s in zip(xs, sp)]
        n *= 2
    return xs


def _gdn_chunk(q, k, v, g_row, b_row):
    C = q.shape[0]
    ii = lax.broadcasted_iota(jnp.int32, (C, C), 0)
    jj = lax.broadcasted_iota(jnp.int32, (C, C), 1)
    low, strict, eye = jj <= ii, jj < ii, ii == jj
    g_col = jnp.sum(jnp.where(eye, g_row, 0.0), axis=1, keepdims=True)
    b_col = jnp.sum(jnp.where(eye, b_row, 0.0), axis=1, keepdims=True)
    gam_col = jnp.sum(jnp.where(low, g_row, 0.0), axis=1, keepdims=True)
    gam_row = jnp.sum(jnp.where(jj >= ii, g_col, 0.0), axis=0, keepdims=True)
    gam_last = jnp.sum(g_row, axis=1, keepdims=True)
    decay = jnp.where(low, jnp.exp(jnp.where(low, gam_col - gam_row, 0.0)), 0.0)
    eg = jnp.exp(gam_col)
    ekd = jnp.exp(gam_last - gam_col)
    qs = q * (_GD ** -0.5)
    kb = k * b_col
    kk = _dot(kb, k, _NT)
    qkraw = _dot(qs, k, _NT)
    return dict(low=low, strict=strict, eye=eye, ii=ii, jj=jj, b_col=b_col, decay=decay, eg=eg, ekd=ekd,
                gl=jnp.exp(gam_last), qs=qs, kb=kb, kk=kk, qkraw=qkraw,
                A=jnp.where(strict, kk * decay, 0.0), vb=v * b_col, kbg=kb * eg,
                qk=qkraw * decay, q_dec=qs * eg, k_dec=k * ekd)


def _gdn_fwd(qn, kn, vs, g_r, b_r, name):
    T, D = qn.shape
    H, N, C = D // _GD, T // _BC, _BC
    hb = min(_HB, H)

    def body(q_ref, k_ref, v_ref, g_ref, b_ref, o_ref, s_ref, t_ref, S):
        @pl.when(pl.program_id(1) == 0)
        def _():
            S[...] = jnp.zeros_like(S)

        hs = range(hb)
        sls = [slice(hh * _GD, (hh + 1) * _GD) for hh in hs]
        cms = [_gdn_chunk(q_ref[:, sl], k_ref[:, sl], v_ref[:, sl], g_ref[hh], b_ref[hh]) for hh, sl in zip(hs, sls)]
        tms = _inv_unit_lower([cm["A"] for cm in cms])
        us = [_dot(tm, cm["vb"]) for tm, cm in zip(tms, cms)]
        ws = [_dot(tm, cm["kbg"]) for tm, cm in zip(tms, cms)]
        s0s = [S[hh] for hh in hs]
        for hh in hs:
            s_ref[hh] = s0s[hh]
            t_ref[hh] = tms[hh]
        v_news = [u - _dot(w, s0) for u, w, s0 in zip(us, ws, s0s)]
        qss = [_dot(cm["q_dec"], s0) for cm, s0 in zip(cms, s0s)]
        for hh in hs:
            o_ref[:, sls[hh]] = qss[hh] + _dot(cms[hh]["qk"], v_news[hh])
        for hh in hs:
            S[hh] = s0s[hh] * cms[hh]["gl"] + _dot(cms[hh]["k_dec"], v_news[hh], _TN)

    qspec = pl.BlockSpec((C, hb * _GD), lambda h, n: (n, h))
    gspec = pl.BlockSpec((hb, None, 1, C), lambda h, n: (h, n, 0, 0))
    return pl.pallas_call(
        body, name=name, grid=(H // hb, N),
        in_specs=[qspec, qspec, qspec, gspec, gspec],
        out_specs=[qspec, pl.BlockSpec((hb, None, _GD, _GD), lambda h, n: (h, n, 0, 0)),
                   pl.BlockSpec((hb, None, C, C), lambda h, n: (h, n, 0, 0))],
        out_shape=[jax.ShapeDtypeStruct((T, D), _F32), jax.ShapeDtypeStruct((H, N, _GD, _GD), _F32),
                   jax.ShapeDtypeStruct((H, N, C, C), _F32)],
        scratch_shapes=[pltpu.VMEM((hb, _GD, _GD), _F32)],
        compiler_params=_cp("arbitrary", "arbitrary"),
    )(qn, kn, vs, g_r, b_r)


def _gdn_bwd(qn, kn, vs, g_r, b_r, s_all, t_all, do, name):
    T, D = qn.shape
    H, N, C = D // _GD, T // _BC, _BC
    hb = min(_HB, H)

    def body(q_ref, k_ref, v_ref, g_ref, b_ref, s_ref, t_ref, do_ref, dq_ref, dk_ref, dv_ref, dg_ref, db_ref, dS):
        @pl.when(pl.program_id(1) == 0)
        def _():
            dS[...] = jnp.zeros_like(dS)

        hs = range(hb)
        sls = [slice(hh * _GD, (hh + 1) * _GD) for hh in hs]
        ks = [k_ref[:, sl] for sl in sls]
        vs_ = [v_ref[:, sl] for sl in sls]
        cms = [_gdn_chunk(q_ref[:, sl], k, v, g_ref[hh], b_ref[hh]) for hh, sl, k, v in zip(hs, sls, ks, vs_)]
        low, strict, eye, ii, jj = (cms[0][n] for n in ("low", "strict", "eye", "ii", "jj"))
        tms, s0s, dos, ds1s = [t_ref[hh] for hh in hs], [s_ref[hh] for hh in hs], [do_ref[:, sl] for sl in sls], \
            [dS[hh] for hh in hs]
        us = [_dot(tm, cm["vb"]) for tm, cm in zip(tms, cms)]
        ws = [_dot(tm, cm["kbg"]) for tm, cm in zip(tms, cms)]
        v_news = [u - _dot(w, s0) for u, w, s0 in zip(us, ws, s0s)]
        dv_news = [_dot(cm["qk"], do_, _TN) + _dot(cm["k_dec"], ds1) for cm, do_, ds1 in zip(cms, dos, ds1s)]
        dqks = [jnp.where(low, _dot(do_, vn, _NT), 0.0) for do_, vn in zip(dos, v_news)]
        dq_decs = [_dot(do_, s0, _NT) for do_, s0 in zip(dos, s0s)]
        dk_decs = [_dot(vn, ds1, _NT) for vn, ds1 in zip(v_news, ds1s)]
        dgls = [jnp.sum(jnp.sum(ds1 * s0, axis=1, keepdims=True), axis=0, keepdims=True) for ds1, s0 in zip(ds1s, s0s)]
        dws = [-_dot(dvn, s0, _NT) for dvn, s0 in zip(dv_news, s0s)]
        for hh in hs:
            dS[hh] = (_dot(cms[hh]["q_dec"], dos[hh], _TN) + cms[hh]["gl"] * ds1s[hh]
                      - _dot(ws[hh], dv_news[hh], _TN))
        dvbs = [_dot(tm, dvn, _TN) for tm, dvn in zip(tms, dv_news)]
        dkbgs = [_dot(tm, dw, _TN) for tm, dw in zip(tms, dws)]
        dAs = [-jnp.where(strict, _dot(dvb, u, _NT) + _dot(dkbg, w, _NT), 0.0)
               for dvb, u, dkbg, w in zip(dvbs, us, dkbgs, ws)]
        dkks = [dA * cm["decay"] for dA, cm in zip(dAs, cms)]
        dqkraws = [dqk * cm["decay"] for dqk, cm in zip(dqks, cms)]
        Es = [(dA * cm["kk"] + dqk * cm["qkraw"]) * cm["decay"] for dA, dqk, cm in zip(dAs, dqks, cms)]
        dkbs = [_dot(dkk, k) + dkbg * cm["eg"] for dkk, k, dkbg, cm in zip(dkks, ks, dkbgs, cms)]
        dqss = [_dot(dqr, k) + dqd * cm["eg"] for dqr, k, dqd, cm in zip(dqkraws, ks, dq_decs, cms)]
        for hh in hs:
            cm = cms[hh]
            dk_ref[:, sls[hh]] = (_dot(dqkraws[hh], cm["qs"], _TN) + _dot(dkks[hh], cm["kb"], _TN)
                                  + dk_decs[hh] * cm["ekd"] + dkbs[hh] * cm["b_col"])
            dv_ref[:, sls[hh]] = dvbs[hh] * cm["b_col"]
            dq_ref[:, sls[hh]] = dqss[hh] * (_GD ** -0.5)
        for hh in hs:
            cm, k, E = cms[hh], ks[hh], Es[hh]
            eg, ekd = cm["eg"], cm["ekd"]
            dbeta_col = jnp.sum(dvbs[hh] * vs_[hh] + dkbs[hh] * k, axis=1, keepdims=True)
            t_kd = jnp.sum(dk_decs[hh] * k, axis=1, keepdims=True) * ekd
            c1 = (jnp.sum(E, axis=1, keepdims=True) + jnp.sum(dkbgs[hh] * cm["kb"], axis=1, keepdims=True) * eg
                  + jnp.sum(dq_decs[hh] * cm["qs"], axis=1, keepdims=True) * eg - t_kd)
            r1 = jnp.sum(E, axis=0, keepdims=True)
            dgam_last = jnp.sum(t_kd, axis=0, keepdims=True) + dgls[hh] * cm["gl"]
            dgam_col = c1 - jnp.sum(jnp.where(eye, r1, 0.0), axis=1, keepdims=True)
            dg_ref[hh] = jnp.sum(jnp.where(ii >= jj, dgam_col, 0.0), axis=0, keepdims=True) + dgam_last
            db_ref[hh] = jnp.sum(jnp.where(eye, dbeta_col, 0.0), axis=0, keepdims=True)

    qspec = pl.BlockSpec((C, hb * _GD), lambda h, n: (N - 1 - n, h))
    gspec = pl.BlockSpec((hb, None, 1, C), lambda h, n: (h, N - 1 - n, 0, 0))
    return pl.pallas_call(
        body, name=name, grid=(H // hb, N),
        in_specs=[qspec, qspec, qspec, gspec, gspec,
                  pl.BlockSpec((hb, None, _GD, _GD), lambda h, n: (h, N - 1 - n, 0, 0)),
                  pl.BlockSpec((hb, None, C, C), lambda h, n: (h, N - 1 - n, 0, 0)), qspec],
        out_specs=[qspec, qspec, qspec, gspec, gspec],
        out_shape=[jax.ShapeDtypeStruct((T, D), _F32)] * 3 + [jax.ShapeDtypeStruct((H, N, 1, C), _F32)] * 2,
        scratch_shapes=[pltpu.VMEM((hb, _GD, _GD), _F32)],
        compiler_params=_cp("arbitrary", "arbitrary"),
    )(qn, kn, vs, g_r, b_r, s_all, t_all, do)


def _onorm_fwd(o, proj, go, li, D, name, tb=512):
    T = o.shape[0]
    H = D // _GD
    tb = _pick(T, tb, 8)

    def body(o_ref, z_ref, go_ref, y_ref):
        ov, zv, g = o_ref[...], z_ref[...], go_ref[...]
        for h in range(H):
            sl = slice(h * _GD, (h + 1) * _GD)
            oh = ov[:, sl]
            r = lax.rsqrt(jnp.mean(oh * oh, axis=1, keepdims=True) + _EPS)
            y_ref[:, sl] = (oh * r * g * _silu(zv[:, sl])).astype(y_ref.dtype)

    return pl.pallas_call(
        body, name=name, grid=(T // tb,), in_specs=[_rspec(tb, D), _rspec(tb, D, 5), _lspec((1, _GD), li)],
        out_specs=_rspec(tb, D), out_shape=jax.ShapeDtypeStruct((T, D), _MMT), compiler_params=_cp("parallel"),
    )(o, proj, go)


def _onorm_bwd(dy, o, proj, go, li, D, into, name, tb=256):
    T = o.shape[0]
    H = D // _GD
    tb = _pick(T, tb, 8)

    def body(dy_ref, o_ref, z_ref, go_ref, _, do_ref, dz_ref, dgo_ref):
        @pl.when(pl.program_id(0) == 0)
        def _():
            dgo_ref[...] = jnp.zeros_like(dgo_ref)

        dyv, ov, zv, g = dy_ref[...].astype(_F32), o_ref[...], z_ref[...], go_ref[...]
        dgo = jnp.zeros((1, _GD), _F32)
        for h in range(H):
            sl = slice(h * _GD, (h + 1) * _GD)
            oh, zh, dyh = ov[:, sl], zv[:, sl], dyv[:, sl]
            r = lax.rsqrt(jnp.mean(oh * oh, axis=1, keepdims=True) + _EPS)
            on = oh * r
            sz = _silu(zh)
            dgo = dgo + jnp.sum(dyh * sz * on, axis=0, keepdims=True)
            don = dyh * sz * g
            do_ref[:, sl] = r * (don - on * jnp.mean(don * on, axis=1, keepdims=True))
            dz_ref[:, sl] = (dyh * on * g * _dsilu(zh)).astype(dz_ref.dtype)
        dgo_ref[...] += dgo

    return pl.pallas_call(
        body, name=name, grid=(T // tb,),
        in_specs=[_rspec(tb, D), _rspec(tb, D), _rspec(tb, D, 5), _lspec((1, _GD), li), _HBM],
        out_specs=[_rspec(tb, D), _rspec(tb, D, 5), _fspec((1, _GD))],
        out_shape=[jax.ShapeDtypeStruct((T, D), _F32), jax.ShapeDtypeStruct(into.shape, into.dtype),
                   jax.ShapeDtypeStruct((1, _GD), _F32)],
        input_output_aliases={4: 1}, compiler_params=_cp("arbitrary"),
    )(dy, o, proj, go, into)


def _merge_bwd(dm, pa, pb, proj, D, name, tb=256):
    T, PW = proj.shape
    tb = _pick(T, tb, 8)

    def body(dm_ref, pa_ref, pb_ref, ga_ref, gb_ref, dpa_ref, dpb_ref, dg_ref):
        d = dm_ref[...].astype(_F32)
        sa, sb = _sigmoid(ga_ref[...]), _sigmoid(gb_ref[...])
        dpa_ref[...] = (d * sa).astype(dpa_ref.dtype)
        dpb_ref[...] = (d * sb).astype(dpb_ref.dtype)
        dg_ref[:, :D] = (d * pa_ref[...].astype(_F32) * sa * (1.0 - sa)).astype(dg_ref.dtype)
        dg_ref[:, D:] = (d * pb_ref[...].astype(_F32) * sb * (1.0 - sb)).astype(dg_ref.dtype)

    return pl.pallas_call(
        body, name=name, grid=(T // tb,),
        in_specs=[_rspec(tb, D), _rspec(tb, D), _rspec(tb, D), _rspec(tb, D, 6), _rspec(tb, D, 7)],
        out_specs=[_rspec(tb, D), _rspec(tb, D), _rspec(tb, 2 * D, 3)],
        out_shape=[jax.ShapeDtypeStruct((T, D), _MMT)] * 2 + [jax.ShapeDtypeStruct((T, PW), _MMT)],
        compiler_params=_cp("parallel"),
    )(dm, pa, pb, proj, proj)


def _swiglu_bwd(da, gate, up, name, tb=256):
    T, F = gate.shape
    F2 = 2 * F
    tb = _pick(T, tb, 8)

    def body(da_ref, g_ref, u_ref, o_ref):
        d, g = da_ref[...].astype(_F32), g_ref[...].astype(_F32)
        o_ref[:, :F] = (d * u_ref[...].astype(_F32) * _dsilu(g)).astype(o_ref.dtype)
        o_ref[:, F:] = (d * _silu(g)).astype(o_ref.dtype)

    return pl.pallas_call(
        body, name=name, grid=(T // tb,), in_specs=[_rspec(tb, F), _rspec(tb, F), _rspec(tb, F)],
        out_specs=_rspec(tb, F2), out_shape=jax.ShapeDtypeStruct((T, F2), _MMT), compiler_params=_cp("parallel"),
    )(da, gate, up)


def _loss_head(x, tgt, fg, name, tb=256):
    T, D = x.shape
    tb = _pick(T, tb, 8)

    def body(x_ref, t_ref, fg_ref, loss_ref, dx_ref, dfg_ref):
        @pl.when(pl.program_id(0) == 0)
        def _():
            loss_ref[...] = jnp.zeros_like(loss_ref)
            dfg_ref[...] = jnp.zeros_like(dfg_ref)

        xv, fg_ = x_ref[...], fg_ref[...]
        r = lax.rsqrt(jnp.mean(xv * xv, axis=1, keepdims=True) + _EPS)
        xn = xv * r
        e = xn * fg_ - t_ref[...]
        loss_ref[...] += (0.5 / D) * jnp.sum(jnp.sum(e * e, axis=1, keepdims=True), axis=0, keepdims=True)
        dy = e * (1.0 / D)
        dfg_ref[...] += jnp.sum(dy * xn, axis=0, keepdims=True)
        dxn = dy * fg_
        dx_ref[...] = r * (dxn - xn * jnp.mean(dxn * xn, axis=1, keepdims=True))

    return pl.pallas_call(
        body, name=name, grid=(T // tb,), in_specs=[_rspec(tb, D), _rspec(tb, D), _fspec((1, D))],
        out_specs=[_fspec((1, 1)), _rspec(tb, D), _fspec((1, D))],
        out_shape=[jax.ShapeDtypeStruct((1, 1), _F32), jax.ShapeDtypeStruct((T, D), _F32),
                   jax.ShapeDtypeStruct((1, D), _F32)],
        compiler_params=_cp("arbitrary"),
    )(x, tgt, fg)


def _row_tile(R, W, budget=1 << 20, unit=8):
    if R * W * 4 <= budget or R % unit:
        return R
    best = unit
    for t in range(unit, R + 1, unit):
        if R % t == 0 and t * W * 4 <= budget:
            best = t
    return best


def _sum_slots(st, name):
    P, R, W = st.shape
    tb = _row_tile(R, W, 1 << 20, 16)

    def body(s_ref, o_ref):
        acc = s_ref[0].astype(_F32)
        for p in range(1, P):
            acc = acc + s_ref[p].astype(_F32)
        o_ref[...] = acc

    return pl.pallas_call(
        body, name=name, grid=(R // tb,), in_specs=[pl.BlockSpec((P, tb, W), lambda i: (0, i, 0))],
        out_specs=_rspec(tb, W), out_shape=jax.ShapeDtypeStruct((R, W), _F32), compiler_params=_cp("parallel"),
    )(st)


def _adamw(w, gst, m, v, name):
    R, W = w.shape
    P = gst.shape[0]
    tb = _row_tile(R, W, 1 << 20)
    c1, c2 = 1.0 - _B1 ** _STEP, 1.0 - _B2 ** _STEP

    def body(w_ref, g_ref, m_ref, v_ref, go_ref, d_ref, mo_ref, vo_ref):
        g = g_ref[0]
        for p in range(1, P):
            g = g + g_ref[p]
        mn = _B1 * m_ref[...] + (1.0 - _B1) * g
        vn = _B2 * v_ref[...] + (1.0 - _B2) * (g * g)
        go_ref[...] = g
        mo_ref[...] = mn
        vo_ref[...] = vn
        d_ref[...] = -_LR * ((mn / c1) / (jnp.sqrt(vn / c2) + _AEPS) + _WD * w_ref[...])

    spec = _rspec(tb, W)
    return pl.pallas_call(
        body, name=name, grid=(R // tb,),
        in_specs=[spec, pl.BlockSpec((P, tb, W), lambda i: (0, i, 0)), spec, spec],
        out_specs=[spec] * 4, out_shape=[jax.ShapeDtypeStruct((R, W), _F32)] * 4, compiler_params=_cp("parallel"),
    )(w, gst, m, v)


def _as2d(a):
    if a.ndim == 1:
        return a.reshape(1, -1)
    return a.reshape(-1, a.shape[-1])


def kernel(x, c, ada_w, ada_b, norm1_g, w_in, conv_w, spatial_w, spatial_b, v_norm_g, a_log, dt_bias, o_norm_g, w_branch_a, w_branch_b, w_out, norm2_g, w_ffn_in, w_ffn_out, final_g, loss_target, m_ada_w, m_ada_b, m_norm1_g, m_w_in, m_conv_w, m_spatial_w, m_spatial_b, m_v_norm_g, m_a_log, m_dt_bias, m_o_norm_g, m_w_branch_a, m_w_branch_b, m_w_out, m_norm2_g, m_w_ffn_in, m_w_ffn_out, m_final_g, v_ada_w, v_ada_b, v_norm1_g, v_w_in, v_conv_w, v_spatial_w, v_spatial_b, v_v_norm_g, v_a_log, v_dt_bias, v_o_norm_g, v_w_branch_a, v_w_branch_b, v_w_out, v_norm2_g, v_w_ffn_in, v_w_ffn_out, v_final_g):
    xb, tgt = x[0], loss_target[0]
    T, D = xb.shape
    L, H, G = ada_w.shape[0], a_log.shape[1], spatial_w.shape[1]
    F = 4 * w_ffn_out.shape[1]
    N = T // _BC
    Ws = ada_w.shape[2]
    Wc = w_in.shape[2]
    PW = 8 * D + _LANE
    ix, iy, ic = lax.axis_index("x"), lax.axis_index("y"), lax.axis_index("c")
    me = 4 * ix + 2 * iy + ic

    c_all = _gather8(c, "gather_c").reshape(8, D)
    modp = _ada_fwd(c_all, ada_w, "ada_fwd")
    n_mod, n_cw = L * 8 * Ws, L * _KC * conv_w.shape[2]
    pad = (-(n_mod + n_cw)) % _LANE
    pay = jnp.concatenate([modp.reshape(-1), conv_w.reshape(-1), jnp.zeros((pad,), _F32)]).reshape(-1, _LANE)
    pay_all = _gather8(pay, "gather_mod").reshape(8, -1)
    mod_full = jnp.concatenate([pay_all[2 * k, :n_mod].reshape(L, 8, Ws) for k in range(4)], axis=-1)
    cw_full = jnp.concatenate([pay_all[2 * k, n_mod:n_mod + n_cw].reshape(L, _KC, -1) for k in range(4)], axis=-1)
    mod = lax.dynamic_index_in_dim(mod_full, me, axis=1, keepdims=False) + ada_b
    mods = [[mod[l, j * D:(j + 1) * D].reshape(1, D) for j in range(6)] for l in range(L)]

    big = [w_in, w_branch_a, w_branch_b, w_out, w_ffn_in, w_ffn_out]
    chip = 2 * ix + iy
    starts = [(k * Wc) // 16 * 16 for k in range(4)]
    Hh = max(-(-((k + 1) * Wc) // 16) * 16 - starts[k] for k in range(4))
    No = max(s + Hh for s in starts)
    my_off = jnp.asarray([k * Wc - starts[k] for k in range(4)], jnp.int32)[chip]
    cuts = sorted(set(starts + [s + Hh for s in starts]))

    pers = [Hh, D // 4, D // 4, D // 4, 2 * F // 4, F // 4]
    roff = [0]
    for p in pers:
        roff.append(roff[-1] + L * p)
    Rp = -(-roff[-1] // (32 * _NCH)) * (32 * _NCH)
    rpad = Rp - roff[-1]

    hull = lax.dynamic_update_slice(jnp.zeros((L, Hh, D), _F32), jnp.transpose(w_in, (0, 2, 1)), (0, my_off, 0))
    shard = jnp.concatenate(
        [hull.reshape(-1, D).astype(_MMT), w_branch_a.reshape(-1, D).astype(_MMT),
         w_branch_b.reshape(-1, D).astype(_MMT), w_out.reshape(-1, D).astype(_MMT),
         jnp.transpose(w_ffn_in, (0, 2, 1)).reshape(-1, D).astype(_MMT), w_ffn_out.reshape(-1, D).astype(_MMT),
         jnp.zeros((rpad, D), _MMT)], axis=0)
    gw = _fill_from_sibling(_gather_chips(shard, "gather_w"), "gather_w_sib")

    def slab(i, l, k):
        a = roff[i] + l * pers[i]
        return gw[k, a:a + pers[i]]

    def joined(i, l):
        return jnp.concatenate([slab(i, l, k) for k in range(4)], axis=0)

    def orig_rows(hulls, a, b):
        edges = sorted(set([a, b] + [c_ for c_ in cuts if a < c_ < b]))
        out = []
        for lo, hi in zip(edges[:-1], edges[1:]):
            cov = [k for k in range(4) if starts[k] <= lo and hi <= starts[k] + Hh]
            piece = hulls[cov[0]][lo - starts[cov[0]]:hi - starts[cov[0]]]
            for k in cov[1:]:
                piece = piece + hulls[k][lo - starts[k]:hi - starts[k]]
            out.append(piece)
        return out

    wt_in_p = []
    for l in range(L):
        hulls = [slab(0, l, k) for k in range(4)]
        wt_in_p.append(jnp.concatenate(
            orig_rows(hulls, 0, 6 * D) + orig_rows(hulls, 6 * D + 2 * H, 8 * D + 2 * H)
            + orig_rows(hulls, 6 * D, 6 * D + 2 * H) + [jnp.zeros((_LANE - 2 * H, D), _MMT)], axis=0))
    w_a, w_b, w_o, wt_fi, w_fo = ([joined(i, l) for l in range(L)] for i in range(1, 6))

    sbt = jnp.transpose(spatial_b, (0, 2, 1))
    gv3, go3 = v_norm_g.reshape(L, 1, D), o_norm_g.reshape(L, 1, _GD)
    zpad = jnp.zeros((L, _LANE - 2 * H), _F32)
    alog_row = jnp.concatenate([jnp.zeros((L, H), _F32), a_log, zpad], axis=1).reshape(L, 1, _LANE)
    dtb_row = jnp.concatenate([jnp.zeros((L, H), _F32), dt_bias, zpad], axis=1).reshape(L, 1, _LANE)

    def rows_of(tok):
        return jnp.transpose(tok.reshape(N, _BC, H), (2, 0, 1)).reshape(H, N, 1, _BC)

    def toks_of(rows):
        return jnp.transpose(rows.reshape(H, N, _BC), (1, 2, 0)).reshape(T, H)

    saved = []
    xc = xb
    for l in range(L):
        sh1, sc1, gt1, sh2, sc2, gt2 = mods[l]
        g1, g2 = norm1_g[l].reshape(1, D), norm2_g[l].reshape(1, D)
        h = _norm_mod(xc, g1, sc1, sh1, f"norm1_{l}")
        proj = _mm(h, wt_in_p[l], f"proj_{l}", trans_b=True, tn=1664)
        ya = _gmlp_fwd(proj, spatial_w, sbt, gv3, l, D, f"gmlp_{l}")
        qn, kn, vs, bg = _conv_fwd(proj, cw_full, alog_row, dtb_row, l, D, f"conv_{l}")
        g_r, b_r = rows_of(bg[:, H:2 * H]), rows_of(bg[:, :H])
        o, s_all, t_all = _gdn_fwd(qn, kn, vs, g_r, b_r, f"gdn_{l}")
        yb = _onorm_fwd(o, proj, go3, l, D, f"onorm_{l}")
        pa, pb, mg = _branch_merge(ya, yb, w_a[l], w_b[l], proj, f"branch_{l}")
        p1, x1 = _mm_res(mg, w_o[l], xc, gt1, f"wout_{l}")
        h2 = _norm_mod(x1, g2, sc2, sh2, f"norm2_{l}")
        gate, up, act = _ffin_swiglu(h2, wt_fi[l], f"ffin_{l}")
        p2, x2 = _mm_res(act, w_fo[l], x1, gt2, f"ffout_{l}")
        saved.append(dict(x=xc, h=h, proj=proj, ya=ya, yb=yb, qn=qn, kn=kn, vs=vs, g_r=g_r, b_r=b_r, o=o,
                          s_all=s_all, t_all=t_all, pa=pa, pb=pb, mg=mg, p1=p1, x1=x1, h2=h2, gate=gate, up=up,
                          act=act, p2=p2))
        xc = x2

    loss11, dx, dfg = _loss_head(xc, tgt, final_g.reshape(1, D), "loss_head")
    loss = lax.psum(loss11[0, 0], ("x", "y", "c"))

    gbig = {k: [None] * L for k in ("w_in", "w_a", "w_b", "w_o", "w_fi", "w_fo")}
    small = {k: [None] * L for k in ("dmod", "n1", "n2", "sw", "sb", "gv", "cw", "al", "dt", "go")}
    for l in reversed(range(L)):
        sv = saved[l]
        sh1, sc1, gt1, sh2, sc2, gt2 = mods[l]
        g1, g2 = norm1_g[l].reshape(1, D), norm2_g[l].reshape(1, D)
        proj = sv["proj"]
        dp2, dgt2 = _resid_bwd(dx, sv["p2"], gt2, f"res2b_{l}")
        da = _mm(dp2, w_fo[l], f"ffoutb_{l}", trans_b=True, out_dtype=_MMT)
        gbig["w_fo"][l] = _mm_tn(sv["act"], dp2, f"ffoutw_{l}")
        dgu = _swiglu_bwd(da, sv["gate"], sv["up"], f"swiglub_{l}")
        dh2 = _mm(dgu, wt_fi[l], f"ffinb_{l}")
        gbig["w_fi"][l] = _mm_tn(dgu, sv["h2"], f"ffinw_{l}")
        dx1, dgm2, dsh2 = _norm_mod_bwd(sv["x1"], dh2, dx, g2, sc2, f"norm2b_{l}")
        dp1, dgt1 = _resid_bwd(dx1, sv["p1"], gt1, f"res1b_{l}")
        dmg = _mm(dp1, w_o[l], f"woutb_{l}", trans_b=True, out_dtype=_MMT)
        gbig["w_o"][l] = _mm_tn(sv["mg"], dp1, f"woutw_{l}")
        dpa, dpb, dproj = _merge_bwd(dmg, sv["pa"], sv["pb"], proj, D, f"mergeb_{l}")
        dya = _mm(dpa, w_a[l], f"brab_{l}", trans_b=True, out_dtype=_MMT)
        gbig["w_a"][l] = _mm_tn(sv["ya"], dpa, f"braw_{l}")
        dyb = _mm(dpb, w_b[l], f"brbb_{l}", trans_b=True, out_dtype=_MMT)
        gbig["w_b"][l] = _mm_tn(sv["yb"], dpb, f"brbw_{l}")
        dproj, dsw, dsa, dgv = _gmlp_bwd(proj, dya, spatial_w, sbt, gv3, l, D, dproj, f"gmlpb_{l}")
        do, dproj, dgo = _onorm_bwd(dyb, sv["o"], proj, go3, l, D, dproj, f"onormb_{l}")
        dqn, dkn, dvs, dg_r, db_r = _gdn_bwd(sv["qn"], sv["kn"], sv["vs"], sv["g_r"], sv["b_r"], sv["s_all"],
                                             sv["t_all"], do, f"gdnb_{l}")
        dbg = jnp.concatenate([toks_of(db_r), toks_of(dg_r), jnp.zeros((T, _LANE - 2 * H), _F32)], axis=1)
        dc, dproj, dcw, dal, ddt = _conv_bwd1(proj, dqn, dkn, dvs, dbg, cw_full, alog_row, dtb_row, l, D, dproj,
                                              f"convb_{l}")
        dproj = _conv_bwd2(dc, cw_full, l, dproj, f"convx_{l}")
        dh = _mm(dproj, wt_in_p[l], f"projb_{l}", tk=1664)
        gbig["w_in"][l] = _mm_tn(dproj, sv["h"], f"projw_{l}", tm=640)
        dx, dgm1, dsh1 = _norm_mod_bwd(sv["x"], dh, dx1, g1, sc1, f"norm1b_{l}")
        small["dmod"][l] = jnp.concatenate([dsh1, dgm1 * g1, dgt1, dsh2, dgm2 * g2, dgt2], axis=1)
        small["n1"][l], small["n2"][l] = dgm1 * (1.0 + sc1), dgm2 * (1.0 + sc2)
        small["sw"][l], small["gv"][l], small["cw"][l], small["go"][l] = dsw, dgv, dcw, dgo
        small["sb"][l] = jnp.transpose(dsa.reshape(_AC, G, _GD).sum(axis=-1))
        small["al"][l], small["dt"][l] = dal[:, H:2 * H], ddt[:, H:2 * H]
    grad_x = dx.reshape(1, T, D)

    names_small = ["dmod", "n1", "n2", "sw", "sb", "gv", "cw", "al", "dt", "go"]
    flat = [jnp.stack(small[k]).reshape(-1) for k in names_small] + [dfg.reshape(-1)]
    sizes = [f.shape[0] for f in flat]
    tot = sum(sizes)
    pad = (-tot) % 1024
    pay = jnp.concatenate(flat + [jnp.zeros((pad,), _F32)]).reshape(-1, 1024)
    sm_all = _gather8(pay, "gather_small").reshape(8, -1)
    offs = [0]
    for s in sizes:
        offs.append(offs[-1] + s)
    part = {k: sm_all[:, offs[i]:offs[i + 1]] for i, k in enumerate(names_small + ["fg"])}
    dmod_all = part["dmod"].reshape(8, L, 6 * D)

    outs = {}

    def update(nm, w, gst, m, v):
        shp = w.shape
        w2 = _as2d(w)
        g, d, mn, vn = _adamw(w2, gst.reshape((gst.shape[0],) + w2.shape), _as2d(m), _as2d(v), f"adamw_{nm}")
        outs[nm] = (g.reshape(shp), d.reshape(shp), mn.reshape(shp), vn.reshape(shp))

    chip = 2 * ix + iy
    dmod_t = jnp.transpose(dmod_all, (1, 0, 2))
    dmod_mine = lax.dynamic_slice_in_dim(dmod_t, chip * Ws, Ws, axis=2)
    g_ada_w = _ada_bwd(jnp.transpose(c_all), dmod_mine, "ada_bwd")
    update("ada_w", ada_w, g_ada_w[None], m_ada_w, v_ada_w)
    update("ada_b", ada_b, dmod_all, m_ada_b, v_ada_b)
    update("norm1_g", norm1_g, part["n1"], m_norm1_g, v_norm1_g)
    update("norm2_g", norm2_g, part["n2"], m_norm2_g, v_norm2_g)
    update("spatial_w", spatial_w, part["sw"], m_spatial_w, v_spatial_w)
    update("spatial_b", spatial_b, part["sb"], m_spatial_b, v_spatial_b)
    update("v_norm_g", v_norm_g, part["gv"], m_v_norm_g, v_v_norm_g)
    update("a_log", a_log, part["al"], m_a_log, v_a_log)
    update("dt_bias", dt_bias, part["dt"], m_dt_bias, v_dt_bias)
    update("o_norm_g", o_norm_g, part["go"], m_o_norm_g, v_o_norm_g)
    update("final_g", final_g, part["fg"], m_final_g, v_final_g)
    cw_cols = conv_w.shape[2]
    dcw_all = part["cw"].reshape(8, L, _KC, 4 * cw_cols)
    update("conv_w", conv_w, lax.dynamic_slice_in_dim(dcw_all, chip * cw_cols, cw_cols, axis=3), m_conv_w, v_conv_w)

    srcs = [jnp.zeros((max(rpad, No - (8 * D + 2 * H), 16), D), _MMT)]
    where = {}
    for nm in ("w_in", "w_a", "w_b", "w_o", "w_fi", "w_fo"):
        for l in range(L):
            where[nm, l] = len(srcs)
            srcs.append(gbig[nm][l])
    pieces = []
    for k in range(4):
        f = 0
        for l in range(L):
            a, b = starts[k], starts[k] + Hh
            for lo, hi, plo in ((0, 6 * D, 0), (6 * D, 6 * D + 2 * H, 8 * D), (6 * D + 2 * H, 8 * D + 2 * H, 6 * D),
                                (8 * D + 2 * H, No, None)):
                s, e = max(a, lo), min(b, hi)
                if s < e:
                    pieces.append((0, 0, e - s, k, f) if plo is None else (where["w_in", l], plo + s - lo, e - s, k, f))
                    f += e - s
        for i, nm in enumerate(("w_a", "w_b", "w_o", "w_fi", "w_fo")):
            per = pers[i + 1]
            for l in range(L):
                pieces.append((where[nm, l], k * per, per, k, f))
                f += per
        if rpad:
            pieces.append((0, 0, rpad, k, f))
            f += rpad
        assert f == Rp
    own, got = _pack_halves(srcs, pieces, 4, Rp, "reduce_cores")
    chipsum = _add_pair(own, got, "add_cores")
    parts = _scatter_to_chips(chipsum, "reduce_chips")
    mine = _sum_slots(parts, "add_chips")
    other = _swap_with_sibling(mine, "swap_cores")
    first = ic == 0
    gsum = jnp.concatenate([jnp.where(first, mine, other), jnp.where(first, other, mine)], axis=0)
    big_names = ["w_in", "w_branch_a", "w_branch_b", "w_out", "w_ffn_in", "w_ffn_out"]
    big_m = [m_w_in, m_w_branch_a, m_w_branch_b, m_w_out, m_w_ffn_in, m_w_ffn_out]
    big_v = [v_w_in, v_w_branch_a, v_w_branch_b, v_w_out, v_w_ffn_in, v_w_ffn_out]
    for i, (nm, w, m, v) in enumerate(zip(big_names, big, big_m, big_v)):
        g = gsum[roff[i]:roff[i + 1]].reshape(L, pers[i], D)
        if i == 0:
            g = jnp.transpose(lax.dynamic_slice_in_dim(g, my_off, Wc, axis=1), (0, 2, 1))
        elif i == 4:
            g = jnp.transpose(g, (0, 2, 1))
        update(nm, w, g[None], m, v)

    order = ["ada_w", "ada_b", "norm1_g", "w_in", "conv_w", "spatial_w", "spatial_b", "v_norm_g", "a_log", "dt_bias",
             "o_norm_g", "w_branch_a", "w_branch_b", "w_out", "norm2_g", "w_ffn_in", "w_ffn_out", "final_g"]
    return (loss, grad_x, *[outs[n][0] for n in order], *[outs[n][1] for n in order],
            *[outs[n][2] for n in order], *[outs[n][3] for n in order])
```

```python
import functools
import math

import jax
import jax.numpy as jnp
from jax import lax
from jax.experimental import pallas as pl
from jax.experimental.pallas import tpu as pltpu

_F32 = jnp.float32
_BF = jnp.bfloat16
_MMT = jnp.bfloat16
_EPS = 1e-6
_GD = 128
_AC = 128
_BC = 64
_KC = 4
_HB = 8
_NCH = 8
_LANE = 128
_VMEM_LIMIT = 56 * 1024 * 1024

_LR, _B1, _B2, _AEPS, _WD, _STEP = 0.001, 0.9, 0.999, 1e-08, 0.01, 10

_NN = (((1,), (0,)), ((), ()))
_NT = (((1,), (1,)), ((), ()))
_TN = (((0,), (0,)), ((), ()))

_MESH = pl.DeviceIdType.MESH


def _cp(*sem):
    return pltpu.CompilerParams(dimension_semantics=tuple(sem), vmem_limit_bytes=_VMEM_LIMIT)


def _dot(a, b, dn=_NN):
    return lax.dot_general(a.astype(_MMT), b.astype(_MMT), dn, preferred_element_type=_F32)


def _pick(n, target, unit=_LANE):
    if n <= target:
        return n
    best = None
    for t in range(unit, target + 1, unit):
        if n % t == 0:
            best = t
    assert best is not None, (n, target)
    return best


def _sigmoid(x):
    return 0.5 * jnp.tanh(0.5 * x) + 0.5


def _silu(x):
    return x * _sigmoid(x)


def _dsilu(x):
    s = _sigmoid(x)
    return s * (1.0 + x * (1.0 - s))


_GK = math.sqrt(2.0 / math.pi)


def _gelu(x):
    return 0.5 * x * (1.0 + jnp.tanh(_GK * (x + 0.044715 * x * x * x)))


def _dgelu(x):
    t = jnp.tanh(_GK * (x + 0.044715 * x * x * x))
    return 0.5 * (1.0 + t) + 0.5 * x * (1.0 - t * t) * _GK * (1.0 + 3.0 * 0.044715 * x * x)


def _softplus(x):
    return jnp.maximum(x, 0.0) + jnp.log(1.0 + jnp.exp(-jnp.abs(x)))


def _rspec(tb, w, cb=0):
    return pl.BlockSpec((tb, w), lambda i: (i, cb))


def _fspec(shape):
    nd = len(shape)
    return pl.BlockSpec(tuple(shape), lambda i: (0,) * nd)


def _lspec(tail, li):
    nd = len(tail)
    return pl.BlockSpec((None,) + tuple(tail), lambda i: (li,) + (0,) * nd)


def _slot_all8(x, y, c):
    return 4 * x + 2 * y + c


def _gather8(v, name):
    R, W = v.shape

    def body(v_ref, o_ref, ssem, rsem, lsem):
        x, y, c = lax.axis_index("x"), lax.axis_index("y"), lax.axis_index("c")
        sib = (x, y, 1 - c)
        chips = _other_chips(x, y)

        def slot(px, py, pc):
            return o_ref.at[_slot_all8(px, py, pc)]

        own = pltpu.make_async_copy(v_ref, slot(x, y, c), lsem)
        own.start()
        started = [_rcopy(v_ref, slot(x, y, c), ssem.at[0], rsem.at[0], sib)]
        started += [_rcopy(v_ref, slot(x, y, c), ssem.at[1 + j], rsem.at[1 + j], (px, py, c))
                    for j, (px, py) in enumerate(chips)]
        for cp in started:
            cp.start()
        for j, (px, py) in enumerate(chips):
            blk = slot(px, py, c)
            _rcopy(blk, blk, ssem.at[1 + j], rsem.at[1 + j], (px, py, c)).wait_recv()
            fw = _rcopy(blk, blk, ssem.at[4 + j], rsem.at[4 + j], sib)
            fw.start()
            started.append(fw)
        blk = slot(x, y, 1 - c)
        _rcopy(blk, blk, ssem.at[0], rsem.at[0], sib).wait_recv()
        for j, (px, py) in enumerate(chips):
            blk = slot(px, py, 1 - c)
            _rcopy(blk, blk, ssem.at[4 + j], rsem.at[4 + j], sib).wait_recv()
        for cp in started:
            cp.wait_send()
        own.wait()

    return pl.pallas_call(
        body, name=name, out_shape=jax.ShapeDtypeStruct((8, R, W), v.dtype), in_specs=[_HBM], out_specs=_HBM,
        scratch_shapes=[pltpu.SemaphoreType.DMA((7,)), pltpu.SemaphoreType.DMA((7,)), pltpu.SemaphoreType.DMA],
    )(v)


def _rcopy(src, dst, ssem, rsem, dev):
    return pltpu.make_async_remote_copy(src_ref=src, dst_ref=dst, send_sem=ssem, recv_sem=rsem,
                                        device_id=dev, device_id_type=_MESH)


def _other_chips(x, y):
    return [(1 - x, y), (x, 1 - y), (1 - x, 1 - y)]


_HBM = pl.BlockSpec(memory_space=pl.ANY)


def _gather_chips(shard, name):
    Rp, W = shard.shape
    Rh = Rp // 2
    rc = Rh // _NCH
    hq = _NCH // 2

    def body(s_ref, o_ref, ssem, rsem, lsem):
        x, y, c = lax.axis_index("x"), lax.axis_index("y"), lax.axis_index("c")
        chip = 2 * x + y
        xn, yn, dg = _other_chips(x, y)
        cx, cy, cd = 2 * xn[0] + xn[1], 2 * yn[0] + yn[1], 2 * dg[0] + dg[1]

        def rows(q):
            return pl.ds(c * Rh + q * rc, rc)

        locs = []
        for q in range(_NCH):
            lc = pltpu.make_async_copy(s_ref.at[rows(q)], o_ref.at[chip, rows(q)], lsem.at[q])
            lc.start()
            locs.append(lc)
        started = []
        for q in range(_NCH):
            for j, nb in ((0, xn), (1, yn)):
                cp = _rcopy(s_ref.at[rows(q)], o_ref.at[chip, rows(q)], ssem.at[j * _NCH + q], rsem.at[j * _NCH + q],
                            (nb[0], nb[1], c))
                cp.start()
                started.append(cp)
        for q in range(_NCH):
            bx = o_ref.at[cx, rows(q)]
            _rcopy(bx, bx, ssem.at[q], rsem.at[q], (xn[0], xn[1], c)).wait_recv()
            if q >= hq:
                rl = _rcopy(bx, bx, ssem.at[2 * _NCH + q], rsem.at[2 * _NCH + q], (yn[0], yn[1], c))
                rl.start()
                started.append(rl)
            by = o_ref.at[cy, rows(q)]
            _rcopy(by, by, ssem.at[_NCH + q], rsem.at[_NCH + q], (yn[0], yn[1], c)).wait_recv()
            if q < hq:
                rl = _rcopy(by, by, ssem.at[2 * _NCH + q], rsem.at[2 * _NCH + q], (xn[0], xn[1], c))
                rl.start()
                started.append(rl)
        for q in range(_NCH):
            bd = o_ref.at[cd, rows(q)]
            _rcopy(bd, bd, ssem.at[2 * _NCH + q], rsem.at[2 * _NCH + q], (dg[0], dg[1], c)).wait_recv()
        for cp in started:
            cp.wait_send()
        for lc in locs:
            lc.wait()

    return pl.pallas_call(
        body, name=name, out_shape=jax.ShapeDtypeStruct((4, Rp, W), shard.dtype), in_specs=[_HBM], out_specs=_HBM,
        scratch_shapes=[pltpu.SemaphoreType.DMA((3 * _NCH,))] * 2 + [pltpu.SemaphoreType.DMA((_NCH,))],
    )(shard)


def _fill_from_sibling(buf, name):
    P, Rp, W = buf.shape
    Rh = Rp // 2
    rc = Rh // _NCH

    def body(s_ref, o_ref, ssem, rsem):
        x, y, c = lax.axis_index("x"), lax.axis_index("y"), lax.axis_index("c")
        cps = []
        for k in range(P):
            for q in range(_NCH):
                r = pl.ds(c * Rh + q * rc, rc)
                cp = _rcopy(s_ref.at[k, r], o_ref.at[k, r], ssem.at[k * _NCH + q], rsem.at[k * _NCH + q],
                            (x, y, 1 - c))
                cp.start()
                cps.append(cp)
        for k in range(P):
            for q in range(_NCH):
                blk = o_ref.at[k, pl.ds((1 - c) * Rh + q * rc, rc)]
                _rcopy(blk, blk, ssem.at[k * _NCH + q], rsem.at[k * _NCH + q], (x, y, 1 - c)).wait_recv()
        for cp in cps:
            cp.wait_send()

    return pl.pallas_call(
        body, name=name, out_shape=jax.ShapeDtypeStruct(buf.shape, buf.dtype), in_specs=[_HBM], out_specs=_HBM,
        scratch_shapes=[pltpu.SemaphoreType.DMA((P * _NCH,))] * 2, input_output_aliases={0: 0},
    )(buf)


def _send_half_to_sibling(send, name):
    P, Rp, W = send.shape
    Rh = Rp // 2
    rc = Rh // _NCH

    def body(s_ref, o_ref, ssem, rsem):
        x, y, c = lax.axis_index("x"), lax.axis_index("y"), lax.axis_index("c")
        cps = []
        for k in range(P):
            for q in range(_NCH):
                cp = _rcopy(s_ref.at[k, pl.ds((1 - c) * Rh + q * rc, rc)], o_ref.at[k, pl.ds(q * rc, rc)],
                            ssem.at[k * _NCH + q], rsem.at[k * _NCH + q], (x, y, 1 - c))
                cp.start()
                cps.append(cp)
        for cp in cps:
            cp.wait()

    return pl.pallas_call(
        body, name=name, out_shape=jax.ShapeDtypeStruct((P, Rh, W), send.dtype), in_specs=[_HBM], out_specs=_HBM,
        scratch_shapes=[pltpu.SemaphoreType.DMA((P * _NCH,))] * 2,
    )(send)


def _scatter_to_chips(cs, name):
    P, Rh, W = cs.shape
    rc = Rh // _NCH

    def body(s_ref, o_ref, ssem, rsem, lsem):
        x, y, c = lax.axis_index("x"), lax.axis_index("y"), lax.axis_index("c")
        chip = 2 * x + y
        peers = _other_chips(x, y)
        locs = []
        for q in range(_NCH):
            r = pl.ds(q * rc, rc)
            lc = pltpu.make_async_copy(s_ref.at[chip, r], o_ref.at[chip, r], lsem.at[q])
            lc.start()
            locs.append(lc)
        cps = []
        for j, (px, py) in enumerate(peers):
            for q in range(_NCH):
                r = pl.ds(q * rc, rc)
                cp = _rcopy(s_ref.at[2 * px + py, r], o_ref.at[chip, r], ssem.at[j * _NCH + q], rsem.at[j * _NCH + q],
                            (px, py, c))
                cp.start()
                cps.append(cp)
        for j, (px, py) in enumerate(peers):
            for q in range(_NCH):
                blk = o_ref.at[2 * px + py, pl.ds(q * rc, rc)]
                _rcopy(blk, blk, ssem.at[j * _NCH + q], rsem.at[j * _NCH + q], (px, py, c)).wait_recv()
        for cp in cps:
            cp.wait_send()
        for lc in locs:
            lc.wait()

    return pl.pallas_call(
        body, name=name, out_shape=jax.ShapeDtypeStruct((P, Rh, W), cs.dtype), in_specs=[_HBM], out_specs=_HBM,
        scratch_shapes=[pltpu.SemaphoreType.DMA((3 * _NCH,))] * 2 + [pltpu.SemaphoreType.DMA((_NCH,))],
    )(cs)


def _swap_with_sibling(v, name):
    R, W = v.shape
    rc = R // _NCH

    def body(s_ref, o_ref, ssem, rsem):
        x, y, c = lax.axis_index("x"), lax.axis_index("y"), lax.axis_index("c")
        cps = []
        for q in range(_NCH):
            r = pl.ds(q * rc, rc)
            cp = _rcopy(s_ref.at[r], o_ref.at[r], ssem.at[q], rsem.at[q], (x, y, 1 - c))
            cp.start()
            cps.append(cp)
        for cp in cps:
            cp.wait()

    return pl.pallas_call(
        body, name=name, out_shape=jax.ShapeDtypeStruct((R, W), v.dtype), in_specs=[_HBM], out_specs=_HBM,
        scratch_shapes=[pltpu.SemaphoreType.DMA((_NCH,))] * 2,
    )(v)


def _mm(a, b, name, li=None, trans_b=False, out_dtype=_F32, tm=1024, tn=1024, tk=2048):
    M, K = a.shape
    bs = b.shape[-2:]
    N = bs[0] if trans_b else bs[1]
    tm, tn, tk = _pick(M, tm, 8), _pick(N, tn), _pick(K, tk)
    nk = K // tk
    lead = () if li is None else (None,)

    def bmap(i, j, k):
        idx = (j, k) if trans_b else (k, j)
        return idx if li is None else (li,) + idx

    def body(a_ref, b_ref, o_ref, acc):
        k = pl.program_id(2)
        part = lax.dot_general(a_ref[...], b_ref[...], _NT if trans_b else _NN, preferred_element_type=_F32)
        if nk == 1:
            o_ref[...] = part.astype(o_ref.dtype)
        else:
            @pl.when(k == 0)
            def _():
                acc[...] = part

            @pl.when(k > 0)
            def _():
                acc[...] += part

            @pl.when(k == nk - 1)
            def _():
                o_ref[...] = acc[...].astype(o_ref.dtype)

    return pl.pallas_call(
        body, name=name, grid=(M // tm, N // tn, nk),
        in_specs=[pl.BlockSpec((tm, tk), lambda i, j, k: (i, k)),
                  pl.BlockSpec(lead + ((tn, tk) if trans_b else (tk, tn)), bmap)],
        out_specs=pl.BlockSpec((tm, tn), lambda i, j, k: (i, j)),
        out_shape=jax.ShapeDtypeStruct((M, N), out_dtype),
        scratch_shapes=[pltpu.VMEM((tm, tn) if nk > 1 else (8, _LANE), _F32)],
        compiler_params=_cp("parallel", "parallel", "arbitrary"),
    )(a, b)


def _mm_tn(a, b, name, li, nl, into=None, tm=512, tn=1024):
    T, M = a.shape
    N = b.shape[1]
    tm, tn = _pick(M, tm), _pick(N, tn)

    def body(a_ref, b_ref, *rest):
        o_ref = rest[-1]
        o_ref[...] = lax.dot_general(a_ref[...], b_ref[...], _TN, preferred_element_type=_F32).astype(o_ref.dtype)

    ins = [pl.BlockSpec((T, tm), lambda i, j: (0, i)), pl.BlockSpec((T, tn), lambda i, j: (0, j))]
    return pl.pallas_call(
        body, name=name, grid=(M // tm, N // tn),
        in_specs=ins if into is None else ins + [_HBM],
        out_specs=pl.BlockSpec((None, tm, tn), lambda i, j: (li, i, j)),
        out_shape=jax.ShapeDtypeStruct((nl, M, N), _MMT),
        input_output_aliases={} if into is None else {2: 0},
        compiler_params=_cp("parallel", "parallel"),
    )(*((a, b) if into is None else (a, b, into)))


def _mm_res(a, b, x, gt, name, tm=1024, tn=1024, tk=2048):
    M, K = a.shape
    N = b.shape[1]
    tm, tn, tk = _pick(M, tm, 8), _pick(N, tn), _pick(K, tk)
    nk = K // tk

    def body(a_ref, b_ref, x_ref, gt_ref, p_ref, o_ref):
        k = pl.program_id(2)
        part = lax.dot_general(a_ref[...], b_ref[...], _NN, preferred_element_type=_F32)

        @pl.when(k == 0)
        def _():
            p_ref[...] = part

        @pl.when(k > 0)
        def _():
            p_ref[...] += part

        @pl.when(k == nk - 1)
        def _():
            o_ref[...] = x_ref[...] + gt_ref[...] * p_ref[...]

    tile = pl.BlockSpec((tm, tn), lambda i, j, k: (i, j))
    return pl.pallas_call(
        body, name=name, grid=(M // tm, N // tn, nk),
        in_specs=[pl.BlockSpec((tm, tk), lambda i, j, k: (i, k)), pl.BlockSpec((tk, tn), lambda i, j, k: (k, j)),
                  tile, pl.BlockSpec((1, tn), lambda i, j, k: (0, j))],
        out_specs=[tile, tile], out_shape=[jax.ShapeDtypeStruct((M, N), _F32)] * 2,
        compiler_params=_cp("parallel", "parallel", "arbitrary"),
    )(a, b, x, gt)


def _ffin_swiglu(a, wt, name, tm=1024, tn=1408):
    M, K = a.shape
    F = wt.shape[0] // 2
    tm, tn = _pick(M, tm, 8), _pick(F, tn)
    nj = F // tn

    def body(a_ref, bg_ref, bu_ref, g_ref, u_ref, act_ref):
        av = a_ref[...]
        g = lax.dot_general(av, bg_ref[...], _NT, preferred_element_type=_F32)
        u = lax.dot_general(av, bu_ref[...], _NT, preferred_element_type=_F32)
        g_ref[...] = g.astype(g_ref.dtype)
        u_ref[...] = u.astype(u_ref.dtype)
        act_ref[...] = (_silu(g) * u).astype(act_ref.dtype)

    tile = pl.BlockSpec((tm, tn), lambda i, j: (i, j))
    return pl.pallas_call(
        body, name=name, grid=(M // tm, nj),
        in_specs=[pl.BlockSpec((tm, K), lambda i, j: (i, 0)), pl.BlockSpec((tn, K), lambda i, j: (j, 0)),
                  pl.BlockSpec((tn, K), lambda i, j: (j + nj, 0))],
        out_specs=[tile] * 3, out_shape=[jax.ShapeDtypeStruct((M, F), _MMT)] * 3,
        compiler_params=_cp("parallel", "parallel"),
    )(a, wt, wt)


def _branch_merge(ya, yb, w_a, w_b, proj, name, tm=1024, tn=512):
    M, K = ya.shape
    N = w_a.shape[1]
    tm, tn = _pick(M, tm, 8), _pick(N, tn)
    nj = N // tn

    def body(ya_ref, yb_ref, wa_ref, wb_ref, ga_ref, gb_ref, pa_ref, pb_ref, m_ref):
        pa = lax.dot_general(ya_ref[...], wa_ref[...], _NN, preferred_element_type=_F32)
        pb = lax.dot_general(yb_ref[...], wb_ref[...], _NN, preferred_element_type=_F32)
        pa_ref[...] = pa.astype(pa_ref.dtype)
        pb_ref[...] = pb.astype(pb_ref.dtype)
        m_ref[...] = (_sigmoid(ga_ref[...]) * pa + _sigmoid(gb_ref[...]) * pb).astype(m_ref.dtype)

    row = pl.BlockSpec((tm, K), lambda i, j: (i, 0))
    col = pl.BlockSpec((K, tn), lambda i, j: (0, j))
    tile = pl.BlockSpec((tm, tn), lambda i, j: (i, j))
    return pl.pallas_call(
        body, name=name, grid=(M // tm, nj),
        in_specs=[row, row, col, col, pl.BlockSpec((tm, tn), lambda i, j: (i, 6 * nj + j)),
                  pl.BlockSpec((tm, tn), lambda i, j: (i, 7 * nj + j))],
        out_specs=[tile] * 3, out_shape=[jax.ShapeDtypeStruct((M, N), _MMT)] * 3,
        compiler_params=_cp("parallel", "parallel"),
    )(ya, yb, w_a, w_b, proj, proj)


def _ada_fwd(c_all, ada_w, name):
    L, D, Ws = ada_w.shape
    B = c_all.shape[0]

    def body(c_ref, w_ref, o_ref):
        o_ref[...] = _dot(_silu(c_ref[...]), w_ref[...])

    return pl.pallas_call(
        body, name=name, grid=(L,),
        in_specs=[_fspec((B, D)), pl.BlockSpec((None, D, Ws), lambda l: (l, 0, 0))],
        out_specs=pl.BlockSpec((None, B, Ws), lambda l: (l, 0, 0)),
        out_shape=jax.ShapeDtypeStruct((L, B, Ws), _F32), compiler_params=_cp("parallel"),
    )(c_all, ada_w)


def _ada_bwd(c_all_t, dmod, name):
    D, B = c_all_t.shape
    L, _, Ws = dmod.shape

    def body(c_ref, d_ref, o_ref):
        ct = _silu(c_ref[...])
        d = d_ref[...]
        acc = ct[:, 0:1] * d[0:1, :]
        for b in range(1, B):
            acc = acc + ct[:, b:b + 1] * d[b:b + 1, :]
        o_ref[...] = acc

    return pl.pallas_call(
        body, name=name, grid=(L,),
        in_specs=[_fspec((D, B)), pl.BlockSpec((None, B, Ws), lambda l: (l, 0, 0))],
        out_specs=pl.BlockSpec((None, D, Ws), lambda l: (l, 0, 0)),
        out_shape=jax.ShapeDtypeStruct((L, D, Ws), _F32), compiler_params=_cp("parallel"),
    )(c_all_t, dmod)


def _norm_mod(x, g, sc, sh, name, tb=512):
    T, D = x.shape
    tb = _pick(T, tb, 8)

    def body(x_ref, g_ref, sc_ref, sh_ref, h_ref):
        xv = x_ref[...]
        r = lax.rsqrt(jnp.mean(xv * xv, axis=1, keepdims=True) + _EPS)
        h_ref[...] = (xv * r * (g_ref[...] * (1.0 + sc_ref[...])) + sh_ref[...]).astype(h_ref.dtype)

    return pl.pallas_call(
        body, name=name, grid=(T // tb,),
        in_specs=[_rspec(tb, D), _fspec((1, D)), _fspec((1, D)), _fspec((1, D))],
        out_specs=_rspec(tb, D), out_shape=jax.ShapeDtypeStruct((T, D), _MMT), compiler_params=_cp("parallel"),
    )(x, g, sc, sh)


def _norm_mod_bwd(x, dh, dres, g, sc, name, tb=256):
    T, D = x.shape
    tb = _pick(T, tb, 8)

    def body(x_ref, dh_ref, dr_ref, g_ref, sc_ref, dx_ref, dgm_ref, dsh_ref):
        i = pl.program_id(0)
        xv, dh_ = x_ref[...], dh_ref[...]
        r = lax.rsqrt(jnp.mean(xv * xv, axis=1, keepdims=True) + _EPS)
        xn = xv * r
        dxn = dh_ * (g_ref[...] * (1.0 + sc_ref[...]))
        dx_ref[...] = dr_ref[...] + r * (dxn - xn * jnp.mean(dxn * xn, axis=1, keepdims=True))

        @pl.when(i == 0)
        def _():
            dgm_ref[...] = jnp.zeros_like(dgm_ref)
            dsh_ref[...] = jnp.zeros_like(dsh_ref)

        dgm_ref[...] += jnp.sum(dh_ * xn, axis=0, keepdims=True)
        dsh_ref[...] += jnp.sum(dh_, axis=0, keepdims=True)

    return pl.pallas_call(
        body, name=name, grid=(T // tb,),
        in_specs=[_rspec(tb, D), _rspec(tb, D), _rspec(tb, D), _fspec((1, D)), _fspec((1, D))],
        out_specs=[_rspec(tb, D), _fspec((1, D)), _fspec((1, D))],
        out_shape=[jax.ShapeDtypeStruct((T, D), _F32), jax.ShapeDtypeStruct((1, D), _F32),
                   jax.ShapeDtypeStruct((1, D), _F32)],
        compiler_params=_cp("arbitrary"),
    )(x, dh, dres, g, sc)


def _resid_bwd(dx, p, gt, name, tb=512):
    T, D = dx.shape
    tb = _pick(T, tb, 8)

    def body(dx_ref, p_ref, gt_ref, dp_ref, dgt_ref):
        i = pl.program_id(0)
        d = dx_ref[...]
        dp_ref[...] = (d * gt_ref[...]).astype(dp_ref.dtype)

        @pl.when(i == 0)
        def _():
            dgt_ref[...] = jnp.zeros_like(dgt_ref)

        dgt_ref[...] += jnp.sum(d * p_ref[...], axis=0, keepdims=True)

    return pl.pallas_call(
        body, name=name, grid=(T // tb,), in_specs=[_rspec(tb, D), _rspec(tb, D), _fspec((1, D))],
        out_specs=[_rspec(tb, D), _fspec((1, D))],
        out_shape=[jax.ShapeDtypeStruct((T, D), _MMT), jax.ShapeDtypeStruct((1, D), _F32)],
        compiler_params=_cp("arbitrary"),
    )(dx, p, gt)


def _gmlp_chunk(u_raw, v_raw, sw_ref, sbt, gv, G):
    u, v = _gelu(u_raw), _gelu(v_raw)
    ii = lax.broadcasted_iota(jnp.int32, (_AC, _AC), 0)
    jj = lax.broadcasted_iota(jnp.int32, (_AC, _AC), 1)
    out = []
    for gi in range(G):
        sl = slice(gi * _GD, (gi + 1) * _GD)
        vg = v[:, sl]
        r = lax.rsqrt(jnp.mean(vg * vg, axis=1, keepdims=True) + _EPS)
        vhat = vg * r
        W = jnp.where(jj <= ii, sw_ref[gi], 0.0)
        s = _dot(W, vhat * gv[:, sl]) + sbt[:, gi:gi + 1]
        out.append((u[:, sl], s, vhat, r, W))
    return out


def _gmlp_fwd(proj, sw, sbt, gv, li, D, name):
    T = proj.shape[0]
    G = D // _GD

    def body(u_ref, v_ref, sw_ref, sbt_ref, gv_ref, y_ref):
        parts = _gmlp_chunk(u_ref[...], v_ref[...], sw_ref, sbt_ref[...], gv_ref[...], G)
        for gi, (u, s, _, _, _) in enumerate(parts):
            y_ref[:, gi * _GD:(gi + 1) * _GD] = (u * s).astype(y_ref.dtype)

    return pl.pallas_call(
        body, name=name, grid=(T // _AC,),
        in_specs=[_rspec(_AC, D, 0), _rspec(_AC, D, 1), _lspec((G, _AC, _AC), li), _lspec((_AC, G), li),
                  _lspec((1, D), li)],
        out_specs=_rspec(_AC, D), out_shape=jax.ShapeDtypeStruct((T, D), _MMT), compiler_params=_cp("parallel"),
    )(proj, proj, sw, sbt, gv)


def _gmlp_bwd(proj, dy, sw, sbt, gv, li, D, into, name):
    T = proj.shape[0]
    G = D // _GD

    def body(u_ref, v_ref, dy_ref, sw_ref, sbt_ref, gv_ref, _, duv_ref, dsw_ref, dsa_ref, dgv_ref):
        i = pl.program_id(0)

        @pl.when(i == 0)
        def _():
            dsw_ref[...] = jnp.zeros_like(dsw_ref)
            dsa_ref[...] = jnp.zeros_like(dsa_ref)
            dgv_ref[...] = jnp.zeros_like(dgv_ref)

        u_raw, v_raw, dy_, gv_ = u_ref[...], v_ref[...], dy_ref[...].astype(_F32), gv_ref[...]
        parts = _gmlp_chunk(u_raw, v_raw, sw_ref, sbt_ref[...], gv_, G)
        ii = lax.broadcasted_iota(jnp.int32, (_AC, _AC), 0)
        jj = lax.broadcasted_iota(jnp.int32, (_AC, _AC), 1)
        dgu, dgv = _dgelu(u_raw), _dgelu(v_raw)
        for gi, (u, s, vhat, r, W) in enumerate(parts):
            sl = slice(gi * _GD, (gi + 1) * _GD)
            dyg = dy_[:, sl]
            ds = dyg * u
            vn = vhat * gv_[:, sl]
            dsw_ref[gi] += jnp.where(jj <= ii, _dot(ds, vn, _NT), 0.0)
            dsa_ref[:, sl] += ds
            dvn = _dot(W, ds, _TN)
            dgv_ref[:, sl] += jnp.sum(dvn * vhat, axis=0, keepdims=True)
            dvh = dvn * gv_[:, sl]
            dvg = r * (dvh - vhat * jnp.mean(dvh * vhat, axis=1, keepdims=True))
            duv_ref[:, sl] = (dyg * s * dgu[:, sl]).astype(duv_ref.dtype)
            duv_ref[:, D + gi * _GD:D + (gi + 1) * _GD] = (dvg * dgv[:, sl]).astype(duv_ref.dtype)

    return pl.pallas_call(
        body, name=name, grid=(T // _AC,),
        in_specs=[_rspec(_AC, D, 0), _rspec(_AC, D, 1), _rspec(_AC, D), _lspec((G, _AC, _AC), li),
                  _lspec((_AC, G), li), _lspec((1, D), li), _HBM],
        out_specs=[_rspec(_AC, 2 * D), _fspec((G, _AC, _AC)), _fspec((_AC, D)), _fspec((1, D))],
        out_shape=[jax.ShapeDtypeStruct(into.shape, into.dtype), jax.ShapeDtypeStruct((G, _AC, _AC), _F32),
                   jax.ShapeDtypeStruct((_AC, D), _F32), jax.ShapeDtypeStruct((1, D), _F32)],
        input_output_aliases={6: 0}, compiler_params=_cp("arbitrary"),
    )(proj, proj, dy, sw, sbt, gv, into)


def _conv_taps(halo, cur, first):
    tb = cur.shape[0]
    full = jnp.concatenate([jnp.where(first, 0.0, halo), cur], axis=0)
    return [full[8:] if j == _KC - 1 else pltpu.roll(full, _KC - 1 - j, 0)[8:] for j in range(_KC)]


def _prev_spec(tb, w, cb):
    return pl.BlockSpec((8, w), lambda i: (jnp.maximum(i * (tb // 8) - 1, 0), cb))


def _l2_heads(x, H):
    outs, rs = [], []
    for h in range(H):
        xh = x[:, h * _GD:(h + 1) * _GD]
        r = lax.rsqrt(jnp.sum(xh * xh, axis=1, keepdims=True) + _EPS)
        outs.append(xh * r)
        rs.append(r)
    return outs, rs


def _gate_rows(ba, alog_row, dtb_row, H):
    lane = lax.broadcasted_iota(jnp.int32, ba.shape, 1)
    beta = _sigmoid(ba)
    g = -jnp.exp(alog_row) * _softplus(ba + dtb_row)
    return lane, beta, g


def _conv_fwd(proj, cw, alog_row, dtb_row, li, D, name, tb=256):
    T = proj.shape[0]
    H = D // _GD
    tb = _pick(T, tb, 8)
    bac = (8 * D) // _LANE

    def body(q_ref, k_ref, v_ref, qh_ref, kh_ref, vh_ref, ba_ref, cw_ref, al_ref, dtb_ref,
             qo_ref, ko_ref, vo_ref, bg_ref):
        first = pl.program_id(0) == 0
        cw_ = cw_ref[...]
        for idx, (cur, halo, out) in enumerate(((q_ref, qh_ref, qo_ref), (k_ref, kh_ref, ko_ref),
                                                 (v_ref, vh_ref, vo_ref))):
            taps = _conv_taps(halo[...], cur[...], first)
            w = cw_[:, idx * D:(idx + 1) * D]
            cv = taps[0] * w[0:1, :]
            for j in range(1, _KC):
                cv = cv + taps[j] * w[j:j + 1, :]
            act = _silu(cv)
            if idx < 2:
                outs, _ = _l2_heads(act, H)
                for h in range(H):
                    out[:, h * _GD:(h + 1) * _GD] = outs[h]
            else:
                out[...] = act
        lane, beta, g = _gate_rows(ba_ref[...], al_ref[...], dtb_ref[...], H)
        bg_ref[...] = jnp.where(lane < H, beta, jnp.where(lane < 2 * H, g, 0.0))

    return pl.pallas_call(
        body, name=name, grid=(T // tb,),
        in_specs=[_rspec(tb, D, 2), _rspec(tb, D, 3), _rspec(tb, D, 4),
                  _prev_spec(tb, D, 2), _prev_spec(tb, D, 3), _prev_spec(tb, D, 4),
                  _rspec(tb, _LANE, bac), _lspec((_KC, 3 * D), li), _lspec((1, _LANE), li), _lspec((1, _LANE), li)],
        out_specs=[_rspec(tb, D), _rspec(tb, D), _rspec(tb, D), _rspec(tb, _LANE)],
        out_shape=[jax.ShapeDtypeStruct((T, D), _F32)] * 3 + [jax.ShapeDtypeStruct((T, _LANE), _F32)],
        compiler_params=_cp("parallel"),
    )(proj, proj, proj, proj, proj, proj, proj, cw, alog_row, dtb_row)


def _conv_bwd1(proj, dqn, dkn, dvs, dbg, cw, alog_row, dtb_row, li, D, into, name, tb=256):
    T = proj.shape[0]
    H = D // _GD
    tb = _pick(T, tb, 8)
    bac = (8 * D) // _LANE

    def body(q_ref, k_ref, v_ref, qh_ref, kh_ref, vh_ref, ba_ref, dq_ref, dk_ref, dv_ref, dbg_ref,
             cw_ref, al_ref, dtb_ref, _, dc_ref, dba_ref, dcw_ref, dal_ref, ddt_ref):
        i = pl.program_id(0)
        first = i == 0

        @pl.when(first)
        def _():
            dcw_ref[...] = jnp.zeros_like(dcw_ref)
            dal_ref[...] = jnp.zeros_like(dal_ref)
            ddt_ref[...] = jnp.zeros_like(ddt_ref)

        cw_ = cw_ref[...]
        for idx, (cur, halo, dref) in enumerate(((q_ref, qh_ref, dq_ref), (k_ref, kh_ref, dk_ref),
                                                  (v_ref, vh_ref, dv_ref))):
            taps = _conv_taps(halo[...], cur[...], first)
            w = cw_[:, idx * D:(idx + 1) * D]
            cv = taps[0] * w[0:1, :]
            for j in range(1, _KC):
                cv = cv + taps[j] * w[j:j + 1, :]
            dact = dref[...]
            if idx < 2:
                outs, rs = _l2_heads(_silu(cv), H)
                pieces = []
                for h in range(H):
                    dy = dact[:, h * _GD:(h + 1) * _GD]
                    pieces.append(rs[h] * (dy - outs[h] * jnp.sum(dy * outs[h], axis=1, keepdims=True)))
                dact = jnp.concatenate(pieces, axis=1)
            dcv = dact * _dsilu(cv)
            dc_ref[:, idx * D:(idx + 1) * D] = dcv
            for j in range(_KC):
                colsum = _dot(jnp.ones((8, tb), _F32), dcv * taps[j])
                dcw_ref[j:j + 1, idx * D:(idx + 1) * D] += colsum[0:1, :]

        ba = ba_ref[...]
        lane, beta, g = _gate_rows(ba, al_ref[...], dtb_ref[...], H)
        dbg_ = dbg_ref[...]
        is_b, is_a = lane < H, jnp.logical_and(lane >= H, lane < 2 * H)
        da = dbg_ * (-jnp.exp(al_ref[...])) * _sigmoid(ba + dtb_ref[...])
        dba_ref[...] = jnp.where(is_b, dbg_ * beta * (1.0 - beta), jnp.where(is_a, da, 0.0)).astype(dba_ref.dtype)
        dal_ref[...] += jnp.sum(jnp.where(is_a, dbg_ * g, 0.0), axis=0, keepdims=True)
        ddt_ref[...] += jnp.sum(jnp.where(is_a, da, 0.0), axis=0, keepdims=True)

    return pl.pallas_call(
        body, name=name, grid=(T // tb,),
        in_specs=[_rspec(tb, D, 2), _rspec(tb, D, 3), _rspec(tb, D, 4),
                  _prev_spec(tb, D, 2), _prev_spec(tb, D, 3), _prev_spec(tb, D, 4),
                  _rspec(tb, _LANE, bac), _rspec(tb, D), _rspec(tb, D), _rspec(tb, D), _rspec(tb, _LANE),
                  _lspec((_KC, 3 * D), li), _lspec((1, _LANE), li), _lspec((1, _LANE), li), _HBM],
        out_specs=[_rspec(tb, 3 * D), _rspec(tb, _LANE, bac), _fspec((_KC, 3 * D)), _fspec((1, _LANE)),
                   _fspec((1, _LANE))],
        out_shape=[jax.ShapeDtypeStruct((T, 3 * D), _F32), jax.ShapeDtypeStruct(into.shape, into.dtype),
                   jax.ShapeDtypeStruct((_KC, 3 * D), _F32), jax.ShapeDtypeStruct((1, _LANE), _F32),
                   jax.ShapeDtypeStruct((1, _LANE), _F32)],
        input_output_aliases={14: 1}, compiler_params=_cp("arbitrary"),
    )(proj, proj, proj, proj, proj, proj, proj, dqn, dkn, dvs, dbg, cw, alog_row, dtb_row, into)


def _conv_bwd2(dc, cw, li, into, name, tb=256):
    T, W3 = dc.shape
    W = W3 // 3
    tb = _pick(T, tb, 8)
    nb8 = T // 8
    nrow = T // tb

    def body(dc_ref, nx_ref, cw_ref, _, o_ref):
        last = pl.program_id(0) == nrow - 1
        full = jnp.concatenate([dc_ref[...], jnp.where(last, 0.0, nx_ref[...])], axis=0)
        w = cw_ref[...]
        acc = full[:tb] * w[_KC - 1:_KC, :]
        for j in range(_KC - 1):
            sh = _KC - 1 - j
            acc = acc + pltpu.roll(full, tb + 8 - sh, 0)[:tb] * w[j:j + 1, :]
        o_ref[...] = acc.astype(o_ref.dtype)

    return pl.pallas_call(
        body, name=name, grid=(nrow, 3),
        in_specs=[pl.BlockSpec((tb, W), lambda i, j: (i, j)),
                  pl.BlockSpec((8, W), lambda i, j: (jnp.minimum((i + 1) * (tb // 8), nb8 - 1), j)),
                  pl.BlockSpec((None, _KC, W), lambda i, j: (li, 0, j)), _HBM],
        out_specs=pl.BlockSpec((tb, W), lambda i, j: (i, 2 + j)),
        out_shape=jax.ShapeDtypeStruct(into.shape, into.dtype), input_output_aliases={3: 0},
        compiler_params=_cp("parallel", "parallel"),
    )(dc, dc, cw, into)


def _split(a):
    hi = a.astype(_BF)
    return hi, (a - hi.astype(_F32)).astype(_BF)


def _dot3(a, b):
    (ah, al), (bh, bl) = a, b
    f = functools.partial(lax.dot_general, dimension_numbers=_NN, preferred_element_type=_F32)
    return f(ah, bh) + f(ah, bl) + f(al, bh)


def _inv_unit_lower(mats):
    C = mats[0].shape[0]
    ii = lax.broadcasted_iota(jnp.int32, (C, C), 0)
    jj = lax.broadcasted_iota(jnp.int32, (C, C), 1)
    xs = [jnp.where(ii == jj, 1.0, 0.0) - a for a in mats]
    ps = list(mats)
    n = 1
    while 2 * n < C:
        sp = [_split(p) for p in ps]
        ps = [_dot3(s, s) for s in sp]
        sp = [_split(p) for p in ps]
        xs = [x + _dot3(_split(x), s) for x, s in zip(xs, sp)]
        n *= 2
    return xs


def _gdn_chunk(q, k, v, g_row, b_row):
    C = q.shape[0]
    ii = lax.broadcasted_iota(jnp.int32, (C, C), 0)
    jj = lax.broadcasted_iota(jnp.int32, (C, C), 1)
    low, strict, eye = jj <= ii, jj < ii, ii == jj
    g_col = jnp.sum(jnp.where(eye, g_row, 0.0), axis=1, keepdims=True)
    b_col = jnp.sum(jnp.where(eye, b_row, 0.0), axis=1, keepdims=True)
    gam_col = jnp.sum(jnp.where(low, g_row, 0.0), axis=1, keepdims=True)
    gam_row = jnp.sum(jnp.where(jj >= ii, g_col, 0.0), axis=0, keepdims=True)
    gam_last = jnp.sum(g_row, axis=1, keepdims=True)
    decay = jnp.where(low, jnp.exp(jnp.where(low, gam_col - gam_row, 0.0)), 0.0)
    eg = jnp.exp(gam_col)
    ekd = jnp.exp(gam_last - gam_col)
    qs = q * (_GD ** -0.5)
    kb = k * b_col
    kk = _dot(kb, k, _NT)
    qkraw = _dot(qs, k, _NT)
    return dict(low=low, strict=strict, eye=eye, ii=ii, jj=jj, b_col=b_col, decay=decay, eg=eg, ekd=ekd,
                gl=jnp.exp(gam_last), qs=qs, kb=kb, kk=kk, qkraw=qkraw,
                A=jnp.where(strict, kk * decay, 0.0), vb=v * b_col, kbg=kb * eg,
                qk=qkraw * decay, q_dec=qs * eg, k_dec=k * ekd)


def _gdn_fwd(qn, kn, vs, g_r, b_r, name):
    T, D = qn.shape
    H, N, C = D // _GD, T // _BC, _BC
    hb = min(_HB, H)

    def body(q_ref, k_ref, v_ref, g_ref, b_ref, o_ref, s_ref, t_ref, S):
        @pl.when(pl.program_id(1) == 0)
        def _():
            S[...] = jnp.zeros_like(S)

        hs = range(hb)
        sls = [slice(hh * _GD, (hh + 1) * _GD) for hh in hs]
        cms = [_gdn_chunk(q_ref[:, sl], k_ref[:, sl], v_ref[:, sl], g_ref[hh], b_ref[hh]) for hh, sl in zip(hs, sls)]
        tms = _inv_unit_lower([cm["A"] for cm in cms])
        us = [_dot(tm, cm["vb"]) for tm, cm in zip(tms, cms)]
        ws = [_dot(tm, cm["kbg"]) for tm, cm in zip(tms, cms)]
        s0s = [S[hh] for hh in hs]
        for hh in hs:
            s_ref[hh] = s0s[hh]
            t_ref[hh] = tms[hh]
        v_news = [u - _dot(w, s0) for u, w, s0 in zip(us, ws, s0s)]
        qss = [_dot(cm["q_dec"], s0) for cm, s0 in zip(cms, s0s)]
        for hh in hs:
            o_ref[:, sls[hh]] = qss[hh] + _dot(cms[hh]["qk"], v_news[hh])
        for hh in hs:
            S[hh] = s0s[hh] * cms[hh]["gl"] + _dot(cms[hh]["k_dec"], v_news[hh], _TN)

    qspec = pl.BlockSpec((C, hb * _GD), lambda h, n: (n, h))
    gspec = pl.BlockSpec((hb, None, 1, C), lambda h, n: (h, n, 0, 0))
    return pl.pallas_call(
        body, name=name, grid=(H // hb, N),
        in_specs=[qspec, qspec, qspec, gspec, gspec],
        out_specs=[qspec, pl.BlockSpec((hb, None, _GD, _GD), lambda h, n: (h, n, 0, 0)),
                   pl.BlockSpec((hb, None, C, C), lambda h, n: (h, n, 0, 0))],
        out_shape=[jax.ShapeDtypeStruct((T, D), _F32), jax.ShapeDtypeStruct((H, N, _GD, _GD), _F32),
                   jax.ShapeDtypeStruct((H, N, C, C), _F32)],
        scratch_shapes=[pltpu.VMEM((hb, _GD, _GD), _F32)],
        compiler_params=_cp("arbitrary", "arbitrary"),
    )(qn, kn, vs, g_r, b_r)


def _gdn_bwd(qn, kn, vs, g_r, b_r, s_all, t_all, do, name):
    T, D = qn.shape
    H, N, C = D // _GD, T // _BC, _BC
    hb = min(_HB, H)

    def body(q_ref, k_ref, v_ref, g_ref, b_ref, s_ref, t_ref, do_ref, dq_ref, dk_ref, dv_ref, dg_ref, db_ref, dS):
        @pl.when(pl.program_id(1) == 0)
        def _():
            dS[...] = jnp.zeros_like(dS)

        hs = range(hb)
        sls = [slice(hh * _GD, (hh + 1) * _GD) for hh in hs]
        ks = [k_ref[:, sl] for sl in sls]
        vs_ = [v_ref[:, sl] for sl in sls]
        cms = [_gdn_chunk(q_ref[:, sl], k, v, g_ref[hh], b_ref[hh]) for hh, sl, k, v in zip(hs, sls, ks, vs_)]
        low, strict, eye, ii, jj = (cms[0][n] for n in ("low", "strict", "eye", "ii", "jj"))
        tms, s0s, dos, ds1s = [t_ref[hh] for hh in hs], [s_ref[hh] for hh in hs], [do_ref[:, sl] for sl in sls], \
            [dS[hh] for hh in hs]
        us = [_dot(tm, cm["vb"]) for tm, cm in zip(tms, cms)]
        ws = [_dot(tm, cm["kbg"]) for tm, cm in zip(tms, cms)]
        v_news = [u - _dot(w, s0) for u, w, s0 in zip(us, ws, s0s)]
        dv_news = [_dot(cm["qk"], do_, _TN) + _dot(cm["k_dec"], ds1) for cm, do_, ds1 in zip(cms, dos, ds1s)]
        dqks = [jnp.where(low, _dot(do_, vn, _NT), 0.0) for do_, vn in zip(dos, v_news)]
        dq_decs = [_dot(do_, s0, _NT) for do_, s0 in zip(dos, s0s)]
        dk_decs = [_dot(vn, ds1, _NT) for vn, ds1 in zip(v_news, ds1s)]
        dgls = [jnp.sum(jnp.sum(ds1 * s0, axis=1, keepdims=True), axis=0, keepdims=True) for ds1, s0 in zip(ds1s, s0s)]
        dws = [-_dot(dvn, s0, _NT) for dvn, s0 in zip(dv_news, s0s)]
        for hh in hs:
            dS[hh] = (_dot(cms[hh]["q_dec"], dos[hh], _TN) + cms[hh]["gl"] * ds1s[hh]
                      - _dot(ws[hh], dv_news[hh], _TN))
        dvbs = [_dot(tm, dvn, _TN) for tm, dvn in zip(tms, dv_news)]
        dkbgs = [_dot(tm, dw, _TN) for tm, dw in zip(tms, dws)]
        dAs = [-jnp.where(strict, _dot(dvb, u, _NT) + _dot(dkbg, w, _NT), 0.0)
               for dvb, u, dkbg, w in zip(dvbs, us, dkbgs, ws)]
        dkks = [dA * cm["decay"] for dA, cm in zip(dAs, cms)]
        dqkraws = [dqk * cm["decay"] for dqk, cm in zip(dqks, cms)]
        Es = [(dA * cm["kk"] + dqk * cm["qkraw"]) * cm["decay"] for dA, dqk, cm in zip(dAs, dqks, cms)]
        dkbs = [_dot(dkk, k) + dkbg * cm["eg"] for dkk, k, dkbg, cm in zip(dkks, ks, dkbgs, cms)]
        dqss = [_dot(dqr, k) + dqd * cm["eg"] for dqr, k, dqd, cm in zip(dqkraws, ks, dq_decs, cms)]
        for hh in hs:
            cm = cms[hh]
            dk_ref[:, sls[hh]] = (_dot(dqkraws[hh], cm["qs"], _TN) + _dot(dkks[hh], cm["kb"], _TN)
                                  + dk_decs[hh] * cm["ekd"] + dkbs[hh] * cm["b_col"])
            dv_ref[:, sls[hh]] = dvbs[hh] * cm["b_col"]
            dq_ref[:, sls[hh]] = dqss[hh] * (_GD ** -0.5)
        for hh in hs:
            cm, k, E = cms[hh], ks[hh], Es[hh]
            eg, ekd = cm["eg"], cm["ekd"]
            dbeta_col = jnp.sum(dvbs[hh] * vs_[hh] + dkbs[hh] * k, axis=1, keepdims=True)
            t_kd = jnp.sum(dk_decs[hh] * k, axis=1, keepdims=True) * ekd
            c1 = (jnp.sum(E, axis=1, keepdims=True) + jnp.sum(dkbgs[hh] * cm["kb"], axis=1, keepdims=True) * eg
                  + jnp.sum(dq_decs[hh] * cm["qs"], axis=1, keepdims=True) * eg - t_kd)
            r1 = jnp.sum(E, axis=0, keepdims=True)
            dgam_last = jnp.sum(t_kd, axis=0, keepdims=True) + dgls[hh] * cm["gl"]
            dgam_col = c1 - jnp.sum(jnp.where(eye, r1, 0.0), axis=1, keepdims=True)
            dg_ref[hh] = jnp.sum(jnp.where(ii >= jj, dgam_col, 0.0), axis=0, keepdims=True) + dgam_last
            db_ref[hh] = jnp.sum(jnp.where(eye, dbeta_col, 0.0), axis=0, keepdims=True)

    qspec = pl.BlockSpec((C, hb * _GD), lambda h, n: (N - 1 - n, h))
    gspec = pl.BlockSpec((hb, None, 1, C), lambda h, n: (h, N - 1 - n, 0, 0))
    return pl.pallas_call(
        body, name=name, grid=(H // hb, N),
        in_specs=[qspec, qspec, qspec, gspec, gspec,
                  pl.BlockSpec((hb, None, _GD, _GD), lambda h, n: (h, N - 1 - n, 0, 0)),
                  pl.BlockSpec((hb, None, C, C), lambda h, n: (h, N - 1 - n, 0, 0)), qspec],
        out_specs=[qspec, qspec, qspec, gspec, gspec],
        out_shape=[jax.ShapeDtypeStruct((T, D), _F32)] * 3 + [jax.ShapeDtypeStruct((H, N, 1, C), _F32)] * 2,
        scratch_shapes=[pltpu.VMEM((hb, _GD, _GD), _F32)],
        compiler_params=_cp("arbitrary", "arbitrary"),
    )(qn, kn, vs, g_r, b_r, s_all, t_all, do)


def _onorm_fwd(o, proj, go, li, D, name, tb=512):
    T = o.shape[0]
    H = D // _GD
    tb = _pick(T, tb, 8)

    def body(o_ref, z_ref, go_ref, y_ref):
        ov, zv, g = o_ref[...], z_ref[...], go_ref[...]
        for h in range(H):
            sl = slice(h * _GD, (h + 1) * _GD)
            oh = ov[:, sl]
            r = lax.rsqrt(jnp.mean(oh * oh, axis=1, keepdims=True) + _EPS)
            y_ref[:, sl] = (oh * r * g * _silu(zv[:, sl])).astype(y_ref.dtype)

    return pl.pallas_call(
        body, name=name, grid=(T // tb,), in_specs=[_rspec(tb, D), _rspec(tb, D, 5), _lspec((1, _GD), li)],
        out_specs=_rspec(tb, D), out_shape=jax.ShapeDtypeStruct((T, D), _MMT), compiler_params=_cp("parallel"),
    )(o, proj, go)


def _onorm_bwd(dy, o, proj, go, li, D, into, name, tb=256):
    T = o.shape[0]
    H = D // _GD
    tb = _pick(T, tb, 8)

    def body(dy_ref, o_ref, z_ref, go_ref, _, do_ref, dz_ref, dgo_ref):
        @pl.when(pl.program_id(0) == 0)
        def _():
            dgo_ref[...] = jnp.zeros_like(dgo_ref)

        dyv, ov, zv, g = dy_ref[...].astype(_F32), o_ref[...], z_ref[...], go_ref[...]
        dgo = jnp.zeros((1, _GD), _F32)
        for h in range(H):
            sl = slice(h * _GD, (h + 1) * _GD)
            oh, zh, dyh = ov[:, sl], zv[:, sl], dyv[:, sl]
            r = lax.rsqrt(jnp.mean(oh * oh, axis=1, keepdims=True) + _EPS)
            on = oh * r
            sz = _silu(zh)
            dgo = dgo + jnp.sum(dyh * sz * on, axis=0, keepdims=True)
            don = dyh * sz * g
            do_ref[:, sl] = r * (don - on * jnp.mean(don * on, axis=1, keepdims=True))
            dz_ref[:, sl] = (dyh * on * g * _dsilu(zh)).astype(dz_ref.dtype)
        dgo_ref[...] += dgo

    return pl.pallas_call(
        body, name=name, grid=(T // tb,),
        in_specs=[_rspec(tb, D), _rspec(tb, D), _rspec(tb, D, 5), _lspec((1, _GD), li), _HBM],
        out_specs=[_rspec(tb, D), _rspec(tb, D, 5), _fspec((1, _GD))],
        out_shape=[jax.ShapeDtypeStruct((T, D), _F32), jax.ShapeDtypeStruct(into.shape, into.dtype),
                   jax.ShapeDtypeStruct((1, _GD), _F32)],
        input_output_aliases={4: 1}, compiler_params=_cp("arbitrary"),
    )(dy, o, proj, go, into)


def _merge_bwd(dm, pa, pb, proj, D, name, tb=256):
    T, PW = proj.shape
    tb = _pick(T, tb, 8)

    def body(dm_ref, pa_ref, pb_ref, ga_ref, gb_ref, dpa_ref, dpb_ref, dg_ref):
        d = dm_ref[...].astype(_F32)
        sa, sb = _sigmoid(ga_ref[...]), _sigmoid(gb_ref[...])
        dpa_ref[...] = (d * sa).astype(dpa_ref.dtype)
        dpb_ref[...] = (d * sb).astype(dpb_ref.dtype)
        dg_ref[:, :D] = (d * pa_ref[...].astype(_F32) * sa * (1.0 - sa)).astype(dg_ref.dtype)
        dg_ref[:, D:] = (d * pb_ref[...].astype(_F32) * sb * (1.0 - sb)).astype(dg_ref.dtype)

    return pl.pallas_call(
        body, name=name, grid=(T // tb,),
        in_specs=[_rspec(tb, D), _rspec(tb, D), _rspec(tb, D), _rspec(tb, D, 6), _rspec(tb, D, 7)],
        out_specs=[_rspec(tb, D), _rspec(tb, D), _rspec(tb, 2 * D, 3)],
        out_shape=[jax.ShapeDtypeStruct((T, D), _MMT)] * 2 + [jax.ShapeDtypeStruct((T, PW), _MMT)],
        compiler_params=_cp("parallel"),
    )(dm, pa, pb, proj, proj)


def _swiglu_bwd(da, gate, up, name, tb=256):
    T, F = gate.shape
    F2 = 2 * F
    tb = _pick(T, tb, 8)

    def body(da_ref, g_ref, u_ref, o_ref):
        d, g = da_ref[...].astype(_F32), g_ref[...].astype(_F32)
        o_ref[:, :F] = (d * u_ref[...].astype(_F32) * _dsilu(g)).astype(o_ref.dtype)
        o_ref[:, F:] = (d * _silu(g)).astype(o_ref.dtype)

    return pl.pallas_call(
        body, name=name, grid=(T // tb,), in_specs=[_rspec(tb, F), _rspec(tb, F), _rspec(tb, F)],
        out_specs=_rspec(tb, F2), out_shape=jax.ShapeDtypeStruct((T, F2), _MMT), compiler_params=_cp("parallel"),
    )(da, gate, up)


def _loss_head(x, tgt, fg, name, tb=256):
    T, D = x.shape
    tb = _pick(T, tb, 8)

    def body(x_ref, t_ref, fg_ref, loss_ref, dx_ref, dfg_ref):
        @pl.when(pl.program_id(0) == 0)
        def _():
            loss_ref[...] = jnp.zeros_like(loss_ref)
            dfg_ref[...] = jnp.zeros_like(dfg_ref)

        xv, fg_ = x_ref[...], fg_ref[...]
        r = lax.rsqrt(jnp.mean(xv * xv, axis=1, keepdims=True) + _EPS)
        xn = xv * r
        e = xn * fg_ - t_ref[...]
        loss_ref[...] += (0.5 / D) * jnp.sum(jnp.sum(e * e, axis=1, keepdims=True), axis=0, keepdims=True)
        dy = e * (1.0 / D)
        dfg_ref[...] += jnp.sum(dy * xn, axis=0, keepdims=True)
        dxn = dy * fg_
        dx_ref[...] = r * (dxn - xn * jnp.mean(dxn * xn, axis=1, keepdims=True))

    return pl.pallas_call(
        body, name=name, grid=(T // tb,), in_specs=[_rspec(tb, D), _rspec(tb, D), _fspec((1, D))],
        out_specs=[_fspec((1, 1)), _rspec(tb, D), _fspec((1, D))],
        out_shape=[jax.ShapeDtypeStruct((1, 1), _F32), jax.ShapeDtypeStruct((T, D), _F32),
                   jax.ShapeDtypeStruct((1, D), _F32)],
        compiler_params=_cp("arbitrary"),
    )(x, tgt, fg)


def _row_tile(R, W, budget=1 << 20, unit=8):
    if R * W * 4 <= budget or R % unit:
        return R
    best = unit
    for t in range(unit, R + 1, unit):
        if R % t == 0 and t * W * 4 <= budget:
            best = t
    return best


def _add_own_half(send, got, half, name):
    P, Rp, W = send.shape
    Rh = Rp // 2
    tb = _row_tile(Rh, W, 1 << 21, 16)

    def body(h_ref, a_ref, b_ref, o_ref):
        o_ref[...] = (a_ref[...].astype(_F32) + b_ref[...].astype(_F32)).astype(o_ref.dtype)

    return pl.pallas_call(
        body, name=name,
        grid_spec=pltpu.PrefetchScalarGridSpec(
            num_scalar_prefetch=1, grid=(P, Rh // tb),
            in_specs=[pl.BlockSpec((None, None, tb, W), lambda k, i, h: (k, h[0], i, 0)),
                      pl.BlockSpec((None, tb, W), lambda k, i, h: (k, i, 0))],
            out_specs=pl.BlockSpec((None, tb, W), lambda k, i, h: (k, i, 0))),
        out_shape=jax.ShapeDtypeStruct((P, Rh, W), send.dtype), compiler_params=_cp("parallel", "parallel"),
    )(half, send.reshape(P, 2, Rh, W), got)


def _sum_slots(st, name):
    P, R, W = st.shape
    tb = _row_tile(R, W, 1 << 20, 16)

    def body(s_ref, o_ref):
        acc = s_ref[0].astype(_F32)
        for p in range(1, P):
            acc = acc + s_ref[p].astype(_F32)
        o_ref[...] = acc

    return pl.pallas_call(
        body, name=name, grid=(R // tb,), in_specs=[pl.BlockSpec((P, tb, W), lambda i: (0, i, 0))],
        out_specs=_rspec(tb, W), out_shape=jax.ShapeDtypeStruct((R, W), _F32), compiler_params=_cp("parallel"),
    )(st)


def _adamw(w, gst, m, v, name):
    R, W = w.shape
    P = gst.shape[0]
    tb = _row_tile(R, W, 1 << 20)
    c1, c2 = 1.0 - _B1 ** _STEP, 1.0 - _B2 ** _STEP

    def body(w_ref, g_ref, m_ref, v_ref, go_ref, d_ref, mo_ref, vo_ref):
        g = g_ref[0]
        for p in range(1, P):
            g = g + g_ref[p]
        mn = _B1 * m_ref[...] + (1.0 - _B1) * g
        vn = _B2 * v_ref[...] + (1.0 - _B2) * (g * g)
        go_ref[...] = g
        mo_ref[...] = mn
        vo_ref[...] = vn
        d_ref[...] = -_LR * ((mn / c1) / (jnp.sqrt(vn / c2) + _AEPS) + _WD * w_ref[...])

    spec = _rspec(tb, W)
    return pl.pallas_call(
        body, name=name, grid=(R // tb,),
        in_specs=[spec, pl.BlockSpec((P, tb, W), lambda i: (0, i, 0)), spec, spec],
        out_specs=[spec] * 4, out_shape=[jax.ShapeDtypeStruct((R, W), _F32)] * 4, compiler_params=_cp("parallel"),
    )(w, gst, m, v)


def _as2d(a):
    if a.ndim == 1:
        return a.reshape(1, -1)
    return a.reshape(-1, a.shape[-1])


def kernel(x, c, ada_w, ada_b, norm1_g, w_in, conv_w, spatial_w, spatial_b, v_norm_g, a_log, dt_bias, o_norm_g, w_branch_a, w_branch_b, w_out, norm2_g, w_ffn_in, w_ffn_out, final_g, loss_target, m_ada_w, m_ada_b, m_norm1_g, m_w_in, m_conv_w, m_spatial_w, m_spatial_b, m_v_norm_g, m_a_log, m_dt_bias, m_o_norm_g, m_w_branch_a, m_w_branch_b, m_w_out, m_norm2_g, m_w_ffn_in, m_w_ffn_out, m_final_g, v_ada_w, v_ada_b, v_norm1_g, v_w_in, v_conv_w, v_spatial_w, v_spatial_b, v_v_norm_g, v_a_log, v_dt_bias, v_o_norm_g, v_w_branch_a, v_w_branch_b, v_w_out, v_norm2_g, v_w_ffn_in, v_w_ffn_out, v_final_g):
    xb, tgt = x[0], loss_target[0]
    T, D = xb.shape
    L, H, G = ada_w.shape[0], a_log.shape[1], spatial_w.shape[1]
    F = 4 * w_ffn_out.shape[1]
    N = T // _BC
    Ws = ada_w.shape[2]
    Wc = w_in.shape[2]
    PW = 8 * D + _LANE
    ix, iy, ic = lax.axis_index("x"), lax.axis_index("y"), lax.axis_index("c")
    me = 4 * ix + 2 * iy + ic

    c_all = _gather8(c, "gather_c").reshape(8, D)
    modp = _ada_fwd(c_all, ada_w, "ada_fwd")
    n_mod, n_cw = L * 8 * Ws, L * _KC * conv_w.shape[2]
    pad = (-(n_mod + n_cw)) % _LANE
    pay = jnp.concatenate([modp.reshape(-1), conv_w.reshape(-1), jnp.zeros((pad,), _F32)]).reshape(-1, _LANE)
    pay_all = _gather8(pay, "gather_mod").reshape(8, -1)
    mod_full = jnp.concatenate([pay_all[2 * k, :n_mod].reshape(L, 8, Ws) for k in range(4)], axis=-1)
    cw_full = jnp.concatenate([pay_all[2 * k, n_mod:n_mod + n_cw].reshape(L, _KC, -1) for k in range(4)], axis=-1)
    mod = lax.dynamic_index_in_dim(mod_full, me, axis=1, keepdims=False) + ada_b
    mods = [[mod[l, j * D:(j + 1) * D].reshape(1, D) for j in range(6)] for l in range(L)]

    big = [w_in, w_branch_a, w_branch_b, w_out, w_ffn_in, w_ffn_out]
    chip = 2 * ix + iy
    starts = [(k * Wc) // 16 * 16 for k in range(4)]
    Hh = max(-(-((k + 1) * Wc) // 16) * 16 - starts[k] for k in range(4))
    No = max(s + Hh for s in starts)
    my_off = jnp.asarray([k * Wc - starts[k] for k in range(4)], jnp.int32)[chip]
    cuts = sorted(set(starts + [s + Hh for s in starts]))

    pers = [Hh, D // 4, D // 4, D // 4, 2 * F // 4, F // 4]
    roff = [0]
    for p in pers:
        roff.append(roff[-1] + L * p)
    Rp = -(-roff[-1] // (32 * _NCH)) * (32 * _NCH)
    rpad = Rp - roff[-1]

    hull = lax.dynamic_update_slice(jnp.zeros((L, Hh, D), _F32), jnp.transpose(w_in, (0, 2, 1)), (0, my_off, 0))
    shard = jnp.concatenate(
        [hull.reshape(-1, D).astype(_MMT), w_branch_a.reshape(-1, D).astype(_MMT),
         w_branch_b.reshape(-1, D).astype(_MMT), w_out.reshape(-1, D).astype(_MMT),
         jnp.transpose(w_ffn_in, (0, 2, 1)).reshape(-1, D).astype(_MMT), w_ffn_out.reshape(-1, D).astype(_MMT),
         jnp.zeros((rpad, D), _MMT)], axis=0)
    gw = _fill_from_sibling(_gather_chips(shard, "gather_w"), "gather_w_sib")

    def slab(i, l, k):
        a = roff[i] + l * pers[i]
        return gw[k, a:a + pers[i]]

    def joined(i, l):
        return jnp.concatenate([slab(i, l, k) for k in range(4)], axis=0)

    def orig_rows(hulls, a, b):
        edges = sorted(set([a, b] + [c_ for c_ in cuts if a < c_ < b]))
        out = []
        for lo, hi in zip(edges[:-1], edges[1:]):
            cov = [k for k in range(4) if starts[k] <= lo and hi <= starts[k] + Hh]
            piece = hulls[cov[0]][lo - starts[cov[0]]:hi - starts[cov[0]]]
            for k in cov[1:]:
                piece = piece + hulls[k][lo - starts[k]:hi - starts[k]]
            out.append(piece)
        return out

    wt_in_p = []
    for l in range(L):
        hulls = [slab(0, l, k) for k in range(4)]
        wt_in_p.append(jnp.concatenate(
            orig_rows(hulls, 0, 6 * D) + orig_rows(hulls, 6 * D + 2 * H, 8 * D + 2 * H)
            + orig_rows(hulls, 6 * D, 6 * D + 2 * H) + [jnp.zeros((_LANE - 2 * H, D), _MMT)], axis=0))
    w_a, w_b, w_o, wt_fi, w_fo = ([joined(i, l) for l in range(L)] for i in range(1, 6))

    sbt = jnp.transpose(spatial_b, (0, 2, 1))
    gv3, go3 = v_norm_g.reshape(L, 1, D), o_norm_g.reshape(L, 1, _GD)
    zpad = jnp.zeros((L, _LANE - 2 * H), _F32)
    alog_row = jnp.concatenate([jnp.zeros((L, H), _F32), a_log, zpad], axis=1).reshape(L, 1, _LANE)
    dtb_row = jnp.concatenate([jnp.zeros((L, H), _F32), dt_bias, zpad], axis=1).reshape(L, 1, _LANE)

    def rows_of(tok):
        return jnp.transpose(tok.reshape(N, _BC, H), (2, 0, 1)).reshape(H, N, 1, _BC)

    def toks_of(rows):
        return jnp.transpose(rows.reshape(H, N, _BC), (1, 2, 0)).reshape(T, H)

    saved = []
    xc = xb
    for l in range(L):
        sh1, sc1, gt1, sh2, sc2, gt2 = mods[l]
        g1, g2 = norm1_g[l].reshape(1, D), norm2_g[l].reshape(1, D)
        h = _norm_mod(xc, g1, sc1, sh1, f"norm1_{l}")
        proj = _mm(h, wt_in_p[l], f"proj_{l}", trans_b=True, tn=1664)
        ya = _gmlp_fwd(proj, spatial_w, sbt, gv3, l, D, f"gmlp_{l}")
        qn, kn, vs, bg = _conv_fwd(proj, cw_full, alog_row, dtb_row, l, D, f"conv_{l}")
        g_r, b_r = rows_of(bg[:, H:2 * H]), rows_of(bg[:, :H])
        o, s_all, t_all = _gdn_fwd(qn, kn, vs, g_r, b_r, f"gdn_{l}")
        yb = _onorm_fwd(o, proj, go3, l, D, f"onorm_{l}")
        pa, pb, mg = _branch_merge(ya, yb, w_a[l], w_b[l], proj, f"branch_{l}")
        p1, x1 = _mm_res(mg, w_o[l], xc, gt1, f"wout_{l}")
        h2 = _norm_mod(x1, g2, sc2, sh2, f"norm2_{l}")
        gate, up, act = _ffin_swiglu(h2, wt_fi[l], f"ffin_{l}")
        p2, x2 = _mm_res(act, w_fo[l], x1, gt2, f"ffout_{l}")
        saved.append(dict(x=xc, h=h, proj=proj, ya=ya, yb=yb, qn=qn, kn=kn, vs=vs, g_r=g_r, b_r=b_r, o=o,
                          s_all=s_all, t_all=t_all, pa=pa, pb=pb, mg=mg, p1=p1, x1=x1, h2=h2, gate=gate, up=up,
                          act=act, p2=p2))
        xc = x2

    loss11, dx, dfg = _loss_head(xc, tgt, final_g.reshape(1, D), "loss_head")
    loss = lax.psum(loss11[0, 0], ("x", "y", "c"))

    gbig = {k: None for k in ("w_in", "w_a", "w_b", "w_o", "w_fi", "w_fo")}
    small = {k: [None] * L for k in ("dmod", "n1", "n2", "sw", "sb", "gv", "cw", "al", "dt", "go")}
    for l in reversed(range(L)):
        sv = saved[l]
        sh1, sc1, gt1, sh2, sc2, gt2 = mods[l]
        g1, g2 = norm1_g[l].reshape(1, D), norm2_g[l].reshape(1, D)
        proj = sv["proj"]
        dp2, dgt2 = _resid_bwd(dx, sv["p2"], gt2, f"res2b_{l}")
        da = _mm(dp2, w_fo[l], f"ffoutb_{l}", trans_b=True, out_dtype=_MMT)
        gbig["w_fo"] = _mm_tn(sv["act"], dp2, f"ffoutw_{l}", l, L, gbig["w_fo"])
        dgu = _swiglu_bwd(da, sv["gate"], sv["up"], f"swiglub_{l}")
        dh2 = _mm(dgu, wt_fi[l], f"ffinb_{l}")
        gbig["w_fi"] = _mm_tn(dgu, sv["h2"], f"ffinw_{l}", l, L, gbig["w_fi"])
        dx1, dgm2, dsh2 = _norm_mod_bwd(sv["x1"], dh2, dx, g2, sc2, f"norm2b_{l}")
        dp1, dgt1 = _resid_bwd(dx1, sv["p1"], gt1, f"res1b_{l}")
        dmg = _mm(dp1, w_o[l], f"woutb_{l}", trans_b=True, out_dtype=_MMT)
        gbig["w_o"] = _mm_tn(sv["mg"], dp1, f"woutw_{l}", l, L, gbig["w_o"])
        dpa, dpb, dproj = _merge_bwd(dmg, sv["pa"], sv["pb"], proj, D, f"mergeb_{l}")
        dya = _mm(dpa, w_a[l], f"brab_{l}", trans_b=True, out_dtype=_MMT)
        gbig["w_a"] = _mm_tn(sv["ya"], dpa, f"braw_{l}", l, L, gbig["w_a"])
        dyb = _mm(dpb, w_b[l], f"brbb_{l}", trans_b=True, out_dtype=_MMT)
        gbig["w_b"] = _mm_tn(sv["yb"], dpb, f"brbw_{l}", l, L, gbig["w_b"])
        dproj, dsw, dsa, dgv = _gmlp_bwd(proj, dya, spatial_w, sbt, gv3, l, D, dproj, f"gmlpb_{l}")
        do, dproj, dgo = _onorm_bwd(dyb, sv["o"], proj, go3, l, D, dproj, f"onormb_{l}")
        dqn, dkn, dvs, dg_r, db_r = _gdn_bwd(sv["qn"], sv["kn"], sv["vs"], sv["g_r"], sv["b_r"], sv["s_all"],
                                             sv["t_all"], do, f"gdnb_{l}")
        dbg = jnp.concatenate([toks_of(db_r), toks_of(dg_r), jnp.zeros((T, _LANE - 2 * H), _F32)], axis=1)
        dc, dproj, dcw, dal, ddt = _conv_bwd1(proj, dqn, dkn, dvs, dbg, cw_full, alog_row, dtb_row, l, D, dproj,
                                              f"convb_{l}")
        dproj = _conv_bwd2(dc, cw_full, l, dproj, f"convx_{l}")
        dh = _mm(dproj, wt_in_p[l], f"projb_{l}", tk=1664)
        gbig["w_in"] = _mm_tn(dproj, sv["h"], f"projw_{l}", l, L, gbig["w_in"], tm=640)
        dx, dgm1, dsh1 = _norm_mod_bwd(sv["x"], dh, dx1, g1, sc1, f"norm1b_{l}")
        small["dmod"][l] = jnp.concatenate([dsh1, dgm1 * g1, dgt1, dsh2, dgm2 * g2, dgt2], axis=1)
        small["n1"][l], small["n2"][l] = dgm1 * (1.0 + sc1), dgm2 * (1.0 + sc2)
        small["sw"][l], small["gv"][l], small["cw"][l], small["go"][l] = dsw, dgv, dcw, dgo
        small["sb"][l] = jnp.transpose(dsa.reshape(_AC, G, _GD).sum(axis=-1))
        small["al"][l], small["dt"][l] = dal[:, H:2 * H], ddt[:, H:2 * H]
    grad_x = dx.reshape(1, T, D)

    names_small = ["dmod", "n1", "n2", "sw", "sb", "gv", "cw", "al", "dt", "go"]
    flat = [jnp.stack(small[k]).reshape(-1) for k in names_small] + [dfg.reshape(-1)]
    sizes = [f.shape[0] for f in flat]
    tot = sum(sizes)
    pad = (-tot) % 1024
    pay = jnp.concatenate(flat + [jnp.zeros((pad,), _F32)]).reshape(-1, 1024)
    sm_all = _gather8(pay, "gather_small").reshape(8, -1)
    offs = [0]
    for s in sizes:
        offs.append(offs[-1] + s)
    part = {k: sm_all[:, offs[i]:offs[i + 1]] for i, k in enumerate(names_small + ["fg"])}
    dmod_all = part["dmod"].reshape(8, L, 6 * D)

    outs = {}

    def update(nm, w, gst, m, v):
        shp = w.shape
        w2 = _as2d(w)
        g, d, mn, vn = _adamw(w2, gst.reshape((gst.shape[0],) + w2.shape), _as2d(m), _as2d(v), f"adamw_{nm}")
        outs[nm] = (g.reshape(shp), d.reshape(shp), mn.reshape(shp), vn.reshape(shp))

    chip = 2 * ix + iy
    dmod_t = jnp.transpose(dmod_all, (1, 0, 2))
    dmod_mine = lax.dynamic_slice_in_dim(dmod_t, chip * Ws, Ws, axis=2)
    g_ada_w = _ada_bwd(jnp.transpose(c_all), dmod_mine, "ada_bwd")
    update("ada_w", ada_w, g_ada_w[None], m_ada_w, v_ada_w)
    update("ada_b", ada_b, dmod_all, m_ada_b, v_ada_b)
    update("norm1_g", norm1_g, part["n1"], m_norm1_g, v_norm1_g)
    update("norm2_g", norm2_g, part["n2"], m_norm2_g, v_norm2_g)
    update("spatial_w", spatial_w, part["sw"], m_spatial_w, v_spatial_w)
    update("spatial_b", spatial_b, part["sb"], m_spatial_b, v_spatial_b)
    update("v_norm_g", v_norm_g, part["gv"], m_v_norm_g, v_v_norm_g)
    update("a_log", a_log, part["al"], m_a_log, v_a_log)
    update("dt_bias", dt_bias, part["dt"], m_dt_bias, v_dt_bias)
    update("o_norm_g", o_norm_g, part["go"], m_o_norm_g, v_o_norm_g)
    update("final_g", final_g, part["fg"], m_final_g, v_final_g)
    cw_cols = conv_w.shape[2]
    dcw_all = part["cw"].reshape(8, L, _KC, 4 * cw_cols)
    update("conv_w", conv_w, lax.dynamic_slice_in_dim(dcw_all, chip * cw_cols, cw_cols, axis=3), m_conv_w, v_conv_w)

    def hull_of(p, k):
        a, b = starts[k], starts[k] + Hh
        out = []
        for lo, hi, plo in ((0, 6 * D, 0), (6 * D, 6 * D + 2 * H, 8 * D), (6 * D + 2 * H, 8 * D + 2 * H, 6 * D),
                            (8 * D + 2 * H, No, None)):
            s, e = max(a, lo), min(b, hi)
            if s < e:
                out.append(jnp.zeros((L, e - s, D), _MMT) if plo is None else p[:, plo + s - lo:plo + e - lo])
        return (out[0] if len(out) == 1 else jnp.concatenate(out, axis=1)).reshape(L * Hh, D)

    pieces = []
    for k in range(4):
        pieces.append(hull_of(gbig["w_in"], k))
        for i, nm in enumerate(("w_a", "w_b", "w_o", "w_fi", "w_fo")):
            per = pers[i + 1]
            pieces.append(gbig[nm][:, k * per:(k + 1) * per].reshape(L * per, D))
        pieces.append(jnp.zeros((rpad, D), _MMT))
    send = jnp.concatenate(pieces, axis=0).reshape(4, Rp, D)
    got = _send_half_to_sibling(send, "reduce_cores")
    chipsum = _add_own_half(send, got, ic.astype(jnp.int32).reshape(1), "add_cores")
    parts = _scatter_to_chips(chipsum, "reduce_chips")
    mine = _sum_slots(parts, "add_chips")
    other = _swap_with_sibling(mine, "swap_cores")
    first = ic == 0
    gsum = jnp.concatenate([jnp.where(first, mine, other), jnp.where(first, other, mine)], axis=0)
    big_names = ["w_in", "w_branch_a", "w_branch_b", "w_out", "w_ffn_in", "w_ffn_out"]
    big_m = [m_w_in, m_w_branch_a, m_w_branch_b, m_w_out, m_w_ffn_in, m_w_ffn_out]
    big_v = [v_w_in, v_w_branch_a, v_w_branch_b, v_w_out, v_w_ffn_in, v_w_ffn_out]
    for i, (nm, w, m, v) in enumerate(zip(big_names, big, big_m, big_v)):
        g = gsum[roff[i]:roff[i + 1]].reshape(L, pers[i], D)
        if i == 0:
            g = jnp.transpose(lax.dynamic_slice_in_dim(g, my_off, Wc, axis=1), (0, 2, 1))
        elif i == 4:
            g = jnp.transpose(g, (0, 2, 1))
        update(nm, w, g[None], m, v)

    order = ["ada_w", "ada_b", "norm1_g", "w_in", "conv_w", "spatial_w", "spatial_b", "v_norm_g", "a_log", "dt_bias",
             "o_norm_g", "w_branch_a", "w_branch_b", "w_out", "norm2_g", "w_ffn_in", "w_ffn_out", "final_g"]
    return (loss, grad_x, *[outs[n][0] for n in order], *[outs[n][1] for n in order],
            *[outs[n][2] for n in order], *[outs[n][3] for n in order])
```

```python
import functools
import math

import jax
import jax.numpy as jnp
from jax import lax
from jax.experimental import pallas as pl
from jax.experimental.pallas import tpu as pltpu

_F32 = jnp.float32
_BF = jnp.bfloat16
_MMT = jnp.bfloat16
_EPS = 1e-6
_GD = 128
_AC = 128
_BC = 64
_KC = 4
_HB = 8
_NCH = 8
_LANE = 128
_VMEM_LIMIT = 56 * 1024 * 1024

_LR, _B1, _B2, _AEPS, _WD, _STEP = 0.001, 0.9, 0.999, 1e-08, 0.01, 10

_NN = (((1,), (0,)), ((), ()))
_NT = (((1,), (1,)), ((), ()))
_TN = (((0,), (0,)), ((), ()))

_MESH = pl.DeviceIdType.MESH


def _cp(*sem):
    return pltpu.CompilerParams(dimension_semantics=tuple(sem), vmem_limit_bytes=_VMEM_LIMIT)


def _dot(a, b, dn=_NN):
    return lax.dot_general(a.astype(_MMT), b.astype(_MMT), dn, preferred_element_type=_F32)


def _pick(n, target, unit=_LANE):
    if n <= target:
        return n
    best = None
    for t in range(unit, target + 1, unit):
        if n % t == 0:
            best = t
    assert best is not None, (n, target)
    return best


def _sigmoid(x):
    return 0.5 * jnp.tanh(0.5 * x) + 0.5


def _silu(x):
    return x * _sigmoid(x)


def _dsilu(x):
    s = _sigmoid(x)
    return s * (1.0 + x * (1.0 - s))


_GK = math.sqrt(2.0 / math.pi)


def _gelu(x):
    return 0.5 * x * (1.0 + jnp.tanh(_GK * (x + 0.044715 * x * x * x)))


def _dgelu(x):
    t = jnp.tanh(_GK * (x + 0.044715 * x * x * x))
    return 0.5 * (1.0 + t) + 0.5 * x * (1.0 - t * t) * _GK * (1.0 + 3.0 * 0.044715 * x * x)


def _softplus(x):
    return jnp.maximum(x, 0.0) + jnp.log(1.0 + jnp.exp(-jnp.abs(x)))


def _rspec(tb, w, cb=0):
    return pl.BlockSpec((tb, w), lambda i: (i, cb))


def _fspec(shape):
    nd = len(shape)
    return pl.BlockSpec(tuple(shape), lambda i: (0,) * nd)


def _lspec(tail, li):
    nd = len(tail)
    return pl.BlockSpec((None,) + tuple(tail), lambda i: (li,) + (0,) * nd)


def _slot_all8(x, y, c):
    return 4 * x + 2 * y + c


def _gather8(v, name):
    R, W = v.shape

    def body(v_ref, o_ref, ssem, rsem, lsem):
        x, y, c = lax.axis_index("x"), lax.axis_index("y"), lax.axis_index("c")
        sib = (x, y, 1 - c)
        chips = _other_chips(x, y)

        def slot(px, py, pc):
            return o_ref.at[_slot_all8(px, py, pc)]

        own = pltpu.make_async_copy(v_ref, slot(x, y, c), lsem)
        own.start()
        started = [_rcopy(v_ref, slot(x, y, c), ssem.at[0], rsem.at[0], sib)]
        started += [_rcopy(v_ref, slot(x, y, c), ssem.at[1 + j], rsem.at[1 + j], (px, py, c))
                    for j, (px, py) in enumerate(chips)]
        for cp in started:
            cp.start()
        for j, (px, py) in enumerate(chips):
            blk = slot(px, py, c)
            _rcopy(blk, blk, ssem.at[1 + j], rsem.at[1 + j], (px, py, c)).wait_recv()
            fw = _rcopy(blk, blk, ssem.at[4 + j], rsem.at[4 + j], sib)
            fw.start()
            started.append(fw)
        blk = slot(x, y, 1 - c)
        _rcopy(blk, blk, ssem.at[0], rsem.at[0], sib).wait_recv()
        for j, (px, py) in enumerate(chips):
            blk = slot(px, py, 1 - c)
            _rcopy(blk, blk, ssem.at[4 + j], rsem.at[4 + j], sib).wait_recv()
        for cp in started:
            cp.wait_send()
        own.wait()

    return pl.pallas_call(
        body, name=name, out_shape=jax.ShapeDtypeStruct((8, R, W), v.dtype), in_specs=[_HBM], out_specs=_HBM,
        scratch_shapes=[pltpu.SemaphoreType.DMA((7,)), pltpu.SemaphoreType.DMA((7,)), pltpu.SemaphoreType.DMA],
    )(v)


def _rcopy(src, dst, ssem, rsem, dev):
    return pltpu.make_async_remote_copy(src_ref=src, dst_ref=dst, send_sem=ssem, recv_sem=rsem,
                                        device_id=dev, device_id_type=_MESH)


def _other_chips(x, y):
    return [(1 - x, y), (x, 1 - y), (1 - x, 1 - y)]


_HBM = pl.BlockSpec(memory_space=pl.ANY)


def _gather_chips(shard, name):
    Rp, W = shard.shape
    Rh = Rp // 2
    rc = Rh // _NCH
    hq = _NCH // 2

    def body(s_ref, o_ref, ssem, rsem, lsem):
        x, y, c = lax.axis_index("x"), lax.axis_index("y"), lax.axis_index("c")
        chip = 2 * x + y
        xn, yn, dg = _other_chips(x, y)
        cx, cy, cd = 2 * xn[0] + xn[1], 2 * yn[0] + yn[1], 2 * dg[0] + dg[1]

        def rows(q):
            return pl.ds(c * Rh + q * rc, rc)

        locs = []
        for q in range(_NCH):
            lc = pltpu.make_async_copy(s_ref.at[rows(q)], o_ref.at[chip, rows(q)], lsem.at[q])
            lc.start()
            locs.append(lc)
        started = []
        for q in range(_NCH):
            for j, nb in ((0, xn), (1, yn)):
                cp = _rcopy(s_ref.at[rows(q)], o_ref.at[chip, rows(q)], ssem.at[j * _NCH + q], rsem.at[j * _NCH + q],
                            (nb[0], nb[1], c))
                cp.start()
                started.append(cp)
        for q in range(_NCH):
            bx = o_ref.at[cx, rows(q)]
            _rcopy(bx, bx, ssem.at[q], rsem.at[q], (xn[0], xn[1], c)).wait_recv()
            if q >= hq:
                rl = _rcopy(bx, bx, ssem.at[2 * _NCH + q], rsem.at[2 * _NCH + q], (yn[0], yn[1], c))
                rl.start()
                started.append(rl)
            by = o_ref.at[cy, rows(q)]
            _rcopy(by, by, ssem.at[_NCH + q], rsem.at[_NCH + q], (yn[0], yn[1], c)).wait_recv()
            if q < hq:
                rl = _rcopy(by, by, ssem.at[2 * _NCH + q], rsem.at[2 * _NCH + q], (xn[0], xn[1], c))
                rl.start()
                started.append(rl)
        for q in range(_NCH):
            bd = o_ref.at[cd, rows(q)]
            _rcopy(bd, bd, ssem.at[2 * _NCH + q], rsem.at[2 * _NCH + q], (dg[0], dg[1], c)).wait_recv()
        for cp in started:
            cp.wait_send()
        for lc in locs:
            lc.wait()

    return pl.pallas_call(
        body, name=name, out_shape=jax.ShapeDtypeStruct((4, Rp, W), shard.dtype), in_specs=[_HBM], out_specs=_HBM,
        scratch_shapes=[pltpu.SemaphoreType.DMA((3 * _NCH,))] * 2 + [pltpu.SemaphoreType.DMA((_NCH,))],
    )(shard)


def _fill_from_sibling(buf, name):
    P, Rp, W = buf.shape
    Rh = Rp // 2
    rc = Rh // _NCH

    def body(s_ref, o_ref, ssem, rsem):
        x, y, c = lax.axis_index("x"), lax.axis_index("y"), lax.axis_index("c")
        cps = []
        for k in range(P):
            for q in range(_NCH):
                r = pl.ds(c * Rh + q * rc, rc)
                cp = _rcopy(s_ref.at[k, r], o_ref.at[k, r], ssem.at[k * _NCH + q], rsem.at[k * _NCH + q],
                            (x, y, 1 - c))
                cp.start()
                cps.append(cp)
        for k in range(P):
            for q in range(_NCH):
                blk = o_ref.at[k, pl.ds((1 - c) * Rh + q * rc, rc)]
                _rcopy(blk, blk, ssem.at[k * _NCH + q], rsem.at[k * _NCH + q], (x, y, 1 - c)).wait_recv()
        for cp in cps:
            cp.wait_send()

    return pl.pallas_call(
        body, name=name, out_shape=jax.ShapeDtypeStruct(buf.shape, buf.dtype), in_specs=[_HBM], out_specs=_HBM,
        scratch_shapes=[pltpu.SemaphoreType.DMA((P * _NCH,))] * 2, input_output_aliases={0: 0},
    )(buf)


def _send_half_to_sibling(send, name):
    P, Rp, W = send.shape
    Rh = Rp // 2
    rc = Rh // _NCH

    def body(s_ref, o_ref, ssem, rsem):
        x, y, c = lax.axis_index("x"), lax.axis_index("y"), lax.axis_index("c")
        cps = []
        for k in range(P):
            for q in range(_NCH):
                cp = _rcopy(s_ref.at[k, pl.ds((1 - c) * Rh + q * rc, rc)], o_ref.at[k, pl.ds(q * rc, rc)],
                            ssem.at[k * _NCH + q], rsem.at[k * _NCH + q], (x, y, 1 - c))
                cp.start()
                cps.append(cp)
        for cp in cps:
            cp.wait()

    return pl.pallas_call(
        body, name=name, out_shape=jax.ShapeDtypeStruct((P, Rh, W), send.dtype), in_specs=[_HBM], out_specs=_HBM,
        scratch_shapes=[pltpu.SemaphoreType.DMA((P * _NCH,))] * 2,
    )(send)


def _scatter_to_chips(cs, name):
    P, Rh, W = cs.shape
    rc = Rh // _NCH

    def body(s_ref, o_ref, ssem, rsem, lsem):
        x, y, c = lax.axis_index("x"), lax.axis_index("y"), lax.axis_index("c")
        chip = 2 * x + y
        peers = _other_chips(x, y)
        locs = []
        for q in range(_NCH):
            r = pl.ds(q * rc, rc)
            lc = pltpu.make_async_copy(s_ref.at[chip, r], o_ref.at[chip, r], lsem.at[q])
            lc.start()
            locs.append(lc)
        cps = []
        for j, (px, py) in enumerate(peers):
            for q in range(_NCH):
                r = pl.ds(q * rc, rc)
                cp = _rcopy(s_ref.at[2 * px + py, r], o_ref.at[chip, r], ssem.at[j * _NCH + q], rsem.at[j * _NCH + q],
                            (px, py, c))
                cp.start()
                cps.append(cp)
        for j, (px, py) in enumerate(peers):
            for q in range(_NCH):
                blk = o_ref.at[2 * px + py, pl.ds(q * rc, rc)]
                _rcopy(blk, blk, ssem.at[j * _NCH + q], rsem.at[j * _NCH + q], (px, py, c)).wait_recv()
        for cp in cps:
            cp.wait_send()
        for lc in locs:
            lc.wait()

    return pl.pallas_call(
        body, name=name, out_shape=jax.ShapeDtypeStruct((P, Rh, W), cs.dtype), in_specs=[_HBM], out_specs=_HBM,
        scratch_shapes=[pltpu.SemaphoreType.DMA((3 * _NCH,))] * 2 + [pltpu.SemaphoreType.DMA((_NCH,))],
    )(cs)


def _swap_with_sibling(v, name):
    R, W = v.shape
    rc = R // _NCH

    def body(s_ref, o_ref, ssem, rsem):
        x, y, c = lax.axis_index("x"), lax.axis_index("y"), lax.axis_index("c")
        cps = []
        for q in range(_NCH):
            r = pl.ds(q * rc, rc)
            cp = _rcopy(s_ref.at[r], o_ref.at[r], ssem.at[q], rsem.at[q], (x, y, 1 - c))
            cp.start()
            cps.append(cp)
        for cp in cps:
            cp.wait()

    return pl.pallas_call(
        body, name=name, out_shape=jax.ShapeDtypeStruct((R, W), v.dtype), in_specs=[_HBM], out_specs=_HBM,
        scratch_shapes=[pltpu.SemaphoreType.DMA((_NCH,))] * 2,
    )(v)


def _mm(a, b, name, li=None, trans_b=False, out_dtype=_F32, tm=1024, tn=1024, tk=2048):
    M, K = a.shape
    bs = b.shape[-2:]
    N = bs[0] if trans_b else bs[1]
    tm, tn, tk = _pick(M, tm, 8), _pick(N, tn), _pick(K, tk)
    nk = K // tk
    lead = () if li is None else (None,)

    def bmap(i, j, k):
        idx = (j, k) if trans_b else (k, j)
        return idx if li is None else (li,) + idx

    def body(a_ref, b_ref, o_ref, acc):
        k = pl.program_id(2)
        part = lax.dot_general(a_ref[...], b_ref[...], _NT if trans_b else _NN, preferred_element_type=_F32)
        if nk == 1:
            o_ref[...] = part.astype(o_ref.dtype)
        else:
            @pl.when(k == 0)
            def _():
                acc[...] = part

            @pl.when(k > 0)
            def _():
                acc[...] += part

            @pl.when(k == nk - 1)
            def _():
                o_ref[...] = acc[...].astype(o_ref.dtype)

    return pl.pallas_call(
        body, name=name, grid=(M // tm, N // tn, nk),
        in_specs=[pl.BlockSpec((tm, tk), lambda i, j, k: (i, k)),
                  pl.BlockSpec(lead + ((tn, tk) if trans_b else (tk, tn)), bmap)],
        out_specs=pl.BlockSpec((tm, tn), lambda i, j, k: (i, j)),
        out_shape=jax.ShapeDtypeStruct((M, N), out_dtype),
        scratch_shapes=[pltpu.VMEM((tm, tn) if nk > 1 else (8, _LANE), _F32)],
        compiler_params=_cp("parallel", "parallel", "arbitrary"),
    )(a, b)


def _mm_tn(a, b, name, li, nl, into=None, tm=512, tn=1024):
    T, M = a.shape
    N = b.shape[1]
    tm, tn = _pick(M, tm), _pick(N, tn)

    def body(a_ref, b_ref, *rest):
        o_ref = rest[-1]
        o_ref[...] = lax.dot_general(a_ref[...], b_ref[...], _TN, preferred_element_type=_F32).astype(o_ref.dtype)

    ins = [pl.BlockSpec((T, tm), lambda i, j: (0, i)), pl.BlockSpec((T, tn), lambda i, j: (0, j))]
    return pl.pallas_call(
        body, name=name, grid=(M // tm, N // tn),
        in_specs=ins if into is None else ins + [_HBM],
        out_specs=pl.BlockSpec((None, tm, tn), lambda i, j: (li, i, j)),
        out_shape=jax.ShapeDtypeStruct((nl, M, N), _MMT),
        input_output_aliases={} if into is None else {2: 0},
        compiler_params=_cp("parallel", "parallel"),
    )(*((a, b) if into is None else (a, b, into)))


def _mm_res(a, b, x, gt, name, tm=1024, tn=1024, tk=2048):
    M, K = a.shape
    N = b.shape[1]
    tm, tn, tk = _pick(M, tm, 8), _pick(N, tn), _pick(K, tk)
    nk = K // tk

    def body(a_ref, b_ref, x_ref, gt_ref, p_ref, o_ref):
        k = pl.program_id(2)
        part = lax.dot_general(a_ref[...], b_ref[...], _NN, preferred_element_type=_F32)

        @pl.when(k == 0)
        def _():
            p_ref[...] = part

        @pl.when(k > 0)
        def _():
            p_ref[...] += part

        @pl.when(k == nk - 1)
        def _():
            o_ref[...] = x_ref[...] + gt_ref[...] * p_ref[...]

    tile = pl.BlockSpec((tm, tn), lambda i, j, k: (i, j))
    return pl.pallas_call(
        body, name=name, grid=(M // tm, N // tn, nk),
        in_specs=[pl.BlockSpec((tm, tk), lambda i, j, k: (i, k)), pl.BlockSpec((tk, tn), lambda i, j, k: (k, j)),
                  tile, pl.BlockSpec((1, tn), lambda i, j, k: (0, j))],
        out_specs=[tile, tile], out_shape=[jax.ShapeDtypeStruct((M, N), _F32)] * 2,
        compiler_params=_cp("parallel", "parallel", "arbitrary"),
    )(a, b, x, gt)


def _mm_normb(a, b, x, dres, g, sc, name, tm=512, tk=2048):
    M, K = a.shape
    D = b.shape[1]
    tm, tk = _pick(M, tm, 8), _pick(K, tk)
    nk = K // tk

    def body(a_ref, b_ref, x_ref, dr_ref, g_ref, sc_ref, dx_ref, dgm_ref, dsh_ref, acc):
        i, k = pl.program_id(0), pl.program_id(1)
        part = lax.dot_general(a_ref[...], b_ref[...], _NN, preferred_element_type=_F32)

        @pl.when(k == 0)
        def _():
            acc[...] = part

        @pl.when(k > 0)
        def _():
            acc[...] += part

        @pl.when(jnp.logical_and(i == 0, k == 0))
        def _():
            dgm_ref[...] = jnp.zeros_like(dgm_ref)
            dsh_ref[...] = jnp.zeros_like(dsh_ref)

        @pl.when(k == nk - 1)
        def _():
            dh_, xv = acc[...], x_ref[...]
            r = lax.rsqrt(jnp.mean(xv * xv, axis=1, keepdims=True) + _EPS)
            xn = xv * r
            dxn = dh_ * (g_ref[...] * (1.0 + sc_ref[...]))
            dx_ref[...] = dr_ref[...] + r * (dxn - xn * jnp.mean(dxn * xn, axis=1, keepdims=True))
            dgm_ref[...] += jnp.sum(dh_ * xn, axis=0, keepdims=True)
            dsh_ref[...] += jnp.sum(dh_, axis=0, keepdims=True)

    row = pl.BlockSpec((tm, D), lambda i, k: (i, 0))
    vec = pl.BlockSpec((1, D), lambda i, k: (0, 0))
    return pl.pallas_call(
        body, name=name, grid=(M // tm, nk),
        in_specs=[pl.BlockSpec((tm, tk), lambda i, k: (i, k)), pl.BlockSpec((tk, D), lambda i, k: (k, 0)),
                  row, row, vec, vec],
        out_specs=[row, vec, vec],
        out_shape=[jax.ShapeDtypeStruct((M, D), _F32), jax.ShapeDtypeStruct((1, D), _F32),
                   jax.ShapeDtypeStruct((1, D), _F32)],
        scratch_shapes=[pltpu.VMEM((tm, D), _F32)],
        compiler_params=_cp("arbitrary", "arbitrary"),
    )(a, b, x, dres, g, sc)


def _ffin_swiglu(a, wt, name, tm=1024, tn=1408):
    M, K = a.shape
    F = wt.shape[0] // 2
    tm, tn = _pick(M, tm, 8), _pick(F, tn)
    nj = F // tn

    def body(a_ref, bg_ref, bu_ref, g_ref, u_ref, act_ref):
        av = a_ref[...]
        g = lax.dot_general(av, bg_ref[...], _NT, preferred_element_type=_F32)
        u = lax.dot_general(av, bu_ref[...], _NT, preferred_element_type=_F32)
        g_ref[...] = g.astype(g_ref.dtype)
        u_ref[...] = u.astype(u_ref.dtype)
        act_ref[...] = (_silu(g) * u).astype(act_ref.dtype)

    tile = pl.BlockSpec((tm, tn), lambda i, j: (i, j))
    return pl.pallas_call(
        body, name=name, grid=(M // tm, nj),
        in_specs=[pl.BlockSpec((tm, K), lambda i, j: (i, 0)), pl.BlockSpec((tn, K), lambda i, j: (j, 0)),
                  pl.BlockSpec((tn, K), lambda i, j: (j + nj, 0))],
        out_specs=[tile] * 3, out_shape=[jax.ShapeDtypeStruct((M, F), _MMT)] * 3,
        compiler_params=_cp("parallel", "parallel"),
    )(a, wt, wt)


def _branch_merge(ya, yb, w_a, w_b, proj, name, tm=1024, tn=512):
    M, K = ya.shape
    N = w_a.shape[1]
    tm, tn = _pick(M, tm, 8), _pick(N, tn)
    nj = N // tn

    def body(ya_ref, yb_ref, wa_ref, wb_ref, ga_ref, gb_ref, pa_ref, pb_ref, m_ref):
        pa = lax.dot_general(ya_ref[...], wa_ref[...], _NN, preferred_element_type=_F32)
        pb = lax.dot_general(yb_ref[...], wb_ref[...], _NN, preferred_element_type=_F32)
        pa_ref[...] = pa.astype(pa_ref.dtype)
        pb_ref[...] = pb.astype(pb_ref.dtype)
        m_ref[...] = (_sigmoid(ga_ref[...]) * pa + _sigmoid(gb_ref[...]) * pb).astype(m_ref.dtype)

    row = pl.BlockSpec((tm, K), lambda i, j: (i, 0))
    col = pl.BlockSpec((K, tn), lambda i, j: (0, j))
    tile = pl.BlockSpec((tm, tn), lambda i, j: (i, j))
    return pl.pallas_call(
        body, name=name, grid=(M // tm, nj),
        in_specs=[row, row, col, col, pl.BlockSpec((tm, tn), lambda i, j: (i, 6 * nj + j)),
                  pl.BlockSpec((tm, tn), lambda i, j: (i, 7 * nj + j))],
        out_specs=[tile] * 3, out_shape=[jax.ShapeDtypeStruct((M, N), _MMT)] * 3,
        compiler_params=_cp("parallel", "parallel"),
    )(ya, yb, w_a, w_b, proj, proj)


def _ada_fwd(c_all, ada_w, name):
    L, D, Ws = ada_w.shape
    B = c_all.shape[0]

    def body(c_ref, w_ref, o_ref):
        o_ref[...] = _dot(_silu(c_ref[...]), w_ref[...])

    return pl.pallas_call(
        body, name=name, grid=(L,),
        in_specs=[_fspec((B, D)), pl.BlockSpec((None, D, Ws), lambda l: (l, 0, 0))],
        out_specs=pl.BlockSpec((None, B, Ws), lambda l: (l, 0, 0)),
        out_shape=jax.ShapeDtypeStruct((L, B, Ws), _F32), compiler_params=_cp("parallel"),
    )(c_all, ada_w)


def _ada_bwd(c_all_t, dmod, name):
    D, B = c_all_t.shape
    L, _, Ws = dmod.shape

    def body(c_ref, d_ref, o_ref):
        ct = _silu(c_ref[...])
        d = d_ref[...]
        acc = ct[:, 0:1] * d[0:1, :]
        for b in range(1, B):
            acc = acc + ct[:, b:b + 1] * d[b:b + 1, :]
        o_ref[...] = acc

    return pl.pallas_call(
        body, name=name, grid=(L,),
        in_specs=[_fspec((D, B)), pl.BlockSpec((None, B, Ws), lambda l: (l, 0, 0))],
        out_specs=pl.BlockSpec((None, D, Ws), lambda l: (l, 0, 0)),
        out_shape=jax.ShapeDtypeStruct((L, D, Ws), _F32), compiler_params=_cp("parallel"),
    )(c_all_t, dmod)


def _norm_mod(x, g, sc, sh, name, tb=512):
    T, D = x.shape
    tb = _pick(T, tb, 8)

    def body(x_ref, g_ref, sc_ref, sh_ref, h_ref):
        xv = x_ref[...]
        r = lax.rsqrt(jnp.mean(xv * xv, axis=1, keepdims=True) + _EPS)
        h_ref[...] = (xv * r * (g_ref[...] * (1.0 + sc_ref[...])) + sh_ref[...]).astype(h_ref.dtype)

    return pl.pallas_call(
        body, name=name, grid=(T // tb,),
        in_specs=[_rspec(tb, D), _fspec((1, D)), _fspec((1, D)), _fspec((1, D))],
        out_specs=_rspec(tb, D), out_shape=jax.ShapeDtypeStruct((T, D), _MMT), compiler_params=_cp("parallel"),
    )(x, g, sc, sh)


def _resid_bwd(dx, p, gt, name, tb=512):
    T, D = dx.shape
    tb = _pick(T, tb, 8)

    def body(dx_ref, p_ref, gt_ref, dp_ref, dgt_ref):
        i = pl.program_id(0)
        d = dx_ref[...]
        dp_ref[...] = (d * gt_ref[...]).astype(dp_ref.dtype)

        @pl.when(i == 0)
        def _():
            dgt_ref[...] = jnp.zeros_like(dgt_ref)

        dgt_ref[...] += jnp.sum(d * p_ref[...], axis=0, keepdims=True)

    return pl.pallas_call(
        body, name=name, grid=(T // tb,), in_specs=[_rspec(tb, D), _rspec(tb, D), _fspec((1, D))],
        out_specs=[_rspec(tb, D), _fspec((1, D))],
        out_shape=[jax.ShapeDtypeStruct((T, D), _MMT), jax.ShapeDtypeStruct((1, D), _F32)],
        compiler_params=_cp("arbitrary"),
    )(dx, p, gt)


def _gmlp_chunk(u_raw, v_raw, sw_ref, sbt, gv, G):
    u, v = _gelu(u_raw), _gelu(v_raw)
    ii = lax.broadcasted_iota(jnp.int32, (_AC, _AC), 0)
    jj = lax.broadcasted_iota(jnp.int32, (_AC, _AC), 1)
    out = []
    for gi in range(G):
        sl = slice(gi * _GD, (gi + 1) * _GD)
        vg = v[:, sl]
        r = lax.rsqrt(jnp.mean(vg * vg, axis=1, keepdims=True) + _EPS)
        vhat = vg * r
        W = jnp.where(jj <= ii, sw_ref[gi], 0.0)
        s = _dot(W, vhat * gv[:, sl]) + sbt[:, gi:gi + 1]
        out.append((u[:, sl], s, vhat, r, W))
    return out


def _gmlp_fwd(proj, sw, sbt, gv, li, D, name):
    T = proj.shape[0]
    G = D // _GD

    def body(u_ref, v_ref, sw_ref, sbt_ref, gv_ref, y_ref):
        parts = _gmlp_chunk(u_ref[...], v_ref[...], sw_ref, sbt_ref[...], gv_ref[...], G)
        for gi, (u, s, _, _, _) in enumerate(parts):
            y_ref[:, gi * _GD:(gi + 1) * _GD] = (u * s).astype(y_ref.dtype)

    return pl.pallas_call(
        body, name=name, grid=(T // _AC,),
        in_specs=[_rspec(_AC, D, 0), _rspec(_AC, D, 1), _lspec((G, _AC, _AC), li), _lspec((_AC, G), li),
                  _lspec((1, D), li)],
        out_specs=_rspec(_AC, D), out_shape=jax.ShapeDtypeStruct((T, D), _MMT), compiler_params=_cp("parallel"),
    )(proj, proj, sw, sbt, gv)


def _gmlp_bwd(proj, dy, sw, sbt, gv, li, D, into, name):
    T = proj.shape[0]
    G = D // _GD

    def body(u_ref, v_ref, dy_ref, sw_ref, sbt_ref, gv_ref, _, duv_ref, dsw_ref, dsa_ref, dgv_ref):
        i = pl.program_id(0)

        @pl.when(i == 0)
        def _():
            dsw_ref[...] = jnp.zeros_like(dsw_ref)
            dsa_ref[...] = jnp.zeros_like(dsa_ref)
            dgv_ref[...] = jnp.zeros_like(dgv_ref)

        u_raw, v_raw, dy_, gv_ = u_ref[...], v_ref[...], dy_ref[...].astype(_F32), gv_ref[...]
        parts = _gmlp_chunk(u_raw, v_raw, sw_ref, sbt_ref[...], gv_, G)
        ii = lax.broadcasted_iota(jnp.int32, (_AC, _AC), 0)
        jj = lax.broadcasted_iota(jnp.int32, (_AC, _AC), 1)
        dgu, dgv = _dgelu(u_raw), _dgelu(v_raw)
        for gi, (u, s, vhat, r, W) in enumerate(parts):
            sl = slice(gi * _GD, (gi + 1) * _GD)
            dyg = dy_[:, sl]
            ds = dyg * u
            vn = vhat * gv_[:, sl]
            dsw_ref[gi] += jnp.where(jj <= ii, _dot(ds, vn, _NT), 0.0)
            dsa_ref[:, sl] += ds
            dvn = _dot(W, ds, _TN)
            dgv_ref[:, sl] += jnp.sum(dvn * vhat, axis=0, keepdims=True)
            dvh = dvn * gv_[:, sl]
            dvg = r * (dvh - vhat * jnp.mean(dvh * vhat, axis=1, keepdims=True))
            duv_ref[:, sl] = (dyg * s * dgu[:, sl]).astype(duv_ref.dtype)
            duv_ref[:, D + gi * _GD:D + (gi + 1) * _GD] = (dvg * dgv[:, sl]).astype(duv_ref.dtype)

    return pl.pallas_call(
        body, name=name, grid=(T // _AC,),
        in_specs=[_rspec(_AC, D, 0), _rspec(_AC, D, 1), _rspec(_AC, D), _lspec((G, _AC, _AC), li),
                  _lspec((_AC, G), li), _lspec((1, D), li), _HBM],
        out_specs=[_rspec(_AC, 2 * D), _fspec((G, _AC, _AC)), _fspec((_AC, D)), _fspec((1, D))],
        out_shape=[jax.ShapeDtypeStruct(into.shape, into.dtype), jax.ShapeDtypeStruct((G, _AC, _AC), _F32),
                   jax.ShapeDtypeStruct((_AC, D), _F32), jax.ShapeDtypeStruct((1, D), _F32)],
        input_output_aliases={6: 0}, compiler_params=_cp("arbitrary"),
    )(proj, proj, dy, sw, sbt, gv, into)


def _conv_taps(halo, cur, first):
    tb = cur.shape[0]
    full = jnp.concatenate([jnp.where(first, 0.0, halo), cur], axis=0)
    return [full[8:] if j == _KC - 1 else pltpu.roll(full, _KC - 1 - j, 0)[8:] for j in range(_KC)]


def _prev_spec(tb, w, cb):
    return pl.BlockSpec((8, w), lambda i: (jnp.maximum(i * (tb // 8) - 1, 0), cb))


def _l2_heads(x, H):
    outs, rs = [], []
    for h in range(H):
        xh = x[:, h * _GD:(h + 1) * _GD]
        r = lax.rsqrt(jnp.sum(xh * xh, axis=1, keepdims=True) + _EPS)
        outs.append(xh * r)
        rs.append(r)
    return outs, rs


def _gate_rows(ba, alog_row, dtb_row, H):
    lane = lax.broadcasted_iota(jnp.int32, ba.shape, 1)
    beta = _sigmoid(ba)
    g = -jnp.exp(alog_row) * _softplus(ba + dtb_row)
    return lane, beta, g


def _conv_fwd(proj, cw, alog_row, dtb_row, li, D, name, tb=256):
    T = proj.shape[0]
    H = D // _GD
    tb = _pick(T, tb, 8)
    bac = (8 * D) // _LANE

    def body(q_ref, k_ref, v_ref, qh_ref, kh_ref, vh_ref, ba_ref, cw_ref, al_ref, dtb_ref,
             qo_ref, ko_ref, vo_ref, bg_ref):
        first = pl.program_id(0) == 0
        cw_ = cw_ref[...]
        for idx, (cur, halo, out) in enumerate(((q_ref, qh_ref, qo_ref), (k_ref, kh_ref, ko_ref),
                                                 (v_ref, vh_ref, vo_ref))):
            taps = _conv_taps(halo[...], cur[...], first)
            w = cw_[:, idx * D:(idx + 1) * D]
            cv = taps[0] * w[0:1, :]
            for j in range(1, _KC):
                cv = cv + taps[j] * w[j:j + 1, :]
            act = _silu(cv)
            if idx < 2:
                outs, _ = _l2_heads(act, H)
                for h in range(H):
                    out[:, h * _GD:(h + 1) * _GD] = outs[h]
            else:
                out[...] = act
        lane, beta, g = _gate_rows(ba_ref[...], al_ref[...], dtb_ref[...], H)
        bg_ref[...] = jnp.where(lane < H, beta, jnp.where(lane < 2 * H, g, 0.0))

    return pl.pallas_call(
        body, name=name, grid=(T // tb,),
        in_specs=[_rspec(tb, D, 2), _rspec(tb, D, 3), _rspec(tb, D, 4),
                  _prev_spec(tb, D, 2), _prev_spec(tb, D, 3), _prev_spec(tb, D, 4),
                  _rspec(tb, _LANE, bac), _lspec((_KC, 3 * D), li), _lspec((1, _LANE), li), _lspec((1, _LANE), li)],
        out_specs=[_rspec(tb, D), _rspec(tb, D), _rspec(tb, D), _rspec(tb, _LANE)],
        out_shape=[jax.ShapeDtypeStruct((T, D), _F32)] * 3 + [jax.ShapeDtypeStruct((T, _LANE), _F32)],
        compiler_params=_cp("parallel"),
    )(proj, proj, proj, proj, proj, proj, proj, cw, alog_row, dtb_row)


def _conv_bwd1(proj, dqn, dkn, dvs, dbg, cw, alog_row, dtb_row, li, D, into, name, tb=256):
    T = proj.shape[0]
    H = D // _GD
    tb = _pick(T, tb, 8)
    bac = (8 * D) // _LANE

    def body(q_ref, k_ref, v_ref, qh_ref, kh_ref, vh_ref, ba_ref, dq_ref, dk_ref, dv_ref, dbg_ref,
             cw_ref, al_ref, dtb_ref, _, dc_ref, dba_ref, dcw_ref, dal_ref, ddt_ref):
        i = pl.program_id(0)
        first = i == 0

        @pl.when(first)
        def _():
            dcw_ref[...] = jnp.zeros_like(dcw_ref)
            dal_ref[...] = jnp.zeros_like(dal_ref)
            ddt_ref[...] = jnp.zeros_like(ddt_ref)

        cw_ = cw_ref[...]
        for idx, (cur, halo, dref) in enumerate(((q_ref, qh_ref, dq_ref), (k_ref, kh_ref, dk_ref),
                                                  (v_ref, vh_ref, dv_ref))):
            taps = _conv_taps(halo[...], cur[...], first)
            w = cw_[:, idx * D:(idx + 1) * D]
            cv = taps[0] * w[0:1, :]
            for j in range(1, _KC):
                cv = cv + taps[j] * w[j:j + 1, :]
            dact = dref[...]
            if idx < 2:
                outs, rs = _l2_heads(_silu(cv), H)
                pieces = []
                for h in range(H):
                    dy = dact[:, h * _GD:(h + 1) * _GD]
                    pieces.append(rs[h] * (dy - outs[h] * jnp.sum(dy * outs[h], axis=1, keepdims=True)))
                dact = jnp.concatenate(pieces, axis=1)
            dcv = dact * _dsilu(cv)
            dc_ref[:, idx * D:(idx + 1) * D] = dcv
            for j in range(_KC):
                colsum = _dot(jnp.ones((8, tb), _F32), dcv * taps[j])
                dcw_ref[j:j + 1, idx * D:(idx + 1) * D] += colsum[0:1, :]

        ba = ba_ref[...]
        lane, beta, g = _gate_rows(ba, al_ref[...], dtb_ref[...], H)
        dbg_ = dbg_ref[...]
        is_b, is_a = lane < H, jnp.logical_and(lane >= H, lane < 2 * H)
        da = dbg_ * (-jnp.exp(al_ref[...])) * _sigmoid(ba + dtb_ref[...])
        dba_ref[...] = jnp.where(is_b, dbg_ * beta * (1.0 - beta), jnp.where(is_a, da, 0.0)).astype(dba_ref.dtype)
        dal_ref[...] += jnp.sum(jnp.where(is_a, dbg_ * g, 0.0), axis=0, keepdims=True)
        ddt_ref[...] += jnp.sum(jnp.where(is_a, da, 0.0), axis=0, keepdims=True)

    return pl.pallas_call(
        body, name=name, grid=(T // tb,),
        in_specs=[_rspec(tb, D, 2), _rspec(tb, D, 3), _rspec(tb, D, 4),
                  _prev_spec(tb, D, 2), _prev_spec(tb, D, 3), _prev_spec(tb, D, 4),
                  _rspec(tb, _LANE, bac), _rspec(tb, D), _rspec(tb, D), _rspec(tb, D), _rspec(tb, _LANE),
                  _lspec((_KC, 3 * D), li), _lspec((1, _LANE), li), _lspec((1, _LANE), li), _HBM],
        out_specs=[_rspec(tb, 3 * D), _rspec(tb, _LANE, bac), _fspec((_KC, 3 * D)), _fspec((1, _LANE)),
                   _fspec((1, _LANE))],
        out_shape=[jax.ShapeDtypeStruct((T, 3 * D), _F32), jax.ShapeDtypeStruct(into.shape, into.dtype),
                   jax.ShapeDtypeStruct((_KC, 3 * D), _F32), jax.ShapeDtypeStruct((1, _LANE), _F32),
                   jax.ShapeDtypeStruct((1, _LANE), _F32)],
        input_output_aliases={14: 1}, compiler_params=_cp("arbitrary"),
    )(proj, proj, proj, proj, proj, proj, proj, dqn, dkn, dvs, dbg, cw, alog_row, dtb_row, into)


def _conv_bwd2(dc, cw, li, into, name, tb=256):
    T, W3 = dc.shape
    W = W3 // 3
    tb = _pick(T, tb, 8)
    nb8 = T // 8
    nrow = T // tb

    def body(dc_ref, nx_ref, cw_ref, _, o_ref):
        last = pl.program_id(0) == nrow - 1
        full = jnp.concatenate([dc_ref[...], jnp.where(last, 0.0, nx_ref[...])], axis=0)
        w = cw_ref[...]
        acc = full[:tb] * w[_KC - 1:_KC, :]
        for j in range(_KC - 1):
            sh = _KC - 1 - j
            acc = acc + pltpu.roll(full, tb + 8 - sh, 0)[:tb] * w[j:j + 1, :]
        o_ref[...] = acc.astype(o_ref.dtype)

    return pl.pallas_call(
        body, name=name, grid=(nrow, 3),
        in_specs=[pl.BlockSpec((tb, W), lambda i, j: (i, j)),
                  pl.BlockSpec((8, W), lambda i, j: (jnp.minimum((i + 1) * (tb // 8), nb8 - 1), j)),
                  pl.BlockSpec((None, _KC, W), lambda i, j: (li, 0, j)), _HBM],
        out_specs=pl.BlockSpec((tb, W), lambda i, j: (i, 2 + j)),
        out_shape=jax.ShapeDtypeStruct(into.shape, into.dtype), input_output_aliases={3: 0},
        compiler_params=_cp("parallel", "parallel"),
    )(dc, dc, cw, into)


def _split(a):
    hi = a.astype(_BF)
    return hi, (a - hi.astype(_F32)).astype(_BF)


def _dot3(a, b):
    (ah, al), (bh, bl) = a, b
    f = functools.partial(lax.dot_general, dimension_numbers=_NN, preferred_element_type=_F32)
    return f(ah, bh) + f(ah, bl) + f(al, bh)


def _inv_unit_lower(mats):
    C = mats[0].shape[0]
    ii = lax.broadcasted_iota(jnp.int32, (C, C), 0)
    jj = lax.broadcasted_iota(jnp.int32, (C, C), 1)
    xs = [jnp.where(ii == jj, 1.0, 0.0) - a for a in mats]
    ps = list(mats)
    n = 1
    while 2 * n < C:
        sp = [_split(p) for p in ps]
        ps = [_dot3(s, s) for s in sp]
        sp = [_split(p) for p in ps]
        xs = [x + _dot3(_split(x), s) for x, s in zip(xs, sp)]
        n *= 2
    return xs


def _gdn_chunk(q, k, v, g_row, b_row):
    C = q.shape[0]
    ii = lax.broadcasted_iota(jnp.int32, (C, C), 0)
    jj = lax.broadcasted_iota(jnp.int32, (C, C), 1)
    low, strict, eye = jj <= ii, jj < ii, ii == jj
    g_col = jnp.sum(jnp.where(eye, g_row, 0.0), axis=1, keepdims=True)
    b_col = jnp.sum(jnp.where(eye, b_row, 0.0), axis=1, keepdims=True)
    gam_col = jnp.sum(jnp.where(low, g_row, 0.0), axis=1, keepdims=True)
    gam_row = jnp.sum(jnp.where(jj >= ii, g_col, 0.0), axis=0, keepdims=True)
    gam_last = jnp.sum(g_row, axis=1, keepdims=True)
    decay = jnp.where(low, jnp.exp(jnp.where(low, gam_col - gam_row, 0.0)), 0.0)
    eg = jnp.exp(gam_col)
    ekd = jnp.exp(gam_last - gam_col)
    qs = q * (_GD ** -0.5)
    kb = k * b_col
    kk = _dot(kb, k, _NT)
    qkraw = _dot(qs, k, _NT)
    return dict(low=low, strict=strict, eye=eye, ii=ii, jj=jj, b_col=b_col, decay=decay, eg=eg, ekd=ekd,
                gl=jnp.exp(gam_last), qs=qs, kb=kb, kk=kk, qkraw=qkraw,
                A=jnp.where(strict, kk * decay, 0.0), vb=v * b_col, kbg=kb * eg,
                qk=qkraw * decay, q_dec=qs * eg, k_dec=k * ekd)


def _gdn_fwd(qn, kn, vs, g_r, b_r, name):
    T, D = qn.shape
    H, N, C = D // _GD, T // _BC, _BC
    hb = min(_HB, H)

    def body(q_ref, k_ref, v_ref, g_ref, b_ref, o_ref, s_ref, t_ref, S):
        @pl.when(pl.program_id(1) == 0)
        def _():
            S[...] = jnp.zeros_like(S)

        hs = range(hb)
        sls = [slice(hh * _GD, (hh + 1) * _GD) for hh in hs]
        cms = [_gdn_chunk(q_ref[:, sl], k_ref[:, sl], v_ref[:, sl], g_ref[hh], b_ref[hh]) for hh, sl in zip(hs, sls)]
        tms = _inv_unit_lower([cm["A"] for cm in cms])
        us = [_dot(tm, cm["vb"]) for tm, cm in zip(tms, cms)]
        ws = [_dot(tm, cm["kbg"]) for tm, cm in zip(tms, cms)]
        s0s = [S[hh] for hh in hs]
        for hh in hs:
            s_ref[hh] = s0s[hh]
            t_ref[hh] = tms[hh]
        v_news = [u - _dot(w, s0) for u, w, s0 in zip(us, ws, s0s)]
        qss = [_dot(cm["q_dec"], s0) for cm, s0 in zip(cms, s0s)]
        for hh in hs:
            o_ref[:, sls[hh]] = qss[hh] + _dot(cms[hh]["qk"], v_news[hh])
        for hh in hs:
            S[hh] = s0s[hh] * cms[hh]["gl"] + _dot(cms[hh]["k_dec"], v_news[hh], _TN)

    qspec = pl.BlockSpec((C, hb * _GD), lambda h, n: (n, h))
    gspec = pl.BlockSpec((hb, None, 1, C), lambda h, n: (h, n, 0, 0))
    return pl.pallas_call(
        body, name=name, grid=(H // hb, N),
        in_specs=[qspec, qspec, qspec, gspec, gspec],
        out_specs=[qspec, pl.BlockSpec((hb, None, _GD, _GD), lambda h, n: (h, n, 0, 0)),
                   pl.BlockSpec((hb, None, C, C), lambda h, n: (h, n, 0, 0))],
        out_shape=[jax.ShapeDtypeStruct((T, D), _F32), jax.ShapeDtypeStruct((H, N, _GD, _GD), _F32),
                   jax.ShapeDtypeStruct((H, N, C, C), _F32)],
        scratch_shapes=[pltpu.VMEM((hb, _GD, _GD), _F32)],
        compiler_params=_cp("arbitrary", "arbitrary"),
    )(qn, kn, vs, g_r, b_r)


def _gdn_bwd(qn, kn, vs, g_r, b_r, s_all, t_all, do, name):
    T, D = qn.shape
    H, N, C = D // _GD, T // _BC, _BC
    hb = min(_HB, H)

    def body(q_ref, k_ref, v_ref, g_ref, b_ref, s_ref, t_ref, do_ref, dq_ref, dk_ref, dv_ref, dg_ref, db_ref, dS):
        @pl.when(pl.program_id(1) == 0)
        def _():
            dS[...] = jnp.zeros_like(dS)

        hs = range(hb)
        sls = [slice(hh * _GD, (hh + 1) * _GD) for hh in hs]
        ks = [k_ref[:, sl] for sl in sls]
        vs_ = [v_ref[:, sl] for sl in sls]
        cms = [_gdn_chunk(q_ref[:, sl], k, v, g_ref[hh], b_ref[hh]) for hh, sl, k, v in zip(hs, sls, ks, vs_)]
        low, strict, eye, ii, jj = (cms[0][n] for n in ("low", "strict", "eye", "ii", "jj"))
        tms, s0s, dos, ds1s = [t_ref[hh] for hh in hs], [s_ref[hh] for hh in hs], [do_ref[:, sl] for sl in sls], \
            [dS[hh] for hh in hs]
        us = [_dot(tm, cm["vb"]) for tm, cm in zip(tms, cms)]
        ws = [_dot(tm, cm["kbg"]) for tm, cm in zip(tms, cms)]
        v_news = [u - _dot(w, s0) for u, w, s0 in zip(us, ws, s0s)]
        dv_news = [_dot(cm["qk"], do_, _TN) + _dot(cm["k_dec"], ds1) for cm, do_, ds1 in zip(cms, dos, ds1s)]
        dqks = [jnp.where(low, _dot(do_, vn, _NT), 0.0) for do_, vn in zip(dos, v_news)]
        dq_decs = [_dot(do_, s0, _NT) for do_, s0 in zip(dos, s0s)]
        dk_decs = [_dot(vn, ds1, _NT) for vn, ds1 in zip(v_news, ds1s)]
        dgls = [jnp.sum(jnp.sum(ds1 * s0, axis=1, keepdims=True), axis=0, keepdims=True) for ds1, s0 in zip(ds1s, s0s)]
        dws = [-_dot(dvn, s0, _NT) for dvn, s0 in zip(dv_news, s0s)]
        for hh in hs:
            dS[hh] = (_dot(cms[hh]["q_dec"], dos[hh], _TN) + cms[hh]["gl"] * ds1s[hh]
                      - _dot(ws[hh], dv_news[hh], _TN))
        dvbs = [_dot(tm, dvn, _TN) for tm, dvn in zip(tms, dv_news)]
        dkbgs = [_dot(tm, dw, _TN) for tm, dw in zip(tms, dws)]
        dAs = [-jnp.where(strict, _dot(dvb, u, _NT) + _dot(dkbg, w, _NT), 0.0)
               for dvb, u, dkbg, w in zip(dvbs, us, dkbgs, ws)]
        dkks = [dA * cm["decay"] for dA, cm in zip(dAs, cms)]
        dqkraws = [dqk * cm["decay"] for dqk, cm in zip(dqks, cms)]
        Es = [(dA * cm["kk"] + dqk * cm["qkraw"]) * cm["decay"] for dA, dqk, cm in zip(dAs, dqks, cms)]
        dkbs = [_dot(dkk, k) + dkbg * cm["eg"] for dkk, k, dkbg, cm in zip(dkks, ks, dkbgs, cms)]
        dqss = [_dot(dqr, k) + dqd * cm["eg"] for dqr, k, dqd, cm in zip(dqkraws, ks, dq_decs, cms)]
        for hh in hs:
            cm = cms[hh]
            dk_ref[:, sls[hh]] = (_dot(dqkraws[hh], cm["qs"], _TN) + _dot(dkks[hh], cm["kb"], _TN)
                                  + dk_decs[hh] * cm["ekd"] + dkbs[hh] * cm["b_col"])
            dv_ref[:, sls[hh]] = dvbs[hh] * cm["b_col"]
            dq_ref[:, sls[hh]] = dqss[hh] * (_GD ** -0.5)
        for hh in hs:
            cm, k, E = cms[hh], ks[hh], Es[hh]
            eg, ekd = cm["eg"], cm["ekd"]
            dbeta_col = jnp.sum(dvbs[hh] * vs_[hh] + dkbs[hh] * k, axis=1, keepdims=True)
            t_kd = jnp.sum(dk_decs[hh] * k, axis=1, keepdims=True) * ekd
            c1 = (jnp.sum(E, axis=1, keepdims=True) + jnp.sum(dkbgs[hh] * cm["kb"], axis=1, keepdims=True) * eg
                  + jnp.sum(dq_decs[hh] * cm["qs"], axis=1, keepdims=True) * eg - t_kd)
            r1 = jnp.sum(E, axis=0, keepdims=True)
            dgam_last = jnp.sum(t_kd, axis=0, keepdims=True) + dgls[hh] * cm["gl"]
            dgam_col = c1 - jnp.sum(jnp.where(eye, r1, 0.0), axis=1, keepdims=True)
            dg_ref[hh] = jnp.sum(jnp.where(ii >= jj, dgam_col, 0.0), axis=0, keepdims=True) + dgam_last
            db_ref[hh] = jnp.sum(jnp.where(eye, dbeta_col, 0.0), axis=0, keepdims=True)

    qspec = pl.BlockSpec((C, hb * _GD), lambda h, n: (N - 1 - n, h))
    gspec = pl.BlockSpec((hb, None, 1, C), lambda h, n: (h, N - 1 - n, 0, 0))
    return pl.pallas_call(
        body, name=name, grid=(H // hb, N),
        in_specs=[qspec, qspec, qspec, gspec, gspec,
                  pl.BlockSpec((hb, None, _GD, _GD), lambda h, n: (h, N - 1 - n, 0, 0)),
                  pl.BlockSpec((hb, None, C, C), lambda h, n: (h, N - 1 - n, 0, 0)), qspec],
        out_specs=[qspec, qspec, qspec, gspec, gspec],
        out_shape=[jax.ShapeDtypeStruct((T, D), _F32)] * 3 + [jax.ShapeDtypeStruct((H, N, 1, C), _F32)] * 2,
        scratch_shapes=[pltpu.VMEM((hb, _GD, _GD), _F32)],
        compiler_params=_cp("arbitrary", "arbitrary"),
    )(qn, kn, vs, g_r, b_r, s_all, t_all, do)


def _onorm_fwd(o, proj, go, li, D, name, tb=512):
    T = o.shape[0]
    H = D // _GD
    tb = _pick(T, tb, 8)

    def body(o_ref, z_ref, go_ref, y_ref):
        ov, zv, g = o_ref[...], z_ref[...], go_ref[...]
        for h in range(H):
            sl = slice(h * _GD, (h + 1) * _GD)
            oh = ov[:, sl]
            r = lax.rsqrt(jnp.mean(oh * oh, axis=1, keepdims=True) + _EPS)
            y_ref[:, sl] = (oh * r * g * _silu(zv[:, sl])).astype(y_ref.dtype)

    return pl.pallas_call(
        body, name=name, grid=(T // tb,), in_specs=[_rspec(tb, D), _rspec(tb, D, 5), _lspec((1, _GD), li)],
        out_specs=_rspec(tb, D), out_shape=jax.ShapeDtypeStruct((T, D), _MMT), compiler_params=_cp("parallel"),
    )(o, proj, go)


def _onorm_bwd(dy, o, proj, go, li, D, into, name, tb=256):
    T = o.shape[0]
    H = D // _GD
    tb = _pick(T, tb, 8)

    def body(dy_ref, o_ref, z_ref, go_ref, _, do_ref, dz_ref, dgo_ref):
        @pl.when(pl.program_id(0) == 0)
        def _():
            dgo_ref[...] = jnp.zeros_like(dgo_ref)

        dyv, ov, zv, g = dy_ref[...].astype(_F32), o_ref[...], z_ref[...], go_ref[...]
        dgo = jnp.zeros((1, _GD), _F32)
        for h in range(H):
            sl = slice(h * _GD, (h + 1) * _GD)
            oh, zh, dyh = ov[:, sl], zv[:, sl], dyv[:, sl]
            r = lax.rsqrt(jnp.mean(oh * oh, axis=1, keepdims=True) + _EPS)
            on = oh * r
            sz = _silu(zh)
            dgo = dgo + jnp.sum(dyh * sz * on, axis=0, keepdims=True)
            don = dyh * sz * g
            do_ref[:, sl] = r * (don - on * jnp.mean(don * on, axis=1, keepdims=True))
            dz_ref[:, sl] = (dyh * on * g * _dsilu(zh)).astype(dz_ref.dtype)
        dgo_ref[...] += dgo

    return pl.pallas_call(
        body, name=name, grid=(T // tb,),
        in_specs=[_rspec(tb, D), _rspec(tb, D), _rspec(tb, D, 5), _lspec((1, _GD), li), _HBM],
        out_specs=[_rspec(tb, D), _rspec(tb, D, 5), _fspec((1, _GD))],
        out_shape=[jax.ShapeDtypeStruct((T, D), _F32), jax.ShapeDtypeStruct(into.shape, into.dtype),
                   jax.ShapeDtypeStruct((1, _GD), _F32)],
        input_output_aliases={4: 1}, compiler_params=_cp("arbitrary"),
    )(dy, o, proj, go, into)


def _merge_bwd(dm, pa, pb, proj, D, name, tb=256):
    T, PW = proj.shape
    tb = _pick(T, tb, 8)

    def body(dm_ref, pa_ref, pb_ref, ga_ref, gb_ref, dpa_ref, dpb_ref, dg_ref):
        d = dm_ref[...].astype(_F32)
        sa, sb = _sigmoid(ga_ref[...]), _sigmoid(gb_ref[...])
        dpa_ref[...] = (d * sa).astype(dpa_ref.dtype)
        dpb_ref[...] = (d * sb).astype(dpb_ref.dtype)
        dg_ref[:, :D] = (d * pa_ref[...].astype(_F32) * sa * (1.0 - sa)).astype(dg_ref.dtype)
        dg_ref[:, D:] = (d * pb_ref[...].astype(_F32) * sb * (1.0 - sb)).astype(dg_ref.dtype)

    return pl.pallas_call(
        body, name=name, grid=(T // tb,),
        in_specs=[_rspec(tb, D), _rspec(tb, D), _rspec(tb, D), _rspec(tb, D, 6), _rspec(tb, D, 7)],
        out_specs=[_rspec(tb, D), _rspec(tb, D), _rspec(tb, 2 * D, 3)],
        out_shape=[jax.ShapeDtypeStruct((T, D), _MMT)] * 2 + [jax.ShapeDtypeStruct((T, PW), _MMT)],
        compiler_params=_cp("parallel"),
    )(dm, pa, pb, proj, proj)


def _swiglu_bwd(da, gate, up, name, tb=256):
    T, F = gate.shape
    F2 = 2 * F
    tb = _pick(T, tb, 8)

    def body(da_ref, g_ref, u_ref, o_ref):
        d, g = da_ref[...].astype(_F32), g_ref[...].astype(_F32)
        o_ref[:, :F] = (d * u_ref[...].astype(_F32) * _dsilu(g)).astype(o_ref.dtype)
        o_ref[:, F:] = (d * _silu(g)).astype(o_ref.dtype)

    return pl.pallas_call(
        body, name=name, grid=(T // tb,), in_specs=[_rspec(tb, F), _rspec(tb, F), _rspec(tb, F)],
        out_specs=_rspec(tb, F2), out_shape=jax.ShapeDtypeStruct((T, F2), _MMT), compiler_params=_cp("parallel"),
    )(da, gate, up)


def _loss_head(x, tgt, fg, name, tb=256):
    T, D = x.shape
    tb = _pick(T, tb, 8)

    def body(x_ref, t_ref, fg_ref, loss_ref, dx_ref, dfg_ref):
        @pl.when(pl.program_id(0) == 0)
        def _():
            loss_ref[...] = jnp.zeros_like(loss_ref)
            dfg_ref[...] = jnp.zeros_like(dfg_ref)

        xv, fg_ = x_ref[...], fg_ref[...]
        r = lax.rsqrt(jnp.mean(xv * xv, axis=1, keepdims=True) + _EPS)
        xn = xv * r
        e = xn * fg_ - t_ref[...]
        loss_ref[...] += (0.5 / D) * jnp.sum(jnp.sum(e * e, axis=1, keepdims=True), axis=0, keepdims=True)
        dy = e * (1.0 / D)
        dfg_ref[...] += jnp.sum(dy * xn, axis=0, keepdims=True)
        dxn = dy * fg_
        dx_ref[...] = r * (dxn - xn * jnp.mean(dxn * xn, axis=1, keepdims=True))

    return pl.pallas_call(
        body, name=name, grid=(T // tb,), in_specs=[_rspec(tb, D), _rspec(tb, D), _fspec((1, D))],
        out_specs=[_fspec((1, 1)), _rspec(tb, D), _fspec((1, D))],
        out_shape=[jax.ShapeDtypeStruct((1, 1), _F32), jax.ShapeDtypeStruct((T, D), _F32),
                   jax.ShapeDtypeStruct((1, D), _F32)],
        compiler_params=_cp("arbitrary"),
    )(x, tgt, fg)


def _row_tile(R, W, budget=1 << 20, unit=8):
    if R * W * 4 <= budget or R % unit:
        return R
    best = unit
    for t in range(unit, R + 1, unit):
        if R % t == 0 and t * W * 4 <= budget:
            best = t
    return best


def _add_own_half(send, got, half, name):
    P, Rp, W = send.shape
    Rh = Rp // 2
    tb = _row_tile(Rh, W, 1 << 21, 16)

    def body(h_ref, a_ref, b_ref, o_ref):
        o_ref[...] = (a_ref[...].astype(_F32) + b_ref[...].astype(_F32)).astype(o_ref.dtype)

    return pl.pallas_call(
        body, name=name,
        grid_spec=pltpu.PrefetchScalarGridSpec(
            num_scalar_prefetch=1, grid=(P, Rh // tb),
            in_specs=[pl.BlockSpec((None, None, tb, W), lambda k, i, h: (k, h[0], i, 0)),
                      pl.BlockSpec((None, tb, W), lambda k, i, h: (k, i, 0))],
            out_specs=pl.BlockSpec((None, tb, W), lambda k, i, h: (k, i, 0))),
        out_shape=jax.ShapeDtypeStruct((P, Rh, W), send.dtype), compiler_params=_cp("parallel", "parallel"),
    )(half, send.reshape(P, 2, Rh, W), got)


def _sum_slots(st, name):
    P, R, W = st.shape
    tb = _row_tile(R, W, 1 << 20, 16)

    def body(s_ref, o_ref):
        acc = s_ref[0].astype(_F32)
        for p in range(1, P):
            acc = acc + s_ref[p].astype(_F32)
        o_ref[...] = acc

    return pl.pallas_call(
        body, name=name, grid=(R // tb,), in_specs=[pl.BlockSpec((P, tb, W), lambda i: (0, i, 0))],
        out_specs=_rspec(tb, W), out_shape=jax.ShapeDtypeStruct((R, W), _F32), compiler_params=_cp("parallel"),
    )(st)


def _adamw(w, gst, m, v, name):
    R, W = w.shape
    P = gst.shape[0]
    tb = _row_tile(R, W, 1 << 20)
    c1, c2 = 1.0 - _B1 ** _STEP, 1.0 - _B2 ** _STEP

    def body(w_ref, g_ref, m_ref, v_ref, go_ref, d_ref, mo_ref, vo_ref):
        g = g_ref[0]
        for p in range(1, P):
            g = g + g_ref[p]
        mn = _B1 * m_ref[...] + (1.0 - _B1) * g
        vn = _B2 * v_ref[...] + (1.0 - _B2) * (g * g)
        go_ref[...] = g
        mo_ref[...] = mn
        vo_ref[...] = vn
        d_ref[...] = -_LR * ((mn / c1) / (jnp.sqrt(vn / c2) + _AEPS) + _WD * w_ref[...])

    spec = _rspec(tb, W)
    return pl.pallas_call(
        body, name=name, grid=(R // tb,),
        in_specs=[spec, pl.BlockSpec((P, tb, W), lambda i: (0, i, 0)), spec, spec],
        out_specs=[spec] * 4, out_shape=[jax.ShapeDtypeStruct((R, W), _F32)] * 4, compiler_params=_cp("parallel"),
    )(w, gst, m, v)


def _as2d(a):
    if a.ndim == 1:
        return a.reshape(1, -1)
    return a.reshape(-1, a.shape[-1])


def kernel(x, c, ada_w, ada_b, norm1_g, w_in, conv_w, spatial_w, spatial_b, v_norm_g, a_log, dt_bias, o_norm_g, w_branch_a, w_branch_b, w_out, norm2_g, w_ffn_in, w_ffn_out, final_g, loss_target, m_ada_w, m_ada_b, m_norm1_g, m_w_in, m_conv_w, m_spatial_w, m_spatial_b, m_v_norm_g, m_a_log, m_dt_bias, m_o_norm_g, m_w_branch_a, m_w_branch_b, m_w_out, m_norm2_g, m_w_ffn_in, m_w_ffn_out, m_final_g, v_ada_w, v_ada_b, v_norm1_g, v_w_in, v_conv_w, v_spatial_w, v_spatial_b, v_v_norm_g, v_a_log, v_dt_bias, v_o_norm_g, v_w_branch_a, v_w_branch_b, v_w_out, v_norm2_g, v_w_ffn_in, v_w_ffn_out, v_final_g):
    xb, tgt = x[0], loss_target[0]
    T, D = xb.shape
    L, H, G = ada_w.shape[0], a_log.shape[1], spatial_w.shape[1]
    F = 4 * w_ffn_out.shape[1]
    N = T // _BC
    Ws = ada_w.shape[2]
    Wc = w_in.shape[2]
    PW = 8 * D + _LANE
    ix, iy, ic = lax.axis_index("x"), lax.axis_index("y"), lax.axis_index("c")
    me = 4 * ix + 2 * iy + ic

    c_all = _gather8(c, "gather_c").reshape(8, D)
    modp = _ada_fwd(c_all, ada_w, "ada_fwd")
    n_mod, n_cw = L * 8 * Ws, L * _KC * conv_w.shape[2]
    pad = (-(n_mod + n_cw)) % _LANE
    pay = jnp.concatenate([modp.reshape(-1), conv_w.reshape(-1), jnp.zeros((pad,), _F32)]).reshape(-1, _LANE)
    pay_all = _gather8(pay, "gather_mod").reshape(8, -1)
    mod_full = jnp.concatenate([pay_all[2 * k, :n_mod].reshape(L, 8, Ws) for k in range(4)], axis=-1)
    cw_full = jnp.concatenate([pay_all[2 * k, n_mod:n_mod + n_cw].reshape(L, _KC, -1) for k in range(4)], axis=-1)
    mod = lax.dynamic_index_in_dim(mod_full, me, axis=1, keepdims=False) + ada_b
    mods = [[mod[l, j * D:(j + 1) * D].reshape(1, D) for j in range(6)] for l in range(L)]

    big = [w_in, w_branch_a, w_branch_b, w_out, w_ffn_in, w_ffn_out]
    chip = 2 * ix + iy
    starts = [(k * Wc) // 16 * 16 for k in range(4)]
    Hh = max(-(-((k + 1) * Wc) // 16) * 16 - starts[k] for k in range(4))
    No = max(s + Hh for s in starts)
    my_off = jnp.asarray([k * Wc - starts[k] for k in range(4)], jnp.int32)[chip]
    cuts = sorted(set(starts + [s + Hh for s in starts]))

    pers = [Hh, D // 4, D // 4, D // 4, 2 * F // 4, F // 4]
    roff = [0]
    for p in pers:
        roff.append(roff[-1] + L * p)
    Rp = -(-roff[-1] // (32 * _NCH)) * (32 * _NCH)
    rpad = Rp - roff[-1]

    hull = lax.dynamic_update_slice(jnp.zeros((L, Hh, D), _F32), jnp.transpose(w_in, (0, 2, 1)), (0, my_off, 0))
    shard = jnp.concatenate(
        [hull.reshape(-1, D).astype(_MMT), w_branch_a.reshape(-1, D).astype(_MMT),
         w_branch_b.reshape(-1, D).astype(_MMT), w_out.reshape(-1, D).astype(_MMT),
         jnp.transpose(w_ffn_in, (0, 2, 1)).reshape(-1, D).astype(_MMT), w_ffn_out.reshape(-1, D).astype(_MMT),
         jnp.zeros((rpad, D), _MMT)], axis=0)
    gw = _fill_from_sibling(_gather_chips(shard, "gather_w"), "gather_w_sib")

    def slab(i, l, k):
        a = roff[i] + l * pers[i]
        return gw[k, a:a + pers[i]]

    def joined(i, l):
        return jnp.concatenate([slab(i, l, k) for k in range(4)], axis=0)

    def orig_rows(hulls, a, b):
        edges = sorted(set([a, b] + [c_ for c_ in cuts if a < c_ < b]))
        out = []
        for lo, hi in zip(edges[:-1], edges[1:]):
            cov = [k for k in range(4) if starts[k] <= lo and hi <= starts[k] + Hh]
            piece = hulls[cov[0]][lo - starts[cov[0]]:hi - starts[cov[0]]]
            for k in cov[1:]:
                piece = piece + hulls[k][lo - starts[k]:hi - starts[k]]
            out.append(piece)
        return out

    wt_in_p = []
    for l in range(L):
        hulls = [slab(0, l, k) for k in range(4)]
        wt_in_p.append(jnp.concatenate(
            orig_rows(hulls, 0, 6 * D) + orig_rows(hulls, 6 * D + 2 * H, 8 * D + 2 * H)
            + orig_rows(hulls, 6 * D, 6 * D + 2 * H) + [jnp.zeros((_LANE - 2 * H, D), _MMT)], axis=0))
    w_a, w_b, w_o, wt_fi, w_fo = ([joined(i, l) for l in range(L)] for i in range(1, 6))

    sbt = jnp.transpose(spatial_b, (0, 2, 1))
    gv3, go3 = v_norm_g.reshape(L, 1, D), o_norm_g.reshape(L, 1, _GD)
    zpad = jnp.zeros((L, _LANE - 2 * H), _F32)
    alog_row = jnp.concatenate([jnp.zeros((L, H), _F32), a_log, zpad], axis=1).reshape(L, 1, _LANE)
    dtb_row = jnp.concatenate([jnp.zeros((L, H), _F32), dt_bias, zpad], axis=1).reshape(L, 1, _LANE)

    def rows_of(tok):
        return jnp.transpose(tok.reshape(N, _BC, H), (2, 0, 1)).reshape(H, N, 1, _BC)

    def toks_of(rows):
        return jnp.transpose(rows.reshape(H, N, _BC), (1, 2, 0)).reshape(T, H)

    saved = []
    xc = xb
    for l in range(L):
        sh1, sc1, gt1, sh2, sc2, gt2 = mods[l]
        g1, g2 = norm1_g[l].reshape(1, D), norm2_g[l].reshape(1, D)
        h = _norm_mod(xc, g1, sc1, sh1, f"norm1_{l}")
        proj = _mm(h, wt_in_p[l], f"proj_{l}", trans_b=True, tn=1664)
        ya = _gmlp_fwd(proj, spatial_w, sbt, gv3, l, D, f"gmlp_{l}")
        qn, kn, vs, bg = _conv_fwd(proj, cw_full, alog_row, dtb_row, l, D, f"conv_{l}")
        g_r, b_r = rows_of(bg[:, H:2 * H]), rows_of(bg[:, :H])
        o, s_all, t_all = _gdn_fwd(qn, kn, vs, g_r, b_r, f"gdn_{l}")
        yb = _onorm_fwd(o, proj, go3, l, D, f"onorm_{l}")
        pa, pb, mg = _branch_merge(ya, yb, w_a[l], w_b[l], proj, f"branch_{l}")
        p1, x1 = _mm_res(mg, w_o[l], xc, gt1, f"wout_{l}")
        h2 = _norm_mod(x1, g2, sc2, sh2, f"norm2_{l}")
        gate, up, act = _ffin_swiglu(h2, wt_fi[l], f"ffin_{l}")
        p2, x2 = _mm_res(act, w_fo[l], x1, gt2, f"ffout_{l}")
        saved.append(dict(x=xc, h=h, proj=proj, ya=ya, yb=yb, qn=qn, kn=kn, vs=vs, g_r=g_r, b_r=b_r, o=o,
                          s_all=s_all, t_all=t_all, pa=pa, pb=pb, mg=mg, p1=p1, x1=x1, h2=h2, gate=gate, up=up,
                          act=act, p2=p2))
        xc = x2

    loss11, dx, dfg = _loss_head(xc, tgt, final_g.reshape(1, D), "loss_head")
    loss = lax.psum(loss11[0, 0], ("x", "y", "c"))

    gbig = {k: None for k in ("w_in", "w_a", "w_b", "w_o", "w_fi", "w_fo")}
    small = {k: [None] * L for k in ("dmod", "n1", "n2", "sw", "sb", "gv", "cw", "al", "dt", "go")}
    for l in reversed(range(L)):
        sv = saved[l]
        sh1, sc1, gt1, sh2, sc2, gt2 = mods[l]
        g1, g2 = norm1_g[l].reshape(1, D), norm2_g[l].reshape(1, D)
        proj = sv["proj"]
        dp2, dgt2 = _resid_bwd(dx, sv["p2"], gt2, f"res2b_{l}")
        da = _mm(dp2, w_fo[l], f"ffoutb_{l}", trans_b=True, out_dtype=_MMT)
        gbig["w_fo"] = _mm_tn(sv["act"], dp2, f"ffoutw_{l}", l, L, gbig["w_fo"])
        dgu = _swiglu_bwd(da, sv["gate"], sv["up"], f"swiglub_{l}")
        dx1, dgm2, dsh2 = _mm_normb(dgu, wt_fi[l], sv["x1"], dx, g2, sc2, f"ffinb_{l}")
        gbig["w_fi"] = _mm_tn(dgu, sv["h2"], f"ffinw_{l}", l, L, gbig["w_fi"])
        dp1, dgt1 = _resid_bwd(dx1, sv["p1"], gt1, f"res1b_{l}")
        dmg = _mm(dp1, w_o[l], f"woutb_{l}", trans_b=True, out_dtype=_MMT)
        gbig["w_o"] = _mm_tn(sv["mg"], dp1, f"woutw_{l}", l, L, gbig["w_o"])
        dpa, dpb, dproj = _merge_bwd(dmg, sv["pa"], sv["pb"], proj, D, f"mergeb_{l}")
        dya = _mm(dpa, w_a[l], f"brab_{l}", trans_b=True, out_dtype=_MMT)
        gbig["w_a"] = _mm_tn(sv["ya"], dpa, f"braw_{l}", l, L, gbig["w_a"])
        dyb = _mm(dpb, w_b[l], f"brbb_{l}", trans_b=True, out_dtype=_MMT)
        gbig["w_b"] = _mm_tn(sv["yb"], dpb, f"brbw_{l}", l, L, gbig["w_b"])
        dproj, dsw, dsa, dgv = _gmlp_bwd(proj, dya, spatial_w, sbt, gv3, l, D, dproj, f"gmlpb_{l}")
        do, dproj, dgo = _onorm_bwd(dyb, sv["o"], proj, go3, l, D, dproj, f"onormb_{l}")
        dqn, dkn, dvs, dg_r, db_r = _gdn_bwd(sv["qn"], sv["kn"], sv["vs"], sv["g_r"], sv["b_r"], sv["s_all"],
                                             sv["t_all"], do, f"gdnb_{l}")
        dbg = jnp.concatenate([toks_of(db_r), toks_of(dg_r), jnp.zeros((T, _LANE - 2 * H), _F32)], axis=1)
        dc, dproj, dcw, dal, ddt = _conv_bwd1(proj, dqn, dkn, dvs, dbg, cw_full, alog_row, dtb_row, l, D, dproj,
                                              f"convb_{l}")
        dproj = _conv_bwd2(dc, cw_full, l, dproj, f"convx_{l}")
        gbig["w_in"] = _mm_tn(dproj, sv["h"], f"projw_{l}", l, L, gbig["w_in"], tm=640)
        dx, dgm1, dsh1 = _mm_normb(dproj, wt_in_p[l], sv["x"], dx1, g1, sc1, f"projb_{l}", tk=1664)
        small["dmod"][l] = jnp.concatenate([dsh1, dgm1 * g1, dgt1, dsh2, dgm2 * g2, dgt2], axis=1)
        small["n1"][l], small["n2"][l] = dgm1 * (1.0 + sc1), dgm2 * (1.0 + sc2)
        small["sw"][l], small["gv"][l], small["cw"][l], small["go"][l] = dsw, dgv, dcw, dgo
        small["sb"][l] = jnp.transpose(dsa.reshape(_AC, G, _GD).sum(axis=-1))
        small["al"][l], small["dt"][l] = dal[:, H:2 * H], ddt[:, H:2 * H]
    grad_x = dx.reshape(1, T, D)

    names_small = ["dmod", "n1", "n2", "sw", "sb", "gv", "cw", "al", "dt", "go"]
    flat = [jnp.stack(small[k]).reshape(-1) for k in names_small] + [dfg.reshape(-1)]
    sizes = [f.shape[0] for f in flat]
    tot = sum(sizes)
    pad = (-tot) % 1024
    pay = jnp.concatenate(flat + [jnp.zeros((pad,), _F32)]).reshape(-1, 1024)
    sm_all = _gather8(pay, "gather_small").reshape(8, -1)
    offs = [0]
    for s in sizes:
        offs.append(offs[-1] + s)
    part = {k: sm_all[:, offs[i]:offs[i + 1]] for i, k in enumerate(names_small + ["fg"])}
    dmod_all = part["dmod"].reshape(8, L, 6 * D)

    outs = {}

    def update(nm, w, gst, m, v):
        shp = w.shape
        w2 = _as2d(w)
        g, d, mn, vn = _adamw(w2, gst.reshape((gst.shape[0],) + w2.shape), _as2d(m), _as2d(v), f"adamw_{nm}")
        outs[nm] = (g.reshape(shp), d.reshape(shp), mn.reshape(shp), vn.reshape(shp))

    chip = 2 * ix + iy
    dmod_t = jnp.transpose(dmod_all, (1, 0, 2))
    dmod_mine = lax.dynamic_slice_in_dim(dmod_t, chip * Ws, Ws, axis=2)
    g_ada_w = _ada_bwd(jnp.transpose(c_all), dmod_mine, "ada_bwd")
    update("ada_w", ada_w, g_ada_w[None], m_ada_w, v_ada_w)
    update("ada_b", ada_b, dmod_all, m_ada_b, v_ada_b)
    update("norm1_g", norm1_g, part["n1"], m_norm1_g, v_norm1_g)
    update("norm2_g", norm2_g, part["n2"], m_norm2_g, v_norm2_g)
    update("spatial_w", spatial_w, part["sw"], m_spatial_w, v_spatial_w)
    update("spatial_b", spatial_b, part["sb"], m_spatial_b, v_spatial_b)
    update("v_norm_g", v_norm_g, part["gv"], m_v_norm_g, v_v_norm_g)
    update("a_log", a_log, part["al"], m_a_log, v_a_log)
    update("dt_bias", dt_bias, part["dt"], m_dt_bias, v_dt_bias)
    update("o_norm_g", o_norm_g, part["go"], m_o_norm_g, v_o_norm_g)
    update("final_g", final_g, part["fg"], m_final_g, v_final_g)
    cw_cols = conv_w.shape[2]
    dcw_all = part["cw"].reshape(8, L, _KC, 4 * cw_cols)
    update("conv_w", conv_w, lax.dynamic_slice_in_dim(dcw_all, chip * cw_cols, cw_cols, axis=3), m_conv_w, v_conv_w)

    def hull_of(p, k):
        a, b = starts[k], starts[k] + Hh
        out = []
        for lo, hi, plo in ((0, 6 * D, 0), (6 * D, 6 * D + 2 * H, 8 * D), (6 * D + 2 * H, 8 * D + 2 * H, 6 * D),
                            (8 * D + 2 * H, No, None)):
            s, e = max(a, lo), min(b, hi)
            if s < e:
                out.append(jnp.zeros((L, e - s, D), _MMT) if plo is None else p[:, plo + s - lo:plo + e - lo])
        return (out[0] if len(out) == 1 else jnp.concatenate(out, axis=1)).reshape(L * Hh, D)

    pieces = []
    for k in range(4):
        pieces.append(hull_of(gbig["w_in"], k))
        for i, nm in enumerate(("w_a", "w_b", "w_o", "w_fi", "w_fo")):
            per = pers[i + 1]
            pieces.append(gbig[nm][:, k * per:(k + 1) * per].reshape(L * per, D))
        pieces.append(jnp.zeros((rpad, D), _MMT))
    send = jnp.concatenate(pieces, axis=0).reshape(4, Rp, D)
    got = _send_half_to_sibling(send, "reduce_cores")
    chipsum = _add_own_half(send, got, ic.astype(jnp.int32).reshape(1), "add_cores")
    parts = _scatter_to_chips(chipsum, "reduce_chips")
    mine = _sum_slots(parts, "add_chips")
    other = _swap_with_sibling(mine, "swap_cores")
    first = ic == 0
    gsum = jnp.concatenate([jnp.where(first, mine, other), jnp.where(first, other, mine)], axis=0)
    big_names = ["w_in", "w_branch_a", "w_branch_b", "w_out", "w_ffn_in", "w_ffn_out"]
    big_m = [m_w_in, m_w_branch_a, m_w_branch_b, m_w_out, m_w_ffn_in, m_w_ffn_out]
    big_v = [v_w_in, v_w_branch_a, v_w_branch_b, v_w_out, v_w_ffn_in, v_w_ffn_out]
    for i, (nm, w, m, v) in enumerate(zip(big_names, big, big_m, big_v)):
        g = gsum[roff[i]:roff[i + 1]].reshape(L, pers[i], D)
        if i == 0:
            g = jnp.transpose(lax.dynamic_slice_in_dim(g, my_off, Wc, axis=1), (0, 2, 1))
        elif i == 4:
            g = jnp.transpose(g, (0, 2, 1))
        update(nm, w, g[None], m, v)

    order = ["ada_w", "ada_b", "norm1_g", "w_in", "conv_w", "spatial_w", "spatial_b", "v_norm_g", "a_log", "dt_bias",
             "o_norm_g", "w_branch_a", "w_branch_b", "w_out", "norm2_g", "w_ffn_in", "w_ffn_out", "final_g"]
    return (loss, grad_x, *[outs[n][0] for n in order], *[outs[n][1] for n in order],
            *[outs[n][2] for n in order], *[outs[n][3] for n in order])
```

```python
import functools
import math

import jax
import jax.numpy as jnp
from jax import lax
from jax.experimental import pallas as pl
from jax.experimental.pallas import tpu as pltpu

_F32 = jnp.float32
_BF = jnp.bfloat16
_MMT = jnp.bfloat16
_EPS = 1e-6
_GD = 128
_AC = 128
_BC = 64
_KC = 4
_HB = 8
_NCH = 8
_LANE = 128
_VMEM_LIMIT = 56 * 1024 * 1024

_LR, _B1, _B2, _AEPS, _WD, _STEP = 0.001, 0.9, 0.999, 1e-08, 0.01, 10

_NN = (((1,), (0,)), ((), ()))
_NT = (((1,), (1,)), ((), ()))
_TN = (((0,), (0,)), ((), ()))

_MESH = pl.DeviceIdType.MESH


def _cp(*sem):
    return pltpu.CompilerParams(dimension_semantics=tuple(sem), vmem_limit_bytes=_VMEM_LIMIT)


def _dot(a, b, dn=_NN):
    return lax.dot_general(a.astype(_MMT), b.astype(_MMT), dn, preferred_element_type=_F32)


def _pick(n, target, unit=_LANE):
    if n <= target:
        return n
    best = None
    for t in range(unit, target + 1, unit):
        if n % t == 0:
            best = t
    assert best is not None, (n, target)
    return best


def _sigmoid(x):
    return 0.5 * jnp.tanh(0.5 * x) + 0.5


def _silu(x):
    return x * _sigmoid(x)


def _dsilu(x):
    s = _sigmoid(x)
    return s * (1.0 + x * (1.0 - s))


_GK = math.sqrt(2.0 / math.pi)


def _gelu(x):
    return 0.5 * x * (1.0 + jnp.tanh(_GK * (x + 0.044715 * x * x * x)))


def _dgelu(x):
    t = jnp.tanh(_GK * (x + 0.044715 * x * x * x))
    return 0.5 * (1.0 + t) + 0.5 * x * (1.0 - t * t) * _GK * (1.0 + 3.0 * 0.044715 * x * x)


def _softplus(x):
    return jnp.maximum(x, 0.0) + jnp.log(1.0 + jnp.exp(-jnp.abs(x)))


def _rspec(tb, w, cb=0):
    return pl.BlockSpec((tb, w), lambda i: (i, cb))


def _fspec(shape):
    nd = len(shape)
    return pl.BlockSpec(tuple(shape), lambda i: (0,) * nd)


def _lspec(tail, li):
    nd = len(tail)
    return pl.BlockSpec((None,) + tuple(tail), lambda i: (li,) + (0,) * nd)


def _slot_all8(x, y, c):
    return 4 * x + 2 * y + c


def _gather8(v, name):
    R, W = v.shape

    def body(v_ref, o_ref, ssem, rsem, lsem):
        x, y, c = lax.axis_index("x"), lax.axis_index("y"), lax.axis_index("c")
        sib = (x, y, 1 - c)
        chips = _other_chips(x, y)

        def slot(px, py, pc):
            return o_ref.at[_slot_all8(px, py, pc)]

        own = pltpu.make_async_copy(v_ref, slot(x, y, c), lsem)
        own.start()
        started = [_rcopy(v_ref, slot(x, y, c), ssem.at[0], rsem.at[0], sib)]
        started += [_rcopy(v_ref, slot(x, y, c), ssem.at[1 + j], rsem.at[1 + j], (px, py, c))
                    for j, (px, py) in enumerate(chips)]
        for cp in started:
            cp.start()
        for j, (px, py) in enumerate(chips):
            blk = slot(px, py, c)
            _rcopy(blk, blk, ssem.at[1 + j], rsem.at[1 + j], (px, py, c)).wait_recv()
            fw = _rcopy(blk, blk, ssem.at[4 + j], rsem.at[4 + j], sib)
            fw.start()
            started.append(fw)
        blk = slot(x, y, 1 - c)
        _rcopy(blk, blk, ssem.at[0], rsem.at[0], sib).wait_recv()
        for j, (px, py) in enumerate(chips):
            blk = slot(px, py, 1 - c)
            _rcopy(blk, blk, ssem.at[4 + j], rsem.at[4 + j], sib).wait_recv()
        for cp in started:
            cp.wait_send()
        own.wait()

    return pl.pallas_call(
        body, name=name, out_shape=jax.ShapeDtypeStruct((8, R, W), v.dtype), in_specs=[_HBM], out_specs=_HBM,
        scratch_shapes=[pltpu.SemaphoreType.DMA((7,)), pltpu.SemaphoreType.DMA((7,)), pltpu.SemaphoreType.DMA],
    )(v)


def _rcopy(src, dst, ssem, rsem, dev):
    return pltpu.make_async_remote_copy(src_ref=src, dst_ref=dst, send_sem=ssem, recv_sem=rsem,
                                        device_id=dev, device_id_type=_MESH)


def _other_chips(x, y):
    return [(1 - x, y), (x, 1 - y), (1 - x, 1 - y)]


_HBM = pl.BlockSpec(memory_space=pl.ANY)


def _gather_chips(shard, name):
    Rp, W = shard.shape
    Rh = Rp // 2
    rc = Rh // _NCH
    hq = _NCH // 2

    def body(s_ref, o_ref, ssem, rsem, lsem):
        x, y, c = lax.axis_index("x"), lax.axis_index("y"), lax.axis_index("c")
        chip = 2 * x + y
        xn, yn, dg = _other_chips(x, y)
        cx, cy, cd = 2 * xn[0] + xn[1], 2 * yn[0] + yn[1], 2 * dg[0] + dg[1]

        def rows(q):
            return pl.ds(c * Rh + q * rc, rc)

        locs = []
        for q in range(_NCH):
            lc = pltpu.make_async_copy(s_ref.at[rows(q)], o_ref.at[chip, rows(q)], lsem.at[q])
            lc.start()
            locs.append(lc)
        started = []
        for q in range(_NCH):
            for j, nb in ((0, xn), (1, yn)):
                cp = _rcopy(s_ref.at[rows(q)], o_ref.at[chip, rows(q)], ssem.at[j * _NCH + q], rsem.at[j * _NCH + q],
                            (nb[0], nb[1], c))
                cp.start()
                started.append(cp)
        for q in range(_NCH):
            bx = o_ref.at[cx, rows(q)]
            _rcopy(bx, bx, ssem.at[q], rsem.at[q], (xn[0], xn[1], c)).wait_recv()
            if q >= hq:
                rl = _rcopy(bx, bx, ssem.at[2 * _NCH + q], rsem.at[2 * _NCH + q], (yn[0], yn[1], c))
                rl.start()
                started.append(rl)
            by = o_ref.at[cy, rows(q)]
            _rcopy(by, by, ssem.at[_NCH + q], rsem.at[_NCH + q], (yn[0], yn[1], c)).wait_recv()
            if q < hq:
                rl = _rcopy(by, by, ssem.at[2 * _NCH + q], rsem.at[2 * _NCH + q], (xn[0], xn[1], c))
                rl.start()
                started.append(rl)
        for q in range(_NCH):
            bd = o_ref.at[cd, rows(q)]
            _rcopy(bd, bd, ssem.at[2 * _NCH + q], rsem.at[2 * _NCH + q], (dg[0], dg[1], c)).wait_recv()
        for cp in started:
            cp.wait_send()
        for lc in locs:
            lc.wait()

    return pl.pallas_call(
        body, name=name, out_shape=jax.ShapeDtypeStruct((4, Rp, W), shard.dtype), in_specs=[_HBM], out_specs=_HBM,
        scratch_shapes=[pltpu.SemaphoreType.DMA((3 * _NCH,))] * 2 + [pltpu.SemaphoreType.DMA((_NCH,))],
    )(shard)


def _fill_from_sibling(buf, name):
    P, Rp, W = buf.shape
    Rh = Rp // 2
    rc = Rh // _NCH

    def body(s_ref, o_ref, ssem, rsem):
        x, y, c = lax.axis_index("x"), lax.axis_index("y"), lax.axis_index("c")
        cps = []
        for k in range(P):
            for q in range(_NCH):
                r = pl.ds(c * Rh + q * rc, rc)
                cp = _rcopy(s_ref.at[k, r], o_ref.at[k, r], ssem.at[k * _NCH + q], rsem.at[k * _NCH + q],
                            (x, y, 1 - c))
                cp.start()
                cps.append(cp)
        for k in range(P):
            for q in range(_NCH):
                blk = o_ref.at[k, pl.ds((1 - c) * Rh + q * rc, rc)]
                _rcopy(blk, blk, ssem.at[k * _NCH + q], rsem.at[k * _NCH + q], (x, y, 1 - c)).wait_recv()
        for cp in cps:
            cp.wait_send()

    return pl.pallas_call(
        body, name=name, out_shape=jax.ShapeDtypeStruct(buf.shape, buf.dtype), in_specs=[_HBM], out_specs=_HBM,
        scratch_shapes=[pltpu.SemaphoreType.DMA((P * _NCH,))] * 2, input_output_aliases={0: 0},
    )(buf)


def _send_half_to_sibling(send, name):
    P, Rp, W = send.shape
    Rh = Rp // 2
    rc = Rh // _NCH

    def body(s_ref, o_ref, ssem, rsem):
        x, y, c = lax.axis_index("x"), lax.axis_index("y"), lax.axis_index("c")
        cps = []
        for k in range(P):
            for q in range(_NCH):
                cp = _rcopy(s_ref.at[k, pl.ds((1 - c) * Rh + q * rc, rc)], o_ref.at[k, pl.ds(q * rc, rc)],
                            ssem.at[k * _NCH + q], rsem.at[k * _NCH + q], (x, y, 1 - c))
                cp.start()
                cps.append(cp)
        for cp in cps:
            cp.wait()

    return pl.pallas_call(
        body, name=name, out_shape=jax.ShapeDtypeStruct((P, Rh, W), send.dtype), in_specs=[_HBM], out_specs=_HBM,
        scratch_shapes=[pltpu.SemaphoreType.DMA((P * _NCH,))] * 2,
    )(send)


def _scatter_to_chips(cs, name):
    P, Rh, W = cs.shape
    rc = Rh // _NCH

    def body(s_ref, o_ref, ssem, rsem, lsem):
        x, y, c = lax.axis_index("x"), lax.axis_index("y"), lax.axis_index("c")
        chip = 2 * x + y
        peers = _other_chips(x, y)
        locs = []
        for q in range(_NCH):
            r = pl.ds(q * rc, rc)
            lc = pltpu.make_async_copy(s_ref.at[chip, r], o_ref.at[chip, r], lsem.at[q])
            lc.start()
            locs.append(lc)
        cps = []
        for j, (px, py) in enumerate(peers):
            for q in range(_NCH):
                r = pl.ds(q * rc, rc)
                cp = _rcopy(s_ref.at[2 * px + py, r], o_ref.at[chip, r], ssem.at[j * _NCH + q], rsem.at[j * _NCH + q],
                            (px, py, c))
                cp.start()
                cps.append(cp)
        for j, (px, py) in enumerate(peers):
            for q in range(_NCH):
                blk = o_ref.at[2 * px + py, pl.ds(q * rc, rc)]
                _rcopy(blk, blk, ssem.at[j * _NCH + q], rsem.at[j * _NCH + q], (px, py, c)).wait_recv()
        for cp in cps:
            cp.wait_send()
        for lc in locs:
            lc.wait()

    return pl.pallas_call(
        body, name=name, out_shape=jax.ShapeDtypeStruct((P, Rh, W), cs.dtype), in_specs=[_HBM], out_specs=_HBM,
        scratch_shapes=[pltpu.SemaphoreType.DMA((3 * _NCH,))] * 2 + [pltpu.SemaphoreType.DMA((_NCH,))],
    )(cs)


def _swap_with_sibling(v, name):
    R, W = v.shape
    rc = R // _NCH

    def body(s_ref, o_ref, ssem, rsem):
        x, y, c = lax.axis_index("x"), lax.axis_index("y"), lax.axis_index("c")
        cps = []
        for q in range(_NCH):
            r = pl.ds(q * rc, rc)
            cp = _rcopy(s_ref.at[r], o_ref.at[r], ssem.at[q], rsem.at[q], (x, y, 1 - c))
            cp.start()
            cps.append(cp)
        for cp in cps:
            cp.wait()

    return pl.pallas_call(
        body, name=name, out_shape=jax.ShapeDtypeStruct((R, W), v.dtype), in_specs=[_HBM], out_specs=_HBM,
        scratch_shapes=[pltpu.SemaphoreType.DMA((_NCH,))] * 2,
    )(v)


def _mm(a, b, name, li=None, trans_b=False, out_dtype=_F32, tm=1024, tn=1024, tk=2048):
    M, K = a.shape
    bs = b.shape[-2:]
    N = bs[0] if trans_b else bs[1]
    tm, tn, tk = _pick(M, tm, 8), _pick(N, tn), _pick(K, tk)
    nk = K // tk
    lead = () if li is None else (None,)

    def bmap(i, j, k):
        idx = (j, k) if trans_b else (k, j)
        return idx if li is None else (li,) + idx

    def body(a_ref, b_ref, o_ref, acc):
        k = pl.program_id(2)
        part = lax.dot_general(a_ref[...], b_ref[...], _NT if trans_b else _NN, preferred_element_type=_F32)
        if nk == 1:
            o_ref[...] = part.astype(o_ref.dtype)
        else:
            @pl.when(k == 0)
            def _():
                acc[...] = part

            @pl.when(k > 0)
            def _():
                acc[...] += part

            @pl.when(k == nk - 1)
            def _():
                o_ref[...] = acc[...].astype(o_ref.dtype)

    return pl.pallas_call(
        body, name=name, grid=(M // tm, N // tn, nk),
        in_specs=[pl.BlockSpec((tm, tk), lambda i, j, k: (i, k)),
                  pl.BlockSpec(lead + ((tn, tk) if trans_b else (tk, tn)), bmap)],
        out_specs=pl.BlockSpec((tm, tn), lambda i, j, k: (i, j)),
        out_shape=jax.ShapeDtypeStruct((M, N), out_dtype),
        scratch_shapes=[pltpu.VMEM((tm, tn) if nk > 1 else (8, _LANE), _F32)],
        compiler_params=_cp("parallel", "parallel", "arbitrary"),
    )(a, b)


def _mm_tn(a, b, name, li, nl, into=None, tm=512, tn=1024):
    T, M = a.shape
    N = b.shape[1]
    tm, tn = _pick(M, tm), _pick(N, tn)

    def body(a_ref, b_ref, *rest):
        o_ref = rest[-1]
        o_ref[...] = lax.dot_general(a_ref[...], b_ref[...], _TN, preferred_element_type=_F32).astype(o_ref.dtype)

    ins = [pl.BlockSpec((T, tm), lambda i, j: (0, i)), pl.BlockSpec((T, tn), lambda i, j: (0, j))]
    return pl.pallas_call(
        body, name=name, grid=(M // tm, N // tn),
        in_specs=ins if into is None else ins + [_HBM],
        out_specs=pl.BlockSpec((None, tm, tn), lambda i, j: (li, i, j)),
        out_shape=jax.ShapeDtypeStruct((nl, M, N), _MMT),
        input_output_aliases={} if into is None else {2: 0},
        compiler_params=_cp("parallel", "parallel"),
    )(*((a, b) if into is None else (a, b, into)))


def _mm_res(a, b, x, gt, name, tm=1024, tn=1024, tk=2048):
    M, K = a.shape
    N = b.shape[1]
    tm, tn, tk = _pick(M, tm, 8), _pick(N, tn), _pick(K, tk)
    nk = K // tk

    def body(a_ref, b_ref, x_ref, gt_ref, p_ref, o_ref):
        k = pl.program_id(2)
        part = lax.dot_general(a_ref[...], b_ref[...], _NN, preferred_element_type=_F32)

        @pl.when(k == 0)
        def _():
            p_ref[...] = part

        @pl.when(k > 0)
        def _():
            p_ref[...] += part

        @pl.when(k == nk - 1)
        def _():
            o_ref[...] = x_ref[...] + gt_ref[...] * p_ref[...]

    tile = pl.BlockSpec((tm, tn), lambda i, j, k: (i, j))
    return pl.pallas_call(
        body, name=name, grid=(M // tm, N // tn, nk),
        in_specs=[pl.BlockSpec((tm, tk), lambda i, j, k: (i, k)), pl.BlockSpec((tk, tn), lambda i, j, k: (k, j)),
                  tile, pl.BlockSpec((1, tn), lambda i, j, k: (0, j))],
        out_specs=[tile, tile], out_shape=[jax.ShapeDtypeStruct((M, N), _F32)] * 2,
        compiler_params=_cp("parallel", "parallel", "arbitrary"),
    )(a, b, x, gt)


def _mm_normb(a, b, x, dres, g, sc, name, tm=1024, tk=2048):
    M, K = a.shape
    D = b.shape[1]
    tm, tk = _pick(M, tm, 8), _pick(K, tk)
    nk = K // tk

    def body(a_ref, b_ref, x_ref, dr_ref, g_ref, sc_ref, dx_ref, dgm_ref, dsh_ref, acc):
        i, k = pl.program_id(0), pl.program_id(1)
        part = lax.dot_general(a_ref[...], b_ref[...], _NN, preferred_element_type=_F32)

        @pl.when(k == 0)
        def _():
            acc[...] = part

        @pl.when(k > 0)
        def _():
            acc[...] += part

        @pl.when(jnp.logical_and(i == 0, k == 0))
        def _():
            dgm_ref[...] = jnp.zeros_like(dgm_ref)
            dsh_ref[...] = jnp.zeros_like(dsh_ref)

        @pl.when(k == nk - 1)
        def _():
            dh_, xv = acc[...], x_ref[...]
            r = lax.rsqrt(jnp.mean(xv * xv, axis=1, keepdims=True) + _EPS)
            xn = xv * r
            dxn = dh_ * (g_ref[...] * (1.0 + sc_ref[...]))
            dx_ref[...] = dr_ref[...] + r * (dxn - xn * jnp.mean(dxn * xn, axis=1, keepdims=True))
            dgm_ref[...] += jnp.sum(dh_ * xn, axis=0, keepdims=True)
            dsh_ref[...] += jnp.sum(dh_, axis=0, keepdims=True)

    row = pl.BlockSpec((tm, D), lambda i, k: (i, 0))
    vec = pl.BlockSpec((1, D), lambda i, k: (0, 0))
    return pl.pallas_call(
        body, name=name, grid=(M // tm, nk),
        in_specs=[pl.BlockSpec((tm, tk), lambda i, k: (i, k)), pl.BlockSpec((tk, D), lambda i, k: (k, 0)),
                  row, row, vec, vec],
        out_specs=[row, vec, vec],
        out_shape=[jax.ShapeDtypeStruct((M, D), _F32), jax.ShapeDtypeStruct((1, D), _F32),
                   jax.ShapeDtypeStruct((1, D), _F32)],
        scratch_shapes=[pltpu.VMEM((tm, D), _F32)],
        compiler_params=_cp("arbitrary", "arbitrary"),
    )(a, b, x, dres, g, sc)


def _ffin_swiglu(a, wt, name, tm=1024, tn=1408):
    M, K = a.shape
    F = wt.shape[0] // 2
    tm, tn = _pick(M, tm, 8), _pick(F, tn)
    nj = F // tn

    def body(a_ref, bg_ref, bu_ref, g_ref, u_ref, act_ref):
        av = a_ref[...]
        g = lax.dot_general(av, bg_ref[...], _NT, preferred_element_type=_F32)
        u = lax.dot_general(av, bu_ref[...], _NT, preferred_element_type=_F32)
        g_ref[...] = g.astype(g_ref.dtype)
        u_ref[...] = u.astype(u_ref.dtype)
        act_ref[...] = (_silu(g) * u).astype(act_ref.dtype)

    tile = pl.BlockSpec((tm, tn), lambda i, j: (i, j))
    return pl.pallas_call(
        body, name=name, grid=(M // tm, nj),
        in_specs=[pl.BlockSpec((tm, K), lambda i, j: (i, 0)), pl.BlockSpec((tn, K), lambda i, j: (j, 0)),
                  pl.BlockSpec((tn, K), lambda i, j: (j + nj, 0))],
        out_specs=[tile] * 3, out_shape=[jax.ShapeDtypeStruct((M, F), _MMT)] * 3,
        compiler_params=_cp("parallel", "parallel"),
    )(a, wt, wt)


def _branch_merge(ya, yb, w_a, w_b, proj, name, tm=1024, tn=512):
    M, K = ya.shape
    N = w_a.shape[1]
    tm, tn = _pick(M, tm, 8), _pick(N, tn)
    nj = N // tn

    def body(ya_ref, yb_ref, wa_ref, wb_ref, ga_ref, gb_ref, pa_ref, pb_ref, m_ref):
        pa = lax.dot_general(ya_ref[...], wa_ref[...], _NN, preferred_element_type=_F32)
        pb = lax.dot_general(yb_ref[...], wb_ref[...], _NN, preferred_element_type=_F32)
        pa_ref[...] = pa.astype(pa_ref.dtype)
        pb_ref[...] = pb.astype(pb_ref.dtype)
        m_ref[...] = (_sigmoid(ga_ref[...]) * pa + _sigmoid(gb_ref[...]) * pb).astype(m_ref.dtype)

    row = pl.BlockSpec((tm, K), lambda i, j: (i, 0))
    col = pl.BlockSpec((K, tn), lambda i, j: (0, j))
    tile = pl.BlockSpec((tm, tn), lambda i, j: (i, j))
    return pl.pallas_call(
        body, name=name, grid=(M // tm, nj),
        in_specs=[row, row, col, col, pl.BlockSpec((tm, tn), lambda i, j: (i, 6 * nj + j)),
                  pl.BlockSpec((tm, tn), lambda i, j: (i, 7 * nj + j))],
        out_specs=[tile] * 3, out_shape=[jax.ShapeDtypeStruct((M, N), _MMT)] * 3,
        compiler_params=_cp("parallel", "parallel"),
    )(ya, yb, w_a, w_b, proj, proj)


def _ada_fwd(c_all, ada_w, name):
    L, D, Ws = ada_w.shape
    B = c_all.shape[0]

    def body(c_ref, w_ref, o_ref):
        o_ref[...] = _dot(_silu(c_ref[...]), w_ref[...])

    return pl.pallas_call(
        body, name=name, grid=(L,),
        in_specs=[_fspec((B, D)), pl.BlockSpec((None, D, Ws), lambda l: (l, 0, 0))],
        out_specs=pl.BlockSpec((None, B, Ws), lambda l: (l, 0, 0)),
        out_shape=jax.ShapeDtypeStruct((L, B, Ws), _F32), compiler_params=_cp("parallel"),
    )(c_all, ada_w)


def _ada_bwd(c_all_t, dmod, name):
    D, B = c_all_t.shape
    L, _, Ws = dmod.shape

    def body(c_ref, d_ref, o_ref):
        ct = _silu(c_ref[...])
        d = d_ref[...]
        acc = ct[:, 0:1] * d[0:1, :]
        for b in range(1, B):
            acc = acc + ct[:, b:b + 1] * d[b:b + 1, :]
        o_ref[...] = acc

    return pl.pallas_call(
        body, name=name, grid=(L,),
        in_specs=[_fspec((D, B)), pl.BlockSpec((None, B, Ws), lambda l: (l, 0, 0))],
        out_specs=pl.BlockSpec((None, D, Ws), lambda l: (l, 0, 0)),
        out_shape=jax.ShapeDtypeStruct((L, D, Ws), _F32), compiler_params=_cp("parallel"),
    )(c_all_t, dmod)


def _norm_mod(x, g, sc, sh, name, tb=512):
    T, D = x.shape
    tb = _pick(T, tb, 8)

    def body(x_ref, g_ref, sc_ref, sh_ref, h_ref):
        xv = x_ref[...]
        r = lax.rsqrt(jnp.mean(xv * xv, axis=1, keepdims=True) + _EPS)
        h_ref[...] = (xv * r * (g_ref[...] * (1.0 + sc_ref[...])) + sh_ref[...]).astype(h_ref.dtype)

    return pl.pallas_call(
        body, name=name, grid=(T // tb,),
        in_specs=[_rspec(tb, D), _fspec((1, D)), _fspec((1, D)), _fspec((1, D))],
        out_specs=_rspec(tb, D), out_shape=jax.ShapeDtypeStruct((T, D), _MMT), compiler_params=_cp("parallel"),
    )(x, g, sc, sh)


def _resid_bwd(dx, p, gt, name, tb=512):
    T, D = dx.shape
    tb = _pick(T, tb, 8)

    def body(dx_ref, p_ref, gt_ref, dp_ref, dgt_ref):
        i = pl.program_id(0)
        d = dx_ref[...]
        dp_ref[...] = (d * gt_ref[...]).astype(dp_ref.dtype)

        @pl.when(i == 0)
        def _():
            dgt_ref[...] = jnp.zeros_like(dgt_ref)

        dgt_ref[...] += jnp.sum(d * p_ref[...], axis=0, keepdims=True)

    return pl.pallas_call(
        body, name=name, grid=(T // tb,), in_specs=[_rspec(tb, D), _rspec(tb, D), _fspec((1, D))],
        out_specs=[_rspec(tb, D), _fspec((1, D))],
        out_shape=[jax.ShapeDtypeStruct((T, D), _MMT), jax.ShapeDtypeStruct((1, D), _F32)],
        compiler_params=_cp("arbitrary"),
    )(dx, p, gt)


def _gmlp_chunk(u_raw, v_raw, sw_ref, sbt, gv, G):
    u, v = _gelu(u_raw), _gelu(v_raw)
    ii = lax.broadcasted_iota(jnp.int32, (_AC, _AC), 0)
    jj = lax.broadcasted_iota(jnp.int32, (_AC, _AC), 1)
    out = []
    for gi in range(G):
        sl = slice(gi * _GD, (gi + 1) * _GD)
        vg = v[:, sl]
        r = lax.rsqrt(jnp.mean(vg * vg, axis=1, keepdims=True) + _EPS)
        vhat = vg * r
        W = jnp.where(jj <= ii, sw_ref[gi], 0.0)
        s = _dot(W, vhat * gv[:, sl]) + sbt[:, gi:gi + 1]
        out.append((u[:, sl], s, vhat, r, W))
    return out


def _gmlp_fwd(proj, sw, sbt, gv, li, D, name):
    T = proj.shape[0]
    G = D // _GD

    def body(u_ref, v_ref, sw_ref, sbt_ref, gv_ref, y_ref):
        parts = _gmlp_chunk(u_ref[...], v_ref[...], sw_ref, sbt_ref[...], gv_ref[...], G)
        for gi, (u, s, _, _, _) in enumerate(parts):
            y_ref[:, gi * _GD:(gi + 1) * _GD] = (u * s).astype(y_ref.dtype)

    return pl.pallas_call(
        body, name=name, grid=(T // _AC,),
        in_specs=[_rspec(_AC, D, 0), _rspec(_AC, D, 1), _lspec((G, _AC, _AC), li), _lspec((_AC, G), li),
                  _lspec((1, D), li)],
        out_specs=_rspec(_AC, D), out_shape=jax.ShapeDtypeStruct((T, D), _MMT), compiler_params=_cp("parallel"),
    )(proj, proj, sw, sbt, gv)


def _gmlp_bwd(proj, dy, sw, sbt, gv, li, D, into, name):
    T = proj.shape[0]
    G = D // _GD

    def body(u_ref, v_ref, dy_ref, sw_ref, sbt_ref, gv_ref, _, duv_ref, dsw_ref, dsa_ref, dgv_ref):
        i = pl.program_id(0)

        @pl.when(i == 0)
        def _():
            dsw_ref[...] = jnp.zeros_like(dsw_ref)
            dsa_ref[...] = jnp.zeros_like(dsa_ref)
            dgv_ref[...] = jnp.zeros_like(dgv_ref)

        u_raw, v_raw, dy_, gv_ = u_ref[...], v_ref[...], dy_ref[...].astype(_F32), gv_ref[...]
        parts = _gmlp_chunk(u_raw, v_raw, sw_ref, sbt_ref[...], gv_, G)
        ii = lax.broadcasted_iota(jnp.int32, (_AC, _AC), 0)
        jj = lax.broadcasted_iota(jnp.int32, (_AC, _AC), 1)
        dgu, dgv = _dgelu(u_raw), _dgelu(v_raw)
        for gi, (u, s, vhat, r, W) in enumerate(parts):
            sl = slice(gi * _GD, (gi + 1) * _GD)
            dyg = dy_[:, sl]
            ds = dyg * u
            vn = vhat * gv_[:, sl]
            dsw_ref[gi] += jnp.where(jj <= ii, _dot(ds, vn, _NT), 0.0)
            dsa_ref[:, sl] += ds
            dvn = _dot(W, ds, _TN)
            dgv_ref[:, sl] += jnp.sum(dvn * vhat, axis=0, keepdims=True)
            dvh = dvn * gv_[:, sl]
            dvg = r * (dvh - vhat * jnp.mean(dvh * vhat, axis=1, keepdims=True))
            duv_ref[:, sl] = (dyg * s * dgu[:, sl]).astype(duv_ref.dtype)
            duv_ref[:, D + gi * _GD:D + (gi + 1) * _GD] = (dvg * dgv[:, sl]).astype(duv_ref.dtype)

    return pl.pallas_call(
        body, name=name, grid=(T // _AC,),
        in_specs=[_rspec(_AC, D, 0), _rspec(_AC, D, 1), _rspec(_AC, D), _lspec((G, _AC, _AC), li),
                  _lspec((_AC, G), li), _lspec((1, D), li), _HBM],
        out_specs=[_rspec(_AC, 2 * D), _fspec((G, _AC, _AC)), _fspec((_AC, D)), _fspec((1, D))],
        out_shape=[jax.ShapeDtypeStruct(into.shape, into.dtype), jax.ShapeDtypeStruct((G, _AC, _AC), _F32),
                   jax.ShapeDtypeStruct((_AC, D), _F32), jax.ShapeDtypeStruct((1, D), _F32)],
        input_output_aliases={6: 0}, compiler_params=_cp("arbitrary"),
    )(proj, proj, dy, sw, sbt, gv, into)


def _conv_taps(halo, cur, first):
    tb = cur.shape[0]
    full = jnp.concatenate([jnp.where(first, 0.0, halo), cur], axis=0)
    return [full[8:] if j == _KC - 1 else pltpu.roll(full, _KC - 1 - j, 0)[8:] for j in range(_KC)]


def _prev_spec(tb, w, cb):
    return pl.BlockSpec((8, w), lambda i: (jnp.maximum(i * (tb // 8) - 1, 0), cb))


def _l2_heads(x, H):
    outs, rs = [], []
    for h in range(H):
        xh = x[:, h * _GD:(h + 1) * _GD]
        r = lax.rsqrt(jnp.sum(xh * xh, axis=1, keepdims=True) + _EPS)
        outs.append(xh * r)
        rs.append(r)
    return outs, rs


def _gate_rows(ba, alog_row, dtb_row, H):
    lane = lax.broadcasted_iota(jnp.int32, ba.shape, 1)
    beta = _sigmoid(ba)
    g = -jnp.exp(alog_row) * _softplus(ba + dtb_row)
    return lane, beta, g


def _conv_fwd(proj, cw, alog_row, dtb_row, li, D, name, tb=256):
    T = proj.shape[0]
    H = D // _GD
    tb = _pick(T, tb, 8)
    bac = (8 * D) // _LANE

    def body(q_ref, k_ref, v_ref, qh_ref, kh_ref, vh_ref, ba_ref, cw_ref, al_ref, dtb_ref,
             qo_ref, ko_ref, vo_ref, bg_ref):
        first = pl.program_id(0) == 0
        cw_ = cw_ref[...]
        for idx, (cur, halo, out) in enumerate(((q_ref, qh_ref, qo_ref), (k_ref, kh_ref, ko_ref),
                                                 (v_ref, vh_ref, vo_ref))):
            taps = _conv_taps(halo[...], cur[...], first)
            w = cw_[:, idx * D:(idx + 1) * D]
            cv = taps[0] * w[0:1, :]
            for j in range(1, _KC):
                cv = cv + taps[j] * w[j:j + 1, :]
            act = _silu(cv)
            if idx < 2:
                outs, _ = _l2_heads(act, H)
                for h in range(H):
                    out[:, h * _GD:(h + 1) * _GD] = outs[h]
            else:
                out[...] = act
        lane, beta, g = _gate_rows(ba_ref[...], al_ref[...], dtb_ref[...], H)
        bg_ref[...] = jnp.where(lane < H, beta, jnp.where(lane < 2 * H, g, 0.0))

    return pl.pallas_call(
        body, name=name, grid=(T // tb,),
        in_specs=[_rspec(tb, D, 2), _rspec(tb, D, 3), _rspec(tb, D, 4),
                  _prev_spec(tb, D, 2), _prev_spec(tb, D, 3), _prev_spec(tb, D, 4),
                  _rspec(tb, _LANE, bac), _lspec((_KC, 3 * D), li), _lspec((1, _LANE), li), _lspec((1, _LANE), li)],
        out_specs=[_rspec(tb, D), _rspec(tb, D), _rspec(tb, D), _rspec(tb, _LANE)],
        out_shape=[jax.ShapeDtypeStruct((T, D), _F32)] * 3 + [jax.ShapeDtypeStruct((T, _LANE), _F32)],
        compiler_params=_cp("parallel"),
    )(proj, proj, proj, proj, proj, proj, proj, cw, alog_row, dtb_row)


def _conv_bwd1(proj, dqn, dkn, dvs, dbg, cw, alog_row, dtb_row, li, D, into, name, tb=256):
    T = proj.shape[0]
    H = D // _GD
    tb = _pick(T, tb, 8)
    bac = (8 * D) // _LANE

    def body(q_ref, k_ref, v_ref, qh_ref, kh_ref, vh_ref, ba_ref, dq_ref, dk_ref, dv_ref, dbg_ref,
             cw_ref, al_ref, dtb_ref, _, dc_ref, dba_ref, dcw_ref, dal_ref, ddt_ref):
        i = pl.program_id(0)
        first = i == 0

        @pl.when(first)
        def _():
            dcw_ref[...] = jnp.zeros_like(dcw_ref)
            dal_ref[...] = jnp.zeros_like(dal_ref)
            ddt_ref[...] = jnp.zeros_like(ddt_ref)

        cw_ = cw_ref[...]
        for idx, (cur, halo, dref) in enumerate(((q_ref, qh_ref, dq_ref), (k_ref, kh_ref, dk_ref),
                                                  (v_ref, vh_ref, dv_ref))):
            taps = _conv_taps(halo[...], cur[...], first)
            w = cw_[:, idx * D:(idx + 1) * D]
            cv = taps[0] * w[0:1, :]
            for j in range(1, _KC):
                cv = cv + taps[j] * w[j:j + 1, :]
            dact = dref[...]
            if idx < 2:
                outs, rs = _l2_heads(_silu(cv), H)
                pieces = []
                for h in range(H):
                    dy = dact[:, h * _GD:(h + 1) * _GD]
                    pieces.append(rs[h] * (dy - outs[h] * jnp.sum(dy * outs[h], axis=1, keepdims=True)))
                dact = jnp.concatenate(pieces, axis=1)
            dcv = dact * _dsilu(cv)
            dc_ref[:, idx * D:(idx + 1) * D] = dcv
            for j in range(_KC):
                colsum = _dot(jnp.ones((8, tb), _F32), dcv * taps[j])
                dcw_ref[j:j + 1, idx * D:(idx + 1) * D] += colsum[0:1, :]

        ba = ba_ref[...]
        lane, beta, g = _gate_rows(ba, al_ref[...], dtb_ref[...], H)
        dbg_ = dbg_ref[...]
        is_b, is_a = lane < H, jnp.logical_and(lane >= H, lane < 2 * H)
        da = dbg_ * (-jnp.exp(al_ref[...])) * _sigmoid(ba + dtb_ref[...])
        dba_ref[...] = jnp.where(is_b, dbg_ * beta * (1.0 - beta), jnp.where(is_a, da, 0.0)).astype(dba_ref.dtype)
        dal_ref[...] += jnp.sum(jnp.where(is_a, dbg_ * g, 0.0), axis=0, keepdims=True)
        ddt_ref[...] += jnp.sum(jnp.where(is_a, da, 0.0), axis=0, keepdims=True)

    return pl.pallas_call(
        body, name=name, grid=(T // tb,),
        in_specs=[_rspec(tb, D, 2), _rspec(tb, D, 3), _rspec(tb, D, 4),
                  _prev_spec(tb, D, 2), _prev_spec(tb, D, 3), _prev_spec(tb, D, 4),
                  _rspec(tb, _LANE, bac), _rspec(tb, D), _rspec(tb, D), _rspec(tb, D), _rspec(tb, _LANE),
                  _lspec((_KC, 3 * D), li), _lspec((1, _LANE), li), _lspec((1, _LANE), li), _HBM],
        out_specs=[_rspec(tb, 3 * D), _rspec(tb, _LANE, bac), _fspec((_KC, 3 * D)), _fspec((1, _LANE)),
                   _fspec((1, _LANE))],
        out_shape=[jax.ShapeDtypeStruct((T, 3 * D), _F32), jax.ShapeDtypeStruct(into.shape, into.dtype),
                   jax.ShapeDtypeStruct((_KC, 3 * D), _F32), jax.ShapeDtypeStruct((1, _LANE), _F32),
                   jax.ShapeDtypeStruct((1, _LANE), _F32)],
        input_output_aliases={14: 1}, compiler_params=_cp("arbitrary"),
    )(proj, proj, proj, proj, proj, proj, proj, dqn, dkn, dvs, dbg, cw, alog_row, dtb_row, into)


def _conv_bwd2(dc, cw, li, into, name, tb=256):
    T, W3 = dc.shape
    W = W3 // 3
    tb = _pick(T, tb, 8)
    nb8 = T // 8
    nrow = T // tb

    def body(dc_ref, nx_ref, cw_ref, _, o_ref):
        last = pl.program_id(0) == nrow - 1
        full = jnp.concatenate([dc_ref[...], jnp.where(last, 0.0, nx_ref[...])], axis=0)
        w = cw_ref[...]
        acc = full[:tb] * w[_KC - 1:_KC, :]
        for j in range(_KC - 1):
            sh = _KC - 1 - j
            acc = acc + pltpu.roll(full, tb + 8 - sh, 0)[:tb] * w[j:j + 1, :]
        o_ref[...] = acc.astype(o_ref.dtype)

    return pl.pallas_call(
        body, name=name, grid=(nrow, 3),
        in_specs=[pl.BlockSpec((tb, W), lambda i, j: (i, j)),
                  pl.BlockSpec((8, W), lambda i, j: (jnp.minimum((i + 1) * (tb // 8), nb8 - 1), j)),
                  pl.BlockSpec((None, _KC, W), lambda i, j: (li, 0, j)), _HBM],
        out_specs=pl.BlockSpec((tb, W), lambda i, j: (i, 2 + j)),
        out_shape=jax.ShapeDtypeStruct(into.shape, into.dtype), input_output_aliases={3: 0},
        compiler_params=_cp("parallel", "parallel"),
    )(dc, dc, cw, into)


def _split(a):
    hi = a.astype(_BF)
    return hi, (a - hi.astype(_F32)).astype(_BF)


def _dot3(a, b):
    (ah, al), (bh, bl) = a, b
    f = functools.partial(lax.dot_general, dimension_numbers=_NN, preferred_element_type=_F32)
    return f(ah, bh) + f(ah, bl) + f(al, bh)


def _inv_unit_lower(mats):
    C = mats[0].shape[0]
    ii = lax.broadcasted_iota(jnp.int32, (C, C), 0)
    jj = lax.broadcasted_iota(jnp.int32, (C, C), 1)
    xs = [jnp.where(ii == jj, 1.0, 0.0) - a for a in mats]
    ps = list(mats)
    n = 1
    while 2 * n < C:
        sp = [_split(p) for p in ps]
        ps = [_dot3(s, s) for s in sp]
        sp = [_split(p) for p in ps]
        xs = [x + _dot3(_split(x), s) for x, s in zip(xs, sp)]
        n *= 2
    return xs


def _gdn_chunk(q, k, v, g_row, b_row):
    C = q.shape[0]
    ii = lax.broadcasted_iota(jnp.int32, (C, C), 0)
    jj = lax.broadcasted_iota(jnp.int32, (C, C), 1)
    low, strict, eye = jj <= ii, jj < ii, ii == jj
    g_col = jnp.sum(jnp.where(eye, g_row, 0.0), axis=1, keepdims=True)
    b_col = jnp.sum(jnp.where(eye, b_row, 0.0), axis=1, keepdims=True)
    gam_col = jnp.sum(jnp.where(low, g_row, 0.0), axis=1, keepdims=True)
    gam_row = jnp.sum(jnp.where(jj >= ii, g_col, 0.0), axis=0, keepdims=True)
    gam_last = jnp.sum(g_row, axis=1, keepdims=True)
    decay = jnp.where(low, jnp.exp(jnp.where(low, gam_col - gam_row, 0.0)), 0.0)
    eg = jnp.exp(gam_col)
    ekd = jnp.exp(gam_last - gam_col)
    qs = q * (_GD ** -0.5)
    kb = k * b_col
    kk = _dot(kb, k, _NT)
    qkraw = _dot(qs, k, _NT)
    return dict(low=low, strict=strict, eye=eye, ii=ii, jj=jj, b_col=b_col, decay=decay, eg=eg, ekd=ekd,
                gl=jnp.exp(gam_last), qs=qs, kb=kb, kk=kk, qkraw=qkraw,
                A=jnp.where(strict, kk * decay, 0.0), vb=v * b_col, kbg=kb * eg,
                qk=qkraw * decay, q_dec=qs * eg, k_dec=k * ekd)


def _gdn_fwd(qn, kn, vs, g_r, b_r, name):
    T, D = qn.shape
    H, N, C = D // _GD, T // _BC, _BC
    hb = min(_HB, H)

    def body(q_ref, k_ref, v_ref, g_ref, b_ref, o_ref, s_ref, t_ref, S):
        @pl.when(pl.program_id(1) == 0)
        def _():
            S[...] = jnp.zeros_like(S)

        hs = range(hb)
        sls = [slice(hh * _GD, (hh + 1) * _GD) for hh in hs]
        cms = [_gdn_chunk(q_ref[:, sl], k_ref[:, sl], v_ref[:, sl], g_ref[hh], b_ref[hh]) for hh, sl in zip(hs, sls)]
        tms = _inv_unit_lower([cm["A"] for cm in cms])
        us = [_dot(tm, cm["vb"]) for tm, cm in zip(tms, cms)]
        ws = [_dot(tm, cm["kbg"]) for tm, cm in zip(tms, cms)]
        s0s = [S[hh] for hh in hs]
        for hh in hs:
            s_ref[hh] = s0s[hh]
            t_ref[hh] = tms[hh]
        v_news = [u - _dot(w, s0) for u, w, s0 in zip(us, ws, s0s)]
        qss = [_dot(cm["q_dec"], s0) for cm, s0 in zip(cms, s0s)]
        for hh in hs:
            o_ref[:, sls[hh]] = qss[hh] + _dot(cms[hh]["qk"], v_news[hh])
        for hh in hs:
            S[hh] = s0s[hh] * cms[hh]["gl"] + _dot(cms[hh]["k_dec"], v_news[hh], _TN)

    qspec = pl.BlockSpec((C, hb * _GD), lambda h, n: (n, h))
    gspec = pl.BlockSpec((hb, None, 1, C), lambda h, n: (h, n, 0, 0))
    return pl.pallas_call(
        body, name=name, grid=(H // hb, N),
        in_specs=[qspec, qspec, qspec, gspec, gspec],
        out_specs=[qspec, pl.BlockSpec((hb, None, _GD, _GD), lambda h, n: (h, n, 0, 0)),
                   pl.BlockSpec((hb, None, C, C), lambda h, n: (h, n, 0, 0))],
        out_shape=[jax.ShapeDtypeStruct((T, D), _F32), jax.ShapeDtypeStruct((H, N, _GD, _GD), _F32),
                   jax.ShapeDtypeStruct((H, N, C, C), _F32)],
        scratch_shapes=[pltpu.VMEM((hb, _GD, _GD), _F32)],
        compiler_params=_cp("arbitrary", "arbitrary"),
    )(qn, kn, vs, g_r, b_r)


def _gdn_bwd(qn, kn, vs, g_r, b_r, s_all, t_all, do, name):
    T, D = qn.shape
    H, N, C = D // _GD, T // _BC, _BC
    hb = min(_HB, H)

    def body(q_ref, k_ref, v_ref, g_ref, b_ref, s_ref, t_ref, do_ref, dq_ref, dk_ref, dv_ref, dg_ref, db_ref, dS):
        @pl.when(pl.program_id(1) == 0)
        def _():
            dS[...] = jnp.zeros_like(dS)

        hs = range(hb)
        sls = [slice(hh * _GD, (hh + 1) * _GD) for hh in hs]
        ks = [k_ref[:, sl] for sl in sls]
        vs_ = [v_ref[:, sl] for sl in sls]
        cms = [_gdn_chunk(q_ref[:, sl], k, v, g_ref[hh], b_ref[hh]) for hh, sl, k, v in zip(hs, sls, ks, vs_)]
        low, strict, eye, ii, jj = (cms[0][n] for n in ("low", "strict", "eye", "ii", "jj"))
        tms, s0s, dos, ds1s = [t_ref[hh] for hh in hs], [s_ref[hh] for hh in hs], [do_ref[:, sl] for sl in sls], \
            [dS[hh] for hh in hs]
        us = [_dot(tm, cm["vb"]) for tm, cm in zip(tms, cms)]
        ws = [_dot(tm, cm["kbg"]) for tm, cm in zip(tms, cms)]
        v_news = [u - _dot(w, s0) for u, w, s0 in zip(us, ws, s0s)]
        dv_news = [_dot(cm["qk"], do_, _TN) + _dot(cm["k_dec"], ds1) for cm, do_, ds1 in zip(cms, dos, ds1s)]
        dqks = [jnp.where(low, _dot(do_, vn, _NT), 0.0) for do_, vn in zip(dos, v_news)]
        dq_decs = [_dot(do_, s0, _NT) for do_, s0 in zip(dos, s0s)]
        dk_decs = [_dot(vn, ds1, _NT) for vn, ds1 in zip(v_news, ds1s)]
        dgls = [jnp.sum(jnp.sum(ds1 * s0, axis=1, keepdims=True), axis=0, keepdims=True) for ds1, s0 in zip(ds1s, s0s)]
        dws = [-_dot(dvn, s0, _NT) for dvn, s0 in zip(dv_news, s0s)]
        for hh in hs:
            dS[hh] = (_dot(cms[hh]["q_dec"], dos[hh], _TN) + cms[hh]["gl"] * ds1s[hh]
                      - _dot(ws[hh], dv_news[hh], _TN))
        dvbs = [_dot(tm, dvn, _TN) for tm, dvn in zip(tms, dv_news)]
        dkbgs = [_dot(tm, dw, _TN) for tm, dw in zip(tms, dws)]
        dAs = [-jnp.where(strict, _dot(dvb, u, _NT) + _dot(dkbg, w, _NT), 0.0)
               for dvb, u, dkbg, w in zip(dvbs, us, dkbgs, ws)]
        dkks = [dA * cm["decay"] for dA, cm in zip(dAs, cms)]
        dqkraws = [dqk * cm["decay"] for dqk, cm in zip(dqks, cms)]
        Es = [(dA * cm["kk"] + dqk * cm["qkraw"]) * cm["decay"] for dA, dqk, cm in zip(dAs, dqks, cms)]
        dkbs = [_dot(dkk, k) + dkbg * cm["eg"] for dkk, k, dkbg, cm in zip(dkks, ks, dkbgs, cms)]
        dqss = [_dot(dqr, k) + dqd * cm["eg"] for dqr, k, dqd, cm in zip(dqkraws, ks, dq_decs, cms)]
        for hh in hs:
            cm = cms[hh]
            dk_ref[:, sls[hh]] = (_dot(dqkraws[hh], cm["qs"], _TN) + _dot(dkks[hh], cm["kb"], _TN)
                                  + dk_decs[hh] * cm["ekd"] + dkbs[hh] * cm["b_col"])
            dv_ref[:, sls[hh]] = dvbs[hh] * cm["b_col"]
            dq_ref[:, sls[hh]] = dqss[hh] * (_GD ** -0.5)
        for hh in hs:
            cm, k, E = cms[hh], ks[hh], Es[hh]
            eg, ekd = cm["eg"], cm["ekd"]
            dbeta_col = jnp.sum(dvbs[hh] * vs_[hh] + dkbs[hh] * k, axis=1, keepdims=True)
            t_kd = jnp.sum(dk_decs[hh] * k, axis=1, keepdims=True) * ekd
            c1 = (jnp.sum(E, axis=1, keepdims=True) + jnp.sum(dkbgs[hh] * cm["kb"], axis=1, keepdims=True) * eg
                  + jnp.sum(dq_decs[hh] * cm["qs"], axis=1, keepdims=True) * eg - t_kd)
            r1 = jnp.sum(E, axis=0, keepdims=True)
            dgam_last = jnp.sum(t_kd, axis=0, keepdims=True) + dgls[hh] * cm["gl"]
            dgam_col = c1 - jnp.sum(jnp.where(eye, r1, 0.0), axis=1, keepdims=True)
            dg_ref[hh] = jnp.sum(jnp.where(ii >= jj, dgam_col, 0.0), axis=0, keepdims=True) + dgam_last
            db_ref[hh] = jnp.sum(jnp.where(eye, dbeta_col, 0.0), axis=0, keepdims=True)

    qspec = pl.BlockSpec((C, hb * _GD), lambda h, n: (N - 1 - n, h))
    gspec = pl.BlockSpec((hb, None, 1, C), lambda h, n: (h, N - 1 - n, 0, 0))
    return pl.pallas_call(
        body, name=name, grid=(H // hb, N),
        in_specs=[qspec, qspec, qspec, gspec, gspec,
                  pl.BlockSpec((hb, None, _GD, _GD), lambda h, n: (h, N - 1 - n, 0, 0)),
                  pl.BlockSpec((hb, None, C, C), lambda h, n: (h, N - 1 - n, 0, 0)), qspec],
        out_specs=[qspec, qspec, qspec, gspec, gspec],
        out_shape=[jax.ShapeDtypeStruct((T, D), _F32)] * 3 + [jax.ShapeDtypeStruct((H, N, 1, C), _F32)] * 2,
        scratch_shapes=[pltpu.VMEM((hb, _GD, _GD), _F32)],
        compiler_params=_cp("arbitrary", "arbitrary"),
    )(qn, kn, vs, g_r, b_r, s_all, t_all, do)


def _onorm_fwd(o, proj, go, li, D, name, tb=512):
    T = o.shape[0]
    H = D // _GD
    tb = _pick(T, tb, 8)

    def body(o_ref, z_ref, go_ref, y_ref):
        ov, zv, g = o_ref[...], z_ref[...], go_ref[...]
        for h in range(H):
            sl = slice(h * _GD, (h + 1) * _GD)
            oh = ov[:, sl]
            r = lax.rsqrt(jnp.mean(oh * oh, axis=1, keepdims=True) + _EPS)
            y_ref[:, sl] = (oh * r * g * _silu(zv[:, sl])).astype(y_ref.dtype)

    return pl.pallas_call(
        body, name=name, grid=(T // tb,), in_specs=[_rspec(tb, D), _rspec(tb, D, 5), _lspec((1, _GD), li)],
        out_specs=_rspec(tb, D), out_shape=jax.ShapeDtypeStruct((T, D), _MMT), compiler_params=_cp("parallel"),
    )(o, proj, go)


def _onorm_bwd(dy, o, proj, go, li, D, into, name, tb=256):
    T = o.shape[0]
    H = D // _GD
    tb = _pick(T, tb, 8)

    def body(dy_ref, o_ref, z_ref, go_ref, _, do_ref, dz_ref, dgo_ref):
        @pl.when(pl.program_id(0) == 0)
        def _():
            dgo_ref[...] = jnp.zeros_like(dgo_ref)

        dyv, ov, zv, g = dy_ref[...].astype(_F32), o_ref[...], z_ref[...], go_ref[...]
        dgo = jnp.zeros((1, _GD), _F32)
        for h in range(H):
            sl = slice(h * _GD, (h + 1) * _GD)
            oh, zh, dyh = ov[:, sl], zv[:, sl], dyv[:, sl]
            r = lax.rsqrt(jnp.mean(oh * oh, axis=1, keepdims=True) + _EPS)
            on = oh * r
            sz = _silu(zh)
            dgo = dgo + jnp.sum(dyh * sz * on, axis=0, keepdims=True)
            don = dyh * sz * g
            do_ref[:, sl] = r * (don - on * jnp.mean(don * on, axis=1, keepdims=True))
            dz_ref[:, sl] = (dyh * on * g * _dsilu(zh)).astype(dz_ref.dtype)
        dgo_ref[...] += dgo

    return pl.pallas_call(
        body, name=name, grid=(T // tb,),
        in_specs=[_rspec(tb, D), _rspec(tb, D), _rspec(tb, D, 5), _lspec((1, _GD), li), _HBM],
        out_specs=[_rspec(tb, D), _rspec(tb, D, 5), _fspec((1, _GD))],
        out_shape=[jax.ShapeDtypeStruct((T, D), _F32), jax.ShapeDtypeStruct(into.shape, into.dtype),
                   jax.ShapeDtypeStruct((1, _GD), _F32)],
        input_output_aliases={4: 1}, compiler_params=_cp("arbitrary"),
    )(dy, o, proj, go, into)


def _merge_bwd(dm, pa, pb, proj, D, name, tb=256):
    T, PW = proj.shape
    tb = _pick(T, tb, 8)

    def body(dm_ref, pa_ref, pb_ref, ga_ref, gb_ref, dpa_ref, dpb_ref, dg_ref):
        d = dm_ref[...].astype(_F32)
        sa, sb = _sigmoid(ga_ref[...]), _sigmoid(gb_ref[...])
        dpa_ref[...] = (d * sa).astype(dpa_ref.dtype)
        dpb_ref[...] = (d * sb).astype(dpb_ref.dtype)
        dg_ref[:, :D] = (d * pa_ref[...].astype(_F32) * sa * (1.0 - sa)).astype(dg_ref.dtype)
        dg_ref[:, D:] = (d * pb_ref[...].astype(_F32) * sb * (1.0 - sb)).astype(dg_ref.dtype)

    return pl.pallas_call(
        body, name=name, grid=(T // tb,),
        in_specs=[_rspec(tb, D), _rspec(tb, D), _rspec(tb, D), _rspec(tb, D, 6), _rspec(tb, D, 7)],
        out_specs=[_rspec(tb, D), _rspec(tb, D), _rspec(tb, 2 * D, 3)],
        out_shape=[jax.ShapeDtypeStruct((T, D), _MMT)] * 2 + [jax.ShapeDtypeStruct((T, PW), _MMT)],
        compiler_params=_cp("parallel"),
    )(dm, pa, pb, proj, proj)


def _swiglu_bwd(da, gate, up, name, tb=256):
    T, F = gate.shape
    F2 = 2 * F
    tb = _pick(T, tb, 8)

    def body(da_ref, g_ref, u_ref, o_ref):
        d, g = da_ref[...].astype(_F32), g_ref[...].astype(_F32)
        o_ref[:, :F] = (d * u_ref[...].astype(_F32) * _dsilu(g)).astype(o_ref.dtype)
        o_ref[:, F:] = (d * _silu(g)).astype(o_ref.dtype)

    return pl.pallas_call(
        body, name=name, grid=(T // tb,), in_specs=[_rspec(tb, F), _rspec(tb, F), _rspec(tb, F)],
        out_specs=_rspec(tb, F2), out_shape=jax.ShapeDtypeStruct((T, F2), _MMT), compiler_params=_cp("parallel"),
    )(da, gate, up)


def _loss_head(x, tgt, fg, name, tb=256):
    T, D = x.shape
    tb = _pick(T, tb, 8)

    def body(x_ref, t_ref, fg_ref, loss_ref, dx_ref, dfg_ref):
        @pl.when(pl.program_id(0) == 0)
        def _():
            loss_ref[...] = jnp.zeros_like(loss_ref)
            dfg_ref[...] = jnp.zeros_like(dfg_ref)

        xv, fg_ = x_ref[...], fg_ref[...]
        r = lax.rsqrt(jnp.mean(xv * xv, axis=1, keepdims=True) + _EPS)
        xn = xv * r
        e = xn * fg_ - t_ref[...]
        loss_ref[...] += (0.5 / D) * jnp.sum(jnp.sum(e * e, axis=1, keepdims=True), axis=0, keepdims=True)
        dy = e * (1.0 / D)
        dfg_ref[...] += jnp.sum(dy * xn, axis=0, keepdims=True)
        dxn = dy * fg_
        dx_ref[...] = r * (dxn - xn * jnp.mean(dxn * xn, axis=1, keepdims=True))

    return pl.pallas_call(
        body, name=name, grid=(T // tb,), in_specs=[_rspec(tb, D), _rspec(tb, D), _fspec((1, D))],
        out_specs=[_fspec((1, 1)), _rspec(tb, D), _fspec((1, D))],
        out_shape=[jax.ShapeDtypeStruct((1, 1), _F32), jax.ShapeDtypeStruct((T, D), _F32),
                   jax.ShapeDtypeStruct((1, D), _F32)],
        compiler_params=_cp("arbitrary"),
    )(x, tgt, fg)


def _row_tile(R, W, budget=1 << 20, unit=8):
    if R * W * 4 <= budget or R % unit:
        return R
    best = unit
    for t in range(unit, R + 1, unit):
        if R % t == 0 and t * W * 4 <= budget:
            best = t
    return best


def _add_own_half(send, got, half, name):
    P, Rp, W = send.shape
    Rh = Rp // 2
    tb = _row_tile(Rh, W, 1 << 21, 16)

    def body(h_ref, a_ref, b_ref, o_ref):
        o_ref[...] = (a_ref[...].astype(_F32) + b_ref[...].astype(_F32)).astype(o_ref.dtype)

    return pl.pallas_call(
        body, name=name,
        grid_spec=pltpu.PrefetchScalarGridSpec(
            num_scalar_prefetch=1, grid=(P, Rh // tb),
            in_specs=[pl.BlockSpec((None, None, tb, W), lambda k, i, h: (k, h[0], i, 0)),
                      pl.BlockSpec((None, tb, W), lambda k, i, h: (k, i, 0))],
            out_specs=pl.BlockSpec((None, tb, W), lambda k, i, h: (k, i, 0))),
        out_shape=jax.ShapeDtypeStruct((P, Rh, W), send.dtype), compiler_params=_cp("parallel", "parallel"),
    )(half, send.reshape(P, 2, Rh, W), got)


def _sum_slots(st, name):
    P, R, W = st.shape
    tb = _row_tile(R, W, 1 << 20, 16)

    def body(s_ref, o_ref):
        acc = s_ref[0].astype(_F32)
        for p in range(1, P):
            acc = acc + s_ref[p].astype(_F32)
        o_ref[...] = acc

    return pl.pallas_call(
        body, name=name, grid=(R // tb,), in_specs=[pl.BlockSpec((P, tb, W), lambda i: (0, i, 0))],
        out_specs=_rspec(tb, W), out_shape=jax.ShapeDtypeStruct((R, W), _F32), compiler_params=_cp("parallel"),
    )(st)


def _adamw(w, gst, m, v, name):
    R, W = w.shape
    P = gst.shape[0]
    tb = _row_tile(R, W, 1 << 20)
    c1, c2 = 1.0 - _B1 ** _STEP, 1.0 - _B2 ** _STEP

    def body(w_ref, g_ref, m_ref, v_ref, go_ref, d_ref, mo_ref, vo_ref):
        g = g_ref[0]
        for p in range(1, P):
            g = g + g_ref[p]
        mn = _B1 * m_ref[...] + (1.0 - _B1) * g
        vn = _B2 * v_ref[...] + (1.0 - _B2) * (g * g)
        go_ref[...] = g
        mo_ref[...] = mn
        vo_ref[...] = vn
        d_ref[...] = -_LR * ((mn / c1) / (jnp.sqrt(vn / c2) + _AEPS) + _WD * w_ref[...])

    spec = _rspec(tb, W)
    return pl.pallas_call(
        body, name=name, grid=(R // tb,),
        in_specs=[spec, pl.BlockSpec((P, tb, W), lambda i: (0, i, 0)), spec, spec],
        out_specs=[spec] * 4, out_shape=[jax.ShapeDtypeStruct((R, W), _F32)] * 4, compiler_params=_cp("parallel"),
    )(w, gst, m, v)


def _as2d(a):
    if a.ndim == 1:
        return a.reshape(1, -1)
    return a.reshape(-1, a.shape[-1])


def kernel(x, c, ada_w, ada_b, norm1_g, w_in, conv_w, spatial_w, spatial_b, v_norm_g, a_log, dt_bias, o_norm_g, w_branch_a, w_branch_b, w_out, norm2_g, w_ffn_in, w_ffn_out, final_g, loss_target, m_ada_w, m_ada_b, m_norm1_g, m_w_in, m_conv_w, m_spatial_w, m_spatial_b, m_v_norm_g, m_a_log, m_dt_bias, m_o_norm_g, m_w_branch_a, m_w_branch_b, m_w_out, m_norm2_g, m_w_ffn_in, m_w_ffn_out, m_final_g, v_ada_w, v_ada_b, v_norm1_g, v_w_in, v_conv_w, v_spatial_w, v_spatial_b, v_v_norm_g, v_a_log, v_dt_bias, v_o_norm_g, v_w_branch_a, v_w_branch_b, v_w_out, v_norm2_g, v_w_ffn_in, v_w_ffn_out, v_final_g):
    xb, tgt = x[0], loss_target[0]
    T, D = xb.shape
    L, H, G = ada_w.shape[0], a_log.shape[1], spatial_w.shape[1]
    F = 4 * w_ffn_out.shape[1]
    N = T // _BC
    Ws = ada_w.shape[2]
    Wc = w_in.shape[2]
    PW = 8 * D + _LANE
    ix, iy, ic = lax.axis_index("x"), lax.axis_index("y"), lax.axis_index("c")
    me = 4 * ix + 2 * iy + ic

    c_all = _gather8(c, "gather_c").reshape(8, D)
    modp = _ada_fwd(c_all, ada_w, "ada_fwd")
    n_mod, n_cw = L * 8 * Ws, L * _KC * conv_w.shape[2]
    pad = (-(n_mod + n_cw)) % _LANE
    pay = jnp.concatenate([modp.reshape(-1), conv_w.reshape(-1), jnp.zeros((pad,), _F32)]).reshape(-1, _LANE)
    pay_all = _gather8(pay, "gather_mod").reshape(8, -1)
    mod_full = jnp.concatenate([pay_all[2 * k, :n_mod].reshape(L, 8, Ws) for k in range(4)], axis=-1)
    cw_full = jnp.concatenate([pay_all[2 * k, n_mod:n_mod + n_cw].reshape(L, _KC, -1) for k in range(4)], axis=-1)
    mod = lax.dynamic_index_in_dim(mod_full, me, axis=1, keepdims=False) + ada_b
    mods = [[mod[l, j * D:(j + 1) * D].reshape(1, D) for j in range(6)] for l in range(L)]

    big = [w_in, w_branch_a, w_branch_b, w_out, w_ffn_in, w_ffn_out]
    chip = 2 * ix + iy
    starts = [(k * Wc) // 16 * 16 for k in range(4)]
    Hh = max(-(-((k + 1) * Wc) // 16) * 16 - starts[k] for k in range(4))
    No = max(s + Hh for s in starts)
    my_off = jnp.asarray([k * Wc - starts[k] for k in range(4)], jnp.int32)[chip]
    cuts = sorted(set(starts + [s + Hh for s in starts]))

    pers = [Hh, D // 4, D // 4, D // 4, 2 * F // 4, F // 4]
    roff = [0]
    for p in pers:
        roff.append(roff[-1] + L * p)
    Rp = -(-roff[-1] // (32 * _NCH)) * (32 * _NCH)
    rpad = Rp - roff[-1]

    hull = lax.dynamic_update_slice(jnp.zeros((L, Hh, D), _F32), jnp.transpose(w_in, (0, 2, 1)), (0, my_off, 0))
    shard = jnp.concatenate(
        [hull.reshape(-1, D).astype(_MMT), w_branch_a.reshape(-1, D).astype(_MMT),
         w_branch_b.reshape(-1, D).astype(_MMT), w_out.reshape(-1, D).astype(_MMT),
         jnp.transpose(w_ffn_in, (0, 2, 1)).reshape(-1, D).astype(_MMT), w_ffn_out.reshape(-1, D).astype(_MMT),
         jnp.zeros((rpad, D), _MMT)], axis=0)
    gw = _fill_from_sibling(_gather_chips(shard, "gather_w"), "gather_w_sib")

    def slab(i, l, k):
        a = roff[i] + l * pers[i]
        return gw[k, a:a + pers[i]]

    def joined(i, l):
        return jnp.concatenate([slab(i, l, k) for k in range(4)], axis=0)

    def orig_rows(hulls, a, b):
        edges = sorted(set([a, b] + [c_ for c_ in cuts if a < c_ < b]))
        out = []
        for lo, hi in zip(edges[:-1], edges[1:]):
            cov = [k for k in range(4) if starts[k] <= lo and hi <= starts[k] + Hh]
            piece = hulls[cov[0]][lo - starts[cov[0]]:hi - starts[cov[0]]]
            for k in cov[1:]:
                piece = piece + hulls[k][lo - starts[k]:hi - starts[k]]
            out.append(piece)
        return out

    wt_in_p = []
    for l in range(L):
        hulls = [slab(0, l, k) for k in range(4)]
        wt_in_p.append(jnp.concatenate(
            orig_rows(hulls, 0, 6 * D) + orig_rows(hulls, 6 * D + 2 * H, 8 * D + 2 * H)
            + orig_rows(hulls, 6 * D, 6 * D + 2 * H) + [jnp.zeros((_LANE - 2 * H, D), _MMT)], axis=0))
    w_a, w_b, w_o, wt_fi, w_fo = ([joined(i, l) for l in range(L)] for i in range(1, 6))

    sbt = jnp.transpose(spatial_b, (0, 2, 1))
    gv3, go3 = v_norm_g.reshape(L, 1, D), o_norm_g.reshape(L, 1, _GD)
    zpad = jnp.zeros((L, _LANE - 2 * H), _F32)
    alog_row = jnp.concatenate([jnp.zeros((L, H), _F32), a_log, zpad], axis=1).reshape(L, 1, _LANE)
    dtb_row = jnp.concatenate([jnp.zeros((L, H), _F32), dt_bias, zpad], axis=1).reshape(L, 1, _LANE)

    def rows_of(tok):
        return jnp.transpose(tok.reshape(N, _BC, H), (2, 0, 1)).reshape(H, N, 1, _BC)

    def toks_of(rows):
        return jnp.transpose(rows.reshape(H, N, _BC), (1, 2, 0)).reshape(T, H)

    saved = []
    xc = xb
    for l in range(L):
        sh1, sc1, gt1, sh2, sc2, gt2 = mods[l]
        g1, g2 = norm1_g[l].reshape(1, D), norm2_g[l].reshape(1, D)
        h = _norm_mod(xc, g1, sc1, sh1, f"norm1_{l}")
        proj = _mm(h, wt_in_p[l], f"proj_{l}", trans_b=True, tn=1664)
        ya = _gmlp_fwd(proj, spatial_w, sbt, gv3, l, D, f"gmlp_{l}")
        qn, kn, vs, bg = _conv_fwd(proj, cw_full, alog_row, dtb_row, l, D, f"conv_{l}")
        g_r, b_r = rows_of(bg[:, H:2 * H]), rows_of(bg[:, :H])
        o, s_all, t_all = _gdn_fwd(qn, kn, vs, g_r, b_r, f"gdn_{l}")
        yb = _onorm_fwd(o, proj, go3, l, D, f"onorm_{l}")
        pa, pb, mg = _branch_merge(ya, yb, w_a[l], w_b[l], proj, f"branch_{l}")
        p1, x1 = _mm_res(mg, w_o[l], xc, gt1, f"wout_{l}")
        h2 = _norm_mod(x1, g2, sc2, sh2, f"norm2_{l}")
        gate, up, act = _ffin_swiglu(h2, wt_fi[l], f"ffin_{l}")
        p2, x2 = _mm_res(act, w_fo[l], x1, gt2, f"ffout_{l}")
        saved.append(dict(x=xc, h=h, proj=proj, ya=ya, yb=yb, qn=qn, kn=kn, vs=vs, g_r=g_r, b_r=b_r, o=o,
                          s_all=s_all, t_all=t_all, pa=pa, pb=pb, mg=mg, p1=p1, x1=x1, h2=h2, gate=gate, up=up,
                          act=act, p2=p2))
        xc = x2

    loss11, dx, dfg = _loss_head(xc, tgt, final_g.reshape(1, D), "loss_head")
    loss = lax.psum(loss11[0, 0], ("x", "y", "c"))

    gbig = {k: None for k in ("w_in", "w_a", "w_b", "w_o", "w_fi", "w_fo")}
    small = {k: [None] * L for k in ("dmod", "n1", "n2", "sw", "sb", "gv", "cw", "al", "dt", "go")}
    for l in reversed(range(L)):
        sv = saved[l]
        sh1, sc1, gt1, sh2, sc2, gt2 = mods[l]
        g1, g2 = norm1_g[l].reshape(1, D), norm2_g[l].reshape(1, D)
        proj = sv["proj"]
        dp2, dgt2 = _resid_bwd(dx, sv["p2"], gt2, f"res2b_{l}")
        da = _mm(dp2, w_fo[l], f"ffoutb_{l}", trans_b=True, out_dtype=_MMT)
        gbig["w_fo"] = _mm_tn(sv["act"], dp2, f"ffoutw_{l}", l, L, gbig["w_fo"])
        dgu = _swiglu_bwd(da, sv["gate"], sv["up"], f"swiglub_{l}")
        dx1, dgm2, dsh2 = _mm_normb(dgu, wt_fi[l], sv["x1"], dx, g2, sc2, f"ffinb_{l}")
        gbig["w_fi"] = _mm_tn(dgu, sv["h2"], f"ffinw_{l}", l, L, gbig["w_fi"])
        dp1, dgt1 = _resid_bwd(dx1, sv["p1"], gt1, f"res1b_{l}")
        dmg = _mm(dp1, w_o[l], f"woutb_{l}", trans_b=True, out_dtype=_MMT)
        gbig["w_o"] = _mm_tn(sv["mg"], dp1, f"woutw_{l}", l, L, gbig["w_o"])
        dpa, dpb, dproj = _merge_bwd(dmg, sv["pa"], sv["pb"], proj, D, f"mergeb_{l}")
        dya = _mm(dpa, w_a[l], f"brab_{l}", trans_b=True, out_dtype=_MMT)
        gbig["w_a"] = _mm_tn(sv["ya"], dpa, f"braw_{l}", l, L, gbig["w_a"])
        dyb = _mm(dpb, w_b[l], f"brbb_{l}", trans_b=True, out_dtype=_MMT)
        gbig["w_b"] = _mm_tn(sv["yb"], dpb, f"brbw_{l}", l, L, gbig["w_b"])
        dproj, dsw, dsa, dgv = _gmlp_bwd(proj, dya, spatial_w, sbt, gv3, l, D, dproj, f"gmlpb_{l}")
        do, dproj, dgo = _onorm_bwd(dyb, sv["o"], proj, go3, l, D, dproj, f"onormb_{l}")
        dqn, dkn, dvs, dg_r, db_r = _gdn_bwd(sv["qn"], sv["kn"], sv["vs"], sv["g_r"], sv["b_r"], sv["s_all"],
                                             sv["t_all"], do, f"gdnb_{l}")
        dbg = jnp.concatenate([toks_of(db_r), toks_of(dg_r), jnp.zeros((T, _LANE - 2 * H), _F32)], axis=1)
        dc, dproj, dcw, dal, ddt = _conv_bwd1(proj, dqn, dkn, dvs, dbg, cw_full, alog_row, dtb_row, l, D, dproj,
                                              f"convb_{l}")
        dproj = _conv_bwd2(dc, cw_full, l, dproj, f"convx_{l}")
        gbig["w_in"] = _mm_tn(dproj, sv["h"], f"projw_{l}", l, L, gbig["w_in"], tm=640)
        dx, dgm1, dsh1 = _mm_normb(dproj, wt_in_p[l], sv["x"], dx1, g1, sc1, f"projb_{l}", tk=1664)
        small["dmod"][l] = jnp.concatenate([dsh1, dgm1 * g1, dgt1, dsh2, dgm2 * g2, dgt2], axis=1)
        small["n1"][l], small["n2"][l] = dgm1 * (1.0 + sc1), dgm2 * (1.0 + sc2)
        small["sw"][l], small["gv"][l], small["cw"][l], small["go"][l] = dsw, dgv, dcw, dgo
        small["sb"][l] = jnp.transpose(dsa.reshape(_AC, G, _GD).sum(axis=-1))
        small["al"][l], small["dt"][l] = dal[:, H:2 * H], ddt[:, H:2 * H]
    grad_x = dx.reshape(1, T, D)

    names_small = ["dmod", "n1", "n2", "sw", "sb", "gv", "cw", "al", "dt", "go"]
    flat = [jnp.stack(small[k]).reshape(-1) for k in names_small] + [dfg.reshape(-1)]
    sizes = [f.shape[0] for f in flat]
    tot = sum(sizes)
    pad = (-tot) % 1024
    pay = jnp.concatenate(flat + [jnp.zeros((pad,), _F32)]).reshape(-1, 1024)
    sm_all = _gather8(pay, "gather_small").reshape(8, -1)
    offs = [0]
    for s in sizes:
        offs.append(offs[-1] + s)
    part = {k: sm_all[:, offs[i]:offs[i + 1]] for i, k in enumerate(names_small + ["fg"])}
    dmod_all = part["dmod"].reshape(8, L, 6 * D)

    outs = {}

    def update(nm, w, gst, m, v):
        shp = w.shape
        w2 = _as2d(w)
        g, d, mn, vn = _adamw(w2, gst.reshape((gst.shape[0],) + w2.shape), _as2d(m), _as2d(v), f"adamw_{nm}")
        outs[nm] = (g.reshape(shp), d.reshape(shp), mn.reshape(shp), vn.reshape(shp))

    chip = 2 * ix + iy
    dmod_t = jnp.transpose(dmod_all, (1, 0, 2))
    dmod_mine = lax.dynamic_slice_in_dim(dmod_t, chip * Ws, Ws, axis=2)
    g_ada_w = _ada_bwd(jnp.transpose(c_all), dmod_mine, "ada_bwd")
    update("ada_w", ada_w, g_ada_w[None], m_ada_w, v_ada_w)
    update("ada_b", ada_b, dmod_all, m_ada_b, v_ada_b)
    update("norm1_g", norm1_g, part["n1"], m_norm1_g, v_norm1_g)
    update("norm2_g", norm2_g, part["n2"], m_norm2_g, v_norm2_g)
    update("spatial_w", spatial_w, part["sw"], m_spatial_w, v_spatial_w)
    update("spatial_b", spatial_b, part["sb"], m_spatial_b, v_spatial_b)
    update("v_norm_g", v_norm_g, part["gv"], m_v_norm_g, v_v_norm_g)
    update("a_log", a_log, part["al"], m_a_log, v_a_log)
    update("dt_bias", dt_bias, part["dt"], m_dt_bias, v_dt_bias)
    update("o_norm_g", o_norm_g, part["go"], m_o_norm_g, v_o_norm_g)
    update("final_g", final_g, part["fg"], m_final_g, v_final_g)
    cw_cols = conv_w.shape[2]
    dcw_all = part["cw"].reshape(8, L, _KC, 4 * cw_cols)
    update("conv_w", conv_w, lax.dynamic_slice_in_dim(dcw_all, chip * cw_cols, cw_cols, axis=3), m_conv_w, v_conv_w)

    def hull_of(p, k):
        a, b = starts[k], starts[k] + Hh
        out = []
        for lo, hi, plo in ((0, 6 * D, 0), (6 * D, 6 * D + 2 * H, 8 * D), (6 * D + 2 * H, 8 * D + 2 * H, 6 * D),
                            (8 * D + 2 * H, No, None)):
            s, e = max(a, lo), min(b, hi)
            if s < e:
                out.append(jnp.zeros((L, e - s, D), _MMT) if plo is None else p[:, plo + s - lo:plo + e - lo])
        return (out[0] if len(out) == 1 else jnp.concatenate(out, axis=1)).reshape(L * Hh, D)

    pieces = []
    for k in range(4):
        pieces.append(hull_of(gbig["w_in"], k))
        for i, nm in enumerate(("w_a", "w_b", "w_o", "w_fi", "w_fo")):
            per = pers[i + 1]
            pieces.append(gbig[nm][:, k * per:(k + 1) * per].reshape(L * per, D))
        pieces.append(jnp.zeros((rpad, D), _MMT))
    send = jnp.concatenate(pieces, axis=0).reshape(4, Rp, D)
    got = _send_half_to_sibling(send, "reduce_cores")
    chipsum = _add_own_half(send, got, ic.astype(jnp.int32).reshape(1), "add_cores")
    parts = _scatter_to_chips(chipsum, "reduce_chips")
    mine = _sum_slots(parts, "add_chips")
    other = _swap_with_sibling(mine, "swap_cores")
    first = ic == 0
    gsum = jnp.concatenate([jnp.where(first, mine, other), jnp.where(first, other, mine)], axis=0)
    big_names = ["w_in", "w_branch_a", "w_branch_b", "w_out", "w_ffn_in", "w_ffn_out"]
    big_m = [m_w_in, m_w_branch_a, m_w_branch_b, m_w_out, m_w_ffn_in, m_w_ffn_out]
    big_v = [v_w_in, v_w_branch_a, v_w_branch_b, v_w_out, v_w_ffn_in, v_w_ffn_out]
    for i, (nm, w, m, v) in enumerate(zip(big_names, big, big_m, big_v)):
        g = gsum[roff[i]:roff[i + 1]].reshape(L, pers[i], D)
        if i == 0:
            g = jnp.transpose(lax.dynamic_slice_in_dim(g, my_off, Wc, axis=1), (0, 2, 1))
        elif i == 4:
            g = jnp.transpose(g, (0, 2, 1))
        update(nm, w, g[None], m, v)

    order = ["ada_w", "ada_b", "norm1_g", "w_in", "conv_w", "spatial_w", "spatial_b", "v_norm_g", "a_log", "dt_bias",
             "o_norm_g", "w_branch_a", "w_branch_b", "w_out", "norm2_g", "w_ffn_in", "w_ffn_out", "final_g"]
    return (loss, grad_x, *[outs[n][0] for n in order], *[outs[n][1] for n in order],
            *[outs[n][2] for n in order], *[outs[n][3] for n in order])
```

```python
import functools
import math

import jax
import jax.numpy as jnp
from jax import lax
from jax.experimental import pallas as pl
from jax.experimental.pallas import tpu as pltpu

_F32 = jnp.float32
_BF = jnp.bfloat16
_MMT = jnp.bfloat16
_EPS = 1e-6
_GD = 128
_AC = 128
_BC = 64
_KC = 4
_HB = 8
_NCH = 8
_LANE = 128
_VMEM_LIMIT = 56 * 1024 * 1024

_LR, _B1, _B2, _AEPS, _WD, _STEP = 0.001, 0.9, 0.999, 1e-08, 0.01, 10

_NN = (((1,), (0,)), ((), ()))
_NT = (((1,), (1,)), ((), ()))
_TN = (((0,), (0,)), ((), ()))

_MESH = pl.DeviceIdType.MESH


def _cp(*sem):
    return pltpu.CompilerParams(dimension_semantics=tuple(sem), vmem_limit_bytes=_VMEM_LIMIT)


def _dot(a, b, dn=_NN):
    return lax.dot_general(a.astype(_MMT), b.astype(_MMT), dn, preferred_element_type=_F32)


def _pick(n, target, unit=_LANE):
    if n <= target:
        return n
    best = None
    for t in range(unit, target + 1, unit):
        if n % t == 0:
            best = t
    assert best is not None, (n, target)
    return best


def _sigmoid(x):
    return 0.5 * jnp.tanh(0.5 * x) + 0.5


def _silu(x):
    return x * _sigmoid(x)


def _dsilu(x):
    s = _sigmoid(x)
    return s * (1.0 + x * (1.0 - s))


_GK = math.sqrt(2.0 / math.pi)


def _gelu(x):
    return 0.5 * x * (1.0 + jnp.tanh(_GK * (x + 0.044715 * x * x * x)))


def _dgelu(x):
    t = jnp.tanh(_GK * (x + 0.044715 * x * x * x))
    return 0.5 * (1.0 + t) + 0.5 * x * (1.0 - t * t) * _GK * (1.0 + 3.0 * 0.044715 * x * x)


def _softplus(x):
    return jnp.maximum(x, 0.0) + jnp.log(1.0 + jnp.exp(-jnp.abs(x)))


def _rspec(tb, w, cb=0):
    return pl.BlockSpec((tb, w), lambda i: (i, cb))


def _fspec(shape):
    nd = len(shape)
    return pl.BlockSpec(tuple(shape), lambda i: (0,) * nd)


def _lspec(tail, li):
    nd = len(tail)
    return pl.BlockSpec((None,) + tuple(tail), lambda i: (li,) + (0,) * nd)


def _slot_all8(x, y, c):
    return 4 * x + 2 * y + c


def _gather8(v, name):
    R, W = v.shape

    def body(v_ref, o_ref, ssem, rsem, lsem):
        x, y, c = lax.axis_index("x"), lax.axis_index("y"), lax.axis_index("c")
        sib = (x, y, 1 - c)
        chips = _other_chips(x, y)

        def slot(px, py, pc):
            return o_ref.at[_slot_all8(px, py, pc)]

        own = pltpu.make_async_copy(v_ref, slot(x, y, c), lsem)
        own.start()
        started = [_rcopy(v_ref, slot(x, y, c), ssem.at[0], rsem.at[0], sib)]
        started += [_rcopy(v_ref, slot(x, y, c), ssem.at[1 + j], rsem.at[1 + j], (px, py, c))
                    for j, (px, py) in enumerate(chips)]
        for cp in started:
            cp.start()
        for j, (px, py) in enumerate(chips):
            blk = slot(px, py, c)
            _rcopy(blk, blk, ssem.at[1 + j], rsem.at[1 + j], (px, py, c)).wait_recv()
            fw = _rcopy(blk, blk, ssem.at[4 + j], rsem.at[4 + j], sib)
            fw.start()
            started.append(fw)
        blk = slot(x, y, 1 - c)
        _rcopy(blk, blk, ssem.at[0], rsem.at[0], sib).wait_recv()
        for j, (px, py) in enumerate(chips):
            blk = slot(px, py, 1 - c)
            _rcopy(blk, blk, ssem.at[4 + j], rsem.at[4 + j], sib).wait_recv()
        for cp in started:
            cp.wait_send()
        own.wait()

    return pl.pallas_call(
        body, name=name, out_shape=jax.ShapeDtypeStruct((8, R, W), v.dtype), in_specs=[_HBM], out_specs=_HBM,
        scratch_shapes=[pltpu.SemaphoreType.DMA((7,)), pltpu.SemaphoreType.DMA((7,)), pltpu.SemaphoreType.DMA],
    )(v)


def _rcopy(src, dst, ssem, rsem, dev):
    return pltpu.make_async_remote_copy(src_ref=src, dst_ref=dst, send_sem=ssem, recv_sem=rsem,
                                        device_id=dev, device_id_type=_MESH)


def _other_chips(x, y):
    return [(1 - x, y), (x, 1 - y), (1 - x, 1 - y)]


_HBM = pl.BlockSpec(memory_space=pl.ANY)


def _gather_chips(shard, name):
    Rp, W = shard.shape
    Rh = Rp // 2
    rc = Rh // _NCH
    hq = _NCH // 2

    def body(s_ref, o_ref, ssem, rsem, lsem):
        x, y, c = lax.axis_index("x"), lax.axis_index("y"), lax.axis_index("c")
        chip = 2 * x + y
        xn, yn, dg = _other_chips(x, y)
        cx, cy, cd = 2 * xn[0] + xn[1], 2 * yn[0] + yn[1], 2 * dg[0] + dg[1]

        def rows(q):
            return pl.ds(c * Rh + q * rc, rc)

        locs = []
        for q in range(_NCH):
            lc = pltpu.make_async_copy(s_ref.at[rows(q)], o_ref.at[chip, rows(q)], lsem.at[q])
            lc.start()
            locs.append(lc)
        started = []
        for q in range(_NCH):
            for j, nb in ((0, xn), (1, yn)):
                cp = _rcopy(s_ref.at[rows(q)], o_ref.at[chip, rows(q)], ssem.at[j * _NCH + q], rsem.at[j * _NCH + q],
                            (nb[0], nb[1], c))
                cp.start()
                started.append(cp)
        for q in range(_NCH):
            bx = o_ref.at[cx, rows(q)]
            _rcopy(bx, bx, ssem.at[q], rsem.at[q], (xn[0], xn[1], c)).wait_recv()
            if q >= hq:
                rl = _rcopy(bx, bx, ssem.at[2 * _NCH + q], rsem.at[2 * _NCH + q], (yn[0], yn[1], c))
                rl.start()
                started.append(rl)
            by = o_ref.at[cy, rows(q)]
            _rcopy(by, by, ssem.at[_NCH + q], rsem.at[_NCH + q], (yn[0], yn[1], c)).wait_recv()
            if q < hq:
                rl = _rcopy(by, by, ssem.at[2 * _NCH + q], rsem.at[2 * _NCH + q], (xn[0], xn[1], c))
                rl.start()
                started.append(rl)
        for q in range(_NCH):
            bd = o_ref.at[cd, rows(q)]
            _rcopy(bd, bd, ssem.at[2 * _NCH + q], rsem.at[2 * _NCH + q], (dg[0], dg[1], c)).wait_recv()
        for cp in started:
            cp.wait_send()
        for lc in locs:
            lc.wait()

    return pl.pallas_call(
        body, name=name, out_shape=jax.ShapeDtypeStruct((4, Rp, W), shard.dtype), in_specs=[_HBM], out_specs=_HBM,
        scratch_shapes=[pltpu.SemaphoreType.DMA((3 * _NCH,))] * 2 + [pltpu.SemaphoreType.DMA((_NCH,))],
    )(shard)


def _fill_from_sibling(buf, name):
    P, Rp, W = buf.shape
    Rh = Rp // 2
    rc = Rh // _NCH

    def body(s_ref, o_ref, ssem, rsem):
        x, y, c = lax.axis_index("x"), lax.axis_index("y"), lax.axis_index("c")
        cps = []
        for k in range(P):
            for q in range(_NCH):
                r = pl.ds(c * Rh + q * rc, rc)
                cp = _rcopy(s_ref.at[k, r], o_ref.at[k, r], ssem.at[k * _NCH + q], rsem.at[k * _NCH + q],
                            (x, y, 1 - c))
                cp.start()
                cps.append(cp)
        for k in range(P):
            for q in range(_NCH):
                blk = o_ref.at[k, pl.ds((1 - c) * Rh + q * rc, rc)]
                _rcopy(blk, blk, ssem.at[k * _NCH + q], rsem.at[k * _NCH + q], (x, y, 1 - c)).wait_recv()
        for cp in cps:
            cp.wait_send()

    return pl.pallas_call(
        body, name=name, out_shape=jax.ShapeDtypeStruct(buf.shape, buf.dtype), in_specs=[_HBM], out_specs=_HBM,
        scratch_shapes=[pltpu.SemaphoreType.DMA((P * _NCH,))] * 2, input_output_aliases={0: 0},
    )(buf)


def _send_half_to_sibling(send, name):
    P, Rp, W = send.shape
    Rh = Rp // 2
    rc = Rh // _NCH

    def body(s_ref, o_ref, ssem, rsem):
        x, y, c = lax.axis_index("x"), lax.axis_index("y"), lax.axis_index("c")
        cps = []
        for k in range(P):
            for q in range(_NCH):
                cp = _rcopy(s_ref.at[k, pl.ds((1 - c) * Rh + q * rc, rc)], o_ref.at[k, pl.ds(q * rc, rc)],
                            ssem.at[k * _NCH + q], rsem.at[k * _NCH + q], (x, y, 1 - c))
                cp.start()
                cps.append(cp)
        for cp in cps:
            cp.wait()

    return pl.pallas_call(
        body, name=name, out_shape=jax.ShapeDtypeStruct((P, Rh, W), send.dtype), in_specs=[_HBM], out_specs=_HBM,
        scratch_shapes=[pltpu.SemaphoreType.DMA((P * _NCH,))] * 2,
    )(send)


def _scatter_to_chips(cs, name):
    P, Rh, W = cs.shape
    rc = Rh // _NCH

    def body(s_ref, o_ref, ssem, rsem, lsem):
        x, y, c = lax.axis_index("x"), lax.axis_index("y"), lax.axis_index("c")
        chip = 2 * x + y
        peers = _other_chips(x, y)
        locs = []
        for q in range(_NCH):
            r = pl.ds(q * rc, rc)
            lc = pltpu.make_async_copy(s_ref.at[chip, r], o_ref.at[chip, r], lsem.at[q])
            lc.start()
            locs.append(lc)
        cps = []
        for j, (px, py) in enumerate(peers):
            for q in range(_NCH):
                r = pl.ds(q * rc, rc)
                cp = _rcopy(s_ref.at[2 * px + py, r], o_ref.at[chip, r], ssem.at[j * _NCH + q], rsem.at[j * _NCH + q],
                            (px, py, c))
                cp.start()
                cps.append(cp)
        for j, (px, py) in enumerate(peers):
            for q in range(_NCH):
                blk = o_ref.at[2 * px + py, pl.ds(q * rc, rc)]
                _rcopy(blk, blk, ssem.at[j * _NCH + q], rsem.at[j * _NCH + q], (px, py, c)).wait_recv()
        for cp in cps:
            cp.wait_send()
        for lc in locs:
            lc.wait()

    return pl.pallas_call(
        body, name=name, out_shape=jax.ShapeDtypeStruct((P, Rh, W), cs.dtype), in_specs=[_HBM], out_specs=_HBM,
        scratch_shapes=[pltpu.SemaphoreType.DMA((3 * _NCH,))] * 2 + [pltpu.SemaphoreType.DMA((_NCH,))],
    )(cs)


def _swap_with_sibling(v, name):
    R, W = v.shape
    rc = R // _NCH

    def body(s_ref, o_ref, ssem, rsem):
        x, y, c = lax.axis_index("x"), lax.axis_index("y"), lax.axis_index("c")
        cps = []
        for q in range(_NCH):
            r = pl.ds(q * rc, rc)
            cp = _rcopy(s_ref.at[r], o_ref.at[r], ssem.at[q], rsem.at[q], (x, y, 1 - c))
            cp.start()
            cps.append(cp)
        for cp in cps:
            cp.wait()

    return pl.pallas_call(
        body, name=name, out_shape=jax.ShapeDtypeStruct((R, W), v.dtype), in_specs=[_HBM], out_specs=_HBM,
        scratch_shapes=[pltpu.SemaphoreType.DMA((_NCH,))] * 2,
    )(v)


def _mm(a, b, name, li=None, trans_b=False, out_dtype=_F32, tm=1024, tn=1024, tk=2048):
    M, K = a.shape
    bs = b.shape[-2:]
    N = bs[0] if trans_b else bs[1]
    tm, tn, tk = _pick(M, tm, 8), _pick(N, tn), _pick(K, tk)
    nk = K // tk
    lead = () if li is None else (None,)

    def bmap(i, j, k):
        idx = (j, k) if trans_b else (k, j)
        return idx if li is None else (li,) + idx

    def body(a_ref, b_ref, o_ref, acc):
        k = pl.program_id(2)
        part = lax.dot_general(a_ref[...], b_ref[...], _NT if trans_b else _NN, preferred_element_type=_F32)
        if nk == 1:
            o_ref[...] = part.astype(o_ref.dtype)
        else:
            @pl.when(k == 0)
            def _():
                acc[...] = part

            @pl.when(k > 0)
            def _():
                acc[...] += part

            @pl.when(k == nk - 1)
            def _():
                o_ref[...] = acc[...].astype(o_ref.dtype)

    return pl.pallas_call(
        body, name=name, grid=(M // tm, N // tn, nk),
        in_specs=[pl.BlockSpec((tm, tk), lambda i, j, k: (i, k)),
                  pl.BlockSpec(lead + ((tn, tk) if trans_b else (tk, tn)), bmap)],
        out_specs=pl.BlockSpec((tm, tn), lambda i, j, k: (i, j)),
        out_shape=jax.ShapeDtypeStruct((M, N), out_dtype),
        scratch_shapes=[pltpu.VMEM((tm, tn) if nk > 1 else (8, _LANE), _F32)],
        compiler_params=_cp("parallel", "parallel", "arbitrary"),
    )(a, b)


def _mm_tn(a, b, name, li, nl, into=None, row0=0, mtot=None, tm=512, tn=1024):
    T, M = a.shape
    N = b.shape[1]
    tm, tn = _pick(M, tm), _pick(N, tn)
    mtot = M if mtot is None else mtot
    assert row0 % tm == 0
    r0 = row0 // tm

    def body(a_ref, b_ref, *rest):
        o_ref = rest[-1]
        o_ref[...] = lax.dot_general(a_ref[...], b_ref[...], _TN, preferred_element_type=_F32).astype(o_ref.dtype)

    ins = [pl.BlockSpec((T, tm), lambda i, j: (0, i)), pl.BlockSpec((T, tn), lambda i, j: (0, j))]
    return pl.pallas_call(
        body, name=name, grid=(M // tm, N // tn),
        in_specs=ins if into is None else ins + [_HBM],
        out_specs=pl.BlockSpec((None, tm, tn), lambda i, j: (li, i + r0, j)),
        out_shape=jax.ShapeDtypeStruct((nl, mtot, N), _MMT),
        input_output_aliases={} if into is None else {2: 0},
        compiler_params=_cp("parallel", "parallel"),
    )(*((a, b) if into is None else (a, b, into)))


def _mm_res(a, b, x, gt, name, tm=1024, tn=1024, tk=2048):
    M, K = a.shape
    N = b.shape[1]
    tm, tn, tk = _pick(M, tm, 8), _pick(N, tn), _pick(K, tk)
    nk = K // tk

    def body(a_ref, b_ref, x_ref, gt_ref, p_ref, o_ref):
        k = pl.program_id(2)
        part = lax.dot_general(a_ref[...], b_ref[...], _NN, preferred_element_type=_F32)

        @pl.when(k == 0)
        def _():
            p_ref[...] = part

        @pl.when(k > 0)
        def _():
            p_ref[...] += part

        @pl.when(k == nk - 1)
        def _():
            o_ref[...] = x_ref[...] + gt_ref[...] * p_ref[...]

    tile = pl.BlockSpec((tm, tn), lambda i, j, k: (i, j))
    return pl.pallas_call(
        body, name=name, grid=(M // tm, N // tn, nk),
        in_specs=[pl.BlockSpec((tm, tk), lambda i, j, k: (i, k)), pl.BlockSpec((tk, tn), lambda i, j, k: (k, j)),
                  tile, pl.BlockSpec((1, tn), lambda i, j, k: (0, j))],
        out_specs=[tile, tile], out_shape=[jax.ShapeDtypeStruct((M, N), _F32)] * 2,
        compiler_params=_cp("parallel", "parallel", "arbitrary"),
    )(a, b, x, gt)


def _mm_normb(a, b, x, dres, g, sc, name, tm=1024, tk=2048):
    segs = tuple(a) if isinstance(a, (tuple, list)) else (a,)
    ns = len(segs)
    M, K1 = segs[0].shape
    D = b.shape[1]
    tm, tk = _pick(M, tm, 8), _pick(K1, tk)
    nh = K1 // tk
    nk = ns * nh

    def body(*refs):
        a_refs = refs[:ns]
        b_ref, x_ref, dr_ref, g_ref, sc_ref, dx_ref, dgm_ref, dsh_ref, acc = refs[ns:]
        i, k = pl.program_id(0), pl.program_id(1)

        @pl.when(k == 0)
        def _():
            acc[...] = jnp.zeros_like(acc)

        for s in range(ns):
            @pl.when(jnp.logical_and(k >= s * nh, k < (s + 1) * nh))
            def _():
                acc[...] += lax.dot_general(a_refs[s][...], b_ref[...], _NN, preferred_element_type=_F32)

        @pl.when(jnp.logical_and(i == 0, k == 0))
        def _():
            dgm_ref[...] = jnp.zeros_like(dgm_ref)
            dsh_ref[...] = jnp.zeros_like(dsh_ref)

        @pl.when(k == nk - 1)
        def _():
            dh_, xv = acc[...], x_ref[...]
            r = lax.rsqrt(jnp.mean(xv * xv, axis=1, keepdims=True) + _EPS)
            xn = xv * r
            dxn = dh_ * (g_ref[...] * (1.0 + sc_ref[...]))
            dx_ref[...] = dr_ref[...] + r * (dxn - xn * jnp.mean(dxn * xn, axis=1, keepdims=True))
            dgm_ref[...] += jnp.sum(dh_ * xn, axis=0, keepdims=True)
            dsh_ref[...] += jnp.sum(dh_, axis=0, keepdims=True)

    row = pl.BlockSpec((tm, D), lambda i, k: (i, 0))
    vec = pl.BlockSpec((1, D), lambda i, k: (0, 0))
    a_specs = [pl.BlockSpec((tm, tk), lambda i, k, s=s: (i, jnp.clip(k - s * nh, 0, nh - 1))) for s in range(ns)]
    return pl.pallas_call(
        body, name=name, grid=(M // tm, nk),
        in_specs=a_specs + [pl.BlockSpec((tk, D), lambda i, k: (k, 0)), row, row, vec, vec],
        out_specs=[row, vec, vec],
        out_shape=[jax.ShapeDtypeStruct((M, D), _F32), jax.ShapeDtypeStruct((1, D), _F32),
                   jax.ShapeDtypeStruct((1, D), _F32)],
        scratch_shapes=[pltpu.VMEM((tm, D), _F32)],
        compiler_params=_cp("arbitrary", "arbitrary"),
    )(*segs, b, x, dres, g, sc)


def _ffoutb_swiglu(dp, w, gate, up, name, tm=1024, tn=1408):
    M, K = dp.shape
    F = w.shape[0]
    tm, tn = _pick(M, tm, 8), _pick(F, tn)

    def body(a_ref, b_ref, g_ref, u_ref, dg_ref, du_ref):
        d = lax.dot_general(a_ref[...], b_ref[...], _NT, preferred_element_type=_F32)
        gv = g_ref[...].astype(_F32)
        dg_ref[...] = (d * u_ref[...].astype(_F32) * _dsilu(gv)).astype(dg_ref.dtype)
        du_ref[...] = (d * _silu(gv)).astype(du_ref.dtype)

    tile = pl.BlockSpec((tm, tn), lambda i, j: (i, j))
    return pl.pallas_call(
        body, name=name, grid=(M // tm, F // tn),
        in_specs=[pl.BlockSpec((tm, K), lambda i, j: (i, 0)), pl.BlockSpec((tn, K), lambda i, j: (j, 0)), tile, tile],
        out_specs=[tile, tile], out_shape=[jax.ShapeDtypeStruct((M, F), _MMT)] * 2,
        compiler_params=_cp("parallel", "parallel"),
    )(dp, w, gate, up)


def _ffin_swiglu(a, wt, name, tm=1024, tn=1408):
    M, K = a.shape
    F = wt.shape[0] // 2
    tm, tn = _pick(M, tm, 8), _pick(F, tn)
    nj = F // tn

    def body(a_ref, bg_ref, bu_ref, g_ref, u_ref, act_ref):
        av = a_ref[...]
        g = lax.dot_general(av, bg_ref[...], _NT, preferred_element_type=_F32)
        u = lax.dot_general(av, bu_ref[...], _NT, preferred_element_type=_F32)
        g_ref[...] = g.astype(g_ref.dtype)
        u_ref[...] = u.astype(u_ref.dtype)
        act_ref[...] = (_silu(g) * u).astype(act_ref.dtype)

    tile = pl.BlockSpec((tm, tn), lambda i, j: (i, j))
    return pl.pallas_call(
        body, name=name, grid=(M // tm, nj),
        in_specs=[pl.BlockSpec((tm, K), lambda i, j: (i, 0)), pl.BlockSpec((tn, K), lambda i, j: (j, 0)),
                  pl.BlockSpec((tn, K), lambda i, j: (j + nj, 0))],
        out_specs=[tile] * 3, out_shape=[jax.ShapeDtypeStruct((M, F), _MMT)] * 3,
        compiler_params=_cp("parallel", "parallel"),
    )(a, wt, wt)


def _branch_merge(ya, yb, w_a, w_b, proj, name, tm=1024, tn=512):
    M, K = ya.shape
    N = w_a.shape[1]
    tm, tn = _pick(M, tm, 8), _pick(N, tn)
    nj = N // tn

    def body(ya_ref, yb_ref, wa_ref, wb_ref, ga_ref, gb_ref, pa_ref, pb_ref, m_ref):
        pa = lax.dot_general(ya_ref[...], wa_ref[...], _NN, preferred_element_type=_F32)
        pb = lax.dot_general(yb_ref[...], wb_ref[...], _NN, preferred_element_type=_F32)
        pa_ref[...] = pa.astype(pa_ref.dtype)
        pb_ref[...] = pb.astype(pb_ref.dtype)
        m_ref[...] = (_sigmoid(ga_ref[...]) * pa + _sigmoid(gb_ref[...]) * pb).astype(m_ref.dtype)

    row = pl.BlockSpec((tm, K), lambda i, j: (i, 0))
    col = pl.BlockSpec((K, tn), lambda i, j: (0, j))
    tile = pl.BlockSpec((tm, tn), lambda i, j: (i, j))
    return pl.pallas_call(
        body, name=name, grid=(M // tm, nj),
        in_specs=[row, row, col, col, pl.BlockSpec((tm, tn), lambda i, j: (i, 6 * nj + j)),
                  pl.BlockSpec((tm, tn), lambda i, j: (i, 7 * nj + j))],
        out_specs=[tile] * 3, out_shape=[jax.ShapeDtypeStruct((M, N), _MMT)] * 3,
        compiler_params=_cp("parallel", "parallel"),
    )(ya, yb, w_a, w_b, proj, proj)


def _ada_fwd(c_all, ada_w, name):
    L, D, Ws = ada_w.shape
    B = c_all.shape[0]

    def body(c_ref, w_ref, o_ref):
        o_ref[...] = _dot(_silu(c_ref[...]), w_ref[...])

    return pl.pallas_call(
        body, name=name, grid=(L,),
        in_specs=[_fspec((B, D)), pl.BlockSpec((None, D, Ws), lambda l: (l, 0, 0))],
        out_specs=pl.BlockSpec((None, B, Ws), lambda l: (l, 0, 0)),
        out_shape=jax.ShapeDtypeStruct((L, B, Ws), _F32), compiler_params=_cp("parallel"),
    )(c_all, ada_w)


def _ada_bwd(c_all_t, dmod, name):
    D, B = c_all_t.shape
    L, _, Ws = dmod.shape

    def body(c_ref, d_ref, o_ref):
        ct = _silu(c_ref[...])
        d = d_ref[...]
        acc = ct[:, 0:1] * d[0:1, :]
        for b in range(1, B):
            acc = acc + ct[:, b:b + 1] * d[b:b + 1, :]
        o_ref[...] = acc

    return pl.pallas_call(
        body, name=name, grid=(L,),
        in_specs=[_fspec((D, B)), pl.BlockSpec((None, B, Ws), lambda l: (l, 0, 0))],
        out_specs=pl.BlockSpec((None, D, Ws), lambda l: (l, 0, 0)),
        out_shape=jax.ShapeDtypeStruct((L, D, Ws), _F32), compiler_params=_cp("parallel"),
    )(c_all_t, dmod)


def _norm_mod(x, g, sc, sh, name, tb=512):
    T, D = x.shape
    tb = _pick(T, tb, 8)

    def body(x_ref, g_ref, sc_ref, sh_ref, h_ref):
        xv = x_ref[...]
        r = lax.rsqrt(jnp.mean(xv * xv, axis=1, keepdims=True) + _EPS)
        h_ref[...] = (xv * r * (g_ref[...] * (1.0 + sc_ref[...])) + sh_ref[...]).astype(h_ref.dtype)

    return pl.pallas_call(
        body, name=name, grid=(T // tb,),
        in_specs=[_rspec(tb, D), _fspec((1, D)), _fspec((1, D)), _fspec((1, D))],
        out_specs=_rspec(tb, D), out_shape=jax.ShapeDtypeStruct((T, D), _MMT), compiler_params=_cp("parallel"),
    )(x, g, sc, sh)


def _resid_bwd(dx, p, gt, name, tb=512):
    T, D = dx.shape
    tb = _pick(T, tb, 8)

    def body(dx_ref, p_ref, gt_ref, dp_ref, dgt_ref):
        i = pl.program_id(0)
        d = dx_ref[...]
        dp_ref[...] = (d * gt_ref[...]).astype(dp_ref.dtype)

        @pl.when(i == 0)
        def _():
            dgt_ref[...] = jnp.zeros_like(dgt_ref)

        dgt_ref[...] += jnp.sum(d * p_ref[...], axis=0, keepdims=True)

    return pl.pallas_call(
        body, name=name, grid=(T // tb,), in_specs=[_rspec(tb, D), _rspec(tb, D), _fspec((1, D))],
        out_specs=[_rspec(tb, D), _fspec((1, D))],
        out_shape=[jax.ShapeDtypeStruct((T, D), _MMT), jax.ShapeDtypeStruct((1, D), _F32)],
        compiler_params=_cp("arbitrary"),
    )(dx, p, gt)


def _gmlp_chunk(u_raw, v_raw, sw_ref, sbt, gv, G):
    u, v = _gelu(u_raw), _gelu(v_raw)
    ii = lax.broadcasted_iota(jnp.int32, (_AC, _AC), 0)
    jj = lax.broadcasted_iota(jnp.int32, (_AC, _AC), 1)
    out = []
    for gi in range(G):
        sl = slice(gi * _GD, (gi + 1) * _GD)
        vg = v[:, sl]
        r = lax.rsqrt(jnp.mean(vg * vg, axis=1, keepdims=True) + _EPS)
        vhat = vg * r
        W = jnp.where(jj <= ii, sw_ref[gi], 0.0)
        s = _dot(W, vhat * gv[:, sl]) + sbt[:, gi:gi + 1]
        out.append((u[:, sl], s, vhat, r, W))
    return out


def _gmlp_fwd(proj, sw, sbt, gv, li, D, name):
    T = proj.shape[0]
    G = D // _GD

    def body(u_ref, v_ref, sw_ref, sbt_ref, gv_ref, y_ref):
        parts = _gmlp_chunk(u_ref[...], v_ref[...], sw_ref, sbt_ref[...], gv_ref[...], G)
        for gi, (u, s, _, _, _) in enumerate(parts):
            y_ref[:, gi * _GD:(gi + 1) * _GD] = (u * s).astype(y_ref.dtype)

    return pl.pallas_call(
        body, name=name, grid=(T // _AC,),
        in_specs=[_rspec(_AC, D, 0), _rspec(_AC, D, 1), _lspec((G, _AC, _AC), li), _lspec((_AC, G), li),
                  _lspec((1, D), li)],
        out_specs=_rspec(_AC, D), out_shape=jax.ShapeDtypeStruct((T, D), _MMT), compiler_params=_cp("parallel"),
    )(proj, proj, sw, sbt, gv)


def _gmlp_bwd(proj, dy, sw, sbt, gv, li, D, into, name):
    T = proj.shape[0]
    G = D // _GD

    def body(u_ref, v_ref, dy_ref, sw_ref, sbt_ref, gv_ref, _, duv_ref, dsw_ref, dsa_ref, dgv_ref):
        i = pl.program_id(0)

        @pl.when(i == 0)
        def _():
            dsw_ref[...] = jnp.zeros_like(dsw_ref)
            dsa_ref[...] = jnp.zeros_like(dsa_ref)
            dgv_ref[...] = jnp.zeros_like(dgv_ref)

        u_raw, v_raw, dy_, gv_ = u_ref[...], v_ref[...], dy_ref[...].astype(_F32), gv_ref[...]
        parts = _gmlp_chunk(u_raw, v_raw, sw_ref, sbt_ref[...], gv_, G)
        ii = lax.broadcasted_iota(jnp.int32, (_AC, _AC), 0)
        jj = lax.broadcasted_iota(jnp.int32, (_AC, _AC), 1)
        dgu, dgv = _dgelu(u_raw), _dgelu(v_raw)
        for gi, (u, s, vhat, r, W) in enumerate(parts):
            sl = slice(gi * _GD, (gi + 1) * _GD)
            dyg = dy_[:, sl]
            ds = dyg * u
            vn = vhat * gv_[:, sl]
            dsw_ref[gi] += jnp.where(jj <= ii, _dot(ds, vn, _NT), 0.0)
            dsa_ref[:, sl] += ds
            dvn = _dot(W, ds, _TN)
            dgv_ref[:, sl] += jnp.sum(dvn * vhat, axis=0, keepdims=True)
            dvh = dvn * gv_[:, sl]
            dvg = r * (dvh - vhat * jnp.mean(dvh * vhat, axis=1, keepdims=True))
            duv_ref[:, sl] = (dyg * s * dgu[:, sl]).astype(duv_ref.dtype)
            duv_ref[:, D + gi * _GD:D + (gi + 1) * _GD] = (dvg * dgv[:, sl]).astype(duv_ref.dtype)

    return pl.pallas_call(
        body, name=name, grid=(T // _AC,),
        in_specs=[_rspec(_AC, D, 0), _rspec(_AC, D, 1), _rspec(_AC, D), _lspec((G, _AC, _AC), li),
                  _lspec((_AC, G), li), _lspec((1, D), li), _HBM],
        out_specs=[_rspec(_AC, 2 * D), _fspec((G, _AC, _AC)), _fspec((_AC, D)), _fspec((1, D))],
        out_shape=[jax.ShapeDtypeStruct(into.shape, into.dtype), jax.ShapeDtypeStruct((G, _AC, _AC), _F32),
                   jax.ShapeDtypeStruct((_AC, D), _F32), jax.ShapeDtypeStruct((1, D), _F32)],
        input_output_aliases={6: 0}, compiler_params=_cp("arbitrary"),
    )(proj, proj, dy, sw, sbt, gv, into)


def _conv_taps(halo, cur, first):
    tb = cur.shape[0]
    full = jnp.concatenate([jnp.where(first, 0.0, halo), cur], axis=0)
    return [full[8:] if j == _KC - 1 else pltpu.roll(full, _KC - 1 - j, 0)[8:] for j in range(_KC)]


def _prev_spec(tb, w, cb):
    return pl.BlockSpec((8, w), lambda i: (jnp.maximum(i * (tb // 8) - 1, 0), cb))


def _l2_heads(x, H):
    outs, rs = [], []
    for h in range(H):
        xh = x[:, h * _GD:(h + 1) * _GD]
        r = lax.rsqrt(jnp.sum(xh * xh, axis=1, keepdims=True) + _EPS)
        outs.append(xh * r)
        rs.append(r)
    return outs, rs


def _gate_rows(ba, alog_row, dtb_row, H):
    lane = lax.broadcasted_iota(jnp.int32, ba.shape, 1)
    beta = _sigmoid(ba)
    g = -jnp.exp(alog_row) * _softplus(ba + dtb_row)
    return lane, beta, g


def _conv_fwd(proj, cw, alog_row, dtb_row, li, D, name, tb=256):
    T = proj.shape[0]
    H = D // _GD
    tb = _pick(T, tb, 8)
    bac = (8 * D) // _LANE

    def body(q_ref, k_ref, v_ref, qh_ref, kh_ref, vh_ref, ba_ref, cw_ref, al_ref, dtb_ref,
             qo_ref, ko_ref, vo_ref, bg_ref):
        first = pl.program_id(0) == 0
        cw_ = cw_ref[...]
        for idx, (cur, halo, out) in enumerate(((q_ref, qh_ref, qo_ref), (k_ref, kh_ref, ko_ref),
                                                 (v_ref, vh_ref, vo_ref))):
            taps = _conv_taps(halo[...], cur[...], first)
            w = cw_[:, idx * D:(idx + 1) * D]
            cv = taps[0] * w[0:1, :]
            for j in range(1, _KC):
                cv = cv + taps[j] * w[j:j + 1, :]
            act = _silu(cv)
            if idx < 2:
                outs, _ = _l2_heads(act, H)
                for h in range(H):
                    out[:, h * _GD:(h + 1) * _GD] = outs[h]
            else:
                out[...] = act
        lane, beta, g = _gate_rows(ba_ref[...], al_ref[...], dtb_ref[...], H)
        bg_ref[...] = jnp.where(lane < H, beta, jnp.where(lane < 2 * H, g, 0.0))

    return pl.pallas_call(
        body, name=name, grid=(T // tb,),
        in_specs=[_rspec(tb, D, 2), _rspec(tb, D, 3), _rspec(tb, D, 4),
                  _prev_spec(tb, D, 2), _prev_spec(tb, D, 3), _prev_spec(tb, D, 4),
                  _rspec(tb, _LANE, bac), _lspec((_KC, 3 * D), li), _lspec((1, _LANE), li), _lspec((1, _LANE), li)],
        out_specs=[_rspec(tb, D), _rspec(tb, D), _rspec(tb, D), _rspec(tb, _LANE)],
        out_shape=[jax.ShapeDtypeStruct((T, D), _F32)] * 3 + [jax.ShapeDtypeStruct((T, _LANE), _F32)],
        compiler_params=_cp("parallel"),
    )(proj, proj, proj, proj, proj, proj, proj, cw, alog_row, dtb_row)


def _conv_bwd1(proj, dqn, dkn, dvs, dbg, cw, alog_row, dtb_row, li, D, into, name, tb=256):
    T = proj.shape[0]
    H = D // _GD
    tb = _pick(T, tb, 8)
    bac = (8 * D) // _LANE

    def body(q_ref, k_ref, v_ref, qh_ref, kh_ref, vh_ref, ba_ref, dq_ref, dk_ref, dv_ref, dbg_ref,
             cw_ref, al_ref, dtb_ref, _, dc_ref, dba_ref, dcw_ref, dal_ref, ddt_ref):
        i = pl.program_id(0)
        first = i == 0

        @pl.when(first)
        def _():
            dcw_ref[...] = jnp.zeros_like(dcw_ref)
            dal_ref[...] = jnp.zeros_like(dal_ref)
            ddt_ref[...] = jnp.zeros_like(ddt_ref)

        cw_ = cw_ref[...]
        for idx, (cur, halo, dref) in enumerate(((q_ref, qh_ref, dq_ref), (k_ref, kh_ref, dk_ref),
                                                  (v_ref, vh_ref, dv_ref))):
            taps = _conv_taps(halo[...], cur[...], first)
            w = cw_[:, idx * D:(idx + 1) * D]
            cv = taps[0] * w[0:1, :]
            for j in range(1, _KC):
                cv = cv + taps[j] * w[j:j + 1, :]
            dact = dref[...]
            if idx < 2:
                outs, rs = _l2_heads(_silu(cv), H)
                pieces = []
                for h in range(H):
                    dy = dact[:, h * _GD:(h + 1) * _GD]
                    pieces.append(rs[h] * (dy - outs[h] * jnp.sum(dy * outs[h], axis=1, keepdims=True)))
                dact = jnp.concatenate(pieces, axis=1)
            dcv = dact * _dsilu(cv)
            dc_ref[:, idx * D:(idx + 1) * D] = dcv
            for j in range(_KC):
                colsum = _dot(jnp.ones((8, tb), _F32), dcv * taps[j])
                dcw_ref[j:j + 1, idx * D:(idx + 1) * D] += colsum[0:1, :]

        ba = ba_ref[...]
        lane, beta, g = _gate_rows(ba, al_ref[...], dtb_ref[...], H)
        dbg_ = dbg_ref[...]
        is_b, is_a = lane < H, jnp.logical_and(lane >= H, lane < 2 * H)
        da = dbg_ * (-jnp.exp(al_ref[...])) * _sigmoid(ba + dtb_ref[...])
        dba_ref[...] = jnp.where(is_b, dbg_ * beta * (1.0 - beta), jnp.where(is_a, da, 0.0)).astype(dba_ref.dtype)
        dal_ref[...] += jnp.sum(jnp.where(is_a, dbg_ * g, 0.0), axis=0, keepdims=True)
        ddt_ref[...] += jnp.sum(jnp.where(is_a, da, 0.0), axis=0, keepdims=True)

    return pl.pallas_call(
        body, name=name, grid=(T // tb,),
        in_specs=[_rspec(tb, D, 2), _rspec(tb, D, 3), _rspec(tb, D, 4),
                  _prev_spec(tb, D, 2), _prev_spec(tb, D, 3), _prev_spec(tb, D, 4),
                  _rspec(tb, _LANE, bac), _rspec(tb, D), _rspec(tb, D), _rspec(tb, D), _rspec(tb, _LANE),
                  _lspec((_KC, 3 * D), li), _lspec((1, _LANE), li), _lspec((1, _LANE), li), _HBM],
        out_specs=[_rspec(tb, 3 * D), _rspec(tb, _LANE, bac), _fspec((_KC, 3 * D)), _fspec((1, _LANE)),
                   _fspec((1, _LANE))],
        out_shape=[jax.ShapeDtypeStruct((T, 3 * D), _F32), jax.ShapeDtypeStruct(into.shape, into.dtype),
                   jax.ShapeDtypeStruct((_KC, 3 * D), _F32), jax.ShapeDtypeStruct((1, _LANE), _F32),
                   jax.ShapeDtypeStruct((1, _LANE), _F32)],
        input_output_aliases={14: 1}, compiler_params=_cp("arbitrary"),
    )(proj, proj, proj, proj, proj, proj, proj, dqn, dkn, dvs, dbg, cw, alog_row, dtb_row, into)


def _conv_bwd2(dc, cw, li, into, name, tb=256):
    T, W3 = dc.shape
    W = W3 // 3
    tb = _pick(T, tb, 8)
    nb8 = T // 8
    nrow = T // tb

    def body(dc_ref, nx_ref, cw_ref, _, o_ref):
        last = pl.program_id(0) == nrow - 1
        full = jnp.concatenate([dc_ref[...], jnp.where(last, 0.0, nx_ref[...])], axis=0)
        w = cw_ref[...]
        acc = full[:tb] * w[_KC - 1:_KC, :]
        for j in range(_KC - 1):
            sh = _KC - 1 - j
            acc = acc + pltpu.roll(full, tb + 8 - sh, 0)[:tb] * w[j:j + 1, :]
        o_ref[...] = acc.astype(o_ref.dtype)

    return pl.pallas_call(
        body, name=name, grid=(nrow, 3),
        in_specs=[pl.BlockSpec((tb, W), lambda i, j: (i, j)),
                  pl.BlockSpec((8, W), lambda i, j: (jnp.minimum((i + 1) * (tb // 8), nb8 - 1), j)),
                  pl.BlockSpec((None, _KC, W), lambda i, j: (li, 0, j)), _HBM],
        out_specs=pl.BlockSpec((tb, W), lambda i, j: (i, 2 + j)),
        out_shape=jax.ShapeDtypeStruct(into.shape, into.dtype), input_output_aliases={3: 0},
        compiler_params=_cp("parallel", "parallel"),
    )(dc, dc, cw, into)


def _split(a):
    hi = a.astype(_BF)
    return hi, (a - hi.astype(_F32)).astype(_BF)


def _dot3(a, b):
    (ah, al), (bh, bl) = a, b
    f = functools.partial(lax.dot_general, dimension_numbers=_NN, preferred_element_type=_F32)
    return f(ah, bh) + f(ah, bl) + f(al, bh)


def _inv_unit_lower(mats):
    C = mats[0].shape[0]
    ii = lax.broadcasted_iota(jnp.int32, (C, C), 0)
    jj = lax.broadcasted_iota(jnp.int32, (C, C), 1)
    xs = [jnp.where(ii == jj, 1.0, 0.0) - a for a in mats]
    ps = list(mats)
    n = 1
    while 2 * n < C:
        sp = [_split(p) for p in ps]
        ps = [_dot3(s, s) for s in sp]
        sp = [_split(p) for p in ps]
        xs = [x + _dot3(_split(x), s) for x, s in zip(xs, sp)]
        n *= 2
    return xs


def _gdn_chunk(q, k, v, g_row, b_row):
    C = q.shape[0]
    ii = lax.broadcasted_iota(jnp.int32, (C, C), 0)
    jj = lax.broadcasted_iota(jnp.int32, (C, C), 1)
    low, strict, eye = jj <= ii, jj < ii, ii == jj
    g_col = jnp.sum(jnp.where(eye, g_row, 0.0), axis=1, keepdims=True)
    b_col = jnp.sum(jnp.where(eye, b_row, 0.0), axis=1, keepdims=True)
    gam_col = jnp.sum(jnp.where(low, g_row, 0.0), axis=1, keepdims=True)
    gam_row = jnp.sum(jnp.where(jj >= ii, g_col, 0.0), axis=0, keepdims=True)
    gam_last = jnp.sum(g_row, axis=1, keepdims=True)
    decay = jnp.where(low, jnp.exp(jnp.where(low, gam_col - gam_row, 0.0)), 0.0)
    eg = jnp.exp(gam_col)
    ekd = jnp.exp(gam_last - gam_col)
    qs = q * (_GD ** -0.5)
    kb = k * b_col
    kk = _dot(kb, k, _NT)
    qkraw = _dot(qs, k, _NT)
    return dict(low=low, strict=strict, eye=eye, ii=ii, jj=jj, b_col=b_col, decay=decay, eg=eg, ekd=ekd,
                gl=jnp.exp(gam_last), qs=qs, kb=kb, kk=kk, qkraw=qkraw,
                A=jnp.where(strict, kk * decay, 0.0), vb=v * b_col, kbg=kb * eg,
                qk=qkraw * decay, q_dec=qs * eg, k_dec=k * ekd)


def _gdn_fwd(qn, kn, vs, g_r, b_r, name):
    T, D = qn.shape
    H, N, C = D // _GD, T // _BC, _BC
    hb = min(_HB, H)

    def body(q_ref, k_ref, v_ref, g_ref, b_ref, o_ref, s_ref, t_ref, S):
        @pl.when(pl.program_id(1) == 0)
        def _():
            S[...] = jnp.zeros_like(S)

        hs = range(hb)
        sls = [slice(hh * _GD, (hh + 1) * _GD) for hh in hs]
        cms = [_gdn_chunk(q_ref[:, sl], k_ref[:, sl], v_ref[:, sl], g_ref[hh], b_ref[hh]) for hh, sl in zip(hs, sls)]
        tms = _inv_unit_lower([cm["A"] for cm in cms])
        us = [_dot(tm, cm["vb"]) for tm, cm in zip(tms, cms)]
        ws = [_dot(tm, cm["kbg"]) for tm, cm in zip(tms, cms)]
        s0s = [S[hh] for hh in hs]
        for hh in hs:
            s_ref[hh] = s0s[hh]
            t_ref[hh] = tms[hh]
        v_news = [u - _dot(w, s0) for u, w, s0 in zip(us, ws, s0s)]
        qss = [_dot(cm["q_dec"], s0) for cm, s0 in zip(cms, s0s)]
        for hh in hs:
            o_ref[:, sls[hh]] = qss[hh] + _dot(cms[hh]["qk"], v_news[hh])
        for hh in hs:
            S[hh] = s0s[hh] * cms[hh]["gl"] + _dot(cms[hh]["k_dec"], v_news[hh], _TN)

    qspec = pl.BlockSpec((C, hb * _GD), lambda h, n: (n, h))
    gspec = pl.BlockSpec((hb, None, 1, C), lambda h, n: (h, n, 0, 0))
    return pl.pallas_call(
        body, name=name, grid=(H // hb, N),
        in_specs=[qspec, qspec, qspec, gspec, gspec],
        out_specs=[qspec, pl.BlockSpec((hb, None, _GD, _GD), lambda h, n: (h, n, 0, 0)),
                   pl.BlockSpec((hb, None, C, C), lambda h, n: (h, n, 0, 0))],
        out_shape=[jax.ShapeDtypeStruct((T, D), _F32), jax.ShapeDtypeStruct((H, N, _GD, _GD), _F32),
                   jax.ShapeDtypeStruct((H, N, C, C), _F32)],
        scratch_shapes=[pltpu.VMEM((hb, _GD, _GD), _F32)],
        compiler_params=_cp("arbitrary", "arbitrary"),
    )(qn, kn, vs, g_r, b_r)


def _gdn_bwd(qn, kn, vs, g_r, b_r, s_all, t_all, do, name):
    T, D = qn.shape
    H, N, C = D // _GD, T // _BC, _BC
    hb = min(_HB, H)

    def body(q_ref, k_ref, v_ref, g_ref, b_ref, s_ref, t_ref, do_ref, dq_ref, dk_ref, dv_ref, dg_ref, db_ref, dS):
        @pl.when(pl.program_id(1) == 0)
        def _():
            dS[...] = jnp.zeros_like(dS)

        hs = range(hb)
        sls = [slice(hh * _GD, (hh + 1) * _GD) for hh in hs]
        ks = [k_ref[:, sl] for sl in sls]
        vs_ = [v_ref[:, sl] for sl in sls]
        cms = [_gdn_chunk(q_ref[:, sl], k, v, g_ref[hh], b_ref[hh]) for hh, sl, k, v in zip(hs, sls, ks, vs_)]
        low, strict, eye, ii, jj = (cms[0][n] for n in ("low", "strict", "eye", "ii", "jj"))
        tms, s0s, dos, ds1s = [t_ref[hh] for hh in hs], [s_ref[hh] for hh in hs], [do_ref[:, sl] for sl in sls], \
            [dS[hh] for hh in hs]
        us = [_dot(tm, cm["vb"]) for tm, cm in zip(tms, cms)]
        ws = [_dot(tm, cm["kbg"]) for tm, cm in zip(tms, cms)]
        v_news = [u - _dot(w, s0) for u, w, s0 in zip(us, ws, s0s)]
        dv_news = [_dot(cm["qk"], do_, _TN) + _dot(cm["k_dec"], ds1) for cm, do_, ds1 in zip(cms, dos, ds1s)]
        dqks = [jnp.where(low, _dot(do_, vn, _NT), 0.0) for do_, vn in zip(dos, v_news)]
        dq_decs = [_dot(do_, s0, _NT) for do_, s0 in zip(dos, s0s)]
        dk_decs = [_dot(vn, ds1, _NT) for vn, ds1 in zip(v_news, ds1s)]
        dgls = [jnp.sum(jnp.sum(ds1 * s0, axis=1, keepdims=True), axis=0, keepdims=True) for ds1, s0 in zip(ds1s, s0s)]
        dws = [-_dot(dvn, s0, _NT) for dvn, s0 in zip(dv_news, s0s)]
        for hh in hs:
            dS[hh] = (_dot(cms[hh]["q_dec"], dos[hh], _TN) + cms[hh]["gl"] * ds1s[hh]
                      - _dot(ws[hh], dv_news[hh], _TN))
        dvbs = [_dot(tm, dvn, _TN) for tm, dvn in zip(tms, dv_news)]
        dkbgs = [_dot(tm, dw, _TN) for tm, dw in zip(tms, dws)]
        dAs = [-jnp.where(strict, _dot(dvb, u, _NT) + _dot(dkbg, w, _NT), 0.0)
               for dvb, u, dkbg, w in zip(dvbs, us, dkbgs, ws)]
        dkks = [dA * cm["decay"] for dA, cm in zip(dAs, cms)]
        dqkraws = [dqk * cm["decay"] for dqk, cm in zip(dqks, cms)]
        Es = [(dA * cm["kk"] + dqk * cm["qkraw"]) * cm["decay"] for dA, dqk, cm in zip(dAs, dqks, cms)]
        dkbs = [_dot(dkk, k) + dkbg * cm["eg"] for dkk, k, dkbg, cm in zip(dkks, ks, dkbgs, cms)]
        dqss = [_dot(dqr, k) + dqd * cm["eg"] for dqr, k, dqd, cm in zip(dqkraws, ks, dq_decs, cms)]
        for hh in hs:
            cm = cms[hh]
            dk_ref[:, sls[hh]] = (_dot(dqkraws[hh], cm["qs"], _TN) + _dot(dkks[hh], cm["kb"], _TN)
                                  + dk_decs[hh] * cm["ekd"] + dkbs[hh] * cm["b_col"])
            dv_ref[:, sls[hh]] = dvbs[hh] * cm["b_col"]
            dq_ref[:, sls[hh]] = dqss[hh] * (_GD ** -0.5)
        for hh in hs:
            cm, k, E = cms[hh], ks[hh], Es[hh]
            eg, ekd = cm["eg"], cm["ekd"]
            dbeta_col = jnp.sum(dvbs[hh] * vs_[hh] + dkbs[hh] * k, axis=1, keepdims=True)
            t_kd = jnp.sum(dk_decs[hh] * k, axis=1, keepdims=True) * ekd
            c1 = (jnp.sum(E, axis=1, keepdims=True) + jnp.sum(dkbgs[hh] * cm["kb"], axis=1, keepdims=True) * eg
                  + jnp.sum(dq_decs[hh] * cm["qs"], axis=1, keepdims=True) * eg - t_kd)
            r1 = jnp.sum(E, axis=0, keepdims=True)
            dgam_last = jnp.sum(t_kd, axis=0, keepdims=True) + dgls[hh] * cm["gl"]
            dgam_col = c1 - jnp.sum(jnp.where(eye, r1, 0.0), axis=1, keepdims=True)
            dg_ref[hh] = jnp.sum(jnp.where(ii >= jj, dgam_col, 0.0), axis=0, keepdims=True) + dgam_last
            db_ref[hh] = jnp.sum(jnp.where(eye, dbeta_col, 0.0), axis=0, keepdims=True)

    qspec = pl.BlockSpec((C, hb * _GD), lambda h, n: (N - 1 - n, h))
    gspec = pl.BlockSpec((hb, None, 1, C), lambda h, n: (h, N - 1 - n, 0, 0))
    return pl.pallas_call(
        body, name=name, grid=(H // hb, N),
        in_specs=[qspec, qspec, qspec, gspec, gspec,
                  pl.BlockSpec((hb, None, _GD, _GD), lambda h, n: (h, N - 1 - n, 0, 0)),
                  pl.BlockSpec((hb, None, C, C), lambda h, n: (h, N - 1 - n, 0, 0)), qspec],
        out_specs=[qspec, qspec, qspec, gspec, gspec],
        out_shape=[jax.ShapeDtypeStruct((T, D), _F32)] * 3 + [jax.ShapeDtypeStruct((H, N, 1, C), _F32)] * 2,
        scratch_shapes=[pltpu.VMEM((hb, _GD, _GD), _F32)],
        compiler_params=_cp("arbitrary", "arbitrary"),
    )(qn, kn, vs, g_r, b_r, s_all, t_all, do)


def _onorm_fwd(o, proj, go, li, D, name, tb=512):
    T = o.shape[0]
    H = D // _GD
    tb = _pick(T, tb, 8)

    def body(o_ref, z_ref, go_ref, y_ref):
        ov, zv, g = o_ref[...], z_ref[...], go_ref[...]
        for h in range(H):
            sl = slice(h * _GD, (h + 1) * _GD)
            oh = ov[:, sl]
            r = lax.rsqrt(jnp.mean(oh * oh, axis=1, keepdims=True) + _EPS)
            y_ref[:, sl] = (oh * r * g * _silu(zv[:, sl])).astype(y_ref.dtype)

    return pl.pallas_call(
        body, name=name, grid=(T // tb,), in_specs=[_rspec(tb, D), _rspec(tb, D, 5), _lspec((1, _GD), li)],
        out_specs=_rspec(tb, D), out_shape=jax.ShapeDtypeStruct((T, D), _MMT), compiler_params=_cp("parallel"),
    )(o, proj, go)


def _onorm_bwd(dy, o, proj, go, li, D, into, name, tb=256):
    T = o.shape[0]
    H = D // _GD
    tb = _pick(T, tb, 8)

    def body(dy_ref, o_ref, z_ref, go_ref, _, do_ref, dz_ref, dgo_ref):
        @pl.when(pl.program_id(0) == 0)
        def _():
            dgo_ref[...] = jnp.zeros_like(dgo_ref)

        dyv, ov, zv, g = dy_ref[...].astype(_F32), o_ref[...], z_ref[...], go_ref[...]
        dgo = jnp.zeros((1, _GD), _F32)
        for h in range(H):
            sl = slice(h * _GD, (h + 1) * _GD)
            oh, zh, dyh = ov[:, sl], zv[:, sl], dyv[:, sl]
            r = lax.rsqrt(jnp.mean(oh * oh, axis=1, keepdims=True) + _EPS)
            on = oh * r
            sz = _silu(zh)
            dgo = dgo + jnp.sum(dyh * sz * on, axis=0, keepdims=True)
            don = dyh * sz * g
            do_ref[:, sl] = r * (don - on * jnp.mean(don * on, axis=1, keepdims=True))
            dz_ref[:, sl] = (dyh * on * g * _dsilu(zh)).astype(dz_ref.dtype)
        dgo_ref[...] += dgo

    return pl.pallas_call(
        body, name=name, grid=(T // tb,),
        in_specs=[_rspec(tb, D), _rspec(tb, D), _rspec(tb, D, 5), _lspec((1, _GD), li), _HBM],
        out_specs=[_rspec(tb, D), _rspec(tb, D, 5), _fspec((1, _GD))],
        out_shape=[jax.ShapeDtypeStruct((T, D), _F32), jax.ShapeDtypeStruct(into.shape, into.dtype),
                   jax.ShapeDtypeStruct((1, _GD), _F32)],
        input_output_aliases={4: 1}, compiler_params=_cp("arbitrary"),
    )(dy, o, proj, go, into)


def _merge_bwd(dm, pa, pb, proj, D, name, tb=256):
    T, PW = proj.shape
    tb = _pick(T, tb, 8)

    def body(dm_ref, pa_ref, pb_ref, ga_ref, gb_ref, dpa_ref, dpb_ref, dg_ref):
        d = dm_ref[...].astype(_F32)
        sa, sb = _sigmoid(ga_ref[...]), _sigmoid(gb_ref[...])
        dpa_ref[...] = (d * sa).astype(dpa_ref.dtype)
        dpb_ref[...] = (d * sb).astype(dpb_ref.dtype)
        dg_ref[:, :D] = (d * pa_ref[...].astype(_F32) * sa * (1.0 - sa)).astype(dg_ref.dtype)
        dg_ref[:, D:] = (d * pb_ref[...].astype(_F32) * sb * (1.0 - sb)).astype(dg_ref.dtype)

    return pl.pallas_call(
        body, name=name, grid=(T // tb,),
        in_specs=[_rspec(tb, D), _rspec(tb, D), _rspec(tb, D), _rspec(tb, D, 6), _rspec(tb, D, 7)],
        out_specs=[_rspec(tb, D), _rspec(tb, D), _rspec(tb, 2 * D, 3)],
        out_shape=[jax.ShapeDtypeStruct((T, D), _MMT)] * 2 + [jax.ShapeDtypeStruct((T, PW), _MMT)],
        compiler_params=_cp("parallel"),
    )(dm, pa, pb, proj, proj)


def _loss_head(x, tgt, fg, name, tb=256):
    T, D = x.shape
    tb = _pick(T, tb, 8)

    def body(x_ref, t_ref, fg_ref, loss_ref, dx_ref, dfg_ref):
        @pl.when(pl.program_id(0) == 0)
        def _():
            loss_ref[...] = jnp.zeros_like(loss_ref)
            dfg_ref[...] = jnp.zeros_like(dfg_ref)

        xv, fg_ = x_ref[...], fg_ref[...]
        r = lax.rsqrt(jnp.mean(xv * xv, axis=1, keepdims=True) + _EPS)
        xn = xv * r
        e = xn * fg_ - t_ref[...]
        loss_ref[...] += (0.5 / D) * jnp.sum(jnp.sum(e * e, axis=1, keepdims=True), axis=0, keepdims=True)
        dy = e * (1.0 / D)
        dfg_ref[...] += jnp.sum(dy * xn, axis=0, keepdims=True)
        dxn = dy * fg_
        dx_ref[...] = r * (dxn - xn * jnp.mean(dxn * xn, axis=1, keepdims=True))

    return pl.pallas_call(
        body, name=name, grid=(T // tb,), in_specs=[_rspec(tb, D), _rspec(tb, D), _fspec((1, D))],
        out_specs=[_fspec((1, 1)), _rspec(tb, D), _fspec((1, D))],
        out_shape=[jax.ShapeDtypeStruct((1, 1), _F32), jax.ShapeDtypeStruct((T, D), _F32),
                   jax.ShapeDtypeStruct((1, D), _F32)],
        compiler_params=_cp("arbitrary"),
    )(x, tgt, fg)


def _row_tile(R, W, budget=1 << 20, unit=8):
    if R * W * 4 <= budget or R % unit:
        return R
    best = unit
    for t in range(unit, R + 1, unit):
        if R % t == 0 and t * W * 4 <= budget:
            best = t
    return best


def _add_own_half(send, got, half, name):
    P, Rp, W = send.shape
    Rh = Rp // 2
    tb = _row_tile(Rh, W, 1 << 21, 16)

    def body(h_ref, a_ref, b_ref, o_ref):
        o_ref[...] = (a_ref[...].astype(_F32) + b_ref[...].astype(_F32)).astype(o_ref.dtype)

    return pl.pallas_call(
        body, name=name,
        grid_spec=pltpu.PrefetchScalarGridSpec(
            num_scalar_prefetch=1, grid=(P, Rh // tb),
            in_specs=[pl.BlockSpec((None, None, tb, W), lambda k, i, h: (k, h[0], i, 0)),
                      pl.BlockSpec((None, tb, W), lambda k, i, h: (k, i, 0))],
            out_specs=pl.BlockSpec((None, tb, W), lambda k, i, h: (k, i, 0))),
        out_shape=jax.ShapeDtypeStruct((P, Rh, W), send.dtype), compiler_params=_cp("parallel", "parallel"),
    )(half, send.reshape(P, 2, Rh, W), got)


def _sum_slots(st, name):
    P, R, W = st.shape
    tb = _row_tile(R, W, 1 << 20, 16)

    def body(s_ref, o_ref):
        acc = s_ref[0].astype(_F32)
        for p in range(1, P):
            acc = acc + s_ref[p].astype(_F32)
        o_ref[...] = acc

    return pl.pallas_call(
        body, name=name, grid=(R // tb,), in_specs=[pl.BlockSpec((P, tb, W), lambda i: (0, i, 0))],
        out_specs=_rspec(tb, W), out_shape=jax.ShapeDtypeStruct((R, W), _F32), compiler_params=_cp("parallel"),
    )(st)


def _adamw(w, gst, m, v, name):
    R, W = w.shape
    P = gst.shape[0]
    tb = _row_tile(R, W, 1 << 20)
    c1, c2 = 1.0 - _B1 ** _STEP, 1.0 - _B2 ** _STEP

    def body(w_ref, g_ref, m_ref, v_ref, go_ref, d_ref, mo_ref, vo_ref):
        g = g_ref[0]
        for p in range(1, P):
            g = g + g_ref[p]
        mn = _B1 * m_ref[...] + (1.0 - _B1) * g
        vn = _B2 * v_ref[...] + (1.0 - _B2) * (g * g)
        go_ref[...] = g
        mo_ref[...] = mn
        vo_ref[...] = vn
        d_ref[...] = -_LR * ((mn / c1) / (jnp.sqrt(vn / c2) + _AEPS) + _WD * w_ref[...])

    spec = _rspec(tb, W)
    return pl.pallas_call(
        body, name=name, grid=(R // tb,),
        in_specs=[spec, pl.BlockSpec((P, tb, W), lambda i: (0, i, 0)), spec, spec],
        out_specs=[spec] * 4, out_shape=[jax.ShapeDtypeStruct((R, W), _F32)] * 4, compiler_params=_cp("parallel"),
    )(w, gst, m, v)


def _as2d(a):
    if a.ndim == 1:
        return a.reshape(1, -1)
    return a.reshape(-1, a.shape[-1])


def kernel(x, c, ada_w, ada_b, norm1_g, w_in, conv_w, spatial_w, spatial_b, v_norm_g, a_log, dt_bias, o_norm_g, w_branch_a, w_branch_b, w_out, norm2_g, w_ffn_in, w_ffn_out, final_g, loss_target, m_ada_w, m_ada_b, m_norm1_g, m_w_in, m_conv_w, m_spatial_w, m_spatial_b, m_v_norm_g, m_a_log, m_dt_bias, m_o_norm_g, m_w_branch_a, m_w_branch_b, m_w_out, m_norm2_g, m_w_ffn_in, m_w_ffn_out, m_final_g, v_ada_w, v_ada_b, v_norm1_g, v_w_in, v_conv_w, v_spatial_w, v_spatial_b, v_v_norm_g, v_a_log, v_dt_bias, v_o_norm_g, v_w_branch_a, v_w_branch_b, v_w_out, v_norm2_g, v_w_ffn_in, v_w_ffn_out, v_final_g):
    xb, tgt = x[0], loss_target[0]
    T, D = xb.shape
    L, H, G = ada_w.shape[0], a_log.shape[1], spatial_w.shape[1]
    F = 4 * w_ffn_out.shape[1]
    N = T // _BC
    Ws = ada_w.shape[2]
    Wc = w_in.shape[2]
    PW = 8 * D + _LANE
    ix, iy, ic = lax.axis_index("x"), lax.axis_index("y"), lax.axis_index("c")
    me = 4 * ix + 2 * iy + ic

    c_all = _gather8(c, "gather_c").reshape(8, D)
    modp = _ada_fwd(c_all, ada_w, "ada_fwd")
    n_mod, n_cw = L * 8 * Ws, L * _KC * conv_w.shape[2]
    pad = (-(n_mod + n_cw)) % _LANE
    pay = jnp.concatenate([modp.reshape(-1), conv_w.reshape(-1), jnp.zeros((pad,), _F32)]).reshape(-1, _LANE)
    pay_all = _gather8(pay, "gather_mod").reshape(8, -1)
    mod_full = jnp.concatenate([pay_all[2 * k, :n_mod].reshape(L, 8, Ws) for k in range(4)], axis=-1)
    cw_full = jnp.concatenate([pay_all[2 * k, n_mod:n_mod + n_cw].reshape(L, _KC, -1) for k in range(4)], axis=-1)
    mod = lax.dynamic_index_in_dim(mod_full, me, axis=1, keepdims=False) + ada_b
    mods = [[mod[l, j * D:(j + 1) * D].reshape(1, D) for j in range(6)] for l in range(L)]

    big = [w_in, w_branch_a, w_branch_b, w_out, w_ffn_in, w_ffn_out]
    chip = 2 * ix + iy
    starts = [(k * Wc) // 16 * 16 for k in range(4)]
    Hh = max(-(-((k + 1) * Wc) // 16) * 16 - starts[k] for k in range(4))
    No = max(s + Hh for s in starts)
    my_off = jnp.asarray([k * Wc - starts[k] for k in range(4)], jnp.int32)[chip]
    cuts = sorted(set(starts + [s + Hh for s in starts]))

    pers = [Hh, D // 4, D // 4, D // 4, 2 * F // 4, F // 4]
    roff = [0]
    for p in pers:
        roff.append(roff[-1] + L * p)
    Rp = -(-roff[-1] // (32 * _NCH)) * (32 * _NCH)
    rpad = Rp - roff[-1]

    hull = lax.dynamic_update_slice(jnp.zeros((L, Hh, D), _F32), jnp.transpose(w_in, (0, 2, 1)), (0, my_off, 0))
    shard = jnp.concatenate(
        [hull.reshape(-1, D).astype(_MMT), w_branch_a.reshape(-1, D).astype(_MMT),
         w_branch_b.reshape(-1, D).astype(_MMT), w_out.reshape(-1, D).astype(_MMT),
         jnp.transpose(w_ffn_in, (0, 2, 1)).reshape(-1, D).astype(_MMT), w_ffn_out.reshape(-1, D).astype(_MMT),
         jnp.zeros((rpad, D), _MMT)], axis=0)
    gw = _fill_from_sibling(_gather_chips(shard, "gather_w"), "gather_w_sib")

    def slab(i, l, k):
        a = roff[i] + l * pers[i]
        return gw[k, a:a + pers[i]]

    def joined(i, l):
        return jnp.concatenate([slab(i, l, k) for k in range(4)], axis=0)

    def orig_rows(hulls, a, b):
        edges = sorted(set([a, b] + [c_ for c_ in cuts if a < c_ < b]))
        out = []
        for lo, hi in zip(edges[:-1], edges[1:]):
            cov = [k for k in range(4) if starts[k] <= lo and hi <= starts[k] + Hh]
            piece = hulls[cov[0]][lo - starts[cov[0]]:hi - starts[cov[0]]]
            for k in cov[1:]:
                piece = piece + hulls[k][lo - starts[k]:hi - starts[k]]
            out.append(piece)
        return out

    wt_in_p = []
    for l in range(L):
        hulls = [slab(0, l, k) for k in range(4)]
        wt_in_p.append(jnp.concatenate(
            orig_rows(hulls, 0, 6 * D) + orig_rows(hulls, 6 * D + 2 * H, 8 * D + 2 * H)
            + orig_rows(hulls, 6 * D, 6 * D + 2 * H) + [jnp.zeros((_LANE - 2 * H, D), _MMT)], axis=0))
    w_a, w_b, w_o, wt_fi, w_fo = ([joined(i, l) for l in range(L)] for i in range(1, 6))

    sbt = jnp.transpose(spatial_b, (0, 2, 1))
    gv3, go3 = v_norm_g.reshape(L, 1, D), o_norm_g.reshape(L, 1, _GD)
    zpad = jnp.zeros((L, _LANE - 2 * H), _F32)
    alog_row = jnp.concatenate([jnp.zeros((L, H), _F32), a_log, zpad], axis=1).reshape(L, 1, _LANE)
    dtb_row = jnp.concatenate([jnp.zeros((L, H), _F32), dt_bias, zpad], axis=1).reshape(L, 1, _LANE)

    def rows_of(tok):
        return jnp.transpose(tok.reshape(N, _BC, H), (2, 0, 1)).reshape(H, N, 1, _BC)

    def toks_of(rows):
        return jnp.transpose(rows.reshape(H, N, _BC), (1, 2, 0)).reshape(T, H)

    saved = []
    xc = xb
    for l in range(L):
        sh1, sc1, gt1, sh2, sc2, gt2 = mods[l]
        g1, g2 = norm1_g[l].reshape(1, D), norm2_g[l].reshape(1, D)
        h = _norm_mod(xc, g1, sc1, sh1, f"norm1_{l}")
        proj = _mm(h, wt_in_p[l], f"proj_{l}", trans_b=True, tn=1664)
        ya = _gmlp_fwd(proj, spatial_w, sbt, gv3, l, D, f"gmlp_{l}")
        qn, kn, vs, bg = _conv_fwd(proj, cw_full, alog_row, dtb_row, l, D, f"conv_{l}")
        g_r, b_r = rows_of(bg[:, H:2 * H]), rows_of(bg[:, :H])
        o, s_all, t_all = _gdn_fwd(qn, kn, vs, g_r, b_r, f"gdn_{l}")
        yb = _onorm_fwd(o, proj, go3, l, D, f"onorm_{l}")
        pa, pb, mg = _branch_merge(ya, yb, w_a[l], w_b[l], proj, f"branch_{l}")
        p1, x1 = _mm_res(mg, w_o[l], xc, gt1, f"wout_{l}")
        h2 = _norm_mod(x1, g2, sc2, sh2, f"norm2_{l}")
        gate, up, act = _ffin_swiglu(h2, wt_fi[l], f"ffin_{l}")
        p2, x2 = _mm_res(act, w_fo[l], x1, gt2, f"ffout_{l}")
        saved.append(dict(x=xc, h=h, proj=proj, ya=ya, yb=yb, qn=qn, kn=kn, vs=vs, g_r=g_r, b_r=b_r, o=o,
                          s_all=s_all, t_all=t_all, pa=pa, pb=pb, mg=mg, p1=p1, x1=x1, h2=h2, gate=gate, up=up,
                          act=act, p2=p2))
        xc = x2

    loss11, dx, dfg = _loss_head(xc, tgt, final_g.reshape(1, D), "loss_head")
    loss = lax.psum(loss11[0, 0], ("x", "y", "c"))

    gbig = {k: None for k in ("w_in", "w_a", "w_b", "w_o", "w_fi", "w_fo")}
    small = {k: [None] * L for k in ("dmod", "n1", "n2", "sw", "sb", "gv", "cw", "al", "dt", "go")}
    for l in reversed(range(L)):
        sv = saved[l]
        sh1, sc1, gt1, sh2, sc2, gt2 = mods[l]
        g1, g2 = norm1_g[l].reshape(1, D), norm2_g[l].reshape(1, D)
        proj = sv["proj"]
        dp2, dgt2 = _resid_bwd(dx, sv["p2"], gt2, f"res2b_{l}")
        dgate, dup = _ffoutb_swiglu(dp2, w_fo[l], sv["gate"], sv["up"], f"ffoutb_{l}")
        gbig["w_fo"] = _mm_tn(sv["act"], dp2, f"ffoutw_{l}", l, L, gbig["w_fo"])
        dx1, dgm2, dsh2 = _mm_normb((dgate, dup), wt_fi[l], sv["x1"], dx, g2, sc2, f"ffinb_{l}")
        gbig["w_fi"] = _mm_tn(dgate, sv["h2"], f"ffinwg_{l}", l, L, gbig["w_fi"], mtot=2 * F)
        gbig["w_fi"] = _mm_tn(dup, sv["h2"], f"ffinwu_{l}", l, L, gbig["w_fi"], row0=F, mtot=2 * F)
        dp1, dgt1 = _resid_bwd(dx1, sv["p1"], gt1, f"res1b_{l}")
        dmg = _mm(dp1, w_o[l], f"woutb_{l}", trans_b=True, out_dtype=_MMT)
        gbig["w_o"] = _mm_tn(sv["mg"], dp1, f"woutw_{l}", l, L, gbig["w_o"])
        dpa, dpb, dproj = _merge_bwd(dmg, sv["pa"], sv["pb"], proj, D, f"mergeb_{l}")
        dya = _mm(dpa, w_a[l], f"brab_{l}", trans_b=True, out_dtype=_MMT)
        gbig["w_a"] = _mm_tn(sv["ya"], dpa, f"braw_{l}", l, L, gbig["w_a"])
        dyb = _mm(dpb, w_b[l], f"brbb_{l}", trans_b=True, out_dtype=_MMT)
        gbig["w_b"] = _mm_tn(sv["yb"], dpb, f"brbw_{l}", l, L, gbig["w_b"])
        dproj, dsw, dsa, dgv = _gmlp_bwd(proj, dya, spatial_w, sbt, gv3, l, D, dproj, f"gmlpb_{l}")
        do, dproj, dgo = _onorm_bwd(dyb, sv["o"], proj, go3, l, D, dproj, f"onormb_{l}")
        dqn, dkn, dvs, dg_r, db_r = _gdn_bwd(sv["qn"], sv["kn"], sv["vs"], sv["g_r"], sv["b_r"], sv["s_all"],
                                             sv["t_all"], do, f"gdnb_{l}")
        dbg = jnp.concatenate([toks_of(db_r), toks_of(dg_r), jnp.zeros((T, _LANE - 2 * H), _F32)], axis=1)
        dc, dproj, dcw, dal, ddt = _conv_bwd1(proj, dqn, dkn, dvs, dbg, cw_full, alog_row, dtb_row, l, D, dproj,
                                              f"convb_{l}")
        dproj = _conv_bwd2(dc, cw_full, l, dproj, f"convx_{l}")
        gbig["w_in"] = _mm_tn(dproj, sv["h"], f"projw_{l}", l, L, gbig["w_in"], tm=640)
        dx, dgm1, dsh1 = _mm_normb(dproj, wt_in_p[l], sv["x"], dx1, g1, sc1, f"projb_{l}", tk=1664)
        small["dmod"][l] = jnp.concatenate([dsh1, dgm1 * g1, dgt1, dsh2, dgm2 * g2, dgt2], axis=1)
        small["n1"][l], small["n2"][l] = dgm1 * (1.0 + sc1), dgm2 * (1.0 + sc2)
        small["sw"][l], small["gv"][l], small["cw"][l], small["go"][l] = dsw, dgv, dcw, dgo
        small["sb"][l] = jnp.transpose(dsa.reshape(_AC, G, _GD).sum(axis=-1))
        small["al"][l], small["dt"][l] = dal[:, H:2 * H], ddt[:, H:2 * H]
    grad_x = dx.reshape(1, T, D)

    names_small = ["dmod", "n1", "n2", "sw", "sb", "gv", "cw", "al", "dt", "go"]
    flat = [jnp.stack(small[k]).reshape(-1) for k in names_small] + [dfg.reshape(-1)]
    sizes = [f.shape[0] for f in flat]
    tot = sum(sizes)
    pad = (-tot) % 1024
    pay = jnp.concatenate(flat + [jnp.zeros((pad,), _F32)]).reshape(-1, 1024)
    sm_all = _gather8(pay, "gather_small").reshape(8, -1)
    offs = [0]
    for s in sizes:
        offs.append(offs[-1] + s)
    part = {k: sm_all[:, offs[i]:offs[i + 1]] for i, k in enumerate(names_small + ["fg"])}
    dmod_all = part["dmod"].reshape(8, L, 6 * D)

    outs = {}

    def update(nm, w, gst, m, v):
        shp = w.shape
        w2 = _as2d(w)
        g, d, mn, vn = _adamw(w2, gst.reshape((gst.shape[0],) + w2.shape), _as2d(m), _as2d(v), f"adamw_{nm}")
        outs[nm] = (g.reshape(shp), d.reshape(shp), mn.reshape(shp), vn.reshape(shp))

    chip = 2 * ix + iy
    dmod_t = jnp.transpose(dmod_all, (1, 0, 2))
    dmod_mine = lax.dynamic_slice_in_dim(dmod_t, chip * Ws, Ws, axis=2)
    g_ada_w = _ada_bwd(jnp.transpose(c_all), dmod_mine, "ada_bwd")
    update("ada_w", ada_w, g_ada_w[None], m_ada_w, v_ada_w)
    update("ada_b", ada_b, dmod_all, m_ada_b, v_ada_b)
    update("norm1_g", norm1_g, part["n1"], m_norm1_g, v_norm1_g)
    update("norm2_g", norm2_g, part["n2"], m_norm2_g, v_norm2_g)
    update("spatial_w", spatial_w, part["sw"], m_spatial_w, v_spatial_w)
    update("spatial_b", spatial_b, part["sb"], m_spatial_b, v_spatial_b)
    update("v_norm_g", v_norm_g, part["gv"], m_v_norm_g, v_v_norm_g)
    update("a_log", a_log, part["al"], m_a_log, v_a_log)
    update("dt_bias", dt_bias, part["dt"], m_dt_bias, v_dt_bias)
    update("o_norm_g", o_norm_g, part["go"], m_o_norm_g, v_o_norm_g)
    update("final_g", final_g, part["fg"], m_final_g, v_final_g)
    cw_cols = conv_w.shape[2]
    dcw_all = part["cw"].reshape(8, L, _KC, 4 * cw_cols)
    update("conv_w", conv_w, lax.dynamic_slice_in_dim(dcw_all, chip * cw_cols, cw_cols, axis=3), m_conv_w, v_conv_w)

    def hull_of(p, k):
        a, b = starts[k], starts[k] + Hh
        out = []
        for lo, hi, plo in ((0, 6 * D, 0), (6 * D, 6 * D + 2 * H, 8 * D), (6 * D + 2 * H, 8 * D + 2 * H, 6 * D),
                            (8 * D + 2 * H, No, None)):
            s, e = max(a, lo), min(b, hi)
            if s < e:
                out.append(jnp.zeros((L, e - s, D), _MMT) if plo is None else p[:, plo + s - lo:plo + e - lo])
        return (out[0] if len(out) == 1 else jnp.concatenate(out, axis=1)).reshape(L * Hh, D)

    pieces = []
    for k in range(4):
        pieces.append(hull_of(gbig["w_in"], k))
        for i, nm in enumerate(("w_a", "w_b", "w_o", "w_fi", "w_fo")):
            per = pers[i + 1]
            pieces.append(gbig[nm][:, k * per:(k + 1) * per].reshape(L * per, D))
        pieces.append(jnp.zeros((rpad, D), _MMT))
    send = jnp.concatenate(pieces, axis=0).reshape(4, Rp, D)
    got = _send_half_to_sibling(send, "reduce_cores")
    chipsum = _add_own_half(send, got, ic.astype(jnp.int32).reshape(1), "add_cores")
    parts = _scatter_to_chips(chipsum, "reduce_chips")
    mine = _sum_slots(parts, "add_chips")
    other = _swap_with_sibling(mine, "swap_cores")
    first = ic == 0
    gsum = jnp.concatenate([jnp.where(first, mine, other), jnp.where(first, other, mine)], axis=0)
    big_names = ["w_in", "w_branch_a", "w_branch_b", "w_out", "w_ffn_in", "w_ffn_out"]
    big_m = [m_w_in, m_w_branch_a, m_w_branch_b, m_w_out, m_w_ffn_in, m_w_ffn_out]
    big_v = [v_w_in, v_w_branch_a, v_w_branch_b, v_w_out, v_w_ffn_in, v_w_ffn_out]
    for i, (nm, w, m, v) in enumerate(zip(big_names, big, big_m, big_v)):
        g = gsum[roff[i]:roff[i + 1]].reshape(L, pers[i], D)
        if i == 0:
            g = jnp.transpose(lax.dynamic_slice_in_dim(g, my_off, Wc, axis=1), (0, 2, 1))
        elif i == 4:
            g = jnp.transpose(g, (0, 2, 1))
        update(nm, w, g[None], m, v)

    order = ["ada_w", "ada_b", "norm1_g", "w_in", "conv_w", "spatial_w", "spatial_b", "v_norm_g", "a_log", "dt_bias",
             "o_norm_g", "w_branch_a", "w_branch_b", "w_out", "norm2_g", "w_ffn_in", "w_ffn_out", "final_g"]
    return (loss, grad_x, *[outs[n][0] for n in order], *[outs[n][1] for n in order],
            *[outs[n][2] for n in order], *[outs[n][3] for n in order])
```

```python
import functools
import math

import jax
import jax.numpy as jnp
from jax import lax
from jax.experimental import pallas as pl
from jax.experimental.pallas import tpu as pltpu

_F32 = jnp.float32
_BF = jnp.bfloat16
_MMT = jnp.bfloat16
_EPS = 1e-6
_GD = 128
_AC = 128
_BC = 64
_KC = 4
_HB = 8
_NCH = 8
_LANE = 128
_VMEM_LIMIT = 56 * 1024 * 1024

_LR, _B1, _B2, _AEPS, _WD, _STEP = 0.001, 0.9, 0.999, 1e-08, 0.01, 10

_NN = (((1,), (0,)), ((), ()))
_NT = (((1,), (1,)), ((), ()))
_TN = (((0,), (0,)), ((), ()))

_MESH = pl.DeviceIdType.MESH


def _cp(*sem):
    return pltpu.CompilerParams(dimension_semantics=tuple(sem), vmem_limit_bytes=_VMEM_LIMIT)


def _dot(a, b, dn=_NN):
    return lax.dot_general(a.astype(_MMT), b.astype(_MMT), dn, preferred_element_type=_F32)


def _pick(n, target, unit=_LANE):
    if n <= target:
        return n
    best = None
    for t in range(unit, target + 1, unit):
        if n % t == 0:
            best = t
    assert best is not None, (n, target)
    return best


def _sigmoid(x):
    return 0.5 * jnp.tanh(0.5 * x) + 0.5


def _silu(x):
    return x * _sigmoid(x)


def _dsilu(x):
    s = _sigmoid(x)
    return s * (1.0 + x * (1.0 - s))


_GK = math.sqrt(2.0 / math.pi)


def _gelu(x):
    return 0.5 * x * (1.0 + jnp.tanh(_GK * (x + 0.044715 * x * x * x)))


def _dgelu(x):
    t = jnp.tanh(_GK * (x + 0.044715 * x * x * x))
    return 0.5 * (1.0 + t) + 0.5 * x * (1.0 - t * t) * _GK * (1.0 + 3.0 * 0.044715 * x * x)


def _softplus(x):
    return jnp.maximum(x, 0.0) + jnp.log(1.0 + jnp.exp(-jnp.abs(x)))


def _rspec(tb, w, cb=0):
    return pl.BlockSpec((tb, w), lambda i: (i, cb))


def _fspec(shape):
    nd = len(shape)
    return pl.BlockSpec(tuple(shape), lambda i: (0,) * nd)


def _lspec(tail, li):
    nd = len(tail)
    return pl.BlockSpec((None,) + tuple(tail), lambda i: (li,) + (0,) * nd)


def _slot_all8(x, y, c):
    return 4 * x + 2 * y + c


def _gather8(v, name):
    R, W = v.shape

    def body(v_ref, o_ref, ssem, rsem, lsem):
        x, y, c = lax.axis_index("x"), lax.axis_index("y"), lax.axis_index("c")
        sib = (x, y, 1 - c)
        chips = _other_chips(x, y)

        def slot(px, py, pc):
            return o_ref.at[_slot_all8(px, py, pc)]

        own = pltpu.make_async_copy(v_ref, slot(x, y, c), lsem)
        own.start()
        started = [_rcopy(v_ref, slot(x, y, c), ssem.at[0], rsem.at[0], sib)]
        started += [_rcopy(v_ref, slot(x, y, c), ssem.at[1 + j], rsem.at[1 + j], (px, py, c))
                    for j, (px, py) in enumerate(chips)]
        for cp in started:
            cp.start()
        for j, (px, py) in enumerate(chips):
            blk = slot(px, py, c)
            _rcopy(blk, blk, ssem.at[1 + j], rsem.at[1 + j], (px, py, c)).wait_recv()
            fw = _rcopy(blk, blk, ssem.at[4 + j], rsem.at[4 + j], sib)
            fw.start()
            started.append(fw)
        blk = slot(x, y, 1 - c)
        _rcopy(blk, blk, ssem.at[0], rsem.at[0], sib).wait_recv()
        for j, (px, py) in enumerate(chips):
            blk = slot(px, py, 1 - c)
            _rcopy(blk, blk, ssem.at[4 + j], rsem.at[4 + j], sib).wait_recv()
        for cp in started:
            cp.wait_send()
        own.wait()

    return pl.pallas_call(
        body, name=name, out_shape=jax.ShapeDtypeStruct((8, R, W), v.dtype), in_specs=[_HBM], out_specs=_HBM,
        scratch_shapes=[pltpu.SemaphoreType.DMA((7,)), pltpu.SemaphoreType.DMA((7,)), pltpu.SemaphoreType.DMA],
    )(v)


def _rcopy(src, dst, ssem, rsem, dev):
    return pltpu.make_async_remote_copy(src_ref=src, dst_ref=dst, send_sem=ssem, recv_sem=rsem,
                                        device_id=dev, device_id_type=_MESH)


def _other_chips(x, y):
    return [(1 - x, y), (x, 1 - y), (1 - x, 1 - y)]


_HBM = pl.BlockSpec(memory_space=pl.ANY)


def _gather_chips(shard, name):
    Rp, W = shard.shape
    Rh = Rp // 2
    rc = Rh // _NCH
    hq = _NCH // 2

    def body(s_ref, o_ref, ssem, rsem, lsem):
        x, y, c = lax.axis_index("x"), lax.axis_index("y"), lax.axis_index("c")
        chip = 2 * x + y
        xn, yn, dg = _other_chips(x, y)
        cx, cy, cd = 2 * xn[0] + xn[1], 2 * yn[0] + yn[1], 2 * dg[0] + dg[1]

        def rows(q):
            return pl.ds(c * Rh + q * rc, rc)

        locs = []
        for q in range(_NCH):
            lc = pltpu.make_async_copy(s_ref.at[rows(q)], o_ref.at[chip, rows(q)], lsem.at[q])
            lc.start()
            locs.append(lc)
        started = []
        for q in range(_NCH):
            for j, nb in ((0, xn), (1, yn)):
                cp = _rcopy(s_ref.at[rows(q)], o_ref.at[chip, rows(q)], ssem.at[j * _NCH + q], rsem.at[j * _NCH + q],
                            (nb[0], nb[1], c))
                cp.start()
                started.append(cp)
        for q in range(_NCH):
            bx = o_ref.at[cx, rows(q)]
            _rcopy(bx, bx, ssem.at[q], rsem.at[q], (xn[0], xn[1], c)).wait_recv()
            if q >= hq:
                rl = _rcopy(bx, bx, ssem.at[2 * _NCH + q], rsem.at[2 * _NCH + q], (yn[0], yn[1], c))
                rl.start()
                started.append(rl)
            by = o_ref.at[cy, rows(q)]
            _rcopy(by, by, ssem.at[_NCH + q], rsem.at[_NCH + q], (yn[0], yn[1], c)).wait_recv()
            if q < hq:
                rl = _rcopy(by, by, ssem.at[2 * _NCH + q], rsem.at[2 * _NCH + q], (xn[0], xn[1], c))
                rl.start()
                started.append(rl)
        for q in range(_NCH):
            bd = o_ref.at[cd, rows(q)]
            _rcopy(bd, bd, ssem.at[2 * _NCH + q], rsem.at[2 * _NCH + q], (dg[0], dg[1], c)).wait_recv()
        for cp in started:
            cp.wait_send()
        for lc in locs:
            lc.wait()

    return pl.pallas_call(
        body, name=name, out_shape=jax.ShapeDtypeStruct((4, Rp, W), shard.dtype), in_specs=[_HBM], out_specs=_HBM,
        scratch_shapes=[pltpu.SemaphoreType.DMA((3 * _NCH,))] * 2 + [pltpu.SemaphoreType.DMA((_NCH,))],
    )(shard)


def _fill_from_sibling(buf, name):
    P, Rp, W = buf.shape
    Rh = Rp // 2
    rc = Rh // _NCH

    def body(s_ref, o_ref, ssem, rsem):
        x, y, c = lax.axis_index("x"), lax.axis_index("y"), lax.axis_index("c")
        cps = []
        for k in range(P):
            for q in range(_NCH):
                r = pl.ds(c * Rh + q * rc, rc)
                cp = _rcopy(s_ref.at[k, r], o_ref.at[k, r], ssem.at[k * _NCH + q], rsem.at[k * _NCH + q],
                            (x, y, 1 - c))
                cp.start()
                cps.append(cp)
        for k in range(P):
            for q in range(_NCH):
                blk = o_ref.at[k, pl.ds((1 - c) * Rh + q * rc, rc)]
                _rcopy(blk, blk, ssem.at[k * _NCH + q], rsem.at[k * _NCH + q], (x, y, 1 - c)).wait_recv()
        for cp in cps:
            cp.wait_send()

    return pl.pallas_call(
        body, name=name, out_shape=jax.ShapeDtypeStruct(buf.shape, buf.dtype), in_specs=[_HBM], out_specs=_HBM,
        scratch_shapes=[pltpu.SemaphoreType.DMA((P * _NCH,))] * 2, input_output_aliases={0: 0},
    )(buf)


def _send_half_to_sibling(send, name):
    P, Rp, W = send.shape
    Rh = Rp // 2
    rc = Rh // _NCH

    def body(s_ref, o_ref, ssem, rsem):
        x, y, c = lax.axis_index("x"), lax.axis_index("y"), lax.axis_index("c")
        cps = []
        for k in range(P):
            for q in range(_NCH):
                cp = _rcopy(s_ref.at[k, pl.ds((1 - c) * Rh + q * rc, rc)], o_ref.at[k, pl.ds(q * rc, rc)],
                            ssem.at[k * _NCH + q], rsem.at[k * _NCH + q], (x, y, 1 - c))
                cp.start()
                cps.append(cp)
        for cp in cps:
            cp.wait()

    return pl.pallas_call(
        body, name=name, out_shape=jax.ShapeDtypeStruct((P, Rh, W), send.dtype), in_specs=[_HBM], out_specs=_HBM,
        scratch_shapes=[pltpu.SemaphoreType.DMA((P * _NCH,))] * 2,
    )(send)


def _scatter_to_chips(cs, name):
    P, Rh, W = cs.shape
    rc = Rh // _NCH

    def body(s_ref, o_ref, ssem, rsem, lsem):
        x, y, c = lax.axis_index("x"), lax.axis_index("y"), lax.axis_index("c")
        chip = 2 * x + y
        peers = _other_chips(x, y)
        locs = []
        for q in range(_NCH):
            r = pl.ds(q * rc, rc)
            lc = pltpu.make_async_copy(s_ref.at[chip, r], o_ref.at[chip, r], lsem.at[q])
            lc.start()
            locs.append(lc)
        cps = []
        for j, (px, py) in enumerate(peers):
            for q in range(_NCH):
                r = pl.ds(q * rc, rc)
                cp = _rcopy(s_ref.at[2 * px + py, r], o_ref.at[chip, r], ssem.at[j * _NCH + q], rsem.at[j * _NCH + q],
                            (px, py, c))
                cp.start()
                cps.append(cp)
        for j, (px, py) in enumerate(peers):
            for q in range(_NCH):
                blk = o_ref.at[2 * px + py, pl.ds(q * rc, rc)]
                _rcopy(blk, blk, ssem.at[j * _NCH + q], rsem.at[j * _NCH + q], (px, py, c)).wait_recv()
        for cp in cps:
            cp.wait_send()
        for lc in locs:
            lc.wait()

    return pl.pallas_call(
        body, name=name, out_shape=jax.ShapeDtypeStruct((P, Rh, W), cs.dtype), in_specs=[_HBM], out_specs=_HBM,
        scratch_shapes=[pltpu.SemaphoreType.DMA((3 * _NCH,))] * 2 + [pltpu.SemaphoreType.DMA((_NCH,))],
    )(cs)


def _swap_with_sibling(v, name):
    R, W = v.shape
    rc = R // _NCH

    def body(s_ref, o_ref, ssem, rsem):
        x, y, c = lax.axis_index("x"), lax.axis_index("y"), lax.axis_index("c")
        cps = []
        for q in range(_NCH):
            r = pl.ds(q * rc, rc)
            cp = _rcopy(s_ref.at[r], o_ref.at[r], ssem.at[q], rsem.at[q], (x, y, 1 - c))
            cp.start()
            cps.append(cp)
        for cp in cps:
            cp.wait()

    return pl.pallas_call(
        body, name=name, out_shape=jax.ShapeDtypeStruct((R, W), v.dtype), in_specs=[_HBM], out_specs=_HBM,
        scratch_shapes=[pltpu.SemaphoreType.DMA((_NCH,))] * 2,
    )(v)


def _mm_nt(a, b, name, out_dtype=_F32, tm=1024, tn=1024):
    M, K = a.shape
    N = b.shape[0]
    tm, tn = _pick(M, tm, 8), _pick(N, tn)

    def body(a_ref, b_ref, o_ref):
        o_ref[...] = lax.dot_general(a_ref[...], b_ref[...], _NT, preferred_element_type=_F32).astype(o_ref.dtype)

    return pl.pallas_call(
        body, name=name, grid=(M // tm, N // tn),
        in_specs=[pl.BlockSpec((tm, K), lambda i, j: (i, 0)), pl.BlockSpec((tn, K), lambda i, j: (j, 0))],
        out_specs=pl.BlockSpec((tm, tn), lambda i, j: (i, j)),
        out_shape=jax.ShapeDtypeStruct((M, N), out_dtype),
        compiler_params=_cp("parallel", "parallel"),
    )(a, b)


def _mm_tn(a, b, name, li, nl, into=None, row0=0, mtot=None, tm=512, tn=1024):
    T, M = a.shape
    N = b.shape[1]
    tm, tn = _pick(M, tm), _pick(N, tn)
    mtot = M if mtot is None else mtot
    assert row0 % tm == 0
    r0 = row0 // tm

    def body(a_ref, b_ref, *rest):
        o_ref = rest[-1]
        o_ref[...] = lax.dot_general(a_ref[...], b_ref[...], _TN, preferred_element_type=_F32).astype(o_ref.dtype)

    ins = [pl.BlockSpec((T, tm), lambda i, j: (0, i)), pl.BlockSpec((T, tn), lambda i, j: (0, j))]
    return pl.pallas_call(
        body, name=name, grid=(M // tm, N // tn),
        in_specs=ins if into is None else ins + [_HBM],
        out_specs=pl.BlockSpec((None, tm, tn), lambda i, j: (li, i + r0, j)),
        out_shape=jax.ShapeDtypeStruct((nl, mtot, N), _MMT),
        input_output_aliases={} if into is None else {2: 0},
        compiler_params=_cp("parallel", "parallel"),
    )(*((a, b) if into is None else (a, b, into)))


def _mm_res(a, b, x, gt, name, tm=1024, tn=1024, tk=2048):
    M, K = a.shape
    N = b.shape[1]
    tm, tn, tk = _pick(M, tm, 8), _pick(N, tn), _pick(K, tk)
    nk = K // tk

    def body(a_ref, b_ref, x_ref, gt_ref, p_ref, o_ref):
        k = pl.program_id(2)
        part = lax.dot_general(a_ref[...], b_ref[...], _NN, preferred_element_type=_F32)

        @pl.when(k == 0)
        def _():
            p_ref[...] = part

        @pl.when(k > 0)
        def _():
            p_ref[...] += part

        @pl.when(k == nk - 1)
        def _():
            o_ref[...] = x_ref[...] + gt_ref[...] * p_ref[...]

    tile = pl.BlockSpec((tm, tn), lambda i, j, k: (i, j))
    return pl.pallas_call(
        body, name=name, grid=(M // tm, N // tn, nk),
        in_specs=[pl.BlockSpec((tm, tk), lambda i, j, k: (i, k)), pl.BlockSpec((tk, tn), lambda i, j, k: (k, j)),
                  tile, pl.BlockSpec((1, tn), lambda i, j, k: (0, j))],
        out_specs=[tile, tile], out_shape=[jax.ShapeDtypeStruct((M, N), _F32)] * 2,
        compiler_params=_cp("parallel", "parallel", "arbitrary"),
    )(a, b, x, gt)


def _mm_normb(a, b, x, dres, g, sc, name, tm=1024, tk=2048):
    segs = tuple(a) if isinstance(a, (tuple, list)) else (a,)
    ns = len(segs)
    M, K1 = segs[0].shape
    D = b.shape[1]
    tm, tk = _pick(M, tm, 8), _pick(K1, tk)
    nh = K1 // tk
    nk = ns * nh

    def body(*refs):
        a_refs = refs[:ns]
        b_ref, x_ref, dr_ref, g_ref, sc_ref, dx_ref, dgm_ref, dsh_ref, acc = refs[ns:]
        i, k = pl.program_id(0), pl.program_id(1)

        @pl.when(k == 0)
        def _():
            acc[...] = jnp.zeros_like(acc)

        for s in range(ns):
            @pl.when(jnp.logical_and(k >= s * nh, k < (s + 1) * nh))
            def _():
                acc[...] += lax.dot_general(a_refs[s][...], b_ref[...], _NN, preferred_element_type=_F32)

        @pl.when(jnp.logical_and(i == 0, k == 0))
        def _():
            dgm_ref[...] = jnp.zeros_like(dgm_ref)
            dsh_ref[...] = jnp.zeros_like(dsh_ref)

        @pl.when(k == nk - 1)
        def _():
            dh_, xv = acc[...], x_ref[...]
            r = lax.rsqrt(jnp.mean(xv * xv, axis=1, keepdims=True) + _EPS)
            xn = xv * r
            dxn = dh_ * (g_ref[...] * (1.0 + sc_ref[...]))
            dx_ref[...] = dr_ref[...] + r * (dxn - xn * jnp.mean(dxn * xn, axis=1, keepdims=True))
            dgm_ref[...] += jnp.sum(dh_ * xn, axis=0, keepdims=True)
            dsh_ref[...] += jnp.sum(dh_, axis=0, keepdims=True)

    row = pl.BlockSpec((tm, D), lambda i, k: (i, 0))
    vec = pl.BlockSpec((1, D), lambda i, k: (0, 0))
    a_specs = [pl.BlockSpec((tm, tk), lambda i, k, s=s: (i, jnp.clip(k - s * nh, 0, nh - 1))) for s in range(ns)]
    return pl.pallas_call(
        body, name=name, grid=(M // tm, nk),
        in_specs=a_specs + [pl.BlockSpec((tk, D), lambda i, k: (k, 0)), row, row, vec, vec],
        out_specs=[row, vec, vec],
        out_shape=[jax.ShapeDtypeStruct((M, D), _F32), jax.ShapeDtypeStruct((1, D), _F32),
                   jax.ShapeDtypeStruct((1, D), _F32)],
        scratch_shapes=[pltpu.VMEM((tm, D), _F32)],
        compiler_params=_cp("arbitrary", "arbitrary"),
    )(*segs, b, x, dres, g, sc)


def _ffoutb_swiglu(dp, w, gate, up, name, tm=1024, tn=1408):
    M, K = dp.shape
    F = w.shape[0]
    tm, tn = _pick(M, tm, 8), _pick(F, tn)

    def body(a_ref, b_ref, g_ref, u_ref, dg_ref, du_ref):
        d = lax.dot_general(a_ref[...], b_ref[...], _NT, preferred_element_type=_F32)
        gv = g_ref[...].astype(_F32)
        dg_ref[...] = (d * u_ref[...].astype(_F32) * _dsilu(gv)).astype(dg_ref.dtype)
        du_ref[...] = (d * _silu(gv)).astype(du_ref.dtype)

    tile = pl.BlockSpec((tm, tn), lambda i, j: (i, j))
    return pl.pallas_call(
        body, name=name, grid=(M // tm, F // tn),
        in_specs=[pl.BlockSpec((tm, K), lambda i, j: (i, 0)), pl.BlockSpec((tn, K), lambda i, j: (j, 0)), tile, tile],
        out_specs=[tile, tile], out_shape=[jax.ShapeDtypeStruct((M, F), _MMT)] * 2,
        compiler_params=_cp("parallel", "parallel"),
    )(dp, w, gate, up)


def _ffin_swiglu(a, wt, name, tm=1024, tn=1408):
    M, K = a.shape
    F = wt.shape[0] // 2
    tm, tn = _pick(M, tm, 8), _pick(F, tn)
    nj = F // tn

    def body(a_ref, bg_ref, bu_ref, g_ref, u_ref, act_ref):
        av = a_ref[...]
        g = lax.dot_general(av, bg_ref[...], _NT, preferred_element_type=_F32)
        u = lax.dot_general(av, bu_ref[...], _NT, preferred_element_type=_F32)
        g_ref[...] = g.astype(g_ref.dtype)
        u_ref[...] = u.astype(u_ref.dtype)
        act_ref[...] = (_silu(g) * u).astype(act_ref.dtype)

    tile = pl.BlockSpec((tm, tn), lambda i, j: (i, j))
    return pl.pallas_call(
        body, name=name, grid=(M // tm, nj),
        in_specs=[pl.BlockSpec((tm, K), lambda i, j: (i, 0)), pl.BlockSpec((tn, K), lambda i, j: (j, 0)),
                  pl.BlockSpec((tn, K), lambda i, j: (j + nj, 0))],
        out_specs=[tile] * 3, out_shape=[jax.ShapeDtypeStruct((M, F), _MMT)] * 3,
        compiler_params=_cp("parallel", "parallel"),
    )(a, wt, wt)


def _branch_merge(ya, yb, w_a, w_b, proj, name, tm=1024, tn=512):
    M, K = ya.shape
    N = w_a.shape[1]
    tm, tn = _pick(M, tm, 8), _pick(N, tn)
    nj = N // tn

    def body(ya_ref, yb_ref, wa_ref, wb_ref, ga_ref, gb_ref, pa_ref, pb_ref, m_ref):
        pa = lax.dot_general(ya_ref[...], wa_ref[...], _NN, preferred_element_type=_F32)
        pb = lax.dot_general(yb_ref[...], wb_ref[...], _NN, preferred_element_type=_F32)
        pa_ref[...] = pa.astype(pa_ref.dtype)
        pb_ref[...] = pb.astype(pb_ref.dtype)
        m_ref[...] = (_sigmoid(ga_ref[...]) * pa + _sigmoid(gb_ref[...]) * pb).astype(m_ref.dtype)

    row = pl.BlockSpec((tm, K), lambda i, j: (i, 0))
    col = pl.BlockSpec((K, tn), lambda i, j: (0, j))
    tile = pl.BlockSpec((tm, tn), lambda i, j: (i, j))
    return pl.pallas_call(
        body, name=name, grid=(M // tm, nj),
        in_specs=[row, row, col, col, pl.BlockSpec((tm, tn), lambda i, j: (i, 6 * nj + j)),
                  pl.BlockSpec((tm, tn), lambda i, j: (i, 7 * nj + j))],
        out_specs=[tile] * 3, out_shape=[jax.ShapeDtypeStruct((M, N), _MMT)] * 3,
        compiler_params=_cp("parallel", "parallel"),
    )(ya, yb, w_a, w_b, proj, proj)


def _ada_fwd(c_all, ada_w, name):
    L, D, Ws = ada_w.shape
    B = c_all.shape[0]

    def body(c_ref, w_ref, o_ref):
        o_ref[...] = _dot(_silu(c_ref[...]), w_ref[...])

    return pl.pallas_call(
        body, name=name, grid=(L,),
        in_specs=[_fspec((B, D)), pl.BlockSpec((None, D, Ws), lambda l: (l, 0, 0))],
        out_specs=pl.BlockSpec((None, B, Ws), lambda l: (l, 0, 0)),
        out_shape=jax.ShapeDtypeStruct((L, B, Ws), _F32), compiler_params=_cp("parallel"),
    )(c_all, ada_w)


def _ada_bwd(c_all_t, dmod, name):
    D, B = c_all_t.shape
    L, _, Ws = dmod.shape

    def body(c_ref, d_ref, o_ref):
        ct = _silu(c_ref[...])
        d = d_ref[...]
        acc = ct[:, 0:1] * d[0:1, :]
        for b in range(1, B):
            acc = acc + ct[:, b:b + 1] * d[b:b + 1, :]
        o_ref[...] = acc

    return pl.pallas_call(
        body, name=name, grid=(L,),
        in_specs=[_fspec((D, B)), pl.BlockSpec((None, B, Ws), lambda l: (l, 0, 0))],
        out_specs=pl.BlockSpec((None, D, Ws), lambda l: (l, 0, 0)),
        out_shape=jax.ShapeDtypeStruct((L, D, Ws), _F32), compiler_params=_cp("parallel"),
    )(c_all_t, dmod)


def _norm_mod(x, g, sc, sh, name, tb=512):
    T, D = x.shape
    tb = _pick(T, tb, 8)

    def body(x_ref, g_ref, sc_ref, sh_ref, h_ref):
        xv = x_ref[...]
        r = lax.rsqrt(jnp.mean(xv * xv, axis=1, keepdims=True) + _EPS)
        h_ref[...] = (xv * r * (g_ref[...] * (1.0 + sc_ref[...])) + sh_ref[...]).astype(h_ref.dtype)

    return pl.pallas_call(
        body, name=name, grid=(T // tb,),
        in_specs=[_rspec(tb, D), _fspec((1, D)), _fspec((1, D)), _fspec((1, D))],
        out_specs=_rspec(tb, D), out_shape=jax.ShapeDtypeStruct((T, D), _MMT), compiler_params=_cp("parallel"),
    )(x, g, sc, sh)


def _resid_bwd(dx, p, gt, name, tb=512):
    T, D = dx.shape
    tb = _pick(T, tb, 8)

    def body(dx_ref, p_ref, gt_ref, dp_ref, dgt_ref):
        i = pl.program_id(0)
        d = dx_ref[...]
        dp_ref[...] = (d * gt_ref[...]).astype(dp_ref.dtype)

        @pl.when(i == 0)
        def _():
            dgt_ref[...] = jnp.zeros_like(dgt_ref)

        dgt_ref[...] += jnp.sum(d * p_ref[...], axis=0, keepdims=True)

    return pl.pallas_call(
        body, name=name, grid=(T // tb,), in_specs=[_rspec(tb, D), _rspec(tb, D), _fspec((1, D))],
        out_specs=[_rspec(tb, D), _fspec((1, D))],
        out_shape=[jax.ShapeDtypeStruct((T, D), _MMT), jax.ShapeDtypeStruct((1, D), _F32)],
        compiler_params=_cp("arbitrary"),
    )(dx, p, gt)


def _gmlp_chunk(u_raw, v_raw, sw_ref, sbt, gv, G):
    u, v = _gelu(u_raw), _gelu(v_raw)
    ii = lax.broadcasted_iota(jnp.int32, (_AC, _AC), 0)
    jj = lax.broadcasted_iota(jnp.int32, (_AC, _AC), 1)
    out = []
    for gi in range(G):
        sl = slice(gi * _GD, (gi + 1) * _GD)
        vg = v[:, sl]
        r = lax.rsqrt(jnp.mean(vg * vg, axis=1, keepdims=True) + _EPS)
        vhat = vg * r
        W = jnp.where(jj <= ii, sw_ref[gi], 0.0)
        s = _dot(W, vhat * gv[:, sl]) + sbt[:, gi:gi + 1]
        out.append((u[:, sl], s, vhat, r, W))
    return out


def _gmlp_fwd(proj, sw, sbt, gv, li, D, name):
    T = proj.shape[0]
    G = D // _GD

    def body(u_ref, v_ref, sw_ref, sbt_ref, gv_ref, y_ref):
        parts = _gmlp_chunk(u_ref[...], v_ref[...], sw_ref, sbt_ref[...], gv_ref[...], G)
        for gi, (u, s, _, _, _) in enumerate(parts):
            y_ref[:, gi * _GD:(gi + 1) * _GD] = (u * s).astype(y_ref.dtype)

    return pl.pallas_call(
        body, name=name, grid=(T // _AC,),
        in_specs=[_rspec(_AC, D, 0), _rspec(_AC, D, 1), _lspec((G, _AC, _AC), li), _lspec((_AC, G), li),
                  _lspec((1, D), li)],
        out_specs=_rspec(_AC, D), out_shape=jax.ShapeDtypeStruct((T, D), _MMT), compiler_params=_cp("parallel"),
    )(proj, proj, sw, sbt, gv)


def _gmlp_bwd(proj, dy, sw, sbt, gv, li, D, into, name):
    T = proj.shape[0]
    G = D // _GD

    def body(u_ref, v_ref, dy_ref, sw_ref, sbt_ref, gv_ref, _, duv_ref, dsw_ref, dsa_ref, dgv_ref):
        i = pl.program_id(0)

        @pl.when(i == 0)
        def _():
            dsw_ref[...] = jnp.zeros_like(dsw_ref)
            dsa_ref[...] = jnp.zeros_like(dsa_ref)
            dgv_ref[...] = jnp.zeros_like(dgv_ref)

        u_raw, v_raw, dy_, gv_ = u_ref[...], v_ref[...], dy_ref[...].astype(_F32), gv_ref[...]
        parts = _gmlp_chunk(u_raw, v_raw, sw_ref, sbt_ref[...], gv_, G)
        ii = lax.broadcasted_iota(jnp.int32, (_AC, _AC), 0)
        jj = lax.broadcasted_iota(jnp.int32, (_AC, _AC), 1)
        dgu, dgv = _dgelu(u_raw), _dgelu(v_raw)
        for gi, (u, s, vhat, r, W) in enumerate(parts):
            sl = slice(gi * _GD, (gi + 1) * _GD)
            dyg = dy_[:, sl]
            ds = dyg * u
            vn = vhat * gv_[:, sl]
            dsw_ref[gi] += jnp.where(jj <= ii, _dot(ds, vn, _NT), 0.0)
            dsa_ref[:, sl] += ds
            dvn = _dot(W, ds, _TN)
            dgv_ref[:, sl] += jnp.sum(dvn * vhat, axis=0, keepdims=True)
            dvh = dvn * gv_[:, sl]
            dvg = r * (dvh - vhat * jnp.mean(dvh * vhat, axis=1, keepdims=True))
            duv_ref[:, sl] = (dyg * s * dgu[:, sl]).astype(duv_ref.dtype)
            duv_ref[:, D + gi * _GD:D + (gi + 1) * _GD] = (dvg * dgv[:, sl]).astype(duv_ref.dtype)

    return pl.pallas_call(
        body, name=name, grid=(T // _AC,),
        in_specs=[_rspec(_AC, D, 0), _rspec(_AC, D, 1), _rspec(_AC, D), _lspec((G, _AC, _AC), li),
                  _lspec((_AC, G), li), _lspec((1, D), li), _HBM],
        out_specs=[_rspec(_AC, 2 * D), _fspec((G, _AC, _AC)), _fspec((_AC, D)), _fspec((1, D))],
        out_shape=[jax.ShapeDtypeStruct(into.shape, into.dtype), jax.ShapeDtypeStruct((G, _AC, _AC), _F32),
                   jax.ShapeDtypeStruct((_AC, D), _F32), jax.ShapeDtypeStruct((1, D), _F32)],
        input_output_aliases={6: 0}, compiler_params=_cp("arbitrary"),
    )(proj, proj, dy, sw, sbt, gv, into)


def _conv_taps(halo, cur, first):
    tb = cur.shape[0]
    full = jnp.concatenate([jnp.where(first, 0.0, halo), cur], axis=0)
    return [full[8:] if j == _KC - 1 else pltpu.roll(full, _KC - 1 - j, 0)[8:] for j in range(_KC)]


def _prev_spec(tb, w, cb):
    return pl.BlockSpec((8, w), lambda i: (jnp.maximum(i * (tb // 8) - 1, 0), cb))


def _l2_heads(x, H):
    outs, rs = [], []
    for h in range(H):
        xh = x[:, h * _GD:(h + 1) * _GD]
        r = lax.rsqrt(jnp.sum(xh * xh, axis=1, keepdims=True) + _EPS)
        outs.append(xh * r)
        rs.append(r)
    return outs, rs


def _gate_rows(ba, alog_row, dtb_row, H):
    lane = lax.broadcasted_iota(jnp.int32, ba.shape, 1)
    beta = _sigmoid(ba)
    g = -jnp.exp(alog_row) * _softplus(ba + dtb_row)
    return lane, beta, g


def _conv_fwd(proj, cw, alog_row, dtb_row, li, D, name, tb=256):
    T = proj.shape[0]
    H = D // _GD
    tb = _pick(T, tb, 8)
    bac = (8 * D) // _LANE

    def body(q_ref, k_ref, v_ref, qh_ref, kh_ref, vh_ref, ba_ref, cw_ref, al_ref, dtb_ref,
             qo_ref, ko_ref, vo_ref, bg_ref):
        first = pl.program_id(0) == 0
        cw_ = cw_ref[...]
        for idx, (cur, halo, out) in enumerate(((q_ref, qh_ref, qo_ref), (k_ref, kh_ref, ko_ref),
                                                 (v_ref, vh_ref, vo_ref))):
            taps = _conv_taps(halo[...], cur[...], first)
            w = cw_[:, idx * D:(idx + 1) * D]
            cv = taps[0] * w[0:1, :]
            for j in range(1, _KC):
                cv = cv + taps[j] * w[j:j + 1, :]
            act = _silu(cv)
            if idx < 2:
                outs, _ = _l2_heads(act, H)
                for h in range(H):
                    out[:, h * _GD:(h + 1) * _GD] = outs[h]
            else:
                out[...] = act
        lane, beta, g = _gate_rows(ba_ref[...], al_ref[...], dtb_ref[...], H)
        bg_ref[...] = jnp.where(lane < H, beta, jnp.where(lane < 2 * H, g, 0.0))

    return pl.pallas_call(
        body, name=name, grid=(T // tb,),
        in_specs=[_rspec(tb, D, 2), _rspec(tb, D, 3), _rspec(tb, D, 4),
                  _prev_spec(tb, D, 2), _prev_spec(tb, D, 3), _prev_spec(tb, D, 4),
                  _rspec(tb, _LANE, bac), _lspec((_KC, 3 * D), li), _lspec((1, _LANE), li), _lspec((1, _LANE), li)],
        out_specs=[_rspec(tb, D), _rspec(tb, D), _rspec(tb, D), _rspec(tb, _LANE)],
        out_shape=[jax.ShapeDtypeStruct((T, D), _F32)] * 3 + [jax.ShapeDtypeStruct((T, _LANE), _F32)],
        compiler_params=_cp("parallel"),
    )(proj, proj, proj, proj, proj, proj, proj, cw, alog_row, dtb_row)


def _conv_bwd1(proj, dqn, dkn, dvs, dbg, cw, alog_row, dtb_row, li, D, into, name, tb=256):
    T = proj.shape[0]
    H = D // _GD
    tb = _pick(T, tb, 8)
    bac = (8 * D) // _LANE

    def body(q_ref, k_ref, v_ref, qh_ref, kh_ref, vh_ref, ba_ref, dq_ref, dk_ref, dv_ref, dbg_ref,
             cw_ref, al_ref, dtb_ref, _, dc_ref, dba_ref, dcw_ref, dal_ref, ddt_ref):
        i = pl.program_id(0)
        first = i == 0

        @pl.when(first)
        def _():
            dcw_ref[...] = jnp.zeros_like(dcw_ref)
            dal_ref[...] = jnp.zeros_like(dal_ref)
            ddt_ref[...] = jnp.zeros_like(ddt_ref)

        cw_ = cw_ref[...]
        for idx, (cur, halo, dref) in enumerate(((q_ref, qh_ref, dq_ref), (k_ref, kh_ref, dk_ref),
                                                  (v_ref, vh_ref, dv_ref))):
            taps = _conv_taps(halo[...], cur[...], first)
            w = cw_[:, idx * D:(idx + 1) * D]
            cv = taps[0] * w[0:1, :]
            for j in range(1, _KC):
                cv = cv + taps[j] * w[j:j + 1, :]
            dact = dref[...]
            if idx < 2:
                outs, rs = _l2_heads(_silu(cv), H)
                pieces = []
                for h in range(H):
                    dy = dact[:, h * _GD:(h + 1) * _GD]
                    pieces.append(rs[h] * (dy - outs[h] * jnp.sum(dy * outs[h], axis=1, keepdims=True)))
                dact = jnp.concatenate(pieces, axis=1)
            dcv = dact * _dsilu(cv)
            dc_ref[:, idx * D:(idx + 1) * D] = dcv
            for j in range(_KC):
                colsum = _dot(jnp.ones((8, tb), _F32), dcv * taps[j])
                dcw_ref[j:j + 1, idx * D:(idx + 1) * D] += colsum[0:1, :]

        ba = ba_ref[...]
        lane, beta, g = _gate_rows(ba, al_ref[...], dtb_ref[...], H)
        dbg_ = dbg_ref[...]
        is_b, is_a = lane < H, jnp.logical_and(lane >= H, lane < 2 * H)
        da = dbg_ * (-jnp.exp(al_ref[...])) * _sigmoid(ba + dtb_ref[...])
        dba_ref[...] = jnp.where(is_b, dbg_ * beta * (1.0 - beta), jnp.where(is_a, da, 0.0)).astype(dba_ref.dtype)
        dal_ref[...] += jnp.sum(jnp.where(is_a, dbg_ * g, 0.0), axis=0, keepdims=True)
        ddt_ref[...] += jnp.sum(jnp.where(is_a, da, 0.0), axis=0, keepdims=True)

    return pl.pallas_call(
        body, name=name, grid=(T // tb,),
        in_specs=[_rspec(tb, D, 2), _rspec(tb, D, 3), _rspec(tb, D, 4),
                  _prev_spec(tb, D, 2), _prev_spec(tb, D, 3), _prev_spec(tb, D, 4),
                  _rspec(tb, _LANE, bac), _rspec(tb, D), _rspec(tb, D), _rspec(tb, D), _rspec(tb, _LANE),
                  _lspec((_KC, 3 * D), li), _lspec((1, _LANE), li), _lspec((1, _LANE), li), _HBM],
        out_specs=[_rspec(tb, 3 * D), _rspec(tb, _LANE, bac), _fspec((_KC, 3 * D)), _fspec((1, _LANE)),
                   _fspec((1, _LANE))],
        out_shape=[jax.ShapeDtypeStruct((T, 3 * D), _F32), jax.ShapeDtypeStruct(into.shape, into.dtype),
                   jax.ShapeDtypeStruct((_KC, 3 * D), _F32), jax.ShapeDtypeStruct((1, _LANE), _F32),
                   jax.ShapeDtypeStruct((1, _LANE), _F32)],
        input_output_aliases={14: 1}, compiler_params=_cp("arbitrary"),
    )(proj, proj, proj, proj, proj, proj, proj, dqn, dkn, dvs, dbg, cw, alog_row, dtb_row, into)


def _conv_bwd2(dc, cw, li, into, name, tb=256):
    T, W3 = dc.shape
    W = W3 // 3
    tb = _pick(T, tb, 8)
    nb8 = T // 8
    nrow = T // tb

    def body(dc_ref, nx_ref, cw_ref, _, o_ref):
        last = pl.program_id(0) == nrow - 1
        full = jnp.concatenate([dc_ref[...], jnp.where(last, 0.0, nx_ref[...])], axis=0)
        w = cw_ref[...]
        acc = full[:tb] * w[_KC - 1:_KC, :]
        for j in range(_KC - 1):
            sh = _KC - 1 - j
            acc = acc + pltpu.roll(full, tb + 8 - sh, 0)[:tb] * w[j:j + 1, :]
        o_ref[...] = acc.astype(o_ref.dtype)

    return pl.pallas_call(
        body, name=name, grid=(nrow, 3),
        in_specs=[pl.BlockSpec((tb, W), lambda i, j: (i, j)),
                  pl.BlockSpec((8, W), lambda i, j: (jnp.minimum((i + 1) * (tb // 8), nb8 - 1), j)),
                  pl.BlockSpec((None, _KC, W), lambda i, j: (li, 0, j)), _HBM],
        out_specs=pl.BlockSpec((tb, W), lambda i, j: (i, 2 + j)),
        out_shape=jax.ShapeDtypeStruct(into.shape, into.dtype), input_output_aliases={3: 0},
        compiler_params=_cp("parallel", "parallel"),
    )(dc, dc, cw, into)


def _split(a):
    hi = a.astype(_BF)
    return hi, (a - hi.astype(_F32)).astype(_BF)


def _dot3(a, b):
    (ah, al), (bh, bl) = a, b
    f = functools.partial(lax.dot_general, dimension_numbers=_NN, preferred_element_type=_F32)
    return f(ah, bh) + f(ah, bl) + f(al, bh)


def _inv_unit_lower(mats):
    C = mats[0].shape[0]
    ii = lax.broadcasted_iota(jnp.int32, (C, C), 0)
    jj = lax.broadcasted_iota(jnp.int32, (C, C), 1)
    xs = [jnp.where(ii == jj, 1.0, 0.0) - a for a in mats]
    ps = list(mats)
    n = 1
    while 2 * n < C:
        sp = [_split(p) for p in ps]
        ps = [_dot3(s, s) for s in sp]
        sp = [_split(p) for p in ps]
        xs = [x + _dot3(_split(x), s) for x, s in zip(xs, sp)]
        n *= 2
    return xs


def _gdn_chunk(q, k, v, g_row, b_row):
    C = q.shape[0]
    ii = lax.broadcasted_iota(jnp.int32, (C, C), 0)
    jj = lax.broadcasted_iota(jnp.int32, (C, C), 1)
    low, strict, eye = jj <= ii, jj < ii, ii == jj
    g_col = jnp.sum(jnp.where(eye, g_row, 0.0), axis=1, keepdims=True)
    b_col = jnp.sum(jnp.where(eye, b_row, 0.0), axis=1, keepdims=True)
    gam_col = jnp.sum(jnp.where(low, g_row, 0.0), axis=1, keepdims=True)
    gam_row = jnp.sum(jnp.where(jj >= ii, g_col, 0.0), axis=0, keepdims=True)
    gam_last = jnp.sum(g_row, axis=1, keepdims=True)
    decay = jnp.where(low, jnp.exp(jnp.where(low, gam_col - gam_row, 0.0)), 0.0)
    eg = jnp.exp(gam_col)
    ekd = jnp.exp(gam_last - gam_col)
    qs = q * (_GD ** -0.5)
    kb = k * b_col
    kk = _dot(kb, k, _NT)
    qkraw = _dot(qs, k, _NT)
    return dict(low=low, strict=strict, eye=eye, ii=ii, jj=jj, b_col=b_col, decay=decay, eg=eg, ekd=ekd,
                gl=jnp.exp(gam_last), qs=qs, kb=kb, kk=kk, qkraw=qkraw,
                A=jnp.where(strict, kk * decay, 0.0), vb=v * b_col, kbg=kb * eg,
                qk=qkraw * decay, q_dec=qs * eg, k_dec=k * ekd)


def _gdn_fwd(qn, kn, vs, g_r, b_r, name):
    T, D = qn.shape
    H, N, C = D // _GD, T // _BC, _BC
    hb = min(_HB, H)

    def body(q_ref, k_ref, v_ref, g_ref, b_ref, o_ref, s_ref, t_ref, S):
        @pl.when(pl.program_id(1) == 0)
        def _():
            S[...] = jnp.zeros_like(S)

        hs = range(hb)
        sls = [slice(hh * _GD, (hh + 1) * _GD) for hh in hs]
        cms = [_gdn_chunk(q_ref[:, sl], k_ref[:, sl], v_ref[:, sl], g_ref[hh], b_ref[hh]) for hh, sl in zip(hs, sls)]
        tms = _inv_unit_lower([cm["A"] for cm in cms])
        us = [_dot(tm, cm["vb"]) for tm, cm in zip(tms, cms)]
        ws = [_dot(tm, cm["kbg"]) for tm, cm in zip(tms, cms)]
        s0s = [S[hh] for hh in hs]
        for hh in hs:
            s_ref[hh] = s0s[hh]
            t_ref[hh] = tms[hh]
        v_news = [u - _dot(w, s0) for u, w, s0 in zip(us, ws, s0s)]
        qss = [_dot(cm["q_dec"], s0) for cm, s0 in zip(cms, s0s)]
        for hh in hs:
            o_ref[:, sls[hh]] = qss[hh] + _dot(cms[hh]["qk"], v_news[hh])
        for hh in hs:
            S[hh] = s0s[hh] * cms[hh]["gl"] + _dot(cms[hh]["k_dec"], v_news[hh], _TN)

    qspec = pl.BlockSpec((C, hb * _GD), lambda h, n: (n, h))
    gspec = pl.BlockSpec((hb, None, 1, C), lambda h, n: (h, n, 0, 0))
    return pl.pallas_call(
        body, name=name, grid=(H // hb, N),
        in_specs=[qspec, qspec, qspec, gspec, gspec],
        out_specs=[qspec, pl.BlockSpec((hb, None, _GD, _GD), lambda h, n: (h, n, 0, 0)),
                   pl.BlockSpec((hb, None, C, C), lambda h, n: (h, n, 0, 0))],
        out_shape=[jax.ShapeDtypeStruct((T, D), _F32), jax.ShapeDtypeStruct((H, N, _GD, _GD), _F32),
                   jax.ShapeDtypeStruct((H, N, C, C), _F32)],
        scratch_shapes=[pltpu.VMEM((hb, _GD, _GD), _F32)],
        compiler_params=_cp("arbitrary", "arbitrary"),
    )(qn, kn, vs, g_r, b_r)


def _gdn_bwd(qn, kn, vs, g_r, b_r, s_all, t_all, do, name):
    T, D = qn.shape
    H, N, C = D // _GD, T // _BC, _BC
    hb = min(_HB, H)

    def body(q_ref, k_ref, v_ref, g_ref, b_ref, s_ref, t_ref, do_ref, dq_ref, dk_ref, dv_ref, dg_ref, db_ref, dS):
        @pl.when(pl.program_id(1) == 0)
        def _():
            dS[...] = jnp.zeros_like(dS)

        hs = range(hb)
        sls = [slice(hh * _GD, (hh + 1) * _GD) for hh in hs]
        ks = [k_ref[:, sl] for sl in sls]
        vs_ = [v_ref[:, sl] for sl in sls]
        cms = [_gdn_chunk(q_ref[:, sl], k, v, g_ref[hh], b_ref[hh]) for hh, sl, k, v in zip(hs, sls, ks, vs_)]
        low, strict, eye, ii, jj = (cms[0][n] for n in ("low", "strict", "eye", "ii", "jj"))
        tms, s0s, dos, ds1s = [t_ref[hh] for hh in hs], [s_ref[hh] for hh in hs], [do_ref[:, sl] for sl in sls], \
            [dS[hh] for hh in hs]
        us = [_dot(tm, cm["vb"]) for tm, cm in zip(tms, cms)]
        ws = [_dot(tm, cm["kbg"]) for tm, cm in zip(tms, cms)]
        v_news = [u - _dot(w, s0) for u, w, s0 in zip(us, ws, s0s)]
        dv_news = [_dot(cm["qk"], do_, _TN) + _dot(cm["k_dec"], ds1) for cm, do_, ds1 in zip(cms, dos, ds1s)]
        dqks = [jnp.where(low, _dot(do_, vn, _NT), 0.0) for do_, vn in zip(dos, v_news)]
        dq_decs = [_dot(do_, s0, _NT) for do_, s0 in zip(dos, s0s)]
        dk_decs = [_dot(vn, ds1, _NT) for vn, ds1 in zip(v_news, ds1s)]
        dgls = [jnp.sum(jnp.sum(ds1 * s0, axis=1, keepdims=True), axis=0, keepdims=True) for ds1, s0 in zip(ds1s, s0s)]
        dws = [-_dot(dvn, s0, _NT) for dvn, s0 in zip(dv_news, s0s)]
        for hh in hs:
            dS[hh] = (_dot(cms[hh]["q_dec"], dos[hh], _TN) + cms[hh]["gl"] * ds1s[hh]
                      - _dot(ws[hh], dv_news[hh], _TN))
        dvbs = [_dot(tm, dvn, _TN) for tm, dvn in zip(tms, dv_news)]
        dkbgs = [_dot(tm, dw, _TN) for tm, dw in zip(tms, dws)]
        dAs = [-jnp.where(strict, _dot(dvb, u, _NT) + _dot(dkbg, w, _NT), 0.0)
               for dvb, u, dkbg, w in zip(dvbs, us, dkbgs, ws)]
        dkks = [dA * cm["decay"] for dA, cm in zip(dAs, cms)]
        dqkraws = [dqk * cm["decay"] for dqk, cm in zip(dqks, cms)]
        Es = [(dA * cm["kk"] + dqk * cm["qkraw"]) * cm["decay"] for dA, dqk, cm in zip(dAs, dqks, cms)]
        dkbs = [_dot(dkk, k) + dkbg * cm["eg"] for dkk, k, dkbg, cm in zip(dkks, ks, dkbgs, cms)]
        dqss = [_dot(dqr, k) + dqd * cm["eg"] for dqr, k, dqd, cm in zip(dqkraws, ks, dq_decs, cms)]
        for hh in hs:
            cm = cms[hh]
            dk_ref[:, sls[hh]] = (_dot(dqkraws[hh], cm["qs"], _TN) + _dot(dkks[hh], cm["kb"], _TN)
                                  + dk_decs[hh] * cm["ekd"] + dkbs[hh] * cm["b_col"])
            dv_ref[:, sls[hh]] = dvbs[hh] * cm["b_col"]
            dq_ref[:, sls[hh]] = dqss[hh] * (_GD ** -0.5)
        for hh in hs:
            cm, k, E = cms[hh], ks[hh], Es[hh]
            eg, ekd = cm["eg"], cm["ekd"]
            dbeta_col = jnp.sum(dvbs[hh] * vs_[hh] + dkbs[hh] * k, axis=1, keepdims=True)
            t_kd = jnp.sum(dk_decs[hh] * k, axis=1, keepdims=True) * ekd
            c1 = (jnp.sum(E, axis=1, keepdims=True) + jnp.sum(dkbgs[hh] * cm["kb"], axis=1, keepdims=True) * eg
                  + jnp.sum(dq_decs[hh] * cm["qs"], axis=1, keepdims=True) * eg - t_kd)
            r1 = jnp.sum(E, axis=0, keepdims=True)
            dgam_last = jnp.sum(t_kd, axis=0, keepdims=True) + dgls[hh] * cm["gl"]
            dgam_col = c1 - jnp.sum(jnp.where(eye, r1, 0.0), axis=1, keepdims=True)
            dg_ref[hh] = jnp.sum(jnp.where(ii >= jj, dgam_col, 0.0), axis=0, keepdims=True) + dgam_last
            db_ref[hh] = jnp.sum(jnp.where(eye, dbeta_col, 0.0), axis=0, keepdims=True)

    qspec = pl.BlockSpec((C, hb * _GD), lambda h, n: (N - 1 - n, h))
    gspec = pl.BlockSpec((hb, None, 1, C), lambda h, n: (h, N - 1 - n, 0, 0))
    return pl.pallas_call(
        body, name=name, grid=(H // hb, N),
        in_specs=[qspec, qspec, qspec, gspec, gspec,
                  pl.BlockSpec((hb, None, _GD, _GD), lambda h, n: (h, N - 1 - n, 0, 0)),
                  pl.BlockSpec((hb, None, C, C), lambda h, n: (h, N - 1 - n, 0, 0)), qspec],
        out_specs=[qspec, qspec, qspec, gspec, gspec],
        out_shape=[jax.ShapeDtypeStruct((T, D), _F32)] * 3 + [jax.ShapeDtypeStruct((H, N, 1, C), _F32)] * 2,
        scratch_shapes=[pltpu.VMEM((hb, _GD, _GD), _F32)],
        compiler_params=_cp("arbitrary", "arbitrary"),
    )(qn, kn, vs, g_r, b_r, s_all, t_all, do)


def _onorm_fwd(o, proj, go, li, D, name, tb=512):
    T = o.shape[0]
    H = D // _GD
    tb = _pick(T, tb, 8)

    def body(o_ref, z_ref, go_ref, y_ref):
        ov, zv, g = o_ref[...], z_ref[...], go_ref[...]
        for h in range(H):
            sl = slice(h * _GD, (h + 1) * _GD)
            oh = ov[:, sl]
            r = lax.rsqrt(jnp.mean(oh * oh, axis=1, keepdims=True) + _EPS)
            y_ref[:, sl] = (oh * r * g * _silu(zv[:, sl])).astype(y_ref.dtype)

    return pl.pallas_call(
        body, name=name, grid=(T // tb,), in_specs=[_rspec(tb, D), _rspec(tb, D, 5), _lspec((1, _GD), li)],
        out_specs=_rspec(tb, D), out_shape=jax.ShapeDtypeStruct((T, D), _MMT), compiler_params=_cp("parallel"),
    )(o, proj, go)


def _onorm_bwd(dy, o, proj, go, li, D, into, name, tb=256):
    T = o.shape[0]
    H = D // _GD
    tb = _pick(T, tb, 8)

    def body(dy_ref, o_ref, z_ref, go_ref, _, do_ref, dz_ref, dgo_ref):
        @pl.when(pl.program_id(0) == 0)
        def _():
            dgo_ref[...] = jnp.zeros_like(dgo_ref)

        dyv, ov, zv, g = dy_ref[...].astype(_F32), o_ref[...], z_ref[...], go_ref[...]
        dgo = jnp.zeros((1, _GD), _F32)
        for h in range(H):
            sl = slice(h * _GD, (h + 1) * _GD)
            oh, zh, dyh = ov[:, sl], zv[:, sl], dyv[:, sl]
            r = lax.rsqrt(jnp.mean(oh * oh, axis=1, keepdims=True) + _EPS)
            on = oh * r
            sz = _silu(zh)
            dgo = dgo + jnp.sum(dyh * sz * on, axis=0, keepdims=True)
            don = dyh * sz * g
            do_ref[:, sl] = r * (don - on * jnp.mean(don * on, axis=1, keepdims=True))
            dz_ref[:, sl] = (dyh * on * g * _dsilu(zh)).astype(dz_ref.dtype)
        dgo_ref[...] += dgo

    return pl.pallas_call(
        body, name=name, grid=(T // tb,),
        in_specs=[_rspec(tb, D), _rspec(tb, D), _rspec(tb, D, 5), _lspec((1, _GD), li), _HBM],
        out_specs=[_rspec(tb, D), _rspec(tb, D, 5), _fspec((1, _GD))],
        out_shape=[jax.ShapeDtypeStruct((T, D), _F32), jax.ShapeDtypeStruct(into.shape, into.dtype),
                   jax.ShapeDtypeStruct((1, _GD), _F32)],
        input_output_aliases={4: 1}, compiler_params=_cp("arbitrary"),
    )(dy, o, proj, go, into)


def _merge_bwd(dm, pa, pb, proj, D, name, tb=256):
    T, PW = proj.shape
    tb = _pick(T, tb, 8)

    def body(dm_ref, pa_ref, pb_ref, ga_ref, gb_ref, dpa_ref, dpb_ref, dg_ref):
        d = dm_ref[...].astype(_F32)
        sa, sb = _sigmoid(ga_ref[...]), _sigmoid(gb_ref[...])
        dpa_ref[...] = (d * sa).astype(dpa_ref.dtype)
        dpb_ref[...] = (d * sb).astype(dpb_ref.dtype)
        dg_ref[:, :D] = (d * pa_ref[...].astype(_F32) * sa * (1.0 - sa)).astype(dg_ref.dtype)
        dg_ref[:, D:] = (d * pb_ref[...].astype(_F32) * sb * (1.0 - sb)).astype(dg_ref.dtype)

    return pl.pallas_call(
        body, name=name, grid=(T // tb,),
        in_specs=[_rspec(tb, D), _rspec(tb, D), _rspec(tb, D), _rspec(tb, D, 6), _rspec(tb, D, 7)],
        out_specs=[_rspec(tb, D), _rspec(tb, D), _rspec(tb, 2 * D, 3)],
        out_shape=[jax.ShapeDtypeStruct((T, D), _MMT)] * 2 + [jax.ShapeDtypeStruct((T, PW), _MMT)],
        compiler_params=_cp("parallel"),
    )(dm, pa, pb, proj, proj)


def _loss_head(x, tgt, fg, name, tb=256):
    T, D = x.shape
    tb = _pick(T, tb, 8)

    def body(x_ref, t_ref, fg_ref, loss_ref, dx_ref, dfg_ref):
        @pl.when(pl.program_id(0) == 0)
        def _():
            loss_ref[...] = jnp.zeros_like(loss_ref)
            dfg_ref[...] = jnp.zeros_like(dfg_ref)

        xv, fg_ = x_ref[...], fg_ref[...]
        r = lax.rsqrt(jnp.mean(xv * xv, axis=1, keepdims=True) + _EPS)
        xn = xv * r
        e = xn * fg_ - t_ref[...]
        loss_ref[...] += (0.5 / D) * jnp.sum(jnp.sum(e * e, axis=1, keepdims=True), axis=0, keepdims=True)
        dy = e * (1.0 / D)
        dfg_ref[...] += jnp.sum(dy * xn, axis=0, keepdims=True)
        dxn = dy * fg_
        dx_ref[...] = r * (dxn - xn * jnp.mean(dxn * xn, axis=1, keepdims=True))

    return pl.pallas_call(
        body, name=name, grid=(T // tb,), in_specs=[_rspec(tb, D), _rspec(tb, D), _fspec((1, D))],
        out_specs=[_fspec((1, 1)), _rspec(tb, D), _fspec((1, D))],
        out_shape=[jax.ShapeDtypeStruct((1, 1), _F32), jax.ShapeDtypeStruct((T, D), _F32),
                   jax.ShapeDtypeStruct((1, D), _F32)],
        compiler_params=_cp("arbitrary"),
    )(x, tgt, fg)


def _row_tile(R, W, budget=1 << 20, unit=8):
    if R * W * 4 <= budget or R % unit:
        return R
    best = unit
    for t in range(unit, R + 1, unit):
        if R % t == 0 and t * W * 4 <= budget:
            best = t
    return best


def _add_own_half(send, got, half, name):
    P, Rp, W = send.shape
    Rh = Rp // 2
    tb = _row_tile(Rh, W, 1 << 21, 16)

    def body(h_ref, a_ref, b_ref, o_ref):
        o_ref[...] = (a_ref[...].astype(_F32) + b_ref[...].astype(_F32)).astype(o_ref.dtype)

    return pl.pallas_call(
        body, name=name,
        grid_spec=pltpu.PrefetchScalarGridSpec(
            num_scalar_prefetch=1, grid=(P, Rh // tb),
            in_specs=[pl.BlockSpec((None, None, tb, W), lambda k, i, h: (k, h[0], i, 0)),
                      pl.BlockSpec((None, tb, W), lambda k, i, h: (k, i, 0))],
            out_specs=pl.BlockSpec((None, tb, W), lambda k, i, h: (k, i, 0))),
        out_shape=jax.ShapeDtypeStruct((P, Rh, W), send.dtype), compiler_params=_cp("parallel", "parallel"),
    )(half, send.reshape(P, 2, Rh, W), got)


def _sum_slots(st, name):
    P, R, W = st.shape
    tb = _row_tile(R, W, 1 << 20, 16)

    def body(s_ref, o_ref):
        acc = s_ref[0].astype(_F32)
        for p in range(1, P):
            acc = acc + s_ref[p].astype(_F32)
        o_ref[...] = acc

    return pl.pallas_call(
        body, name=name, grid=(R // tb,), in_specs=[pl.BlockSpec((P, tb, W), lambda i: (0, i, 0))],
        out_specs=_rspec(tb, W), out_shape=jax.ShapeDtypeStruct((R, W), _F32), compiler_params=_cp("parallel"),
    )(st)


def _adamw(w, gst, m, v, name):
    R, W = w.shape
    P = gst.shape[0]
    tb = _row_tile(R, W, 1 << 20)
    c1, c2 = 1.0 - _B1 ** _STEP, 1.0 - _B2 ** _STEP

    def body(w_ref, g_ref, m_ref, v_ref, go_ref, d_ref, mo_ref, vo_ref):
        g = g_ref[0]
        for p in range(1, P):
            g = g + g_ref[p]
        mn = _B1 * m_ref[...] + (1.0 - _B1) * g
        vn = _B2 * v_ref[...] + (1.0 - _B2) * (g * g)
        go_ref[...] = g
        mo_ref[...] = mn
        vo_ref[...] = vn
        d_ref[...] = -_LR * ((mn / c1) / (jnp.sqrt(vn / c2) + _AEPS) + _WD * w_ref[...])

    spec = _rspec(tb, W)
    return pl.pallas_call(
        body, name=name, grid=(R // tb,),
        in_specs=[spec, pl.BlockSpec((P, tb, W), lambda i: (0, i, 0)), spec, spec],
        out_specs=[spec] * 4, out_shape=[jax.ShapeDtypeStruct((R, W), _F32)] * 4, compiler_params=_cp("parallel"),
    )(w, gst, m, v)


def _as2d(a):
    if a.ndim == 1:
        return a.reshape(1, -1)
    return a.reshape(-1, a.shape[-1])


def kernel(x, c, ada_w, ada_b, norm1_g, w_in, conv_w, spatial_w, spatial_b, v_norm_g, a_log, dt_bias, o_norm_g, w_branch_a, w_branch_b, w_out, norm2_g, w_ffn_in, w_ffn_out, final_g, loss_target, m_ada_w, m_ada_b, m_norm1_g, m_w_in, m_conv_w, m_spatial_w, m_spatial_b, m_v_norm_g, m_a_log, m_dt_bias, m_o_norm_g, m_w_branch_a, m_w_branch_b, m_w_out, m_norm2_g, m_w_ffn_in, m_w_ffn_out, m_final_g, v_ada_w, v_ada_b, v_norm1_g, v_w_in, v_conv_w, v_spatial_w, v_spatial_b, v_v_norm_g, v_a_log, v_dt_bias, v_o_norm_g, v_w_branch_a, v_w_branch_b, v_w_out, v_norm2_g, v_w_ffn_in, v_w_ffn_out, v_final_g):
    xb, tgt = x[0], loss_target[0]
    T, D = xb.shape
    L, H, G = ada_w.shape[0], a_log.shape[1], spatial_w.shape[1]
    F = 4 * w_ffn_out.shape[1]
    N = T // _BC
    Ws = ada_w.shape[2]
    Wc = w_in.shape[2]
    PW = 8 * D + _LANE
    ix, iy, ic = lax.axis_index("x"), lax.axis_index("y"), lax.axis_index("c")
    me = 4 * ix + 2 * iy + ic

    c_all = _gather8(c, "gather_c").reshape(8, D)
    modp = _ada_fwd(c_all, ada_w, "ada_fwd")
    n_mod, n_cw = L * 8 * Ws, L * _KC * conv_w.shape[2]
    pad = (-(n_mod + n_cw)) % _LANE
    pay = jnp.concatenate([modp.reshape(-1), conv_w.reshape(-1), jnp.zeros((pad,), _F32)]).reshape(-1, _LANE)
    pay_all = _gather8(pay, "gather_mod").reshape(8, -1)
    mod_full = jnp.concatenate([pay_all[2 * k, :n_mod].reshape(L, 8, Ws) for k in range(4)], axis=-1)
    cw_full = jnp.concatenate([pay_all[2 * k, n_mod:n_mod + n_cw].reshape(L, _KC, -1) for k in range(4)], axis=-1)
    mod = lax.dynamic_index_in_dim(mod_full, me, axis=1, keepdims=False) + ada_b
    mods = [[mod[l, j * D:(j + 1) * D].reshape(1, D) for j in range(6)] for l in range(L)]

    big = [w_in, w_branch_a, w_branch_b, w_out, w_ffn_in, w_ffn_out]
    chip = 2 * ix + iy
    starts = [(k * Wc) // 16 * 16 for k in range(4)]
    Hh = max(-(-((k + 1) * Wc) // 16) * 16 - starts[k] for k in range(4))
    No = max(s + Hh for s in starts)
    my_off = jnp.asarray([k * Wc - starts[k] for k in range(4)], jnp.int32)[chip]
    cuts = sorted(set(starts + [s + Hh for s in starts]))

    pers = [Hh, D // 4, D // 4, D // 4, 2 * F // 4, F // 4]
    roff = [0]
    for p in pers:
        roff.append(roff[-1] + L * p)
    Rp = -(-roff[-1] // (32 * _NCH)) * (32 * _NCH)
    rpad = Rp - roff[-1]

    hull = lax.dynamic_update_slice(jnp.zeros((L, Hh, D), _F32), jnp.transpose(w_in, (0, 2, 1)), (0, my_off, 0))
    shard = jnp.concatenate(
        [hull.reshape(-1, D).astype(_MMT), w_branch_a.reshape(-1, D).astype(_MMT),
         w_branch_b.reshape(-1, D).astype(_MMT), w_out.reshape(-1, D).astype(_MMT),
         jnp.transpose(w_ffn_in, (0, 2, 1)).reshape(-1, D).astype(_MMT), w_ffn_out.reshape(-1, D).astype(_MMT),
         jnp.zeros((rpad, D), _MMT)], axis=0)
    gw = _fill_from_sibling(_gather_chips(shard, "gather_w"), "gather_w_sib")

    def slab(i, l, k):
        a = roff[i] + l * pers[i]
        return gw[k, a:a + pers[i]]

    def joined(i, l):
        return jnp.concatenate([slab(i, l, k) for k in range(4)], axis=0)

    def orig_rows(hulls, a, b):
        edges = sorted(set([a, b] + [c_ for c_ in cuts if a < c_ < b]))
        out = []
        for lo, hi in zip(edges[:-1], edges[1:]):
            cov = [k for k in range(4) if starts[k] <= lo and hi <= starts[k] + Hh]
            piece = hulls[cov[0]][lo - starts[cov[0]]:hi - starts[cov[0]]]
            for k in cov[1:]:
                piece = piece + hulls[k][lo - starts[k]:hi - starts[k]]
            out.append(piece)
        return out

    wt_in_p = []
    for l in range(L):
        hulls = [slab(0, l, k) for k in range(4)]
        wt_in_p.append(jnp.concatenate(
            orig_rows(hulls, 0, 6 * D) + orig_rows(hulls, 6 * D + 2 * H, 8 * D + 2 * H)
            + orig_rows(hulls, 6 * D, 6 * D + 2 * H) + [jnp.zeros((_LANE - 2 * H, D), _MMT)], axis=0))
    w_a, w_b, w_o, wt_fi, w_fo = ([joined(i, l) for l in range(L)] for i in range(1, 6))

    sbt = jnp.transpose(spatial_b, (0, 2, 1))
    gv3, go3 = v_norm_g.reshape(L, 1, D), o_norm_g.reshape(L, 1, _GD)
    zpad = jnp.zeros((L, _LANE - 2 * H), _F32)
    alog_row = jnp.concatenate([jnp.zeros((L, H), _F32), a_log, zpad], axis=1).reshape(L, 1, _LANE)
    dtb_row = jnp.concatenate([jnp.zeros((L, H), _F32), dt_bias, zpad], axis=1).reshape(L, 1, _LANE)

    def rows_of(tok):
        return jnp.transpose(tok.reshape(N, _BC, H), (2, 0, 1)).reshape(H, N, 1, _BC)

    def toks_of(rows):
        return jnp.transpose(rows.reshape(H, N, _BC), (1, 2, 0)).reshape(T, H)

    saved = []
    xc = xb
    for l in range(L):
        sh1, sc1, gt1, sh2, sc2, gt2 = mods[l]
        g1, g2 = norm1_g[l].reshape(1, D), norm2_g[l].reshape(1, D)
        h = _norm_mod(xc, g1, sc1, sh1, f"norm1_{l}")
        proj = _mm_nt(h, wt_in_p[l], f"proj_{l}", tn=1664)
        ya = _gmlp_fwd(proj, spatial_w, sbt, gv3, l, D, f"gmlp_{l}")
        qn, kn, vs, bg = _conv_fwd(proj, cw_full, alog_row, dtb_row, l, D, f"conv_{l}")
        g_r, b_r = rows_of(bg[:, H:2 * H]), rows_of(bg[:, :H])
        o, s_all, t_all = _gdn_fwd(qn, kn, vs, g_r, b_r, f"gdn_{l}")
        yb = _onorm_fwd(o, proj, go3, l, D, f"onorm_{l}")
        pa, pb, mg = _branch_merge(ya, yb, w_a[l], w_b[l], proj, f"branch_{l}")
        p1, x1 = _mm_res(mg, w_o[l], xc, gt1, f"wout_{l}")
        h2 = _norm_mod(x1, g2, sc2, sh2, f"norm2_{l}")
        gate, up, act = _ffin_swiglu(h2, wt_fi[l], f"ffin_{l}")
        p2, x2 = _mm_res(act, w_fo[l], x1, gt2, f"ffout_{l}")
        saved.append(dict(x=xc, h=h, proj=proj, ya=ya, yb=yb, qn=qn, kn=kn, vs=vs, g_r=g_r, b_r=b_r, o=o,
                          s_all=s_all, t_all=t_all, pa=pa, pb=pb, mg=mg, p1=p1, x1=x1, h2=h2, gate=gate, up=up,
                          act=act, p2=p2))
        xc = x2

    loss11, dx, dfg = _loss_head(xc, tgt, final_g.reshape(1, D), "loss_head")
    loss = lax.psum(loss11[0, 0], ("x", "y", "c"))

    gbig = {k: None for k in ("w_in", "w_a", "w_b", "w_o", "w_fi", "w_fo")}
    small = {k: [None] * L for k in ("dmod", "n1", "n2", "sw", "sb", "gv", "cw", "al", "dt", "go")}
    for l in reversed(range(L)):
        sv = saved[l]
        sh1, sc1, gt1, sh2, sc2, gt2 = mods[l]
        g1, g2 = norm1_g[l].reshape(1, D), norm2_g[l].reshape(1, D)
        proj = sv["proj"]
        dp2, dgt2 = _resid_bwd(dx, sv["p2"], gt2, f"res2b_{l}")
        dgate, dup = _ffoutb_swiglu(dp2, w_fo[l], sv["gate"], sv["up"], f"ffoutb_{l}")
        gbig["w_fo"] = _mm_tn(sv["act"], dp2, f"ffoutw_{l}", l, L, gbig["w_fo"], tm=1408)
        dx1, dgm2, dsh2 = _mm_normb((dgate, dup), wt_fi[l], sv["x1"], dx, g2, sc2, f"ffinb_{l}")
        gbig["w_fi"] = _mm_tn(dgate, sv["h2"], f"ffinwg_{l}", l, L, gbig["w_fi"], mtot=2 * F, tm=1408)
        gbig["w_fi"] = _mm_tn(dup, sv["h2"], f"ffinwu_{l}", l, L, gbig["w_fi"], row0=F, mtot=2 * F, tm=1408)
        dp1, dgt1 = _resid_bwd(dx1, sv["p1"], gt1, f"res1b_{l}")
        dmg = _mm_nt(dp1, w_o[l], f"woutb_{l}", out_dtype=_MMT)
        gbig["w_o"] = _mm_tn(sv["mg"], dp1, f"woutw_{l}", l, L, gbig["w_o"])
        dpa, dpb, dproj = _merge_bwd(dmg, sv["pa"], sv["pb"], proj, D, f"mergeb_{l}")
        dya = _mm_nt(dpa, w_a[l], f"brab_{l}", out_dtype=_MMT)
        gbig["w_a"] = _mm_tn(sv["ya"], dpa, f"braw_{l}", l, L, gbig["w_a"])
        dyb = _mm_nt(dpb, w_b[l], f"brbb_{l}", out_dtype=_MMT)
        gbig["w_b"] = _mm_tn(sv["yb"], dpb, f"brbw_{l}", l, L, gbig["w_b"])
        dproj, dsw, dsa, dgv = _gmlp_bwd(proj, dya, spatial_w, sbt, gv3, l, D, dproj, f"gmlpb_{l}")
        do, dproj, dgo = _onorm_bwd(dyb, sv["o"], proj, go3, l, D, dproj, f"onormb_{l}")
        dqn, dkn, dvs, dg_r, db_r = _gdn_bwd(sv["qn"], sv["kn"], sv["vs"], sv["g_r"], sv["b_r"], sv["s_all"],
                                             sv["t_all"], do, f"gdnb_{l}")
        dbg = jnp.concatenate([toks_of(db_r), toks_of(dg_r), jnp.zeros((T, _LANE - 2 * H), _F32)], axis=1)
        dc, dproj, dcw, dal, ddt = _conv_bwd1(proj, dqn, dkn, dvs, dbg, cw_full, alog_row, dtb_row, l, D, dproj,
                                              f"convb_{l}")
        dproj = _conv_bwd2(dc, cw_full, l, dproj, f"convx_{l}")
        gbig["w_in"] = _mm_tn(dproj, sv["h"], f"projw_{l}", l, L, gbig["w_in"], tm=640)
        dx, dgm1, dsh1 = _mm_normb(dproj, wt_in_p[l], sv["x"], dx1, g1, sc1, f"projb_{l}", tk=1664)
        small["dmod"][l] = jnp.concatenate([dsh1, dgm1 * g1, dgt1, dsh2, dgm2 * g2, dgt2], axis=1)
        small["n1"][l], small["n2"][l] = dgm1 * (1.0 + sc1), dgm2 * (1.0 + sc2)
        small["sw"][l], small["gv"][l], small["cw"][l], small["go"][l] = dsw, dgv, dcw, dgo
        small["sb"][l] = jnp.transpose(dsa.reshape(_AC, G, _GD).sum(axis=-1))
        small["al"][l], small["dt"][l] = dal[:, H:2 * H], ddt[:, H:2 * H]
    grad_x = dx.reshape(1, T, D)

    names_small = ["dmod", "n1", "n2", "sw", "sb", "gv", "cw", "al", "dt", "go"]
    flat = [jnp.stack(small[k]).reshape(-1) for k in names_small] + [dfg.reshape(-1)]
    sizes = [f.shape[0] for f in flat]
    tot = sum(sizes)
    pad = (-tot) % 1024
    pay = jnp.concatenate(flat + [jnp.zeros((pad,), _F32)]).reshape(-1, 1024)
    sm_all = _gather8(pay, "gather_small").reshape(8, -1)
    offs = [0]
    for s in sizes:
        offs.append(offs[-1] + s)
    part = {k: sm_all[:, offs[i]:offs[i + 1]] for i, k in enumerate(names_small + ["fg"])}
    dmod_all = part["dmod"].reshape(8, L, 6 * D)

    outs = {}

    def update(nm, w, gst, m, v):
        shp = w.shape
        w2 = _as2d(w)
        g, d, mn, vn = _adamw(w2, gst.reshape((gst.shape[0],) + w2.shape), _as2d(m), _as2d(v), f"adamw_{nm}")
        outs[nm] = (g.reshape(shp), d.reshape(shp), mn.reshape(shp), vn.reshape(shp))

    chip = 2 * ix + iy
    dmod_t = jnp.transpose(dmod_all, (1, 0, 2))
    dmod_mine = lax.dynamic_slice_in_dim(dmod_t, chip * Ws, Ws, axis=2)
    g_ada_w = _ada_bwd(jnp.transpose(c_all), dmod_mine, "ada_bwd")
    update("ada_w", ada_w, g_ada_w[None], m_ada_w, v_ada_w)
    update("ada_b", ada_b, dmod_all, m_ada_b, v_ada_b)
    update("norm1_g", norm1_g, part["n1"], m_norm1_g, v_norm1_g)
    update("norm2_g", norm2_g, part["n2"], m_norm2_g, v_norm2_g)
    update("spatial_w", spatial_w, part["sw"], m_spatial_w, v_spatial_w)
    update("spatial_b", spatial_b, part["sb"], m_spatial_b, v_spatial_b)
    update("v_norm_g", v_norm_g, part["gv"], m_v_norm_g, v_v_norm_g)
    update("a_log", a_log, part["al"], m_a_log, v_a_log)
    update("dt_bias", dt_bias, part["dt"], m_dt_bias, v_dt_bias)
    update("o_norm_g", o_norm_g, part["go"], m_o_norm_g, v_o_norm_g)
    update("final_g", final_g, part["fg"], m_final_g, v_final_g)
    cw_cols = conv_w.shape[2]
    dcw_all = part["cw"].reshape(8, L, _KC, 4 * cw_cols)
    update("conv_w", conv_w, lax.dynamic_slice_in_dim(dcw_all, chip * cw_cols, cw_cols, axis=3), m_conv_w, v_conv_w)

    def hull_of(p, k):
        a, b = starts[k], starts[k] + Hh
        out = []
        for lo, hi, plo in ((0, 6 * D, 0), (6 * D, 6 * D + 2 * H, 8 * D), (6 * D + 2 * H, 8 * D + 2 * H, 6 * D),
                            (8 * D + 2 * H, No, None)):
            s, e = max(a, lo), min(b, hi)
            if s < e:
                out.append(jnp.zeros((L, e - s, D), _MMT) if plo is None else p[:, plo + s - lo:plo + e - lo])
        return (out[0] if len(out) == 1 else jnp.concatenate(out, axis=1)).reshape(L * Hh, D)

    pieces = []
    for k in range(4):
        pieces.append(hull_of(gbig["w_in"], k))
        for i, nm in enumerate(("w_a", "w_b", "w_o", "w_fi", "w_fo")):
            per = pers[i + 1]
            pieces.append(gbig[nm][:, k * per:(k + 1) * per].reshape(L * per, D))
        pieces.append(jnp.zeros((rpad, D), _MMT))
    send = jnp.concatenate(pieces, axis=0).reshape(4, Rp, D)
    got = _send_half_to_sibling(send, "reduce_cores")
    chipsum = _add_own_half(send, got, ic.astype(jnp.int32).reshape(1), "add_cores")
    parts = _scatter_to_chips(chipsum, "reduce_chips")
    mine = _sum_slots(parts, "add_chips")
    other = _swap_with_sibling(mine, "swap_cores")
    first = ic == 0
    gsum = jnp.concatenate([jnp.where(first, mine, other), jnp.where(first, other, mine)], axis=0)
    big_names = ["w_in", "w_branch_a", "w_branch_b", "w_out", "w_ffn_in", "w_ffn_out"]
    big_m = [m_w_in, m_w_branch_a, m_w_branch_b, m_w_out, m_w_ffn_in, m_w_ffn_out]
    big_v = [v_w_in, v_w_branch_a, v_w_branch_b, v_w_out, v_w_ffn_in, v_w_ffn_out]
    for i, (nm, w, m, v) in enumerate(zip(big_names, big, big_m, big_v)):
        g = gsum[roff[i]:roff[i + 1]].reshape(L, pers[i], D)
        if i == 0:
            g = jnp.transpose(lax.dynamic_slice_in_dim(g, my_off, Wc, axis=1), (0, 2, 1))
        elif i == 4:
            g = jnp.transpose(g, (0, 2, 1))
        update(nm, w, g[None], m, v)

    order = ["ada_w", "ada_b", "norm1_g", "w_in", "conv_w", "spatial_w", "spatial_b", "v_norm_g", "a_log", "dt_bias",
             "o_norm_g", "w_branch_a", "w_branch_b", "w_out", "norm2_g", "w_ffn_in", "w_ffn_out", "final_g"]
    return (loss, grad_x, *[outs[n][0] for n in order], *[outs[n][1] for n in order],
            *[outs[n][2] for n in order], *[outs[n][3] for n in order])
```

```python
import functools
import math

import jax
import jax.numpy as jnp
from jax import lax
from jax.experimental import pallas as pl
from jax.experimental.pallas import tpu as pltpu

_F32 = jnp.float32
_BF = jnp.bfloat16
_MMT = jnp.bfloat16
_EPS = 1e-6
_GD = 128
_AC = 128
_BC = 64
_KC = 4
_HB = 8
_NCH = 8
_LANE = 128
_VMEM_LIMIT = 56 * 1024 * 1024

_LR, _B1, _B2, _AEPS, _WD, _STEP = 0.001, 0.9, 0.999, 1e-08, 0.01, 10

_NN = (((1,), (0,)), ((), ()))
_NT = (((1,), (1,)), ((), ()))
_TN = (((0,), (0,)), ((), ()))

_MESH = pl.DeviceIdType.MESH


def _cp(*sem):
    return pltpu.CompilerParams(dimension_semantics=tuple(sem), vmem_limit_bytes=_VMEM_LIMIT)


def _dot(a, b, dn=_NN):
    return lax.dot_general(a.astype(_MMT), b.astype(_MMT), dn, preferred_element_type=_F32)


def _pick(n, target, unit=_LANE):
    if n <= target:
        return n
    best = None
    for t in range(unit, target + 1, unit):
        if n % t == 0:
            best = t
    assert best is not None, (n, target)
    return best


def _sigmoid(x):
    return 0.5 * jnp.tanh(0.5 * x) + 0.5


def _silu(x):
    return x * _sigmoid(x)


def _dsilu(x):
    s = _sigmoid(x)
    return s * (1.0 + x * (1.0 - s))


_GK = math.sqrt(2.0 / math.pi)


def _gelu(x):
    return 0.5 * x * (1.0 + jnp.tanh(_GK * (x + 0.044715 * x * x * x)))


def _dgelu(x):
    t = jnp.tanh(_GK * (x + 0.044715 * x * x * x))
    return 0.5 * (1.0 + t) + 0.5 * x * (1.0 - t * t) * _GK * (1.0 + 3.0 * 0.044715 * x * x)


def _softplus(x):
    return jnp.maximum(x, 0.0) + jnp.log(1.0 + jnp.exp(-jnp.abs(x)))


def _rspec(tb, w, cb=0):
    return pl.BlockSpec((tb, w), lambda i: (i, cb))


def _fspec(shape):
    nd = len(shape)
    return pl.BlockSpec(tuple(shape), lambda i: (0,) * nd)


def _lspec(tail, li):
    nd = len(tail)
    return pl.BlockSpec((None,) + tuple(tail), lambda i: (li,) + (0,) * nd)


def _slot_all8(x, y, c):
    return 4 * x + 2 * y + c


def _gather8(v, name):
    R, W = v.shape

    def body(v_ref, o_ref, ssem, rsem, lsem):
        x, y, c = lax.axis_index("x"), lax.axis_index("y"), lax.axis_index("c")
        sib = (x, y, 1 - c)
        chips = _other_chips(x, y)

        def slot(px, py, pc):
            return o_ref.at[_slot_all8(px, py, pc)]

        own = pltpu.make_async_copy(v_ref, slot(x, y, c), lsem)
        own.start()
        started = [_rcopy(v_ref, slot(x, y, c), ssem.at[0], rsem.at[0], sib)]
        started += [_rcopy(v_ref, slot(x, y, c), ssem.at[1 + j], rsem.at[1 + j], (px, py, c))
                    for j, (px, py) in enumerate(chips)]
        for cp in started:
            cp.start()
        for j, (px, py) in enumerate(chips):
            blk = slot(px, py, c)
            _rcopy(blk, blk, ssem.at[1 + j], rsem.at[1 + j], (px, py, c)).wait_recv()
            fw = _rcopy(blk, blk, ssem.at[4 + j], rsem.at[4 + j], sib)
            fw.start()
            started.append(fw)
        blk = slot(x, y, 1 - c)
        _rcopy(blk, blk, ssem.at[0], rsem.at[0], sib).wait_recv()
        for j, (px, py) in enumerate(chips):
            blk = slot(px, py, 1 - c)
            _rcopy(blk, blk, ssem.at[4 + j], rsem.at[4 + j], sib).wait_recv()
        for cp in started:
            cp.wait_send()
        own.wait()

    return pl.pallas_call(
        body, name=name, out_shape=jax.ShapeDtypeStruct((8, R, W), v.dtype), in_specs=[_HBM], out_specs=_HBM,
        scratch_shapes=[pltpu.SemaphoreType.DMA((7,)), pltpu.SemaphoreType.DMA((7,)), pltpu.SemaphoreType.DMA],
    )(v)


def _rcopy(src, dst, ssem, rsem, dev):
    return pltpu.make_async_remote_copy(src_ref=src, dst_ref=dst, send_sem=ssem, recv_sem=rsem,
                                        device_id=dev, device_id_type=_MESH)


def _other_chips(x, y):
    return [(1 - x, y), (x, 1 - y), (1 - x, 1 - y)]


_HBM = pl.BlockSpec(memory_space=pl.ANY)


def _gather_chips(shard, name):
    Rp, W = shard.shape
    Rh = Rp // 2
    rc = Rh // _NCH
    hq = _NCH // 2

    def body(s_ref, o_ref, ssem, rsem, lsem):
        x, y, c = lax.axis_index("x"), lax.axis_index("y"), lax.axis_index("c")
        chip = 2 * x + y
        xn, yn, dg = _other_chips(x, y)
        cx, cy, cd = 2 * xn[0] + xn[1], 2 * yn[0] + yn[1], 2 * dg[0] + dg[1]

        def rows(q):
            return pl.ds(c * Rh + q * rc, rc)

        locs = []
        for q in range(_NCH):
            lc = pltpu.make_async_copy(s_ref.at[rows(q)], o_ref.at[chip, rows(q)], lsem.at[q])
            lc.start()
            locs.append(lc)
        started = []
        for q in range(_NCH):
            for j, nb in ((0, xn), (1, yn)):
                cp = _rcopy(s_ref.at[rows(q)], o_ref.at[chip, rows(q)], ssem.at[j * _NCH + q], rsem.at[j * _NCH + q],
                            (nb[0], nb[1], c))
                cp.start()
                started.append(cp)
        for q in range(_NCH):
            bx = o_ref.at[cx, rows(q)]
            _rcopy(bx, bx, ssem.at[q], rsem.at[q], (xn[0], xn[1], c)).wait_recv()
            if q >= hq:
                rl = _rcopy(bx, bx, ssem.at[2 * _NCH + q], rsem.at[2 * _NCH + q], (yn[0], yn[1], c))
                rl.start()
                started.append(rl)
            by = o_ref.at[cy, rows(q)]
            _rcopy(by, by, ssem.at[_NCH + q], rsem.at[_NCH + q], (yn[0], yn[1], c)).wait_recv()
            if q < hq:
                rl = _rcopy(by, by, ssem.at[2 * _NCH + q], rsem.at[2 * _NCH + q], (xn[0], xn[1], c))
                rl.start()
                started.append(rl)
        for q in range(_NCH):
            bd = o_ref.at[cd, rows(q)]
            _rcopy(bd, bd, ssem.at[2 * _NCH + q], rsem.at[2 * _NCH + q], (dg[0], dg[1], c)).wait_recv()
        for cp in started:
            cp.wait_send()
        for lc in locs:
            lc.wait()

    return pl.pallas_call(
        body, name=name, out_shape=jax.ShapeDtypeStruct((4, Rp, W), shard.dtype), in_specs=[_HBM], out_specs=_HBM,
        scratch_shapes=[pltpu.SemaphoreType.DMA((3 * _NCH,))] * 2 + [pltpu.SemaphoreType.DMA((_NCH,))],
    )(shard)


def _fill_from_sibling(buf, name):
    P, Rp, W = buf.shape
    Rh = Rp // 2
    rc = Rh // _NCH

    def body(s_ref, o_ref, ssem, rsem):
        x, y, c = lax.axis_index("x"), lax.axis_index("y"), lax.axis_index("c")
        cps = []
        for k in range(P):
            for q in range(_NCH):
                r = pl.ds(c * Rh + q * rc, rc)
                cp = _rcopy(s_ref.at[k, r], o_ref.at[k, r], ssem.at[k * _NCH + q], rsem.at[k * _NCH + q],
                            (x, y, 1 - c))
                cp.start()
                cps.append(cp)
        for k in range(P):
            for q in range(_NCH):
                blk = o_ref.at[k, pl.ds((1 - c) * Rh + q * rc, rc)]
                _rcopy(blk, blk, ssem.at[k * _NCH + q], rsem.at[k * _NCH + q], (x, y, 1 - c)).wait_recv()
        for cp in cps:
            cp.wait_send()

    return pl.pallas_call(
        body, name=name, out_shape=jax.ShapeDtypeStruct(buf.shape, buf.dtype), in_specs=[_HBM], out_specs=_HBM,
        scratch_shapes=[pltpu.SemaphoreType.DMA((P * _NCH,))] * 2, input_output_aliases={0: 0},
    )(buf)


def _send_half_to_sibling(send, name):
    P, Rp, W = send.shape
    Rh = Rp // 2
    rc = Rh // _NCH

    def body(s_ref, o_ref, ssem, rsem):
        x, y, c = lax.axis_index("x"), lax.axis_index("y"), lax.axis_index("c")
        cps = []
        for k in range(P):
            for q in range(_NCH):
                cp = _rcopy(s_ref.at[k, pl.ds((1 - c) * Rh + q * rc, rc)], o_ref.at[k, pl.ds(q * rc, rc)],
                            ssem.at[k * _NCH + q], rsem.at[k * _NCH + q], (x, y, 1 - c))
                cp.start()
                cps.append(cp)
        for cp in cps:
            cp.wait()

    return pl.pallas_call(
        body, name=name, out_shape=jax.ShapeDtypeStruct((P, Rh, W), send.dtype), in_specs=[_HBM], out_specs=_HBM,
        scratch_shapes=[pltpu.SemaphoreType.DMA((P * _NCH,))] * 2,
    )(send)


def _scatter_to_chips(cs, name):
    P, Rh, W = cs.shape
    rc = Rh // _NCH

    def body(s_ref, o_ref, ssem, rsem, lsem):
        x, y, c = lax.axis_index("x"), lax.axis_index("y"), lax.axis_index("c")
        chip = 2 * x + y
        peers = _other_chips(x, y)
        locs = []
        for q in range(_NCH):
            r = pl.ds(q * rc, rc)
            lc = pltpu.make_async_copy(s_ref.at[chip, r], o_ref.at[chip, r], lsem.at[q])
            lc.start()
            locs.append(lc)
        cps = []
        for j, (px, py) in enumerate(peers):
            for q in range(_NCH):
                r = pl.ds(q * rc, rc)
                cp = _rcopy(s_ref.at[2 * px + py, r], o_ref.at[chip, r], ssem.at[j * _NCH + q], rsem.at[j * _NCH + q],
                            (px, py, c))
                cp.start()
                cps.append(cp)
        for j, (px, py) in enumerate(peers):
            for q in range(_NCH):
                blk = o_ref.at[2 * px + py, pl.ds(q * rc, rc)]
                _rcopy(blk, blk, ssem.at[j * _NCH + q], rsem.at[j * _NCH + q], (px, py, c)).wait_recv()
        for cp in cps:
            cp.wait_send()
        for lc in locs:
            lc.wait()

    return pl.pallas_call(
        body, name=name, out_shape=jax.ShapeDtypeStruct((P, Rh, W), cs.dtype), in_specs=[_HBM], out_specs=_HBM,
        scratch_shapes=[pltpu.SemaphoreType.DMA((3 * _NCH,))] * 2 + [pltpu.SemaphoreType.DMA((_NCH,))],
    )(cs)


def _swap_with_sibling(v, name):
    R, W = v.shape
    rc = R // _NCH

    def body(s_ref, o_ref, ssem, rsem):
        x, y, c = lax.axis_index("x"), lax.axis_index("y"), lax.axis_index("c")
        cps = []
        for q in range(_NCH):
            r = pl.ds(q * rc, rc)
            cp = _rcopy(s_ref.at[r], o_ref.at[r], ssem.at[q], rsem.at[q], (x, y, 1 - c))
            cp.start()
            cps.append(cp)
        for cp in cps:
            cp.wait()

    return pl.pallas_call(
        body, name=name, out_shape=jax.ShapeDtypeStruct((R, W), v.dtype), in_specs=[_HBM], out_specs=_HBM,
        scratch_shapes=[pltpu.SemaphoreType.DMA((_NCH,))] * 2,
    )(v)


def _mm_nt(a, b, name, out_dtype=_F32, tm=1024, tn=1024):
    M, K = a.shape
    N = b.shape[0]
    tm, tn = _pick(M, tm, 8), _pick(N, tn)

    def body(a_ref, b_ref, o_ref):
        o_ref[...] = lax.dot_general(a_ref[...], b_ref[...], _NT, preferred_element_type=_F32).astype(o_ref.dtype)

    return pl.pallas_call(
        body, name=name, grid=(M // tm, N // tn),
        in_specs=[pl.BlockSpec((tm, K), lambda i, j: (i, 0)), pl.BlockSpec((tn, K), lambda i, j: (j, 0))],
        out_specs=pl.BlockSpec((tm, tn), lambda i, j: (i, j)),
        out_shape=jax.ShapeDtypeStruct((M, N), out_dtype),
        compiler_params=_cp("parallel", "parallel"),
    )(a, b)


def _mm_tn(a, b, name, li, nl, into=None, row0=0, mtot=None, tm=512, tn=1024):
    T, M = a.shape
    N = b.shape[1]
    tm, tn = _pick(M, tm), _pick(N, tn)
    mtot = M if mtot is None else mtot
    assert row0 % tm == 0
    r0 = row0 // tm

    def body(a_ref, b_ref, *rest):
        o_ref = rest[-1]
        o_ref[...] = lax.dot_general(a_ref[...], b_ref[...], _TN, preferred_element_type=_F32).astype(o_ref.dtype)

    ins = [pl.BlockSpec((T, tm), lambda i, j: (0, i)), pl.BlockSpec((T, tn), lambda i, j: (0, j))]
    return pl.pallas_call(
        body, name=name, grid=(M // tm, N // tn),
        in_specs=ins if into is None else ins + [_HBM],
        out_specs=pl.BlockSpec((None, tm, tn), lambda i, j: (li, i + r0, j)),
        out_shape=jax.ShapeDtypeStruct((nl, mtot, N), _MMT),
        input_output_aliases={} if into is None else {2: 0},
        compiler_params=_cp("parallel", "parallel"),
    )(*((a, b) if into is None else (a, b, into)))


def _mm_res(a, b, x, gt, name, tm=1024, tn=1024, tk=2048):
    M, K = a.shape
    N = b.shape[1]
    tm, tn, tk = _pick(M, tm, 8), _pick(N, tn), _pick(K, tk)
    nk = K // tk

    def body(a_ref, b_ref, x_ref, gt_ref, p_ref, o_ref):
        k = pl.program_id(2)
        part = lax.dot_general(a_ref[...], b_ref[...], _NN, preferred_element_type=_F32)

        @pl.when(k == 0)
        def _():
            p_ref[...] = part

        @pl.when(k > 0)
        def _():
            p_ref[...] += part

        @pl.when(k == nk - 1)
        def _():
            o_ref[...] = x_ref[...] + gt_ref[...] * p_ref[...]

    tile = pl.BlockSpec((tm, tn), lambda i, j, k: (i, j))
    return pl.pallas_call(
        body, name=name, grid=(M // tm, N // tn, nk),
        in_specs=[pl.BlockSpec((tm, tk), lambda i, j, k: (i, k)), pl.BlockSpec((tk, tn), lambda i, j, k: (k, j)),
                  tile, pl.BlockSpec((1, tn), lambda i, j, k: (0, j))],
        out_specs=[tile, tile], out_shape=[jax.ShapeDtypeStruct((M, N), _F32)] * 2,
        compiler_params=_cp("parallel", "parallel", "arbitrary"),
    )(a, b, x, gt)


def _mm_normb(a, b, x, dres, g, sc, name, tm=1024, tk=2048):
    segs = tuple(a) if isinstance(a, (tuple, list)) else (a,)
    ns = len(segs)
    M, K1 = segs[0].shape
    D = b.shape[1]
    tm, tk = _pick(M, tm, 8), _pick(K1, tk)
    nh = K1 // tk
    nk = ns * nh

    def body(*refs):
        a_refs = refs[:ns]
        b_ref, x_ref, dr_ref, g_ref, sc_ref, dx_ref, dgm_ref, dsh_ref, acc = refs[ns:]
        i, k = pl.program_id(0), pl.program_id(1)

        @pl.when(k == 0)
        def _():
            acc[...] = jnp.zeros_like(acc)

        for s in range(ns):
            @pl.when(jnp.logical_and(k >= s * nh, k < (s + 1) * nh))
            def _():
                acc[...] += lax.dot_general(a_refs[s][...], b_ref[...], _NN, preferred_element_type=_F32)

        @pl.when(jnp.logical_and(i == 0, k == 0))
        def _():
            dgm_ref[...] = jnp.zeros_like(dgm_ref)
            dsh_ref[...] = jnp.zeros_like(dsh_ref)

        @pl.when(k == nk - 1)
        def _():
            dh_, xv = acc[...], x_ref[...]
            r = lax.rsqrt(jnp.mean(xv * xv, axis=1, keepdims=True) + _EPS)
            xn = xv * r
            dxn = dh_ * (g_ref[...] * (1.0 + sc_ref[...]))
            dx_ref[...] = dr_ref[...] + r * (dxn - xn * jnp.mean(dxn * xn, axis=1, keepdims=True))
            dgm_ref[...] += jnp.sum(dh_ * xn, axis=0, keepdims=True)
            dsh_ref[...] += jnp.sum(dh_, axis=0, keepdims=True)

    row = pl.BlockSpec((tm, D), lambda i, k: (i, 0))
    vec = pl.BlockSpec((1, D), lambda i, k: (0, 0))
    a_specs = [pl.BlockSpec((tm, tk), lambda i, k, s=s: (i, jnp.clip(k - s * nh, 0, nh - 1))) for s in range(ns)]
    return pl.pallas_call(
        body, name=name, grid=(M // tm, nk),
        in_specs=a_specs + [pl.BlockSpec((tk, D), lambda i, k: (k, 0)), row, row, vec, vec],
        out_specs=[row, vec, vec],
        out_shape=[jax.ShapeDtypeStruct((M, D), _F32), jax.ShapeDtypeStruct((1, D), _F32),
                   jax.ShapeDtypeStruct((1, D), _F32)],
        scratch_shapes=[pltpu.VMEM((tm, D), _F32)],
        compiler_params=_cp("arbitrary", "arbitrary"),
    )(*segs, b, x, dres, g, sc)


def _ffoutb_swiglu(dp, w, gate, up, name, tm=1024, tn=1408):
    M, K = dp.shape
    F = w.shape[0]
    tm, tn = _pick(M, tm, 8), _pick(F, tn)

    def body(a_ref, b_ref, g_ref, u_ref, dg_ref, du_ref):
        d = lax.dot_general(a_ref[...], b_ref[...], _NT, preferred_element_type=_F32)
        gv = g_ref[...].astype(_F32)
        dg_ref[...] = (d * u_ref[...].astype(_F32) * _dsilu(gv)).astype(dg_ref.dtype)
        du_ref[...] = (d * _silu(gv)).astype(du_ref.dtype)

    tile = pl.BlockSpec((tm, tn), lambda i, j: (i, j))
    return pl.pallas_call(
        body, name=name, grid=(M // tm, F // tn),
        in_specs=[pl.BlockSpec((tm, K), lambda i, j: (i, 0)), pl.BlockSpec((tn, K), lambda i, j: (j, 0)), tile, tile],
        out_specs=[tile, tile], out_shape=[jax.ShapeDtypeStruct((M, F), _MMT)] * 2,
        compiler_params=_cp("parallel", "parallel"),
    )(dp, w, gate, up)


def _ffin_swiglu(a, wt, name, tm=1024, tn=1408):
    M, K = a.shape
    F = wt.shape[0] // 2
    tm, tn = _pick(M, tm, 8), _pick(F, tn)
    nj = F // tn

    def body(a_ref, bg_ref, bu_ref, g_ref, u_ref, act_ref):
        av = a_ref[...]
        g = lax.dot_general(av, bg_ref[...], _NT, preferred_element_type=_F32)
        u = lax.dot_general(av, bu_ref[...], _NT, preferred_element_type=_F32)
        g_ref[...] = g.astype(g_ref.dtype)
        u_ref[...] = u.astype(u_ref.dtype)
        act_ref[...] = (_silu(g) * u).astype(act_ref.dtype)

    tile = pl.BlockSpec((tm, tn), lambda i, j: (i, j))
    return pl.pallas_call(
        body, name=name, grid=(M // tm, nj),
        in_specs=[pl.BlockSpec((tm, K), lambda i, j: (i, 0)), pl.BlockSpec((tn, K), lambda i, j: (j, 0)),
                  pl.BlockSpec((tn, K), lambda i, j: (j + nj, 0))],
        out_specs=[tile] * 3, out_shape=[jax.ShapeDtypeStruct((M, F), _MMT)] * 3,
        compiler_params=_cp("parallel", "parallel"),
    )(a, wt, wt)


def _branch_merge(ya, yb, w_a, w_b, proj, name, tm=1024, tn=1024):
    M, K = ya.shape
    N = w_a.shape[1]
    tm, tn = _pick(M, tm, 8), _pick(N, tn)
    nj = N // tn

    def body(ya_ref, yb_ref, wa_ref, wb_ref, ga_ref, gb_ref, pa_ref, pb_ref, m_ref):
        pa = lax.dot_general(ya_ref[...], wa_ref[...], _NN, preferred_element_type=_F32)
        pb = lax.dot_general(yb_ref[...], wb_ref[...], _NN, preferred_element_type=_F32)
        pa_ref[...] = pa.astype(pa_ref.dtype)
        pb_ref[...] = pb.astype(pb_ref.dtype)
        m_ref[...] = (_sigmoid(ga_ref[...]) * pa + _sigmoid(gb_ref[...]) * pb).astype(m_ref.dtype)

    row = pl.BlockSpec((tm, K), lambda i, j: (i, 0))
    col = pl.BlockSpec((K, tn), lambda i, j: (0, j))
    tile = pl.BlockSpec((tm, tn), lambda i, j: (i, j))
    return pl.pallas_call(
        body, name=name, grid=(M // tm, nj),
        in_specs=[row, row, col, col, pl.BlockSpec((tm, tn), lambda i, j: (i, 6 * nj + j)),
                  pl.BlockSpec((tm, tn), lambda i, j: (i, 7 * nj + j))],
        out_specs=[tile] * 3, out_shape=[jax.ShapeDtypeStruct((M, N), _MMT)] * 3,
        compiler_params=_cp("parallel", "parallel"),
    )(ya, yb, w_a, w_b, proj, proj)


def _ada_fwd(c_all, ada_w, name):
    L, D, Ws = ada_w.shape
    B = c_all.shape[0]

    def body(c_ref, w_ref, o_ref):
        o_ref[...] = _dot(_silu(c_ref[...]), w_ref[...])

    return pl.pallas_call(
        body, name=name, grid=(L,),
        in_specs=[_fspec((B, D)), pl.BlockSpec((None, D, Ws), lambda l: (l, 0, 0))],
        out_specs=pl.BlockSpec((None, B, Ws), lambda l: (l, 0, 0)),
        out_shape=jax.ShapeDtypeStruct((L, B, Ws), _F32), compiler_params=_cp("parallel"),
    )(c_all, ada_w)


def _ada_bwd(c_all_t, dmod, name):
    D, B = c_all_t.shape
    L, _, Ws = dmod.shape

    def body(c_ref, d_ref, o_ref):
        ct = _silu(c_ref[...])
        d = d_ref[...]
        acc = ct[:, 0:1] * d[0:1, :]
        for b in range(1, B):
            acc = acc + ct[:, b:b + 1] * d[b:b + 1, :]
        o_ref[...] = acc

    return pl.pallas_call(
        body, name=name, grid=(L,),
        in_specs=[_fspec((D, B)), pl.BlockSpec((None, B, Ws), lambda l: (l, 0, 0))],
        out_specs=pl.BlockSpec((None, D, Ws), lambda l: (l, 0, 0)),
        out_shape=jax.ShapeDtypeStruct((L, D, Ws), _F32), compiler_params=_cp("parallel"),
    )(c_all_t, dmod)


def _norm_mod(x, g, sc, sh, name, tb=512):
    T, D = x.shape
    tb = _pick(T, tb, 8)

    def body(x_ref, g_ref, sc_ref, sh_ref, h_ref):
        xv = x_ref[...]
        r = lax.rsqrt(jnp.mean(xv * xv, axis=1, keepdims=True) + _EPS)
        h_ref[...] = (xv * r * (g_ref[...] * (1.0 + sc_ref[...])) + sh_ref[...]).astype(h_ref.dtype)

    return pl.pallas_call(
        body, name=name, grid=(T // tb,),
        in_specs=[_rspec(tb, D), _fspec((1, D)), _fspec((1, D)), _fspec((1, D))],
        out_specs=_rspec(tb, D), out_shape=jax.ShapeDtypeStruct((T, D), _MMT), compiler_params=_cp("parallel"),
    )(x, g, sc, sh)


def _resid_bwd(dx, p, gt, name, tb=512):
    T, D = dx.shape
    tb = _pick(T, tb, 8)

    def body(dx_ref, p_ref, gt_ref, dp_ref, dgt_ref):
        i = pl.program_id(0)
        d = dx_ref[...]
        dp_ref[...] = (d * gt_ref[...]).astype(dp_ref.dtype)

        @pl.when(i == 0)
        def _():
            dgt_ref[...] = jnp.zeros_like(dgt_ref)

        dgt_ref[...] += jnp.sum(d * p_ref[...], axis=0, keepdims=True)

    return pl.pallas_call(
        body, name=name, grid=(T // tb,), in_specs=[_rspec(tb, D), _rspec(tb, D), _fspec((1, D))],
        out_specs=[_rspec(tb, D), _fspec((1, D))],
        out_shape=[jax.ShapeDtypeStruct((T, D), _MMT), jax.ShapeDtypeStruct((1, D), _F32)],
        compiler_params=_cp("arbitrary"),
    )(dx, p, gt)


def _gmlp_chunk(u_raw, v_raw, sw_ref, sbt, gv, G):
    u, v = _gelu(u_raw), _gelu(v_raw)
    ii = lax.broadcasted_iota(jnp.int32, (_AC, _AC), 0)
    jj = lax.broadcasted_iota(jnp.int32, (_AC, _AC), 1)
    out = []
    for gi in range(G):
        sl = slice(gi * _GD, (gi + 1) * _GD)
        vg = v[:, sl]
        r = lax.rsqrt(jnp.mean(vg * vg, axis=1, keepdims=True) + _EPS)
        vhat = vg * r
        W = jnp.where(jj <= ii, sw_ref[gi], 0.0)
        s = _dot(W, vhat * gv[:, sl]) + sbt[:, gi:gi + 1]
        out.append((u[:, sl], s, vhat, r, W))
    return out


def _gmlp_fwd(proj, sw, sbt, gv, li, D, name):
    T = proj.shape[0]
    G = D // _GD

    def body(u_ref, v_ref, sw_ref, sbt_ref, gv_ref, y_ref):
        parts = _gmlp_chunk(u_ref[...], v_ref[...], sw_ref, sbt_ref[...], gv_ref[...], G)
        for gi, (u, s, _, _, _) in enumerate(parts):
            y_ref[:, gi * _GD:(gi + 1) * _GD] = (u * s).astype(y_ref.dtype)

    return pl.pallas_call(
        body, name=name, grid=(T // _AC,),
        in_specs=[_rspec(_AC, D, 0), _rspec(_AC, D, 1), _lspec((G, _AC, _AC), li), _lspec((_AC, G), li),
                  _lspec((1, D), li)],
        out_specs=_rspec(_AC, D), out_shape=jax.ShapeDtypeStruct((T, D), _MMT), compiler_params=_cp("parallel"),
    )(proj, proj, sw, sbt, gv)


def _gmlp_bwd(proj, dy, sw, sbt, gv, li, D, into, name):
    T = proj.shape[0]
    G = D // _GD

    def body(u_ref, v_ref, dy_ref, sw_ref, sbt_ref, gv_ref, _, duv_ref, dsw_ref, dsa_ref, dgv_ref):
        i = pl.program_id(0)

        @pl.when(i == 0)
        def _():
            dsw_ref[...] = jnp.zeros_like(dsw_ref)
            dsa_ref[...] = jnp.zeros_like(dsa_ref)
            dgv_ref[...] = jnp.zeros_like(dgv_ref)

        u_raw, v_raw, dy_, gv_ = u_ref[...], v_ref[...], dy_ref[...].astype(_F32), gv_ref[...]
        parts = _gmlp_chunk(u_raw, v_raw, sw_ref, sbt_ref[...], gv_, G)
        ii = lax.broadcasted_iota(jnp.int32, (_AC, _AC), 0)
        jj = lax.broadcasted_iota(jnp.int32, (_AC, _AC), 1)
        dgu, dgv = _dgelu(u_raw), _dgelu(v_raw)
        for gi, (u, s, vhat, r, W) in enumerate(parts):
            sl = slice(gi * _GD, (gi + 1) * _GD)
            dyg = dy_[:, sl]
            ds = dyg * u
            vn = vhat * gv_[:, sl]
            dsw_ref[gi] += jnp.where(jj <= ii, _dot(ds, vn, _NT), 0.0)
            dsa_ref[:, sl] += ds
            dvn = _dot(W, ds, _TN)
            dgv_ref[:, sl] += jnp.sum(dvn * vhat, axis=0, keepdims=True)
            dvh = dvn * gv_[:, sl]
            dvg = r * (dvh - vhat * jnp.mean(dvh * vhat, axis=1, keepdims=True))
            duv_ref[:, sl] = (dyg * s * dgu[:, sl]).astype(duv_ref.dtype)
            duv_ref[:, D + gi * _GD:D + (gi + 1) * _GD] = (dvg * dgv[:, sl]).astype(duv_ref.dtype)

    return pl.pallas_call(
        body, name=name, grid=(T // _AC,),
        in_specs=[_rspec(_AC, D, 0), _rspec(_AC, D, 1), _rspec(_AC, D), _lspec((G, _AC, _AC), li),
                  _lspec((_AC, G), li), _lspec((1, D), li), _HBM],
        out_specs=[_rspec(_AC, 2 * D), _fspec((G, _AC, _AC)), _fspec((_AC, D)), _fspec((1, D))],
        out_shape=[jax.ShapeDtypeStruct(into.shape, into.dtype), jax.ShapeDtypeStruct((G, _AC, _AC), _F32),
                   jax.ShapeDtypeStruct((_AC, D), _F32), jax.ShapeDtypeStruct((1, D), _F32)],
        input_output_aliases={6: 0}, compiler_params=_cp("arbitrary"),
    )(proj, proj, dy, sw, sbt, gv, into)


def _conv_taps(halo, cur, first):
    tb = cur.shape[0]
    full = jnp.concatenate([jnp.where(first, 0.0, halo), cur], axis=0)
    return [full[8:] if j == _KC - 1 else pltpu.roll(full, _KC - 1 - j, 0)[8:] for j in range(_KC)]


def _prev_spec(tb, w, cb):
    return pl.BlockSpec((8, w), lambda i: (jnp.maximum(i * (tb // 8) - 1, 0), cb))


def _l2_heads(x, H):
    outs, rs = [], []
    for h in range(H):
        xh = x[:, h * _GD:(h + 1) * _GD]
        r = lax.rsqrt(jnp.sum(xh * xh, axis=1, keepdims=True) + _EPS)
        outs.append(xh * r)
        rs.append(r)
    return outs, rs


def _gate_rows(ba, alog_row, dtb_row, H):
    lane = lax.broadcasted_iota(jnp.int32, ba.shape, 1)
    beta = _sigmoid(ba)
    g = -jnp.exp(alog_row) * _softplus(ba + dtb_row)
    return lane, beta, g


def _conv_fwd(proj, cw, alog_row, dtb_row, li, D, name, tb=256):
    T = proj.shape[0]
    H = D // _GD
    tb = _pick(T, tb, 8)
    bac = (8 * D) // _LANE

    def body(q_ref, k_ref, v_ref, qh_ref, kh_ref, vh_ref, ba_ref, cw_ref, al_ref, dtb_ref,
             qo_ref, ko_ref, vo_ref, bg_ref):
        first = pl.program_id(0) == 0
        cw_ = cw_ref[...]
        for idx, (cur, halo, out) in enumerate(((q_ref, qh_ref, qo_ref), (k_ref, kh_ref, ko_ref),
                                                 (v_ref, vh_ref, vo_ref))):
            taps = _conv_taps(halo[...], cur[...], first)
            w = cw_[:, idx * D:(idx + 1) * D]
            cv = taps[0] * w[0:1, :]
            for j in range(1, _KC):
                cv = cv + taps[j] * w[j:j + 1, :]
            act = _silu(cv)
            if idx < 2:
                outs, _ = _l2_heads(act, H)
                for h in range(H):
                    out[:, h * _GD:(h + 1) * _GD] = outs[h]
            else:
                out[...] = act
        lane, beta, g = _gate_rows(ba_ref[...], al_ref[...], dtb_ref[...], H)
        bg_ref[...] = jnp.where(lane < H, beta, jnp.where(lane < 2 * H, g, 0.0))

    return pl.pallas_call(
        body, name=name, grid=(T // tb,),
        in_specs=[_rspec(tb, D, 2), _rspec(tb, D, 3), _rspec(tb, D, 4),
                  _prev_spec(tb, D, 2), _prev_spec(tb, D, 3), _prev_spec(tb, D, 4),
                  _rspec(tb, _LANE, bac), _lspec((_KC, 3 * D), li), _lspec((1, _LANE), li), _lspec((1, _LANE), li)],
        out_specs=[_rspec(tb, D), _rspec(tb, D), _rspec(tb, D), _rspec(tb, _LANE)],
        out_shape=[jax.ShapeDtypeStruct((T, D), _F32)] * 3 + [jax.ShapeDtypeStruct((T, _LANE), _F32)],
        compiler_params=_cp("parallel"),
    )(proj, proj, proj, proj, proj, proj, proj, cw, alog_row, dtb_row)


def _conv_bwd1(proj, dqn, dkn, dvs, dbg, cw, alog_row, dtb_row, li, D, into, name, tb=256):
    T = proj.shape[0]
    H = D // _GD
    tb = _pick(T, tb, 8)
    bac = (8 * D) // _LANE

    def body(q_ref, k_ref, v_ref, qh_ref, kh_ref, vh_ref, ba_ref, dq_ref, dk_ref, dv_ref, dbg_ref,
             cw_ref, al_ref, dtb_ref, _, dc_ref, dba_ref, dcw_ref, dal_ref, ddt_ref):
        i = pl.program_id(0)
        first = i == 0

        @pl.when(first)
        def _():
            dcw_ref[...] = jnp.zeros_like(dcw_ref)
            dal_ref[...] = jnp.zeros_like(dal_ref)
            ddt_ref[...] = jnp.zeros_like(ddt_ref)

        cw_ = cw_ref[...]
        for idx, (cur, halo, dref) in enumerate(((q_ref, qh_ref, dq_ref), (k_ref, kh_ref, dk_ref),
                                                  (v_ref, vh_ref, dv_ref))):
            taps = _conv_taps(halo[...], cur[...], first)
            w = cw_[:, idx * D:(idx + 1) * D]
            cv = taps[0] * w[0:1, :]
            for j in range(1, _KC):
                cv = cv + taps[j] * w[j:j + 1, :]
            dact = dref[...]
            if idx < 2:
                outs, rs = _l2_heads(_silu(cv), H)
                pieces = []
                for h in range(H):
                    dy = dact[:, h * _GD:(h + 1) * _GD]
                    pieces.append(rs[h] * (dy - outs[h] * jnp.sum(dy * outs[h], axis=1, keepdims=True)))
                dact = jnp.concatenate(pieces, axis=1)
            dcv = dact * _dsilu(cv)
            dc_ref[:, idx * D:(idx + 1) * D] = dcv
            for j in range(_KC):
                colsum = _dot(jnp.ones((8, tb), _F32), dcv * taps[j])
                dcw_ref[j:j + 1, idx * D:(idx + 1) * D] += colsum[0:1, :]

        ba = ba_ref[...]
        lane, beta, g = _gate_rows(ba, al_ref[...], dtb_ref[...], H)
        dbg_ = dbg_ref[...]
        is_b, is_a = lane < H, jnp.logical_and(lane >= H, lane < 2 * H)
        da = dbg_ * (-jnp.exp(al_ref[...])) * _sigmoid(ba + dtb_ref[...])
        dba_ref[...] = jnp.where(is_b, dbg_ * beta * (1.0 - beta), jnp.where(is_a, da, 0.0)).astype(dba_ref.dtype)
        dal_ref[...] += jnp.sum(jnp.where(is_a, dbg_ * g, 0.0), axis=0, keepdims=True)
        ddt_ref[...] += jnp.sum(jnp.where(is_a, da, 0.0), axis=0, keepdims=True)

    return pl.pallas_call(
        body, name=name, grid=(T // tb,),
        in_specs=[_rspec(tb, D, 2), _rspec(tb, D, 3), _rspec(tb, D, 4),
                  _prev_spec(tb, D, 2), _prev_spec(tb, D, 3), _prev_spec(tb, D, 4),
                  _rspec(tb, _LANE, bac), _rspec(tb, D), _rspec(tb, D), _rspec(tb, D), _rspec(tb, _LANE),
                  _lspec((_KC, 3 * D), li), _lspec((1, _LANE), li), _lspec((1, _LANE), li), _HBM],
        out_specs=[_rspec(tb, 3 * D), _rspec(tb, _LANE, bac), _fspec((_KC, 3 * D)), _fspec((1, _LANE)),
                   _fspec((1, _LANE))],
        out_shape=[jax.ShapeDtypeStruct((T, 3 * D), _F32), jax.ShapeDtypeStruct(into.shape, into.dtype),
                   jax.ShapeDtypeStruct((_KC, 3 * D), _F32), jax.ShapeDtypeStruct((1, _LANE), _F32),
                   jax.ShapeDtypeStruct((1, _LANE), _F32)],
        input_output_aliases={14: 1}, compiler_params=_cp("arbitrary"),
    )(proj, proj, proj, proj, proj, proj, proj, dqn, dkn, dvs, dbg, cw, alog_row, dtb_row, into)


def _conv_bwd2(dc, cw, li, into, name, tb=256):
    T, W3 = dc.shape
    W = W3 // 3
    tb = _pick(T, tb, 8)
    nb8 = T // 8
    nrow = T // tb

    def body(dc_ref, nx_ref, cw_ref, _, o_ref):
        last = pl.program_id(0) == nrow - 1
        full = jnp.concatenate([dc_ref[...], jnp.where(last, 0.0, nx_ref[...])], axis=0)
        w = cw_ref[...]
        acc = full[:tb] * w[_KC - 1:_KC, :]
        for j in range(_KC - 1):
            sh = _KC - 1 - j
            acc = acc + pltpu.roll(full, tb + 8 - sh, 0)[:tb] * w[j:j + 1, :]
        o_ref[...] = acc.astype(o_ref.dtype)

    return pl.pallas_call(
        body, name=name, grid=(nrow, 3),
        in_specs=[pl.BlockSpec((tb, W), lambda i, j: (i, j)),
                  pl.BlockSpec((8, W), lambda i, j: (jnp.minimum((i + 1) * (tb // 8), nb8 - 1), j)),
                  pl.BlockSpec((None, _KC, W), lambda i, j: (li, 0, j)), _HBM],
        out_specs=pl.BlockSpec((tb, W), lambda i, j: (i, 2 + j)),
        out_shape=jax.ShapeDtypeStruct(into.shape, into.dtype), input_output_aliases={3: 0},
        compiler_params=_cp("parallel", "parallel"),
    )(dc, dc, cw, into)


def _split(a):
    hi = a.astype(_BF)
    return hi, (a - hi.astype(_F32)).astype(_BF)


def _dot3(a, b):
    (ah, al), (bh, bl) = a, b
    f = functools.partial(lax.dot_general, dimension_numbers=_NN, preferred_element_type=_F32)
    return f(ah, bh) + f(ah, bl) + f(al, bh)


def _inv_unit_lower(mats):
    C = mats[0].shape[0]
    ii = lax.broadcasted_iota(jnp.int32, (C, C), 0)
    jj = lax.broadcasted_iota(jnp.int32, (C, C), 1)
    xs = [jnp.where(ii == jj, 1.0, 0.0) - a for a in mats]
    ps = list(mats)
    n = 1
    while 2 * n < C:
        sp = [_split(p) for p in ps]
        ps = [_dot3(s, s) for s in sp]
        sp = [_split(p) for p in ps]
        xs = [x + _dot3(_split(x), s) for x, s in zip(xs, sp)]
        n *= 2
    return xs


def _gdn_chunk(q, k, v, g_row, b_row):
    C = q.shape[0]
    ii = lax.broadcasted_iota(jnp.int32, (C, C), 0)
    jj = lax.broadcasted_iota(jnp.int32, (C, C), 1)
    low, strict, eye = jj <= ii, jj < ii, ii == jj
    g_col = jnp.sum(jnp.where(eye, g_row, 0.0), axis=1, keepdims=True)
    b_col = jnp.sum(jnp.where(eye, b_row, 0.0), axis=1, keepdims=True)
    gam_col = jnp.sum(jnp.where(low, g_row, 0.0), axis=1, keepdims=True)
    gam_row = jnp.sum(jnp.where(jj >= ii, g_col, 0.0), axis=0, keepdims=True)
    gam_last = jnp.sum(g_row, axis=1, keepdims=True)
    decay = jnp.where(low, jnp.exp(jnp.where(low, gam_col - gam_row, 0.0)), 0.0)
    eg = jnp.exp(gam_col)
    ekd = jnp.exp(gam_last - gam_col)
    qs = q * (_GD ** -0.5)
    kb = k * b_col
    kk = _dot(kb, k, _NT)
    qkraw = _dot(qs, k, _NT)
    return dict(low=low, strict=strict, eye=eye, ii=ii, jj=jj, b_col=b_col, decay=decay, eg=eg, ekd=ekd,
                gl=jnp.exp(gam_last), qs=qs, kb=kb, kk=kk, qkraw=qkraw,
                A=jnp.where(strict, kk * decay, 0.0), vb=v * b_col, kbg=kb * eg,
                qk=qkraw * decay, q_dec=qs * eg, k_dec=k * ekd)


def _gdn_fwd(qn, kn, vs, g_r, b_r, name):
    T, D = qn.shape
    H, N, C = D // _GD, T // _BC, _BC
    hb = min(_HB, H)

    def body(q_ref, k_ref, v_ref, g_ref, b_ref, o_ref, s_ref, t_ref, S):
        @pl.when(pl.program_id(1) == 0)
        def _():
            S[...] = jnp.zeros_like(S)

        hs = range(hb)
        sls = [slice(hh * _GD, (hh + 1) * _GD) for hh in hs]
        cms = [_gdn_chunk(q_ref[:, sl], k_ref[:, sl], v_ref[:, sl], g_ref[hh], b_ref[hh]) for hh, sl in zip(hs, sls)]
        tms = _inv_unit_lower([cm["A"] for cm in cms])
        us = [_dot(tm, cm["vb"]) for tm, cm in zip(tms, cms)]
        ws = [_dot(tm, cm["kbg"]) for tm, cm in zip(tms, cms)]
        s0s = [S[hh] for hh in hs]
        for hh in hs:
            s_ref[hh] = s0s[hh]
            t_ref[hh] = tms[hh]
        v_news = [u - _dot(w, s0) for u, w, s0 in zip(us, ws, s0s)]
        qss = [_dot(cm["q_dec"], s0) for cm, s0 in zip(cms, s0s)]
        for hh in hs:
            o_ref[:, sls[hh]] = qss[hh] + _dot(cms[hh]["qk"], v_news[hh])
        for hh in hs:
            S[hh] = s0s[hh] * cms[hh]["gl"] + _dot(cms[hh]["k_dec"], v_news[hh], _TN)

    qspec = pl.BlockSpec((C, hb * _GD), lambda h, n: (n, h))
    gspec = pl.BlockSpec((hb, None, 1, C), lambda h, n: (h, n, 0, 0))
    return pl.pallas_call(
        body, name=name, grid=(H // hb, N),
        in_specs=[qspec, qspec, qspec, gspec, gspec],
        out_specs=[qspec, pl.BlockSpec((hb, None, _GD, _GD), lambda h, n: (h, n, 0, 0)),
                   pl.BlockSpec((hb, None, C, C), lambda h, n: (h, n, 0, 0))],
        out_shape=[jax.ShapeDtypeStruct((T, D), _F32), jax.ShapeDtypeStruct((H, N, _GD, _GD), _F32),
                   jax.ShapeDtypeStruct((H, N, C, C), _F32)],
        scratch_shapes=[pltpu.VMEM((hb, _GD, _GD), _F32)],
        compiler_params=_cp("arbitrary", "arbitrary"),
    )(qn, kn, vs, g_r, b_r)


def _gdn_bwd(qn, kn, vs, g_r, b_r, s_all, t_all, do, name):
    T, D = qn.shape
    H, N, C = D // _GD, T // _BC, _BC
    hb = min(_HB, H)

    def body(q_ref, k_ref, v_ref, g_ref, b_ref, s_ref, t_ref, do_ref, dq_ref, dk_ref, dv_ref, dg_ref, db_ref, dS):
        @pl.when(pl.program_id(1) == 0)
        def _():
            dS[...] = jnp.zeros_like(dS)

        hs = range(hb)
        sls = [slice(hh * _GD, (hh + 1) * _GD) for hh in hs]
        ks = [k_ref[:, sl] for sl in sls]
        vs_ = [v_ref[:, sl] for sl in sls]
        cms = [_gdn_chunk(q_ref[:, sl], k, v, g_ref[hh], b_ref[hh]) for hh, sl, k, v in zip(hs, sls, ks, vs_)]
        low, strict, eye, ii, jj = (cms[0][n] for n in ("low", "strict", "eye", "ii", "jj"))
        tms, s0s, dos, ds1s = [t_ref[hh] for hh in hs], [s_ref[hh] for hh in hs], [do_ref[:, sl] for sl in sls], \
            [dS[hh] for hh in hs]
        us = [_dot(tm, cm["vb"]) for tm, cm in zip(tms, cms)]
        ws = [_dot(tm, cm["kbg"]) for tm, cm in zip(tms, cms)]
        v_news = [u - _dot(w, s0) for u, w, s0 in zip(us, ws, s0s)]
        dv_news = [_dot(cm["qk"], do_, _TN) + _dot(cm["k_dec"], ds1) for cm, do_, ds1 in zip(cms, dos, ds1s)]
        dqks = [jnp.where(low, _dot(do_, vn, _NT), 0.0) for do_, vn in zip(dos, v_news)]
        dq_decs = [_dot(do_, s0, _NT) for do_, s0 in zip(dos, s0s)]
        dk_decs = [_dot(vn, ds1, _NT) for vn, ds1 in zip(v_news, ds1s)]
        dgls = [jnp.sum(jnp.sum(ds1 * s0, axis=1, keepdims=True), axis=0, keepdims=True) for ds1, s0 in zip(ds1s, s0s)]
        dws = [-_dot(dvn, s0, _NT) for dvn, s0 in zip(dv_news, s0s)]
        for hh in hs:
            dS[hh] = (_dot(cms[hh]["q_dec"], dos[hh], _TN) + cms[hh]["gl"] * ds1s[hh]
                      - _dot(ws[hh], dv_news[hh], _TN))
        dvbs = [_dot(tm, dvn, _TN) for tm, dvn in zip(tms, dv_news)]
        dkbgs = [_dot(tm, dw, _TN) for tm, dw in zip(tms, dws)]
        dAs = [-jnp.where(strict, _dot(dvb, u, _NT) + _dot(dkbg, w, _NT), 0.0)
               for dvb, u, dkbg, w in zip(dvbs, us, dkbgs, ws)]
        dkks = [dA * cm["decay"] for dA, cm in zip(dAs, cms)]
        dqkraws = [dqk * cm["decay"] for dqk, cm in zip(dqks, cms)]
        Es = [(dA * cm["kk"] + dqk * cm["qkraw"]) * cm["decay"] for dA, dqk, cm in zip(dAs, dqks, cms)]
        dkbs = [_dot(dkk, k) + dkbg * cm["eg"] for dkk, k, dkbg, cm in zip(dkks, ks, dkbgs, cms)]
        dqss = [_dot(dqr, k) + dqd * cm["eg"] for dqr, k, dqd, cm in zip(dqkraws, ks, dq_decs, cms)]
        for hh in hs:
            cm = cms[hh]
            dk_ref[:, sls[hh]] = (_dot(dqkraws[hh], cm["qs"], _TN) + _dot(dkks[hh], cm["kb"], _TN)
                                  + dk_decs[hh] * cm["ekd"] + dkbs[hh] * cm["b_col"])
            dv_ref[:, sls[hh]] = dvbs[hh] * cm["b_col"]
            dq_ref[:, sls[hh]] = dqss[hh] * (_GD ** -0.5)
        for hh in hs:
            cm, k, E = cms[hh], ks[hh], Es[hh]
            eg, ekd = cm["eg"], cm["ekd"]
            dbeta_col = jnp.sum(dvbs[hh] * vs_[hh] + dkbs[hh] * k, axis=1, keepdims=True)
            t_kd = jnp.sum(dk_decs[hh] * k, axis=1, keepdims=True) * ekd
            c1 = (jnp.sum(E, axis=1, keepdims=True) + jnp.sum(dkbgs[hh] * cm["kb"], axis=1, keepdims=True) * eg
                  + jnp.sum(dq_decs[hh] * cm["qs"], axis=1, keepdims=True) * eg - t_kd)
            r1 = jnp.sum(E, axis=0, keepdims=True)
            dgam_last = jnp.sum(t_kd, axis=0, keepdims=True) + dgls[hh] * cm["gl"]
            dgam_col = c1 - jnp.sum(jnp.where(eye, r1, 0.0), axis=1, keepdims=True)
            dg_ref[hh] = jnp.sum(jnp.where(ii >= jj, dgam_col, 0.0), axis=0, keepdims=True) + dgam_last
            db_ref[hh] = jnp.sum(jnp.where(eye, dbeta_col, 0.0), axis=0, keepdims=True)

    qspec = pl.BlockSpec((C, hb * _GD), lambda h, n: (N - 1 - n, h))
    gspec = pl.BlockSpec((hb, None, 1, C), lambda h, n: (h, N - 1 - n, 0, 0))
    return pl.pallas_call(
        body, name=name, grid=(H // hb, N),
        in_specs=[qspec, qspec, qspec, gspec, gspec,
                  pl.BlockSpec((hb, None, _GD, _GD), lambda h, n: (h, N - 1 - n, 0, 0)),
                  pl.BlockSpec((hb, None, C, C), lambda h, n: (h, N - 1 - n, 0, 0)), qspec],
        out_specs=[qspec, qspec, qspec, gspec, gspec],
        out_shape=[jax.ShapeDtypeStruct((T, D), _F32)] * 3 + [jax.ShapeDtypeStruct((H, N, 1, C), _F32)] * 2,
        scratch_shapes=[pltpu.VMEM((hb, _GD, _GD), _F32)],
        compiler_params=_cp("arbitrary", "arbitrary"),
    )(qn, kn, vs, g_r, b_r, s_all, t_all, do)


def _onorm_fwd(o, proj, go, li, D, name, tb=512):
    T = o.shape[0]
    H = D // _GD
    tb = _pick(T, tb, 8)

    def body(o_ref, z_ref, go_ref, y_ref):
        ov, zv, g = o_ref[...], z_ref[...], go_ref[...]
        for h in range(H):
            sl = slice(h * _GD, (h + 1) * _GD)
            oh = ov[:, sl]
            r = lax.rsqrt(jnp.mean(oh * oh, axis=1, keepdims=True) + _EPS)
            y_ref[:, sl] = (oh * r * g * _silu(zv[:, sl])).astype(y_ref.dtype)

    return pl.pallas_call(
        body, name=name, grid=(T // tb,), in_specs=[_rspec(tb, D), _rspec(tb, D, 5), _lspec((1, _GD), li)],
        out_specs=_rspec(tb, D), out_shape=jax.ShapeDtypeStruct((T, D), _MMT), compiler_params=_cp("parallel"),
    )(o, proj, go)


def _onorm_bwd(dy, o, proj, go, li, D, into, name, tb=256):
    T = o.shape[0]
    H = D // _GD
    tb = _pick(T, tb, 8)

    def body(dy_ref, o_ref, z_ref, go_ref, _, do_ref, dz_ref, dgo_ref):
        @pl.when(pl.program_id(0) == 0)
        def _():
            dgo_ref[...] = jnp.zeros_like(dgo_ref)

        dyv, ov, zv, g = dy_ref[...].astype(_F32), o_ref[...], z_ref[...], go_ref[...]
        dgo = jnp.zeros((1, _GD), _F32)
        for h in range(H):
            sl = slice(h * _GD, (h + 1) * _GD)
            oh, zh, dyh = ov[:, sl], zv[:, sl], dyv[:, sl]
            r = lax.rsqrt(jnp.mean(oh * oh, axis=1, keepdims=True) + _EPS)
            on = oh * r
            sz = _silu(zh)
            dgo = dgo + jnp.sum(dyh * sz * on, axis=0, keepdims=True)
            don = dyh * sz * g
            do_ref[:, sl] = r * (don - on * jnp.mean(don * on, axis=1, keepdims=True))
            dz_ref[:, sl] = (dyh * on * g * _dsilu(zh)).astype(dz_ref.dtype)
        dgo_ref[...] += dgo

    return pl.pallas_call(
        body, name=name, grid=(T // tb,),
        in_specs=[_rspec(tb, D), _rspec(tb, D), _rspec(tb, D, 5), _lspec((1, _GD), li), _HBM],
        out_specs=[_rspec(tb, D), _rspec(tb, D, 5), _fspec((1, _GD))],
        out_shape=[jax.ShapeDtypeStruct((T, D), _F32), jax.ShapeDtypeStruct(into.shape, into.dtype),
                   jax.ShapeDtypeStruct((1, _GD), _F32)],
        input_output_aliases={4: 1}, compiler_params=_cp("arbitrary"),
    )(dy, o, proj, go, into)


def _merge_bwd(dm, pa, pb, proj, D, name, tb=256):
    T, PW = proj.shape
    tb = _pick(T, tb, 8)

    def body(dm_ref, pa_ref, pb_ref, ga_ref, gb_ref, dpa_ref, dpb_ref, dg_ref):
        d = dm_ref[...].astype(_F32)
        sa, sb = _sigmoid(ga_ref[...]), _sigmoid(gb_ref[...])
        dpa_ref[...] = (d * sa).astype(dpa_ref.dtype)
        dpb_ref[...] = (d * sb).astype(dpb_ref.dtype)
        dg_ref[:, :D] = (d * pa_ref[...].astype(_F32) * sa * (1.0 - sa)).astype(dg_ref.dtype)
        dg_ref[:, D:] = (d * pb_ref[...].astype(_F32) * sb * (1.0 - sb)).astype(dg_ref.dtype)

    return pl.pallas_call(
        body, name=name, grid=(T // tb,),
        in_specs=[_rspec(tb, D), _rspec(tb, D), _rspec(tb, D), _rspec(tb, D, 6), _rspec(tb, D, 7)],
        out_specs=[_rspec(tb, D), _rspec(tb, D), _rspec(tb, 2 * D, 3)],
        out_shape=[jax.ShapeDtypeStruct((T, D), _MMT)] * 2 + [jax.ShapeDtypeStruct((T, PW), _MMT)],
        compiler_params=_cp("parallel"),
    )(dm, pa, pb, proj, proj)


def _loss_head(x, tgt, fg, name, tb=256):
    T, D = x.shape
    tb = _pick(T, tb, 8)

    def body(x_ref, t_ref, fg_ref, loss_ref, dx_ref, dfg_ref):
        @pl.when(pl.program_id(0) == 0)
        def _():
            loss_ref[...] = jnp.zeros_like(loss_ref)
            dfg_ref[...] = jnp.zeros_like(dfg_ref)

        xv, fg_ = x_ref[...], fg_ref[...]
        r = lax.rsqrt(jnp.mean(xv * xv, axis=1, keepdims=True) + _EPS)
        xn = xv * r
        e = xn * fg_ - t_ref[...]
        loss_ref[...] += (0.5 / D) * jnp.sum(jnp.sum(e * e, axis=1, keepdims=True), axis=0, keepdims=True)
        dy = e * (1.0 / D)
        dfg_ref[...] += jnp.sum(dy * xn, axis=0, keepdims=True)
        dxn = dy * fg_
        dx_ref[...] = r * (dxn - xn * jnp.mean(dxn * xn, axis=1, keepdims=True))

    return pl.pallas_call(
        body, name=name, grid=(T // tb,), in_specs=[_rspec(tb, D), _rspec(tb, D), _fspec((1, D))],
        out_specs=[_fspec((1, 1)), _rspec(tb, D), _fspec((1, D))],
        out_shape=[jax.ShapeDtypeStruct((1, 1), _F32), jax.ShapeDtypeStruct((T, D), _F32),
                   jax.ShapeDtypeStruct((1, D), _F32)],
        compiler_params=_cp("arbitrary"),
    )(x, tgt, fg)


def _row_tile(R, W, budget=1 << 20, unit=8):
    if R * W * 4 <= budget or R % unit:
        return R
    best = unit
    for t in range(unit, R + 1, unit):
        if R % t == 0 and t * W * 4 <= budget:
            best = t
    return best


def _add_own_half(send, got, half, name):
    P, Rp, W = send.shape
    Rh = Rp // 2
    tb = _row_tile(Rh, W, 1 << 21, 16)

    def body(h_ref, a_ref, b_ref, o_ref):
        o_ref[...] = (a_ref[...].astype(_F32) + b_ref[...].astype(_F32)).astype(o_ref.dtype)

    return pl.pallas_call(
        body, name=name,
        grid_spec=pltpu.PrefetchScalarGridSpec(
            num_scalar_prefetch=1, grid=(P, Rh // tb),
            in_specs=[pl.BlockSpec((None, None, tb, W), lambda k, i, h: (k, h[0], i, 0)),
                      pl.BlockSpec((None, tb, W), lambda k, i, h: (k, i, 0))],
            out_specs=pl.BlockSpec((None, tb, W), lambda k, i, h: (k, i, 0))),
        out_shape=jax.ShapeDtypeStruct((P, Rh, W), send.dtype), compiler_params=_cp("parallel", "parallel"),
    )(half, send.reshape(P, 2, Rh, W), got)


def _sum_slots(st, name):
    P, R, W = st.shape
    tb = _row_tile(R, W, 1 << 20, 16)

    def body(s_ref, o_ref):
        acc = s_ref[0].astype(_F32)
        for p in range(1, P):
            acc = acc + s_ref[p].astype(_F32)
        o_ref[...] = acc

    return pl.pallas_call(
        body, name=name, grid=(R // tb,), in_specs=[pl.BlockSpec((P, tb, W), lambda i: (0, i, 0))],
        out_specs=_rspec(tb, W), out_shape=jax.ShapeDtypeStruct((R, W), _F32), compiler_params=_cp("parallel"),
    )(st)


def _adamw(w, gst, m, v, name):
    R, W = w.shape
    P = gst.shape[0]
    tb = _row_tile(R, W, 1 << 21)
    c1, c2 = 1.0 - _B1 ** _STEP, 1.0 - _B2 ** _STEP

    def body(w_ref, g_ref, m_ref, v_ref, go_ref, d_ref, mo_ref, vo_ref):
        g = g_ref[0]
        for p in range(1, P):
            g = g + g_ref[p]
        mn = _B1 * m_ref[...] + (1.0 - _B1) * g
        vn = _B2 * v_ref[...] + (1.0 - _B2) * (g * g)
        go_ref[...] = g
        mo_ref[...] = mn
        vo_ref[...] = vn
        d_ref[...] = -_LR * ((mn / c1) / (jnp.sqrt(vn / c2) + _AEPS) + _WD * w_ref[...])

    spec = _rspec(tb, W)
    return pl.pallas_call(
        body, name=name, grid=(R // tb,),
        in_specs=[spec, pl.BlockSpec((P, tb, W), lambda i: (0, i, 0)), spec, spec],
        out_specs=[spec] * 4, out_shape=[jax.ShapeDtypeStruct((R, W), _F32)] * 4, compiler_params=_cp("parallel"),
    )(w, gst, m, v)


def _as2d(a):
    if a.ndim == 1:
        return a.reshape(1, -1)
    return a.reshape(-1, a.shape[-1])


def kernel(x, c, ada_w, ada_b, norm1_g, w_in, conv_w, spatial_w, spatial_b, v_norm_g, a_log, dt_bias, o_norm_g, w_branch_a, w_branch_b, w_out, norm2_g, w_ffn_in, w_ffn_out, final_g, loss_target, m_ada_w, m_ada_b, m_norm1_g, m_w_in, m_conv_w, m_spatial_w, m_spatial_b, m_v_norm_g, m_a_log, m_dt_bias, m_o_norm_g, m_w_branch_a, m_w_branch_b, m_w_out, m_norm2_g, m_w_ffn_in, m_w_ffn_out, m_final_g, v_ada_w, v_ada_b, v_norm1_g, v_w_in, v_conv_w, v_spatial_w, v_spatial_b, v_v_norm_g, v_a_log, v_dt_bias, v_o_norm_g, v_w_branch_a, v_w_branch_b, v_w_out, v_norm2_g, v_w_ffn_in, v_w_ffn_out, v_final_g):
    xb, tgt = x[0], loss_target[0]
    T, D = xb.shape
    L, H, G = ada_w.shape[0], a_log.shape[1], spatial_w.shape[1]
    F = 4 * w_ffn_out.shape[1]
    N = T // _BC
    Ws = ada_w.shape[2]
    Wc = w_in.shape[2]
    PW = 8 * D + _LANE
    ix, iy, ic = lax.axis_index("x"), lax.axis_index("y"), lax.axis_index("c")
    me = 4 * ix + 2 * iy + ic

    c_all = _gather8(c, "gather_c").reshape(8, D)
    modp = _ada_fwd(c_all, ada_w, "ada_fwd")
    n_mod, n_cw = L * 8 * Ws, L * _KC * conv_w.shape[2]
    pad = (-(n_mod + n_cw)) % _LANE
    pay = jnp.concatenate([modp.reshape(-1), conv_w.reshape(-1), jnp.zeros((pad,), _F32)]).reshape(-1, _LANE)
    pay_all = _gather8(pay, "gather_mod").reshape(8, -1)
    mod_full = jnp.concatenate([pay_all[2 * k, :n_mod].reshape(L, 8, Ws) for k in range(4)], axis=-1)
    cw_full = jnp.concatenate([pay_all[2 * k, n_mod:n_mod + n_cw].reshape(L, _KC, -1) for k in range(4)], axis=-1)
    mod = lax.dynamic_index_in_dim(mod_full, me, axis=1, keepdims=False) + ada_b
    mods = [[mod[l, j * D:(j + 1) * D].reshape(1, D) for j in range(6)] for l in range(L)]

    big = [w_in, w_branch_a, w_branch_b, w_out, w_ffn_in, w_ffn_out]
    chip = 2 * ix + iy
    starts = [(k * Wc) // 16 * 16 for k in range(4)]
    Hh = max(-(-((k + 1) * Wc) // 16) * 16 - starts[k] for k in range(4))
    No = max(s + Hh for s in starts)
    my_off = jnp.asarray([k * Wc - starts[k] for k in range(4)], jnp.int32)[chip]
    cuts = sorted(set(starts + [s + Hh for s in starts]))

    pers = [Hh, D // 4, D // 4, D // 4, 2 * F // 4, F // 4]
    roff = [0]
    for p in pers:
        roff.append(roff[-1] + L * p)
    Rp = -(-roff[-1] // (32 * _NCH)) * (32 * _NCH)
    rpad = Rp - roff[-1]

    hull = lax.dynamic_update_slice(jnp.zeros((L, Hh, D), _F32), jnp.transpose(w_in, (0, 2, 1)), (0, my_off, 0))
    shard = jnp.concatenate(
        [hull.reshape(-1, D).astype(_MMT), w_branch_a.reshape(-1, D).astype(_MMT),
         w_branch_b.reshape(-1, D).astype(_MMT), w_out.reshape(-1, D).astype(_MMT),
         jnp.transpose(w_ffn_in, (0, 2, 1)).reshape(-1, D).astype(_MMT), w_ffn_out.reshape(-1, D).astype(_MMT),
         jnp.zeros((rpad, D), _MMT)], axis=0)
    gw = _fill_from_sibling(_gather_chips(shard, "gather_w"), "gather_w_sib")

    def slab(i, l, k):
        a = roff[i] + l * pers[i]
        return gw[k, a:a + pers[i]]

    def joined(i, l):
        return jnp.concatenate([slab(i, l, k) for k in range(4)], axis=0)

    def orig_rows(hulls, a, b):
        edges = sorted(set([a, b] + [c_ for c_ in cuts if a < c_ < b]))
        out = []
        for lo, hi in zip(edges[:-1], edges[1:]):
            cov = [k for k in range(4) if starts[k] <= lo and hi <= starts[k] + Hh]
            piece = hulls[cov[0]][lo - starts[cov[0]]:hi - starts[cov[0]]]
            for k in cov[1:]:
                piece = piece + hulls[k][lo - starts[k]:hi - starts[k]]
            out.append(piece)
        return out

    wt_in_p = []
    for l in range(L):
        hulls = [slab(0, l, k) for k in range(4)]
        wt_in_p.append(jnp.concatenate(
            orig_rows(hulls, 0, 6 * D) + orig_rows(hulls, 6 * D + 2 * H, 8 * D + 2 * H)
            + orig_rows(hulls, 6 * D, 6 * D + 2 * H) + [jnp.zeros((_LANE - 2 * H, D), _MMT)], axis=0))
    w_a, w_b, w_o, wt_fi, w_fo = ([joined(i, l) for l in range(L)] for i in range(1, 6))

    sbt = jnp.transpose(spatial_b, (0, 2, 1))
    gv3, go3 = v_norm_g.reshape(L, 1, D), o_norm_g.reshape(L, 1, _GD)
    zpad = jnp.zeros((L, _LANE - 2 * H), _F32)
    alog_row = jnp.concatenate([jnp.zeros((L, H), _F32), a_log, zpad], axis=1).reshape(L, 1, _LANE)
    dtb_row = jnp.concatenate([jnp.zeros((L, H), _F32), dt_bias, zpad], axis=1).reshape(L, 1, _LANE)

    def rows_of(tok):
        return jnp.transpose(tok.reshape(N, _BC, H), (2, 0, 1)).reshape(H, N, 1, _BC)

    def toks_of(rows):
        return jnp.transpose(rows.reshape(H, N, _BC), (1, 2, 0)).reshape(T, H)

    saved = []
    xc = xb
    for l in range(L):
        sh1, sc1, gt1, sh2, sc2, gt2 = mods[l]
        g1, g2 = norm1_g[l].reshape(1, D), norm2_g[l].reshape(1, D)
        h = _norm_mod(xc, g1, sc1, sh1, f"norm1_{l}")
        proj = _mm_nt(h, wt_in_p[l], f"proj_{l}", tn=1664)
        ya = _gmlp_fwd(proj, spatial_w, sbt, gv3, l, D, f"gmlp_{l}")
        qn, kn, vs, bg = _conv_fwd(proj, cw_full, alog_row, dtb_row, l, D, f"conv_{l}")
        g_r, b_r = rows_of(bg[:, H:2 * H]), rows_of(bg[:, :H])
        o, s_all, t_all = _gdn_fwd(qn, kn, vs, g_r, b_r, f"gdn_{l}")
        yb = _onorm_fwd(o, proj, go3, l, D, f"onorm_{l}")
        pa, pb, mg = _branch_merge(ya, yb, w_a[l], w_b[l], proj, f"branch_{l}")
        p1, x1 = _mm_res(mg, w_o[l], xc, gt1, f"wout_{l}")
        h2 = _norm_mod(x1, g2, sc2, sh2, f"norm2_{l}")
        gate, up, act = _ffin_swiglu(h2, wt_fi[l], f"ffin_{l}")
        p2, x2 = _mm_res(act, w_fo[l], x1, gt2, f"ffout_{l}")
        saved.append(dict(x=xc, h=h, proj=proj, ya=ya, yb=yb, qn=qn, kn=kn, vs=vs, g_r=g_r, b_r=b_r, o=o,
                          s_all=s_all, t_all=t_all, pa=pa, pb=pb, mg=mg, p1=p1, x1=x1, h2=h2, gate=gate, up=up,
                          act=act, p2=p2))
        xc = x2

    loss11, dx, dfg = _loss_head(xc, tgt, final_g.reshape(1, D), "loss_head")
    loss = lax.psum(loss11[0, 0], ("x", "y", "c"))

    gbig = {k: None for k in ("w_in", "w_a", "w_b", "w_o", "w_fi", "w_fo")}
    small = {k: [None] * L for k in ("dmod", "n1", "n2", "sw", "sb", "gv", "cw", "al", "dt", "go")}
    for l in reversed(range(L)):
        sv = saved[l]
        sh1, sc1, gt1, sh2, sc2, gt2 = mods[l]
        g1, g2 = norm1_g[l].reshape(1, D), norm2_g[l].reshape(1, D)
        proj = sv["proj"]
        dp2, dgt2 = _resid_bwd(dx, sv["p2"], gt2, f"res2b_{l}")
        dgate, dup = _ffoutb_swiglu(dp2, w_fo[l], sv["gate"], sv["up"], f"ffoutb_{l}")
        gbig["w_fo"] = _mm_tn(sv["act"], dp2, f"ffoutw_{l}", l, L, gbig["w_fo"], tm=1408)
        dx1, dgm2, dsh2 = _mm_normb((dgate, dup), wt_fi[l], sv["x1"], dx, g2, sc2, f"ffinb_{l}")
        gbig["w_fi"] = _mm_tn(dgate, sv["h2"], f"ffinwg_{l}", l, L, gbig["w_fi"], mtot=2 * F, tm=1408)
        gbig["w_fi"] = _mm_tn(dup, sv["h2"], f"ffinwu_{l}", l, L, gbig["w_fi"], row0=F, mtot=2 * F, tm=1408)
        dp1, dgt1 = _resid_bwd(dx1, sv["p1"], gt1, f"res1b_{l}")
        dmg = _mm_nt(dp1, w_o[l], f"woutb_{l}", out_dtype=_MMT)
        gbig["w_o"] = _mm_tn(sv["mg"], dp1, f"woutw_{l}", l, L, gbig["w_o"])
        dpa, dpb, dproj = _merge_bwd(dmg, sv["pa"], sv["pb"], proj, D, f"mergeb_{l}")
        dya = _mm_nt(dpa, w_a[l], f"brab_{l}", out_dtype=_MMT)
        gbig["w_a"] = _mm_tn(sv["ya"], dpa, f"braw_{l}", l, L, gbig["w_a"])
        dyb = _mm_nt(dpb, w_b[l], f"brbb_{l}", out_dtype=_MMT)
        gbig["w_b"] = _mm_tn(sv["yb"], dpb, f"brbw_{l}", l, L, gbig["w_b"])
        dproj, dsw, dsa, dgv = _gmlp_bwd(proj, dya, spatial_w, sbt, gv3, l, D, dproj, f"gmlpb_{l}")
        do, dproj, dgo = _onorm_bwd(dyb, sv["o"], proj, go3, l, D, dproj, f"onormb_{l}")
        dqn, dkn, dvs, dg_r, db_r = _gdn_bwd(sv["qn"], sv["kn"], sv["vs"], sv["g_r"], sv["b_r"], sv["s_all"],
                                             sv["t_all"], do, f"gdnb_{l}")
        dbg = jnp.concatenate([toks_of(db_r), toks_of(dg_r), jnp.zeros((T, _LANE - 2 * H), _F32)], axis=1)
        dc, dproj, dcw, dal, ddt = _conv_bwd1(proj, dqn, dkn, dvs, dbg, cw_full, alog_row, dtb_row, l, D, dproj,
                                              f"convb_{l}")
        dproj = _conv_bwd2(dc, cw_full, l, dproj, f"convx_{l}")
        gbig["w_in"] = _mm_tn(dproj, sv["h"], f"projw_{l}", l, L, gbig["w_in"], tm=640)
        dx, dgm1, dsh1 = _mm_normb(dproj, wt_in_p[l], sv["x"], dx1, g1, sc1, f"projb_{l}", tk=1664)
        small["dmod"][l] = jnp.concatenate([dsh1, dgm1 * g1, dgt1, dsh2, dgm2 * g2, dgt2], axis=1)
        small["n1"][l], small["n2"][l] = dgm1 * (1.0 + sc1), dgm2 * (1.0 + sc2)
        small["sw"][l], small["gv"][l], small["cw"][l], small["go"][l] = dsw, dgv, dcw, dgo
        small["sb"][l] = jnp.transpose(dsa.reshape(_AC, G, _GD).sum(axis=-1))
        small["al"][l], small["dt"][l] = dal[:, H:2 * H], ddt[:, H:2 * H]
    grad_x = dx.reshape(1, T, D)

    names_small = ["dmod", "n1", "n2", "sw", "sb", "gv", "cw", "al", "dt", "go"]
    flat = [jnp.stack(small[k]).reshape(-1) for k in names_small] + [dfg.reshape(-1)]
    sizes = [f.shape[0] for f in flat]
    tot = sum(sizes)
    pad = (-tot) % 1024
    pay = jnp.concatenate(flat + [jnp.zeros((pad,), _F32)]).reshape(-1, 1024)
    sm_all = _gather8(pay, "gather_small").reshape(8, -1)
    offs = [0]
    for s in sizes:
        offs.append(offs[-1] + s)
    part = {k: sm_all[:, offs[i]:offs[i + 1]] for i, k in enumerate(names_small + ["fg"])}
    dmod_all = part["dmod"].reshape(8, L, 6 * D)

    outs = {}

    def update(nm, w, gst, m, v):
        shp = w.shape
        w2 = _as2d(w)
        g, d, mn, vn = _adamw(w2, gst.reshape((gst.shape[0],) + w2.shape), _as2d(m), _as2d(v), f"adamw_{nm}")
        outs[nm] = (g.reshape(shp), d.reshape(shp), mn.reshape(shp), vn.reshape(shp))

    chip = 2 * ix + iy
    dmod_t = jnp.transpose(dmod_all, (1, 0, 2))
    dmod_mine = lax.dynamic_slice_in_dim(dmod_t, chip * Ws, Ws, axis=2)
    g_ada_w = _ada_bwd(jnp.transpose(c_all), dmod_mine, "ada_bwd")
    update("ada_w", ada_w, g_ada_w[None], m_ada_w, v_ada_w)
    update("ada_b", ada_b, dmod_all, m_ada_b, v_ada_b)
    update("norm1_g", norm1_g, part["n1"], m_norm1_g, v_norm1_g)
    update("norm2_g", norm2_g, part["n2"], m_norm2_g, v_norm2_g)
    update("spatial_w", spatial_w, part["sw"], m_spatial_w, v_spatial_w)
    update("spatial_b", spatial_b, part["sb"], m_spatial_b, v_spatial_b)
    update("v_norm_g", v_norm_g, part["gv"], m_v_norm_g, v_v_norm_g)
    update("a_log", a_log, part["al"], m_a_log, v_a_log)
    update("dt_bias", dt_bias, part["dt"], m_dt_bias, v_dt_bias)
    update("o_norm_g", o_norm_g, part["go"], m_o_norm_g, v_o_norm_g)
    update("final_g", final_g, part["fg"], m_final_g, v_final_g)
    cw_cols = conv_w.shape[2]
    dcw_all = part["cw"].reshape(8, L, _KC, 4 * cw_cols)
    update("conv_w", conv_w, lax.dynamic_slice_in_dim(dcw_all, chip * cw_cols, cw_cols, axis=3), m_conv_w, v_conv_w)

    def hull_of(p, k):
        a, b = starts[k], starts[k] + Hh
        out = []
        for lo, hi, plo in ((0, 6 * D, 0), (6 * D, 6 * D + 2 * H, 8 * D), (6 * D + 2 * H, 8 * D + 2 * H, 6 * D),
                            (8 * D + 2 * H, No, None)):
            s, e = max(a, lo), min(b, hi)
            if s < e:
                out.append(jnp.zeros((L, e - s, D), _MMT) if plo is None else p[:, plo + s - lo:plo + e - lo])
        return (out[0] if len(out) == 1 else jnp.concatenate(out, axis=1)).reshape(L * Hh, D)

    pieces = []
    for k in range(4):
        pieces.append(hull_of(gbig["w_in"], k))
        for i, nm in enumerate(("w_a", "w_b", "w_o", "w_fi", "w_fo")):
            per = pers[i + 1]
            pieces.append(gbig[nm][:, k * per:(k + 1) * per].reshape(L * per, D))
        pieces.append(jnp.zeros((rpad, D), _MMT))
    send = jnp.concatenate(pieces, axis=0).reshape(4, Rp, D)
    got = _send_half_to_sibling(send, "reduce_cores")
    chipsum = _add_own_half(send, got, ic.astype(jnp.int32).reshape(1), "add_cores")
    parts = _scatter_to_chips(chipsum, "reduce_chips")
    mine = _sum_slots(parts, "add_chips")
    other = _swap_with_sibling(mine, "swap_cores")
    first = ic == 0
    gsum = jnp.concatenate([jnp.where(first, mine, other), jnp.where(first, other, mine)], axis=0)
    big_names = ["w_in", "w_branch_a", "w_branch_b", "w_out", "w_ffn_in", "w_ffn_out"]
    big_m = [m_w_in, m_w_branch_a, m_w_branch_b, m_w_out, m_w_ffn_in, m_w_ffn_out]
    big_v = [v_w_in, v_w_branch_a, v_w_branch_b, v_w_out, v_w_ffn_in, v_w_ffn_out]
    for i, (nm, w, m, v) in enumerate(zip(big_names, big, big_m, big_v)):
        g = gsum[roff[i]:roff[i + 1]].reshape(L, pers[i], D)
        if i == 0:
            g = jnp.transpose(lax.dynamic_slice_in_dim(g, my_off, Wc, axis=1), (0, 2, 1))
        elif i == 4:
            g = jnp.transpose(g, (0, 2, 1))
        update(nm, w, g[None], m, v)

    order = ["ada_w", "ada_b", "norm1_g", "w_in", "conv_w", "spatial_w", "spatial_b", "v_norm_g", "a_log", "dt_bias",
             "o_norm_g", "w_branch_a", "w_branch_b", "w_out", "norm2_g", "w_ffn_in", "w_ffn_out", "final_g"]
    return (loss, grad_x, *[outs[n][0] for n in order], *[outs[n][1] for n in order],
            *[outs[n][2] for n in order], *[outs[n][3] for n in order])
```

```python
import functools
import math

import jax
import jax.numpy as jnp
from jax import lax
from jax.experimental import pallas as pl
from jax.experimental.pallas import tpu as pltpu

_F32 = jnp.float32
_BF = jnp.bfloat16
_MMT = jnp.bfloat16
_EPS = 1e-6
_GD = 128
_AC = 128
_BC = 64
_KC = 4
_HB = 8
_NCH = 8
_LANE = 128
_VMEM_LIMIT = 56 * 1024 * 1024

_LR, _B1, _B2, _AEPS, _WD, _STEP = 0.001, 0.9, 0.999, 1e-08, 0.01, 10

_NN = (((1,), (0,)), ((), ()))
_NT = (((1,), (1,)), ((), ()))
_TN = (((0,), (0,)), ((), ()))

_MESH = pl.DeviceIdType.MESH


def _cp(*sem):
    return pltpu.CompilerParams(dimension_semantics=tuple(sem), vmem_limit_bytes=_VMEM_LIMIT)


def _dot(a, b, dn=_NN):
    return lax.dot_general(a.astype(_MMT), b.astype(_MMT), dn, preferred_element_type=_F32)


def _pick(n, target, unit=_LANE):
    if n <= target:
        return n
    best = None
    for t in range(unit, target + 1, unit):
        if n % t == 0:
            best = t
    assert best is not None, (n, target)
    return best


def _sigmoid(x):
    return 0.5 * jnp.tanh(0.5 * x) + 0.5


def _silu(x):
    return x * _sigmoid(x)


def _silu_and_grad(x):
    s = _sigmoid(x)
    xs = x * s
    return xs, s + xs * (1.0 - s)


_GK = math.sqrt(2.0 / math.pi)


def _gelu(x):
    return 0.5 * x * (1.0 + jnp.tanh(_GK * (x + 0.044715 * x * x * x)))


def _dgelu(x):
    t = jnp.tanh(_GK * (x + 0.044715 * x * x * x))
    return 0.5 * (1.0 + t) + 0.5 * x * (1.0 - t * t) * _GK * (1.0 + 3.0 * 0.044715 * x * x)


def _softplus(x):
    return jnp.maximum(x, 0.0) + jnp.log(1.0 + jnp.exp(-jnp.abs(x)))


def _rspec(tb, w, cb=0):
    return pl.BlockSpec((tb, w), lambda i: (i, cb))


def _fspec(shape):
    nd = len(shape)
    return pl.BlockSpec(tuple(shape), lambda i: (0,) * nd)


def _lspec(tail, li):
    nd = len(tail)
    return pl.BlockSpec((None,) + tuple(tail), lambda i: (li,) + (0,) * nd)


def _slot_all8(x, y, c):
    return 4 * x + 2 * y + c


def _gather8(v, name):
    R, W = v.shape

    def body(v_ref, o_ref, ssem, rsem, lsem):
        x, y, c = lax.axis_index("x"), lax.axis_index("y"), lax.axis_index("c")
        sib = (x, y, 1 - c)
        chips = _other_chips(x, y)

        def slot(px, py, pc):
            return o_ref.at[_slot_all8(px, py, pc)]

        own = pltpu.make_async_copy(v_ref, slot(x, y, c), lsem)
        own.start()
        started = [_rcopy(v_ref, slot(x, y, c), ssem.at[0], rsem.at[0], sib)]
        started += [_rcopy(v_ref, slot(x, y, c), ssem.at[1 + j], rsem.at[1 + j], (px, py, c))
                    for j, (px, py) in enumerate(chips)]
        for cp in started:
            cp.start()
        for j, (px, py) in enumerate(chips):
            blk = slot(px, py, c)
            _rcopy(blk, blk, ssem.at[1 + j], rsem.at[1 + j], (px, py, c)).wait_recv()
            fw = _rcopy(blk, blk, ssem.at[4 + j], rsem.at[4 + j], sib)
            fw.start()
            started.append(fw)
        blk = slot(x, y, 1 - c)
        _rcopy(blk, blk, ssem.at[0], rsem.at[0], sib).wait_recv()
        for j, (px, py) in enumerate(chips):
            blk = slot(px, py, 1 - c)
            _rcopy(blk, blk, ssem.at[4 + j], rsem.at[4 + j], sib).wait_recv()
        for cp in started:
            cp.wait_send()
        own.wait()

    return pl.pallas_call(
        body, name=name, out_shape=jax.ShapeDtypeStruct((8, R, W), v.dtype), in_specs=[_HBM], out_specs=_HBM,
        scratch_shapes=[pltpu.SemaphoreType.DMA((7,)), pltpu.SemaphoreType.DMA((7,)), pltpu.SemaphoreType.DMA],
    )(v)


def _rcopy(src, dst, ssem, rsem, dev):
    return pltpu.make_async_remote_copy(src_ref=src, dst_ref=dst, send_sem=ssem, recv_sem=rsem,
                                        device_id=dev, device_id_type=_MESH)


def _other_chips(x, y):
    return [(1 - x, y), (x, 1 - y), (1 - x, 1 - y)]


_HBM = pl.BlockSpec(memory_space=pl.ANY)


def _gather_chips(shard, name):
    Rp, W = shard.shape
    Rh = Rp // 2
    rc = Rh // _NCH
    hq = _NCH // 2

    def body(s_ref, o_ref, ssem, rsem, lsem):
        x, y, c = lax.axis_index("x"), lax.axis_index("y"), lax.axis_index("c")
        chip = 2 * x + y
        xn, yn, dg = _other_chips(x, y)
        cx, cy, cd = 2 * xn[0] + xn[1], 2 * yn[0] + yn[1], 2 * dg[0] + dg[1]

        def rows(q):
            return pl.ds(c * Rh + q * rc, rc)

        locs = []
        for q in range(_NCH):
            lc = pltpu.make_async_copy(s_ref.at[rows(q)], o_ref.at[chip, rows(q)], lsem.at[q])
            lc.start()
            locs.append(lc)
        started = []
        for q in range(_NCH):
            for j, nb in ((0, xn), (1, yn)):
                cp = _rcopy(s_ref.at[rows(q)], o_ref.at[chip, rows(q)], ssem.at[j * _NCH + q], rsem.at[j * _NCH + q],
                            (nb[0], nb[1], c))
                cp.start()
                started.append(cp)
        for q in range(_NCH):
            bx = o_ref.at[cx, rows(q)]
            _rcopy(bx, bx, ssem.at[q], rsem.at[q], (xn[0], xn[1], c)).wait_recv()
            if q >= hq:
                rl = _rcopy(bx, bx, ssem.at[2 * _NCH + q], rsem.at[2 * _NCH + q], (yn[0], yn[1], c))
                rl.start()
                started.append(rl)
            by = o_ref.at[cy, rows(q)]
            _rcopy(by, by, ssem.at[_NCH + q], rsem.at[_NCH + q], (yn[0], yn[1], c)).wait_recv()
            if q < hq:
                rl = _rcopy(by, by, ssem.at[2 * _NCH + q], rsem.at[2 * _NCH + q], (xn[0], xn[1], c))
                rl.start()
                started.append(rl)
        for q in range(_NCH):
            bd = o_ref.at[cd, rows(q)]
            _rcopy(bd, bd, ssem.at[2 * _NCH + q], rsem.at[2 * _NCH + q], (dg[0], dg[1], c)).wait_recv()
        for cp in started:
            cp.wait_send()
        for lc in locs:
            lc.wait()

    return pl.pallas_call(
        body, name=name, out_shape=jax.ShapeDtypeStruct((4, Rp, W), shard.dtype), in_specs=[_HBM], out_specs=_HBM,
        scratch_shapes=[pltpu.SemaphoreType.DMA((3 * _NCH,))] * 2 + [pltpu.SemaphoreType.DMA((_NCH,))],
    )(shard)


def _fill_from_sibling(buf, name):
    P, Rp, W = buf.shape
    Rh = Rp // 2
    rc = Rh // _NCH

    def body(s_ref, o_ref, ssem, rsem):
        x, y, c = lax.axis_index("x"), lax.axis_index("y"), lax.axis_index("c")
        cps = []
        for k in range(P):
            for q in range(_NCH):
                r = pl.ds(c * Rh + q * rc, rc)
                cp = _rcopy(s_ref.at[k, r], o_ref.at[k, r], ssem.at[k * _NCH + q], rsem.at[k * _NCH + q],
                            (x, y, 1 - c))
                cp.start()
                cps.append(cp)
        for k in range(P):
            for q in range(_NCH):
                blk = o_ref.at[k, pl.ds((1 - c) * Rh + q * rc, rc)]
                _rcopy(blk, blk, ssem.at[k * _NCH + q], rsem.at[k * _NCH + q], (x, y, 1 - c)).wait_recv()
        for cp in cps:
            cp.wait_send()

    return pl.pallas_call(
        body, name=name, out_shape=jax.ShapeDtypeStruct(buf.shape, buf.dtype), in_specs=[_HBM], out_specs=_HBM,
        scratch_shapes=[pltpu.SemaphoreType.DMA((P * _NCH,))] * 2, input_output_aliases={0: 0},
    )(buf)


def _send_half_to_sibling(send, name):
    P, Rp, W = send.shape
    Rh = Rp // 2
    rc = Rh // _NCH

    def body(s_ref, o_ref, ssem, rsem):
        x, y, c = lax.axis_index("x"), lax.axis_index("y"), lax.axis_index("c")
        cps = []
        for k in range(P):
            for q in range(_NCH):
                cp = _rcopy(s_ref.at[k, pl.ds((1 - c) * Rh + q * rc, rc)], o_ref.at[k, pl.ds(q * rc, rc)],
                            ssem.at[k * _NCH + q], rsem.at[k * _NCH + q], (x, y, 1 - c))
                cp.start()
                cps.append(cp)
        for cp in cps:
            cp.wait()

    return pl.pallas_call(
        body, name=name, out_shape=jax.ShapeDtypeStruct((P, Rh, W), send.dtype), in_specs=[_HBM], out_specs=_HBM,
        scratch_shapes=[pltpu.SemaphoreType.DMA((P * _NCH,))] * 2,
    )(send)


def _scatter_to_chips(cs, name):
    P, Rh, W = cs.shape
    rc = Rh // _NCH

    def body(s_ref, o_ref, ssem, rsem, lsem):
        x, y, c = lax.axis_index("x"), lax.axis_index("y"), lax.axis_index("c")
        chip = 2 * x + y
        peers = _other_chips(x, y)
        locs = []
        for q in range(_NCH):
            r = pl.ds(q * rc, rc)
            lc = pltpu.make_async_copy(s_ref.at[chip, r], o_ref.at[chip, r], lsem.at[q])
            lc.start()
            locs.append(lc)
        cps = []
        for j, (px, py) in enumerate(peers):
            for q in range(_NCH):
                r = pl.ds(q * rc, rc)
                cp = _rcopy(s_ref.at[2 * px + py, r], o_ref.at[chip, r], ssem.at[j * _NCH + q], rsem.at[j * _NCH + q],
                            (px, py, c))
                cp.start()
                cps.append(cp)
        for j, (px, py) in enumerate(peers):
            for q in range(_NCH):
                blk = o_ref.at[2 * px + py, pl.ds(q * rc, rc)]
                _rcopy(blk, blk, ssem.at[j * _NCH + q], rsem.at[j * _NCH + q], (px, py, c)).wait_recv()
        for cp in cps:
            cp.wait_send()
        for lc in locs:
            lc.wait()

    return pl.pallas_call(
        body, name=name, out_shape=jax.ShapeDtypeStruct((P, Rh, W), cs.dtype), in_specs=[_HBM], out_specs=_HBM,
        scratch_shapes=[pltpu.SemaphoreType.DMA((3 * _NCH,))] * 2 + [pltpu.SemaphoreType.DMA((_NCH,))],
    )(cs)


def _swap_with_sibling(v, name):
    R, W = v.shape
    rc = R // _NCH

    def body(s_ref, o_ref, ssem, rsem):
        x, y, c = lax.axis_index("x"), lax.axis_index("y"), lax.axis_index("c")
        cps = []
        for q in range(_NCH):
            r = pl.ds(q * rc, rc)
            cp = _rcopy(s_ref.at[r], o_ref.at[r], ssem.at[q], rsem.at[q], (x, y, 1 - c))
            cp.start()
            cps.append(cp)
        for cp in cps:
            cp.wait()

    return pl.pallas_call(
        body, name=name, out_shape=jax.ShapeDtypeStruct((R, W), v.dtype), in_specs=[_HBM], out_specs=_HBM,
        scratch_shapes=[pltpu.SemaphoreType.DMA((_NCH,))] * 2,
    )(v)


def _mm_nt(a, b, name, out_dtype=_F32, tm=1024, tn=1024):
    M, K = a.shape
    N = b.shape[0]
    tm, tn = _pick(M, tm, 8), _pick(N, tn)

    def body(a_ref, b_ref, o_ref):
        o_ref[...] = lax.dot_general(a_ref[...], b_ref[...], _NT, preferred_element_type=_F32).astype(o_ref.dtype)

    return pl.pallas_call(
        body, name=name, grid=(M // tm, N // tn),
        in_specs=[pl.BlockSpec((tm, K), lambda i, j: (i, 0)), pl.BlockSpec((tn, K), lambda i, j: (j, 0))],
        out_specs=pl.BlockSpec((tm, tn), lambda i, j: (i, j)),
        out_shape=jax.ShapeDtypeStruct((M, N), out_dtype),
        compiler_params=_cp("parallel", "parallel"),
    )(a, b)


def _mm_tn(a, b, name, li, nl, into=None, row0=0, mtot=None, tm=512, tn=1024):
    T, M = a.shape
    N = b.shape[1]
    tm, tn = _pick(M, tm), _pick(N, tn)
    mtot = M if mtot is None else mtot
    assert row0 % tm == 0
    r0 = row0 // tm

    def body(a_ref, b_ref, *rest):
        o_ref = rest[-1]
        o_ref[...] = lax.dot_general(a_ref[...], b_ref[...], _TN, preferred_element_type=_F32).astype(o_ref.dtype)

    ins = [pl.BlockSpec((T, tm), lambda i, j: (0, i)), pl.BlockSpec((T, tn), lambda i, j: (0, j))]
    return pl.pallas_call(
        body, name=name, grid=(M // tm, N // tn),
        in_specs=ins if into is None else ins + [_HBM],
        out_specs=pl.BlockSpec((None, tm, tn), lambda i, j: (li, i + r0, j)),
        out_shape=jax.ShapeDtypeStruct((nl, mtot, N), _MMT),
        input_output_aliases={} if into is None else {2: 0},
        compiler_params=_cp("parallel", "parallel"),
    )(*((a, b) if into is None else (a, b, into)))


def _mm_res(a, b, x, gt, name, tm=1024, tn=1024, tk=2048):
    M, K = a.shape
    N = b.shape[1]
    tm, tn, tk = _pick(M, tm, 8), _pick(N, tn), _pick(K, tk)
    nk = K // tk

    def body(a_ref, b_ref, x_ref, gt_ref, p_ref, o_ref):
        k = pl.program_id(2)
        part = lax.dot_general(a_ref[...], b_ref[...], _NN, preferred_element_type=_F32)

        @pl.when(k == 0)
        def _():
            p_ref[...] = part

        @pl.when(k > 0)
        def _():
            p_ref[...] += part

        @pl.when(k == nk - 1)
        def _():
            o_ref[...] = x_ref[...] + gt_ref[...] * p_ref[...]

    tile = pl.BlockSpec((tm, tn), lambda i, j, k: (i, j))
    return pl.pallas_call(
        body, name=name, grid=(M // tm, N // tn, nk),
        in_specs=[pl.BlockSpec((tm, tk), lambda i, j, k: (i, k)), pl.BlockSpec((tk, tn), lambda i, j, k: (k, j)),
                  tile, pl.BlockSpec((1, tn), lambda i, j, k: (0, j))],
        out_specs=[tile, tile], out_shape=[jax.ShapeDtypeStruct((M, N), _F32)] * 2,
        compiler_params=_cp("parallel", "parallel", "arbitrary"),
    )(a, b, x, gt)


def _mm_normb(a, b, x, dres, g, sc, name, tm=1024, tk=2048):
    segs = tuple(a) if isinstance(a, (tuple, list)) else (a,)
    ns = len(segs)
    M, K1 = segs[0].shape
    D = b.shape[1]
    tm, tk = _pick(M, tm, 8), _pick(K1, tk)
    nh = K1 // tk
    nk = ns * nh

    def body(*refs):
        a_refs = refs[:ns]
        b_ref, x_ref, dr_ref, g_ref, sc_ref, dx_ref, dgm_ref, dsh_ref, acc = refs[ns:]
        i, k = pl.program_id(0), pl.program_id(1)

        @pl.when(k == 0)
        def _():
            acc[...] = jnp.zeros_like(acc)

        for s in range(ns):
            @pl.when(jnp.logical_and(k >= s * nh, k < (s + 1) * nh))
            def _():
                acc[...] += lax.dot_general(a_refs[s][...], b_ref[...], _NN, preferred_element_type=_F32)

        @pl.when(jnp.logical_and(i == 0, k == 0))
        def _():
            dgm_ref[...] = jnp.zeros_like(dgm_ref)
            dsh_ref[...] = jnp.zeros_like(dsh_ref)

        @pl.when(k == nk - 1)
        def _():
            dh_, xv = acc[...], x_ref[...]
            r = lax.rsqrt(jnp.mean(xv * xv, axis=1, keepdims=True) + _EPS)
            xn = xv * r
            dxn = dh_ * (g_ref[...] * (1.0 + sc_ref[...]))
            dx_ref[...] = dr_ref[...] + r * (dxn - xn * jnp.mean(dxn * xn, axis=1, keepdims=True))
            dgm_ref[...] += jnp.sum(dh_ * xn, axis=0, keepdims=True)
            dsh_ref[...] += jnp.sum(dh_, axis=0, keepdims=True)

    row = pl.BlockSpec((tm, D), lambda i, k: (i, 0))
    vec = pl.BlockSpec((1, D), lambda i, k: (0, 0))
    a_specs = [pl.BlockSpec((tm, tk), lambda i, k, s=s: (i, jnp.clip(k - s * nh, 0, nh - 1))) for s in range(ns)]
    return pl.pallas_call(
        body, name=name, grid=(M // tm, nk),
        in_specs=a_specs + [pl.BlockSpec((tk, D), lambda i, k: (k, 0)), row, row, vec, vec],
        out_specs=[row, vec, vec],
        out_shape=[jax.ShapeDtypeStruct((M, D), _F32), jax.ShapeDtypeStruct((1, D), _F32),
                   jax.ShapeDtypeStruct((1, D), _F32)],
        scratch_shapes=[pltpu.VMEM((tm, D), _F32)],
        compiler_params=_cp("arbitrary", "arbitrary"),
    )(*segs, b, x, dres, g, sc)


def _ffoutb_swiglu(dp, w, gate, up, name, tm=1024, tn=1408):
    M, K = dp.shape
    F = w.shape[0]
    tm, tn = _pick(M, tm, 8), _pick(F, tn)

    def body(a_ref, b_ref, g_ref, u_ref, dg_ref, du_ref):
        d = lax.dot_general(a_ref[...], b_ref[...], _NT, preferred_element_type=_F32)
        gv = g_ref[...].astype(_F32)
        sg, dsg = _silu_and_grad(gv)
        dg_ref[...] = (d * u_ref[...].astype(_F32) * dsg).astype(dg_ref.dtype)
        du_ref[...] = (d * sg).astype(du_ref.dtype)

    tile = pl.BlockSpec((tm, tn), lambda i, j: (i, j))
    return pl.pallas_call(
        body, name=name, grid=(M // tm, F // tn),
        in_specs=[pl.BlockSpec((tm, K), lambda i, j: (i, 0)), pl.BlockSpec((tn, K), lambda i, j: (j, 0)), tile, tile],
        out_specs=[tile, tile], out_shape=[jax.ShapeDtypeStruct((M, F), _MMT)] * 2,
        compiler_params=_cp("parallel", "parallel"),
    )(dp, w, gate, up)


def _ffin_swiglu(a, wt, name, tm=1024, tn=1408):
    M, K = a.shape
    F = wt.shape[0] // 2
    tm, tn = _pick(M, tm, 8), _pick(F, tn)
    nj = F // tn

    def body(a_ref, bg_ref, bu_ref, g_ref, u_ref, act_ref):
        av = a_ref[...]
        g = lax.dot_general(av, bg_ref[...], _NT, preferred_element_type=_F32)
        u = lax.dot_general(av, bu_ref[...], _NT, preferred_element_type=_F32)
        g_ref[...] = g.astype(g_ref.dtype)
        u_ref[...] = u.astype(u_ref.dtype)
        act_ref[...] = (_silu(g) * u).astype(act_ref.dtype)

    tile = pl.BlockSpec((tm, tn), lambda i, j: (i, j))
    return pl.pallas_call(
        body, name=name, grid=(M // tm, nj),
        in_specs=[pl.BlockSpec((tm, K), lambda i, j: (i, 0)), pl.BlockSpec((tn, K), lambda i, j: (j, 0)),
                  pl.BlockSpec((tn, K), lambda i, j: (j + nj, 0))],
        out_specs=[tile] * 3, out_shape=[jax.ShapeDtypeStruct((M, F), _MMT)] * 3,
        compiler_params=_cp("parallel", "parallel"),
    )(a, wt, wt)


def _branch_merge(ya, yb, w_a, w_b, proj, name, tm=1024, tn=1024):
    M, K = ya.shape
    N = w_a.shape[1]
    tm, tn = _pick(M, tm, 8), _pick(N, tn)
    nj = N // tn

    def body(ya_ref, yb_ref, wa_ref, wb_ref, ga_ref, gb_ref, pa_ref, pb_ref, m_ref):
        pa = lax.dot_general(ya_ref[...], wa_ref[...], _NN, preferred_element_type=_F32)
        pb = lax.dot_general(yb_ref[...], wb_ref[...], _NN, preferred_element_type=_F32)
        pa_ref[...] = pa.astype(pa_ref.dtype)
        pb_ref[...] = pb.astype(pb_ref.dtype)
        m_ref[...] = (_sigmoid(ga_ref[...]) * pa + _sigmoid(gb_ref[...]) * pb).astype(m_ref.dtype)

    row = pl.BlockSpec((tm, K), lambda i, j: (i, 0))
    col = pl.BlockSpec((K, tn), lambda i, j: (0, j))
    tile = pl.BlockSpec((tm, tn), lambda i, j: (i, j))
    return pl.pallas_call(
        body, name=name, grid=(M // tm, nj),
        in_specs=[row, row, col, col, pl.BlockSpec((tm, tn), lambda i, j: (i, 6 * nj + j)),
                  pl.BlockSpec((tm, tn), lambda i, j: (i, 7 * nj + j))],
        out_specs=[tile] * 3, out_shape=[jax.ShapeDtypeStruct((M, N), _MMT)] * 3,
        compiler_params=_cp("parallel", "parallel"),
    )(ya, yb, w_a, w_b, proj, proj)


def _ada_fwd(c_all, ada_w, name):
    L, D, Ws = ada_w.shape
    B = c_all.shape[0]

    def body(c_ref, w_ref, o_ref):
        o_ref[...] = _dot(_silu(c_ref[...]), w_ref[...])

    return pl.pallas_call(
        body, name=name, grid=(L,),
        in_specs=[_fspec((B, D)), pl.BlockSpec((None, D, Ws), lambda l: (l, 0, 0))],
        out_specs=pl.BlockSpec((None, B, Ws), lambda l: (l, 0, 0)),
        out_shape=jax.ShapeDtypeStruct((L, B, Ws), _F32), compiler_params=_cp("parallel"),
    )(c_all, ada_w)


def _ada_bwd(c_all_t, dmod, name):
    D, B = c_all_t.shape
    L, _, Ws = dmod.shape

    def body(c_ref, d_ref, o_ref):
        ct = _silu(c_ref[...])
        d = d_ref[...]
        acc = ct[:, 0:1] * d[0:1, :]
        for b in range(1, B):
            acc = acc + ct[:, b:b + 1] * d[b:b + 1, :]
        o_ref[...] = acc

    return pl.pallas_call(
        body, name=name, grid=(L,),
        in_specs=[_fspec((D, B)), pl.BlockSpec((None, B, Ws), lambda l: (l, 0, 0))],
        out_specs=pl.BlockSpec((None, D, Ws), lambda l: (l, 0, 0)),
        out_shape=jax.ShapeDtypeStruct((L, D, Ws), _F32), compiler_params=_cp("parallel"),
    )(c_all_t, dmod)


def _norm_mod(x, g, sc, sh, name, tb=512):
    T, D = x.shape
    tb = _pick(T, tb, 8)

    def body(x_ref, g_ref, sc_ref, sh_ref, h_ref):
        xv = x_ref[...]
        r = lax.rsqrt(jnp.mean(xv * xv, axis=1, keepdims=True) + _EPS)
        h_ref[...] = (xv * r * (g_ref[...] * (1.0 + sc_ref[...])) + sh_ref[...]).astype(h_ref.dtype)

    return pl.pallas_call(
        body, name=name, grid=(T // tb,),
        in_specs=[_rspec(tb, D), _fspec((1, D)), _fspec((1, D)), _fspec((1, D))],
        out_specs=_rspec(tb, D), out_shape=jax.ShapeDtypeStruct((T, D), _MMT), compiler_params=_cp("parallel"),
    )(x, g, sc, sh)


def _resid_bwd(dx, p, gt, name, tb=512):
    T, D = dx.shape
    tb = _pick(T, tb, 8)

    def body(dx_ref, p_ref, gt_ref, dp_ref, dgt_ref):
        i = pl.program_id(0)
        d = dx_ref[...]
        dp_ref[...] = (d * gt_ref[...]).astype(dp_ref.dtype)

        @pl.when(i == 0)
        def _():
            dgt_ref[...] = jnp.zeros_like(dgt_ref)

        dgt_ref[...] += jnp.sum(d * p_ref[...], axis=0, keepdims=True)

    return pl.pallas_call(
        body, name=name, grid=(T // tb,), in_specs=[_rspec(tb, D), _rspec(tb, D), _fspec((1, D))],
        out_specs=[_rspec(tb, D), _fspec((1, D))],
        out_shape=[jax.ShapeDtypeStruct((T, D), _MMT), jax.ShapeDtypeStruct((1, D), _F32)],
        compiler_params=_cp("arbitrary"),
    )(dx, p, gt)


def _gmlp_chunk(u_raw, v_raw, sw_ref, sbt, gv, G):
    u, v = _gelu(u_raw), _gelu(v_raw)
    ii = lax.broadcasted_iota(jnp.int32, (_AC, _AC), 0)
    jj = lax.broadcasted_iota(jnp.int32, (_AC, _AC), 1)
    out = []
    for gi in range(G):
        sl = slice(gi * _GD, (gi + 1) * _GD)
        vg = v[:, sl]
        r = lax.rsqrt(jnp.mean(vg * vg, axis=1, keepdims=True) + _EPS)
        vhat = vg * r
        W = jnp.where(jj <= ii, sw_ref[gi], 0.0)
        s = _dot(W, vhat * gv[:, sl]) + sbt[:, gi:gi + 1]
        out.append((u[:, sl], s, vhat, r, W))
    return out


def _gmlp_fwd(proj, sw, sbt, gv, li, D, name):
    T = proj.shape[0]
    G = D // _GD

    def body(u_ref, v_ref, sw_ref, sbt_ref, gv_ref, y_ref):
        parts = _gmlp_chunk(u_ref[...], v_ref[...], sw_ref, sbt_ref[...], gv_ref[...], G)
        for gi, (u, s, _, _, _) in enumerate(parts):
            y_ref[:, gi * _GD:(gi + 1) * _GD] = (u * s).astype(y_ref.dtype)

    return pl.pallas_call(
        body, name=name, grid=(T // _AC,),
        in_specs=[_rspec(_AC, D, 0), _rspec(_AC, D, 1), _lspec((G, _AC, _AC), li), _lspec((_AC, G), li),
                  _lspec((1, D), li)],
        out_specs=_rspec(_AC, D), out_shape=jax.ShapeDtypeStruct((T, D), _MMT), compiler_params=_cp("parallel"),
    )(proj, proj, sw, sbt, gv)


def _gmlp_bwd(proj, dy, sw, sbt, gv, li, D, into, name):
    T = proj.shape[0]
    G = D // _GD

    def body(u_ref, v_ref, dy_ref, sw_ref, sbt_ref, gv_ref, _, duv_ref, dsw_ref, dsa_ref, dgv_ref):
        i = pl.program_id(0)

        @pl.when(i == 0)
        def _():
            dsw_ref[...] = jnp.zeros_like(dsw_ref)
            dsa_ref[...] = jnp.zeros_like(dsa_ref)
            dgv_ref[...] = jnp.zeros_like(dgv_ref)

        u_raw, v_raw, dy_, gv_ = u_ref[...], v_ref[...], dy_ref[...].astype(_F32), gv_ref[...]
        parts = _gmlp_chunk(u_raw, v_raw, sw_ref, sbt_ref[...], gv_, G)
        ii = lax.broadcasted_iota(jnp.int32, (_AC, _AC), 0)
        jj = lax.broadcasted_iota(jnp.int32, (_AC, _AC), 1)
        dgu, dgv = _dgelu(u_raw), _dgelu(v_raw)
        for gi, (u, s, vhat, r, W) in enumerate(parts):
            sl = slice(gi * _GD, (gi + 1) * _GD)
            dyg = dy_[:, sl]
            ds = dyg * u
            vn = vhat * gv_[:, sl]
            dsw_ref[gi] += jnp.where(jj <= ii, _dot(ds, vn, _NT), 0.0)
            dsa_ref[:, sl] += ds
            dvn = _dot(W, ds, _TN)
            dgv_ref[:, sl] += jnp.sum(dvn * vhat, axis=0, keepdims=True)
            dvh = dvn * gv_[:, sl]
            dvg = r * (dvh - vhat * jnp.mean(dvh * vhat, axis=1, keepdims=True))
            duv_ref[:, sl] = (dyg * s * dgu[:, sl]).astype(duv_ref.dtype)
            duv_ref[:, D + gi * _GD:D + (gi + 1) * _GD] = (dvg * dgv[:, sl]).astype(duv_ref.dtype)

    return pl.pallas_call(
        body, name=name, grid=(T // _AC,),
        in_specs=[_rspec(_AC, D, 0), _rspec(_AC, D, 1), _rspec(_AC, D), _lspec((G, _AC, _AC), li),
                  _lspec((_AC, G), li), _lspec((1, D), li), _HBM],
        out_specs=[_rspec(_AC, 2 * D), _fspec((G, _AC, _AC)), _fspec((_AC, D)), _fspec((1, D))],
        out_shape=[jax.ShapeDtypeStruct(into.shape, into.dtype), jax.ShapeDtypeStruct((G, _AC, _AC), _F32),
                   jax.ShapeDtypeStruct((_AC, D), _F32), jax.ShapeDtypeStruct((1, D), _F32)],
        input_output_aliases={6: 0}, compiler_params=_cp("arbitrary"),
    )(proj, proj, dy, sw, sbt, gv, into)


def _conv_taps(halo, cur, first):
    tb = cur.shape[0]
    full = jnp.concatenate([jnp.where(first, 0.0, halo), cur], axis=0)
    return [full[8:] if j == _KC - 1 else pltpu.roll(full, _KC - 1 - j, 0)[8:] for j in range(_KC)]


def _prev_spec(tb, w, cb):
    return pl.BlockSpec((8, w), lambda i: (jnp.maximum(i * (tb // 8) - 1, 0), cb))


def _l2_heads(x, H):
    outs, rs = [], []
    for h in range(H):
        xh = x[:, h * _GD:(h + 1) * _GD]
        r = lax.rsqrt(jnp.sum(xh * xh, axis=1, keepdims=True) + _EPS)
        outs.append(xh * r)
        rs.append(r)
    return outs, rs


def _gate_rows(ba, alog_row, dtb_row, H):
    lane = lax.broadcasted_iota(jnp.int32, ba.shape, 1)
    beta = _sigmoid(ba)
    g = -jnp.exp(alog_row) * _softplus(ba + dtb_row)
    return lane, beta, g


def _conv_fwd(proj, cw, alog_row, dtb_row, li, D, name, tb=256):
    T = proj.shape[0]
    H = D // _GD
    tb = _pick(T, tb, 8)
    bac = (8 * D) // _LANE

    def body(q_ref, k_ref, v_ref, qh_ref, kh_ref, vh_ref, ba_ref, cw_ref, al_ref, dtb_ref,
             qo_ref, ko_ref, vo_ref, bg_ref):
        first = pl.program_id(0) == 0
        cw_ = cw_ref[...]
        for idx, (cur, halo, out) in enumerate(((q_ref, qh_ref, qo_ref), (k_ref, kh_ref, ko_ref),
                                                 (v_ref, vh_ref, vo_ref))):
            taps = _conv_taps(halo[...], cur[...], first)
            w = cw_[:, idx * D:(idx + 1) * D]
            cv = taps[0] * w[0:1, :]
            for j in range(1, _KC):
                cv = cv + taps[j] * w[j:j + 1, :]
            act = _silu(cv)
            if idx < 2:
                outs, _ = _l2_heads(act, H)
                for h in range(H):
                    out[:, h * _GD:(h + 1) * _GD] = outs[h]
            else:
                out[...] = act
        lane, beta, g = _gate_rows(ba_ref[...], al_ref[...], dtb_ref[...], H)
        bg_ref[...] = jnp.where(lane < H, beta, jnp.where(lane < 2 * H, g, 0.0))

    return pl.pallas_call(
        body, name=name, grid=(T // tb,),
        in_specs=[_rspec(tb, D, 2), _rspec(tb, D, 3), _rspec(tb, D, 4),
                  _prev_spec(tb, D, 2), _prev_spec(tb, D, 3), _prev_spec(tb, D, 4),
                  _rspec(tb, _LANE, bac), _lspec((_KC, 3 * D), li), _lspec((1, _LANE), li), _lspec((1, _LANE), li)],
        out_specs=[_rspec(tb, D), _rspec(tb, D), _rspec(tb, D), _rspec(tb, _LANE)],
        out_shape=[jax.ShapeDtypeStruct((T, D), _F32)] * 3 + [jax.ShapeDtypeStruct((T, _LANE), _F32)],
        compiler_params=_cp("parallel"),
    )(proj, proj, proj, proj, proj, proj, proj, cw, alog_row, dtb_row)


def _conv_bwd1(proj, dqn, dkn, dvs, dbg, cw, alog_row, dtb_row, li, D, into, name, tb=256):
    T = proj.shape[0]
    H = D // _GD
    tb = _pick(T, tb, 8)
    bac = (8 * D) // _LANE

    def body(q_ref, k_ref, v_ref, qh_ref, kh_ref, vh_ref, ba_ref, dq_ref, dk_ref, dv_ref, dbg_ref,
             cw_ref, al_ref, dtb_ref, _, dc_ref, dba_ref, dcw_ref, dal_ref, ddt_ref):
        i = pl.program_id(0)
        first = i == 0

        @pl.when(first)
        def _():
            dcw_ref[...] = jnp.zeros_like(dcw_ref)
            dal_ref[...] = jnp.zeros_like(dal_ref)
            ddt_ref[...] = jnp.zeros_like(ddt_ref)

        cw_ = cw_ref[...]
        for idx, (cur, halo, dref) in enumerate(((q_ref, qh_ref, dq_ref), (k_ref, kh_ref, dk_ref),
                                                  (v_ref, vh_ref, dv_ref))):
            taps = _conv_taps(halo[...], cur[...], first)
            w = cw_[:, idx * D:(idx + 1) * D]
            cv = taps[0] * w[0:1, :]
            for j in range(1, _KC):
                cv = cv + taps[j] * w[j:j + 1, :]
            dact = dref[...]
            act, dact_dcv = _silu_and_grad(cv)
            if idx < 2:
                outs, rs = _l2_heads(act, H)
                pieces = []
                for h in range(H):
                    dy = dact[:, h * _GD:(h + 1) * _GD]
                    pieces.append(rs[h] * (dy - outs[h] * jnp.sum(dy * outs[h], axis=1, keepdims=True)))
                dact = jnp.concatenate(pieces, axis=1)
            dcv = dact * dact_dcv
            dc_ref[:, idx * D:(idx + 1) * D] = dcv
            for j in range(_KC):
                colsum = _dot(jnp.ones((8, tb), _F32), dcv * taps[j])
                dcw_ref[j:j + 1, idx * D:(idx + 1) * D] += colsum[0:1, :]

        ba = ba_ref[...]
        lane, beta, g = _gate_rows(ba, al_ref[...], dtb_ref[...], H)
        dbg_ = dbg_ref[...]
        is_b, is_a = lane < H, jnp.logical_and(lane >= H, lane < 2 * H)
        da = dbg_ * (-jnp.exp(al_ref[...])) * _sigmoid(ba + dtb_ref[...])
        dba_ref[...] = jnp.where(is_b, dbg_ * beta * (1.0 - beta), jnp.where(is_a, da, 0.0)).astype(dba_ref.dtype)
        dal_ref[...] += jnp.sum(jnp.where(is_a, dbg_ * g, 0.0), axis=0, keepdims=True)
        ddt_ref[...] += jnp.sum(jnp.where(is_a, da, 0.0), axis=0, keepdims=True)

    return pl.pallas_call(
        body, name=name, grid=(T // tb,),
        in_specs=[_rspec(tb, D, 2), _rspec(tb, D, 3), _rspec(tb, D, 4),
                  _prev_spec(tb, D, 2), _prev_spec(tb, D, 3), _prev_spec(tb, D, 4),
                  _rspec(tb, _LANE, bac), _rspec(tb, D), _rspec(tb, D), _rspec(tb, D), _rspec(tb, _LANE),
                  _lspec((_KC, 3 * D), li), _lspec((1, _LANE), li), _lspec((1, _LANE), li), _HBM],
        out_specs=[_rspec(tb, 3 * D), _rspec(tb, _LANE, bac), _fspec((_KC, 3 * D)), _fspec((1, _LANE)),
                   _fspec((1, _LANE))],
        out_shape=[jax.ShapeDtypeStruct((T, 3 * D), _F32), jax.ShapeDtypeStruct(into.shape, into.dtype),
                   jax.ShapeDtypeStruct((_KC, 3 * D), _F32), jax.ShapeDtypeStruct((1, _LANE), _F32),
                   jax.ShapeDtypeStruct((1, _LANE), _F32)],
        input_output_aliases={14: 1}, compiler_params=_cp("arbitrary"),
    )(proj, proj, proj, proj, proj, proj, proj, dqn, dkn, dvs, dbg, cw, alog_row, dtb_row, into)


def _conv_bwd2(dc, cw, li, into, name, tb=256):
    T, W3 = dc.shape
    W = W3 // 3
    tb = _pick(T, tb, 8)
    nb8 = T // 8
    nrow = T // tb

    def body(dc_ref, nx_ref, cw_ref, _, o_ref):
        last = pl.program_id(0) == nrow - 1
        full = jnp.concatenate([dc_ref[...], jnp.where(last, 0.0, nx_ref[...])], axis=0)
        w = cw_ref[...]
        acc = full[:tb] * w[_KC - 1:_KC, :]
        for j in range(_KC - 1):
            sh = _KC - 1 - j
            acc = acc + pltpu.roll(full, tb + 8 - sh, 0)[:tb] * w[j:j + 1, :]
        o_ref[...] = acc.astype(o_ref.dtype)

    return pl.pallas_call(
        body, name=name, grid=(nrow, 3),
        in_specs=[pl.BlockSpec((tb, W), lambda i, j: (i, j)),
                  pl.BlockSpec((8, W), lambda i, j: (jnp.minimum((i + 1) * (tb // 8), nb8 - 1), j)),
                  pl.BlockSpec((None, _KC, W), lambda i, j: (li, 0, j)), _HBM],
        out_specs=pl.BlockSpec((tb, W), lambda i, j: (i, 2 + j)),
        out_shape=jax.ShapeDtypeStruct(into.shape, into.dtype), input_output_aliases={3: 0},
        compiler_params=_cp("parallel", "parallel"),
    )(dc, dc, cw, into)


def _split(a):
    hi = a.astype(_BF)
    return hi, (a - hi.astype(_F32)).astype(_BF)


def _dot3(a, b):
    (ah, al), (bh, bl) = a, b
    f = functools.partial(lax.dot_general, dimension_numbers=_NN, preferred_element_type=_F32)
    return f(ah, bh) + f(ah, bl) + f(al, bh)


def _inv_unit_lower(mats):
    C = mats[0].shape[0]
    ii = lax.broadcasted_iota(jnp.int32, (C, C), 0)
    jj = lax.broadcasted_iota(jnp.int32, (C, C), 1)
    xs = [jnp.where(ii == jj, 1.0, 0.0) - a for a in mats]
    ps = list(mats)
    n = 1
    while 2 * n < C:
        sp = [_split(p) for p in ps]
        ps = [_dot3(s, s) for s in sp]
        sp = [_split(p) for p in ps]
        xs = [x + _dot3(_split(x), s) for x, s in zip(xs, sp)]
        n *= 2
    return xs


def _gdn_chunk(q, k, v, g_row, b_row):
    C = q.shape[0]
    ii = lax.broadcasted_iota(jnp.int32, (C, C), 0)
    jj = lax.broadcasted_iota(jnp.int32, (C, C), 1)
    low, strict, eye = jj <= ii, jj < ii, ii == jj
    g_col = jnp.sum(jnp.where(eye, g_row, 0.0), axis=1, keepdims=True)
    b_col = jnp.sum(jnp.where(eye, b_row, 0.0), axis=1, keepdims=True)
    gam_col = jnp.sum(jnp.where(low, g_row, 0.0), axis=1, keepdims=True)
    gam_row = jnp.sum(jnp.where(jj >= ii, g_col, 0.0), axis=0, keepdims=True)
    gam_last = jnp.sum(g_row, axis=1, keepdims=True)
    decay = jnp.where(low, jnp.exp(jnp.where(low, gam_col - gam_row, 0.0)), 0.0)
    eg = jnp.exp(gam_col)
    ekd = jnp.exp(gam_last - gam_col)
    qs = q * (_GD ** -0.5)
    kb = k * b_col
    kk = _dot(kb, k, _NT)
    qkraw = _dot(qs, k, _NT)
    return dict(low=low, strict=strict, eye=eye, ii=ii, jj=jj, b_col=b_col, decay=decay, eg=eg, ekd=ekd,
                gl=jnp.exp(gam_last), qs=qs, kb=kb, kk=kk, qkraw=qkraw,
                A=jnp.where(strict, kk * decay, 0.0), vb=v * b_col, kbg=kb * eg,
                qk=qkraw * decay, q_dec=qs * eg, k_dec=k * ekd)


def _gdn_fwd(qn, kn, vs, g_r, b_r, name):
    T, D = qn.shape
    H, N, C = D // _GD, T // _BC, _BC
    hb = min(_HB, H)

    def body(q_ref, k_ref, v_ref, g_ref, b_ref, o_ref, s_ref, t_ref, S):
        @pl.when(pl.program_id(1) == 0)
        def _():
            S[...] = jnp.zeros_like(S)

        hs = range(hb)
        sls = [slice(hh * _GD, (hh + 1) * _GD) for hh in hs]
        cms = [_gdn_chunk(q_ref[:, sl], k_ref[:, sl], v_ref[:, sl], g_ref[hh], b_ref[hh]) for hh, sl in zip(hs, sls)]
        tms = _inv_unit_lower([cm["A"] for cm in cms])
        us = [_dot(tm, cm["vb"]) for tm, cm in zip(tms, cms)]
        ws = [_dot(tm, cm["kbg"]) for tm, cm in zip(tms, cms)]
        s0s = [S[hh] for hh in hs]
        for hh in hs:
            s_ref[hh] = s0s[hh]
            t_ref[hh] = tms[hh]
        v_news = [u - _dot(w, s0) for u, w, s0 in zip(us, ws, s0s)]
        qss = [_dot(cm["q_dec"], s0) for cm, s0 in zip(cms, s0s)]
        for hh in hs:
            o_ref[:, sls[hh]] = qss[hh] + _dot(cms[hh]["qk"], v_news[hh])
        for hh in hs:
            S[hh] = s0s[hh] * cms[hh]["gl"] + _dot(cms[hh]["k_dec"], v_news[hh], _TN)

    qspec = pl.BlockSpec((C, hb * _GD), lambda h, n: (n, h))
    gspec = pl.BlockSpec((hb, None, 1, C), lambda h, n: (h, n, 0, 0))
    return pl.pallas_call(
        body, name=name, grid=(H // hb, N),
        in_specs=[qspec, qspec, qspec, gspec, gspec],
        out_specs=[qspec, pl.BlockSpec((hb, None, _GD, _GD), lambda h, n: (h, n, 0, 0)),
                   pl.BlockSpec((hb, None, C, C), lambda h, n: (h, n, 0, 0))],
        out_shape=[jax.ShapeDtypeStruct((T, D), _F32), jax.ShapeDtypeStruct((H, N, _GD, _GD), _F32),
                   jax.ShapeDtypeStruct((H, N, C, C), _F32)],
        scratch_shapes=[pltpu.VMEM((hb, _GD, _GD), _F32)],
        compiler_params=_cp("arbitrary", "arbitrary"),
    )(qn, kn, vs, g_r, b_r)


def _gdn_bwd(qn, kn, vs, g_r, b_r, s_all, t_all, do, name):
    T, D = qn.shape
    H, N, C = D // _GD, T // _BC, _BC
    hb = min(_HB, H)

    def body(q_ref, k_ref, v_ref, g_ref, b_ref, s_ref, t_ref, do_ref, dq_ref, dk_ref, dv_ref, dg_ref, db_ref, dS):
        @pl.when(pl.program_id(1) == 0)
        def _():
            dS[...] = jnp.zeros_like(dS)

        hs = range(hb)
        sls = [slice(hh * _GD, (hh + 1) * _GD) for hh in hs]
        ks = [k_ref[:, sl] for sl in sls]
        vs_ = [v_ref[:, sl] for sl in sls]
        cms = [_gdn_chunk(q_ref[:, sl], k, v, g_ref[hh], b_ref[hh]) for hh, sl, k, v in zip(hs, sls, ks, vs_)]
        low, strict, eye, ii, jj = (cms[0][n] for n in ("low", "strict", "eye", "ii", "jj"))
        tms, s0s, dos, ds1s = [t_ref[hh] for hh in hs], [s_ref[hh] for hh in hs], [do_ref[:, sl] for sl in sls], \
            [dS[hh] for hh in hs]
        us = [_dot(tm, cm["vb"]) for tm, cm in zip(tms, cms)]
        ws = [_dot(tm, cm["kbg"]) for tm, cm in zip(tms, cms)]
        v_news = [u - _dot(w, s0) for u, w, s0 in zip(us, ws, s0s)]
        dv_news = [_dot(cm["qk"], do_, _TN) + _dot(cm["k_dec"], ds1) for cm, do_, ds1 in zip(cms, dos, ds1s)]
        dqks = [jnp.where(low, _dot(do_, vn, _NT), 0.0) for do_, vn in zip(dos, v_news)]
        dq_decs = [_dot(do_, s0, _NT) for do_, s0 in zip(dos, s0s)]
        dk_decs = [_dot(vn, ds1, _NT) for vn, ds1 in zip(v_news, ds1s)]
        dgls = [jnp.sum(jnp.sum(ds1 * s0, axis=1, keepdims=True), axis=0, keepdims=True) for ds1, s0 in zip(ds1s, s0s)]
        dws = [-_dot(dvn, s0, _NT) for dvn, s0 in zip(dv_news, s0s)]
        for hh in hs:
            dS[hh] = (_dot(cms[hh]["q_dec"], dos[hh], _TN) + cms[hh]["gl"] * ds1s[hh]
                      - _dot(ws[hh], dv_news[hh], _TN))
        dvbs = [_dot(tm, dvn, _TN) for tm, dvn in zip(tms, dv_news)]
        dkbgs = [_dot(tm, dw, _TN) for tm, dw in zip(tms, dws)]
        dAs = [-jnp.where(strict, _dot(dvb, u, _NT) + _dot(dkbg, w, _NT), 0.0)
               for dvb, u, dkbg, w in zip(dvbs, us, dkbgs, ws)]
        dkks = [dA * cm["decay"] for dA, cm in zip(dAs, cms)]
        dqkraws = [dqk * cm["decay"] for dqk, cm in zip(dqks, cms)]
        Es = [(dA * cm["kk"] + dqk * cm["qkraw"]) * cm["decay"] for dA, dqk, cm in zip(dAs, dqks, cms)]
        dkbs = [_dot(dkk, k) + dkbg * cm["eg"] for dkk, k, dkbg, cm in zip(dkks, ks, dkbgs, cms)]
        dqss = [_dot(dqr, k) + dqd * cm["eg"] for dqr, k, dqd, cm in zip(dqkraws, ks, dq_decs, cms)]
        for hh in hs:
            cm = cms[hh]
            dk_ref[:, sls[hh]] = (_dot(dqkraws[hh], cm["qs"], _TN) + _dot(dkks[hh], cm["kb"], _TN)
                                  + dk_decs[hh] * cm["ekd"] + dkbs[hh] * cm["b_col"])
            dv_ref[:, sls[hh]] = dvbs[hh] * cm["b_col"]
            dq_ref[:, sls[hh]] = dqss[hh] * (_GD ** -0.5)
        for hh in hs:
            cm, k, E = cms[hh], ks[hh], Es[hh]
            eg, ekd = cm["eg"], cm["ekd"]
            dbeta_col = jnp.sum(dvbs[hh] * vs_[hh] + dkbs[hh] * k, axis=1, keepdims=True)
            t_kd = jnp.sum(dk_decs[hh] * k, axis=1, keepdims=True) * ekd
            c1 = (jnp.sum(E, axis=1, keepdims=True) + jnp.sum(dkbgs[hh] * cm["kb"], axis=1, keepdims=True) * eg
                  + jnp.sum(dq_decs[hh] * cm["qs"], axis=1, keepdims=True) * eg - t_kd)
            r1 = jnp.sum(E, axis=0, keepdims=True)
            dgam_last = jnp.sum(t_kd, axis=0, keepdims=True) + dgls[hh] * cm["gl"]
            dgam_col = c1 - jnp.sum(jnp.where(eye, r1, 0.0), axis=1, keepdims=True)
            dg_ref[hh] = jnp.sum(jnp.where(ii >= jj, dgam_col, 0.0), axis=0, keepdims=True) + dgam_last
            db_ref[hh] = jnp.sum(jnp.where(eye, dbeta_col, 0.0), axis=0, keepdims=True)

    qspec = pl.BlockSpec((C, hb * _GD), lambda h, n: (N - 1 - n, h))
    gspec = pl.BlockSpec((hb, None, 1, C), lambda h, n: (h, N - 1 - n, 0, 0))
    return pl.pallas_call(
        body, name=name, grid=(H // hb, N),
        in_specs=[qspec, qspec, qspec, gspec, gspec,
                  pl.BlockSpec((hb, None, _GD, _GD), lambda h, n: (h, N - 1 - n, 0, 0)),
                  pl.BlockSpec((hb, None, C, C), lambda h, n: (h, N - 1 - n, 0, 0)), qspec],
        out_specs=[qspec, qspec, qspec, gspec, gspec],
        out_shape=[jax.ShapeDtypeStruct((T, D), _F32)] * 3 + [jax.ShapeDtypeStruct((H, N, 1, C), _F32)] * 2,
        scratch_shapes=[pltpu.VMEM((hb, _GD, _GD), _F32)],
        compiler_params=_cp("arbitrary", "arbitrary"),
    )(qn, kn, vs, g_r, b_r, s_all, t_all, do)


def _onorm_fwd(o, proj, go, li, D, name, tb=512):
    T = o.shape[0]
    H = D // _GD
    tb = _pick(T, tb, 8)

    def body(o_ref, z_ref, go_ref, y_ref):
        ov, zv, g = o_ref[...], z_ref[...], go_ref[...]
        for h in range(H):
            sl = slice(h * _GD, (h + 1) * _GD)
            oh = ov[:, sl]
            r = lax.rsqrt(jnp.mean(oh * oh, axis=1, keepdims=True) + _EPS)
            y_ref[:, sl] = (oh * r * g * _silu(zv[:, sl])).astype(y_ref.dtype)

    return pl.pallas_call(
        body, name=name, grid=(T // tb,), in_specs=[_rspec(tb, D), _rspec(tb, D, 5), _lspec((1, _GD), li)],
        out_specs=_rspec(tb, D), out_shape=jax.ShapeDtypeStruct((T, D), _MMT), compiler_params=_cp("parallel"),
    )(o, proj, go)


def _onorm_bwd(dy, o, proj, go, li, D, into, name, tb=256):
    T = o.shape[0]
    H = D // _GD
    tb = _pick(T, tb, 8)

    def body(dy_ref, o_ref, z_ref, go_ref, _, do_ref, dz_ref, dgo_ref):
        @pl.when(pl.program_id(0) == 0)
        def _():
            dgo_ref[...] = jnp.zeros_like(dgo_ref)

        dyv, ov, zv, g = dy_ref[...].astype(_F32), o_ref[...], z_ref[...], go_ref[...]
        dgo = jnp.zeros((1, _GD), _F32)
        for h in range(H):
            sl = slice(h * _GD, (h + 1) * _GD)
            oh, zh, dyh = ov[:, sl], zv[:, sl], dyv[:, sl]
            r = lax.rsqrt(jnp.mean(oh * oh, axis=1, keepdims=True) + _EPS)
            on = oh * r
            sz, dsz = _silu_and_grad(zh)
            dgo = dgo + jnp.sum(dyh * sz * on, axis=0, keepdims=True)
            don = dyh * sz * g
            do_ref[:, sl] = r * (don - on * jnp.mean(don * on, axis=1, keepdims=True))
            dz_ref[:, sl] = (dyh * on * g * dsz).astype(dz_ref.dtype)
        dgo_ref[...] += dgo

    return pl.pallas_call(
        body, name=name, grid=(T // tb,),
        in_specs=[_rspec(tb, D), _rspec(tb, D), _rspec(tb, D, 5), _lspec((1, _GD), li), _HBM],
        out_specs=[_rspec(tb, D), _rspec(tb, D, 5), _fspec((1, _GD))],
        out_shape=[jax.ShapeDtypeStruct((T, D), _F32), jax.ShapeDtypeStruct(into.shape, into.dtype),
                   jax.ShapeDtypeStruct((1, _GD), _F32)],
        input_output_aliases={4: 1}, compiler_params=_cp("arbitrary"),
    )(dy, o, proj, go, into)


def _merge_bwd(dm, pa, pb, proj, D, name, tb=256):
    T, PW = proj.shape
    tb = _pick(T, tb, 8)

    def body(dm_ref, pa_ref, pb_ref, ga_ref, gb_ref, dpa_ref, dpb_ref, dg_ref):
        d = dm_ref[...].astype(_F32)
        sa, sb = _sigmoid(ga_ref[...]), _sigmoid(gb_ref[...])
        dpa_ref[...] = (d * sa).astype(dpa_ref.dtype)
        dpb_ref[...] = (d * sb).astype(dpb_ref.dtype)
        dg_ref[:, :D] = (d * pa_ref[...].astype(_F32) * sa * (1.0 - sa)).astype(dg_ref.dtype)
        dg_ref[:, D:] = (d * pb_ref[...].astype(_F32) * sb * (1.0 - sb)).astype(dg_ref.dtype)

    return pl.pallas_call(
        body, name=name, grid=(T // tb,),
        in_specs=[_rspec(tb, D), _rspec(tb, D), _rspec(tb, D), _rspec(tb, D, 6), _rspec(tb, D, 7)],
        out_specs=[_rspec(tb, D), _rspec(tb, D), _rspec(tb, 2 * D, 3)],
        out_shape=[jax.ShapeDtypeStruct((T, D), _MMT)] * 2 + [jax.ShapeDtypeStruct((T, PW), _MMT)],
        compiler_params=_cp("parallel"),
    )(dm, pa, pb, proj, proj)


def _loss_head(x, tgt, fg, name, tb=256):
    T, D = x.shape
    tb = _pick(T, tb, 8)

    def body(x_ref, t_ref, fg_ref, loss_ref, dx_ref, dfg_ref):
        @pl.when(pl.program_id(0) == 0)
        def _():
            loss_ref[...] = jnp.zeros_like(loss_ref)
            dfg_ref[...] = jnp.zeros_like(dfg_ref)

        xv, fg_ = x_ref[...], fg_ref[...]
        r = lax.rsqrt(jnp.mean(xv * xv, axis=1, keepdims=True) + _EPS)
        xn = xv * r
        e = xn * fg_ - t_ref[...]
        loss_ref[...] += (0.5 / D) * jnp.sum(jnp.sum(e * e, axis=1, keepdims=True), axis=0, keepdims=True)
        dy = e * (1.0 / D)
        dfg_ref[...] += jnp.sum(dy * xn, axis=0, keepdims=True)
        dxn = dy * fg_
        dx_ref[...] = r * (dxn - xn * jnp.mean(dxn * xn, axis=1, keepdims=True))

    return pl.pallas_call(
        body, name=name, grid=(T // tb,), in_specs=[_rspec(tb, D), _rspec(tb, D), _fspec((1, D))],
        out_specs=[_fspec((1, 1)), _rspec(tb, D), _fspec((1, D))],
        out_shape=[jax.ShapeDtypeStruct((1, 1), _F32), jax.ShapeDtypeStruct((T, D), _F32),
                   jax.ShapeDtypeStruct((1, D), _F32)],
        compiler_params=_cp("arbitrary"),
    )(x, tgt, fg)


def _row_tile(R, W, budget=1 << 20, unit=8):
    if R * W * 4 <= budget or R % unit:
        return R
    best = unit
    for t in range(unit, R + 1, unit):
        if R % t == 0 and t * W * 4 <= budget:
            best = t
    return best


def _add_own_half(send, got, half, name):
    P, Rp, W = send.shape
    Rh = Rp // 2
    tb = _row_tile(Rh, W, 1 << 21, 16)

    def body(h_ref, a_ref, b_ref, o_ref):
        o_ref[...] = (a_ref[...].astype(_F32) + b_ref[...].astype(_F32)).astype(o_ref.dtype)

    return pl.pallas_call(
        body, name=name,
        grid_spec=pltpu.PrefetchScalarGridSpec(
            num_scalar_prefetch=1, grid=(P, Rh // tb),
            in_specs=[pl.BlockSpec((None, None, tb, W), lambda k, i, h: (k, h[0], i, 0)),
                      pl.BlockSpec((None, tb, W), lambda k, i, h: (k, i, 0))],
            out_specs=pl.BlockSpec((None, tb, W), lambda k, i, h: (k, i, 0))),
        out_shape=jax.ShapeDtypeStruct((P, Rh, W), send.dtype), compiler_params=_cp("parallel", "parallel"),
    )(half, send.reshape(P, 2, Rh, W), got)


def _sum_slots(st, name):
    P, R, W = st.shape
    tb = _row_tile(R, W, 1 << 20, 16)

    def body(s_ref, o_ref):
        acc = s_ref[0].astype(_F32)
        for p in range(1, P):
            acc = acc + s_ref[p].astype(_F32)
        o_ref[...] = acc

    return pl.pallas_call(
        body, name=name, grid=(R // tb,), in_specs=[pl.BlockSpec((P, tb, W), lambda i: (0, i, 0))],
        out_specs=_rspec(tb, W), out_shape=jax.ShapeDtypeStruct((R, W), _F32), compiler_params=_cp("parallel"),
    )(st)


def _adamw(w, gst, m, v, name):
    R, W = w.shape
    P = gst.shape[0]
    tb = _row_tile(R, W, 1 << 21)
    c1, c2 = 1.0 - _B1 ** _STEP, 1.0 - _B2 ** _STEP

    def body(w_ref, g_ref, m_ref, v_ref, go_ref, d_ref, mo_ref, vo_ref):
        g = g_ref[0]
        for p in range(1, P):
            g = g + g_ref[p]
        mn = _B1 * m_ref[...] + (1.0 - _B1) * g
        vn = _B2 * v_ref[...] + (1.0 - _B2) * (g * g)
        go_ref[...] = g
        mo_ref[...] = mn
        vo_ref[...] = vn
        d_ref[...] = -_LR * ((mn / c1) / (jnp.sqrt(vn / c2) + _AEPS) + _WD * w_ref[...])

    spec = _rspec(tb, W)
    return pl.pallas_call(
        body, name=name, grid=(R // tb,),
        in_specs=[spec, pl.BlockSpec((P, tb, W), lambda i: (0, i, 0)), spec, spec],
        out_specs=[spec] * 4, out_shape=[jax.ShapeDtypeStruct((R, W), _F32)] * 4, compiler_params=_cp("parallel"),
    )(w, gst, m, v)


def _as2d(a):
    if a.ndim == 1:
        return a.reshape(1, -1)
    return a.reshape(-1, a.shape[-1])


def kernel(x, c, ada_w, ada_b, norm1_g, w_in, conv_w, spatial_w, spatial_b, v_norm_g, a_log, dt_bias, o_norm_g, w_branch_a, w_branch_b, w_out, norm2_g, w_ffn_in, w_ffn_out, final_g, loss_target, m_ada_w, m_ada_b, m_norm1_g, m_w_in, m_conv_w, m_spatial_w, m_spatial_b, m_v_norm_g, m_a_log, m_dt_bias, m_o_norm_g, m_w_branch_a, m_w_branch_b, m_w_out, m_norm2_g, m_w_ffn_in, m_w_ffn_out, m_final_g, v_ada_w, v_ada_b, v_norm1_g, v_w_in, v_conv_w, v_spatial_w, v_spatial_b, v_v_norm_g, v_a_log, v_dt_bias, v_o_norm_g, v_w_branch_a, v_w_branch_b, v_w_out, v_norm2_g, v_w_ffn_in, v_w_ffn_out, v_final_g):
    xb, tgt = x[0], loss_target[0]
    T, D = xb.shape
    L, H, G = ada_w.shape[0], a_log.shape[1], spatial_w.shape[1]
    F = 4 * w_ffn_out.shape[1]
    N = T // _BC
    Ws = ada_w.shape[2]
    Wc = w_in.shape[2]
    PW = 8 * D + _LANE
    ix, iy, ic = lax.axis_index("x"), lax.axis_index("y"), lax.axis_index("c")
    me = 4 * ix + 2 * iy + ic

    c_all = _gather8(c, "gather_c").reshape(8, D)
    modp = _ada_fwd(c_all, ada_w, "ada_fwd")
    n_mod, n_cw = L * 8 * Ws, L * _KC * conv_w.shape[2]
    pad = (-(n_mod + n_cw)) % _LANE
    pay = jnp.concatenate([modp.reshape(-1), conv_w.reshape(-1), jnp.zeros((pad,), _F32)]).reshape(-1, _LANE)
    pay_all = _gather8(pay, "gather_mod").reshape(8, -1)
    mod_full = jnp.concatenate([pay_all[2 * k, :n_mod].reshape(L, 8, Ws) for k in range(4)], axis=-1)
    cw_full = jnp.concatenate([pay_all[2 * k, n_mod:n_mod + n_cw].reshape(L, _KC, -1) for k in range(4)], axis=-1)
    mod = lax.dynamic_index_in_dim(mod_full, me, axis=1, keepdims=False) + ada_b
    mods = [[mod[l, j * D:(j + 1) * D].reshape(1, D) for j in range(6)] for l in range(L)]

    big = [w_in, w_branch_a, w_branch_b, w_out, w_ffn_in, w_ffn_out]
    chip = 2 * ix + iy
    starts = [(k * Wc) // 16 * 16 for k in range(4)]
    Hh = max(-(-((k + 1) * Wc) // 16) * 16 - starts[k] for k in range(4))
    No = max(s + Hh for s in starts)
    my_off = jnp.asarray([k * Wc - starts[k] for k in range(4)], jnp.int32)[chip]
    cuts = sorted(set(starts + [s + Hh for s in starts]))

    pers = [Hh, D // 4, D // 4, D // 4, 2 * F // 4, F // 4]
    roff = [0]
    for p in pers:
        roff.append(roff[-1] + L * p)
    Rp = -(-roff[-1] // (32 * _NCH)) * (32 * _NCH)
    rpad = Rp - roff[-1]

    hull = lax.dynamic_update_slice(jnp.zeros((L, Hh, D), _F32), jnp.transpose(w_in, (0, 2, 1)), (0, my_off, 0))
    shard = jnp.concatenate(
        [hull.reshape(-1, D).astype(_MMT), w_branch_a.reshape(-1, D).astype(_MMT),
         w_branch_b.reshape(-1, D).astype(_MMT), w_out.reshape(-1, D).astype(_MMT),
         jnp.transpose(w_ffn_in, (0, 2, 1)).reshape(-1, D).astype(_MMT), w_ffn_out.reshape(-1, D).astype(_MMT),
         jnp.zeros((rpad, D), _MMT)], axis=0)
    gw = _fill_from_sibling(_gather_chips(shard, "gather_w"), "gather_w_sib")

    def slab(i, l, k):
        a = roff[i] + l * pers[i]
        return gw[k, a:a + pers[i]]

    def joined(i, l):
        return jnp.concatenate([slab(i, l, k) for k in range(4)], axis=0)

    def orig_rows(hulls, a, b):
        edges = sorted(set([a, b] + [c_ for c_ in cuts if a < c_ < b]))
        out = []
        for lo, hi in zip(edges[:-1], edges[1:]):
            cov = [k for k in range(4) if starts[k] <= lo and hi <= starts[k] + Hh]
            piece = hulls[cov[0]][lo - starts[cov[0]]:hi - starts[cov[0]]]
            for k in cov[1:]:
                piece = piece + hulls[k][lo - starts[k]:hi - starts[k]]
            out.append(piece)
        return out

    wt_in_p = []
    for l in range(L):
        hulls = [slab(0, l, k) for k in range(4)]
        wt_in_p.append(jnp.concatenate(
            orig_rows(hulls, 0, 6 * D) + orig_rows(hulls, 6 * D + 2 * H, 8 * D + 2 * H)
            + orig_rows(hulls, 6 * D, 6 * D + 2 * H) + [jnp.zeros((_LANE - 2 * H, D), _MMT)], axis=0))
    w_a, w_b, w_o, wt_fi, w_fo = ([joined(i, l) for l in range(L)] for i in range(1, 6))

    sbt = jnp.transpose(spatial_b, (0, 2, 1))
    gv3, go3 = v_norm_g.reshape(L, 1, D), o_norm_g.reshape(L, 1, _GD)
    zpad = jnp.zeros((L, _LANE - 2 * H), _F32)
    alog_row = jnp.concatenate([jnp.zeros((L, H), _F32), a_log, zpad], axis=1).reshape(L, 1, _LANE)
    dtb_row = jnp.concatenate([jnp.zeros((L, H), _F32), dt_bias, zpad], axis=1).reshape(L, 1, _LANE)

    def rows_of(tok):
        return jnp.transpose(tok.reshape(N, _BC, H), (2, 0, 1)).reshape(H, N, 1, _BC)

    def toks_of(rows):
        return jnp.transpose(rows.reshape(H, N, _BC), (1, 2, 0)).reshape(T, H)

    saved = []
    xc = xb
    for l in range(L):
        sh1, sc1, gt1, sh2, sc2, gt2 = mods[l]
        g1, g2 = norm1_g[l].reshape(1, D), norm2_g[l].reshape(1, D)
        h = _norm_mod(xc, g1, sc1, sh1, f"norm1_{l}")
        proj = _mm_nt(h, wt_in_p[l], f"proj_{l}", tn=1664)
        ya = _gmlp_fwd(proj, spatial_w, sbt, gv3, l, D, f"gmlp_{l}")
        qn, kn, vs, bg = _conv_fwd(proj, cw_full, alog_row, dtb_row, l, D, f"conv_{l}")
        g_r, b_r = rows_of(bg[:, H:2 * H]), rows_of(bg[:, :H])
        o, s_all, t_all = _gdn_fwd(qn, kn, vs, g_r, b_r, f"gdn_{l}")
        yb = _onorm_fwd(o, proj, go3, l, D, f"onorm_{l}")
        pa, pb, mg = _branch_merge(ya, yb, w_a[l], w_b[l], proj, f"branch_{l}")
        p1, x1 = _mm_res(mg, w_o[l], xc, gt1, f"wout_{l}")
        h2 = _norm_mod(x1, g2, sc2, sh2, f"norm2_{l}")
        gate, up, act = _ffin_swiglu(h2, wt_fi[l], f"ffin_{l}")
        p2, x2 = _mm_res(act, w_fo[l], x1, gt2, f"ffout_{l}")
        saved.append(dict(x=xc, h=h, proj=proj, ya=ya, yb=yb, qn=qn, kn=kn, vs=vs, g_r=g_r, b_r=b_r, o=o,
                          s_all=s_all, t_all=t_all, pa=pa, pb=pb, mg=mg, p1=p1, x1=x1, h2=h2, gate=gate, up=up,
                          act=act, p2=p2))
        xc = x2

    loss11, dx, dfg = _loss_head(xc, tgt, final_g.reshape(1, D), "loss_head")
    loss = lax.psum(loss11[0, 0], ("x", "y", "c"))

    gbig = {k: None for k in ("w_in", "w_a", "w_b", "w_o", "w_fi", "w_fo")}
    small = {k: [None] * L for k in ("dmod", "n1", "n2", "sw", "sb", "gv", "cw", "al", "dt", "go")}
    for l in reversed(range(L)):
        sv = saved[l]
        sh1, sc1, gt1, sh2, sc2, gt2 = mods[l]
        g1, g2 = norm1_g[l].reshape(1, D), norm2_g[l].reshape(1, D)
        proj = sv["proj"]
        dp2, dgt2 = _resid_bwd(dx, sv["p2"], gt2, f"res2b_{l}")
        dgate, dup = _ffoutb_swiglu(dp2, w_fo[l], sv["gate"], sv["up"], f"ffoutb_{l}")
        gbig["w_fo"] = _mm_tn(sv["act"], dp2, f"ffoutw_{l}", l, L, gbig["w_fo"], tm=1408)
        dx1, dgm2, dsh2 = _mm_normb((dgate, dup), wt_fi[l], sv["x1"], dx, g2, sc2, f"ffinb_{l}")
        gbig["w_fi"] = _mm_tn(dgate, sv["h2"], f"ffinwg_{l}", l, L, gbig["w_fi"], mtot=2 * F, tm=1408)
        gbig["w_fi"] = _mm_tn(dup, sv["h2"], f"ffinwu_{l}", l, L, gbig["w_fi"], row0=F, mtot=2 * F, tm=1408)
        dp1, dgt1 = _resid_bwd(dx1, sv["p1"], gt1, f"res1b_{l}")
        dmg = _mm_nt(dp1, w_o[l], f"woutb_{l}", out_dtype=_MMT)
        gbig["w_o"] = _mm_tn(sv["mg"], dp1, f"woutw_{l}", l, L, gbig["w_o"])
        dpa, dpb, dproj = _merge_bwd(dmg, sv["pa"], sv["pb"], proj, D, f"mergeb_{l}")
        dya = _mm_nt(dpa, w_a[l], f"brab_{l}", out_dtype=_MMT)
        gbig["w_a"] = _mm_tn(sv["ya"], dpa, f"braw_{l}", l, L, gbig["w_a"])
        dyb = _mm_nt(dpb, w_b[l], f"brbb_{l}", out_dtype=_MMT)
        gbig["w_b"] = _mm_tn(sv["yb"], dpb, f"brbw_{l}", l, L, gbig["w_b"])
        dproj, dsw, dsa, dgv = _gmlp_bwd(proj, dya, spatial_w, sbt, gv3, l, D, dproj, f"gmlpb_{l}")
        do, dproj, dgo = _onorm_bwd(dyb, sv["o"], proj, go3, l, D, dproj, f"onormb_{l}")
        dqn, dkn, dvs, dg_r, db_r = _gdn_bwd(sv["qn"], sv["kn"], sv["vs"], sv["g_r"], sv["b_r"], sv["s_all"],
                                             sv["t_all"], do, f"gdnb_{l}")
        dbg = jnp.concatenate([toks_of(db_r), toks_of(dg_r), jnp.zeros((T, _LANE - 2 * H), _F32)], axis=1)
        dc, dproj, dcw, dal, ddt = _conv_bwd1(proj, dqn, dkn, dvs, dbg, cw_full, alog_row, dtb_row, l, D, dproj,
                                              f"convb_{l}")
        dproj = _conv_bwd2(dc, cw_full, l, dproj, f"convx_{l}")
        gbig["w_in"] = _mm_tn(dproj, sv["h"], f"projw_{l}", l, L, gbig["w_in"], tm=640)
        dx, dgm1, dsh1 = _mm_normb(dproj, wt_in_p[l], sv["x"], dx1, g1, sc1, f"projb_{l}", tk=1664)
        small["dmod"][l] = jnp.concatenate([dsh1, dgm1 * g1, dgt1, dsh2, dgm2 * g2, dgt2], axis=1)
        small["n1"][l], small["n2"][l] = dgm1 * (1.0 + sc1), dgm2 * (1.0 + sc2)
        small["sw"][l], small["gv"][l], small["cw"][l], small["go"][l] = dsw, dgv, dcw, dgo
        small["sb"][l] = jnp.transpose(dsa.reshape(_AC, G, _GD).sum(axis=-1))
        small["al"][l], small["dt"][l] = dal[:, H:2 * H], ddt[:, H:2 * H]
    grad_x = dx.reshape(1, T, D)

    names_small = ["dmod", "n1", "n2", "sw", "sb", "gv", "cw", "al", "dt", "go"]
    flat = [jnp.stack(small[k]).reshape(-1) for k in names_small] + [dfg.reshape(-1)]
    sizes = [f.shape[0] for f in flat]
    tot = sum(sizes)
    pad = (-tot) % 1024
    pay = jnp.concatenate(flat + [jnp.zeros((pad,), _F32)]).reshape(-1, 1024)
    sm_all = _gather8(pay, "gather_small").reshape(8, -1)
    offs = [0]
    for s in sizes:
        offs.append(offs[-1] + s)
    part = {k: sm_all[:, offs[i]:offs[i + 1]] for i, k in enumerate(names_small + ["fg"])}
    dmod_all = part["dmod"].reshape(8, L, 6 * D)

    outs = {}

    def update(nm, w, gst, m, v):
        shp = w.shape
        w2 = _as2d(w)
        g, d, mn, vn = _adamw(w2, gst.reshape((gst.shape[0],) + w2.shape), _as2d(m), _as2d(v), f"adamw_{nm}")
        outs[nm] = (g.reshape(shp), d.reshape(shp), mn.reshape(shp), vn.reshape(shp))

    chip = 2 * ix + iy
    dmod_t = jnp.transpose(dmod_all, (1, 0, 2))
    dmod_mine = lax.dynamic_slice_in_dim(dmod_t, chip * Ws, Ws, axis=2)
    g_ada_w = _ada_bwd(jnp.transpose(c_all), dmod_mine, "ada_bwd")
    update("ada_w", ada_w, g_ada_w[None], m_ada_w, v_ada_w)
    update("ada_b", ada_b, dmod_all, m_ada_b, v_ada_b)
    update("norm1_g", norm1_g, part["n1"], m_norm1_g, v_norm1_g)
    update("norm2_g", norm2_g, part["n2"], m_norm2_g, v_norm2_g)
    update("spatial_w", spatial_w, part["sw"], m_spatial_w, v_spatial_w)
    update("spatial_b", spatial_b, part["sb"], m_spatial_b, v_spatial_b)
    update("v_norm_g", v_norm_g, part["gv"], m_v_norm_g, v_v_norm_g)
    update("a_log", a_log, part["al"], m_a_log, v_a_log)
    update("dt_bias", dt_bias, part["dt"], m_dt_bias, v_dt_bias)
    update("o_norm_g", o_norm_g, part["go"], m_o_norm_g, v_o_norm_g)
    update("final_g", final_g, part["fg"], m_final_g, v_final_g)
    cw_cols = conv_w.shape[2]
    dcw_all = part["cw"].reshape(8, L, _KC, 4 * cw_cols)
    update("conv_w", conv_w, lax.dynamic_slice_in_dim(dcw_all, chip * cw_cols, cw_cols, axis=3), m_conv_w, v_conv_w)

    def hull_of(p, k):
        a, b = starts[k], starts[k] + Hh
        out = []
        for lo, hi, plo in ((0, 6 * D, 0), (6 * D, 6 * D + 2 * H, 8 * D), (6 * D + 2 * H, 8 * D + 2 * H, 6 * D),
                            (8 * D + 2 * H, No, None)):
            s, e = max(a, lo), min(b, hi)
            if s < e:
                out.append(jnp.zeros((L, e - s, D), _MMT) if plo is None else p[:, plo + s - lo:plo + e - lo])
        return (out[0] if len(out) == 1 else jnp.concatenate(out, axis=1)).reshape(L * Hh, D)

    pieces = []
    for k in range(4):
        pieces.append(hull_of(gbig["w_in"], k))
        for i, nm in enumerate(("w_a", "w_b", "w_o", "w_fi", "w_fo")):
            per = pers[i + 1]
            pieces.append(gbig[nm][:, k * per:(k + 1) * per].reshape(L * per, D))
        pieces.append(jnp.zeros((rpad, D), _MMT))
    send = jnp.concatenate(pieces, axis=0).reshape(4, Rp, D)
    got = _send_half_to_sibling(send, "reduce_cores")
    chipsum = _add_own_half(send, got, ic.astype(jnp.int32).reshape(1), "add_cores")
    parts = _scatter_to_chips(chipsum, "reduce_chips")
    mine = _sum_slots(parts, "add_chips")
    other = _swap_with_sibling(mine, "swap_cores")
    first = ic == 0
    gsum = jnp.concatenate([jnp.where(first, mine, other), jnp.where(first, other, mine)], axis=0)
    big_names = ["w_in", "w_branch_a", "w_branch_b", "w_out", "w_ffn_in", "w_ffn_out"]
    big_m = [m_w_in, m_w_branch_a, m_w_branch_b, m_w_out, m_w_ffn_in, m_w_ffn_out]
    big_v = [v_w_in, v_w_branch_a, v_w_branch_b, v_w_out, v_w_ffn_in, v_w_ffn_out]
    for i, (nm, w, m, v) in enumerate(zip(big_names, big, big_m, big_v)):
        g = gsum[roff[i]:roff[i + 1]].reshape(L, pers[i], D)
        if i == 0:
            g = jnp.transpose(lax.dynamic_slice_in_dim(g, my_off, Wc, axis=1), (0, 2, 1))
        elif i == 4:
            g = jnp.transpose(g, (0, 2, 1))
        update(nm, w, g[None], m, v)

    order = ["ada_w", "ada_b", "norm1_g", "w_in", "conv_w", "spatial_w", "spatial_b", "v_norm_g", "a_log", "dt_bias",
             "o_norm_g", "w_branch_a", "w_branch_b", "w_out", "norm2_g", "w_ffn_in", "w_ffn_out", "final_g"]
    return (loss, grad_x, *[outs[n][0] for n in order], *[outs[n][1] for n in order],
            *[outs[n][2] for n in order], *[outs[n][3] for n in order])
```
